```python
import math
import jax, jax.numpy as jnp
from jax import lax
import numpy as np

D_MODEL = 1024
BATCH = 16
SEQ = 4096
DEPTH = 4

N_HEADS_A = 16
N_KV_A = 4
HEAD_DIM_A = 64
WINDOW = 128
N_HEADS_B = 8
KEY_DIM_B = 128
VAL_DIM_B = 128
CONV_K = 4
CHUNK = 64
D_FF = 2816
NUM_BUCKETS = 32
MAX_DISTANCE = 128
DN_ALPHA = (2 * DEPTH) ** 0.25
DN_BETA = (8 * DEPTH) ** -0.25
LN_EPS = 1e-5
NORM_EPS = 1e-6
NEG_INF = -1e30

Q_A = N_HEADS_A * HEAD_DIM_A
KV_W = N_KV_A * HEAD_DIM_A
QK_B = N_HEADS_B * KEY_DIM_B
V_B = N_HEADS_B * VAL_DIM_B
CONV_CH = 2 * QK_B + V_B
SPLIT_SIZES = (Q_A, KV_W, KV_W, CONV_CH, N_HEADS_B, N_HEADS_B, V_B, 2 * D_MODEL)
N_IN = sum(SPLIT_SIZES)
SPLIT_POINTS = tuple(int(s) for s in np.cumsum(SPLIT_SIZES)[:-1])

kernel_name = "hybrid_swa_sink_gated_deltanet_macaron_deepnorm"


def layernorm(x, g, b):
    xf = x.astype(jnp.float32)
    mu = xf.mean(-1, keepdims=True)
    var = jnp.square(xf - mu).mean(-1, keepdims=True)
    y = (xf - mu) * lax.rsqrt(var + LN_EPS) * g.astype(jnp.float32) + b.astype(jnp.float32)
    return y.astype(x.dtype)


def swiglu(x, w13, w2):
    gate, up = jnp.split(x @ w13, 2, axis=-1)
    return (jax.nn.silu(gate) * up) @ w2


def t5_bucket(rel):
    n = jnp.maximum(rel, 0)
    max_exact = NUM_BUCKETS // 2
    nf = jnp.maximum(n, 1).astype(jnp.float32)
    large = max_exact + (jnp.log(nf / max_exact) / math.log(MAX_DISTANCE / max_exact)
                         * (NUM_BUCKETS - max_exact)).astype(jnp.int32)
    large = jnp.minimum(large, NUM_BUCKETS - 1)
    return jnp.where(n < max_exact, n, large)


def band_rel():
    r = jnp.arange(WINDOW)[:, None]
    j = jnp.arange(2 * WINDOW)[None, :]
    return r + WINDOW - j


def sliding_window_attention(q, k, v, pos_bias, sink):
    b, l = q.shape[:2]
    nb = l // WINDOW
    grp = N_HEADS_A // N_KV_A
    qb = (q.astype(jnp.float32) * HEAD_DIM_A ** -0.5).reshape(b, nb, WINDOW, N_KV_A, grp, HEAD_DIM_A)
    kb = k.astype(jnp.float32).reshape(b, nb, WINDOW, N_KV_A, HEAD_DIM_A)
    vb = v.astype(jnp.float32).reshape(b, nb, WINDOW, N_KV_A, HEAD_DIM_A)

    def with_prev(t):
        prev = jnp.concatenate([jnp.zeros_like(t[:, :1]), t[:, :-1]], axis=1)
        return jnp.concatenate([prev, t], axis=2)

    kk, vv = with_prev(kb), with_prev(vb)
    rel = band_rel()
    in_band = (rel >= 0) & (rel < WINDOW)
    key_in_own_block = jnp.arange(2 * WINDOW) >= WINDOW
    bias = pos_bias.reshape(N_KV_A, grp, WINDOW, 2 * WINDOW)
    sk = sink.astype(jnp.float32).reshape(N_KV_A, grp, 1, 1)

    def one_block(args):
        qi, ki, vi, i = args
        s = jnp.einsum('bqkgd,bskd->bkgqs', qi, ki) + bias
        valid = in_band & ((i > 0) | key_in_own_block)
        s = jnp.where(valid, s, NEG_INF)
        m = jnp.maximum(s.max(-1, keepdims=True), sk)
        p = jnp.exp(s - m)
        p = p / (p.sum(-1, keepdims=True) + jnp.exp(sk - m))
        return jnp.einsum('bkgqs,bskd->bqkgd', p, vi)

    o = lax.map(one_block, (jnp.moveaxis(qb, 1, 0), jnp.moveaxis(kk, 1, 0),
                            jnp.moveaxis(vv, 1, 0), jnp.arange(nb)))
    return jnp.moveaxis(o, 0, 1).reshape(b, l, Q_A)


def causal_depthwise_conv(u, w):
    return lax.conv_general_dilated(u, w[:, None, :].astype(u.dtype), window_strides=(1,),
                                    padding=[(CONV_K - 1, 0)],
                                    dimension_numbers=('NWC', 'WIO', 'NWC'),
                                    feature_group_count=u.shape[-1])


def l2norm(t):
    t = t.astype(jnp.float32)
    return t * lax.rsqrt(jnp.sum(t * t, -1, keepdims=True) + NORM_EPS)


def gated_delta_rule(q, k, v, g, beta):
    b, l, h, dk = q.shape
    dv = v.shape[-1]
    n = l // CHUNK

    def chunked(t):
        return t.reshape(b, n, CHUNK, h, t.shape[-1]).transpose(0, 3, 1, 2, 4)

    q = chunked(q * dk ** -0.5)
    k = chunked(k)
    v = chunked(v)
    g = jnp.cumsum(g.reshape(b, n, CHUNK, h).transpose(0, 3, 1, 2), axis=-1)
    beta = beta.reshape(b, n, CHUNK, h).transpose(0, 3, 1, 2)[..., None]
    kb, vb = k * beta, v * beta

    causal = jnp.tril(jnp.ones((CHUNK, CHUNK), bool))
    strict = jnp.tril(jnp.ones((CHUNK, CHUNK), bool), -1)
    decay = jnp.exp(jnp.where(causal, g[..., :, None] - g[..., None, :], NEG_INF))
    lower = jnp.where(strict, jnp.einsum('bhncd,bhnmd->bhncm', kb, k) * decay, 0.0)
    eye = jnp.eye(CHUNK, dtype=jnp.float32)
    t_inv = lax.linalg.triangular_solve(lower + eye, jnp.broadcast_to(eye, lower.shape),
                                        left_side=True, lower=True, unit_diagonal=True)
    u = t_inv @ vb
    w = t_inv @ (kb * jnp.exp(g)[..., None])
    a_intra = jnp.where(causal, jnp.einsum('bhncd,bhnmd->bhncm', q, k) * decay, 0.0)
    g_last = g[..., -1]
    k_tail = k * jnp.exp(g_last[..., None] - g)[..., None]
    q_dec = q * jnp.exp(g)[..., None]

    def step(s, xs):
        qd, kt, uc, wc, ac, gl = xs
        v_new = uc - jnp.einsum('bhcd,bhde->bhce', wc, s)
        o = jnp.einsum('bhcd,bhde->bhce', qd, s) + jnp.einsum('bhcm,bhme->bhce', ac, v_new)
        s = s * jnp.exp(gl)[..., None, None] + jnp.einsum('bhcd,bhce->bhde', kt, v_new)
        return s, o

    xs = tuple(jnp.moveaxis(t, 2, 0) for t in (q_dec, k_tail, u, w, a_intra, g_last))
    s0 = jnp.zeros((b, h, dk, dv), jnp.float32)
    _, o = lax.scan(step, s0, xs)
    return o.transpose(1, 0, 3, 2, 4).reshape(b, l, h, dv)


def hybrid_mixer(x, w_in, conv_w, a_log, dt_bias, dn_norm_g, sink,
                 w_branch_a, w_branch_b, w_out, pos_bias):
    b, l, _ = x.shape
    hcat = x @ w_in
    qa, ka, va, qkv_b, beta_raw, dt_raw, z, gate_raw = jnp.split(hcat, SPLIT_POINTS, axis=-1)

    ya = sliding_window_attention(qa.reshape(b, l, N_HEADS_A, HEAD_DIM_A),
                                  ka.reshape(b, l, N_KV_A, HEAD_DIM_A),
                                  va.reshape(b, l, N_KV_A, HEAD_DIM_A), pos_bias, sink)
    ya = ya.astype(x.dtype) @ w_branch_a

    qkv_b = jax.nn.silu(causal_depthwise_conv(qkv_b, conv_w))
    qb, kb, vb = jnp.split(qkv_b, (QK_B, 2 * QK_B), axis=-1)
    qb = l2norm(qb.reshape(b, l, N_HEADS_B, KEY_DIM_B))
    kb = l2norm(kb.reshape(b, l, N_HEADS_B, KEY_DIM_B))
    vb = vb.reshape(b, l, N_HEADS_B, VAL_DIM_B).astype(jnp.float32)
    beta = jax.nn.sigmoid(beta_raw.astype(jnp.float32))
    g = -jnp.exp(a_log.astype(jnp.float32)) * jax.nn.softplus(
        dt_raw.astype(jnp.float32) + dt_bias.astype(jnp.float32))
    o = gated_delta_rule(qb, kb, vb, g, beta)
    o = (o * lax.rsqrt(jnp.mean(o * o, -1, keepdims=True) + NORM_EPS)
         * dn_norm_g.astype(jnp.float32)
         * jax.nn.silu(z.reshape(b, l, N_HEADS_B, VAL_DIM_B).astype(jnp.float32)))
    yb = o.reshape(b, l, V_B).astype(x.dtype) @ w_branch_b

    gate_a, gate_b = jnp.split(jax.nn.sigmoid(gate_raw), 2, axis=-1)
    return (gate_a * ya + gate_b * yb) @ w_out


def _fwd_setup_inputs(seed: int = 0) -> dict:
    key = jax.random.key(seed)
    ks = jax.random.split(key, 16)
    f32 = jnp.float32

    def nrm(k, shape, scale):
        return jax.random.normal(k, shape, f32) * scale

    dt = jnp.exp(jax.random.uniform(ks[9], (DEPTH, N_HEADS_B), f32, math.log(1e-3), math.log(1e-1)))
    return {
        "x": nrm(ks[0], (BATCH, SEQ, D_MODEL), 1.0),
        "rel_bias": nrm(ks[1], (NUM_BUCKETS, N_HEADS_A), 0.5),
        "ln_g": 1.0 + nrm(ks[2], (DEPTH, 3, D_MODEL), 0.05),
        "ln_b": nrm(ks[3], (DEPTH, 3, D_MODEL), 0.02),
        "ffn_w13": nrm(ks[4], (DEPTH, 2, D_MODEL, 2 * D_FF), D_MODEL ** -0.5),
        "ffn_w2": nrm(ks[5], (DEPTH, 2, D_FF, D_MODEL), DN_BETA * D_FF ** -0.5),
        "w_in": nrm(ks[6], (DEPTH, D_MODEL, N_IN), D_MODEL ** -0.5),
        "conv_w": nrm(ks[7], (DEPTH, CONV_K, CONV_CH), CONV_K ** -0.5),
        "a_log": jnp.log(jax.random.uniform(ks[8], (DEPTH, N_HEADS_B), f32, 1.0, 16.0)),
        "dt_bias": dt + jnp.log(-jnp.expm1(-dt)),
        "dn_norm_g": 1.0 + nrm(ks[10], (DEPTH, VAL_DIM_B), 0.05),
        "sinks": nrm(ks[11], (DEPTH, N_HEADS_A), 0.5),
        "w_branch_a": nrm(ks[12], (DEPTH, Q_A, D_MODEL), Q_A ** -0.5),
        "w_branch_b": nrm(ks[13], (DEPTH, V_B, D_MODEL), V_B ** -0.5),
        "w_out": nrm(ks[14], (DEPTH, D_MODEL, D_MODEL), DN_BETA * D_MODEL ** -0.5),
    }


def _fwd_reference(x, rel_bias, ln_g, ln_b, ffn_w13, ffn_w2, w_in, conv_w, a_log, dt_bias,
              dn_norm_g, sinks, w_branch_a, w_branch_b, w_out):
    pos_bias = jnp.transpose(rel_bias[t5_bucket(band_rel())], (2, 0, 1)).astype(jnp.float32)
    for i in range(DEPTH):
        x = layernorm(DN_ALPHA * x + 0.5 * swiglu(x, ffn_w13[i, 0], ffn_w2[i, 0]), ln_g[i, 0], ln_b[i, 0])
        y = hybrid_mixer(x, w_in[i], conv_w[i], a_log[i], dt_bias[i], dn_norm_g[i], sinks[i],
                         w_branch_a[i], w_branch_b[i], w_out[i], pos_bias)
        x = layernorm(DN_ALPHA * x + y, ln_g[i, 1], ln_b[i, 1])
        x = layernorm(DN_ALPHA * x + 0.5 * swiglu(x, ffn_w13[i, 1], ffn_w2[i, 1]), ln_g[i, 2], ln_b[i, 2])
    return x


import jax as _jax
import jax.numpy as _jnp

TWIN_FORMAT = 'train_step'
FWD_PARAMS = ['x', 'rel_bias', 'ln_g', 'ln_b', 'ffn_w13', 'ffn_w2', 'w_in', 'conv_w', 'a_log', 'dt_bias', 'dn_norm_g', 'sinks', 'w_branch_a', 'w_branch_b', 'w_out']
TWIN_WEIGHTS = ['rel_bias', 'ln_g', 'ln_b', 'ffn_w13', 'ffn_w2', 'w_in', 'conv_w', 'a_log', 'dt_bias', 'dn_norm_g', 'sinks', 'w_branch_a', 'w_branch_b', 'w_out']
TWIN_DIFF_INPUT = 'x'
TWIN_INPUTS = ['x', 'rel_bias', 'ln_g', 'ln_b', 'ffn_w13', 'ffn_w2', 'w_in', 'conv_w', 'a_log', 'dt_bias', 'dn_norm_g', 'sinks', 'w_branch_a', 'w_branch_b', 'w_out', 'loss_target', 'm_rel_bias', 'm_ln_g', 'm_ln_b', 'm_ffn_w13', 'm_ffn_w2', 'm_w_in', 'm_conv_w', 'm_a_log', 'm_dt_bias', 'm_dn_norm_g', 'm_sinks', 'm_w_branch_a', 'm_w_branch_b', 'm_w_out', 'v_rel_bias', 'v_ln_g', 'v_ln_b', 'v_ffn_w13', 'v_ffn_w2', 'v_w_in', 'v_conv_w', 'v_a_log', 'v_dt_bias', 'v_dn_norm_g', 'v_sinks', 'v_w_branch_a', 'v_w_branch_b', 'v_w_out']
TWIN_OUTPUTS = ['loss', 'grad_x', 'grad_rel_bias', 'grad_ln_g', 'grad_ln_b', 'grad_ffn_w13', 'grad_ffn_w2', 'grad_w_in', 'grad_conv_w', 'grad_a_log', 'grad_dt_bias', 'grad_dn_norm_g', 'grad_sinks', 'grad_w_branch_a', 'grad_w_branch_b', 'grad_w_out', 'delta_rel_bias', 'delta_ln_g', 'delta_ln_b', 'delta_ffn_w13', 'delta_ffn_w2', 'delta_w_in', 'delta_conv_w', 'delta_a_log', 'delta_dt_bias', 'delta_dn_norm_g', 'delta_sinks', 'delta_w_branch_a', 'delta_w_branch_b', 'delta_w_out', 'new_m_rel_bias', 'new_m_ln_g', 'new_m_ln_b', 'new_m_ffn_w13', 'new_m_ffn_w2', 'new_m_w_in', 'new_m_conv_w', 'new_m_a_log', 'new_m_dt_bias', 'new_m_dn_norm_g', 'new_m_sinks', 'new_m_w_branch_a', 'new_m_w_branch_b', 'new_m_w_out', 'new_v_rel_bias', 'new_v_ln_g', 'new_v_ln_b', 'new_v_ffn_w13', 'new_v_ffn_w2', 'new_v_w_in', 'new_v_conv_w', 'new_v_a_log', 'new_v_dt_bias', 'new_v_dn_norm_g', 'new_v_sinks', 'new_v_w_branch_a', 'new_v_w_branch_b', 'new_v_w_out']
TWIN_LEAF_KINDS = {'loss': 'loss', 'grad_x': 'grad_x', 'grad_rel_bias': 'grad_w', 'grad_ln_g': 'grad_w', 'grad_ln_b': 'grad_w', 'grad_ffn_w13': 'grad_w', 'grad_ffn_w2': 'grad_w', 'grad_w_in': 'grad_w', 'grad_conv_w': 'grad_w', 'grad_a_log': 'grad_w', 'grad_dt_bias': 'grad_w', 'grad_dn_norm_g': 'grad_w', 'grad_sinks': 'grad_w', 'grad_w_branch_a': 'grad_w', 'grad_w_branch_b': 'grad_w', 'grad_w_out': 'grad_w', 'delta_rel_bias': 'delta_w', 'delta_ln_g': 'delta_w', 'delta_ln_b': 'delta_w', 'delta_ffn_w13': 'delta_w', 'delta_ffn_w2': 'delta_w', 'delta_w_in': 'delta_w', 'delta_conv_w': 'delta_w', 'delta_a_log': 'delta_w', 'delta_dt_bias': 'delta_w', 'delta_dn_norm_g': 'delta_w', 'delta_sinks': 'delta_w', 'delta_w_branch_a': 'delta_w', 'delta_w_branch_b': 'delta_w', 'delta_w_out': 'delta_w', 'new_m_rel_bias': 'new_m', 'new_m_ln_g': 'new_m', 'new_m_ln_b': 'new_m', 'new_m_ffn_w13': 'new_m', 'new_m_ffn_w2': 'new_m', 'new_m_w_in': 'new_m', 'new_m_conv_w': 'new_m', 'new_m_a_log': 'new_m', 'new_m_dt_bias': 'new_m', 'new_m_dn_norm_g': 'new_m', 'new_m_sinks': 'new_m', 'new_m_w_branch_a': 'new_m', 'new_m_w_branch_b': 'new_m', 'new_m_w_out': 'new_m', 'new_v_rel_bias': 'new_v', 'new_v_ln_g': 'new_v', 'new_v_ln_b': 'new_v', 'new_v_ffn_w13': 'new_v', 'new_v_ffn_w2': 'new_v', 'new_v_w_in': 'new_v', 'new_v_conv_w': 'new_v', 'new_v_a_log': 'new_v', 'new_v_dt_bias': 'new_v', 'new_v_dn_norm_g': 'new_v', 'new_v_sinks': 'new_v', 'new_v_w_branch_a': 'new_v', 'new_v_w_branch_b': 'new_v', 'new_v_w_out': 'new_v'}


def _forward(args):
    return _fwd_reference(*[args[k] for k in FWD_PARAMS])


def _output_shape():
    out = _jax.eval_shape(lambda: _forward(_fwd_setup_inputs(0)))
    return out.shape, out.dtype

N_MICROBATCH = 1
ADAM_LR = 0.001
ADAM_B1 = 0.9
ADAM_B2 = 0.999
ADAM_EPS = 1e-08
ADAM_WD = 0.01
ADAM_STEP = 10
PER_EXAMPLE_BATCH_AXIS = {'x': 0, 'loss_target': 0}
SHARED_INPUTS = []
_WEIGHT_DTYPES = {'rel_bias': _jnp.float32, 'ln_g': _jnp.float32, 'ln_b': _jnp.float32, 'ffn_w13': _jnp.float32, 'ffn_w2': _jnp.float32, 'w_in': _jnp.float32, 'conv_w': _jnp.float32, 'a_log': _jnp.float32, 'dt_bias': _jnp.float32, 'dn_norm_g': _jnp.float32, 'sinks': _jnp.float32, 'w_branch_a': _jnp.float32, 'w_branch_b': _jnp.float32, 'w_out': _jnp.float32}
MOMENT_SCALE = {'rel_bias': 1.859391e-02, 'ln_g': 2.025436e+01, 'ln_b': 1.206432e+00, 'ffn_w13': 1.176130e-02, 'ffn_w2': 4.584268e-02, 'w_in': 1.368557e-02, 'conv_w': 1.586983e-02, 'a_log': 9.167247e-02, 'dt_bias': 8.834675e-02, 'dn_norm_g': 5.996847e-02, 'sinks': 5.470468e-03, 'w_branch_a': 8.093327e-03, 'w_branch_b': 2.213217e-02, 'w_out': 5.267683e-02}


def _to_microbatches(a, axis):
    t = _jnp.moveaxis(a, axis, 0)
    t = t.reshape((N_MICROBATCH, t.shape[0] // N_MICROBATCH) + t.shape[1:])
    return _jnp.moveaxis(t, 1, axis + 1)


def setup_inputs(seed: int = 0) -> dict:
    inp = _fwd_setup_inputs(seed)
    key = _jax.random.fold_in(_jax.random.key(seed), 7919)
    shape, _ = _output_shape()
    out = dict(inp)
    out["loss_target"] = _jax.random.normal(_jax.random.fold_in(key, 0), shape, _jnp.float32)
    for i, name in enumerate(TWIN_WEIGHTS):
        w = inp[name].astype(_jnp.float32)
        if MOMENT_SCALE is None:
            s = _jnp.sqrt(_jnp.mean(_jnp.square(w)) + 1e-30)
        else:
            s = MOMENT_SCALE[name]
        km, kv = _jax.random.split(_jax.random.fold_in(key, i + 1))
        out[name] = w
        out["m_" + name] = s * _jax.random.normal(km, w.shape, _jnp.float32)
        out["v_" + name] = (s * s) * _jax.random.uniform(kv, w.shape, _jnp.float32, 0.5, 1.5)
    if N_MICROBATCH > 1:
        for name, axis in PER_EXAMPLE_BATCH_AXIS.items():
            out[name] = _to_microbatches(out[name], axis)
    return {'x': out['x'], 'rel_bias': out['rel_bias'], 'ln_g': out['ln_g'], 'ln_b': out['ln_b'], 'ffn_w13': out['ffn_w13'], 'ffn_w2': out['ffn_w2'], 'w_in': out['w_in'], 'conv_w': out['conv_w'], 'a_log': out['a_log'], 'dt_bias': out['dt_bias'], 'dn_norm_g': out['dn_norm_g'], 'sinks': out['sinks'], 'w_branch_a': out['w_branch_a'], 'w_branch_b': out['w_branch_b'], 'w_out': out['w_out'], 'loss_target': out['loss_target'], 'm_rel_bias': out['m_rel_bias'], 'm_ln_g': out['m_ln_g'], 'm_ln_b': out['m_ln_b'], 'm_ffn_w13': out['m_ffn_w13'], 'm_ffn_w2': out['m_ffn_w2'], 'm_w_in': out['m_w_in'], 'm_conv_w': out['m_conv_w'], 'm_a_log': out['m_a_log'], 'm_dt_bias': out['m_dt_bias'], 'm_dn_norm_g': out['m_dn_norm_g'], 'm_sinks': out['m_sinks'], 'm_w_branch_a': out['m_w_branch_a'], 'm_w_branch_b': out['m_w_branch_b'], 'm_w_out': out['m_w_out'], 'v_rel_bias': out['v_rel_bias'], 'v_ln_g': out['v_ln_g'], 'v_ln_b': out['v_ln_b'], 'v_ffn_w13': out['v_ffn_w13'], 'v_ffn_w2': out['v_ffn_w2'], 'v_w_in': out['v_w_in'], 'v_conv_w': out['v_conv_w'], 'v_a_log': out['v_a_log'], 'v_dt_bias': out['v_dt_bias'], 'v_dn_norm_g': out['v_dn_norm_g'], 'v_sinks': out['v_sinks'], 'v_w_branch_a': out['v_w_branch_a'], 'v_w_branch_b': out['v_w_branch_b'], 'v_w_out': out['v_w_out']}


def _loss(weights, diff, rest, loss_target):
    with _jax.named_scope("forward"):
        args = {**rest, TWIN_DIFF_INPUT: diff, **{k: w.astype(_WEIGHT_DTYPES[k]) for k, w in weights.items()}}
        y = _forward(args)
    with _jax.named_scope("loss_head"):
        err = _jnp.square(y.astype(_jnp.float32) - loss_target)
        return 0.5 * _jnp.sum(_jnp.mean(err, axis=-1)) if err.ndim else 0.5 * err


def _adamw(w, g, m, v):
    m = ADAM_B1 * m + (1.0 - ADAM_B1) * g
    v = ADAM_B2 * v + (1.0 - ADAM_B2) * _jnp.square(g)
    m_hat = m / (1.0 - ADAM_B1 ** ADAM_STEP)
    v_hat = v / (1.0 - ADAM_B2 ** ADAM_STEP)
    delta = -ADAM_LR * (m_hat / (_jnp.sqrt(v_hat) + ADAM_EPS) + ADAM_WD * w)
    return delta, m, v


def reference(x, rel_bias, ln_g, ln_b, ffn_w13, ffn_w2, w_in, conv_w, a_log, dt_bias, dn_norm_g, sinks, w_branch_a, w_branch_b, w_out, loss_target, m_rel_bias, m_ln_g, m_ln_b, m_ffn_w13, m_ffn_w2, m_w_in, m_conv_w, m_a_log, m_dt_bias, m_dn_norm_g, m_sinks, m_w_branch_a, m_w_branch_b, m_w_out, v_rel_bias, v_ln_g, v_ln_b, v_ffn_w13, v_ffn_w2, v_w_in, v_conv_w, v_a_log, v_dt_bias, v_dn_norm_g, v_sinks, v_w_branch_a, v_w_branch_b, v_w_out):
    given = dict(x=x, rel_bias=rel_bias, ln_g=ln_g, ln_b=ln_b, ffn_w13=ffn_w13, ffn_w2=ffn_w2, w_in=w_in, conv_w=conv_w, a_log=a_log, dt_bias=dt_bias, dn_norm_g=dn_norm_g, sinks=sinks, w_branch_a=w_branch_a, w_branch_b=w_branch_b, w_out=w_out, loss_target=loss_target, m_rel_bias=m_rel_bias, m_ln_g=m_ln_g, m_ln_b=m_ln_b, m_ffn_w13=m_ffn_w13, m_ffn_w2=m_ffn_w2, m_w_in=m_w_in, m_conv_w=m_conv_w, m_a_log=m_a_log, m_dt_bias=m_dt_bias, m_dn_norm_g=m_dn_norm_g, m_sinks=m_sinks, m_w_branch_a=m_w_branch_a, m_w_branch_b=m_w_branch_b, m_w_out=m_w_out, v_rel_bias=v_rel_bias, v_ln_g=v_ln_g, v_ln_b=v_ln_b, v_ffn_w13=v_ffn_w13, v_ffn_w2=v_ffn_w2, v_w_in=v_w_in, v_conv_w=v_conv_w, v_a_log=v_a_log, v_dt_bias=v_dt_bias, v_dn_norm_g=v_dn_norm_g, v_sinks=v_sinks, v_w_branch_a=v_w_branch_a, v_w_branch_b=v_w_branch_b, v_w_out=v_w_out)
    weights = {n: given[n] for n in TWIN_WEIGHTS}
    shared = {n: given[n] for n in SHARED_INPUTS}
    per_example = {n: given[n] for n in ['x']}
    grad_fn = _jax.value_and_grad(_loss, argnums=(0, 1))

    def one_microbatch(ex, loss_target):
        ex = dict(ex)
        diff = ex.pop(TWIN_DIFF_INPUT)
        return grad_fn(weights, diff, {**shared, **ex}, loss_target)

    if N_MICROBATCH == 1:
        loss, (grad_w, grad_x) = one_microbatch(per_example, given["loss_target"])
    else:
        def body(carry, xs):
            loss_sum, grad_sum = carry
            l_k, (gw_k, gx_k) = one_microbatch(xs[0], xs[1])
            with _jax.named_scope("update"):
                return (loss_sum + l_k, _jax.tree.map(_jnp.add, grad_sum, gw_k)), gx_k

        init = (_jnp.zeros((), _jnp.float32), _jax.tree.map(_jnp.zeros_like, weights))
        (loss, grad_w), grad_x = _jax.lax.scan(body, init, (per_example, given["loss_target"]))
    with _jax.named_scope("update"):
        delta_w, new_m, new_v = {}, {}, {}
        for n in TWIN_WEIGHTS:
            delta_w[n], new_m[n], new_v[n] = _adamw(weights[n], grad_w[n], given["m_" + n], given["v_" + n])
    return (loss, grad_x, *[grad_w[n] for n in TWIN_WEIGHTS], *[delta_w[n] for n in TWIN_WEIGHTS],
            *[new_m[n] for n in TWIN_WEIGHTS], *[new_v[n] for n in TWIN_WEIGHTS])
```

```python
import functools
import math

import numpy as np
import jax
import jax.numpy as jnp
from jax import lax
from jax.experimental import pallas as pl
from jax.experimental.pallas import tpu as pltpu

F32 = jnp.float32
BF16 = jnp.bfloat16
MXU_DTYPE = BF16
HIGHEST = lax.Precision.HIGHEST

D_MODEL = 1024
N_HEADS_A = 16
N_KV_A = 4
HEAD_DIM_A = 64
GROUP_A = N_HEADS_A // N_KV_A
WINDOW = 128
N_HEADS_B = 8
KEY_DIM_B = 128
VAL_DIM_B = 128
CONV_K = 4
CHUNK = 64
D_FF = 2816
NUM_BUCKETS = 32
MAX_DISTANCE = 128
DEPTH = 4
DN_ALPHA = (2 * DEPTH) ** 0.25
LN_EPS = 1e-5
NORM_EPS = 1e-6
NEG_INF = -1e30

Q_A = N_HEADS_A * HEAD_DIM_A
KV_W = N_KV_A * HEAD_DIM_A
QK_B = N_HEADS_B * KEY_DIM_B
V_B = N_HEADS_B * VAL_DIM_B
CONV_CH = 2 * QK_B + V_B
N_IN = Q_A + 2 * KV_W + CONV_CH + 2 * N_HEADS_B + V_B + 2 * D_MODEL

ADAM_LR = 0.001
ADAM_B1 = 0.9
ADAM_B2 = 0.999
ADAM_EPS = 1e-08
ADAM_WD = 0.01
ADAM_STEP = 10

HC_W = 8192
HC_Q = 0
HC_Z = 1024
HC_K = 2048
HC_V = 2304
HC_BD = 2560
HC_CONV = 3072
HC_GATE = 6144

GROUP_T = 256
ROW_T = 256
VMEM_LIMIT_BYTES = 48 * 1024 * 1024


def _params(**kw):
    return pltpu.CompilerParams(vmem_limit_bytes=VMEM_LIMIT_BYTES, **kw)


def _tile(n, cands):
    for c in cands:
        if n % c == 0:
            return c
    return n


def _dot(a, b, dims=(((1,), (0,)), ((), ())), exact=False):
    if exact:
        return lax.dot_general(a.astype(F32), b.astype(F32), dims, precision=HIGHEST, preferred_element_type=F32)
    return lax.dot_general(a.astype(MXU_DTYPE), b.astype(MXU_DTYPE), dims, preferred_element_type=F32)


_NN = (((1,), (0,)), ((), ()))
_NT = (((1,), (1,)), ((), ()))
_TN = (((0,), (0,)), ((), ()))


def _sigmoid(x):
    return 1.0 / (1.0 + jnp.exp(-x))


def _silu(x):
    return x * _sigmoid(x)


def _dsilu(x):
    s = _sigmoid(x)
    return s * (1.0 + x * (1.0 - s))


def _mm(a, b, *, ta=False, tb=False, out_dtype=F32, add=None, exact=False, name):
    (kdim, m) = a.shape if ta else a.shape[::-1]
    (n, kb) = b.shape if tb else b.shape[::-1]
    assert kdim == kb, (a.shape, b.shape, ta, tb)
    tm = _tile(m, (512, 256, 128))
    tn = _tile(n, (1024, 1408, 512, 256, 128))
    tk = _tile(kdim, (1024, 1408, 512, 256, 128))
    nk = kdim // tk
    dims = (((0 if ta else 1,), (1 if tb else 0,)), ((), ()))
    has_add = add is not None

    def body(*refs):
        if has_add:
            a_ref, b_ref, add_ref, o_ref = refs[:4]
        else:
            a_ref, b_ref, o_ref = refs[:3]
            add_ref = None
        part = _dot(a_ref[...], b_ref[...], dims, exact)

        def finish(acc):
            if has_add:
                acc = acc + add_ref[...].astype(F32)
            o_ref[...] = acc.astype(out_dtype)

        if nk == 1:
            finish(part)
        else:
            acc_ref = refs[-1]
            k = pl.program_id(2)

            @pl.when(k == 0)
            def _():
                acc_ref[...] = part

            @pl.when(k > 0)
            def _():
                acc_ref[...] += part

            @pl.when(k == nk - 1)
            def _():
                finish(acc_ref[...])

    a_spec = pl.BlockSpec((tk, tm), lambda i, j, k: (k, i)) if ta else pl.BlockSpec((tm, tk), lambda i, j, k: (i, k))
    b_spec = pl.BlockSpec((tn, tk), lambda i, j, k: (j, k)) if tb else pl.BlockSpec((tk, tn), lambda i, j, k: (k, j))
    o_spec = pl.BlockSpec((tm, tn), lambda i, j, k: (i, j))
    in_specs = [a_spec, b_spec] + ([o_spec] if has_add else [])
    args = (a, b) + ((add,) if has_add else ())
    return pl.pallas_call(
        body, name=name, grid=(m // tm, n // tn, nk),
        in_specs=in_specs, out_specs=o_spec,
        out_shape=jax.ShapeDtypeStruct((m, n), out_dtype),
        scratch_shapes=[pltpu.VMEM((tm, tn), F32)] if nk > 1 else [],
        compiler_params=_params(dimension_semantics=("parallel", "parallel", "arbitrary")),
    )(*args)


def _layernorm_rows(r, g, b):
    mu = jnp.mean(r, axis=-1, keepdims=True)
    xc = r - mu
    var = jnp.mean(xc * xc, axis=-1, keepdims=True)
    return xc * lax.rsqrt(var + LN_EPS) * g + b


def _mm_res_ln(a, w, resid, g, b, *, alpha, c, name):
    m, kdim = a.shape
    n = w.shape[1]
    tm = _tile(m, (512, 256, 128))

    def body(a_ref, w_ref, x_ref, g_ref, b_ref, r_ref, y_ref):
        f = _dot(a_ref[...], w_ref[...])
        r = alpha * x_ref[...] + c * f
        r_ref[...] = r
        y_ref[...] = _layernorm_rows(r, g_ref[...], b_ref[...])

    row = pl.BlockSpec((tm, n), lambda i: (i, 0))
    vec = pl.BlockSpec((1, n), lambda i: (0, 0))
    return pl.pallas_call(
        body, name=name, grid=(m // tm,),
        in_specs=[pl.BlockSpec((tm, kdim), lambda i: (i, 0)), pl.BlockSpec((kdim, n), lambda i: (0, 0)), row, vec, vec],
        out_specs=[row, row],
        out_shape=[jax.ShapeDtypeStruct((m, n), F32)] * 2,
        compiler_params=_params(dimension_semantics=("parallel",)),
    )(a, w, resid, g.reshape(1, n), b.reshape(1, n))


def _ln_bwd(dy, r, g, *, alpha, c, name):
    m, n = dy.shape
    tm = _tile(m, (512, 256, 128))

    def body(dy_ref, r_ref, g_ref, dres_ref, dbr_ref, dg_ref, db_ref):
        i = pl.program_id(0)
        dy_ = dy_ref[...]
        r_ = r_ref[...]
        mu = jnp.mean(r_, axis=-1, keepdims=True)
        xc = r_ - mu
        var = jnp.mean(xc * xc, axis=-1, keepdims=True)
        rstd = lax.rsqrt(var + LN_EPS)
        xh = xc * rstd
        dxh = dy_ * g_ref[...]
        dr = rstd * (dxh - jnp.mean(dxh, axis=-1, keepdims=True) - xh * jnp.mean(dxh * xh, axis=-1, keepdims=True))
        dres_ref[...] = alpha * dr
        dbr_ref[...] = (c * dr).astype(BF16)
        dg_p = jnp.sum(dy_ * xh, axis=0, keepdims=True)
        db_p = jnp.sum(dy_, axis=0, keepdims=True)

        @pl.when(i == 0)
        def _():
            dg_ref[...] = dg_p
            db_ref[...] = db_p

        @pl.when(i > 0)
        def _():
            dg_ref[...] += dg_p
            db_ref[...] += db_p

    row = pl.BlockSpec((tm, n), lambda i: (i, 0))
    vec = pl.BlockSpec((1, n), lambda i: (0, 0))
    return pl.pallas_call(
        body, name=name, grid=(m // tm,),
        in_specs=[row, row, vec], out_specs=[row, row, vec, vec],
        out_shape=[jax.ShapeDtypeStruct((m, n), F32), jax.ShapeDtypeStruct((m, n), BF16),
                   jax.ShapeDtypeStruct((1, n), F32), jax.ShapeDtypeStruct((1, n), F32)],
        compiler_params=_params(dimension_semantics=("arbitrary",)),
    )(dy, r, g.reshape(1, n))


def _act(h, *, name):
    m = h.shape[0]
    tm = _tile(m, (512, 256, 128))
    tc = D_FF // 2

    def body(g_ref, u_ref, o_ref):
        o_ref[...] = (_silu(g_ref[...]) * u_ref[...]).astype(BF16)

    return pl.pallas_call(
        body, name=name, grid=(m // tm, 2),
        in_specs=[pl.BlockSpec((tm, tc), lambda i, j: (i, j)), pl.BlockSpec((tm, tc), lambda i, j: (i, j + 2))],
        out_specs=pl.BlockSpec((tm, tc), lambda i, j: (i, j)),
        out_shape=jax.ShapeDtypeStruct((m, D_FF), BF16),
        compiler_params=_params(dimension_semantics=("parallel", "parallel")),
    )(h, h)


def _act_bwd(da, h, *, name):
    m = h.shape[0]
    tm = _tile(m, (512, 256, 128))
    tc = D_FF // 2

    def body(da_ref, g_ref, u_ref, o_ref):
        j = pl.program_id(1)

        @pl.when(j < 2)
        def _():
            o_ref[...] = (da_ref[...] * u_ref[...] * _dsilu(g_ref[...])).astype(BF16)

        @pl.when(j >= 2)
        def _():
            o_ref[...] = (da_ref[...] * _silu(g_ref[...])).astype(BF16)

    return pl.pallas_call(
        body, name=name, grid=(m // tm, 4),
        in_specs=[pl.BlockSpec((tm, tc), lambda i, j: (i, j % 2)),
                  pl.BlockSpec((tm, tc), lambda i, j: (i, j % 2)),
                  pl.BlockSpec((tm, tc), lambda i, j: (i, j % 2 + 2))],
        out_specs=pl.BlockSpec((tm, tc), lambda i, j: (i, j)),
        out_shape=jax.ShapeDtypeStruct((m, 2 * D_FF), BF16),
        compiler_params=_params(dimension_semantics=("parallel", "parallel")),
    )(da, h, h)


def _t5_bucket_table():
    r = np.arange(WINDOW)[:, None]
    j = np.arange(2 * WINDOW)[None, :]
    rel = r + WINDOW - j
    n = np.maximum(rel, 0)
    max_exact = NUM_BUCKETS // 2
    nf = np.maximum(n, 1).astype(np.float32)
    large = max_exact + (np.log(nf / np.float32(max_exact)) / np.float32(math.log(MAX_DISTANCE / max_exact))
                         * np.float32(NUM_BUCKETS - max_exact)).astype(np.int32)
    large = np.minimum(large, NUM_BUCKETS - 1)
    bucket = np.where(n < max_exact, n, large)
    in_band = (rel >= 0) & (rel < WINDOW)
    return bucket.astype(np.int32), in_band


def _bucket_onehot():
    bucket, _ = _t5_bucket_table()
    oh = np.zeros((WINDOW * 2 * WINDOW, 128), np.float32)
    oh[np.arange(oh.shape[0]), bucket.reshape(-1)] = 1.0
    return oh


def _attn_scores(q_ref, kp_ref, ko_ref, vp_ref, vo_ref, bias_ref, sink_ref, g, first_block):
    hd = HEAD_DIM_A
    q = q_ref[...]
    qs = jnp.concatenate([q[:, (GROUP_A * g + h) * hd:(GROUP_A * g + h + 1) * hd] for h in range(GROUP_A)], axis=0)
    qs = qs * (hd ** -0.5)
    k2 = jnp.concatenate([kp_ref[:, g * hd:(g + 1) * hd], ko_ref[:, g * hd:(g + 1) * hd]], axis=0)
    v2 = jnp.concatenate([vp_ref[:, g * hd:(g + 1) * hd], vo_ref[:, g * hd:(g + 1) * hd]], axis=0)
    s = _dot(qs, k2, _NT)
    s = s + bias_ref[GROUP_A * g:GROUP_A * (g + 1)].reshape(GROUP_A * WINDOW, 2 * WINDOW)
    rr = lax.broadcasted_iota(jnp.int32, (GROUP_A * WINDOW, 2 * WINDOW), 0) % WINDOW
    jj = lax.broadcasted_iota(jnp.int32, (GROUP_A * WINDOW, 2 * WINDOW), 1)
    rel = rr + WINDOW - jj
    valid = (rel >= 0) & (rel < WINDOW) & (jnp.logical_not(first_block) | (jj >= WINDOW))
    s = jnp.where(valid, s, NEG_INF)
    sk = jnp.concatenate(
        [jnp.broadcast_to(sink_ref[0:1, GROUP_A * g + h:GROUP_A * g + h + 1], (WINDOW, 1)) for h in range(GROUP_A)], axis=0)
    mx = jnp.maximum(jnp.max(s, axis=-1, keepdims=True), sk)
    p = jnp.exp(s - mx)
    ps = jnp.exp(sk - mx)
    den = jnp.sum(p, axis=-1, keepdims=True) + ps
    return qs, k2, v2, p / den, ps / den


def _attn_specs(nb):
    def prev(b, i):
        return (b * nb + jnp.maximum(i - 1, 0))

    q_spec = pl.BlockSpec((WINDOW, Q_A), lambda b, i: (b * nb + i, HC_Q // Q_A))
    kp_spec = pl.BlockSpec((WINDOW, KV_W), lambda b, i: (prev(b, i), HC_K // KV_W))
    ko_spec = pl.BlockSpec((WINDOW, KV_W), lambda b, i: (b * nb + i, HC_K // KV_W))
    vp_spec = pl.BlockSpec((WINDOW, KV_W), lambda b, i: (prev(b, i), HC_V // KV_W))
    vo_spec = pl.BlockSpec((WINDOW, KV_W), lambda b, i: (b * nb + i, HC_V // KV_W))
    bias_spec = pl.BlockSpec((N_HEADS_A, WINDOW, 2 * WINDOW), lambda b, i: (0, 0, 0))
    sink_spec = pl.BlockSpec((1, N_HEADS_A), lambda b, i: (0, 0))
    return [q_spec, kp_spec, ko_spec, vp_spec, vo_spec, bias_spec, sink_spec]


def _attn_fwd(hcat, bias, sink, nbatch, *, name):
    t = hcat.shape[0]
    nb = t // nbatch // WINDOW

    def body(q_ref, kp_ref, ko_ref, vp_ref, vo_ref, bias_ref, sink_ref, o_ref):
        first = pl.program_id(1) == 0
        for g in range(N_KV_A):
            _, _, v2, p, _ = _attn_scores(q_ref, kp_ref, ko_ref, vp_ref, vo_ref, bias_ref, sink_ref, g, first)
            o = _dot(p, v2)
            o_ref[:, g * GROUP_A * HEAD_DIM_A:(g + 1) * GROUP_A * HEAD_DIM_A] = jnp.concatenate(
                [o[h * WINDOW:(h + 1) * WINDOW] for h in range(GROUP_A)], axis=1).astype(BF16)

    return pl.pallas_call(
        body, name=name, grid=(nbatch, nb),
        in_specs=_attn_specs(nb),
        out_specs=pl.BlockSpec((WINDOW, Q_A), lambda b, i: (b * nb + i, 0)),
        out_shape=jax.ShapeDtypeStruct((t, Q_A), BF16),
        compiler_params=_params(dimension_semantics=("parallel", "arbitrary")),
    )(hcat, hcat, hcat, hcat, hcat, bias, sink.reshape(1, N_HEADS_A))


def _attn_bwd(hcat, bias, sink, do, nbatch, *, name):
    t = hcat.shape[0]
    nb = t // nbatch // WINDOW
    hd = HEAD_DIM_A

    def body(q_ref, kp_ref, ko_ref, vp_ref, vo_ref, bias_ref, sink_ref, do_ref,
             dq_ref, dk_ref, dv_ref, dbias_ref, dsink_ref, ck_ref, cv_ref):
        b = pl.program_id(0)
        j = pl.program_id(1)
        first = j == nb - 1

        @pl.when((b == 0) & (j == 0))
        def _():
            dbias_ref[...] = jnp.zeros_like(dbias_ref)
            dsink_ref[...] = jnp.zeros_like(dsink_ref)

        @pl.when(j == 0)
        def _():
            ck_ref[...] = jnp.zeros_like(ck_ref)
            cv_ref[...] = jnp.zeros_like(cv_ref)

        do_ = do_ref[...]
        lane = lax.broadcasted_iota(jnp.int32, (1, N_HEADS_A), 1)
        dsink = jnp.zeros((1, N_HEADS_A), F32)
        dq_parts, dk_own, dv_own, dk_prev, dv_prev = [], [], [], [], []
        for g in range(N_KV_A):
            qs, k2, v2, p, ps = _attn_scores(q_ref, kp_ref, ko_ref, vp_ref, vo_ref, bias_ref, sink_ref, g, first)
            dos = jnp.concatenate([do_[:, (GROUP_A * g + h) * hd:(GROUP_A * g + h + 1) * hd] for h in range(GROUP_A)], axis=0)
            dv2 = _dot(p, dos, _TN)
            dp = _dot(dos, v2, _NT)
            delta = jnp.sum(p * dp, axis=-1, keepdims=True)
            ds = p * (dp - delta)
            dsk = -(ps * delta)
            for h in range(GROUP_A):
                tot = jnp.sum(dsk[h * WINDOW:(h + 1) * WINDOW], axis=0, keepdims=True)
                dsink = dsink + jnp.where(lane == GROUP_A * g + h, tot, 0.0)
            dbias_ref[GROUP_A * g:GROUP_A * (g + 1)] += ds.reshape(GROUP_A, WINDOW, 2 * WINDOW)
            dqs = _dot(ds, k2) * (hd ** -0.5)
            dk2 = _dot(ds, qs, _TN)
            dq_parts.append(jnp.concatenate([dqs[h * WINDOW:(h + 1) * WINDOW] for h in range(GROUP_A)], axis=1))
            dk_prev.append(dk2[:WINDOW])
            dk_own.append(dk2[WINDOW:])
            dv_prev.append(dv2[:WINDOW])
            dv_own.append(dv2[WINDOW:])
        dq_ref[...] = jnp.concatenate(dq_parts, axis=1).astype(BF16)
        dk_ref[...] = (jnp.concatenate(dk_own, axis=1) + ck_ref[...]).astype(BF16)
        dv_ref[...] = (jnp.concatenate(dv_own, axis=1) + cv_ref[...]).astype(BF16)
        ck_ref[...] = jnp.concatenate(dk_prev, axis=1)
        cv_ref[...] = jnp.concatenate(dv_prev, axis=1)
        dsink_ref[...] += dsink

    def rev(spec):
        return pl.BlockSpec(spec.block_shape, lambda b, j, f=spec.index_map: f(b, nb - 1 - j))

    in_specs = [rev(s) for s in _attn_specs(nb)[:5]] + _attn_specs(nb)[5:]
    in_specs.append(pl.BlockSpec((WINDOW, Q_A), lambda b, j: (b * nb + nb - 1 - j, 0)))
    return pl.pallas_call(
        body, name=name, grid=(nbatch, nb),
        in_specs=in_specs,
        out_specs=[pl.BlockSpec((WINDOW, Q_A), lambda b, j: (b * nb + nb - 1 - j, 0)),
                   pl.BlockSpec((WINDOW, KV_W), lambda b, j: (b * nb + nb - 1 - j, 0)),
                   pl.BlockSpec((WINDOW, KV_W), lambda b, j: (b * nb + nb - 1 - j, 0)),
                   pl.BlockSpec((N_HEADS_A, WINDOW, 2 * WINDOW), lambda b, j: (0, 0, 0)),
                   pl.BlockSpec((1, N_HEADS_A), lambda b, j: (0, 0))],
        out_shape=[jax.ShapeDtypeStruct((t, Q_A), BF16), jax.ShapeDtypeStruct((t, KV_W), BF16),
                   jax.ShapeDtypeStruct((t, KV_W), BF16),
                   jax.ShapeDtypeStruct((N_HEADS_A, WINDOW, 2 * WINDOW), F32),
                   jax.ShapeDtypeStruct((1, N_HEADS_A), F32)],
        scratch_shapes=[pltpu.VMEM((WINDOW, KV_W), F32), pltpu.VMEM((WINDOW, KV_W), F32)],
        compiler_params=_params(dimension_semantics=("arbitrary", "arbitrary")),
    )(hcat, hcat, hcat, hcat, hcat, bias, sink.reshape(1, N_HEADS_A), do)


def _shift_down(x, halo8, s):
    if s == 0:
        return x
    rolled = pltpu.roll(x, s, axis=0)
    row8 = lax.broadcasted_iota(jnp.int32, halo8.shape, 0)
    top = jnp.where(row8 < s, pltpu.roll(halo8, s, axis=0), rolled[0:8])
    return jnp.concatenate([top, rolled[8:]], axis=0)


def _shift_up(x, halo8, s):
    if s == 0:
        return x
    n = x.shape[0]
    rolled = pltpu.roll(x, n - s, axis=0)
    row8 = lax.broadcasted_iota(jnp.int32, halo8.shape, 0)
    bottom = jnp.where(row8 >= 8 - s, pltpu.roll(halo8, 8 - s, axis=0), rolled[n - 8:n])
    return jnp.concatenate([rolled[:n - 8], bottom], axis=0)


def _l2n(x, scale):
    r = lax.rsqrt(jnp.sum(x * x, axis=-1, keepdims=True) + NORM_EPS)
    return x * (r * scale)


def _conv_prep(hcat, conv_w, nbatch, *, name):
    t = hcat.shape[0]
    nt = t // nbatch // ROW_T
    cb = HC_CONV // CONV_CH

    def body(u_ref, halo_ref, w_ref, c_ref, q_ref, k_ref, v_ref):
        i = pl.program_id(1)
        u = u_ref[...]
        halo = jnp.where(i == 0, 0.0, halo_ref[...])
        c = jnp.zeros_like(u)
        for j in range(CONV_K):
            c = c + w_ref[j:j + 1, :] * _shift_down(u, halo, CONV_K - 1 - j)
        c_ref[...] = c
        s = _silu(c)
        for h in range(N_HEADS_B):
            lo, hi = h * KEY_DIM_B, (h + 1) * KEY_DIM_B
            q_ref[:, lo:hi] = _l2n(s[:, lo:hi], KEY_DIM_B ** -0.5)
            k_ref[:, lo:hi] = _l2n(s[:, QK_B + lo:QK_B + hi], 1.0)
        v_ref[...] = s[:, 2 * QK_B:]

    row = lambda w: pl.BlockSpec((ROW_T, w), lambda b, i: (b * nt + i, 0))
    return pl.pallas_call(
        body, name=name, grid=(nbatch, nt),
        in_specs=[pl.BlockSpec((ROW_T, CONV_CH), lambda b, i: (b * nt + i, cb)),
                  pl.BlockSpec((8, CONV_CH), lambda b, i: (jnp.maximum((b * nt + i) * (ROW_T // 8) - 1, 0), cb)),
                  pl.BlockSpec((CONV_K, CONV_CH), lambda b, i: (0, 0))],
        out_specs=[row(CONV_CH), row(QK_B), row(QK_B), row(V_B)],
        out_shape=[jax.ShapeDtypeStruct((t, CONV_CH), F32)] + [jax.ShapeDtypeStruct((t, QK_B), F32)] * 3,
        compiler_params=_params(dimension_semantics=("parallel", "parallel")),
    )(hcat, hcat, conv_w)


def _conv_prep_bwd_pointwise(dq, dk, dv, c, *, name):
    t = c.shape[0]

    def l2n_bwd(x, dy, scale):
        r = lax.rsqrt(jnp.sum(x * x, axis=-1, keepdims=True) + NORM_EPS)
        return scale * (r * dy - x * (r * r * r) * jnp.sum(x * dy, axis=-1, keepdims=True))

    def body(dq_ref, dk_ref, dv_ref, c_ref, dc_ref):
        c_ = c_ref[...]
        s = _silu(c_)
        ds = _dsilu(c_)
        for h in range(N_HEADS_B):
            lo, hi = h * KEY_DIM_B, (h + 1) * KEY_DIM_B
            dc_ref[:, lo:hi] = l2n_bwd(s[:, lo:hi], dq_ref[:, lo:hi], KEY_DIM_B ** -0.5) * ds[:, lo:hi]
            dc_ref[:, QK_B + lo:QK_B + hi] = (l2n_bwd(s[:, QK_B + lo:QK_B + hi], dk_ref[:, lo:hi], 1.0)
                                              * ds[:, QK_B + lo:QK_B + hi])
        dc_ref[:, 2 * QK_B:] = dv_ref[...] * ds[:, 2 * QK_B:]

    row = lambda w: pl.BlockSpec((ROW_T, w), lambda i: (i, 0))
    return pl.pallas_call(
        body, name=name, grid=(t // ROW_T,),
        in_specs=[row(QK_B), row(QK_B), row(V_B), row(CONV_CH)], out_specs=row(CONV_CH),
        out_shape=jax.ShapeDtypeStruct((t, CONV_CH), F32),
        compiler_params=_params(dimension_semantics=("parallel",)),
    )(dq, dk, dv, c)


def _conv_bwd(dc, hcat, conv_w, nbatch, *, name):
    t = dc.shape[0]
    nt = t // nbatch // ROW_T
    cb = HC_CONV // CONV_CH
    last_blk = t // 8 - 1

    def body(dc_ref, dnext_ref, u_ref, uprev_ref, w_ref, du_ref, dw_ref):
        b = pl.program_id(0)
        i = pl.program_id(1)
        dc_ = dc_ref[...]
        u = u_ref[...]
        dnext = jnp.where(i == nt - 1, 0.0, dnext_ref[...])
        uprev = jnp.where(i == 0, 0.0, uprev_ref[...])
        du = jnp.zeros_like(dc_)
        rows = []
        for j in range(CONV_K):
            s = CONV_K - 1 - j
            du = du + w_ref[j:j + 1, :] * _shift_up(dc_, dnext, s)
            rows.append(jnp.sum(dc_ * _shift_down(u, uprev, s), axis=0, keepdims=True))
        du_ref[...] = du.astype(BF16)
        dw_p = jnp.concatenate(rows + [jnp.zeros((8 - CONV_K, CONV_CH), F32)], axis=0)

        @pl.when((b == 0) & (i == 0))
        def _():
            dw_ref[...] = dw_p

        @pl.when((b > 0) | (i > 0))
        def _():
            dw_ref[...] += dw_p

    return pl.pallas_call(
        body, name=name, grid=(nbatch, nt),
        in_specs=[pl.BlockSpec((ROW_T, CONV_CH), lambda b, i: (b * nt + i, 0)),
                  pl.BlockSpec((8, CONV_CH), lambda b, i: (jnp.minimum((b * nt + i + 1) * (ROW_T // 8), last_blk), 0)),
                  pl.BlockSpec((ROW_T, CONV_CH), lambda b, i: (b * nt + i, cb)),
                  pl.BlockSpec((8, CONV_CH), lambda b, i: (jnp.maximum((b * nt + i) * (ROW_T // 8) - 1, 0), cb)),
                  pl.BlockSpec((CONV_K, CONV_CH), lambda b, i: (0, 0))],
        out_specs=[pl.BlockSpec((ROW_T, CONV_CH), lambda b, i: (b * nt + i, 0)),
                   pl.BlockSpec((8, CONV_CH), lambda b, i: (0, 0))],
        out_shape=[jax.ShapeDtypeStruct((t, CONV_CH), BF16), jax.ShapeDtypeStruct((8, CONV_CH), F32)],
        compiler_params=_params(dimension_semantics=("arbitrary", "arbitrary")),
    )(dc, dc, hcat, hcat, conv_w)


def _softplus(x):
    return jnp.maximum(x, 0.0) + jnp.log(1.0 + jnp.exp(-jnp.abs(x)))


def _gates(hcat, a_row, dt_row, *, name):
    t = hcat.shape[0]

    def body(bd_ref, a_ref, dt_ref, gb_ref, bb_ref):
        bd = bd_ref[...]
        beta = _sigmoid(bd)
        g = -jnp.exp(a_ref[...]) * _softplus(bd + dt_ref[...])
        for h in range(N_HEADS_B):
            lo, hi = h * VAL_DIM_B, (h + 1) * VAL_DIM_B
            bb_ref[:, lo:hi] = jnp.broadcast_to(beta[:, h:h + 1], (ROW_T, VAL_DIM_B))
            gb_ref[:, lo:hi] = jnp.broadcast_to(g[:, N_HEADS_B + h:N_HEADS_B + h + 1], (ROW_T, VAL_DIM_B))

    vec = pl.BlockSpec((1, 128), lambda i: (0, 0))
    row = pl.BlockSpec((ROW_T, V_B), lambda i: (i, 0))
    return pl.pallas_call(
        body, name=name, grid=(t // ROW_T,),
        in_specs=[pl.BlockSpec((ROW_T, 128), lambda i: (i, HC_BD // 128)), vec, vec],
        out_specs=[row, row], out_shape=[jax.ShapeDtypeStruct((t, V_B), F32)] * 2,
        compiler_params=_params(dimension_semantics=("parallel",)),
    )(hcat, a_row, dt_row)


def _gates_bwd(dgb, dbb, hcat, a_row, dt_row, *, name):
    t = hcat.shape[0]

    def body(dgb_ref, dbb_ref, bd_ref, a_ref, dt_ref, dbd_ref, da_ref, ddt_ref):
        i = pl.program_id(0)
        bd = bd_ref[...]
        beta = _sigmoid(bd)
        ea = jnp.exp(a_ref[...])
        x = bd + dt_ref[...]
        g = -ea * _softplus(x)
        lane = lax.broadcasted_iota(jnp.int32, (ROW_T, 128), 1)
        dbeta = jnp.zeros((ROW_T, 128), F32)
        dg = jnp.zeros((ROW_T, 128), F32)
        for h in range(N_HEADS_B):
            lo, hi = h * VAL_DIM_B, (h + 1) * VAL_DIM_B
            dbeta = dbeta + jnp.where(lane == h, jnp.sum(dbb_ref[:, lo:hi], axis=-1, keepdims=True), 0.0)
            dg = dg + jnp.where(lane == N_HEADS_B + h, jnp.sum(dgb_ref[:, lo:hi], axis=-1, keepdims=True), 0.0)
        ddt_raw = dg * (-ea) * _sigmoid(x)
        dbd_ref[...] = (dbeta * beta * (1.0 - beta) + ddt_raw).astype(BF16)
        da_p = jnp.sum(dg * g, axis=0, keepdims=True)
        ddt_p = jnp.sum(ddt_raw, axis=0, keepdims=True)

        @pl.when(i == 0)
        def _():
            da_ref[...] = da_p
            ddt_ref[...] = ddt_p

        @pl.when(i > 0)
        def _():
            da_ref[...] += da_p
            ddt_ref[...] += ddt_p

    vec = pl.BlockSpec((1, 128), lambda i: (0, 0))
    row = pl.BlockSpec((ROW_T, V_B), lambda i: (i, 0))
    return pl.pallas_call(
        body, name=name, grid=(t // ROW_T,),
        in_specs=[row, row, pl.BlockSpec((ROW_T, 128), lambda i: (i, HC_BD // 128)), vec, vec],
        out_specs=[pl.BlockSpec((ROW_T, 128), lambda i: (i, 0)), vec, vec],
        out_shape=[jax.ShapeDtypeStruct((t, 128), BF16), jax.ShapeDtypeStruct((1, 128), F32),
                   jax.ShapeDtypeStruct((1, 128), F32)],
        compiler_params=_params(dimension_semantics=("arbitrary",)),
    )(dgb, dbb, hcat, a_row, dt_row)


def _group_masks():
    r = lax.broadcasted_iota(jnp.int32, (GROUP_T, GROUP_T), 0)
    c = lax.broadcasted_iota(jnp.int32, (GROUP_T, GROUP_T), 1)
    same = (r // CHUNK) == (c // CHUNK)
    return same, same & (r >= c), same & (r > c)


@jax.custom_vjp
def _inv_unit_lower(low):
    eye = (lax.broadcasted_iota(jnp.int32, low.shape, 0) == lax.broadcasted_iota(jnp.int32, low.shape, 1)).astype(F32)
    p = -low
    t = eye + p
    for _ in range(int(math.log2(CHUNK)) - 1):
        p = _dot(p, p, exact=True)
        t = t + _dot(t, p, exact=True)
    return t


def _inv_fwd(low):
    t = _inv_unit_lower(low)
    return t, t


def _inv_bwd(t, dt):
    return (-_dot(t, _dot(dt, t, _NT, exact=True), _TN, exact=True),)


_inv_unit_lower.defvjp(_inv_fwd, _inv_bwd)


def _dn_prep_head(q, k, v, gb, bb):
    same, causal, strict = _group_masks()
    gc = _dot(causal.astype(F32), gb, exact=True)
    glast = _dot(same.astype(F32), gb, exact=True)
    gc_row = gc.T[0:1, :]
    decay = jnp.exp(jnp.where(causal, gc[:, 0:1] - gc_row, NEG_INF))
    kb = k * bb
    vb = v * bb
    lower = jnp.where(strict, _dot(kb, k, _NT) * decay, 0.0)
    tinv = _inv_unit_lower(lower)
    u = _dot(tinv, vb)
    w = _dot(tinv, kb * jnp.exp(gc))
    a = jnp.where(causal, _dot(q, k, _NT) * decay, 0.0)
    a = a[:, 0:CHUNK] + a[:, CHUNK:2 * CHUNK] + a[:, 2 * CHUNK:3 * CHUNK] + a[:, 3 * CHUNK:4 * CHUNK]
    k_tail = k * jnp.exp(glast - gc)
    q_dec = q * jnp.exp(gc)
    return u, w, q_dec, k_tail, a, glast


def _dn_prep(q, k, v, gb, bb, *, name):
    t = q.shape[0]

    def body(q_ref, k_ref, v_ref, gb_ref, bb_ref, u_ref, w_ref, qd_ref, kt_ref, a_ref, gl_ref):
        for h in range(N_HEADS_B):
            sl = slice(h * KEY_DIM_B, (h + 1) * KEY_DIM_B)
            u, w, qd, kt, a, gl = _dn_prep_head(q_ref[:, sl], k_ref[:, sl], v_ref[:, sl], gb_ref[:, sl], bb_ref[:, sl])
            u_ref[:, sl] = u
            w_ref[:, sl] = w
            qd_ref[:, sl] = qd
            kt_ref[:, sl] = kt
            gl_ref[:, sl] = gl
            a_ref[:, h * CHUNK:(h + 1) * CHUNK] = a

    row = pl.BlockSpec((GROUP_T, V_B), lambda i: (i, 0))
    arow = pl.BlockSpec((GROUP_T, N_HEADS_B * CHUNK), lambda i: (i, 0))
    big = jax.ShapeDtypeStruct((t, V_B), F32)
    return pl.pallas_call(
        body, name=name, grid=(t // GROUP_T,),
        in_specs=[row] * 5, out_specs=[row, row, row, row, arow, row],
        out_shape=[big, big, big, big, jax.ShapeDtypeStruct((t, N_HEADS_B * CHUNK), F32), big],
        compiler_params=_params(dimension_semantics=("parallel",)),
    )(q, k, v, gb, bb)


def _dn_prep_bwd(q, k, v, gb, bb, du, dw, dqd, dkt, da, dgl, *, name):
    t = q.shape[0]

    def body(q_ref, k_ref, v_ref, gb_ref, bb_ref, du_ref, dw_ref, dqd_ref, dkt_ref, da_ref, dgl_ref,
             dq_ref, dk_ref, dv_ref, dgb_ref, dbb_ref):
        for h in range(N_HEADS_B):
            sl = slice(h * KEY_DIM_B, (h + 1) * KEY_DIM_B)
            _, vjp = jax.vjp(_dn_prep_head, q_ref[:, sl], k_ref[:, sl], v_ref[:, sl], gb_ref[:, sl], bb_ref[:, sl])
            dq, dk, dv, dgb, dbb = vjp((du_ref[:, sl], dw_ref[:, sl], dqd_ref[:, sl], dkt_ref[:, sl],
                                        da_ref[:, h * CHUNK:(h + 1) * CHUNK], dgl_ref[:, sl]))
            dq_ref[:, sl] = dq
            dk_ref[:, sl] = dk
            dv_ref[:, sl] = dv
            dgb_ref[:, sl] = dgb
            dbb_ref[:, sl] = dbb

    row = pl.BlockSpec((GROUP_T, V_B), lambda i: (i, 0))
    arow = pl.BlockSpec((GROUP_T, N_HEADS_B * CHUNK), lambda i: (i, 0))
    big = jax.ShapeDtypeStruct((t, V_B), F32)
    return pl.pallas_call(
        body, name=name, grid=(t // GROUP_T,),
        in_specs=[row] * 9 + [arow, row], out_specs=[row] * 5, out_shape=[big] * 5,
        compiler_params=_params(dimension_semantics=("parallel",)),
    )(q, k, v, gb, bb, du, dw, dqd, dkt, da, dgl)


def _dn_step(s, qd, kt, u, w, a, gl):
    v_new = u - _dot(w, s)
    o = _dot(qd, s) + _dot(a, v_new)
    s_new = s * jnp.exp(gl[0:1, :]) + _dot(kt, v_new, _TN)
    return s_new, o


def _dn_scan(u, w, qd, kt, a, gl, nbatch, *, name):
    t = u.shape[0]
    ng = t // nbatch // GROUP_T
    cpg = GROUP_T // CHUNK

    def body(u_ref, w_ref, qd_ref, kt_ref, a_ref, gl_ref, o_ref, ss_ref, s_ref):
        @pl.when(pl.program_id(1) == 0)
        def _():
            s_ref[...] = jnp.zeros_like(s_ref)

        def chunk(c, carry):
            rows = pl.ds(pl.multiple_of(c * CHUNK, CHUNK), CHUNK)
            for h in range(N_HEADS_B):
                sl = slice(h * KEY_DIM_B, (h + 1) * KEY_DIM_B)
                s = s_ref[h]
                ss_ref[c, h] = s
                s_new, o = _dn_step(s, qd_ref[rows, sl], kt_ref[rows, sl], u_ref[rows, sl], w_ref[rows, sl],
                                    a_ref[rows, h * CHUNK:(h + 1) * CHUNK], gl_ref[rows, sl])
                s_ref[h] = s_new
                o_ref[rows, sl] = o
            return carry

        lax.fori_loop(0, cpg, chunk, 0)

    row = pl.BlockSpec((GROUP_T, V_B), lambda b, i: (b * ng + i, 0))
    arow = pl.BlockSpec((GROUP_T, N_HEADS_B * CHUNK), lambda b, i: (b * ng + i, 0))
    return pl.pallas_call(
        body, name=name, grid=(nbatch, ng),
        in_specs=[row, row, row, row, arow, row],
        out_specs=[row, pl.BlockSpec((cpg, N_HEADS_B, KEY_DIM_B, VAL_DIM_B), lambda b, i: (b * ng + i, 0, 0, 0))],
        out_shape=[jax.ShapeDtypeStruct((t, V_B), F32),
                   jax.ShapeDtypeStruct((t // CHUNK, N_HEADS_B, KEY_DIM_B, VAL_DIM_B), F32)],
        scratch_shapes=[pltpu.VMEM((N_HEADS_B, KEY_DIM_B, VAL_DIM_B), F32)],
        compiler_params=_params(dimension_semantics=("parallel", "arbitrary")),
    )(u, w, qd, kt, a, gl)


def _dn_scan_bwd(u, w, qd, kt, a, gl, ss, do, nbatch, *, name):
    t = u.shape[0]
    ng = t // nbatch // GROUP_T
    cpg = GROUP_T // CHUNK

    def body(u_ref, w_ref, qd_ref, kt_ref, a_ref, gl_ref, ss_ref, do_ref,
             du_ref, dw_ref, dqd_ref, dkt_ref, da_ref, dgl_ref, ds_ref):
        @pl.when(pl.program_id(1) == 0)
        def _():
            ds_ref[...] = jnp.zeros_like(ds_ref)

        def chunk(cc, carry):
            c = cpg - 1 - cc
            rows = pl.ds(pl.multiple_of(c * CHUNK, CHUNK), CHUNK)
            for h in range(N_HEADS_B):
                sl = slice(h * KEY_DIM_B, (h + 1) * KEY_DIM_B)
                asl = slice(h * CHUNK, (h + 1) * CHUNK)
                _, vjp = jax.vjp(_dn_step, ss_ref[c, h], qd_ref[rows, sl], kt_ref[rows, sl], u_ref[rows, sl],
                                 w_ref[rows, sl], a_ref[rows, asl], gl_ref[rows, sl])
                ds, dqd, dkt, du, dw, da, dgl = vjp((ds_ref[h], do_ref[rows, sl]))
                ds_ref[h] = ds
                dqd_ref[rows, sl] = dqd
                dkt_ref[rows, sl] = dkt
                du_ref[rows, sl] = du
                dw_ref[rows, sl] = dw
                da_ref[rows, asl] = da
                dgl_ref[rows, sl] = dgl
            return carry

        lax.fori_loop(0, cpg, chunk, 0)

    row = pl.BlockSpec((GROUP_T, V_B), lambda b, j: (b * ng + ng - 1 - j, 0))
    arow = pl.BlockSpec((GROUP_T, N_HEADS_B * CHUNK), lambda b, j: (b * ng + ng - 1 - j, 0))
    big = jax.ShapeDtypeStruct((t, V_B), F32)
    return pl.pallas_call(
        body, name=name, grid=(nbatch, ng),
        in_specs=[row, row, row, row, arow, row,
                  pl.BlockSpec((cpg, N_HEADS_B, KEY_DIM_B, VAL_DIM_B), lambda b, j: (b * ng + ng - 1 - j, 0, 0, 0)), row],
        out_specs=[row, row, row, row, arow, row],
        out_shape=[big, big, big, big, jax.ShapeDtypeStruct((t, N_HEADS_B * CHUNK), F32), big],
        scratch_shapes=[pltpu.VMEM((N_HEADS_B, KEY_DIM_B, VAL_DIM_B), F32)],
        compiler_params=_params(dimension_semantics=("parallel", "arbitrary")),
    )(u, w, qd, kt, a, gl, ss, do)


def _rms_gate(o, hcat, dn_g, *, name):
    t = o.shape[0]

    def body(o_ref, z_ref, g_ref, y_ref):
        for h in range(N_HEADS_B):
            sl = slice(h * VAL_DIM_B, (h + 1) * VAL_DIM_B)
            o_ = o_ref[:, sl]
            r = lax.rsqrt(jnp.mean(o_ * o_, axis=-1, keepdims=True) + NORM_EPS)
            y_ref[:, sl] = (o_ * r * g_ref[...] * _silu(z_ref[:, sl])).astype(BF16)

    row = pl.BlockSpec((ROW_T, V_B), lambda i: (i, 0))
    return pl.pallas_call(
        body, name=name, grid=(t // ROW_T,),
        in_specs=[row, pl.BlockSpec((ROW_T, V_B), lambda i: (i, HC_Z // V_B)), pl.BlockSpec((1, VAL_DIM_B), lambda i: (0, 0))],
        out_specs=row, out_shape=jax.ShapeDtypeStruct((t, V_B), BF16),
        compiler_params=_params(dimension_semantics=("parallel",)),
    )(o, hcat, dn_g.reshape(1, VAL_DIM_B))


def _rms_gate_bwd(dy, o, hcat, dn_g, *, name):
    t = o.shape[0]

    def body(dy_ref, o_ref, z_ref, g_ref, do_ref, dz_ref, dg_ref):
        i = pl.program_id(0)
        g = g_ref[...]
        dg_p = jnp.zeros((1, VAL_DIM_B), F32)
        for h in range(N_HEADS_B):
            sl = slice(h * VAL_DIM_B, (h + 1) * VAL_DIM_B)
            o_ = o_ref[:, sl]
            z_ = z_ref[:, sl]
            dy_ = dy_ref[:, sl]
            r = lax.rsqrt(jnp.mean(o_ * o_, axis=-1, keepdims=True) + NORM_EPS)
            n = o_ * r
            sz = _silu(z_)
            dz_ref[:, sl] = (dy_ * n * g * _dsilu(z_)).astype(BF16)
            dg_p = dg_p + jnp.sum(dy_ * n * sz, axis=0, keepdims=True)
            dn = dy_ * g * sz
            do_ref[:, sl] = r * dn - o_ * (r * r * r) * jnp.mean(o_ * dn, axis=-1, keepdims=True)

        @pl.when(i == 0)
        def _():
            dg_ref[...] = dg_p

        @pl.when(i > 0)
        def _():
            dg_ref[...] += dg_p

    row = pl.BlockSpec((ROW_T, V_B), lambda i: (i, 0))
    vec = pl.BlockSpec((1, VAL_DIM_B), lambda i: (0, 0))
    return pl.pallas_call(
        body, name=name, grid=(t // ROW_T,),
        in_specs=[row, row, pl.BlockSpec((ROW_T, V_B), lambda i: (i, HC_Z // V_B)), vec],
        out_specs=[row, row, vec],
        out_shape=[jax.ShapeDtypeStruct((t, V_B), F32), jax.ShapeDtypeStruct((t, V_B), BF16),
                   jax.ShapeDtypeStruct((1, VAL_DIM_B), F32)],
        compiler_params=_params(dimension_semantics=("arbitrary",)),
    )(dy, o, hcat, dn_g.reshape(1, VAL_DIM_B))


def _merge(ya, yb, hcat, *, name):
    t = ya.shape[0]

    def body(ya_ref, yb_ref, ga_ref, gb_ref, y_ref):
        y_ref[...] = (_sigmoid(ga_ref[...]) * ya_ref[...] + _sigmoid(gb_ref[...]) * yb_ref[...]).astype(BF16)

    row = pl.BlockSpec((ROW_T, D_MODEL), lambda i: (i, 0))
    return pl.pallas_call(
        body, name=name, grid=(t // ROW_T,),
        in_specs=[row, row, pl.BlockSpec((ROW_T, D_MODEL), lambda i: (i, HC_GATE // D_MODEL)),
                  pl.BlockSpec((ROW_T, D_MODEL), lambda i: (i, HC_GATE // D_MODEL + 1))],
        out_specs=row, out_shape=jax.ShapeDtypeStruct((t, D_MODEL), BF16),
        compiler_params=_params(dimension_semantics=("parallel",)),
    )(ya, yb, hcat, hcat)


def _merge_bwd(dmix, ya, yb, hcat, *, name):
    t = ya.shape[0]

    def body(d_ref, ya_ref, yb_ref, ga_ref, gb_ref, dya_ref, dyb_ref, dgate_ref):
        d = d_ref[...]
        sa = _sigmoid(ga_ref[...])
        sb = _sigmoid(gb_ref[...])
        dya_ref[...] = (d * sa).astype(BF16)
        dyb_ref[...] = (d * sb).astype(BF16)
        dgate_ref[:, :D_MODEL] = (d * ya_ref[...] * sa * (1.0 - sa)).astype(BF16)
        dgate_ref[:, D_MODEL:] = (d * yb_ref[...] * sb * (1.0 - sb)).astype(BF16)

    row = pl.BlockSpec((ROW_T, D_MODEL), lambda i: (i, 0))
    return pl.pallas_call(
        body, name=name, grid=(t // ROW_T,),
        in_specs=[row, row, row, pl.BlockSpec((ROW_T, D_MODEL), lambda i: (i, HC_GATE // D_MODEL)),
                  pl.BlockSpec((ROW_T, D_MODEL), lambda i: (i, HC_GATE // D_MODEL + 1))],
        out_specs=[row, row, pl.BlockSpec((ROW_T, 2 * D_MODEL), lambda i: (i, 0))],
        out_shape=[jax.ShapeDtypeStruct((t, D_MODEL), BF16)] * 2 + [jax.ShapeDtypeStruct((t, 2 * D_MODEL), BF16)],
        compiler_params=_params(dimension_semantics=("parallel",)),
    )(dmix, ya, yb, hcat, hcat)


def _loss_head(y, target, *, name):
    t, n = y.shape
    tm = _tile(t, (512, 256, 128))

    def body(y_ref, t_ref, part_ref, dy_ref):
        i = pl.program_id(0)
        e = y_ref[...] - t_ref[...]
        dy_ref[...] = e * (1.0 / n)
        p = jnp.sum((e * e).reshape(tm // 8, 8, n), axis=0) * (0.5 / n)

        @pl.when(i == 0)
        def _():
            part_ref[...] = p

        @pl.when(i > 0)
        def _():
            part_ref[...] += p

    row = pl.BlockSpec((tm, n), lambda i: (i, 0))
    return pl.pallas_call(
        body, name=name, grid=(t // tm,),
        in_specs=[row, row], out_specs=[pl.BlockSpec((8, n), lambda i: (0, 0)), row],
        out_shape=[jax.ShapeDtypeStruct((8, n), F32), jax.ShapeDtypeStruct((t, n), F32)],
        compiler_params=_params(dimension_semantics=("arbitrary",)),
    )(y, target)


def _adamw(w, g, m, v, *, name):
    shape = w.shape
    cols = shape[-1]
    rows = int(np.prod(shape[:-1]))
    w2, g2, m2, v2 = (a.reshape(rows, cols) for a in (w, g, m, v))
    tr = rows
    if rows * cols > 512 * 1024:
        tr = _tile(rows, tuple(c for c in (512, 256, 128, 64, 32, 16, 8) if c * cols <= 256 * 1024))

    def body(w_ref, g_ref, m_ref, v_ref, d_ref, nm_ref, nv_ref):
        g_ = g_ref[...]
        nm = ADAM_B1 * m_ref[...] + (1.0 - ADAM_B1) * g_
        nv = ADAM_B2 * v_ref[...] + (1.0 - ADAM_B2) * (g_ * g_)
        m_hat = nm / (1.0 - ADAM_B1 ** ADAM_STEP)
        v_hat = nv / (1.0 - ADAM_B2 ** ADAM_STEP)
        d_ref[...] = -ADAM_LR * (m_hat / (jnp.sqrt(v_hat) + ADAM_EPS) + ADAM_WD * w_ref[...])
        nm_ref[...] = nm
        nv_ref[...] = nv

    blk = pl.BlockSpec((tr, cols), lambda i: (i, 0))
    outs = pl.pallas_call(
        body, name=name, grid=(rows // tr,),
        in_specs=[blk] * 4, out_specs=[blk] * 3,
        out_shape=[jax.ShapeDtypeStruct((rows, cols), F32)] * 3,
        compiler_params=_params(dimension_semantics=("parallel",)),
    )(w2, g2, m2, v2)
    return tuple(o.reshape(shape) for o in outs)


def _repack_w_in(w_in):
    d = w_in.shape[0]
    o = 0
    parts = {}
    for nm, wd in (("q", Q_A), ("k", KV_W), ("v", KV_W), ("conv", CONV_CH), ("beta", N_HEADS_B), ("dt", N_HEADS_B),
                   ("z", V_B), ("gate", 2 * D_MODEL)):
        parts[nm] = w_in[:, o:o + wd]
        o += wd
    z = lambda n: jnp.zeros((d, n), w_in.dtype)
    return jnp.concatenate([parts["q"], parts["z"], parts["k"], parts["v"], parts["beta"], parts["dt"],
                            z(128 - 2 * N_HEADS_B), z(HC_CONV - HC_BD - 128), parts["conv"], parts["gate"]], axis=1)


def _unpack_dw_in(dw):
    return jnp.concatenate([dw[:, HC_Q:HC_Q + Q_A], dw[:, HC_K:HC_K + 2 * KV_W], dw[:, HC_CONV:HC_CONV + CONV_CH],
                            dw[:, HC_BD:HC_BD + 2 * N_HEADS_B], dw[:, HC_Z:HC_Z + V_B],
                            dw[:, HC_GATE:HC_GATE + 2 * D_MODEL]], axis=1)


def _lane_row(vals):
    return jnp.zeros((1, 128), F32).at[0, N_HEADS_B:2 * N_HEADS_B].set(vals.astype(F32))


def _local_step(x, target, wts):
    nbatch, seq, d = x.shape
    t = nbatch * seq
    depth = wts["w_in"].shape[0]
    x0 = x.reshape(t, d)
    tgt = target.reshape(t, d)

    onehot = jnp.asarray(_bucket_onehot())
    rel_t = jnp.pad(wts["rel_bias"].T, ((0, 0), (0, 128 - NUM_BUCKETS)))
    bias = _mm(rel_t, onehot, tb=True, exact=True, name="pos_bias")
    bias = bias.reshape(N_HEADS_A, WINDOW, 2 * WINDOW)

    saved = []
    xin = x0
    for i in range(depth):
        L = {}
        tag = f"_l{i}"
        L["x0"] = xin
        h = _mm(xin, wts["ffn_w13"][i, 0], name="ffn_up" + tag + "a")
        a = _act(h, name="ffn_act" + tag + "a")
        r1, x1 = _mm_res_ln(a, wts["ffn_w2"][i, 0], xin, wts["ln_g"][i, 0], wts["ln_b"][i, 0],
                            alpha=DN_ALPHA, c=0.5, name="ffn_down_ln" + tag + "a")
        L.update(h0=h, a0=a, r1=r1, x1=x1)
        w_in_p = wts["w_in_p"][i]
        hcat = _mm(x1, w_in_p, name="in_proj" + tag)
        ao = _attn_fwd(hcat, bias, wts["sinks"][i], nbatch, name="swa" + tag)
        ya = _mm(ao, wts["w_branch_a"][i], name="branch_a" + tag)
        c, qn, kn, vs = _conv_prep(hcat, wts["conv_w"][i], nbatch, name="conv_prep" + tag)
        a_row = _lane_row(wts["a_log"][i])
        dt_row = _lane_row(wts["dt_bias"][i])
        gb, bb = _gates(hcat, a_row, dt_row, name="gates" + tag)
        u, w, qd, kt, aa, gl = _dn_prep(qn, kn, vs, gb, bb, name="dn_prep" + tag)
        o, ss = _dn_scan(u, w, qd, kt, aa, gl, nbatch, name="dn_scan" + tag)
        on = _rms_gate(o, hcat, wts["dn_norm_g"][i], name="rms_gate" + tag)
        yb = _mm(on, wts["w_branch_b"][i], name="branch_b" + tag)
        mix = _merge(ya, yb, hcat, name="merge" + tag)
        r2, x2 = _mm_res_ln(mix, wts["w_out"][i], x1, wts["ln_g"][i, 1], wts["ln_b"][i, 1],
                            alpha=DN_ALPHA, c=1.0, name="out_proj_ln" + tag)
        L.update(hcat=hcat, ao=ao, ya=ya, c=c, qn=qn, kn=kn, vs=vs, gb=gb, bb=bb, a_row=a_row, dt_row=dt_row,
                 u=u, w=w, qd=qd, kt=kt, aa=aa, gl=gl, o=o, ss=ss, on=on, yb=yb, mix=mix, r2=r2, x2=x2)
        h = _mm(x2, wts["ffn_w13"][i, 1], name="ffn_up" + tag + "b")
        a = _act(h, name="ffn_act" + tag + "b")
        r3, x3 = _mm_res_ln(a, wts["ffn_w2"][i, 1], x2, wts["ln_g"][i, 2], wts["ln_b"][i, 2],
                            alpha=DN_ALPHA, c=0.5, name="ffn_down_ln" + tag + "b")
        L.update(h1=h, a1=a, r3=r3)
        saved.append(L)
        xin = x3

    part, dy = _loss_head(xin, tgt, name="loss_head")
    loss = jnp.sum(part)

    grads = {k: [None] * depth for k in ("ln_g", "ln_b", "ffn_w13", "ffn_w2", "w_in", "conv_w", "a_log", "dt_bias",
                                          "dn_norm_g", "sinks", "w_branch_a", "w_branch_b", "w_out")}
    dbias_total = None
    for i in reversed(range(depth)):
        L = saved[i]
        tag = f"_l{i}"
        dln_g, dln_b, dw13, dw2 = [None] * 3, [None] * 3, [None] * 2, [None] * 2

        def ffn_bwd(dyo, r, xprev, hsave, asave, j, sfx):
            dres, df, dln_g[2 * j], dln_b[2 * j] = _ln_bwd(dyo, r, wts["ln_g"][i, 2 * j], alpha=DN_ALPHA, c=0.5,
                                                           name="ln_bwd" + tag + sfx)
            da = _mm(df, wts["ffn_w2"][i, j], tb=True, name="ffn_down_bwd" + tag + sfx)
            dw2[j] = _mm(asave, df, ta=True, name="ffn_w2_grad" + tag + sfx)
            dh = _act_bwd(da, hsave, name="ffn_act_bwd" + tag + sfx)
            dw13[j] = _mm(xprev, dh, ta=True, name="ffn_w13_grad" + tag + sfx)
            return _mm(dh, wts["ffn_w13"][i, j], tb=True, add=dres, name="ffn_up_bwd" + tag + sfx)

        dx2 = ffn_bwd(dy, L["r3"], L["x2"], L["h1"], L["a1"], 1, "b")

        dres2, dymix, dln_g[1], dln_b[1] = _ln_bwd(dx2, L["r2"], wts["ln_g"][i, 1], alpha=DN_ALPHA, c=1.0,
                                                   name="ln_bwd" + tag + "m")
        hcat = L["hcat"]
        dmix = _mm(dymix, wts["w_out"][i], tb=True, name="out_proj_bwd" + tag)
        grads["w_out"][i] = _mm(L["mix"], dymix, ta=True, name="w_out_grad" + tag)
        dya, dyb, dgate = _merge_bwd(dmix, L["ya"], L["yb"], hcat, name="merge_bwd" + tag)
        dao = _mm(dya, wts["w_branch_a"][i], tb=True, name="branch_a_bwd" + tag)
        grads["w_branch_a"][i] = _mm(L["ao"], dya, ta=True, name="w_branch_a_grad" + tag)
        don = _mm(dyb, wts["w_branch_b"][i], tb=True, name="branch_b_bwd" + tag)
        grads["w_branch_b"][i] = _mm(L["on"], dyb, ta=True, name="w_branch_b_grad" + tag)
        do, dz, ddn = _rms_gate_bwd(don, L["o"], hcat, wts["dn_norm_g"][i], name="rms_gate_bwd" + tag)
        grads["dn_norm_g"][i] = ddn.reshape(VAL_DIM_B)
        du, dw, dqd, dkt, daa, dgl = _dn_scan_bwd(L["u"], L["w"], L["qd"], L["kt"], L["aa"], L["gl"], L["ss"], do,
                                                  nbatch, name="dn_scan_bwd" + tag)
        dqn, dkn, dvs, dgb, dbb = _dn_prep_bwd(L["qn"], L["kn"], L["vs"], L["gb"], L["bb"], du, dw, dqd, dkt, daa, dgl,
                                               name="dn_prep_bwd" + tag)
        dc = _conv_prep_bwd_pointwise(dqn, dkn, dvs, L["c"], name="conv_prep_bwd" + tag)
        dconv, dconv_w = _conv_bwd(dc, hcat, wts["conv_w"][i], nbatch, name="conv_bwd" + tag)
        grads["conv_w"][i] = dconv_w[:CONV_K]
        dbd, da_log, ddt = _gates_bwd(dgb, dbb, hcat, L["a_row"], L["dt_row"], name="gates_bwd" + tag)
        grads["a_log"][i] = da_log[0, N_HEADS_B:2 * N_HEADS_B]
        grads["dt_bias"][i] = ddt[0, N_HEADS_B:2 * N_HEADS_B]
        dq, dk, dv, dbias, dsink = _attn_bwd(hcat, bias, wts["sinks"][i], dao, nbatch, name="swa_bwd" + tag)
        grads["sinks"][i] = dsink.reshape(N_HEADS_A)
        dbias_total = dbias if dbias_total is None else dbias_total + dbias
        dhcat = jnp.concatenate([dq, dz, dk, dv, dbd, jnp.zeros((t, HC_CONV - HC_BD - 128), BF16), dconv, dgate], axis=1)
        dw_in_p = _mm(L["x1"], dhcat, ta=True, name="w_in_grad" + tag)
        grads["w_in"][i] = _unpack_dw_in(dw_in_p)
        dx1 = _mm(dhcat, wts["w_in_p"][i], tb=True, add=dres2, name="in_proj_bwd" + tag)

        dy = ffn_bwd(dx1, L["r1"], L["x0"], L["h0"], L["a0"], 0, "a")
        grads["ln_g"][i] = jnp.concatenate(dln_g, axis=0)
        grads["ln_b"][i] = jnp.concatenate(dln_b, axis=0)
        grads["ffn_w13"][i] = jnp.stack(dw13)
        grads["ffn_w2"][i] = jnp.stack(dw2)

    out = {k: jnp.stack(v) for k, v in grads.items()}
    drel = _mm(dbias_total.reshape(N_HEADS_A, WINDOW * 2 * WINDOW), onehot, name="rel_bias_grad")
    out["rel_bias"] = drel[:, :NUM_BUCKETS].T
    return loss, dy.reshape(nbatch, seq, d), out


N_CHIPS = 4
MESH_ID = pl.DeviceIdType.MESH
HBM_SPEC = pl.BlockSpec(memory_space=pltpu.HBM)
PACK_LANES = 1024


def _place():
    x, y, c = lax.axis_index("x"), lax.axis_index("y"), lax.axis_index("c")
    others = [(1 - x, y), (x, 1 - y), (1 - x, 1 - y)]
    return x, y, c, others


def _chip_index(cx, cy):
    return 2 * cx + cy


def _allgather_chips(tensors, *, name):
    n = len(tensors)

    def body(*refs):
        ins, outs = refs[:n], refs[n:2 * n]
        send_sems, recv_sems, local_sems = refs[2 * n:]
        x, y, c, others = _place()
        me = _chip_index(x, y)
        copies = []
        for i in range(n):
            loc = pltpu.make_async_copy(ins[i], outs[i].at[me], local_sems.at[i])
            loc.start()
            copies.append(loc)
            for k, (ox, oy) in enumerate(others):
                cp = pltpu.make_async_remote_copy(src_ref=ins[i], dst_ref=outs[i].at[me], send_sem=send_sems.at[i, k],
                                                  recv_sem=recv_sems.at[i, k], device_id=(ox, oy, c), device_id_type=MESH_ID)
                cp.start()
                copies.append(cp)
        for cp in copies:
            cp.wait()

    return pl.pallas_call(
        body, name=name,
        in_specs=[HBM_SPEC] * n, out_specs=[HBM_SPEC] * n,
        out_shape=[jax.ShapeDtypeStruct((N_CHIPS,) + t.shape, t.dtype) for t in tensors],
        scratch_shapes=[pltpu.SemaphoreType.DMA((n, 3)), pltpu.SemaphoreType.DMA((n, 3)), pltpu.SemaphoreType.DMA((n,))],
    )(*tensors)


def _swap_halves(p, *, name):
    def body(p_ref, got_ref, send_sem, recv_sem):
        x, y, c, _ = _place()
        cp = pltpu.make_async_remote_copy(src_ref=p_ref.at[1 - c], dst_ref=got_ref, send_sem=send_sem, recv_sem=recv_sem,
                                          device_id=(x, y, 1 - c), device_id_type=MESH_ID)
        cp.start()
        cp.wait()

    return pl.pallas_call(
        body, name=name, in_specs=[HBM_SPEC], out_specs=HBM_SPEC,
        out_shape=jax.ShapeDtypeStruct(p.shape[1:], p.dtype),
        scratch_shapes=[pltpu.SemaphoreType.DMA, pltpu.SemaphoreType.DMA],
    )(p)


def _add_own_half(p, got, c_idx, *, name):
    _, no, r, l = p.shape
    tr = _tile(r, (256, 128, 64, 32, 16, 8))

    def body(c_ref, p_ref, g_ref, o_ref):
        o_ref[...] = p_ref[...] + g_ref[...]

    return pl.pallas_call(
        body, name=name,
        grid_spec=pltpu.PrefetchScalarGridSpec(
            num_scalar_prefetch=1, grid=(no, r // tr),
            in_specs=[pl.BlockSpec((None, None, tr, l), lambda o, i, c_ref: (c_ref[0], o, i, 0)),
                      pl.BlockSpec((None, tr, l), lambda o, i, c_ref: (o, i, 0))],
            out_specs=pl.BlockSpec((None, tr, l), lambda o, i, c_ref: (o, i, 0))),
        out_shape=jax.ShapeDtypeStruct((no, r, l), p.dtype),
        compiler_params=_params(dimension_semantics=("parallel", "parallel")),
    )(c_idx, p, got)


def _exchange_chips(s1, *, name):
    def body(s_ref, g_ref, send_sems, recv_sems, local_sem):
        x, y, c, others = _place()
        me = _chip_index(x, y)
        loc = pltpu.make_async_copy(s_ref.at[me], g_ref.at[me], local_sem)
        loc.start()
        copies = [loc]
        for k, (ox, oy) in enumerate(others):
            cp = pltpu.make_async_remote_copy(src_ref=s_ref.at[_chip_index(ox, oy)], dst_ref=g_ref.at[me],
                                              send_sem=send_sems.at[k], recv_sem=recv_sems.at[k],
                                              device_id=(ox, oy, c), device_id_type=MESH_ID)
            cp.start()
            copies.append(cp)
        for cp in copies:
            cp.wait()

    return pl.pallas_call(
        body, name=name, in_specs=[HBM_SPEC], out_specs=HBM_SPEC,
        out_shape=jax.ShapeDtypeStruct(s1.shape, s1.dtype),
        scratch_shapes=[pltpu.SemaphoreType.DMA((3,)), pltpu.SemaphoreType.DMA((3,)), pltpu.SemaphoreType.DMA],
    )(s1)


def _sum_chips(g, *, name):
    no, r, l = g.shape
    tr = _tile(r, (256, 128, 64, 32, 16, 8))

    def body(g_ref, o_ref):
        o_ref[...] = ((g_ref[0] + g_ref[1]) + g_ref[2]) + g_ref[3]

    return pl.pallas_call(
        body, name=name, grid=(r // tr,),
        in_specs=[pl.BlockSpec((no, tr, l), lambda i: (0, i, 0))], out_specs=pl.BlockSpec((tr, l), lambda i: (i, 0)),
        out_shape=jax.ShapeDtypeStruct((r, l), g.dtype),
        compiler_params=_params(dimension_semantics=("parallel",)),
    )(g)


def _join_halves(f, *, name):
    def body(f_ref, h_ref, send_sem, recv_sem, local_sem):
        x, y, c, _ = _place()
        loc = pltpu.make_async_copy(f_ref, h_ref.at[c], local_sem)
        loc.start()
        cp = pltpu.make_async_remote_copy(src_ref=f_ref, dst_ref=h_ref.at[c], send_sem=send_sem, recv_sem=recv_sem,
                                          device_id=(x, y, 1 - c), device_id_type=MESH_ID)
        cp.start()
        cp.wait()
        loc.wait()

    return pl.pallas_call(
        body, name=name, in_specs=[HBM_SPEC], out_specs=HBM_SPEC,
        out_shape=jax.ShapeDtypeStruct((2,) + f.shape, f.dtype),
        scratch_shapes=[pltpu.SemaphoreType.DMA, pltpu.SemaphoreType.DMA, pltpu.SemaphoreType.DMA],
    )(f)


SHARD_AXIS = {"rel_bias": None, "ln_g": 2, "ln_b": 2, "ffn_w13": 3, "ffn_w2": 2, "w_in": 2, "conv_w": 2, "a_log": None,
              "dt_bias": None, "dn_norm_g": None, "sinks": None, "w_branch_a": 1, "w_branch_b": 1, "w_out": 1}
WEIGHT_NAMES = tuple(SHARD_AXIS)
MATRIX_NAMES = ("ffn_w13", "ffn_w2", "w_in", "w_branch_a", "w_branch_b", "w_out")


def _unshard(gathered, axis):
    g = jnp.moveaxis(gathered, 0, axis)
    return g.reshape(g.shape[:axis] + (g.shape[axis] * g.shape[axis + 1],) + g.shape[axis + 2:])


def _split_shards(full, axis):
    if axis is None:
        return jnp.broadcast_to(full.reshape(1, -1), (N_CHIPS, full.size))
    s = full.shape
    f = full.reshape(s[:axis] + (N_CHIPS, s[axis] // N_CHIPS) + s[axis + 1:])
    return jnp.moveaxis(f, axis, 0).reshape(N_CHIPS, -1)


def _reduce_scatter(grads):
    pieces = [_split_shards(grads[n].astype(F32), SHARD_AXIS[n]) for n in WEIGHT_NAMES]
    sizes = [p.shape[1] for p in pieces]
    total = sum(sizes)
    unit = 2 * 8 * PACK_LANES
    padded = -(-total // unit) * unit
    r = padded // (2 * PACK_LANES)
    vec = jnp.concatenate(pieces + [jnp.zeros((N_CHIPS, padded - total), F32)], axis=1)
    p = jnp.transpose(vec.reshape(N_CHIPS, 2, r, PACK_LANES), (1, 0, 2, 3))
    c_idx = lax.axis_index("c").astype(jnp.int32).reshape(1)
    got = _swap_halves(p, name="rs_swap_halves")
    s1 = _add_own_half(p, got, c_idx, name="rs_pair_sum")
    g = _exchange_chips(s1, name="rs_exchange_chips")
    f = _sum_chips(g, name="rs_chip_sum")
    h = _join_halves(f, name="rs_join_halves").reshape(-1)
    out, o = {}, 0
    for n, sz in zip(WEIGHT_NAMES, sizes):
        out[n] = h[o:o + sz]
        o += sz
    return out


def kernel(x, rel_bias, ln_g, ln_b, ffn_w13, ffn_w2, w_in, conv_w, a_log, dt_bias, dn_norm_g, sinks, w_branch_a, w_branch_b, w_out, loss_target, m_rel_bias, m_ln_g, m_ln_b, m_ffn_w13, m_ffn_w2, m_w_in, m_conv_w, m_a_log, m_dt_bias, m_dn_norm_g, m_sinks, m_w_branch_a, m_w_branch_b, m_w_out, v_rel_bias, v_ln_g, v_ln_b, v_ffn_w13, v_ffn_w2, v_w_in, v_conv_w, v_a_log, v_dt_bias, v_dn_norm_g, v_sinks, v_w_branch_a, v_w_branch_b, v_w_out):
    w = dict(rel_bias=rel_bias, ln_g=ln_g, ln_b=ln_b, ffn_w13=ffn_w13, ffn_w2=ffn_w2, w_in=w_in, conv_w=conv_w,
             a_log=a_log, dt_bias=dt_bias, dn_norm_g=dn_norm_g, sinks=sinks, w_branch_a=w_branch_a,
             w_branch_b=w_branch_b, w_out=w_out)
    m = dict(rel_bias=m_rel_bias, ln_g=m_ln_g, ln_b=m_ln_b, ffn_w13=m_ffn_w13, ffn_w2=m_ffn_w2, w_in=m_w_in,
             conv_w=m_conv_w, a_log=m_a_log, dt_bias=m_dt_bias, dn_norm_g=m_dn_norm_g, sinks=m_sinks,
             w_branch_a=m_w_branch_a, w_branch_b=m_w_branch_b, w_out=m_w_out)
    v = dict(rel_bias=v_rel_bias, ln_g=v_ln_g, ln_b=v_ln_b, ffn_w13=v_ffn_w13, ffn_w2=v_ffn_w2, w_in=v_w_in,
             conv_w=v_conv_w, a_log=v_a_log, dt_bias=v_dt_bias, dn_norm_g=v_dn_norm_g, sinks=v_sinks,
             w_branch_a=v_w_branch_a, w_branch_b=v_w_branch_b, w_out=v_w_out)

    sharded = [n for n in WEIGHT_NAMES if SHARD_AXIS[n] is not None]
    gathered = _allgather_chips([w[n].astype(MXU_DTYPE) if n in MATRIX_NAMES else w[n] for n in sharded],
                                name="weights_allgather")
    wts = {n: w[n] for n in WEIGHT_NAMES if SHARD_AXIS[n] is None}
    for n, g in zip(sharded, gathered):
        wts[n] = _unshard(g, SHARD_AXIS[n])
    wts["w_in_p"] = jax.vmap(_repack_w_in)(wts["w_in"])

    loss_part, grad_x, grads = _local_step(x, loss_target, wts)
    loss = lax.psum(loss_part, ("x", "y", "c"))

    red = _reduce_scatter(grads)
    g_out, d_out, m_out, v_out = [], [], [], []
    for n in WEIGHT_NAMES:
        g = red[n].reshape(w[n].shape)
        d, nm, nv = _adamw(w[n], g, m[n], v[n], name="adamw_" + n)
        g_out.append(g)
        d_out.append(d)
        m_out.append(nm)
        v_out.append(nv)
    return (loss, grad_x, *g_out, *d_out, *m_out, *v_out)
```

```python
import functools
import math

import numpy as np
import jax
import jax.numpy as jnp
from jax import lax
from jax.experimental import pallas as pl
from jax.experimental.pallas import tpu as pltpu

F32 = jnp.float32
BF16 = jnp.bfloat16
MXU_DTYPE = BF16
HIGHEST = lax.Precision.HIGHEST

D_MODEL = 1024
N_HEADS_A = 16
N_KV_A = 4
HEAD_DIM_A = 64
GROUP_A = N_HEADS_A // N_KV_A
WINDOW = 128
N_HEADS_B = 8
KEY_DIM_B = 128
VAL_DIM_B = 128
CONV_K = 4
CHUNK = 64
D_FF = 2816
NUM_BUCKETS = 32
MAX_DISTANCE = 128
DEPTH = 4
DN_ALPHA = (2 * DEPTH) ** 0.25
LN_EPS = 1e-5
NORM_EPS = 1e-6
NEG_INF = -1e30

Q_A = N_HEADS_A * HEAD_DIM_A
KV_W = N_KV_A * HEAD_DIM_A
QK_B = N_HEADS_B * KEY_DIM_B
V_B = N_HEADS_B * VAL_DIM_B
CONV_CH = 2 * QK_B + V_B
N_IN = Q_A + 2 * KV_W + CONV_CH + 2 * N_HEADS_B + V_B + 2 * D_MODEL

ADAM_LR = 0.001
ADAM_B1 = 0.9
ADAM_B2 = 0.999
ADAM_EPS = 1e-08
ADAM_WD = 0.01
ADAM_STEP = 10

HC_W = 8192
HC_Q = 0
HC_Z = 1024
HC_K = 2048
HC_V = 2304
HC_BD = 2560
HC_CONV = 3072
HC_GATE = 6144

GROUP_T = 256
ROW_T = 256
VMEM_LIMIT_BYTES = 48 * 1024 * 1024


def _params(**kw):
    return pltpu.CompilerParams(vmem_limit_bytes=VMEM_LIMIT_BYTES, **kw)


def _tile(n, cands):
    for c in cands:
        if n % c == 0:
            return c
    return n


def _dot(a, b, dims=(((1,), (0,)), ((), ())), exact=False):
    if exact:
        return lax.dot_general(a.astype(F32), b.astype(F32), dims, precision=HIGHEST, preferred_element_type=F32)
    return lax.dot_general(a.astype(MXU_DTYPE), b.astype(MXU_DTYPE), dims, preferred_element_type=F32)


_NN = (((1,), (0,)), ((), ()))
_NT = (((1,), (1,)), ((), ()))
_TN = (((0,), (0,)), ((), ()))


def _sigmoid(x):
    return 1.0 / (1.0 + jnp.exp(-x))


def _silu(x):
    return x * _sigmoid(x)


def _dsilu(x):
    s = _sigmoid(x)
    return s * (1.0 + x * (1.0 - s))


def _mm(a, b, *, ta=False, tb=False, out_dtype=F32, add=None, exact=False, name):
    (kdim, m) = a.shape if ta else a.shape[::-1]
    (n, kb) = b.shape if tb else b.shape[::-1]
    assert kdim == kb, (a.shape, b.shape, ta, tb)
    tm = _tile(m, (512, 256, 128))
    tn = _tile(n, (1024, 1408, 512, 256, 128))
    tk = _tile(kdim, (1024, 1408, 512, 256, 128))
    nk = kdim // tk
    dims = (((0 if ta else 1,), (1 if tb else 0,)), ((), ()))
    has_add = add is not None

    def body(*refs):
        if has_add:
            a_ref, b_ref, add_ref, o_ref = refs[:4]
        else:
            a_ref, b_ref, o_ref = refs[:3]
            add_ref = None
        part = _dot(a_ref[...], b_ref[...], dims, exact)

        def finish(acc):
            if has_add:
                acc = acc + add_ref[...].astype(F32)
            o_ref[...] = acc.astype(out_dtype)

        if nk == 1:
            finish(part)
        else:
            acc_ref = refs[-1]
            k = pl.program_id(2)

            @pl.when(k == 0)
            def _():
                acc_ref[...] = part

            @pl.when(k > 0)
            def _():
                acc_ref[...] += part

            @pl.when(k == nk - 1)
            def _():
                finish(acc_ref[...])

    a_spec = pl.BlockSpec((tk, tm), lambda i, j, k: (k, i)) if ta else pl.BlockSpec((tm, tk), lambda i, j, k: (i, k))
    b_spec = pl.BlockSpec((tn, tk), lambda i, j, k: (j, k)) if tb else pl.BlockSpec((tk, tn), lambda i, j, k: (k, j))
    o_spec = pl.BlockSpec((tm, tn), lambda i, j, k: (i, j))
    in_specs = [a_spec, b_spec] + ([o_spec] if has_add else [])
    args = (a, b) + ((add,) if has_add else ())
    return pl.pallas_call(
        body, name=name, grid=(m // tm, n // tn, nk),
        in_specs=in_specs, out_specs=o_spec,
        out_shape=jax.ShapeDtypeStruct((m, n), out_dtype),
        scratch_shapes=[pltpu.VMEM((tm, tn), F32)] if nk > 1 else [],
        compiler_params=_params(dimension_semantics=("parallel", "parallel", "arbitrary")),
    )(*args)


def _layernorm_rows(r, g, b):
    mu = jnp.mean(r, axis=-1, keepdims=True)
    xc = r - mu
    var = jnp.mean(xc * xc, axis=-1, keepdims=True)
    return xc * lax.rsqrt(var + LN_EPS) * g + b


def _mm_res_ln(a, w, resid, g, b, *, alpha, c, name):
    m, kdim = a.shape
    n = w.shape[1]
    tm = _tile(m, (512, 256, 128))

    def body(a_ref, w_ref, x_ref, g_ref, b_ref, r_ref, y_ref):
        f = _dot(a_ref[...], w_ref[...])
        r = alpha * x_ref[...] + c * f
        r_ref[...] = r
        y_ref[...] = _layernorm_rows(r, g_ref[...], b_ref[...])

    row = pl.BlockSpec((tm, n), lambda i: (i, 0))
    vec = pl.BlockSpec((1, n), lambda i: (0, 0))
    return pl.pallas_call(
        body, name=name, grid=(m // tm,),
        in_specs=[pl.BlockSpec((tm, kdim), lambda i: (i, 0)), pl.BlockSpec((kdim, n), lambda i: (0, 0)), row, vec, vec],
        out_specs=[row, row],
        out_shape=[jax.ShapeDtypeStruct((m, n), F32)] * 2,
        compiler_params=_params(dimension_semantics=("parallel",)),
    )(a, w, resid, g.reshape(1, n), b.reshape(1, n))


def _ln_bwd(dy, r, g, *, alpha, c, name):
    m, n = dy.shape
    tm = _tile(m, (512, 256, 128))

    def body(dy_ref, r_ref, g_ref, dres_ref, dbr_ref, dg_ref, db_ref):
        i = pl.program_id(0)
        dy_ = dy_ref[...]
        r_ = r_ref[...]
        mu = jnp.mean(r_, axis=-1, keepdims=True)
        xc = r_ - mu
        var = jnp.mean(xc * xc, axis=-1, keepdims=True)
        rstd = lax.rsqrt(var + LN_EPS)
        xh = xc * rstd
        dxh = dy_ * g_ref[...]
        dr = rstd * (dxh - jnp.mean(dxh, axis=-1, keepdims=True) - xh * jnp.mean(dxh * xh, axis=-1, keepdims=True))
        dres_ref[...] = alpha * dr
        dbr_ref[...] = (c * dr).astype(BF16)
        dg_p = jnp.sum(dy_ * xh, axis=0, keepdims=True)
        db_p = jnp.sum(dy_, axis=0, keepdims=True)

        @pl.when(i == 0)
        def _():
            dg_ref[...] = dg_p
            db_ref[...] = db_p

        @pl.when(i > 0)
        def _():
            dg_ref[...] += dg_p
            db_ref[...] += db_p

    row = pl.BlockSpec((tm, n), lambda i: (i, 0))
    vec = pl.BlockSpec((1, n), lambda i: (0, 0))
    return pl.pallas_call(
        body, name=name, grid=(m // tm,),
        in_specs=[row, row, vec], out_specs=[row, row, vec, vec],
        out_shape=[jax.ShapeDtypeStruct((m, n), F32), jax.ShapeDtypeStruct((m, n), BF16),
                   jax.ShapeDtypeStruct((1, n), F32), jax.ShapeDtypeStruct((1, n), F32)],
        compiler_params=_params(dimension_semantics=("arbitrary",)),
    )(dy, r, g.reshape(1, n))


def _act(h, *, name):
    m = h.shape[0]
    tm = _tile(m, (512, 256, 128))
    tc = D_FF // 2

    def body(g_ref, u_ref, o_ref):
        o_ref[...] = (_silu(g_ref[...]) * u_ref[...]).astype(BF16)

    return pl.pallas_call(
        body, name=name, grid=(m // tm, 2),
        in_specs=[pl.BlockSpec((tm, tc), lambda i, j: (i, j)), pl.BlockSpec((tm, tc), lambda i, j: (i, j + 2))],
        out_specs=pl.BlockSpec((tm, tc), lambda i, j: (i, j)),
        out_shape=jax.ShapeDtypeStruct((m, D_FF), BF16),
        compiler_params=_params(dimension_semantics=("parallel", "parallel")),
    )(h, h)


def _act_bwd(da, h, *, name):
    m = h.shape[0]
    tm = _tile(m, (512, 256, 128))
    tc = D_FF // 2

    def body(da_ref, g_ref, u_ref, o_ref):
        j = pl.program_id(1)

        @pl.when(j < 2)
        def _():
            o_ref[...] = (da_ref[...] * u_ref[...] * _dsilu(g_ref[...])).astype(BF16)

        @pl.when(j >= 2)
        def _():
            o_ref[...] = (da_ref[...] * _silu(g_ref[...])).astype(BF16)

    return pl.pallas_call(
        body, name=name, grid=(m // tm, 4),
        in_specs=[pl.BlockSpec((tm, tc), lambda i, j: (i, j % 2)),
                  pl.BlockSpec((tm, tc), lambda i, j: (i, j % 2)),
                  pl.BlockSpec((tm, tc), lambda i, j: (i, j % 2 + 2))],
        out_specs=pl.BlockSpec((tm, tc), lambda i, j: (i, j)),
        out_shape=jax.ShapeDtypeStruct((m, 2 * D_FF), BF16),
        compiler_params=_params(dimension_semantics=("parallel", "parallel")),
    )(da, h, h)


def _t5_bucket_table():
    r = np.arange(WINDOW)[:, None]
    j = np.arange(2 * WINDOW)[None, :]
    rel = r + WINDOW - j
    n = np.maximum(rel, 0)
    max_exact = NUM_BUCKETS // 2
    nf = np.maximum(n, 1).astype(np.float32)
    large = max_exact + (np.log(nf / np.float32(max_exact)) / np.float32(math.log(MAX_DISTANCE / max_exact))
                         * np.float32(NUM_BUCKETS - max_exact)).astype(np.int32)
    large = np.minimum(large, NUM_BUCKETS - 1)
    bucket = np.where(n < max_exact, n, large)
    in_band = (rel >= 0) & (rel < WINDOW)
    return bucket.astype(np.int32), in_band


def _bucket_onehot():
    bucket, _ = _t5_bucket_table()
    oh = np.zeros((WINDOW * 2 * WINDOW, 128), np.float32)
    oh[np.arange(oh.shape[0]), bucket.reshape(-1)] = 1.0
    return oh


def _attn_scores(q_ref, kp_ref, ko_ref, vp_ref, vo_ref, bias_ref, sink_ref, g, first_block):
    hd = HEAD_DIM_A
    q = q_ref[...]
    qs = jnp.concatenate([q[:, (GROUP_A * g + h) * hd:(GROUP_A * g + h + 1) * hd] for h in range(GROUP_A)], axis=0)
    qs = qs * (hd ** -0.5)
    k2 = jnp.concatenate([kp_ref[:, g * hd:(g + 1) * hd], ko_ref[:, g * hd:(g + 1) * hd]], axis=0)
    v2 = jnp.concatenate([vp_ref[:, g * hd:(g + 1) * hd], vo_ref[:, g * hd:(g + 1) * hd]], axis=0)
    s = _dot(qs, k2, _NT)
    s = s + bias_ref[GROUP_A * g:GROUP_A * (g + 1)].reshape(GROUP_A * WINDOW, 2 * WINDOW)
    rr = lax.broadcasted_iota(jnp.int32, (GROUP_A * WINDOW, 2 * WINDOW), 0) % WINDOW
    jj = lax.broadcasted_iota(jnp.int32, (GROUP_A * WINDOW, 2 * WINDOW), 1)
    rel = rr + WINDOW - jj
    valid = (rel >= 0) & (rel < WINDOW) & (jnp.logical_not(first_block) | (jj >= WINDOW))
    s = jnp.where(valid, s, NEG_INF)
    sk = jnp.concatenate(
        [jnp.broadcast_to(sink_ref[0:1, GROUP_A * g + h:GROUP_A * g + h + 1], (WINDOW, 1)) for h in range(GROUP_A)], axis=0)
    mx = jnp.maximum(jnp.max(s, axis=-1, keepdims=True), sk)
    p = jnp.exp(s - mx)
    ps = jnp.exp(sk - mx)
    den = jnp.sum(p, axis=-1, keepdims=True) + ps
    return qs, k2, v2, p / den, ps / den


def _attn_specs(nb):
    def prev(b, i):
        return (b * nb + jnp.maximum(i - 1, 0))

    q_spec = pl.BlockSpec((WINDOW, Q_A), lambda b, i: (b * nb + i, HC_Q // Q_A))
    kp_spec = pl.BlockSpec((WINDOW, KV_W), lambda b, i: (prev(b, i), HC_K // KV_W))
    ko_spec = pl.BlockSpec((WINDOW, KV_W), lambda b, i: (b * nb + i, HC_K // KV_W))
    vp_spec = pl.BlockSpec((WINDOW, KV_W), lambda b, i: (prev(b, i), HC_V // KV_W))
    vo_spec = pl.BlockSpec((WINDOW, KV_W), lambda b, i: (b * nb + i, HC_V // KV_W))
    bias_spec = pl.BlockSpec((N_HEADS_A, WINDOW, 2 * WINDOW), lambda b, i: (0, 0, 0))
    sink_spec = pl.BlockSpec((1, N_HEADS_A), lambda b, i: (0, 0))
    return [q_spec, kp_spec, ko_spec, vp_spec, vo_spec, bias_spec, sink_spec]


def _attn_fwd(hcat, bias, sink, nbatch, *, name):
    t = hcat.shape[0]
    nb = t // nbatch // WINDOW

    def body(q_ref, kp_ref, ko_ref, vp_ref, vo_ref, bias_ref, sink_ref, o_ref):
        first = pl.program_id(1) == 0
        for g in range(N_KV_A):
            _, _, v2, p, _ = _attn_scores(q_ref, kp_ref, ko_ref, vp_ref, vo_ref, bias_ref, sink_ref, g, first)
            o = _dot(p, v2)
            o_ref[:, g * GROUP_A * HEAD_DIM_A:(g + 1) * GROUP_A * HEAD_DIM_A] = jnp.concatenate(
                [o[h * WINDOW:(h + 1) * WINDOW] for h in range(GROUP_A)], axis=1).astype(BF16)

    return pl.pallas_call(
        body, name=name, grid=(nbatch, nb),
        in_specs=_attn_specs(nb),
        out_specs=pl.BlockSpec((WINDOW, Q_A), lambda b, i: (b * nb + i, 0)),
        out_shape=jax.ShapeDtypeStruct((t, Q_A), BF16),
        compiler_params=_params(dimension_semantics=("parallel", "arbitrary")),
    )(hcat, hcat, hcat, hcat, hcat, bias, sink.reshape(1, N_HEADS_A))


def _attn_bwd(hcat, bias, sink, do, nbatch, *, name):
    t = hcat.shape[0]
    nb = t // nbatch // WINDOW
    hd = HEAD_DIM_A

    def body(q_ref, kp_ref, ko_ref, vp_ref, vo_ref, bias_ref, sink_ref, do_ref,
             dq_ref, dk_ref, dv_ref, dbias_ref, dsink_ref, ck_ref, cv_ref):
        b = pl.program_id(0)
        j = pl.program_id(1)
        first = j == nb - 1

        @pl.when((b == 0) & (j == 0))
        def _():
            dbias_ref[...] = jnp.zeros_like(dbias_ref)
            dsink_ref[...] = jnp.zeros_like(dsink_ref)

        @pl.when(j == 0)
        def _():
            ck_ref[...] = jnp.zeros_like(ck_ref)
            cv_ref[...] = jnp.zeros_like(cv_ref)

        do_ = do_ref[...]
        lane = lax.broadcasted_iota(jnp.int32, (1, N_HEADS_A), 1)
        dsink = jnp.zeros((1, N_HEADS_A), F32)
        dq_parts, dk_own, dv_own, dk_prev, dv_prev = [], [], [], [], []
        for g in range(N_KV_A):
            qs, k2, v2, p, ps = _attn_scores(q_ref, kp_ref, ko_ref, vp_ref, vo_ref, bias_ref, sink_ref, g, first)
            dos = jnp.concatenate([do_[:, (GROUP_A * g + h) * hd:(GROUP_A * g + h + 1) * hd] for h in range(GROUP_A)], axis=0)
            dv2 = _dot(p, dos, _TN)
            dp = _dot(dos, v2, _NT)
            delta = jnp.sum(p * dp, axis=-1, keepdims=True)
            ds = p * (dp - delta)
            dsk = -(ps * delta)
            for h in range(GROUP_A):
                tot = jnp.sum(dsk[h * WINDOW:(h + 1) * WINDOW], axis=0, keepdims=True)
                dsink = dsink + jnp.where(lane == GROUP_A * g + h, tot, 0.0)
            dbias_ref[GROUP_A * g:GROUP_A * (g + 1)] += ds.reshape(GROUP_A, WINDOW, 2 * WINDOW)
            dqs = _dot(ds, k2) * (hd ** -0.5)
            dk2 = _dot(ds, qs, _TN)
            dq_parts.append(jnp.concatenate([dqs[h * WINDOW:(h + 1) * WINDOW] for h in range(GROUP_A)], axis=1))
            dk_prev.append(dk2[:WINDOW])
            dk_own.append(dk2[WINDOW:])
            dv_prev.append(dv2[:WINDOW])
            dv_own.append(dv2[WINDOW:])
        dq_ref[...] = jnp.concatenate(dq_parts, axis=1).astype(BF16)
        dk_ref[...] = (jnp.concatenate(dk_own, axis=1) + ck_ref[...]).astype(BF16)
        dv_ref[...] = (jnp.concatenate(dv_own, axis=1) + cv_ref[...]).astype(BF16)
        ck_ref[...] = jnp.concatenate(dk_prev, axis=1)
        cv_ref[...] = jnp.concatenate(dv_prev, axis=1)
        dsink_ref[...] += dsink

    def rev(spec):
        return pl.BlockSpec(spec.block_shape, lambda b, j, f=spec.index_map: f(b, nb - 1 - j))

    in_specs = [rev(s) for s in _attn_specs(nb)[:5]] + _attn_specs(nb)[5:]
    in_specs.append(pl.BlockSpec((WINDOW, Q_A), lambda b, j: (b * nb + nb - 1 - j, 0)))
    return pl.pallas_call(
        body, name=name, grid=(nbatch, nb),
        in_specs=in_specs,
        out_specs=[pl.BlockSpec((WINDOW, Q_A), lambda b, j: (b * nb + nb - 1 - j, 0)),
                   pl.BlockSpec((WINDOW, KV_W), lambda b, j: (b * nb + nb - 1 - j, 0)),
                   pl.BlockSpec((WINDOW, KV_W), lambda b, j: (b * nb + nb - 1 - j, 0)),
                   pl.BlockSpec((N_HEADS_A, WINDOW, 2 * WINDOW), lambda b, j: (0, 0, 0)),
                   pl.BlockSpec((1, N_HEADS_A), lambda b, j: (0, 0))],
        out_shape=[jax.ShapeDtypeStruct((t, Q_A), BF16), jax.ShapeDtypeStruct((t, KV_W), BF16),
                   jax.ShapeDtypeStruct((t, KV_W), BF16),
                   jax.ShapeDtypeStruct((N_HEADS_A, WINDOW, 2 * WINDOW), F32),
                   jax.ShapeDtypeStruct((1, N_HEADS_A), F32)],
        scratch_shapes=[pltpu.VMEM((WINDOW, KV_W), F32), pltpu.VMEM((WINDOW, KV_W), F32)],
        compiler_params=_params(dimension_semantics=("arbitrary", "arbitrary")),
    )(hcat, hcat, hcat, hcat, hcat, bias, sink.reshape(1, N_HEADS_A), do)


def _shift_down(x, halo8, s):
    if s == 0:
        return x
    rolled = pltpu.roll(x, s, axis=0)
    row8 = lax.broadcasted_iota(jnp.int32, halo8.shape, 0)
    top = jnp.where(row8 < s, pltpu.roll(halo8, s, axis=0), rolled[0:8])
    return jnp.concatenate([top, rolled[8:]], axis=0)


def _shift_up(x, halo8, s):
    if s == 0:
        return x
    n = x.shape[0]
    rolled = pltpu.roll(x, n - s, axis=0)
    row8 = lax.broadcasted_iota(jnp.int32, halo8.shape, 0)
    bottom = jnp.where(row8 >= 8 - s, pltpu.roll(halo8, 8 - s, axis=0), rolled[n - 8:n])
    return jnp.concatenate([rolled[:n - 8], bottom], axis=0)


def _l2n(x, scale):
    r = lax.rsqrt(jnp.sum(x * x, axis=-1, keepdims=True) + NORM_EPS)
    return x * (r * scale)


def _conv_prep(hcat, conv_w, nbatch, *, name):
    t = hcat.shape[0]
    nt = t // nbatch // ROW_T
    cb = HC_CONV // CONV_CH

    def body(u_ref, halo_ref, w_ref, c_ref, q_ref, k_ref, v_ref):
        i = pl.program_id(1)
        u = u_ref[...]
        halo = jnp.where(i == 0, 0.0, halo_ref[...])
        c = jnp.zeros_like(u)
        for j in range(CONV_K):
            c = c + w_ref[j:j + 1, :] * _shift_down(u, halo, CONV_K - 1 - j)
        c_ref[...] = c
        s = _silu(c)
        for h in range(N_HEADS_B):
            lo, hi = h * KEY_DIM_B, (h + 1) * KEY_DIM_B
            q_ref[:, lo:hi] = _l2n(s[:, lo:hi], KEY_DIM_B ** -0.5)
            k_ref[:, lo:hi] = _l2n(s[:, QK_B + lo:QK_B + hi], 1.0)
        v_ref[...] = s[:, 2 * QK_B:]

    row = lambda w: pl.BlockSpec((ROW_T, w), lambda b, i: (b * nt + i, 0))
    return pl.pallas_call(
        body, name=name, grid=(nbatch, nt),
        in_specs=[pl.BlockSpec((ROW_T, CONV_CH), lambda b, i: (b * nt + i, cb)),
                  pl.BlockSpec((8, CONV_CH), lambda b, i: (jnp.maximum((b * nt + i) * (ROW_T // 8) - 1, 0), cb)),
                  pl.BlockSpec((CONV_K, CONV_CH), lambda b, i: (0, 0))],
        out_specs=[row(CONV_CH), row(QK_B), row(QK_B), row(V_B)],
        out_shape=[jax.ShapeDtypeStruct((t, CONV_CH), F32)] + [jax.ShapeDtypeStruct((t, QK_B), F32)] * 3,
        compiler_params=_params(dimension_semantics=("parallel", "parallel")),
    )(hcat, hcat, conv_w)


def _conv_prep_bwd_pointwise(dq, dk, dv, c, *, name):
    t = c.shape[0]

    def l2n_bwd(x, dy, scale):
        r = lax.rsqrt(jnp.sum(x * x, axis=-1, keepdims=True) + NORM_EPS)
        return scale * (r * dy - x * (r * r * r) * jnp.sum(x * dy, axis=-1, keepdims=True))

    def body(dq_ref, dk_ref, dv_ref, c_ref, dc_ref):
        c_ = c_ref[...]
        s = _silu(c_)
        ds = _dsilu(c_)
        for h in range(N_HEADS_B):
            lo, hi = h * KEY_DIM_B, (h + 1) * KEY_DIM_B
            dc_ref[:, lo:hi] = l2n_bwd(s[:, lo:hi], dq_ref[:, lo:hi], KEY_DIM_B ** -0.5) * ds[:, lo:hi]
            dc_ref[:, QK_B + lo:QK_B + hi] = (l2n_bwd(s[:, QK_B + lo:QK_B + hi], dk_ref[:, lo:hi], 1.0)
                                              * ds[:, QK_B + lo:QK_B + hi])
        dc_ref[:, 2 * QK_B:] = dv_ref[...] * ds[:, 2 * QK_B:]

    row = lambda w: pl.BlockSpec((ROW_T, w), lambda i: (i, 0))
    return pl.pallas_call(
        body, name=name, grid=(t // ROW_T,),
        in_specs=[row(QK_B), row(QK_B), row(V_B), row(CONV_CH)], out_specs=row(CONV_CH),
        out_shape=jax.ShapeDtypeStruct((t, CONV_CH), F32),
        compiler_params=_params(dimension_semantics=("parallel",)),
    )(dq, dk, dv, c)


def _conv_bwd(dc, hcat, conv_w, nbatch, *, name):
    t = dc.shape[0]
    nt = t // nbatch // ROW_T
    cb = HC_CONV // CONV_CH
    last_blk = t // 8 - 1

    def body(dc_ref, dnext_ref, u_ref, uprev_ref, w_ref, du_ref, dw_ref):
        b = pl.program_id(0)
        i = pl.program_id(1)
        dc_ = dc_ref[...]
        u = u_ref[...]
        dnext = jnp.where(i == nt - 1, 0.0, dnext_ref[...])
        uprev = jnp.where(i == 0, 0.0, uprev_ref[...])
        du = jnp.zeros_like(dc_)
        rows = []
        for j in range(CONV_K):
            s = CONV_K - 1 - j
            du = du + w_ref[j:j + 1, :] * _shift_up(dc_, dnext, s)
            rows.append(jnp.sum(dc_ * _shift_down(u, uprev, s), axis=0, keepdims=True))
        du_ref[...] = du.astype(BF16)
        dw_p = jnp.concatenate(rows + [jnp.zeros((8 - CONV_K, CONV_CH), F32)], axis=0)

        @pl.when((b == 0) & (i == 0))
        def _():
            dw_ref[...] = dw_p

        @pl.when((b > 0) | (i > 0))
        def _():
            dw_ref[...] += dw_p

    return pl.pallas_call(
        body, name=name, grid=(nbatch, nt),
        in_specs=[pl.BlockSpec((ROW_T, CONV_CH), lambda b, i: (b * nt + i, 0)),
                  pl.BlockSpec((8, CONV_CH), lambda b, i: (jnp.minimum((b * nt + i + 1) * (ROW_T // 8), last_blk), 0)),
                  pl.BlockSpec((ROW_T, CONV_CH), lambda b, i: (b * nt + i, cb)),
                  pl.BlockSpec((8, CONV_CH), lambda b, i: (jnp.maximum((b * nt + i) * (ROW_T // 8) - 1, 0), cb)),
                  pl.BlockSpec((CONV_K, CONV_CH), lambda b, i: (0, 0))],
        out_specs=[pl.BlockSpec((ROW_T, CONV_CH), lambda b, i: (b * nt + i, 0)),
                   pl.BlockSpec((8, CONV_CH), lambda b, i: (0, 0))],
        out_shape=[jax.ShapeDtypeStruct((t, CONV_CH), BF16), jax.ShapeDtypeStruct((8, CONV_CH), F32)],
        compiler_params=_params(dimension_semantics=("arbitrary", "arbitrary")),
    )(dc, dc, hcat, hcat, conv_w)


def _softplus(x):
    return jnp.maximum(x, 0.0) + jnp.log(1.0 + jnp.exp(-jnp.abs(x)))


def _gates(hcat, a_row, dt_row, *, name):
    t = hcat.shape[0]

    def body(bd_ref, a_ref, dt_ref, gb_ref, bb_ref):
        bd = bd_ref[...]
        beta = _sigmoid(bd)
        g = -jnp.exp(a_ref[...]) * _softplus(bd + dt_ref[...])
        for h in range(N_HEADS_B):
            lo, hi = h * VAL_DIM_B, (h + 1) * VAL_DIM_B
            bb_ref[:, lo:hi] = jnp.broadcast_to(beta[:, h:h + 1], (ROW_T, VAL_DIM_B))
            gb_ref[:, lo:hi] = jnp.broadcast_to(g[:, N_HEADS_B + h:N_HEADS_B + h + 1], (ROW_T, VAL_DIM_B))

    vec = pl.BlockSpec((1, 128), lambda i: (0, 0))
    row = pl.BlockSpec((ROW_T, V_B), lambda i: (i, 0))
    return pl.pallas_call(
        body, name=name, grid=(t // ROW_T,),
        in_specs=[pl.BlockSpec((ROW_T, 128), lambda i: (i, HC_BD // 128)), vec, vec],
        out_specs=[row, row], out_shape=[jax.ShapeDtypeStruct((t, V_B), F32)] * 2,
        compiler_params=_params(dimension_semantics=("parallel",)),
    )(hcat, a_row, dt_row)


def _gates_bwd(dgb, dbb, hcat, a_row, dt_row, *, name):
    t = hcat.shape[0]

    def body(dgb_ref, dbb_ref, bd_ref, a_ref, dt_ref, dbd_ref, da_ref, ddt_ref):
        i = pl.program_id(0)
        bd = bd_ref[...]
        beta = _sigmoid(bd)
        ea = jnp.exp(a_ref[...])
        x = bd + dt_ref[...]
        g = -ea * _softplus(x)
        lane = lax.broadcasted_iota(jnp.int32, (ROW_T, 128), 1)
        dbeta = jnp.zeros((ROW_T, 128), F32)
        dg = jnp.zeros((ROW_T, 128), F32)
        for h in range(N_HEADS_B):
            lo, hi = h * VAL_DIM_B, (h + 1) * VAL_DIM_B
            dbeta = dbeta + jnp.where(lane == h, jnp.sum(dbb_ref[:, lo:hi], axis=-1, keepdims=True), 0.0)
            dg = dg + jnp.where(lane == N_HEADS_B + h, jnp.sum(dgb_ref[:, lo:hi], axis=-1, keepdims=True), 0.0)
        ddt_raw = dg * (-ea) * _sigmoid(x)
        dbd_ref[...] = (dbeta * beta * (1.0 - beta) + ddt_raw).astype(BF16)
        da_p = jnp.sum(dg * g, axis=0, keepdims=True)
        ddt_p = jnp.sum(ddt_raw, axis=0, keepdims=True)

        @pl.when(i == 0)
        def _():
            da_ref[...] = da_p
            ddt_ref[...] = ddt_p

        @pl.when(i > 0)
        def _():
            da_ref[...] += da_p
            ddt_ref[...] += ddt_p

    vec = pl.BlockSpec((1, 128), lambda i: (0, 0))
    row = pl.BlockSpec((ROW_T, V_B), lambda i: (i, 0))
    return pl.pallas_call(
        body, name=name, grid=(t // ROW_T,),
        in_specs=[row, row, pl.BlockSpec((ROW_T, 128), lambda i: (i, HC_BD // 128)), vec, vec],
        out_specs=[pl.BlockSpec((ROW_T, 128), lambda i: (i, 0)), vec, vec],
        out_shape=[jax.ShapeDtypeStruct((t, 128), BF16), jax.ShapeDtypeStruct((1, 128), F32),
                   jax.ShapeDtypeStruct((1, 128), F32)],
        compiler_params=_params(dimension_semantics=("arbitrary",)),
    )(dgb, dbb, hcat, a_row, dt_row)


def _group_masks():
    r = lax.broadcasted_iota(jnp.int32, (GROUP_T, GROUP_T), 0)
    c = lax.broadcasted_iota(jnp.int32, (GROUP_T, GROUP_T), 1)
    same = (r // CHUNK) == (c // CHUNK)
    return same, same & (r >= c), same & (r > c)


def _split2(a):
    hi = a.astype(MXU_DTYPE)
    return hi, (a - hi.astype(F32)).astype(MXU_DTYPE)


def _dot3(a2, b2, dims=_NN):
    (ah, al), (bh, bl) = a2, b2
    d = functools.partial(lax.dot_general, dimension_numbers=dims, preferred_element_type=F32)
    return d(ah, bh) + (d(ah, bl) + d(al, bh))


def _inv_unit_lower(low):
    eye = (lax.broadcasted_iota(jnp.int32, low.shape, 0) == lax.broadcasted_iota(jnp.int32, low.shape, 1)).astype(F32)
    p2 = _split2(-low)
    t = eye - low
    for _ in range(int(math.log2(CHUNK)) - 1):
        p2 = _split2(_dot3(p2, p2))
        t = t + _dot3(_split2(t), p2)
    return t


@jax.custom_vjp
def _inv_saved(low, t):
    return t


def _inv_saved_fwd(low, t):
    return t, t


def _inv_saved_bwd(t, dt):
    t2 = _split2(t)
    return -_dot3(t2, _split2(_dot3(_split2(dt), t2, _NT)), _TN), jnp.zeros_like(t)


_inv_saved.defvjp(_inv_saved_fwd, _inv_saved_bwd)


def _fold_blocks(m):
    return m[:, 0:CHUNK] + m[:, CHUNK:2 * CHUNK] + m[:, 2 * CHUNK:3 * CHUNK] + m[:, 3 * CHUNK:4 * CHUNK]


def _dn_prep_head(q, k, v, gb, bb, tsaved=None):
    same, causal, strict = _group_masks()
    gc = _dot(causal.astype(F32), gb, exact=True)
    glast = _dot(same.astype(F32), gb, exact=True)
    gc_row = gc.T[0:1, :]
    decay = jnp.exp(jnp.where(causal, gc[:, 0:1] - gc_row, NEG_INF))
    kb = k * bb
    vb = v * bb
    lower = jnp.where(strict, _dot(kb, k, _NT) * decay, 0.0)
    if tsaved is None:
        tinv = _inv_unit_lower(lower)
    else:
        tinv = _inv_saved(lower, jnp.where(same, jnp.concatenate([tsaved] * (GROUP_T // CHUNK), axis=1), 0.0))
    u = _dot(tinv, vb)
    w = _dot(tinv, kb * jnp.exp(gc))
    a = _fold_blocks(jnp.where(causal, _dot(q, k, _NT) * decay, 0.0))
    k_tail = k * jnp.exp(glast - gc)
    q_dec = q * jnp.exp(gc)
    return u, w, q_dec, k_tail, a, glast, _fold_blocks(tinv)


def _dn_prep(q, k, v, gb, bb, *, name):
    t = q.shape[0]

    def body(q_ref, k_ref, v_ref, gb_ref, bb_ref, u_ref, w_ref, qd_ref, kt_ref, a_ref, gl_ref, ti_ref):
        for h in range(N_HEADS_B):
            sl = slice(h * KEY_DIM_B, (h + 1) * KEY_DIM_B)
            asl = slice(h * CHUNK, (h + 1) * CHUNK)
            u, w, qd, kt, a, gl, ti = _dn_prep_head(q_ref[:, sl], k_ref[:, sl], v_ref[:, sl], gb_ref[:, sl], bb_ref[:, sl])
            u_ref[:, sl] = u
            w_ref[:, sl] = w
            qd_ref[:, sl] = qd
            kt_ref[:, sl] = kt
            gl_ref[:, sl] = gl
            a_ref[:, asl] = a
            ti_ref[:, asl] = ti

    row = pl.BlockSpec((GROUP_T, V_B), lambda i: (i, 0))
    arow = pl.BlockSpec((GROUP_T, N_HEADS_B * CHUNK), lambda i: (i, 0))
    big = jax.ShapeDtypeStruct((t, V_B), F32)
    small = jax.ShapeDtypeStruct((t, N_HEADS_B * CHUNK), F32)
    return pl.pallas_call(
        body, name=name, grid=(t // GROUP_T,),
        in_specs=[row] * 5, out_specs=[row, row, row, row, arow, row, arow],
        out_shape=[big, big, big, big, small, big, small],
        compiler_params=_params(dimension_semantics=("parallel",)),
    )(q, k, v, gb, bb)


def _dn_prep_bwd(q, k, v, gb, bb, ti, du, dw, dqd, dkt, da, dgl, *, name):
    t = q.shape[0]

    def body(q_ref, k_ref, v_ref, gb_ref, bb_ref, ti_ref, du_ref, dw_ref, dqd_ref, dkt_ref, da_ref, dgl_ref,
             dq_ref, dk_ref, dv_ref, dgb_ref, dbb_ref):
        for h in range(N_HEADS_B):
            sl = slice(h * KEY_DIM_B, (h + 1) * KEY_DIM_B)
            asl = slice(h * CHUNK, (h + 1) * CHUNK)
            tsaved = ti_ref[:, asl]
            _, vjp = jax.vjp(lambda *a: _dn_prep_head(*a, tsaved=tsaved)[:6],
                             q_ref[:, sl], k_ref[:, sl], v_ref[:, sl], gb_ref[:, sl], bb_ref[:, sl])
            dq, dk, dv, dgb, dbb = vjp((du_ref[:, sl], dw_ref[:, sl], dqd_ref[:, sl], dkt_ref[:, sl],
                                        da_ref[:, asl], dgl_ref[:, sl]))
            dq_ref[:, sl] = dq
            dk_ref[:, sl] = dk
            dv_ref[:, sl] = dv
            dgb_ref[:, sl] = dgb
            dbb_ref[:, sl] = dbb

    row = pl.BlockSpec((GROUP_T, V_B), lambda i: (i, 0))
    arow = pl.BlockSpec((GROUP_T, N_HEADS_B * CHUNK), lambda i: (i, 0))
    big = jax.ShapeDtypeStruct((t, V_B), F32)
    return pl.pallas_call(
        body, name=name, grid=(t // GROUP_T,),
        in_specs=[row] * 5 + [arow] + [row] * 4 + [arow, row], out_specs=[row] * 5, out_shape=[big] * 5,
        compiler_params=_params(dimension_semantics=("parallel",)),
    )(q, k, v, gb, bb, ti, du, dw, dqd, dkt, da, dgl)


def _dn_step(s, qd, kt, u, w, a, gl):
    v_new = u - _dot(w, s)
    o = _dot(qd, s) + _dot(a, v_new)
    s_new = s * jnp.exp(gl[0:1, :]) + _dot(kt, v_new, _TN)
    return s_new, o


def _dn_scan(u, w, qd, kt, a, gl, nbatch, *, name):
    t = u.shape[0]
    ng = t // nbatch // GROUP_T
    cpg = GROUP_T // CHUNK

    def body(u_ref, w_ref, qd_ref, kt_ref, a_ref, gl_ref, o_ref, ss_ref, s_ref):
        @pl.when(pl.program_id(1) == 0)
        def _():
            s_ref[...] = jnp.zeros_like(s_ref)

        def chunk(c, carry):
            rows = pl.ds(pl.multiple_of(c * CHUNK, CHUNK), CHUNK)
            for h in range(N_HEADS_B):
                sl = slice(h * KEY_DIM_B, (h + 1) * KEY_DIM_B)
                s = s_ref[h]
                ss_ref[c, h] = s
                s_new, o = _dn_step(s, qd_ref[rows, sl], kt_ref[rows, sl], u_ref[rows, sl], w_ref[rows, sl],
                                    a_ref[rows, h * CHUNK:(h + 1) * CHUNK], gl_ref[rows, sl])
                s_ref[h] = s_new
                o_ref[rows, sl] = o
            return carry

        lax.fori_loop(0, cpg, chunk, 0)

    row = pl.BlockSpec((GROUP_T, V_B), lambda b, i: (b * ng + i, 0))
    arow = pl.BlockSpec((GROUP_T, N_HEADS_B * CHUNK), lambda b, i: (b * ng + i, 0))
    return pl.pallas_call(
        body, name=name, grid=(nbatch, ng),
        in_specs=[row, row, row, row, arow, row],
        out_specs=[row, pl.BlockSpec((cpg, N_HEADS_B, KEY_DIM_B, VAL_DIM_B), lambda b, i: (b * ng + i, 0, 0, 0))],
        out_shape=[jax.ShapeDtypeStruct((t, V_B), F32),
                   jax.ShapeDtypeStruct((t // CHUNK, N_HEADS_B, KEY_DIM_B, VAL_DIM_B), F32)],
        scratch_shapes=[pltpu.VMEM((N_HEADS_B, KEY_DIM_B, VAL_DIM_B), F32)],
        compiler_params=_params(dimension_semantics=("parallel", "arbitrary")),
    )(u, w, qd, kt, a, gl)


def _dn_scan_bwd(u, w, qd, kt, a, gl, ss, do, nbatch, *, name):
    t = u.shape[0]
    ng = t // nbatch // GROUP_T
    cpg = GROUP_T // CHUNK

    def body(u_ref, w_ref, qd_ref, kt_ref, a_ref, gl_ref, ss_ref, do_ref,
             du_ref, dw_ref, dqd_ref, dkt_ref, da_ref, dgl_ref, ds_ref):
        @pl.when(pl.program_id(1) == 0)
        def _():
            ds_ref[...] = jnp.zeros_like(ds_ref)

        def chunk(cc, carry):
            c = cpg - 1 - cc
            rows = pl.ds(pl.multiple_of(c * CHUNK, CHUNK), CHUNK)
            for h in range(N_HEADS_B):
                sl = slice(h * KEY_DIM_B, (h + 1) * KEY_DIM_B)
                asl = slice(h * CHUNK, (h + 1) * CHUNK)
                _, vjp = jax.vjp(_dn_step, ss_ref[c, h], qd_ref[rows, sl], kt_ref[rows, sl], u_ref[rows, sl],
                                 w_ref[rows, sl], a_ref[rows, asl], gl_ref[rows, sl])
                ds, dqd, dkt, du, dw, da, dgl = vjp((ds_ref[h], do_ref[rows, sl]))
                ds_ref[h] = ds
                dqd_ref[rows, sl] = dqd
                dkt_ref[rows, sl] = dkt
                du_ref[rows, sl] = du
                dw_ref[rows, sl] = dw
                da_ref[rows, asl] = da
                dgl_ref[rows, sl] = dgl
            return carry

        lax.fori_loop(0, cpg, chunk, 0)

    row = pl.BlockSpec((GROUP_T, V_B), lambda b, j: (b * ng + ng - 1 - j, 0))
    arow = pl.BlockSpec((GROUP_T, N_HEADS_B * CHUNK), lambda b, j: (b * ng + ng - 1 - j, 0))
    big = jax.ShapeDtypeStruct((t, V_B), F32)
    return pl.pallas_call(
        body, name=name, grid=(nbatch, ng),
        in_specs=[row, row, row, row, arow, row,
                  pl.BlockSpec((cpg, N_HEADS_B, KEY_DIM_B, VAL_DIM_B), lambda b, j: (b * ng + ng - 1 - j, 0, 0, 0)), row],
        out_specs=[row, row, row, row, arow, row],
        out_shape=[big, big, big, big, jax.ShapeDtypeStruct((t, N_HEADS_B * CHUNK), F32), big],
        scratch_shapes=[pltpu.VMEM((N_HEADS_B, KEY_DIM_B, VAL_DIM_B), F32)],
        compiler_params=_params(dimension_semantics=("parallel", "arbitrary")),
    )(u, w, qd, kt, a, gl, ss, do)


def _rms_gate(o, hcat, dn_g, *, name):
    t = o.shape[0]

    def body(o_ref, z_ref, g_ref, y_ref):
        for h in range(N_HEADS_B):
            sl = slice(h * VAL_DIM_B, (h + 1) * VAL_DIM_B)
            o_ = o_ref[:, sl]
            r = lax.rsqrt(jnp.mean(o_ * o_, axis=-1, keepdims=True) + NORM_EPS)
            y_ref[:, sl] = (o_ * r * g_ref[...] * _silu(z_ref[:, sl])).astype(BF16)

    row = pl.BlockSpec((ROW_T, V_B), lambda i: (i, 0))
    return pl.pallas_call(
        body, name=name, grid=(t // ROW_T,),
        in_specs=[row, pl.BlockSpec((ROW_T, V_B), lambda i: (i, HC_Z // V_B)), pl.BlockSpec((1, VAL_DIM_B), lambda i: (0, 0))],
        out_specs=row, out_shape=jax.ShapeDtypeStruct((t, V_B), BF16),
        compiler_params=_params(dimension_semantics=("parallel",)),
    )(o, hcat, dn_g.reshape(1, VAL_DIM_B))


def _rms_gate_bwd(dy, o, hcat, dn_g, *, name):
    t = o.shape[0]

    def body(dy_ref, o_ref, z_ref, g_ref, do_ref, dz_ref, dg_ref):
        i = pl.program_id(0)
        g = g_ref[...]
        dg_p = jnp.zeros((1, VAL_DIM_B), F32)
        for h in range(N_HEADS_B):
            sl = slice(h * VAL_DIM_B, (h + 1) * VAL_DIM_B)
            o_ = o_ref[:, sl]
            z_ = z_ref[:, sl]
            dy_ = dy_ref[:, sl]
            r = lax.rsqrt(jnp.mean(o_ * o_, axis=-1, keepdims=True) + NORM_EPS)
            n = o_ * r
            sz = _silu(z_)
            dz_ref[:, sl] = (dy_ * n * g * _dsilu(z_)).astype(BF16)
            dg_p = dg_p + jnp.sum(dy_ * n * sz, axis=0, keepdims=True)
            dn = dy_ * g * sz
            do_ref[:, sl] = r * dn - o_ * (r * r * r) * jnp.mean(o_ * dn, axis=-1, keepdims=True)

        @pl.when(i == 0)
        def _():
            dg_ref[...] = dg_p

        @pl.when(i > 0)
        def _():
            dg_ref[...] += dg_p

    row = pl.BlockSpec((ROW_T, V_B), lambda i: (i, 0))
    vec = pl.BlockSpec((1, VAL_DIM_B), lambda i: (0, 0))
    return pl.pallas_call(
        body, name=name, grid=(t // ROW_T,),
        in_specs=[row, row, pl.BlockSpec((ROW_T, V_B), lambda i: (i, HC_Z // V_B)), vec],
        out_specs=[row, row, vec],
        out_shape=[jax.ShapeDtypeStruct((t, V_B), F32), jax.ShapeDtypeStruct((t, V_B), BF16),
                   jax.ShapeDtypeStruct((1, VAL_DIM_B), F32)],
        compiler_params=_params(dimension_semantics=("arbitrary",)),
    )(dy, o, hcat, dn_g.reshape(1, VAL_DIM_B))


def _merge(ya, yb, hcat, *, name):
    t = ya.shape[0]

    def body(ya_ref, yb_ref, ga_ref, gb_ref, y_ref):
        y_ref[...] = (_sigmoid(ga_ref[...]) * ya_ref[...] + _sigmoid(gb_ref[...]) * yb_ref[...]).astype(BF16)

    row = pl.BlockSpec((ROW_T, D_MODEL), lambda i: (i, 0))
    return pl.pallas_call(
        body, name=name, grid=(t // ROW_T,),
        in_specs=[row, row, pl.BlockSpec((ROW_T, D_MODEL), lambda i: (i, HC_GATE // D_MODEL)),
                  pl.BlockSpec((ROW_T, D_MODEL), lambda i: (i, HC_GATE // D_MODEL + 1))],
        out_specs=row, out_shape=jax.ShapeDtypeStruct((t, D_MODEL), BF16),
        compiler_params=_params(dimension_semantics=("parallel",)),
    )(ya, yb, hcat, hcat)


def _merge_bwd(dmix, ya, yb, hcat, *, name):
    t = ya.shape[0]

    def body(d_ref, ya_ref, yb_ref, ga_ref, gb_ref, dya_ref, dyb_ref, dgate_ref):
        d = d_ref[...]
        sa = _sigmoid(ga_ref[...])
        sb = _sigmoid(gb_ref[...])
        dya_ref[...] = (d * sa).astype(BF16)
        dyb_ref[...] = (d * sb).astype(BF16)
        dgate_ref[:, :D_MODEL] = (d * ya_ref[...] * sa * (1.0 - sa)).astype(BF16)
        dgate_ref[:, D_MODEL:] = (d * yb_ref[...] * sb * (1.0 - sb)).astype(BF16)

    row = pl.BlockSpec((ROW_T, D_MODEL), lambda i: (i, 0))
    return pl.pallas_call(
        body, name=name, grid=(t // ROW_T,),
        in_specs=[row, row, row, pl.BlockSpec((ROW_T, D_MODEL), lambda i: (i, HC_GATE // D_MODEL)),
                  pl.BlockSpec((ROW_T, D_MODEL), lambda i: (i, HC_GATE // D_MODEL + 1))],
        out_specs=[row, row, pl.BlockSpec((ROW_T, 2 * D_MODEL), lambda i: (i, 0))],
        out_shape=[jax.ShapeDtypeStruct((t, D_MODEL), BF16)] * 2 + [jax.ShapeDtypeStruct((t, 2 * D_MODEL), BF16)],
        compiler_params=_params(dimension_semantics=("parallel",)),
    )(dmix, ya, yb, hcat, hcat)


def _loss_head(y, target, *, name):
    t, n = y.shape
    tm = _tile(t, (512, 256, 128))

    def body(y_ref, t_ref, part_ref, dy_ref):
        i = pl.program_id(0)
        e = y_ref[...] - t_ref[...]
        dy_ref[...] = e * (1.0 / n)
        p = jnp.sum((e * e).reshape(tm // 8, 8, n), axis=0) * (0.5 / n)

        @pl.when(i == 0)
        def _():
            part_ref[...] = p

        @pl.when(i > 0)
        def _():
            part_ref[...] += p

    row = pl.BlockSpec((tm, n), lambda i: (i, 0))
    return pl.pallas_call(
        body, name=name, grid=(t // tm,),
        in_specs=[row, row], out_specs=[pl.BlockSpec((8, n), lambda i: (0, 0)), row],
        out_shape=[jax.ShapeDtypeStruct((8, n), F32), jax.ShapeDtypeStruct((t, n), F32)],
        compiler_params=_params(dimension_semantics=("arbitrary",)),
    )(y, target)


def _adamw(w, g, m, v, *, name):
    shape = w.shape
    cols = shape[-1]
    rows = int(np.prod(shape[:-1]))
    w2, g2, m2, v2 = (a.reshape(rows, cols) for a in (w, g, m, v))
    tr = rows
    if rows * cols > 512 * 1024:
        tr = _tile(rows, tuple(c for c in (512, 256, 128, 64, 32, 16, 8) if c * cols <= 256 * 1024))

    def body(w_ref, g_ref, m_ref, v_ref, d_ref, nm_ref, nv_ref):
        g_ = g_ref[...]
        nm = ADAM_B1 * m_ref[...] + (1.0 - ADAM_B1) * g_
        nv = ADAM_B2 * v_ref[...] + (1.0 - ADAM_B2) * (g_ * g_)
        m_hat = nm / (1.0 - ADAM_B1 ** ADAM_STEP)
        v_hat = nv / (1.0 - ADAM_B2 ** ADAM_STEP)
        d_ref[...] = -ADAM_LR * (m_hat / (jnp.sqrt(v_hat) + ADAM_EPS) + ADAM_WD * w_ref[...])
        nm_ref[...] = nm
        nv_ref[...] = nv

    blk = pl.BlockSpec((tr, cols), lambda i: (i, 0))
    outs = pl.pallas_call(
        body, name=name, grid=(rows // tr,),
        in_specs=[blk] * 4, out_specs=[blk] * 3,
        out_shape=[jax.ShapeDtypeStruct((rows, cols), F32)] * 3,
        compiler_params=_params(dimension_semantics=("parallel",)),
    )(w2, g2, m2, v2)
    return tuple(o.reshape(shape) for o in outs)


def _repack_w_in(w_in):
    d = w_in.shape[0]
    o = 0
    parts = {}
    for nm, wd in (("q", Q_A), ("k", KV_W), ("v", KV_W), ("conv", CONV_CH), ("beta", N_HEADS_B), ("dt", N_HEADS_B),
                   ("z", V_B), ("gate", 2 * D_MODEL)):
        parts[nm] = w_in[:, o:o + wd]
        o += wd
    z = lambda n: jnp.zeros((d, n), w_in.dtype)
    return jnp.concatenate([parts["q"], parts["z"], parts["k"], parts["v"], parts["beta"], parts["dt"],
                            z(128 - 2 * N_HEADS_B), z(HC_CONV - HC_BD - 128), parts["conv"], parts["gate"]], axis=1)


MATRIX_NAMES = ("ffn_w13", "ffn_w2", "w_in", "w_branch_a", "w_branch_b", "w_out")


def _dw_in_by_owner(dw):
    sections = ((Q_A, HC_Q), (2 * KV_W, HC_K), (CONV_CH, HC_CONV), (2 * N_HEADS_B, HC_BD), (V_B, HC_Z),
                (2 * D_MODEL, HC_GATE))
    per = N_IN // 4
    owners = []
    for o in range(4):
        lo, hi, start, parts = o * per, (o + 1) * per, 0, []
        for width, off in sections:
            a, b = max(lo, start), min(hi, start + width)
            if a < b:
                parts.append(dw[:, off + a - start:off + b - start])
            start += width
        owners.append(jnp.concatenate(parts, axis=1))
    return jnp.stack(owners)


def _lane_row(vals):
    return jnp.pad(vals.astype(F32).reshape(1, N_HEADS_B), ((0, 0), (N_HEADS_B, 128 - 2 * N_HEADS_B)))


def _local_step(x, target, wts):
    nbatch, seq, d = x.shape
    t = nbatch * seq
    depth = wts["w_in"].shape[0]
    x0 = x.reshape(t, d)
    tgt = target.reshape(t, d)

    onehot = jnp.asarray(_bucket_onehot())
    rel_t = jnp.pad(wts["rel_bias"].T, ((0, 0), (0, 128 - NUM_BUCKETS)))
    bias = _mm(rel_t, onehot, tb=True, exact=True, name="pos_bias")
    bias = bias.reshape(N_HEADS_A, WINDOW, 2 * WINDOW)

    saved = []
    xin = x0
    for i in range(depth):
        L = {}
        tag = f"_l{i}"
        L["x0"] = xin
        h = _mm(xin, wts["ffn_w13"][i, 0], name="ffn_up" + tag + "a")
        a = _act(h, name="ffn_act" + tag + "a")
        r1, x1 = _mm_res_ln(a, wts["ffn_w2"][i, 0], xin, wts["ln_g"][i, 0], wts["ln_b"][i, 0],
                            alpha=DN_ALPHA, c=0.5, name="ffn_down_ln" + tag + "a")
        L.update(h0=h, a0=a, r1=r1, x1=x1)
        w_in_p = wts["w_in_p"][i]
        hcat = _mm(x1, w_in_p, name="in_proj" + tag)
        ao = _attn_fwd(hcat, bias, wts["sinks"][i], nbatch, name="swa" + tag)
        ya = _mm(ao, wts["w_branch_a"][i], name="branch_a" + tag)
        c, qn, kn, vs = _conv_prep(hcat, wts["conv_w"][i], nbatch, name="conv_prep" + tag)
        a_row = _lane_row(wts["a_log"][i])
        dt_row = _lane_row(wts["dt_bias"][i])
        gb, bb = _gates(hcat, a_row, dt_row, name="gates" + tag)
        u, w, qd, kt, aa, gl, ti = _dn_prep(qn, kn, vs, gb, bb, name="dn_prep" + tag)
        o, ss = _dn_scan(u, w, qd, kt, aa, gl, nbatch, name="dn_scan" + tag)
        on = _rms_gate(o, hcat, wts["dn_norm_g"][i], name="rms_gate" + tag)
        yb = _mm(on, wts["w_branch_b"][i], name="branch_b" + tag)
        mix = _merge(ya, yb, hcat, name="merge" + tag)
        r2, x2 = _mm_res_ln(mix, wts["w_out"][i], x1, wts["ln_g"][i, 1], wts["ln_b"][i, 1],
                            alpha=DN_ALPHA, c=1.0, name="out_proj_ln" + tag)
        L.update(hcat=hcat, ao=ao, ya=ya, c=c, qn=qn, kn=kn, vs=vs, gb=gb, bb=bb, a_row=a_row, dt_row=dt_row,
                 u=u, w=w, qd=qd, kt=kt, aa=aa, gl=gl, ti=ti, o=o, ss=ss, on=on, yb=yb, mix=mix, r2=r2, x2=x2)
        h = _mm(x2, wts["ffn_w13"][i, 1], name="ffn_up" + tag + "b")
        a = _act(h, name="ffn_act" + tag + "b")
        r3, x3 = _mm_res_ln(a, wts["ffn_w2"][i, 1], x2, wts["ln_g"][i, 2], wts["ln_b"][i, 2],
                            alpha=DN_ALPHA, c=0.5, name="ffn_down_ln" + tag + "b")
        L.update(h1=h, a1=a, r3=r3)
        saved.append(L)
        xin = x3

    part, dy = _loss_head(xin, tgt, name="loss_head")
    loss = jnp.sum(part)

    grads = {k: [None] * depth for k in ("ln_g", "ln_b", "ffn_w13", "ffn_w2", "w_in", "conv_w", "a_log", "dt_bias",
                                          "dn_norm_g", "sinks", "w_branch_a", "w_branch_b", "w_out")}
    dbias_total = None
    for i in reversed(range(depth)):
        L = saved[i]
        tag = f"_l{i}"
        dln_g, dln_b, dw13, dw2 = [None] * 3, [None] * 3, [None] * 2, [None] * 2

        def ffn_bwd(dyo, r, xprev, hsave, asave, j, sfx):
            dres, df, dln_g[2 * j], dln_b[2 * j] = _ln_bwd(dyo, r, wts["ln_g"][i, 2 * j], alpha=DN_ALPHA, c=0.5,
                                                           name="ln_bwd" + tag + sfx)
            da = _mm(df, wts["ffn_w2"][i, j], tb=True, name="ffn_down_bwd" + tag + sfx)
            dw2[j] = _mm(asave, df, ta=True, name="ffn_w2_grad" + tag + sfx)
            dh = _act_bwd(da, hsave, name="ffn_act_bwd" + tag + sfx)
            dw13[j] = _mm(xprev, dh, ta=True, name="ffn_w13_grad" + tag + sfx)
            return _mm(dh, wts["ffn_w13"][i, j], tb=True, add=dres, name="ffn_up_bwd" + tag + sfx)

        dx2 = ffn_bwd(dy, L["r3"], L["x2"], L["h1"], L["a1"], 1, "b")

        dres2, dymix, dln_g[1], dln_b[1] = _ln_bwd(dx2, L["r2"], wts["ln_g"][i, 1], alpha=DN_ALPHA, c=1.0,
                                                   name="ln_bwd" + tag + "m")
        hcat = L["hcat"]
        dmix = _mm(dymix, wts["w_out"][i], tb=True, name="out_proj_bwd" + tag)
        grads["w_out"][i] = _mm(L["mix"], dymix, ta=True, name="w_out_grad" + tag)
        dya, dyb, dgate = _merge_bwd(dmix, L["ya"], L["yb"], hcat, name="merge_bwd" + tag)
        dao = _mm(dya, wts["w_branch_a"][i], tb=True, name="branch_a_bwd" + tag)
        grads["w_branch_a"][i] = _mm(L["ao"], dya, ta=True, name="w_branch_a_grad" + tag)
        don = _mm(dyb, wts["w_branch_b"][i], tb=True, name="branch_b_bwd" + tag)
        grads["w_branch_b"][i] = _mm(L["on"], dyb, ta=True, name="w_branch_b_grad" + tag)
        do, dz, ddn = _rms_gate_bwd(don, L["o"], hcat, wts["dn_norm_g"][i], name="rms_gate_bwd" + tag)
        grads["dn_norm_g"][i] = ddn.reshape(VAL_DIM_B)
        du, dw, dqd, dkt, daa, dgl = _dn_scan_bwd(L["u"], L["w"], L["qd"], L["kt"], L["aa"], L["gl"], L["ss"], do,
                                                  nbatch, name="dn_scan_bwd" + tag)
        dqn, dkn, dvs, dgb, dbb = _dn_prep_bwd(L["qn"], L["kn"], L["vs"], L["gb"], L["bb"], L["ti"], du, dw, dqd, dkt,
                                               daa, dgl, name="dn_prep_bwd" + tag)
        dc = _conv_prep_bwd_pointwise(dqn, dkn, dvs, L["c"], name="conv_prep_bwd" + tag)
        dconv, dconv_w = _conv_bwd(dc, hcat, wts["conv_w"][i], nbatch, name="conv_bwd" + tag)
        grads["conv_w"][i] = dconv_w[:CONV_K]
        dbd, da_log, ddt = _gates_bwd(dgb, dbb, hcat, L["a_row"], L["dt_row"], name="gates_bwd" + tag)
        grads["a_log"][i] = da_log[0, N_HEADS_B:2 * N_HEADS_B]
        grads["dt_bias"][i] = ddt[0, N_HEADS_B:2 * N_HEADS_B]
        dq, dk, dv, dbias, dsink = _attn_bwd(hcat, bias, wts["sinks"][i], dao, nbatch, name="swa_bwd" + tag)
        grads["sinks"][i] = dsink.reshape(N_HEADS_A)
        dbias_total = dbias if dbias_total is None else dbias_total + dbias
        dhcat = jnp.concatenate([dq, dz, dk, dv, dbd, jnp.zeros((t, HC_CONV - HC_BD - 128), BF16), dconv, dgate], axis=1)
        dw_in_p = _mm(L["x1"], dhcat, ta=True, name="w_in_grad" + tag)
        grads["w_in"][i] = _dw_in_by_owner(dw_in_p)
        dx1 = _mm(dhcat, wts["w_in_p"][i], tb=True, add=dres2, name="in_proj_bwd" + tag)

        dy = ffn_bwd(dx1, L["r1"], L["x0"], L["h0"], L["a0"], 0, "a")
        grads["ln_g"][i] = jnp.concatenate(dln_g, axis=0)
        grads["ln_b"][i] = jnp.concatenate(dln_b, axis=0)
        grads["ffn_w13"][i] = dw13
        grads["ffn_w2"][i] = dw2

    out = {k: (v if k in MATRIX_NAMES else jnp.stack(v)) for k, v in grads.items()}
    drel = _mm(dbias_total.reshape(N_HEADS_A, WINDOW * 2 * WINDOW), onehot, name="rel_bias_grad")
    out["rel_bias"] = drel[:, :NUM_BUCKETS].T
    return loss, dy.reshape(nbatch, seq, d), out


N_CHIPS = 4
MESH_ID = pl.DeviceIdType.MESH
HBM_SPEC = pl.BlockSpec(memory_space=pltpu.HBM)


def _place():
    x, y, c = lax.axis_index("x"), lax.axis_index("y"), lax.axis_index("c")
    others = [(1 - x, y), (x, 1 - y), (1 - x, 1 - y)]
    return x, y, c, others


def _chip_index(cx, cy):
    return 2 * cx + cy


def _allgather_chips(tensors, *, name):
    n = len(tensors)

    def body(*refs):
        ins, outs = refs[:n], refs[n:2 * n]
        send_sems, recv_sems, local_sems = refs[2 * n:]
        x, y, c, others = _place()
        me = _chip_index(x, y)
        copies = []
        for i in range(n):
            loc = pltpu.make_async_copy(ins[i], outs[i].at[me], local_sems.at[i])
            loc.start()
            copies.append(loc)
            for k, (ox, oy) in enumerate(others):
                cp = pltpu.make_async_remote_copy(src_ref=ins[i], dst_ref=outs[i].at[me], send_sem=send_sems.at[i, k],
                                                  recv_sem=recv_sems.at[i, k], device_id=(ox, oy, c), device_id_type=MESH_ID)
                cp.start()
                copies.append(cp)
        for cp in copies:
            cp.wait()

    return pl.pallas_call(
        body, name=name,
        in_specs=[HBM_SPEC] * n, out_specs=[HBM_SPEC] * n,
        out_shape=[jax.ShapeDtypeStruct((N_CHIPS,) + t.shape, t.dtype) for t in tensors],
        scratch_shapes=[pltpu.SemaphoreType.DMA((n, 3)), pltpu.SemaphoreType.DMA((n, 3)), pltpu.SemaphoreType.DMA((n,))],
    )(*tensors)


def _allgather_devices(v, *, name):
    def body(v_ref, o_ref, send_sems, recv_sems, local_sem):
        x, y, c, _ = _place()
        me = 4 * x + 2 * y + c
        loc = pltpu.make_async_copy(v_ref, o_ref.at[me], local_sem)
        loc.start()
        copies = [loc]
        for k in range(1, 8):
            px, py, pc = x ^ (k >> 2), y ^ ((k >> 1) & 1), c ^ (k & 1)
            cp = pltpu.make_async_remote_copy(src_ref=v_ref, dst_ref=o_ref.at[me], send_sem=send_sems.at[k - 1],
                                              recv_sem=recv_sems.at[k - 1], device_id=(px, py, pc), device_id_type=MESH_ID)
            cp.start()
            copies.append(cp)
        for cp in copies:
            cp.wait()

    return pl.pallas_call(
        body, name=name, in_specs=[HBM_SPEC], out_specs=HBM_SPEC,
        out_shape=jax.ShapeDtypeStruct((8,) + v.shape, v.dtype),
        scratch_shapes=[pltpu.SemaphoreType.DMA((7,)), pltpu.SemaphoreType.DMA((7,)), pltpu.SemaphoreType.DMA],
    )(v)


def _sum_slots(g, *, name):
    n, r, l = g.shape
    tr = r // 2 if r % 32 == 0 else r

    def body(g_ref, o_ref):
        acc = g_ref[0].astype(F32)
        for k in range(1, n):
            acc = acc + g_ref[k].astype(F32)
        o_ref[...] = acc

    return pl.pallas_call(
        body, name=name, grid=(r // tr,),
        in_specs=[pl.BlockSpec((n, tr, l), lambda i: (0, i, 0))], out_specs=pl.BlockSpec((tr, l), lambda i: (i, 0)),
        out_shape=jax.ShapeDtypeStruct((r, l), F32),
        compiler_params=_params(dimension_semantics=("parallel",)),
    )(g)


def _half_window(ref, kind, h):
    if kind == "rows":
        return ref.at[:, h]
    r = ref.shape[0] // 2
    return ref.at[pl.ds(pl.multiple_of(h * r, r), r), :]


def _owner_window(ref, kind, o):
    if kind == "rows":
        return ref.at[o]
    cols = ref.shape[1] // N_CHIPS
    return ref.at[:, pl.ds(pl.multiple_of(o * cols, cols), cols)]


def _half_shape(g, kind):
    return (g.shape[0],) + g.shape[2:] if kind == "rows" else (g.shape[0] // 2, g.shape[1])


def _swap_halves(gs, kinds, *, name):
    n = len(gs)

    def body(*refs):
        ins, outs, send_sems, recv_sems = refs[:n], refs[n:2 * n], refs[2 * n], refs[2 * n + 1]
        x, y, c, _ = _place()
        copies = []
        for i in range(n):
            cp = pltpu.make_async_remote_copy(src_ref=_half_window(ins[i], kinds[i], 1 - c), dst_ref=outs[i],
                                              send_sem=send_sems.at[i], recv_sem=recv_sems.at[i],
                                              device_id=(x, y, 1 - c), device_id_type=MESH_ID)
            cp.start()
            copies.append(cp)
        for cp in copies:
            cp.wait()

    return pl.pallas_call(
        body, name=name, in_specs=[HBM_SPEC] * n, out_specs=[HBM_SPEC] * n,
        out_shape=[jax.ShapeDtypeStruct(_half_shape(g, k), g.dtype) for g, k in zip(gs, kinds)],
        scratch_shapes=[pltpu.SemaphoreType.DMA((n,)), pltpu.SemaphoreType.DMA((n,))],
    )(*gs)


def _pair_sum(g, got, kind, c_idx, *, name):
    hs = _half_shape(g, kind)

    def body(c_ref, g_ref, r_ref, o_ref):
        o_ref[...] = (g_ref[...] + r_ref[...]).astype(BF16)

    if kind == "rows":
        _, _, r, cols = g.shape
        grid = (N_CHIPS,)
        in_specs = [pl.BlockSpec((None, None, r, cols), lambda o, c_ref: (o, c_ref[0], 0, 0)),
                    pl.BlockSpec((None, r, cols), lambda o, c_ref: (o, 0, 0))]
        out_spec = pl.BlockSpec((None, r, cols), lambda o, c_ref: (o, 0, 0))
    else:
        r, cols = hs
        steps = 4
        tr = r // steps
        grid = (steps,)
        in_specs = [pl.BlockSpec((tr, cols), lambda i, c_ref: (c_ref[0] * steps + i, 0)),
                    pl.BlockSpec((tr, cols), lambda i, c_ref: (i, 0))]
        out_spec = pl.BlockSpec((tr, cols), lambda i, c_ref: (i, 0))
    return pl.pallas_call(
        body, name=name,
        grid_spec=pltpu.PrefetchScalarGridSpec(num_scalar_prefetch=1, grid=grid, in_specs=in_specs, out_specs=out_spec),
        out_shape=jax.ShapeDtypeStruct(hs, BF16),
        compiler_params=_params(dimension_semantics=("parallel",)),
    )(c_idx, g, got)


def _exchange_chips(ss, kinds, *, name):
    n = len(ss)

    def shard_shape(s, kind):
        return s.shape[1:] if kind == "rows" else (s.shape[0], s.shape[1] // N_CHIPS)

    def body(*refs):
        ins, outs = refs[:n], refs[n:2 * n]
        send_sems, recv_sems, local_sems = refs[2 * n:]
        x, y, c, others = _place()
        me = _chip_index(x, y)
        copies = []
        for i in range(n):
            loc = pltpu.make_async_copy(_owner_window(ins[i], kinds[i], me), outs[i].at[me], local_sems.at[i])
            loc.start()
            copies.append(loc)
            for k, (ox, oy) in enumerate(others):
                cp = pltpu.make_async_remote_copy(src_ref=_owner_window(ins[i], kinds[i], _chip_index(ox, oy)),
                                                  dst_ref=outs[i].at[me], send_sem=send_sems.at[i, k],
                                                  recv_sem=recv_sems.at[i, k], device_id=(ox, oy, c), device_id_type=MESH_ID)
                cp.start()
                copies.append(cp)
        for cp in copies:
            cp.wait()

    return pl.pallas_call(
        body, name=name, in_specs=[HBM_SPEC] * n, out_specs=[HBM_SPEC] * n,
        out_shape=[jax.ShapeDtypeStruct((N_CHIPS,) + shard_shape(s, k), s.dtype) for s, k in zip(ss, kinds)],
        scratch_shapes=[pltpu.SemaphoreType.DMA((n, 3)), pltpu.SemaphoreType.DMA((n, 3)), pltpu.SemaphoreType.DMA((n,))],
    )(*ss)


def _join_halves(fs, places, out_shapes, *, name):
    n = len(fs)
    no = len(out_shapes)

    def body(*refs):
        ins, outs = refs[:n], refs[n:n + no]
        send_sems, recv_sems, local_sems = refs[n + no:]
        x, y, c, _ = _place()
        copies = []
        for i in range(n):
            oi, lead = places[i]
            r = fs[i].shape[0]
            dst = outs[oi].at[(*lead, pl.ds(pl.multiple_of(c * r, r), r))]
            loc = pltpu.make_async_copy(ins[i], dst, local_sems.at[i])
            loc.start()
            cp = pltpu.make_async_remote_copy(src_ref=ins[i], dst_ref=dst, send_sem=send_sems.at[i], recv_sem=recv_sems.at[i],
                                              device_id=(x, y, 1 - c), device_id_type=MESH_ID)
            cp.start()
            copies += [loc, cp]
        for cp in copies:
            cp.wait()

    return pl.pallas_call(
        body, name=name, in_specs=[HBM_SPEC] * n, out_specs=[HBM_SPEC] * no,
        out_shape=[jax.ShapeDtypeStruct(s, F32) for s in out_shapes],
        scratch_shapes=[pltpu.SemaphoreType.DMA((n,)), pltpu.SemaphoreType.DMA((n,)), pltpu.SemaphoreType.DMA((n,))],
    )(*fs)


SHARD_AXIS = {"rel_bias": None, "ln_g": 2, "ln_b": 2, "ffn_w13": 3, "ffn_w2": 2, "w_in": 2, "conv_w": 2, "a_log": None,
              "dt_bias": None, "dn_norm_g": None, "sinks": None, "w_branch_a": 1, "w_branch_b": 1, "w_out": 1}
WEIGHT_NAMES = tuple(SHARD_AXIS)
SMALL_NAMES = tuple(n for n in WEIGHT_NAMES if n not in MATRIX_NAMES)
PACK_LANES = 1024


def _unshard(gathered, axis):
    g = jnp.moveaxis(gathered, 0, axis)
    return g.reshape(g.shape[:axis] + (g.shape[axis] * g.shape[axis + 1],) + g.shape[axis + 2:])


def _reduce_matrices(grads, shard_shapes):
    gs, kinds, places = [], [], []
    names = [n for n in MATRIX_NAMES]
    for oi, n in enumerate(names):
        for i, g in enumerate(grads[n]):
            if n == "ffn_w13":
                for j, gj in enumerate(g):
                    gs.append(gj)
                    kinds.append("cols")
                    places.append((oi, (i, j)))
            elif n == "ffn_w2":
                for j, gj in enumerate(g):
                    gs.append(gj.reshape(N_CHIPS, 2, gj.shape[0] // (2 * N_CHIPS), gj.shape[1]))
                    kinds.append("rows")
                    places.append((oi, (i, j)))
            elif n == "w_in":
                gs.append(g.reshape(N_CHIPS, 2, g.shape[1] // 2, g.shape[2]))
                kinds.append("rows")
                places.append((oi, (i,)))
            else:
                gs.append(g.reshape(N_CHIPS, 2, g.shape[0] // (2 * N_CHIPS), g.shape[1]))
                kinds.append("rows")
                places.append((oi, (i,)))
    c_idx = lax.axis_index("c").astype(jnp.int32).reshape(1)
    got = _swap_halves(gs, kinds, name="rs_swap_halves")
    ss = [_pair_sum(g, r, k, c_idx, name=f"rs_pair_sum_{i}") for i, (g, r, k) in enumerate(zip(gs, got, kinds))]
    ex = _exchange_chips(ss, kinds, name="rs_exchange_chips")
    fs = [_sum_slots(e, name=f"rs_chip_sum_{i}") for i, e in enumerate(ex)]
    outs = _join_halves(fs, places, [shard_shapes[n] for n in names], name="rs_join_halves")
    return dict(zip(names, outs))


def _reduce_small(grads):
    flat = [grads[n].astype(F32).reshape(-1) for n in SMALL_NAMES]
    total = sum(f.shape[0] for f in flat)
    rows = -(-total // (16 * PACK_LANES)) * 16
    vec = jnp.concatenate(flat + [jnp.zeros((rows * PACK_LANES - total,), F32)]).reshape(rows, PACK_LANES)
    s = _sum_slots(_allgather_devices(vec, name="small_allgather"), name="small_sum").reshape(-1)
    out, o = {}, 0
    for n, f in zip(SMALL_NAMES, flat):
        out[n] = s[o:o + f.shape[0]].reshape(grads[n].shape)
        o += f.shape[0]
    return out


def kernel(x, rel_bias, ln_g, ln_b, ffn_w13, ffn_w2, w_in, conv_w, a_log, dt_bias, dn_norm_g, sinks, w_branch_a, w_branch_b, w_out, loss_target, m_rel_bias, m_ln_g, m_ln_b, m_ffn_w13, m_ffn_w2, m_w_in, m_conv_w, m_a_log, m_dt_bias, m_dn_norm_g, m_sinks, m_w_branch_a, m_w_branch_b, m_w_out, v_rel_bias, v_ln_g, v_ln_b, v_ffn_w13, v_ffn_w2, v_w_in, v_conv_w, v_a_log, v_dt_bias, v_dn_norm_g, v_sinks, v_w_branch_a, v_w_branch_b, v_w_out):
    w = dict(rel_bias=rel_bias, ln_g=ln_g, ln_b=ln_b, ffn_w13=ffn_w13, ffn_w2=ffn_w2, w_in=w_in, conv_w=conv_w,
             a_log=a_log, dt_bias=dt_bias, dn_norm_g=dn_norm_g, sinks=sinks, w_branch_a=w_branch_a,
             w_branch_b=w_branch_b, w_out=w_out)
    m = dict(rel_bias=m_rel_bias, ln_g=m_ln_g, ln_b=m_ln_b, ffn_w13=m_ffn_w13, ffn_w2=m_ffn_w2, w_in=m_w_in,
             conv_w=m_conv_w, a_log=m_a_log, dt_bias=m_dt_bias, dn_norm_g=m_dn_norm_g, sinks=m_sinks,
             w_branch_a=m_w_branch_a, w_branch_b=m_w_branch_b, w_out=m_w_out)
    v = dict(rel_bias=v_rel_bias, ln_g=v_ln_g, ln_b=v_ln_b, ffn_w13=v_ffn_w13, ffn_w2=v_ffn_w2, w_in=v_w_in,
             conv_w=v_conv_w, a_log=v_a_log, dt_bias=v_dt_bias, dn_norm_g=v_dn_norm_g, sinks=v_sinks,
             w_branch_a=v_w_branch_a, w_branch_b=v_w_branch_b, w_out=v_w_out)

    sharded = [n for n in WEIGHT_NAMES if SHARD_AXIS[n] is not None]
    gathered = _allgather_chips([w[n].astype(MXU_DTYPE) if n in MATRIX_NAMES else w[n] for n in sharded],
                                name="weights_allgather")
    wts = {n: w[n] for n in WEIGHT_NAMES if SHARD_AXIS[n] is None}
    for n, g in zip(sharded, gathered):
        wts[n] = _unshard(g, SHARD_AXIS[n])
    wts["w_in_p"] = jax.vmap(_repack_w_in)(wts["w_in"])

    loss_part, grad_x, grads = _local_step(x, loss_target, wts)
    loss = lax.psum(loss_part, ("x", "y", "c"))

    red = _reduce_matrices(grads, {n: w[n].shape for n in MATRIX_NAMES})
    chip = _chip_index(lax.axis_index("x"), lax.axis_index("y"))
    for n, g in _reduce_small(grads).items():
        axis = SHARD_AXIS[n]
        red[n] = g if axis is None else lax.dynamic_slice_in_dim(g, chip * w[n].shape[axis], w[n].shape[axis], axis)
    g_out, d_out, m_out, v_out = [], [], [], []
    for n in WEIGHT_NAMES:
        g = red[n]
        d, nm, nv = _adamw(w[n], g, m[n], v[n], name="adamw_" + n)
        g_out.append(g)
        d_out.append(d)
        m_out.append(nm)
        v_out.append(nv)
    return (loss, grad_x, *g_out, *d_out, *m_out, *v_out)
```

```python
import functools
import math

import numpy as np
import jax
import jax.numpy as jnp
from jax import lax
from jax.experimental import pallas as pl
from jax.experimental.pallas import tpu as pltpu

F32 = jnp.float32
BF16 = jnp.bfloat16
MXU_DTYPE = BF16
HIGHEST = lax.Precision.HIGHEST

D_MODEL = 1024
N_HEADS_A = 16
N_KV_A = 4
HEAD_DIM_A = 64
GROUP_A = N_HEADS_A // N_KV_A
WINDOW = 128
N_HEADS_B = 8
KEY_DIM_B = 128
VAL_DIM_B = 128
CONV_K = 4
CHUNK = 64
D_FF = 2816
NUM_BUCKETS = 32
MAX_DISTANCE = 128
DEPTH = 4
DN_ALPHA = (2 * DEPTH) ** 0.25
LN_EPS = 1e-5
NORM_EPS = 1e-6
NEG_INF = -1e30

Q_A = N_HEADS_A * HEAD_DIM_A
KV_W = N_KV_A * HEAD_DIM_A
QK_B = N_HEADS_B * KEY_DIM_B
V_B = N_HEADS_B * VAL_DIM_B
CONV_CH = 2 * QK_B + V_B
N_IN = Q_A + 2 * KV_W + CONV_CH + 2 * N_HEADS_B + V_B + 2 * D_MODEL

ADAM_LR = 0.001
ADAM_B1 = 0.9
ADAM_B2 = 0.999
ADAM_EPS = 1e-08
ADAM_WD = 0.01
ADAM_STEP = 10

HC_W = 8192
HC_Q = 0
HC_Z = 1024
HC_K = 2048
HC_V = 2304
HC_BD = 2560
HC_CONV = 3072
HC_GATE = 6144

GROUP_T = 256
ROW_T = 256
VMEM_LIMIT_BYTES = 48 * 1024 * 1024


def _params(**kw):
    return pltpu.CompilerParams(vmem_limit_bytes=VMEM_LIMIT_BYTES, **kw)


def _tile(n, cands):
    for c in cands:
        if n % c == 0:
            return c
    return n


def _dot(a, b, dims=(((1,), (0,)), ((), ())), exact=False):
    if exact:
        return lax.dot_general(a.astype(F32), b.astype(F32), dims, precision=HIGHEST, preferred_element_type=F32)
    return lax.dot_general(a.astype(MXU_DTYPE), b.astype(MXU_DTYPE), dims, preferred_element_type=F32)


_NN = (((1,), (0,)), ((), ()))
_NT = (((1,), (1,)), ((), ()))
_TN = (((0,), (0,)), ((), ()))


def _sigmoid(x):
    return 1.0 / (1.0 + jnp.exp(-x))


def _silu(x):
    return x * _sigmoid(x)


def _dsilu(x):
    s = _sigmoid(x)
    return s * (1.0 + x * (1.0 - s))


def _mm(a, b, *, ta=False, tb=False, a_halves=False, b_halves=False, out_dtype=F32, add=None, exact=False, name):
    if a_halves:
        m, kdim = a.shape[1], 2 * a.shape[2]
    else:
        (kdim, m) = a.shape if ta else a.shape[::-1]
    if b_halves:
        kb, n = b.shape[1], 2 * b.shape[2]
    else:
        (n, kb) = b.shape if tb else b.shape[::-1]
    assert kdim == kb, (a.shape, b.shape, ta, tb)
    tm = _tile(m, (1024, 512, 256, 128) if ta else (512, 256, 128))
    tn = _tile(n, (1024, 1408, 512, 256, 128))
    tk = _tile(kdim, (1024, 1408, 512, 256, 128))
    nk = kdim // tk
    nj = n // tn
    dims = (((0 if ta else 1,), (1 if tb else 0,)), ((), ()))
    has_add = add is not None

    def body(*refs):
        if has_add:
            a_ref, b_ref, add_ref, o_ref = refs[:4]
        else:
            a_ref, b_ref, o_ref = refs[:3]
            add_ref = None
        part = _dot(a_ref[...], b_ref[...], dims, exact)

        def finish(acc):
            if has_add:
                acc = acc + add_ref[...].astype(F32)
            o_ref[...] = acc.astype(out_dtype)

        if nk == 1:
            finish(part)
        else:
            acc_ref = refs[-1]
            k = pl.program_id(2)

            @pl.when(k == 0)
            def _():
                acc_ref[...] = part

            @pl.when(k > 0)
            def _():
                acc_ref[...] += part

            @pl.when(k == nk - 1)
            def _():
                finish(acc_ref[...])

    if a_halves:
        assert not ta and nk % 2 == 0
        a_spec = pl.BlockSpec((None, tm, tk), lambda i, j, k: (k // (nk // 2), i, k % (nk // 2)))
    elif ta:
        a_spec = pl.BlockSpec((tk, tm), lambda i, j, k: (k, i))
    else:
        a_spec = pl.BlockSpec((tm, tk), lambda i, j, k: (i, k))
    if b_halves:
        assert not tb and nj % 2 == 0
        b_spec = pl.BlockSpec((None, tk, tn), lambda i, j, k: (j // (nj // 2), k, j % (nj // 2)))
    elif tb:
        b_spec = pl.BlockSpec((tn, tk), lambda i, j, k: (j, k))
    else:
        b_spec = pl.BlockSpec((tk, tn), lambda i, j, k: (k, j))
    o_spec = pl.BlockSpec((tm, tn), lambda i, j, k: (i, j))
    in_specs = [a_spec, b_spec] + ([o_spec] if has_add else [])
    args = (a, b) + ((add,) if has_add else ())
    return pl.pallas_call(
        body, name=name, grid=(m // tm, nj, nk),
        in_specs=in_specs, out_specs=o_spec,
        out_shape=jax.ShapeDtypeStruct((m, n), out_dtype),
        scratch_shapes=[pltpu.VMEM((tm, tn), F32)] if nk > 1 else [],
        compiler_params=_params(dimension_semantics=("parallel", "parallel", "arbitrary")),
    )(*args)


def _layernorm_rows(r, g, b):
    mu = jnp.mean(r, axis=-1, keepdims=True)
    xc = r - mu
    var = jnp.mean(xc * xc, axis=-1, keepdims=True)
    return xc * lax.rsqrt(var + LN_EPS) * g + b


def _mm_res_ln(a, w, resid, g, b, *, alpha, c, name):
    m, kdim = a.shape
    n = w.shape[1]
    tm = _tile(m, (512, 256, 128))

    def body(a_ref, w_ref, x_ref, g_ref, b_ref, r_ref, y_ref, yb_ref):
        f = _dot(a_ref[...], w_ref[...])
        r = alpha * x_ref[...] + c * f
        r_ref[...] = r
        y = _layernorm_rows(r, g_ref[...], b_ref[...])
        y_ref[...] = y
        yb_ref[...] = y.astype(BF16)

    row = pl.BlockSpec((tm, n), lambda i: (i, 0))
    vec = pl.BlockSpec((1, n), lambda i: (0, 0))
    return pl.pallas_call(
        body, name=name, grid=(m // tm,),
        in_specs=[pl.BlockSpec((tm, kdim), lambda i: (i, 0)), pl.BlockSpec((kdim, n), lambda i: (0, 0)), row, vec, vec],
        out_specs=[row, row, row],
        out_shape=[jax.ShapeDtypeStruct((m, n), F32)] * 2 + [jax.ShapeDtypeStruct((m, n), BF16)],
        compiler_params=_params(dimension_semantics=("parallel",)),
    )(a, w, resid, g.reshape(1, n), b.reshape(1, n))


def _ln_bwd(dy, r, g, *, alpha, c, name):
    m, n = dy.shape
    tm = _tile(m, (512, 256, 128))

    def body(dy_ref, r_ref, g_ref, dres_ref, dbr_ref, dg_ref, db_ref):
        i = pl.program_id(0)
        dy_ = dy_ref[...]
        r_ = r_ref[...]
        mu = jnp.mean(r_, axis=-1, keepdims=True)
        xc = r_ - mu
        var = jnp.mean(xc * xc, axis=-1, keepdims=True)
        rstd = lax.rsqrt(var + LN_EPS)
        xh = xc * rstd
        dxh = dy_ * g_ref[...]
        dr = rstd * (dxh - jnp.mean(dxh, axis=-1, keepdims=True) - xh * jnp.mean(dxh * xh, axis=-1, keepdims=True))
        dres_ref[...] = alpha * dr
        dbr_ref[...] = (c * dr).astype(BF16)
        dg_p = jnp.sum(dy_ * xh, axis=0, keepdims=True)
        db_p = jnp.sum(dy_, axis=0, keepdims=True)

        @pl.when(i == 0)
        def _():
            dg_ref[...] = dg_p
            db_ref[...] = db_p

        @pl.when(i > 0)
        def _():
            dg_ref[...] += dg_p
            db_ref[...] += db_p

    row = pl.BlockSpec((tm, n), lambda i: (i, 0))
    vec = pl.BlockSpec((1, n), lambda i: (0, 0))
    return pl.pallas_call(
        body, name=name, grid=(m // tm,),
        in_specs=[row, row, vec], out_specs=[row, row, vec, vec],
        out_shape=[jax.ShapeDtypeStruct((m, n), F32), jax.ShapeDtypeStruct((m, n), BF16),
                   jax.ShapeDtypeStruct((1, n), F32), jax.ShapeDtypeStruct((1, n), F32)],
        compiler_params=_params(dimension_semantics=("arbitrary",)),
    )(dy, r, g.reshape(1, n))


FFN_TN = D_FF // 2


def _ffn_up_act(x, w13, *, name):
    m, d = x.shape
    tm = _tile(m, (512, 256, 128))
    nj = D_FF // FFN_TN

    def body(x_ref, g_ref, u_ref, o_ref):
        x_ = x_ref[...]
        o_ref[...] = (_silu(_dot(x_, g_ref[...])) * _dot(x_, u_ref[...])).astype(BF16)

    return pl.pallas_call(
        body, name=name, grid=(nj, m // tm),
        in_specs=[pl.BlockSpec((tm, d), lambda j, i: (i, 0)), pl.BlockSpec((d, FFN_TN), lambda j, i: (0, j)),
                  pl.BlockSpec((d, FFN_TN), lambda j, i: (0, j + nj))],
        out_specs=pl.BlockSpec((tm, FFN_TN), lambda j, i: (i, j)),
        out_shape=jax.ShapeDtypeStruct((m, D_FF), BF16),
        compiler_params=_params(dimension_semantics=("parallel", "parallel")),
    )(x, w13, w13)


def _ffn_bwd_mid(x, w13, df, w2, *, name):
    m, d = x.shape
    tm = _tile(m, (512, 256, 128))
    nj = D_FF // FFN_TN

    def body(x_ref, g_ref, u_ref, df_ref, w2_ref, o_ref):
        x_ = x_ref[...]
        g = _dot(x_, g_ref[...])
        u = _dot(x_, u_ref[...])
        da = _dot(df_ref[...], w2_ref[...], _NT)
        o_ref[0] = (da * u * _dsilu(g)).astype(BF16)
        o_ref[1] = (da * _silu(g)).astype(BF16)

    return pl.pallas_call(
        body, name=name, grid=(nj, m // tm),
        in_specs=[pl.BlockSpec((tm, d), lambda j, i: (i, 0)), pl.BlockSpec((d, FFN_TN), lambda j, i: (0, j)),
                  pl.BlockSpec((d, FFN_TN), lambda j, i: (0, j + nj)), pl.BlockSpec((tm, d), lambda j, i: (i, 0)),
                  pl.BlockSpec((FFN_TN, d), lambda j, i: (j, 0))],
        out_specs=pl.BlockSpec((2, tm, FFN_TN), lambda j, i: (0, i, j)),
        out_shape=jax.ShapeDtypeStruct((2, m, D_FF), BF16),
        compiler_params=_params(dimension_semantics=("parallel", "parallel")),
    )(x, w13, w13, df, w2)


def _t5_bucket_table():
    r = np.arange(WINDOW)[:, None]
    j = np.arange(2 * WINDOW)[None, :]
    rel = r + WINDOW - j
    n = np.maximum(rel, 0)
    max_exact = NUM_BUCKETS // 2
    nf = np.maximum(n, 1).astype(np.float32)
    large = max_exact + (np.log(nf / np.float32(max_exact)) / np.float32(math.log(MAX_DISTANCE / max_exact))
                         * np.float32(NUM_BUCKETS - max_exact)).astype(np.int32)
    large = np.minimum(large, NUM_BUCKETS - 1)
    bucket = np.where(n < max_exact, n, large)
    in_band = (rel >= 0) & (rel < WINDOW)
    return bucket.astype(np.int32), in_band


def _bucket_onehot():
    bucket, _ = _t5_bucket_table()
    oh = np.zeros((WINDOW * 2 * WINDOW, 128), np.float32)
    oh[np.arange(oh.shape[0]), bucket.reshape(-1)] = 1.0
    return oh


def _attn_scores(q_ref, kp_ref, ko_ref, vp_ref, vo_ref, bias_ref, sink_ref, g, first_block):
    hd = HEAD_DIM_A
    q = q_ref[...]
    qs = jnp.concatenate([q[:, (GROUP_A * g + h) * hd:(GROUP_A * g + h + 1) * hd] for h in range(GROUP_A)], axis=0)
    qs = qs * (hd ** -0.5)
    k2 = jnp.concatenate([kp_ref[:, g * hd:(g + 1) * hd], ko_ref[:, g * hd:(g + 1) * hd]], axis=0)
    v2 = jnp.concatenate([vp_ref[:, g * hd:(g + 1) * hd], vo_ref[:, g * hd:(g + 1) * hd]], axis=0)
    s = _dot(qs, k2, _NT)
    s = s + bias_ref[GROUP_A * g:GROUP_A * (g + 1)].reshape(GROUP_A * WINDOW, 2 * WINDOW)
    rr = lax.broadcasted_iota(jnp.int32, (GROUP_A * WINDOW, 2 * WINDOW), 0) % WINDOW
    jj = lax.broadcasted_iota(jnp.int32, (GROUP_A * WINDOW, 2 * WINDOW), 1)
    rel = rr + WINDOW - jj
    valid = (rel >= 0) & (rel < WINDOW) & (jnp.logical_not(first_block) | (jj >= WINDOW))
    s = jnp.where(valid, s, NEG_INF)
    sk = jnp.concatenate(
        [jnp.broadcast_to(sink_ref[0:1, GROUP_A * g + h:GROUP_A * g + h + 1], (WINDOW, 1)) for h in range(GROUP_A)], axis=0)
    mx = jnp.maximum(jnp.max(s, axis=-1, keepdims=True), sk)
    p = jnp.exp(s - mx)
    ps = jnp.exp(sk - mx)
    den = jnp.sum(p, axis=-1, keepdims=True) + ps
    return qs, k2, v2, p / den, ps / den


def _attn_specs(nb):
    def prev(b, i):
        return (b * nb + jnp.maximum(i - 1, 0))

    q_spec = pl.BlockSpec((WINDOW, Q_A), lambda b, i: (b * nb + i, HC_Q // Q_A))
    kp_spec = pl.BlockSpec((WINDOW, KV_W), lambda b, i: (prev(b, i), HC_K // KV_W))
    ko_spec = pl.BlockSpec((WINDOW, KV_W), lambda b, i: (b * nb + i, HC_K // KV_W))
    vp_spec = pl.BlockSpec((WINDOW, KV_W), lambda b, i: (prev(b, i), HC_V // KV_W))
    vo_spec = pl.BlockSpec((WINDOW, KV_W), lambda b, i: (b * nb + i, HC_V // KV_W))
    bias_spec = pl.BlockSpec((N_HEADS_A, WINDOW, 2 * WINDOW), lambda b, i: (0, 0, 0))
    sink_spec = pl.BlockSpec((1, N_HEADS_A), lambda b, i: (0, 0))
    return [q_spec, kp_spec, ko_spec, vp_spec, vo_spec, bias_spec, sink_spec]


def _attn_fwd(hcat, bias, sink, nbatch, *, name):
    t = hcat.shape[0]
    nb = t // nbatch // WINDOW

    def body(q_ref, kp_ref, ko_ref, vp_ref, vo_ref, bias_ref, sink_ref, o_ref):
        first = pl.program_id(1) == 0
        for g in range(N_KV_A):
            _, _, v2, p, _ = _attn_scores(q_ref, kp_ref, ko_ref, vp_ref, vo_ref, bias_ref, sink_ref, g, first)
            o = _dot(p, v2)
            o_ref[:, g * GROUP_A * HEAD_DIM_A:(g + 1) * GROUP_A * HEAD_DIM_A] = jnp.concatenate(
                [o[h * WINDOW:(h + 1) * WINDOW] for h in range(GROUP_A)], axis=1).astype(BF16)

    return pl.pallas_call(
        body, name=name, grid=(nbatch, nb),
        in_specs=_attn_specs(nb),
        out_specs=pl.BlockSpec((WINDOW, Q_A), lambda b, i: (b * nb + i, 0)),
        out_shape=jax.ShapeDtypeStruct((t, Q_A), BF16),
        compiler_params=_params(dimension_semantics=("parallel", "arbitrary")),
    )(hcat, hcat, hcat, hcat, hcat, bias, sink.reshape(1, N_HEADS_A))


def _attn_bwd(hcat, bias, sink, do, nbatch, *, name):
    t = hcat.shape[0]
    nb = t // nbatch // WINDOW
    hd = HEAD_DIM_A

    def body(q_ref, kp_ref, ko_ref, vp_ref, vo_ref, bias_ref, sink_ref, do_ref,
             dq_ref, dk_ref, dv_ref, dbias_ref, dsink_ref, ck_ref, cv_ref):
        b = pl.program_id(0)
        j = pl.program_id(1)
        first = j == nb - 1

        @pl.when((b == 0) & (j == 0))
        def _():
            dbias_ref[...] = jnp.zeros_like(dbias_ref)
            dsink_ref[...] = jnp.zeros_like(dsink_ref)

        @pl.when(j == 0)
        def _():
            ck_ref[...] = jnp.zeros_like(ck_ref)
            cv_ref[...] = jnp.zeros_like(cv_ref)

        do_ = do_ref[...]
        lane = lax.broadcasted_iota(jnp.int32, (1, N_HEADS_A), 1)
        dsink = jnp.zeros((1, N_HEADS_A), F32)
        dq_parts, dk_own, dv_own, dk_prev, dv_prev = [], [], [], [], []
        for g in range(N_KV_A):
            qs, k2, v2, p, ps = _attn_scores(q_ref, kp_ref, ko_ref, vp_ref, vo_ref, bias_ref, sink_ref, g, first)
            dos = jnp.concatenate([do_[:, (GROUP_A * g + h) * hd:(GROUP_A * g + h + 1) * hd] for h in range(GROUP_A)], axis=0)
            dv2 = _dot(p, dos, _TN)
            dp = _dot(dos, v2, _NT)
            delta = jnp.sum(p * dp, axis=-1, keepdims=True)
            ds = p * (dp - delta)
            dsk = -(ps * delta)
            for h in range(GROUP_A):
                tot = jnp.sum(dsk[h * WINDOW:(h + 1) * WINDOW], axis=0, keepdims=True)
                dsink = dsink + jnp.where(lane == GROUP_A * g + h, tot, 0.0)
            dbias_ref[GROUP_A * g:GROUP_A * (g + 1)] += ds.reshape(GROUP_A, WINDOW, 2 * WINDOW)
            dqs = _dot(ds, k2) * (hd ** -0.5)
            dk2 = _dot(ds, qs, _TN)
            dq_parts.append(jnp.concatenate([dqs[h * WINDOW:(h + 1) * WINDOW] for h in range(GROUP_A)], axis=1))
            dk_prev.append(dk2[:WINDOW])
            dk_own.append(dk2[WINDOW:])
            dv_prev.append(dv2[:WINDOW])
            dv_own.append(dv2[WINDOW:])
        dq_ref[...] = jnp.concatenate(dq_parts, axis=1).astype(BF16)
        dk_ref[...] = (jnp.concatenate(dk_own, axis=1) + ck_ref[...]).astype(BF16)
        dv_ref[...] = (jnp.concatenate(dv_own, axis=1) + cv_ref[...]).astype(BF16)
        ck_ref[...] = jnp.concatenate(dk_prev, axis=1)
        cv_ref[...] = jnp.concatenate(dv_prev, axis=1)
        dsink_ref[...] += dsink

    def rev(spec):
        return pl.BlockSpec(spec.block_shape, lambda b, j, f=spec.index_map: f(b, nb - 1 - j))

    in_specs = [rev(s) for s in _attn_specs(nb)[:5]] + _attn_specs(nb)[5:]
    in_specs.append(pl.BlockSpec((WINDOW, Q_A), lambda b, j: (b * nb + nb - 1 - j, 0)))
    return pl.pallas_call(
        body, name=name, grid=(nbatch, nb),
        in_specs=in_specs,
        out_specs=[pl.BlockSpec((WINDOW, Q_A), lambda b, j: (b * nb + nb - 1 - j, 0)),
                   pl.BlockSpec((WINDOW, KV_W), lambda b, j: (b * nb + nb - 1 - j, 0)),
                   pl.BlockSpec((WINDOW, KV_W), lambda b, j: (b * nb + nb - 1 - j, 0)),
                   pl.BlockSpec((N_HEADS_A, WINDOW, 2 * WINDOW), lambda b, j: (0, 0, 0)),
                   pl.BlockSpec((1, N_HEADS_A), lambda b, j: (0, 0))],
        out_shape=[jax.ShapeDtypeStruct((t, Q_A), BF16), jax.ShapeDtypeStruct((t, KV_W), BF16),
                   jax.ShapeDtypeStruct((t, KV_W), BF16),
                   jax.ShapeDtypeStruct((N_HEADS_A, WINDOW, 2 * WINDOW), F32),
                   jax.ShapeDtypeStruct((1, N_HEADS_A), F32)],
        scratch_shapes=[pltpu.VMEM((WINDOW, KV_W), F32), pltpu.VMEM((WINDOW, KV_W), F32)],
        compiler_params=_params(dimension_semantics=("arbitrary", "arbitrary")),
    )(hcat, hcat, hcat, hcat, hcat, bias, sink.reshape(1, N_HEADS_A), do)


def _shift_down(x, halo8, s):
    if s == 0:
        return x
    rolled = pltpu.roll(x, s, axis=0)
    row8 = lax.broadcasted_iota(jnp.int32, halo8.shape, 0)
    top = jnp.where(row8 < s, pltpu.roll(halo8, s, axis=0), rolled[0:8])
    return jnp.concatenate([top, rolled[8:]], axis=0)


def _shift_up(x, halo8, s):
    if s == 0:
        return x
    n = x.shape[0]
    rolled = pltpu.roll(x, n - s, axis=0)
    row8 = lax.broadcasted_iota(jnp.int32, halo8.shape, 0)
    bottom = jnp.where(row8 >= 8 - s, pltpu.roll(halo8, 8 - s, axis=0), rolled[n - 8:n])
    return jnp.concatenate([rolled[:n - 8], bottom], axis=0)


def _l2n(x, scale):
    r = lax.rsqrt(jnp.sum(x * x, axis=-1, keepdims=True) + NORM_EPS)
    return x * (r * scale)


def _conv_prep(hcat, conv_w, nbatch, *, name):
    t = hcat.shape[0]
    nt = t // nbatch // ROW_T
    cb = HC_CONV // CONV_CH

    def body(u_ref, halo_ref, w_ref, c_ref, q_ref, k_ref, v_ref):
        i = pl.program_id(1)
        u = u_ref[...]
        halo = jnp.where(i == 0, 0.0, halo_ref[...])
        c = jnp.zeros_like(u)
        for j in range(CONV_K):
            c = c + w_ref[j:j + 1, :] * _shift_down(u, halo, CONV_K - 1 - j)
        c_ref[...] = c
        s = _silu(c)
        for h in range(N_HEADS_B):
            lo, hi = h * KEY_DIM_B, (h + 1) * KEY_DIM_B
            q_ref[:, lo:hi] = _l2n(s[:, lo:hi], KEY_DIM_B ** -0.5)
            k_ref[:, lo:hi] = _l2n(s[:, QK_B + lo:QK_B + hi], 1.0)
        v_ref[...] = s[:, 2 * QK_B:]

    row = lambda w: pl.BlockSpec((ROW_T, w), lambda b, i: (b * nt + i, 0))
    return pl.pallas_call(
        body, name=name, grid=(nbatch, nt),
        in_specs=[pl.BlockSpec((ROW_T, CONV_CH), lambda b, i: (b * nt + i, cb)),
                  pl.BlockSpec((8, CONV_CH), lambda b, i: (jnp.maximum((b * nt + i) * (ROW_T // 8) - 1, 0), cb)),
                  pl.BlockSpec((CONV_K, CONV_CH), lambda b, i: (0, 0))],
        out_specs=[row(CONV_CH), row(QK_B), row(QK_B), row(V_B)],
        out_shape=[jax.ShapeDtypeStruct((t, CONV_CH), F32)] + [jax.ShapeDtypeStruct((t, QK_B), F32)] * 3,
        compiler_params=_params(dimension_semantics=("parallel", "parallel")),
    )(hcat, hcat, conv_w)


def _conv_prep_bwd_pointwise(dq, dk, dv, c, *, name):
    t = c.shape[0]

    def l2n_bwd(x, dy, scale):
        r = lax.rsqrt(jnp.sum(x * x, axis=-1, keepdims=True) + NORM_EPS)
        return scale * (r * dy - x * (r * r * r) * jnp.sum(x * dy, axis=-1, keepdims=True))

    def body(dq_ref, dk_ref, dv_ref, c_ref, dc_ref):
        c_ = c_ref[...]
        s = _silu(c_)
        ds = _dsilu(c_)
        for h in range(N_HEADS_B):
            lo, hi = h * KEY_DIM_B, (h + 1) * KEY_DIM_B
            dc_ref[:, lo:hi] = l2n_bwd(s[:, lo:hi], dq_ref[:, lo:hi], KEY_DIM_B ** -0.5) * ds[:, lo:hi]
            dc_ref[:, QK_B + lo:QK_B + hi] = (l2n_bwd(s[:, QK_B + lo:QK_B + hi], dk_ref[:, lo:hi], 1.0)
                                              * ds[:, QK_B + lo:QK_B + hi])
        dc_ref[:, 2 * QK_B:] = dv_ref[...] * ds[:, 2 * QK_B:]

    row = lambda w: pl.BlockSpec((ROW_T, w), lambda i: (i, 0))
    return pl.pallas_call(
        body, name=name, grid=(t // ROW_T,),
        in_specs=[row(QK_B), row(QK_B), row(V_B), row(CONV_CH)], out_specs=row(CONV_CH),
        out_shape=jax.ShapeDtypeStruct((t, CONV_CH), F32),
        compiler_params=_params(dimension_semantics=("parallel",)),
    )(dq, dk, dv, c)


def _conv_bwd(dc, hcat, conv_w, nbatch, *, name):
    t = dc.shape[0]
    nt = t // nbatch // ROW_T
    cb = HC_CONV // CONV_CH
    last_blk = t // 8 - 1

    def body(dc_ref, dnext_ref, u_ref, uprev_ref, w_ref, du_ref, dw_ref):
        b = pl.program_id(0)
        i = pl.program_id(1)
        dc_ = dc_ref[...]
        u = u_ref[...]
        dnext = jnp.where(i == nt - 1, 0.0, dnext_ref[...])
        uprev = jnp.where(i == 0, 0.0, uprev_ref[...])
        du = jnp.zeros_like(dc_)
        rows = []
        for j in range(CONV_K):
            s = CONV_K - 1 - j
            du = du + w_ref[j:j + 1, :] * _shift_up(dc_, dnext, s)
            rows.append(jnp.sum(dc_ * _shift_down(u, uprev, s), axis=0, keepdims=True))
        du_ref[...] = du.astype(BF16)
        dw_p = jnp.concatenate(rows + [jnp.zeros((8 - CONV_K, CONV_CH), F32)], axis=0)

        @pl.when((b == 0) & (i == 0))
        def _():
            dw_ref[...] = dw_p

        @pl.when((b > 0) | (i > 0))
        def _():
            dw_ref[...] += dw_p

    return pl.pallas_call(
        body, name=name, grid=(nbatch, nt),
        in_specs=[pl.BlockSpec((ROW_T, CONV_CH), lambda b, i: (b * nt + i, 0)),
                  pl.BlockSpec((8, CONV_CH), lambda b, i: (jnp.minimum((b * nt + i + 1) * (ROW_T // 8), last_blk), 0)),
                  pl.BlockSpec((ROW_T, CONV_CH), lambda b, i: (b * nt + i, cb)),
                  pl.BlockSpec((8, CONV_CH), lambda b, i: (jnp.maximum((b * nt + i) * (ROW_T // 8) - 1, 0), cb)),
                  pl.BlockSpec((CONV_K, CONV_CH), lambda b, i: (0, 0))],
        out_specs=[pl.BlockSpec((ROW_T, CONV_CH), lambda b, i: (b * nt + i, 0)),
                   pl.BlockSpec((8, CONV_CH), lambda b, i: (0, 0))],
        out_shape=[jax.ShapeDtypeStruct((t, CONV_CH), BF16), jax.ShapeDtypeStruct((8, CONV_CH), F32)],
        compiler_params=_params(dimension_semantics=("arbitrary", "arbitrary")),
    )(dc, dc, hcat, hcat, conv_w)


def _softplus(x):
    return jnp.maximum(x, 0.0) + jnp.log(1.0 + jnp.exp(-jnp.abs(x)))


def _gates(hcat, a_row, dt_row, *, name):
    t = hcat.shape[0]

    def body(bd_ref, a_ref, dt_ref, gb_ref, bb_ref):
        bd = bd_ref[...]
        beta = _sigmoid(bd)
        g = -jnp.exp(a_ref[...]) * _softplus(bd + dt_ref[...])
        for h in range(N_HEADS_B):
            lo, hi = h * VAL_DIM_B, (h + 1) * VAL_DIM_B
            bb_ref[:, lo:hi] = jnp.broadcast_to(beta[:, h:h + 1], (ROW_T, VAL_DIM_B))
            gb_ref[:, lo:hi] = jnp.broadcast_to(g[:, N_HEADS_B + h:N_HEADS_B + h + 1], (ROW_T, VAL_DIM_B))

    vec = pl.BlockSpec((1, 128), lambda i: (0, 0))
    row = pl.BlockSpec((ROW_T, V_B), lambda i: (i, 0))
    return pl.pallas_call(
        body, name=name, grid=(t // ROW_T,),
        in_specs=[pl.BlockSpec((ROW_T, 128), lambda i: (i, HC_BD // 128)), vec, vec],
        out_specs=[row, row], out_shape=[jax.ShapeDtypeStruct((t, V_B), F32)] * 2,
        compiler_params=_params(dimension_semantics=("parallel",)),
    )(hcat, a_row, dt_row)


def _gates_bwd(dgb, dbb, hcat, a_row, dt_row, *, name):
    t = hcat.shape[0]

    def body(dgb_ref, dbb_ref, bd_ref, a_ref, dt_ref, dbd_ref, da_ref, ddt_ref):
        i = pl.program_id(0)
        bd = bd_ref[...]
        beta = _sigmoid(bd)
        ea = jnp.exp(a_ref[...])
        x = bd + dt_ref[...]
        g = -ea * _softplus(x)
        lane = lax.broadcasted_iota(jnp.int32, (ROW_T, 128), 1)
        dbeta = jnp.zeros((ROW_T, 128), F32)
        dg = jnp.zeros((ROW_T, 128), F32)
        for h in range(N_HEADS_B):
            lo, hi = h * VAL_DIM_B, (h + 1) * VAL_DIM_B
            dbeta = dbeta + jnp.where(lane == h, jnp.sum(dbb_ref[:, lo:hi], axis=-1, keepdims=True), 0.0)
            dg = dg + jnp.where(lane == N_HEADS_B + h, jnp.sum(dgb_ref[:, lo:hi], axis=-1, keepdims=True), 0.0)
        ddt_raw = dg * (-ea) * _sigmoid(x)
        dbd_ref[...] = (dbeta * beta * (1.0 - beta) + ddt_raw).astype(BF16)
        da_p = jnp.sum(dg * g, axis=0, keepdims=True)
        ddt_p = jnp.sum(ddt_raw, axis=0, keepdims=True)

        @pl.when(i == 0)
        def _():
            da_ref[...] = da_p
            ddt_ref[...] = ddt_p

        @pl.when(i > 0)
        def _():
            da_ref[...] += da_p
            ddt_ref[...] += ddt_p

    vec = pl.BlockSpec((1, 128), lambda i: (0, 0))
    row = pl.BlockSpec((ROW_T, V_B), lambda i: (i, 0))
    return pl.pallas_call(
        body, name=name, grid=(t // ROW_T,),
        in_specs=[row, row, pl.BlockSpec((ROW_T, 128), lambda i: (i, HC_BD // 128)), vec, vec],
        out_specs=[pl.BlockSpec((ROW_T, 128), lambda i: (i, 0)), vec, vec],
        out_shape=[jax.ShapeDtypeStruct((t, 128), BF16), jax.ShapeDtypeStruct((1, 128), F32),
                   jax.ShapeDtypeStruct((1, 128), F32)],
        compiler_params=_params(dimension_semantics=("arbitrary",)),
    )(dgb, dbb, hcat, a_row, dt_row)


def _group_masks():
    r = lax.broadcasted_iota(jnp.int32, (GROUP_T, GROUP_T), 0)
    c = lax.broadcasted_iota(jnp.int32, (GROUP_T, GROUP_T), 1)
    same = (r // CHUNK) == (c // CHUNK)
    return same, same & (r >= c), same & (r > c)


def _split2(a):
    hi = a.astype(MXU_DTYPE)
    return hi, (a - hi.astype(F32)).astype(MXU_DTYPE)


def _dot3(a2, b2, dims=_NN):
    (ah, al), (bh, bl) = a2, b2
    d = functools.partial(lax.dot_general, dimension_numbers=dims, preferred_element_type=F32)
    return d(ah, bh) + (d(ah, bl) + d(al, bh))


def _inv_unit_lower(low):
    eye = (lax.broadcasted_iota(jnp.int32, low.shape, 0) == lax.broadcasted_iota(jnp.int32, low.shape, 1)).astype(F32)
    p2 = _split2(-low)
    t = eye - low
    for _ in range(int(math.log2(CHUNK)) - 1):
        p2 = _split2(_dot3(p2, p2))
        t = t + _dot3(_split2(t), p2)
    return t


@jax.custom_vjp
def _inv_saved(low, t):
    return t


def _inv_saved_fwd(low, t):
    return t, t


def _inv_saved_bwd(t, dt):
    t2 = _split2(t)
    return -_dot3(t2, _split2(_dot3(_split2(dt), t2, _NT)), _TN), jnp.zeros_like(t)


_inv_saved.defvjp(_inv_saved_fwd, _inv_saved_bwd)


def _fold_blocks(m):
    return m[:, 0:CHUNK] + m[:, CHUNK:2 * CHUNK] + m[:, 2 * CHUNK:3 * CHUNK] + m[:, 3 * CHUNK:4 * CHUNK]


def _dn_prep_head(q, k, v, gb, bb, tsaved=None):
    same, causal, strict = _group_masks()
    gc = _dot(causal.astype(F32), gb, exact=True)
    glast = _dot(same.astype(F32), gb, exact=True)
    gc_row = gc.T[0:1, :]
    decay = jnp.exp(jnp.where(causal, gc[:, 0:1] - gc_row, NEG_INF))
    kb = k * bb
    vb = v * bb
    lower = jnp.where(strict, _dot(kb, k, _NT) * decay, 0.0)
    if tsaved is None:
        tinv = _inv_unit_lower(lower)
    else:
        tinv = _inv_saved(lower, jnp.where(same, jnp.concatenate([tsaved] * (GROUP_T // CHUNK), axis=1), 0.0))
    u = _dot(tinv, vb)
    w = _dot(tinv, kb * jnp.exp(gc))
    a = _fold_blocks(jnp.where(causal, _dot(q, k, _NT) * decay, 0.0))
    k_tail = k * jnp.exp(glast - gc)
    q_dec = q * jnp.exp(gc)
    return u, w, q_dec, k_tail, a, glast, _fold_blocks(tinv)


def _dn_prep(q, k, v, gb, bb, *, side=None, name):
    t = q.shape[0]
    ns = 0 if side is None else len(side)
    steps = t // GROUP_T

    def body(*refs):
        q_ref, k_ref, v_ref, gb_ref, bb_ref = refs[:5]
        u_ref, w_ref, qd_ref, kt_ref, a_ref, gl_ref, ti_ref = refs[5 + ns:12 + ns]
        if ns:
            copies = _gather_copies(refs[5:5 + ns], refs[12 + ns:12 + 2 * ns], *refs[12 + 2 * ns:])

            @pl.when(pl.program_id(0) == 0)
            def _():
                for cp in copies:
                    cp.start()

        for h in range(N_HEADS_B):
            sl = slice(h * KEY_DIM_B, (h + 1) * KEY_DIM_B)
            asl = slice(h * CHUNK, (h + 1) * CHUNK)
            u, w, qd, kt, a, gl, ti = _dn_prep_head(q_ref[:, sl], k_ref[:, sl], v_ref[:, sl], gb_ref[:, sl], bb_ref[:, sl])
            u_ref[:, sl] = u
            w_ref[:, sl] = w
            qd_ref[:, sl] = qd
            kt_ref[:, sl] = kt
            gl_ref[:, sl] = gl
            a_ref[:, asl] = a
            ti_ref[:, asl] = ti

        if ns:
            @pl.when(pl.program_id(0) == steps - 1)
            def _():
                for cp in copies:
                    cp.wait()

    row = pl.BlockSpec((GROUP_T, V_B), lambda i: (i, 0))
    arow = pl.BlockSpec((GROUP_T, N_HEADS_B * CHUNK), lambda i: (i, 0))
    big = jax.ShapeDtypeStruct((t, V_B), F32)
    small = jax.ShapeDtypeStruct((t, N_HEADS_B * CHUNK), F32)
    side = [] if side is None else list(side)
    outs = pl.pallas_call(
        body, name=name, grid=(steps,),
        in_specs=[row] * 5 + [HBM_SPEC] * ns, out_specs=[row, row, row, row, arow, row, arow] + [HBM_SPEC] * ns,
        out_shape=[big, big, big, big, small, big, small]
        + [jax.ShapeDtypeStruct((N_CHIPS,) + s.shape, s.dtype) for s in side],
        scratch_shapes=_gather_sems(ns) if ns else [],
        compiler_params=_params(dimension_semantics=("arbitrary",)),
    )(q, k, v, gb, bb, *side)
    return tuple(outs[:7]), (list(outs[7:]) if ns else None)


def _dn_prep_bwd(q, k, v, gb, bb, ti, du, dw, dqd, dkt, da, dgl, *, name):
    t = q.shape[0]

    def body(q_ref, k_ref, v_ref, gb_ref, bb_ref, ti_ref, du_ref, dw_ref, dqd_ref, dkt_ref, da_ref, dgl_ref,
             dq_ref, dk_ref, dv_ref, dgb_ref, dbb_ref):
        for h in range(N_HEADS_B):
            sl = slice(h * KEY_DIM_B, (h + 1) * KEY_DIM_B)
            asl = slice(h * CHUNK, (h + 1) * CHUNK)
            tsaved = ti_ref[:, asl]
            _, vjp = jax.vjp(lambda *a: _dn_prep_head(*a, tsaved=tsaved)[:6],
                             q_ref[:, sl], k_ref[:, sl], v_ref[:, sl], gb_ref[:, sl], bb_ref[:, sl])
            dq, dk, dv, dgb, dbb = vjp((du_ref[:, sl], dw_ref[:, sl], dqd_ref[:, sl], dkt_ref[:, sl],
                                        da_ref[:, asl], dgl_ref[:, sl]))
            dq_ref[:, sl] = dq
            dk_ref[:, sl] = dk
            dv_ref[:, sl] = dv
            dgb_ref[:, sl] = dgb
            dbb_ref[:, sl] = dbb

    row = pl.BlockSpec((GROUP_T, V_B), lambda i: (i, 0))
    arow = pl.BlockSpec((GROUP_T, N_HEADS_B * CHUNK), lambda i: (i, 0))
    big = jax.ShapeDtypeStruct((t, V_B), F32)
    return pl.pallas_call(
        body, name=name, grid=(t // GROUP_T,),
        in_specs=[row] * 5 + [arow] + [row] * 4 + [arow, row], out_specs=[row] * 5, out_shape=[big] * 5,
        compiler_params=_params(dimension_semantics=("parallel",)),
    )(q, k, v, gb, bb, ti, du, dw, dqd, dkt, da, dgl)


def _dn_step(s, qd, kt, u, w, a, gl):
    v_new = u - _dot(w, s)
    o = _dot(qd, s) + _dot(a, v_new)
    s_new = s * jnp.exp(gl[0:1, :]) + _dot(kt, v_new, _TN)
    return s_new, o


def _dn_scan(u, w, qd, kt, a, gl, nbatch, *, name):
    t = u.shape[0]
    ng = t // nbatch // GROUP_T
    cpg = GROUP_T // CHUNK

    def body(u_ref, w_ref, qd_ref, kt_ref, a_ref, gl_ref, o_ref, ss_ref, s_ref):
        @pl.when(pl.program_id(1) == 0)
        def _():
            s_ref[...] = jnp.zeros_like(s_ref)

        def chunk(c, carry):
            rows = pl.ds(pl.multiple_of(c * CHUNK, CHUNK), CHUNK)
            for h in range(N_HEADS_B):
                sl = slice(h * KEY_DIM_B, (h + 1) * KEY_DIM_B)
                s = s_ref[h]
                ss_ref[c, h] = s
                s_new, o = _dn_step(s, qd_ref[rows, sl], kt_ref[rows, sl], u_ref[rows, sl], w_ref[rows, sl],
                                    a_ref[rows, h * CHUNK:(h + 1) * CHUNK], gl_ref[rows, sl])
                s_ref[h] = s_new
                o_ref[rows, sl] = o
            return carry

        lax.fori_loop(0, cpg, chunk, 0)

    row = pl.BlockSpec((GROUP_T, V_B), lambda b, i: (b * ng + i, 0))
    arow = pl.BlockSpec((GROUP_T, N_HEADS_B * CHUNK), lambda b, i: (b * ng + i, 0))
    return pl.pallas_call(
        body, name=name, grid=(nbatch, ng),
        in_specs=[row, row, row, row, arow, row],
        out_specs=[row, pl.BlockSpec((cpg, N_HEADS_B, KEY_DIM_B, VAL_DIM_B), lambda b, i: (b * ng + i, 0, 0, 0))],
        out_shape=[jax.ShapeDtypeStruct((t, V_B), F32),
                   jax.ShapeDtypeStruct((t // CHUNK, N_HEADS_B, KEY_DIM_B, VAL_DIM_B), F32)],
        scratch_shapes=[pltpu.VMEM((N_HEADS_B, KEY_DIM_B, VAL_DIM_B), F32)],
        compiler_params=_params(dimension_semantics=("parallel", "arbitrary")),
    )(u, w, qd, kt, a, gl)


def _dn_scan_bwd(u, w, qd, kt, a, gl, ss, do, nbatch, *, name):
    t = u.shape[0]
    ng = t // nbatch // GROUP_T
    cpg = GROUP_T // CHUNK

    def body(u_ref, w_ref, qd_ref, kt_ref, a_ref, gl_ref, ss_ref, do_ref,
             du_ref, dw_ref, dqd_ref, dkt_ref, da_ref, dgl_ref, ds_ref):
        @pl.when(pl.program_id(1) == 0)
        def _():
            ds_ref[...] = jnp.zeros_like(ds_ref)

        def chunk(cc, carry):
            c = cpg - 1 - cc
            rows = pl.ds(pl.multiple_of(c * CHUNK, CHUNK), CHUNK)
            for h in range(N_HEADS_B):
                sl = slice(h * KEY_DIM_B, (h + 1) * KEY_DIM_B)
                asl = slice(h * CHUNK, (h + 1) * CHUNK)
                _, vjp = jax.vjp(_dn_step, ss_ref[c, h], qd_ref[rows, sl], kt_ref[rows, sl], u_ref[rows, sl],
                                 w_ref[rows, sl], a_ref[rows, asl], gl_ref[rows, sl])
                ds, dqd, dkt, du, dw, da, dgl = vjp((ds_ref[h], do_ref[rows, sl]))
                ds_ref[h] = ds
                dqd_ref[rows, sl] = dqd
                dkt_ref[rows, sl] = dkt
                du_ref[rows, sl] = du
                dw_ref[rows, sl] = dw
                da_ref[rows, asl] = da
                dgl_ref[rows, sl] = dgl
            return carry

        lax.fori_loop(0, cpg, chunk, 0)

    row = pl.BlockSpec((GROUP_T, V_B), lambda b, j: (b * ng + ng - 1 - j, 0))
    arow = pl.BlockSpec((GROUP_T, N_HEADS_B * CHUNK), lambda b, j: (b * ng + ng - 1 - j, 0))
    big = jax.ShapeDtypeStruct((t, V_B), F32)
    return pl.pallas_call(
        body, name=name, grid=(nbatch, ng),
        in_specs=[row, row, row, row, arow, row,
                  pl.BlockSpec((cpg, N_HEADS_B, KEY_DIM_B, VAL_DIM_B), lambda b, j: (b * ng + ng - 1 - j, 0, 0, 0)), row],
        out_specs=[row, row, row, row, arow, row],
        out_shape=[big, big, big, big, jax.ShapeDtypeStruct((t, N_HEADS_B * CHUNK), F32), big],
        scratch_shapes=[pltpu.VMEM((N_HEADS_B, KEY_DIM_B, VAL_DIM_B), F32)],
        compiler_params=_params(dimension_semantics=("parallel", "arbitrary")),
    )(u, w, qd, kt, a, gl, ss, do)


def _rms_gate(o, hcat, dn_g, *, name):
    t = o.shape[0]

    def body(o_ref, z_ref, g_ref, y_ref):
        for h in range(N_HEADS_B):
            sl = slice(h * VAL_DIM_B, (h + 1) * VAL_DIM_B)
            o_ = o_ref[:, sl]
            r = lax.rsqrt(jnp.mean(o_ * o_, axis=-1, keepdims=True) + NORM_EPS)
            y_ref[:, sl] = (o_ * r * g_ref[...] * _silu(z_ref[:, sl])).astype(BF16)

    row = pl.BlockSpec((ROW_T, V_B), lambda i: (i, 0))
    return pl.pallas_call(
        body, name=name, grid=(t // ROW_T,),
        in_specs=[row, pl.BlockSpec((ROW_T, V_B), lambda i: (i, HC_Z // V_B)), pl.BlockSpec((1, VAL_DIM_B), lambda i: (0, 0))],
        out_specs=row, out_shape=jax.ShapeDtypeStruct((t, V_B), BF16),
        compiler_params=_params(dimension_semantics=("parallel",)),
    )(o, hcat, dn_g.reshape(1, VAL_DIM_B))


def _rms_gate_bwd(dy, o, hcat, dn_g, *, name):
    t = o.shape[0]

    def body(dy_ref, o_ref, z_ref, g_ref, do_ref, dz_ref, dg_ref):
        i = pl.program_id(0)
        g = g_ref[...]
        dg_p = jnp.zeros((1, VAL_DIM_B), F32)
        for h in range(N_HEADS_B):
            sl = slice(h * VAL_DIM_B, (h + 1) * VAL_DIM_B)
            o_ = o_ref[:, sl]
            z_ = z_ref[:, sl]
            dy_ = dy_ref[:, sl]
            r = lax.rsqrt(jnp.mean(o_ * o_, axis=-1, keepdims=True) + NORM_EPS)
            n = o_ * r
            sz = _silu(z_)
            dz_ref[:, sl] = (dy_ * n * g * _dsilu(z_)).astype(BF16)
            dg_p = dg_p + jnp.sum(dy_ * n * sz, axis=0, keepdims=True)
            dn = dy_ * g * sz
            do_ref[:, sl] = r * dn - o_ * (r * r * r) * jnp.mean(o_ * dn, axis=-1, keepdims=True)

        @pl.when(i == 0)
        def _():
            dg_ref[...] = dg_p

        @pl.when(i > 0)
        def _():
            dg_ref[...] += dg_p

    row = pl.BlockSpec((ROW_T, V_B), lambda i: (i, 0))
    vec = pl.BlockSpec((1, VAL_DIM_B), lambda i: (0, 0))
    return pl.pallas_call(
        body, name=name, grid=(t // ROW_T,),
        in_specs=[row, row, pl.BlockSpec((ROW_T, V_B), lambda i: (i, HC_Z // V_B)), vec],
        out_specs=[row, row, vec],
        out_shape=[jax.ShapeDtypeStruct((t, V_B), F32), jax.ShapeDtypeStruct((t, V_B), BF16),
                   jax.ShapeDtypeStruct((1, VAL_DIM_B), F32)],
        compiler_params=_params(dimension_semantics=("arbitrary",)),
    )(dy, o, hcat, dn_g.reshape(1, VAL_DIM_B))


def _merge(ya, yb, hcat, *, name):
    t = ya.shape[0]

    def body(ya_ref, yb_ref, ga_ref, gb_ref, y_ref):
        y_ref[...] = (_sigmoid(ga_ref[...]) * ya_ref[...] + _sigmoid(gb_ref[...]) * yb_ref[...]).astype(BF16)

    row = pl.BlockSpec((ROW_T, D_MODEL), lambda i: (i, 0))
    return pl.pallas_call(
        body, name=name, grid=(t // ROW_T,),
        in_specs=[row, row, pl.BlockSpec((ROW_T, D_MODEL), lambda i: (i, HC_GATE // D_MODEL)),
                  pl.BlockSpec((ROW_T, D_MODEL), lambda i: (i, HC_GATE // D_MODEL + 1))],
        out_specs=row, out_shape=jax.ShapeDtypeStruct((t, D_MODEL), BF16),
        compiler_params=_params(dimension_semantics=("parallel",)),
    )(ya, yb, hcat, hcat)


def _merge_bwd(dmix, ya, yb, hcat, *, name):
    t = ya.shape[0]

    def body(d_ref, ya_ref, yb_ref, ga_ref, gb_ref, dya_ref, dyb_ref, dgate_ref):
        d = d_ref[...]
        sa = _sigmoid(ga_ref[...])
        sb = _sigmoid(gb_ref[...])
        dya_ref[...] = (d * sa).astype(BF16)
        dyb_ref[...] = (d * sb).astype(BF16)
        dgate_ref[:, :D_MODEL] = (d * ya_ref[...] * sa * (1.0 - sa)).astype(BF16)
        dgate_ref[:, D_MODEL:] = (d * yb_ref[...] * sb * (1.0 - sb)).astype(BF16)

    row = pl.BlockSpec((ROW_T, D_MODEL), lambda i: (i, 0))
    return pl.pallas_call(
        body, name=name, grid=(t // ROW_T,),
        in_specs=[row, row, row, pl.BlockSpec((ROW_T, D_MODEL), lambda i: (i, HC_GATE // D_MODEL)),
                  pl.BlockSpec((ROW_T, D_MODEL), lambda i: (i, HC_GATE // D_MODEL + 1))],
        out_specs=[row, row, pl.BlockSpec((ROW_T, 2 * D_MODEL), lambda i: (i, 0))],
        out_shape=[jax.ShapeDtypeStruct((t, D_MODEL), BF16)] * 2 + [jax.ShapeDtypeStruct((t, 2 * D_MODEL), BF16)],
        compiler_params=_params(dimension_semantics=("parallel",)),
    )(dmix, ya, yb, hcat, hcat)


def _loss_head(y, target, *, name):
    t, n = y.shape
    tm = _tile(t, (512, 256, 128))

    def body(y_ref, t_ref, part_ref, dy_ref):
        i = pl.program_id(0)
        e = y_ref[...] - t_ref[...]
        dy_ref[...] = e * (1.0 / n)
        p = jnp.sum((e * e).reshape(tm // 8, 8, n), axis=0) * (0.5 / n)

        @pl.when(i == 0)
        def _():
            part_ref[...] = p

        @pl.when(i > 0)
        def _():
            part_ref[...] += p

    row = pl.BlockSpec((tm, n), lambda i: (i, 0))
    return pl.pallas_call(
        body, name=name, grid=(t // tm,),
        in_specs=[row, row], out_specs=[pl.BlockSpec((8, n), lambda i: (0, 0)), row],
        out_shape=[jax.ShapeDtypeStruct((8, n), F32), jax.ShapeDtypeStruct((t, n), F32)],
        compiler_params=_params(dimension_semantics=("arbitrary",)),
    )(y, target)


def _adamw_math(w, g, m, v):
    nm = ADAM_B1 * m + (1.0 - ADAM_B1) * g
    nv = ADAM_B2 * v + (1.0 - ADAM_B2) * (g * g)
    m_hat = nm / (1.0 - ADAM_B1 ** ADAM_STEP)
    v_hat = nv / (1.0 - ADAM_B2 ** ADAM_STEP)
    return -ADAM_LR * (m_hat / (jnp.sqrt(v_hat) + ADAM_EPS) + ADAM_WD * w), nm, nv


def _adamw(w, g, m, v, *, name):
    shape = w.shape
    cols = shape[-1]
    rows = int(np.prod(shape[:-1]))
    w2, g2, m2, v2 = (a.reshape(rows, cols) for a in (w, g, m, v))
    tr = rows
    if rows * cols > 512 * 1024:
        tr = _tile(rows, tuple(c for c in (512, 256, 128, 64, 32, 16, 8) if c * cols <= 256 * 1024))

    def body(w_ref, g_ref, m_ref, v_ref, d_ref, nm_ref, nv_ref):
        d_ref[...], nm_ref[...], nv_ref[...] = _adamw_math(w_ref[...], g_ref[...], m_ref[...], v_ref[...])

    blk = pl.BlockSpec((tr, cols), lambda i: (i, 0))
    outs = pl.pallas_call(
        body, name=name, grid=(rows // tr,),
        in_specs=[blk] * 4, out_specs=[blk] * 3,
        out_shape=[jax.ShapeDtypeStruct((rows, cols), F32)] * 3,
        compiler_params=_params(dimension_semantics=("parallel",)),
    )(w2, g2, m2, v2)
    return tuple(o.reshape(shape) for o in outs)


def _repack_w_in(w_in):
    d = w_in.shape[0]
    o = 0
    parts = {}
    for nm, wd in (("q", Q_A), ("k", KV_W), ("v", KV_W), ("conv", CONV_CH), ("beta", N_HEADS_B), ("dt", N_HEADS_B),
                   ("z", V_B), ("gate", 2 * D_MODEL)):
        parts[nm] = w_in[:, o:o + wd]
        o += wd
    z = lambda n: jnp.zeros((d, n), w_in.dtype)
    return jnp.concatenate([parts["q"], parts["z"], parts["k"], parts["v"], parts["beta"], parts["dt"],
                            z(128 - 2 * N_HEADS_B), z(HC_CONV - HC_BD - 128), parts["conv"], parts["gate"]], axis=1)


MATRIX_NAMES = ("ffn_w13", "ffn_w2", "w_in", "w_branch_a", "w_branch_b", "w_out")


def _dw_in_by_owner(dw):
    sections = ((Q_A, HC_Q), (2 * KV_W, HC_K), (CONV_CH, HC_CONV), (2 * N_HEADS_B, HC_BD), (V_B, HC_Z),
                (2 * D_MODEL, HC_GATE))
    per = N_IN // 4
    owners = []
    for o in range(4):
        lo, hi, start, parts = o * per, (o + 1) * per, 0, []
        for width, off in sections:
            a, b = max(lo, start), min(hi, start + width)
            if a < b:
                parts.append(dw[:, off + a - start:off + b - start])
            start += width
        owners.append(jnp.concatenate(parts, axis=1))
    return jnp.stack(owners)


def _lane_row(vals):
    return jnp.pad(vals.astype(F32).reshape(1, N_HEADS_B), ((0, 0), (N_HEADS_B, 128 - 2 * N_HEADS_B)))


def _local_step(x, target, rel_bias, layer_wts, side_shards=None, side_assemble=None):
    nbatch, seq, d = x.shape
    t = nbatch * seq
    depth = len(layer_wts)
    layer_wts = list(layer_wts)
    x0 = x.reshape(t, d)
    tgt = target.reshape(t, d)

    onehot = jnp.asarray(_bucket_onehot())
    rel_t = jnp.pad(rel_bias.T, ((0, 0), (0, 128 - NUM_BUCKETS)))
    bias = _mm(rel_t, onehot, tb=True, exact=True, name="pos_bias")
    bias = bias.reshape(N_HEADS_A, WINDOW, 2 * WINDOW)

    saved = []
    xin, xin_b = x0, x0.astype(BF16)
    for i in range(depth):
        L = {}
        W = layer_wts[i]
        tag = f"_l{i}"
        a = _ffn_up_act(xin_b, W["ffn_w13"][0], name="ffn_up_act" + tag + "a")
        r1, x1, x1_b = _mm_res_ln(a, W["ffn_w2"][0], xin, W["ln_g"][0], W["ln_b"][0],
                                  alpha=DN_ALPHA, c=0.5, name="ffn_down_ln" + tag + "a")
        L.update(x0_b=xin_b, a0=a, r1=r1, x1=x1, x1_b=x1_b)
        hcat = _mm(x1_b, W["w_in_p"], name="in_proj" + tag)
        ao = _attn_fwd(hcat, bias, W["sinks"], nbatch, name="swa" + tag)
        ya = _mm(ao, W["w_branch_a"], name="branch_a" + tag)
        c, qn, kn, vs = _conv_prep(hcat, W["conv_w"], nbatch, name="conv_prep" + tag)
        a_row = _lane_row(W["a_log"])
        dt_row = _lane_row(W["dt_bias"])
        gb, bb = _gates(hcat, a_row, dt_row, name="gates" + tag)
        if i + 1 < depth and layer_wts[i + 1] is None:
            (u, w, qd, kt, aa, gl, ti), gathered = _dn_prep(qn, kn, vs, gb, bb, side=side_shards[i + 1],
                                                            name="dn_prep" + tag)
            layer_wts[i + 1] = side_assemble(i + 1, gathered)
        else:
            (u, w, qd, kt, aa, gl, ti), _ = _dn_prep(qn, kn, vs, gb, bb, name="dn_prep" + tag)
        o, ss = _dn_scan(u, w, qd, kt, aa, gl, nbatch, name="dn_scan" + tag)
        on = _rms_gate(o, hcat, W["dn_norm_g"], name="rms_gate" + tag)
        yb = _mm(on, W["w_branch_b"], name="branch_b" + tag)
        mix = _merge(ya, yb, hcat, name="merge" + tag)
        r2, x2, x2_b = _mm_res_ln(mix, W["w_out"], x1, W["ln_g"][1], W["ln_b"][1],
                                  alpha=DN_ALPHA, c=1.0, name="out_proj_ln" + tag)
        L.update(hcat=hcat, ao=ao, ya=ya, c=c, qn=qn, kn=kn, vs=vs, gb=gb, bb=bb, a_row=a_row, dt_row=dt_row,
                 u=u, w=w, qd=qd, kt=kt, aa=aa, gl=gl, ti=ti, o=o, ss=ss, on=on, yb=yb, mix=mix, r2=r2, x2_b=x2_b)
        a = _ffn_up_act(x2_b, W["ffn_w13"][1], name="ffn_up_act" + tag + "b")
        r3, x3, x3_b = _mm_res_ln(a, W["ffn_w2"][1], x2, W["ln_g"][2], W["ln_b"][2],
                                  alpha=DN_ALPHA, c=0.5, name="ffn_down_ln" + tag + "b")
        L.update(a1=a, r3=r3)
        saved.append(L)
        xin, xin_b = x3, x3_b

    part, dy = _loss_head(xin, tgt, name="loss_head")
    loss = jnp.sum(part)

    grads = {k: [None] * depth for k in ("ln_g", "ln_b", "ffn_w13", "ffn_w2", "w_in", "conv_w", "a_log", "dt_bias",
                                          "dn_norm_g", "sinks", "w_branch_a", "w_branch_b", "w_out")}
    dbias_total = None
    for i in reversed(range(depth)):
        L = saved[i]
        W = layer_wts[i]
        tag = f"_l{i}"
        dln_g, dln_b, dw13, dw2 = [None] * 3, [None] * 3, [None] * 2, [None] * 2

        def ffn_bwd(dyo, r, xprev_b, asave, j, sfx):
            dres, df, dln_g[2 * j], dln_b[2 * j] = _ln_bwd(dyo, r, W["ln_g"][2 * j], alpha=DN_ALPHA, c=0.5,
                                                           name="ln_bwd" + tag + sfx)
            dh = _ffn_bwd_mid(xprev_b, W["ffn_w13"][j], df, W["ffn_w2"][j], name="ffn_bwd_mid" + tag + sfx)
            dw2[j] = _mm(asave, df, ta=True, name="ffn_w2_grad" + tag + sfx)
            dw13[j] = _mm(xprev_b, dh, ta=True, b_halves=True, name="ffn_w13_grad" + tag + sfx)
            return _mm(dh, W["ffn_w13"][j], tb=True, a_halves=True, add=dres, name="ffn_up_bwd" + tag + sfx)

        dx2 = ffn_bwd(dy, L["r3"], L["x2_b"], L["a1"], 1, "b")

        dres2, dymix, dln_g[1], dln_b[1] = _ln_bwd(dx2, L["r2"], W["ln_g"][1], alpha=DN_ALPHA, c=1.0,
                                                   name="ln_bwd" + tag + "m")
        hcat = L["hcat"]
        dmix = _mm(dymix, W["w_out"], tb=True, name="out_proj_bwd" + tag)
        grads["w_out"][i] = _mm(L["mix"], dymix, ta=True, name="w_out_grad" + tag)
        dya, dyb, dgate = _merge_bwd(dmix, L["ya"], L["yb"], hcat, name="merge_bwd" + tag)
        dao = _mm(dya, W["w_branch_a"], tb=True, name="branch_a_bwd" + tag)
        grads["w_branch_a"][i] = _mm(L["ao"], dya, ta=True, name="w_branch_a_grad" + tag)
        don = _mm(dyb, W["w_branch_b"], tb=True, name="branch_b_bwd" + tag)
        grads["w_branch_b"][i] = _mm(L["on"], dyb, ta=True, name="w_branch_b_grad" + tag)
        do, dz, ddn = _rms_gate_bwd(don, L["o"], hcat, W["dn_norm_g"], name="rms_gate_bwd" + tag)
        grads["dn_norm_g"][i] = ddn.reshape(VAL_DIM_B)
        du, dw, dqd, dkt, daa, dgl = _dn_scan_bwd(L["u"], L["w"], L["qd"], L["kt"], L["aa"], L["gl"], L["ss"], do,
                                                  nbatch, name="dn_scan_bwd" + tag)
        dqn, dkn, dvs, dgb, dbb = _dn_prep_bwd(L["qn"], L["kn"], L["vs"], L["gb"], L["bb"], L["ti"], du, dw, dqd, dkt,
                                               daa, dgl, name="dn_prep_bwd" + tag)
        dc = _conv_prep_bwd_pointwise(dqn, dkn, dvs, L["c"], name="conv_prep_bwd" + tag)
        dconv, dconv_w = _conv_bwd(dc, hcat, W["conv_w"], nbatch, name="conv_bwd" + tag)
        grads["conv_w"][i] = dconv_w[:CONV_K]
        dbd, da_log, ddt = _gates_bwd(dgb, dbb, hcat, L["a_row"], L["dt_row"], name="gates_bwd" + tag)
        grads["a_log"][i] = da_log[0, N_HEADS_B:2 * N_HEADS_B]
        grads["dt_bias"][i] = ddt[0, N_HEADS_B:2 * N_HEADS_B]
        dq, dk, dv, dbias, dsink = _attn_bwd(hcat, bias, W["sinks"], dao, nbatch, name="swa_bwd" + tag)
        grads["sinks"][i] = dsink.reshape(N_HEADS_A)
        dbias_total = dbias if dbias_total is None else dbias_total + dbias
        dhcat = jnp.concatenate([dq, dz, dk, dv, dbd, jnp.zeros((t, HC_CONV - HC_BD - 128), BF16), dconv, dgate], axis=1)
        dw_in_p = _mm(L["x1_b"], dhcat, ta=True, name="w_in_grad" + tag)
        grads["w_in"][i] = _dw_in_by_owner(dw_in_p)
        dx1 = _mm(dhcat, W["w_in_p"], tb=True, add=dres2, name="in_proj_bwd" + tag)

        dy = ffn_bwd(dx1, L["r1"], L["x0_b"], L["a0"], 0, "a")
        grads["ln_g"][i] = jnp.concatenate(dln_g, axis=0)
        grads["ln_b"][i] = jnp.concatenate(dln_b, axis=0)
        grads["ffn_w13"][i] = dw13
        grads["ffn_w2"][i] = dw2

    out = {k: (v if k in MATRIX_NAMES else jnp.stack(v)) for k, v in grads.items()}
    drel = _mm(dbias_total.reshape(N_HEADS_A, WINDOW * 2 * WINDOW), onehot, name="rel_bias_grad")
    out["rel_bias"] = drel[:, :NUM_BUCKETS].T
    return loss, dy.reshape(nbatch, seq, d), out


N_CHIPS = 4
MESH_ID = pl.DeviceIdType.MESH
HBM_SPEC = pl.BlockSpec(memory_space=pltpu.HBM)


def _place():
    x, y, c = lax.axis_index("x"), lax.axis_index("y"), lax.axis_index("c")
    others = [(1 - x, y), (x, 1 - y), (1 - x, 1 - y)]
    return x, y, c, others


def _chip_index(cx, cy):
    return 2 * cx + cy


def _gather_sems(n):
    return [pltpu.SemaphoreType.DMA((n, 3)), pltpu.SemaphoreType.DMA((n, 3)), pltpu.SemaphoreType.DMA((n,))]


def _gather_copies(ins, outs, send_sems, recv_sems, local_sems):
    x, y, c, others = _place()
    me = _chip_index(x, y)
    copies = []
    for i in range(len(ins)):
        copies.append(pltpu.make_async_copy(ins[i], outs[i].at[me], local_sems.at[i]))
        for k, (ox, oy) in enumerate(others):
            copies.append(pltpu.make_async_remote_copy(src_ref=ins[i], dst_ref=outs[i].at[me], send_sem=send_sems.at[i, k],
                                                       recv_sem=recv_sems.at[i, k], device_id=(ox, oy, c),
                                                       device_id_type=MESH_ID))
    return copies


def _allgather_chips(tensors, *, name):
    n = len(tensors)

    def body(*refs):
        copies = _gather_copies(refs[:n], refs[n:2 * n], *refs[2 * n:])
        for cp in copies:
            cp.start()
        for cp in copies:
            cp.wait()

    return pl.pallas_call(
        body, name=name,
        in_specs=[HBM_SPEC] * n, out_specs=[HBM_SPEC] * n,
        out_shape=[jax.ShapeDtypeStruct((N_CHIPS,) + t.shape, t.dtype) for t in tensors],
        scratch_shapes=_gather_sems(n),
    )(*tensors)


def _allgather_devices(v, *, name):
    def body(v_ref, o_ref, send_sems, recv_sems, local_sem):
        x, y, c, _ = _place()
        me = 4 * x + 2 * y + c
        loc = pltpu.make_async_copy(v_ref, o_ref.at[me], local_sem)
        loc.start()
        copies = [loc]
        for k in range(1, 8):
            px, py, pc = x ^ (k >> 2), y ^ ((k >> 1) & 1), c ^ (k & 1)
            cp = pltpu.make_async_remote_copy(src_ref=v_ref, dst_ref=o_ref.at[me], send_sem=send_sems.at[k - 1],
                                              recv_sem=recv_sems.at[k - 1], device_id=(px, py, pc), device_id_type=MESH_ID)
            cp.start()
            copies.append(cp)
        for cp in copies:
            cp.wait()

    return pl.pallas_call(
        body, name=name, in_specs=[HBM_SPEC], out_specs=HBM_SPEC,
        out_shape=jax.ShapeDtypeStruct((8,) + v.shape, v.dtype),
        scratch_shapes=[pltpu.SemaphoreType.DMA((7,)), pltpu.SemaphoreType.DMA((7,)), pltpu.SemaphoreType.DMA],
    )(v)


def _sum_slots(g, *, name):
    nb, n, r, l = g.shape
    tr = r // 2 if r % 32 == 0 else r

    def body(g_ref, o_ref):
        acc = g_ref[0].astype(F32)
        for k in range(1, n):
            acc = acc + g_ref[k].astype(F32)
        o_ref[...] = acc

    return pl.pallas_call(
        body, name=name, grid=(nb, r // tr),
        in_specs=[pl.BlockSpec((None, n, tr, l), lambda b, i: (b, 0, i, 0))],
        out_specs=pl.BlockSpec((None, tr, l), lambda b, i: (b, i, 0)),
        out_shape=jax.ShapeDtypeStruct((nb, r, l), F32),
        compiler_params=_params(dimension_semantics=("parallel", "parallel")),
    )(g)


def _half_window(ref, kind, h):
    if kind == "rows":
        return ref.at[:, h]
    r = ref.shape[0] // 2
    return ref.at[pl.ds(pl.multiple_of(h * r, r), r), :]


def _owner_window(ref, kind, o):
    if kind == "rows":
        return ref.at[o]
    cols = ref.shape[1] // N_CHIPS
    return ref.at[:, pl.ds(pl.multiple_of(o * cols, cols), cols)]


def _half_shape(g, kind):
    return (g.shape[0],) + g.shape[2:] if kind == "rows" else (g.shape[0] // 2, g.shape[1])


def _swap_halves(gs, kinds, *, name):
    n = len(gs)

    def body(*refs):
        ins, outs, send_sems, recv_sems = refs[:n], refs[n:2 * n], refs[2 * n], refs[2 * n + 1]
        x, y, c, _ = _place()
        copies = []
        for i in range(n):
            cp = pltpu.make_async_remote_copy(src_ref=_half_window(ins[i], kinds[i], 1 - c), dst_ref=outs[i],
                                              send_sem=send_sems.at[i], recv_sem=recv_sems.at[i],
                                              device_id=(x, y, 1 - c), device_id_type=MESH_ID)
            cp.start()
            copies.append(cp)
        for cp in copies:
            cp.wait()

    return pl.pallas_call(
        body, name=name, in_specs=[HBM_SPEC] * n, out_specs=[HBM_SPEC] * n,
        out_shape=[jax.ShapeDtypeStruct(_half_shape(g, k), g.dtype) for g, k in zip(gs, kinds)],
        scratch_shapes=[pltpu.SemaphoreType.DMA((n,)), pltpu.SemaphoreType.DMA((n,))],
    )(*gs)


def _pair_sum(g, got, kind, c_idx, *, name):
    hs = _half_shape(g, kind)

    def body(c_ref, g_ref, r_ref, o_ref):
        o_ref[...] = (g_ref[...] + r_ref[...]).astype(BF16)

    if kind == "rows":
        _, _, r, cols = g.shape
        grid = (N_CHIPS,)
        in_specs = [pl.BlockSpec((None, None, r, cols), lambda o, c_ref: (o, c_ref[0], 0, 0)),
                    pl.BlockSpec((None, r, cols), lambda o, c_ref: (o, 0, 0))]
        out_spec = pl.BlockSpec((None, r, cols), lambda o, c_ref: (o, 0, 0))
    else:
        r, cols = hs
        steps = 4
        tr = r // steps
        grid = (steps,)
        in_specs = [pl.BlockSpec((tr, cols), lambda i, c_ref: (c_ref[0] * steps + i, 0)),
                    pl.BlockSpec((tr, cols), lambda i, c_ref: (i, 0))]
        out_spec = pl.BlockSpec((tr, cols), lambda i, c_ref: (i, 0))
    return pl.pallas_call(
        body, name=name,
        grid_spec=pltpu.PrefetchScalarGridSpec(num_scalar_prefetch=1, grid=grid, in_specs=in_specs, out_specs=out_spec),
        out_shape=jax.ShapeDtypeStruct(hs, BF16),
        compiler_params=_params(dimension_semantics=("parallel",)),
    )(c_idx, g, got)


def _exchange_chips(ss, kinds, places, out_shapes, *, name):
    n = len(ss)
    no = len(out_shapes)

    def body(*refs):
        ins, outs = refs[:n], refs[n:n + no]
        send_sems, recv_sems, local_sems = refs[n + no:]
        x, y, c, others = _place()
        me = _chip_index(x, y)
        copies = []
        for i in range(n):
            oi, lead = places[i]
            dst = outs[oi].at[(*lead, me)]
            copies.append(pltpu.make_async_copy(_owner_window(ins[i], kinds[i], me), dst, local_sems.at[i]))
            for k, (ox, oy) in enumerate(others):
                copies.append(pltpu.make_async_remote_copy(
                    src_ref=_owner_window(ins[i], kinds[i], _chip_index(ox, oy)), dst_ref=dst, send_sem=send_sems.at[i, k],
                    recv_sem=recv_sems.at[i, k], device_id=(ox, oy, c), device_id_type=MESH_ID))
        for cp in copies:
            cp.start()
        for cp in copies:
            cp.wait()

    return pl.pallas_call(
        body, name=name, in_specs=[HBM_SPEC] * n, out_specs=[HBM_SPEC] * no,
        out_shape=[jax.ShapeDtypeStruct(s, BF16) for s in out_shapes],
        scratch_shapes=_gather_sems(n),
    )(*ss)


def _send_halves(fs, *, name):
    n = len(fs)

    def body(*refs):
        ins, outs, send_sems, recv_sems = refs[:n], refs[n:2 * n], refs[2 * n], refs[2 * n + 1]
        x, y, c, _ = _place()
        copies = [pltpu.make_async_remote_copy(src_ref=ins[i], dst_ref=outs[i], send_sem=send_sems.at[i],
                                               recv_sem=recv_sems.at[i], device_id=(x, y, 1 - c), device_id_type=MESH_ID)
                  for i in range(n)]
        for cp in copies:
            cp.start()
        for cp in copies:
            cp.wait()

    return pl.pallas_call(
        body, name=name, in_specs=[HBM_SPEC] * n, out_specs=[HBM_SPEC] * n,
        out_shape=[jax.ShapeDtypeStruct(f.shape, f.dtype) for f in fs],
        scratch_shapes=[pltpu.SemaphoreType.DMA((n,)), pltpu.SemaphoreType.DMA((n,))],
    )(*fs)


def _adamw_halves(w, m, v, own, other, c_idx, *, name):
    shape = w.shape
    nl, r, cols = own.shape
    w4, m4, v4 = (a.reshape(nl, 2, r, cols) for a in (w, m, v))
    tr = r if r * cols * 4 <= 3 * 512 * 1024 else _tile(r, tuple(c for c in (256, 128, 64, 32, 16, 8) if c * cols <= 256 * 1024))

    def body(c_ref, w_ref, m_ref, v_ref, own_ref, other_ref, g_ref, d_ref, nm_ref, nv_ref):
        g_ = jnp.where(pl.program_id(1) == c_ref[0], own_ref[...], other_ref[...])
        g_ref[...] = g_
        d_ref[...], nm_ref[...], nv_ref[...] = _adamw_math(w_ref[...], g_, m_ref[...], v_ref[...])

    full = pl.BlockSpec((None, None, tr, cols), lambda l, h, i, c_ref: (l, h, i, 0))
    half = pl.BlockSpec((None, tr, cols), lambda l, h, i, c_ref: (l, i, 0))
    outs = pl.pallas_call(
        body, name=name,
        grid_spec=pltpu.PrefetchScalarGridSpec(num_scalar_prefetch=1, grid=(nl, 2, r // tr),
                                               in_specs=[full, full, full, half, half], out_specs=[full] * 4),
        out_shape=[jax.ShapeDtypeStruct((nl, 2, r, cols), F32)] * 4,
        compiler_params=_params(dimension_semantics=("parallel", "parallel", "parallel")),
    )(c_idx, w4, m4, v4, own, other)
    return tuple(o.reshape(shape) for o in outs)


SHARD_AXIS = {"rel_bias": None, "ln_g": 2, "ln_b": 2, "ffn_w13": 3, "ffn_w2": 2, "w_in": 2, "conv_w": 2, "a_log": None,
              "dt_bias": None, "dn_norm_g": None, "sinks": None, "w_branch_a": 1, "w_branch_b": 1, "w_out": 1}
WEIGHT_NAMES = tuple(SHARD_AXIS)
SMALL_NAMES = tuple(n for n in WEIGHT_NAMES if n not in MATRIX_NAMES)
PACK_LANES = 1024


def _unshard(gathered, axis):
    g = jnp.moveaxis(gathered, 0, axis)
    return g.reshape(g.shape[:axis] + (g.shape[axis] * g.shape[axis + 1],) + g.shape[axis + 2:])


def _reduce_matrices(grads, shard_shapes, c_idx):
    gs, kinds, places = [], [], []
    names = [n for n in MATRIX_NAMES]
    for oi, n in enumerate(names):
        for i, g in enumerate(grads[n]):
            if n == "ffn_w13":
                for j, gj in enumerate(g):
                    gs.append(gj)
                    kinds.append("cols")
                    places.append((oi, (i, j)))
            elif n == "ffn_w2":
                for j, gj in enumerate(g):
                    gs.append(gj.reshape(N_CHIPS, 2, gj.shape[0] // (2 * N_CHIPS), gj.shape[1]))
                    kinds.append("rows")
                    places.append((oi, (i, j)))
            elif n == "w_in":
                gs.append(g.reshape(N_CHIPS, 2, g.shape[1] // 2, g.shape[2]))
                kinds.append("rows")
                places.append((oi, (i,)))
            else:
                gs.append(g.reshape(N_CHIPS, 2, g.shape[0] // (2 * N_CHIPS), g.shape[1]))
                kinds.append("rows")
                places.append((oi, (i,)))
    got = _swap_halves(gs, kinds, name="rs_swap_halves")
    ss = [_pair_sum(g, r, k, c_idx, name=f"rs_pair_sum_{i}") for i, (g, r, k) in enumerate(zip(gs, got, kinds))]
    ex_shapes = []
    for n in names:
        s = shard_shapes[n]
        ex_shapes.append(s[:-2] + (N_CHIPS, s[-2] // 2, s[-1]))
    ex = _exchange_chips(ss, kinds, places, ex_shapes, name="rs_exchange_chips")
    own = [_sum_slots(e.reshape((-1,) + e.shape[-3:]), name="rs_chip_sum_" + n) for n, e in zip(names, ex)]
    other = _send_halves(own, name="rs_send_halves")
    return {n: (a, b) for n, a, b in zip(names, own, other)}


def _reduce_small(grads):
    flat = [grads[n].astype(F32).reshape(-1) for n in SMALL_NAMES]
    total = sum(f.shape[0] for f in flat)
    rows = -(-total // (16 * PACK_LANES)) * 16
    vec = jnp.concatenate(flat + [jnp.zeros((rows * PACK_LANES - total,), F32)]).reshape(rows, PACK_LANES)
    s = _sum_slots(_allgather_devices(vec, name="small_allgather")[None], name="small_sum").reshape(-1)
    out, o = {}, 0
    for n, f in zip(SMALL_NAMES, flat):
        out[n] = s[o:o + f.shape[0]].reshape(grads[n].shape)
        o += f.shape[0]
    return out


def kernel(x, rel_bias, ln_g, ln_b, ffn_w13, ffn_w2, w_in, conv_w, a_log, dt_bias, dn_norm_g, sinks, w_branch_a, w_branch_b, w_out, loss_target, m_rel_bias, m_ln_g, m_ln_b, m_ffn_w13, m_ffn_w2, m_w_in, m_conv_w, m_a_log, m_dt_bias, m_dn_norm_g, m_sinks, m_w_branch_a, m_w_branch_b, m_w_out, v_rel_bias, v_ln_g, v_ln_b, v_ffn_w13, v_ffn_w2, v_w_in, v_conv_w, v_a_log, v_dt_bias, v_dn_norm_g, v_sinks, v_w_branch_a, v_w_branch_b, v_w_out):
    w = dict(rel_bias=rel_bias, ln_g=ln_g, ln_b=ln_b, ffn_w13=ffn_w13, ffn_w2=ffn_w2, w_in=w_in, conv_w=conv_w,
             a_log=a_log, dt_bias=dt_bias, dn_norm_g=dn_norm_g, sinks=sinks, w_branch_a=w_branch_a,
             w_branch_b=w_branch_b, w_out=w_out)
    m = dict(rel_bias=m_rel_bias, ln_g=m_ln_g, ln_b=m_ln_b, ffn_w13=m_ffn_w13, ffn_w2=m_ffn_w2, w_in=m_w_in,
             conv_w=m_conv_w, a_log=m_a_log, dt_bias=m_dt_bias, dn_norm_g=m_dn_norm_g, sinks=m_sinks,
             w_branch_a=m_w_branch_a, w_branch_b=m_w_branch_b, w_out=m_w_out)
    v = dict(rel_bias=v_rel_bias, ln_g=v_ln_g, ln_b=v_ln_b, ffn_w13=v_ffn_w13, ffn_w2=v_ffn_w2, w_in=v_w_in,
             conv_w=v_conv_w, a_log=v_a_log, dt_bias=v_dt_bias, dn_norm_g=v_dn_norm_g, sinks=v_sinks,
             w_branch_a=v_w_branch_a, w_branch_b=v_w_branch_b, w_out=v_w_out)

    depth = w_in.shape[0]
    sharded = [n for n in WEIGHT_NAMES if SHARD_AXIS[n] is not None]

    def shards_of(i):
        return [w[n][i].astype(MXU_DTYPE) if n in MATRIX_NAMES else w[n][i] for n in sharded]

    def assemble(i, gathered):
        lw = {n: _unshard(g, SHARD_AXIS[n] - 1) for n, g in zip(sharded, gathered)}
        lw["w_in_p"] = _repack_w_in(lw.pop("w_in"))
        for n in ("a_log", "dt_bias", "dn_norm_g", "sinks"):
            lw[n] = w[n][i]
        return lw

    first = assemble(0, _allgather_chips(shards_of(0), name="weights_allgather_l0"))
    loss_part, grad_x, grads = _local_step(x, loss_target, rel_bias, [first] + [None] * (depth - 1),
                                           side_shards=[None] + [shards_of(i) for i in range(1, depth)],
                                           side_assemble=assemble)
    loss = lax.psum(loss_part, ("x", "y", "c"))

    c_idx = lax.axis_index("c").astype(jnp.int32).reshape(1)
    halves = _reduce_matrices(grads, {n: w[n].shape for n in MATRIX_NAMES}, c_idx)
    chip = _chip_index(lax.axis_index("x"), lax.axis_index("y"))
    small = _reduce_small(grads)
    outs = {}
    for n in WEIGHT_NAMES:
        if n in MATRIX_NAMES:
            outs[n] = _adamw_halves(w[n], m[n], v[n], *halves[n], c_idx, name="adamw_" + n)
        else:
            axis = SHARD_AXIS[n]
            g = small[n]
            if axis is not None:
                g = lax.dynamic_slice_in_dim(g, chip * w[n].shape[axis], w[n].shape[axis], axis)
            outs[n] = (g,) + _adamw(w[n], g, m[n], v[n], name="adamw_" + n)
    return (loss, grad_x, *[outs[n][0] for n in WEIGHT_NAMES], *[outs[n][1] for n in WEIGHT_NAMES],
            *[outs[n][2] for n in WEIGHT_NAMES], *[outs[n][3] for n in WEIGHT_NAMES])
```

```python
import functools
import math

import numpy as np
import jax
import jax.numpy as jnp
from jax import lax
from jax.experimental import pallas as pl
from jax.experimental.pallas import tpu as pltpu

F32 = jnp.float32
BF16 = jnp.bfloat16
MXU_DTYPE = BF16
HIGHEST = lax.Precision.HIGHEST

D_MODEL = 1024
N_HEADS_A = 16
N_KV_A = 4
HEAD_DIM_A = 64
GROUP_A = N_HEADS_A // N_KV_A
WINDOW = 128
N_HEADS_B = 8
KEY_DIM_B = 128
VAL_DIM_B = 128
CONV_K = 4
CHUNK = 64
D_FF = 2816
NUM_BUCKETS = 32
MAX_DISTANCE = 128
DEPTH = 4
DN_ALPHA = (2 * DEPTH) ** 0.25
LN_EPS = 1e-5
NORM_EPS = 1e-6
NEG_INF = -1e30

Q_A = N_HEADS_A * HEAD_DIM_A
KV_W = N_KV_A * HEAD_DIM_A
QK_B = N_HEADS_B * KEY_DIM_B
V_B = N_HEADS_B * VAL_DIM_B
CONV_CH = 2 * QK_B + V_B
N_IN = Q_A + 2 * KV_W + CONV_CH + 2 * N_HEADS_B + V_B + 2 * D_MODEL

ADAM_LR = 0.001
ADAM_B1 = 0.9
ADAM_B2 = 0.999
ADAM_EPS = 1e-08
ADAM_WD = 0.01
ADAM_STEP = 10

HC_W = 8192
HC_Q = 0
HC_Z = 1024
HC_K = 2048
HC_V = 2304
HC_BD = 2560
HC_CONV = 3072
HC_GATE = 6144

GROUP_T = 256
DN_BWD_HEADS = 4
ROW_T = 256
VMEM_LIMIT_BYTES = 48 * 1024 * 1024


def _params(**kw):
    return pltpu.CompilerParams(vmem_limit_bytes=VMEM_LIMIT_BYTES, **kw)


def _tile(n, cands):
    for c in cands:
        if n % c == 0:
            return c
    return n


def _dot(a, b, dims=(((1,), (0,)), ((), ())), exact=False):
    if exact:
        return lax.dot_general(a.astype(F32), b.astype(F32), dims, precision=HIGHEST, preferred_element_type=F32)
    return lax.dot_general(a.astype(MXU_DTYPE), b.astype(MXU_DTYPE), dims, preferred_element_type=F32)


_NN = (((1,), (0,)), ((), ()))
_NT = (((1,), (1,)), ((), ()))
_TN = (((0,), (0,)), ((), ()))


def _sigmoid(x):
    return 1.0 / (1.0 + jnp.exp(-x))


def _silu(x):
    return x * _sigmoid(x)


def _dsilu(x):
    s = _sigmoid(x)
    return s * (1.0 + x * (1.0 - s))


def _mm(a, b, *, ta=False, tb=False, a_halves=False, b_halves=False, out_dtype=F32, add=None, exact=False, name):
    if a_halves:
        m, kdim = a.shape[1], 2 * a.shape[2]
    else:
        (kdim, m) = a.shape if ta else a.shape[::-1]
    if b_halves:
        kb, n = b.shape[1], 2 * b.shape[2]
    else:
        (n, kb) = b.shape if tb else b.shape[::-1]
    assert kdim == kb, (a.shape, b.shape, ta, tb)
    tn = _tile(n, (1024, 1408, 512, 256, 128))
    tk = _tile(kdim, (1024, 1408, 512, 256, 128))
    nk = kdim // tk
    tm = _tile(m, (1024, 1408, 512, 256, 128) if nk > 1 else (512, 256, 128))
    nj = n // tn
    dims = (((0 if ta else 1,), (1 if tb else 0,)), ((), ()))
    has_add = add is not None

    def body(*refs):
        if has_add:
            a_ref, b_ref, add_ref, o_ref = refs[:4]
        else:
            a_ref, b_ref, o_ref = refs[:3]
            add_ref = None
        part = _dot(a_ref[...], b_ref[...], dims, exact)

        def finish(acc):
            if has_add:
                acc = acc + add_ref[...].astype(F32)
            o_ref[...] = acc.astype(out_dtype)

        if nk == 1:
            finish(part)
        else:
            acc_ref = refs[-1]
            k = pl.program_id(2)

            @pl.when(k == 0)
            def _():
                acc_ref[...] = part

            @pl.when(k > 0)
            def _():
                acc_ref[...] += part

            @pl.when(k == nk - 1)
            def _():
                finish(acc_ref[...])

    if a_halves:
        assert not ta and nk % 2 == 0
        a_spec = pl.BlockSpec((None, tm, tk), lambda i, j, k: (k // (nk // 2), i, k % (nk // 2)))
    elif ta:
        a_spec = pl.BlockSpec((tk, tm), lambda i, j, k: (k, i))
    else:
        a_spec = pl.BlockSpec((tm, tk), lambda i, j, k: (i, k))
    if b_halves:
        assert not tb and nj % 2 == 0
        b_spec = pl.BlockSpec((None, tk, tn), lambda i, j, k: (j // (nj // 2), k, j % (nj // 2)))
    elif tb:
        b_spec = pl.BlockSpec((tn, tk), lambda i, j, k: (j, k))
    else:
        b_spec = pl.BlockSpec((tk, tn), lambda i, j, k: (k, j))
    o_spec = pl.BlockSpec((tm, tn), lambda i, j, k: (i, j))
    in_specs = [a_spec, b_spec] + ([o_spec] if has_add else [])
    args = (a, b) + ((add,) if has_add else ())
    return pl.pallas_call(
        body, name=name, grid=(m // tm, nj, nk),
        in_specs=in_specs, out_specs=o_spec,
        out_shape=jax.ShapeDtypeStruct((m, n), out_dtype),
        scratch_shapes=[pltpu.VMEM((tm, tn), F32)] if nk > 1 else [],
        compiler_params=_params(dimension_semantics=("parallel", "parallel", "arbitrary")),
    )(*args)


def _layernorm_rows(r, g, b):
    mu = jnp.mean(r, axis=-1, keepdims=True)
    xc = r - mu
    var = jnp.mean(xc * xc, axis=-1, keepdims=True)
    return xc * lax.rsqrt(var + LN_EPS) * g + b


def _mm_res_ln(a, w, resid, g, b, *, alpha, c, name):
    m, kdim = a.shape
    n = w.shape[1]
    tm = _tile(m, (512, 256, 128))

    def body(a_ref, w_ref, x_ref, g_ref, b_ref, r_ref, y_ref, yb_ref):
        f = _dot(a_ref[...], w_ref[...])
        r = alpha * x_ref[...] + c * f
        r_ref[...] = r
        y = _layernorm_rows(r, g_ref[...], b_ref[...])
        y_ref[...] = y
        yb_ref[...] = y.astype(BF16)

    row = pl.BlockSpec((tm, n), lambda i: (i, 0))
    vec = pl.BlockSpec((1, n), lambda i: (0, 0))
    return pl.pallas_call(
        body, name=name, grid=(m // tm,),
        in_specs=[pl.BlockSpec((tm, kdim), lambda i: (i, 0)), pl.BlockSpec((kdim, n), lambda i: (0, 0)), row, vec, vec],
        out_specs=[row, row, row],
        out_shape=[jax.ShapeDtypeStruct((m, n), F32)] * 2 + [jax.ShapeDtypeStruct((m, n), BF16)],
        compiler_params=_params(dimension_semantics=("parallel",)),
    )(a, w, resid, g.reshape(1, n), b.reshape(1, n))


def _ln_bwd(dy, r, g, *, alpha, c, name):
    m, n = dy.shape
    tm = _tile(m, (512, 256, 128))

    def body(dy_ref, r_ref, g_ref, dres_ref, dbr_ref, dg_ref, db_ref):
        i = pl.program_id(0)
        dy_ = dy_ref[...]
        r_ = r_ref[...]
        mu = jnp.mean(r_, axis=-1, keepdims=True)
        xc = r_ - mu
        var = jnp.mean(xc * xc, axis=-1, keepdims=True)
        rstd = lax.rsqrt(var + LN_EPS)
        xh = xc * rstd
        dxh = dy_ * g_ref[...]
        dr = rstd * (dxh - jnp.mean(dxh, axis=-1, keepdims=True) - xh * jnp.mean(dxh * xh, axis=-1, keepdims=True))
        dres_ref[...] = alpha * dr
        dbr_ref[...] = (c * dr).astype(BF16)
        dg_p = jnp.sum(dy_ * xh, axis=0, keepdims=True)
        db_p = jnp.sum(dy_, axis=0, keepdims=True)

        @pl.when(i == 0)
        def _():
            dg_ref[...] = dg_p
            db_ref[...] = db_p

        @pl.when(i > 0)
        def _():
            dg_ref[...] += dg_p
            db_ref[...] += db_p

    row = pl.BlockSpec((tm, n), lambda i: (i, 0))
    vec = pl.BlockSpec((1, n), lambda i: (0, 0))
    return pl.pallas_call(
        body, name=name, grid=(m // tm,),
        in_specs=[row, row, vec], out_specs=[row, row, vec, vec],
        out_shape=[jax.ShapeDtypeStruct((m, n), F32), jax.ShapeDtypeStruct((m, n), BF16),
                   jax.ShapeDtypeStruct((1, n), F32), jax.ShapeDtypeStruct((1, n), F32)],
        compiler_params=_params(dimension_semantics=("arbitrary",)),
    )(dy, r, g.reshape(1, n))


FFN_TN = D_FF // 2


def _ffn_up_act(x, w13, *, name):
    m, d = x.shape
    tm = _tile(m, (512, 256, 128))
    nj = D_FF // FFN_TN

    def body(x_ref, g_ref, u_ref, o_ref):
        x_ = x_ref[...]
        o_ref[...] = (_silu(_dot(x_, g_ref[...])) * _dot(x_, u_ref[...])).astype(BF16)

    return pl.pallas_call(
        body, name=name, grid=(nj, m // tm),
        in_specs=[pl.BlockSpec((tm, d), lambda j, i: (i, 0)), pl.BlockSpec((d, FFN_TN), lambda j, i: (0, j)),
                  pl.BlockSpec((d, FFN_TN), lambda j, i: (0, j + nj))],
        out_specs=pl.BlockSpec((tm, FFN_TN), lambda j, i: (i, j)),
        out_shape=jax.ShapeDtypeStruct((m, D_FF), BF16),
        compiler_params=_params(dimension_semantics=("parallel", "parallel")),
    )(x, w13, w13)


def _ffn_bwd_mid(x, w13, df, w2, *, name):
    m, d = x.shape
    tm = _tile(m, (512, 256, 128))
    nj = D_FF // FFN_TN

    def body(x_ref, g_ref, u_ref, df_ref, w2_ref, o_ref):
        x_ = x_ref[...]
        g = _dot(x_, g_ref[...])
        u = _dot(x_, u_ref[...])
        da = _dot(df_ref[...], w2_ref[...], _NT)
        o_ref[0] = (da * u * _dsilu(g)).astype(BF16)
        o_ref[1] = (da * _silu(g)).astype(BF16)

    return pl.pallas_call(
        body, name=name, grid=(nj, m // tm),
        in_specs=[pl.BlockSpec((tm, d), lambda j, i: (i, 0)), pl.BlockSpec((d, FFN_TN), lambda j, i: (0, j)),
                  pl.BlockSpec((d, FFN_TN), lambda j, i: (0, j + nj)), pl.BlockSpec((tm, d), lambda j, i: (i, 0)),
                  pl.BlockSpec((FFN_TN, d), lambda j, i: (j, 0))],
        out_specs=pl.BlockSpec((2, tm, FFN_TN), lambda j, i: (0, i, j)),
        out_shape=jax.ShapeDtypeStruct((2, m, D_FF), BF16),
        compiler_params=_params(dimension_semantics=("parallel", "parallel")),
    )(x, w13, w13, df, w2)


def _t5_bucket_table():
    r = np.arange(WINDOW)[:, None]
    j = np.arange(2 * WINDOW)[None, :]
    rel = r + WINDOW - j
    n = np.maximum(rel, 0)
    max_exact = NUM_BUCKETS // 2
    nf = np.maximum(n, 1).astype(np.float32)
    large = max_exact + (np.log(nf / np.float32(max_exact)) / np.float32(math.log(MAX_DISTANCE / max_exact))
                         * np.float32(NUM_BUCKETS - max_exact)).astype(np.int32)
    large = np.minimum(large, NUM_BUCKETS - 1)
    bucket = np.where(n < max_exact, n, large)
    in_band = (rel >= 0) & (rel < WINDOW)
    return bucket.astype(np.int32), in_band


def _bucket_onehot():
    bucket, _ = _t5_bucket_table()
    oh = np.zeros((WINDOW * 2 * WINDOW, 128), np.float32)
    oh[np.arange(oh.shape[0]), bucket.reshape(-1)] = 1.0
    return oh


def _stack_heads(x, g):
    hd = HEAD_DIM_A
    return jnp.concatenate([x[:, (GROUP_A * g + h) * hd:(GROUP_A * g + h + 1) * hd] for h in range(GROUP_A)], axis=0)


def _unstack_heads(x):
    return jnp.concatenate([x[h * WINDOW:(h + 1) * WINDOW] for h in range(GROUP_A)], axis=1)


def _attn_probs(q_ref, kp_ref, ko_ref, vp_ref, vo_ref, bias_ref, sink_ref, first_block):
    hd = HEAD_DIM_A
    groups = range(N_KV_A)
    q = q_ref[...]
    qs = [_stack_heads(q, g) * (hd ** -0.5) for g in groups]
    k2 = [jnp.concatenate([kp_ref[:, g * hd:(g + 1) * hd], ko_ref[:, g * hd:(g + 1) * hd]], axis=0) for g in groups]
    v2 = [jnp.concatenate([vp_ref[:, g * hd:(g + 1) * hd], vo_ref[:, g * hd:(g + 1) * hd]], axis=0) for g in groups]
    rr = lax.broadcasted_iota(jnp.int32, (GROUP_A * WINDOW, 2 * WINDOW), 0) % WINDOW
    jj = lax.broadcasted_iota(jnp.int32, (GROUP_A * WINDOW, 2 * WINDOW), 1)
    rel = rr + WINDOW - jj
    valid = (rel >= 0) & (rel < WINDOW) & (jnp.logical_not(first_block) | (jj >= WINDOW))
    s = [_dot(qs[g], k2[g], _NT) for g in groups]
    s = [jnp.where(valid, s[g] + bias_ref[GROUP_A * g:GROUP_A * (g + 1)].reshape(GROUP_A * WINDOW, 2 * WINDOW), NEG_INF)
         for g in groups]
    sk = [jnp.concatenate([jnp.broadcast_to(sink_ref[0:1, GROUP_A * g + h:GROUP_A * g + h + 1], (WINDOW, 1))
                           for h in range(GROUP_A)], axis=0) for g in groups]
    mx = [jnp.maximum(jnp.max(s[g], axis=-1, keepdims=True), sk[g]) for g in groups]
    p = [jnp.exp(s[g] - mx[g]) for g in groups]
    ps = [jnp.exp(sk[g] - mx[g]) for g in groups]
    den = [jnp.sum(p[g], axis=-1, keepdims=True) + ps[g] for g in groups]
    return qs, k2, v2, [p[g] / den[g] for g in groups], [ps[g] / den[g] for g in groups]


def _attn_specs(nb):
    def prev(b, i):
        return (b * nb + jnp.maximum(i - 1, 0))

    q_spec = pl.BlockSpec((WINDOW, Q_A), lambda b, i: (b * nb + i, HC_Q // Q_A))
    kp_spec = pl.BlockSpec((WINDOW, KV_W), lambda b, i: (prev(b, i), HC_K // KV_W))
    ko_spec = pl.BlockSpec((WINDOW, KV_W), lambda b, i: (b * nb + i, HC_K // KV_W))
    vp_spec = pl.BlockSpec((WINDOW, KV_W), lambda b, i: (prev(b, i), HC_V // KV_W))
    vo_spec = pl.BlockSpec((WINDOW, KV_W), lambda b, i: (b * nb + i, HC_V // KV_W))
    bias_spec = pl.BlockSpec((N_HEADS_A, WINDOW, 2 * WINDOW), lambda b, i: (0, 0, 0))
    sink_spec = pl.BlockSpec((1, N_HEADS_A), lambda b, i: (0, 0))
    return [q_spec, kp_spec, ko_spec, vp_spec, vo_spec, bias_spec, sink_spec]


def _attn_fwd(hcat, bias, sink, nbatch, *, name):
    t = hcat.shape[0]
    nb = t // nbatch // WINDOW

    def body(q_ref, kp_ref, ko_ref, vp_ref, vo_ref, bias_ref, sink_ref, o_ref):
        first = pl.program_id(1) == 0
        _, _, v2, p, _ = _attn_probs(q_ref, kp_ref, ko_ref, vp_ref, vo_ref, bias_ref, sink_ref, first)
        o = [_dot(p[g], v2[g]) for g in range(N_KV_A)]
        o_ref[...] = jnp.concatenate([_unstack_heads(og) for og in o], axis=1).astype(BF16)

    return pl.pallas_call(
        body, name=name, grid=(nbatch, nb),
        in_specs=_attn_specs(nb),
        out_specs=pl.BlockSpec((WINDOW, Q_A), lambda b, i: (b * nb + i, 0)),
        out_shape=jax.ShapeDtypeStruct((t, Q_A), BF16),
        compiler_params=_params(dimension_semantics=("parallel", "arbitrary")),
    )(hcat, hcat, hcat, hcat, hcat, bias, sink.reshape(1, N_HEADS_A))


def _attn_bwd(hcat, bias, sink, do, nbatch, *, name):
    t = hcat.shape[0]
    nb = t // nbatch // WINDOW
    hd = HEAD_DIM_A

    def body(q_ref, kp_ref, ko_ref, vp_ref, vo_ref, bias_ref, sink_ref, do_ref,
             dq_ref, dk_ref, dv_ref, dbias_ref, dsink_ref, ck_ref, cv_ref):
        b = pl.program_id(0)
        j = pl.program_id(1)
        first = j == nb - 1

        @pl.when((b == 0) & (j == 0))
        def _():
            dbias_ref[...] = jnp.zeros_like(dbias_ref)
            dsink_ref[...] = jnp.zeros_like(dsink_ref)

        @pl.when(j == 0)
        def _():
            ck_ref[...] = jnp.zeros_like(ck_ref)
            cv_ref[...] = jnp.zeros_like(cv_ref)

        do_ = do_ref[...]
        groups = range(N_KV_A)
        lane = lax.broadcasted_iota(jnp.int32, (1, N_HEADS_A), 1)
        qs, k2, v2, p, ps = _attn_probs(q_ref, kp_ref, ko_ref, vp_ref, vo_ref, bias_ref, sink_ref, first)
        dos = [_stack_heads(do_, g) for g in groups]
        dv2 = [_dot(p[g], dos[g], _TN) for g in groups]
        dp = [_dot(dos[g], v2[g], _NT) for g in groups]
        delta = [jnp.sum(p[g] * dp[g], axis=-1, keepdims=True) for g in groups]
        ds = [p[g] * (dp[g] - delta[g]) for g in groups]
        dqs = [_dot(ds[g], k2[g]) * (hd ** -0.5) for g in groups]
        dk2 = [_dot(ds[g], qs[g], _TN) for g in groups]
        dsink = jnp.zeros((1, N_HEADS_A), F32)
        for g in groups:
            dsk = -(ps[g] * delta[g])
            for h in range(GROUP_A):
                tot = jnp.sum(dsk[h * WINDOW:(h + 1) * WINDOW], axis=0, keepdims=True)
                dsink = dsink + jnp.where(lane == GROUP_A * g + h, tot, 0.0)
            dbias_ref[GROUP_A * g:GROUP_A * (g + 1)] += ds[g].reshape(GROUP_A, WINDOW, 2 * WINDOW)
        dq_ref[...] = jnp.concatenate([_unstack_heads(d) for d in dqs], axis=1).astype(BF16)
        dk_ref[...] = (jnp.concatenate([d[WINDOW:] for d in dk2], axis=1) + ck_ref[...]).astype(BF16)
        dv_ref[...] = (jnp.concatenate([d[WINDOW:] for d in dv2], axis=1) + cv_ref[...]).astype(BF16)
        ck_ref[...] = jnp.concatenate([d[:WINDOW] for d in dk2], axis=1)
        cv_ref[...] = jnp.concatenate([d[:WINDOW] for d in dv2], axis=1)
        dsink_ref[...] += dsink

    def rev(spec):
        return pl.BlockSpec(spec.block_shape, lambda b, j, f=spec.index_map: f(b, nb - 1 - j))

    in_specs = [rev(s) for s in _attn_specs(nb)[:5]] + _attn_specs(nb)[5:]
    in_specs.append(pl.BlockSpec((WINDOW, Q_A), lambda b, j: (b * nb + nb - 1 - j, 0)))
    return pl.pallas_call(
        body, name=name, grid=(nbatch, nb),
        in_specs=in_specs,
        out_specs=[pl.BlockSpec((WINDOW, Q_A), lambda b, j: (b * nb + nb - 1 - j, 0)),
                   pl.BlockSpec((WINDOW, KV_W), lambda b, j: (b * nb + nb - 1 - j, 0)),
                   pl.BlockSpec((WINDOW, KV_W), lambda b, j: (b * nb + nb - 1 - j, 0)),
                   pl.BlockSpec((N_HEADS_A, WINDOW, 2 * WINDOW), lambda b, j: (0, 0, 0)),
                   pl.BlockSpec((1, N_HEADS_A), lambda b, j: (0, 0))],
        out_shape=[jax.ShapeDtypeStruct((t, Q_A), BF16), jax.ShapeDtypeStruct((t, KV_W), BF16),
                   jax.ShapeDtypeStruct((t, KV_W), BF16),
                   jax.ShapeDtypeStruct((N_HEADS_A, WINDOW, 2 * WINDOW), F32),
                   jax.ShapeDtypeStruct((1, N_HEADS_A), F32)],
        scratch_shapes=[pltpu.VMEM((WINDOW, KV_W), F32), pltpu.VMEM((WINDOW, KV_W), F32)],
        compiler_params=_params(dimension_semantics=("arbitrary", "arbitrary")),
    )(hcat, hcat, hcat, hcat, hcat, bias, sink.reshape(1, N_HEADS_A), do)


def _shift_down(x, halo8, s):
    if s == 0:
        return x
    rolled = pltpu.roll(x, s, axis=0)
    row8 = lax.broadcasted_iota(jnp.int32, halo8.shape, 0)
    top = jnp.where(row8 < s, pltpu.roll(halo8, s, axis=0), rolled[0:8])
    return jnp.concatenate([top, rolled[8:]], axis=0)


def _shift_up(x, halo8, s):
    if s == 0:
        return x
    n = x.shape[0]
    rolled = pltpu.roll(x, n - s, axis=0)
    row8 = lax.broadcasted_iota(jnp.int32, halo8.shape, 0)
    bottom = jnp.where(row8 >= 8 - s, pltpu.roll(halo8, 8 - s, axis=0), rolled[n - 8:n])
    return jnp.concatenate([rolled[:n - 8], bottom], axis=0)


def _l2n(x, scale):
    r = lax.rsqrt(jnp.sum(x * x, axis=-1, keepdims=True) + NORM_EPS)
    return x * (r * scale)


def _conv_prep(hcat, conv_w, nbatch, *, name):
    t = hcat.shape[0]
    nt = t // nbatch // ROW_T
    cb = HC_CONV // CONV_CH

    def body(u_ref, halo_ref, w_ref, c_ref, q_ref, k_ref, v_ref):
        i = pl.program_id(1)
        u = u_ref[...]
        halo = jnp.where(i == 0, 0.0, halo_ref[...])
        c = jnp.zeros_like(u)
        for j in range(CONV_K):
            c = c + w_ref[j:j + 1, :] * _shift_down(u, halo, CONV_K - 1 - j)
        c_ref[...] = c
        s = _silu(c)
        for h in range(N_HEADS_B):
            lo, hi = h * KEY_DIM_B, (h + 1) * KEY_DIM_B
            q_ref[:, lo:hi] = _l2n(s[:, lo:hi], KEY_DIM_B ** -0.5)
            k_ref[:, lo:hi] = _l2n(s[:, QK_B + lo:QK_B + hi], 1.0)
        v_ref[...] = s[:, 2 * QK_B:]

    row = lambda w: pl.BlockSpec((ROW_T, w), lambda b, i: (b * nt + i, 0))
    return pl.pallas_call(
        body, name=name, grid=(nbatch, nt),
        in_specs=[pl.BlockSpec((ROW_T, CONV_CH), lambda b, i: (b * nt + i, cb)),
                  pl.BlockSpec((8, CONV_CH), lambda b, i: (jnp.maximum((b * nt + i) * (ROW_T // 8) - 1, 0), cb)),
                  pl.BlockSpec((CONV_K, CONV_CH), lambda b, i: (0, 0))],
        out_specs=[row(CONV_CH), row(QK_B), row(QK_B), row(V_B)],
        out_shape=[jax.ShapeDtypeStruct((t, CONV_CH), F32)] + [jax.ShapeDtypeStruct((t, QK_B), F32)] * 3,
        compiler_params=_params(dimension_semantics=("parallel", "parallel")),
    )(hcat, hcat, conv_w)


def _conv_prep_bwd_pointwise(dq, dk, dv, c, *, name):
    t = c.shape[0]

    def l2n_bwd(x, dy, scale):
        r = lax.rsqrt(jnp.sum(x * x, axis=-1, keepdims=True) + NORM_EPS)
        return scale * (r * dy - x * (r * r * r) * jnp.sum(x * dy, axis=-1, keepdims=True))

    def body(dq_ref, dk_ref, dv_ref, c_ref, dc_ref):
        c_ = c_ref[...]
        s = _silu(c_)
        ds = _dsilu(c_)
        for h in range(N_HEADS_B):
            lo, hi = h * KEY_DIM_B, (h + 1) * KEY_DIM_B
            dc_ref[:, lo:hi] = l2n_bwd(s[:, lo:hi], dq_ref[:, lo:hi], KEY_DIM_B ** -0.5) * ds[:, lo:hi]
            dc_ref[:, QK_B + lo:QK_B + hi] = (l2n_bwd(s[:, QK_B + lo:QK_B + hi], dk_ref[:, lo:hi], 1.0)
                                              * ds[:, QK_B + lo:QK_B + hi])
        dc_ref[:, 2 * QK_B:] = dv_ref[...] * ds[:, 2 * QK_B:]

    row = lambda w: pl.BlockSpec((ROW_T, w), lambda i: (i, 0))
    return pl.pallas_call(
        body, name=name, grid=(t // ROW_T,),
        in_specs=[row(QK_B), row(QK_B), row(V_B), row(CONV_CH)], out_specs=row(CONV_CH),
        out_shape=jax.ShapeDtypeStruct((t, CONV_CH), F32),
        compiler_params=_params(dimension_semantics=("parallel",)),
    )(dq, dk, dv, c)


def _conv_bwd(dc, hcat, conv_w, nbatch, *, name):
    t = dc.shape[0]
    nt = t // nbatch // ROW_T
    cb = HC_CONV // CONV_CH
    last_blk = t // 8 - 1

    def body(dc_ref, dnext_ref, u_ref, uprev_ref, w_ref, du_ref, dw_ref):
        b = pl.program_id(0)
        i = pl.program_id(1)
        dc_ = dc_ref[...]
        u = u_ref[...]
        dnext = jnp.where(i == nt - 1, 0.0, dnext_ref[...])
        uprev = jnp.where(i == 0, 0.0, uprev_ref[...])
        du = jnp.zeros_like(dc_)
        rows = []
        for j in range(CONV_K):
            s = CONV_K - 1 - j
            du = du + w_ref[j:j + 1, :] * _shift_up(dc_, dnext, s)
            rows.append(jnp.sum(dc_ * _shift_down(u, uprev, s), axis=0, keepdims=True))
        du_ref[...] = du.astype(BF16)
        dw_p = jnp.concatenate(rows + [jnp.zeros((8 - CONV_K, CONV_CH), F32)], axis=0)

        @pl.when((b == 0) & (i == 0))
        def _():
            dw_ref[...] = dw_p

        @pl.when((b > 0) | (i > 0))
        def _():
            dw_ref[...] += dw_p

    return pl.pallas_call(
        body, name=name, grid=(nbatch, nt),
        in_specs=[pl.BlockSpec((ROW_T, CONV_CH), lambda b, i: (b * nt + i, 0)),
                  pl.BlockSpec((8, CONV_CH), lambda b, i: (jnp.minimum((b * nt + i + 1) * (ROW_T // 8), last_blk), 0)),
                  pl.BlockSpec((ROW_T, CONV_CH), lambda b, i: (b * nt + i, cb)),
                  pl.BlockSpec((8, CONV_CH), lambda b, i: (jnp.maximum((b * nt + i) * (ROW_T // 8) - 1, 0), cb)),
                  pl.BlockSpec((CONV_K, CONV_CH), lambda b, i: (0, 0))],
        out_specs=[pl.BlockSpec((ROW_T, CONV_CH), lambda b, i: (b * nt + i, 0)),
                   pl.BlockSpec((8, CONV_CH), lambda b, i: (0, 0))],
        out_shape=[jax.ShapeDtypeStruct((t, CONV_CH), BF16), jax.ShapeDtypeStruct((8, CONV_CH), F32)],
        compiler_params=_params(dimension_semantics=("arbitrary", "arbitrary")),
    )(dc, dc, hcat, hcat, conv_w)


def _softplus(x):
    return jnp.maximum(x, 0.0) + jnp.log(1.0 + jnp.exp(-jnp.abs(x)))


def _gates(hcat, a_row, dt_row, *, name):
    t = hcat.shape[0]

    def body(bd_ref, a_ref, dt_ref, gb_ref, bb_ref):
        bd = bd_ref[...]
        beta = _sigmoid(bd)
        g = -jnp.exp(a_ref[...]) * _softplus(bd + dt_ref[...])
        for h in range(N_HEADS_B):
            lo, hi = h * VAL_DIM_B, (h + 1) * VAL_DIM_B
            bb_ref[:, lo:hi] = jnp.broadcast_to(beta[:, h:h + 1], (ROW_T, VAL_DIM_B))
            gb_ref[:, lo:hi] = jnp.broadcast_to(g[:, N_HEADS_B + h:N_HEADS_B + h + 1], (ROW_T, VAL_DIM_B))

    vec = pl.BlockSpec((1, 128), lambda i: (0, 0))
    row = pl.BlockSpec((ROW_T, V_B), lambda i: (i, 0))
    return pl.pallas_call(
        body, name=name, grid=(t // ROW_T,),
        in_specs=[pl.BlockSpec((ROW_T, 128), lambda i: (i, HC_BD // 128)), vec, vec],
        out_specs=[row, row], out_shape=[jax.ShapeDtypeStruct((t, V_B), F32)] * 2,
        compiler_params=_params(dimension_semantics=("parallel",)),
    )(hcat, a_row, dt_row)


def _gates_bwd(dgb, dbb, hcat, a_row, dt_row, *, name):
    t = hcat.shape[0]

    def body(dgb_ref, dbb_ref, bd_ref, a_ref, dt_ref, dbd_ref, da_ref, ddt_ref):
        i = pl.program_id(0)
        bd = bd_ref[...]
        beta = _sigmoid(bd)
        ea = jnp.exp(a_ref[...])
        x = bd + dt_ref[...]
        g = -ea * _softplus(x)
        lane = lax.broadcasted_iota(jnp.int32, (ROW_T, 128), 1)
        dbeta = jnp.zeros((ROW_T, 128), F32)
        dg = jnp.zeros((ROW_T, 128), F32)
        for h in range(N_HEADS_B):
            lo, hi = h * VAL_DIM_B, (h + 1) * VAL_DIM_B
            dbeta = dbeta + jnp.where(lane == h, jnp.sum(dbb_ref[:, lo:hi], axis=-1, keepdims=True), 0.0)
            dg = dg + jnp.where(lane == N_HEADS_B + h, jnp.sum(dgb_ref[:, lo:hi], axis=-1, keepdims=True), 0.0)
        ddt_raw = dg * (-ea) * _sigmoid(x)
        dbd_ref[...] = (dbeta * beta * (1.0 - beta) + ddt_raw).astype(BF16)
        da_p = jnp.sum(dg * g, axis=0, keepdims=True)
        ddt_p = jnp.sum(ddt_raw, axis=0, keepdims=True)

        @pl.when(i == 0)
        def _():
            da_ref[...] = da_p
            ddt_ref[...] = ddt_p

        @pl.when(i > 0)
        def _():
            da_ref[...] += da_p
            ddt_ref[...] += ddt_p

    vec = pl.BlockSpec((1, 128), lambda i: (0, 0))
    row = pl.BlockSpec((ROW_T, V_B), lambda i: (i, 0))
    return pl.pallas_call(
        body, name=name, grid=(t // ROW_T,),
        in_specs=[row, row, pl.BlockSpec((ROW_T, 128), lambda i: (i, HC_BD // 128)), vec, vec],
        out_specs=[pl.BlockSpec((ROW_T, 128), lambda i: (i, 0)), vec, vec],
        out_shape=[jax.ShapeDtypeStruct((t, 128), BF16), jax.ShapeDtypeStruct((1, 128), F32),
                   jax.ShapeDtypeStruct((1, 128), F32)],
        compiler_params=_params(dimension_semantics=("arbitrary",)),
    )(dgb, dbb, hcat, a_row, dt_row)


def _group_masks():
    r = lax.broadcasted_iota(jnp.int32, (GROUP_T, GROUP_T), 0)
    c = lax.broadcasted_iota(jnp.int32, (GROUP_T, GROUP_T), 1)
    same = (r // CHUNK) == (c // CHUNK)
    return same, same & (r >= c), same & (r > c)


def _split2(a):
    hi = a.astype(MXU_DTYPE)
    return hi, (a - hi.astype(F32)).astype(MXU_DTYPE)


def _dot3(a2, b2, dims=_NN):
    (ah, al), (bh, bl) = a2, b2
    d = functools.partial(lax.dot_general, dimension_numbers=dims, preferred_element_type=F32)
    return d(ah, bh) + (d(ah, bl) + d(al, bh))


def _inv_unit_lower(lows):
    shape = lows[0].shape
    eye = (lax.broadcasted_iota(jnp.int32, shape, 0) == lax.broadcasted_iota(jnp.int32, shape, 1)).astype(F32)
    p2 = [_split2(-low) for low in lows]
    ts = [eye - low for low in lows]
    for _ in range(int(math.log2(CHUNK)) - 1):
        p2 = [_split2(_dot3(p, p)) for p in p2]
        ts = [t + _dot3(_split2(t), p) for t, p in zip(ts, p2)]
    return ts


@jax.custom_vjp
def _inv_saved(low, t):
    return t


def _inv_saved_fwd(low, t):
    return t, t


def _inv_saved_bwd(t, dt):
    t2 = _split2(t)
    return -_dot3(t2, _split2(_dot3(_split2(dt), t2, _NT)), _TN), jnp.zeros_like(t)


_inv_saved.defvjp(_inv_saved_fwd, _inv_saved_bwd)


def _mask_dot(mask, x, dims):
    m = mask.astype(MXU_DTYPE)
    hi = x.astype(MXU_DTYPE)
    r1 = x - hi.astype(F32)
    mid = r1.astype(MXU_DTYPE)
    lo = (r1 - mid.astype(F32)).astype(MXU_DTYPE)
    d = functools.partial(lax.dot_general, dimension_numbers=dims, preferred_element_type=F32)
    return d(m, hi) + (d(m, mid) + d(m, lo))


@jax.custom_vjp
def _chunk_sums(gb):
    same, causal, _ = _group_masks()
    return _mask_dot(causal, gb, _NN), _mask_dot(same, gb, _NN)


def _chunk_sums_fwd(gb):
    return _chunk_sums(gb), None


def _chunk_sums_bwd(_, cot):
    same, causal, _ = _group_masks()
    return (_mask_dot(causal, cot[0], _TN) + _mask_dot(same, cot[1], _TN),)


_chunk_sums.defvjp(_chunk_sums_fwd, _chunk_sums_bwd)


def _fold_blocks(m):
    return m[:, 0:CHUNK] + m[:, CHUNK:2 * CHUNK] + m[:, 2 * CHUNK:3 * CHUNK] + m[:, 3 * CHUNK:4 * CHUNK]


def _dn_prep_heads(q, k, v, gb, bb, tsaved=None):
    same, causal, strict = _group_masks()
    heads = range(len(q))
    sums = [_chunk_sums(gb[h]) for h in heads]
    gc = [s[0] for s in sums]
    glast = [s[1] for s in sums]
    decay = [jnp.exp(jnp.where(causal, gc[h][:, 0:1] - gc[h].T[0:1, :], NEG_INF)) for h in heads]
    kb = [k[h] * bb[h] for h in heads]
    vb = [v[h] * bb[h] for h in heads]
    lower = [jnp.where(strict, _dot(kb[h], k[h], _NT) * decay[h], 0.0) for h in heads]
    if tsaved is None:
        tinv = _inv_unit_lower(lower)
    else:
        tinv = [_inv_saved(lower[h], jnp.where(same, jnp.concatenate([tsaved[h]] * (GROUP_T // CHUNK), axis=1), 0.0))
                for h in heads]
    egc = [jnp.exp(gc[h]) for h in heads]
    u = [_dot(tinv[h], vb[h]) for h in heads]
    w = [_dot(tinv[h], kb[h] * egc[h]) for h in heads]
    a = [_fold_blocks(jnp.where(causal, _dot(q[h], k[h], _NT) * decay[h], 0.0)) for h in heads]
    k_tail = [k[h] * jnp.exp(glast[h] - gc[h]) for h in heads]
    q_dec = [q[h] * egc[h] for h in heads]
    return u, w, q_dec, k_tail, a, glast, [_fold_blocks(t) for t in tinv]


def _head_slices(ref, width):
    return [ref[:, h * width:(h + 1) * width] for h in range(N_HEADS_B)]


def _store_heads(ref, vals, width):
    for h, val in enumerate(vals):
        ref[:, h * width:(h + 1) * width] = val


def _dn_prep(q, k, v, gb, bb, *, side=None, name):
    t = q.shape[0]
    ns = 0 if side is None else len(side)
    steps = t // GROUP_T

    def body(*refs):
        q_ref, k_ref, v_ref, gb_ref, bb_ref = refs[:5]
        u_ref, w_ref, qd_ref, kt_ref, a_ref, gl_ref, ti_ref = refs[5 + ns:12 + ns]
        if ns:
            copies = _gather_copies(refs[5:5 + ns], refs[12 + ns:12 + 2 * ns], *refs[12 + 2 * ns:])

            @pl.when(pl.program_id(0) == 0)
            def _():
                for cp in copies:
                    cp.start()

        outs = _dn_prep_heads(*[_head_slices(r, KEY_DIM_B) for r in (q_ref, k_ref, v_ref, gb_ref, bb_ref)])
        for ref, vals in zip((u_ref, w_ref, qd_ref, kt_ref, a_ref, gl_ref, ti_ref), outs):
            _store_heads(ref, vals, vals[0].shape[1])

        if ns:
            @pl.when(pl.program_id(0) == steps - 1)
            def _():
                for cp in copies:
                    cp.wait()

    row = pl.BlockSpec((GROUP_T, V_B), lambda i: (i, 0))
    arow = pl.BlockSpec((GROUP_T, N_HEADS_B * CHUNK), lambda i: (i, 0))
    big = jax.ShapeDtypeStruct((t, V_B), F32)
    small = jax.ShapeDtypeStruct((t, N_HEADS_B * CHUNK), F32)
    side = [] if side is None else list(side)
    outs = pl.pallas_call(
        body, name=name, grid=(steps,),
        in_specs=[row] * 5 + [HBM_SPEC] * ns, out_specs=[row, row, row, row, arow, row, arow] + [HBM_SPEC] * ns,
        out_shape=[big, big, big, big, small, big, small]
        + [jax.ShapeDtypeStruct((N_CHIPS,) + s.shape, s.dtype) for s in side],
        scratch_shapes=_gather_sems(ns) if ns else [],
        compiler_params=_params(dimension_semantics=("arbitrary",)),
    )(q, k, v, gb, bb, *side)
    return tuple(outs[:7]), (list(outs[7:]) if ns else None)


def _dn_prep_bwd(q, k, v, gb, bb, ti, du, dw, dqd, dkt, da, dgl, *, name):
    t = q.shape[0]

    def body(q_ref, k_ref, v_ref, gb_ref, bb_ref, ti_ref, du_ref, dw_ref, dqd_ref, dkt_ref, da_ref, dgl_ref,
             dq_ref, dk_ref, dv_ref, dgb_ref, dbb_ref):
        for lo in range(0, N_HEADS_B, DN_BWD_HEADS):
            grp = slice(lo, lo + DN_BWD_HEADS)
            tsaved = _head_slices(ti_ref, CHUNK)[grp]
            _, vjp = jax.vjp(lambda *a, ts=tsaved: _dn_prep_heads(*a, tsaved=ts)[:6],
                             *[_head_slices(r, KEY_DIM_B)[grp] for r in (q_ref, k_ref, v_ref, gb_ref, bb_ref)])
            cot = tuple(_head_slices(r, CHUNK if r is da_ref else KEY_DIM_B)[grp]
                        for r in (du_ref, dw_ref, dqd_ref, dkt_ref, da_ref, dgl_ref))
            for ref, vals in zip((dq_ref, dk_ref, dv_ref, dgb_ref, dbb_ref), vjp(cot)):
                for h, val in enumerate(vals):
                    ref[:, (lo + h) * KEY_DIM_B:(lo + h + 1) * KEY_DIM_B] = val

    row = pl.BlockSpec((GROUP_T, V_B), lambda i: (i, 0))
    arow = pl.BlockSpec((GROUP_T, N_HEADS_B * CHUNK), lambda i: (i, 0))
    big = jax.ShapeDtypeStruct((t, V_B), F32)
    return pl.pallas_call(
        body, name=name, grid=(t // GROUP_T,),
        in_specs=[row] * 5 + [arow] + [row] * 4 + [arow, row], out_specs=[row] * 5, out_shape=[big] * 5,
        compiler_params=_params(dimension_semantics=("parallel",)),
    )(q, k, v, gb, bb, ti, du, dw, dqd, dkt, da, dgl)


def _dn_steps(s, qd, kt, u, w, a, gl):
    heads = range(len(s))
    v_new = [u[h] - _dot(w[h], s[h]) for h in heads]
    qs = [_dot(qd[h], s[h]) for h in heads]
    o = [qs[h] + _dot(a[h], v_new[h]) for h in heads]
    s_new = [s[h] * jnp.exp(gl[h][0:1, :]) + _dot(kt[h], v_new[h], _TN) for h in heads]
    return s_new, o


def _dn_scan(u, w, qd, kt, a, gl, nbatch, *, name):
    t = u.shape[0]
    ng = t // nbatch // GROUP_T
    cpg = GROUP_T // CHUNK

    def body(u_ref, w_ref, qd_ref, kt_ref, a_ref, gl_ref, o_ref, ss_ref, s_ref):
        @pl.when(pl.program_id(1) == 0)
        def _():
            s_ref[...] = jnp.zeros_like(s_ref)

        def chunk(c, carry):
            rows = pl.ds(pl.multiple_of(c * CHUNK, CHUNK), CHUNK)
            heads = range(N_HEADS_B)
            s = [s_ref[h] for h in heads]
            for h in heads:
                ss_ref[c, h] = s[h]
            s_new, o = _dn_steps(s, *[[r[rows, h * wd:(h + 1) * wd] for h in heads] for r, wd in
                                      ((qd_ref, KEY_DIM_B), (kt_ref, KEY_DIM_B), (u_ref, VAL_DIM_B), (w_ref, KEY_DIM_B),
                                       (a_ref, CHUNK), (gl_ref, VAL_DIM_B))])
            for h in heads:
                s_ref[h] = s_new[h]
                o_ref[rows, h * VAL_DIM_B:(h + 1) * VAL_DIM_B] = o[h]
            return carry

        lax.fori_loop(0, cpg, chunk, 0)

    row = pl.BlockSpec((GROUP_T, V_B), lambda b, i: (b * ng + i, 0))
    arow = pl.BlockSpec((GROUP_T, N_HEADS_B * CHUNK), lambda b, i: (b * ng + i, 0))
    return pl.pallas_call(
        body, name=name, grid=(nbatch, ng),
        in_specs=[row, row, row, row, arow, row],
        out_specs=[row, pl.BlockSpec((cpg, N_HEADS_B, KEY_DIM_B, VAL_DIM_B), lambda b, i: (b * ng + i, 0, 0, 0))],
        out_shape=[jax.ShapeDtypeStruct((t, V_B), F32),
                   jax.ShapeDtypeStruct((t // CHUNK, N_HEADS_B, KEY_DIM_B, VAL_DIM_B), F32)],
        scratch_shapes=[pltpu.VMEM((N_HEADS_B, KEY_DIM_B, VAL_DIM_B), F32)],
        compiler_params=_params(dimension_semantics=("parallel", "arbitrary")),
    )(u, w, qd, kt, a, gl)


def _dn_scan_bwd(u, w, qd, kt, a, gl, ss, do, nbatch, *, name):
    t = u.shape[0]
    ng = t // nbatch // GROUP_T
    cpg = GROUP_T // CHUNK

    def body(u_ref, w_ref, qd_ref, kt_ref, a_ref, gl_ref, ss_ref, do_ref,
             du_ref, dw_ref, dqd_ref, dkt_ref, da_ref, dgl_ref, ds_ref):
        @pl.when(pl.program_id(1) == 0)
        def _():
            ds_ref[...] = jnp.zeros_like(ds_ref)

        def chunk(cc, carry):
            c = cpg - 1 - cc
            rows = pl.ds(pl.multiple_of(c * CHUNK, CHUNK), CHUNK)
            heads = range(N_HEADS_B)
            ins = ((qd_ref, KEY_DIM_B), (kt_ref, KEY_DIM_B), (u_ref, VAL_DIM_B), (w_ref, KEY_DIM_B), (a_ref, CHUNK),
                   (gl_ref, VAL_DIM_B))
            _, vjp = jax.vjp(_dn_steps, [ss_ref[c, h] for h in heads],
                             *[[r[rows, h * wd:(h + 1) * wd] for h in heads] for r, wd in ins])
            grads = vjp(([ds_ref[h] for h in heads], [do_ref[rows, h * VAL_DIM_B:(h + 1) * VAL_DIM_B] for h in heads]))
            for h in heads:
                ds_ref[h] = grads[0][h]
            outs = ((dqd_ref, KEY_DIM_B), (dkt_ref, KEY_DIM_B), (du_ref, VAL_DIM_B), (dw_ref, KEY_DIM_B), (da_ref, CHUNK),
                    (dgl_ref, VAL_DIM_B))
            for (r, wd), vals in zip(outs, grads[1:]):
                for h in heads:
                    r[rows, h * wd:(h + 1) * wd] = vals[h]
            return carry

        lax.fori_loop(0, cpg, chunk, 0)

    row = pl.BlockSpec((GROUP_T, V_B), lambda b, j: (b * ng + ng - 1 - j, 0))
    arow = pl.BlockSpec((GROUP_T, N_HEADS_B * CHUNK), lambda b, j: (b * ng + ng - 1 - j, 0))
    big = jax.ShapeDtypeStruct((t, V_B), F32)
    return pl.pallas_call(
        body, name=name, grid=(nbatch, ng),
        in_specs=[row, row, row, row, arow, row,
                  pl.BlockSpec((cpg, N_HEADS_B, KEY_DIM_B, VAL_DIM_B), lambda b, j: (b * ng + ng - 1 - j, 0, 0, 0)), row],
        out_specs=[row, row, row, row, arow, row],
        out_shape=[big, big, big, big, jax.ShapeDtypeStruct((t, N_HEADS_B * CHUNK), F32), big],
        scratch_shapes=[pltpu.VMEM((N_HEADS_B, KEY_DIM_B, VAL_DIM_B), F32)],
        compiler_params=_params(dimension_semantics=("parallel", "arbitrary")),
    )(u, w, qd, kt, a, gl, ss, do)


def _rms_gate(o, hcat, dn_g, *, name):
    t = o.shape[0]

    def body(o_ref, z_ref, g_ref, y_ref):
        for h in range(N_HEADS_B):
            sl = slice(h * VAL_DIM_B, (h + 1) * VAL_DIM_B)
            o_ = o_ref[:, sl]
            r = lax.rsqrt(jnp.mean(o_ * o_, axis=-1, keepdims=True) + NORM_EPS)
            y_ref[:, sl] = (o_ * r * g_ref[...] * _silu(z_ref[:, sl])).astype(BF16)

    row = pl.BlockSpec((ROW_T, V_B), lambda i: (i, 0))
    return pl.pallas_call(
        body, name=name, grid=(t // ROW_T,),
        in_specs=[row, pl.BlockSpec((ROW_T, V_B), lambda i: (i, HC_Z // V_B)), pl.BlockSpec((1, VAL_DIM_B), lambda i: (0, 0))],
        out_specs=row, out_shape=jax.ShapeDtypeStruct((t, V_B), BF16),
        compiler_params=_params(dimension_semantics=("parallel",)),
    )(o, hcat, dn_g.reshape(1, VAL_DIM_B))


def _rms_gate_bwd(dy, o, hcat, dn_g, *, name):
    t = o.shape[0]

    def body(dy_ref, o_ref, z_ref, g_ref, do_ref, dz_ref, dg_ref):
        i = pl.program_id(0)
        g = g_ref[...]
        dg_p = jnp.zeros((1, VAL_DIM_B), F32)
        for h in range(N_HEADS_B):
            sl = slice(h * VAL_DIM_B, (h + 1) * VAL_DIM_B)
            o_ = o_ref[:, sl]
            z_ = z_ref[:, sl]
            dy_ = dy_ref[:, sl]
            r = lax.rsqrt(jnp.mean(o_ * o_, axis=-1, keepdims=True) + NORM_EPS)
            n = o_ * r
            sz = _silu(z_)
            dz_ref[:, sl] = (dy_ * n * g * _dsilu(z_)).astype(BF16)
            dg_p = dg_p + jnp.sum(dy_ * n * sz, axis=0, keepdims=True)
            dn = dy_ * g * sz
            do_ref[:, sl] = r * dn - o_ * (r * r * r) * jnp.mean(o_ * dn, axis=-1, keepdims=True)

        @pl.when(i == 0)
        def _():
            dg_ref[...] = dg_p

        @pl.when(i > 0)
        def _():
            dg_ref[...] += dg_p

    row = pl.BlockSpec((ROW_T, V_B), lambda i: (i, 0))
    vec = pl.BlockSpec((1, VAL_DIM_B), lambda i: (0, 0))
    return pl.pallas_call(
        body, name=name, grid=(t // ROW_T,),
        in_specs=[row, row, pl.BlockSpec((ROW_T, V_B), lambda i: (i, HC_Z // V_B)), vec],
        out_specs=[row, row, vec],
        out_shape=[jax.ShapeDtypeStruct((t, V_B), F32), jax.ShapeDtypeStruct((t, V_B), BF16),
                   jax.ShapeDtypeStruct((1, VAL_DIM_B), F32)],
        compiler_params=_params(dimension_semantics=("arbitrary",)),
    )(dy, o, hcat, dn_g.reshape(1, VAL_DIM_B))


def _merge(ya, yb, hcat, *, name):
    t = ya.shape[0]

    def body(ya_ref, yb_ref, ga_ref, gb_ref, y_ref):
        y_ref[...] = (_sigmoid(ga_ref[...]) * ya_ref[...] + _sigmoid(gb_ref[...]) * yb_ref[...]).astype(BF16)

    row = pl.BlockSpec((ROW_T, D_MODEL), lambda i: (i, 0))
    return pl.pallas_call(
        body, name=name, grid=(t // ROW_T,),
        in_specs=[row, row, pl.BlockSpec((ROW_T, D_MODEL), lambda i: (i, HC_GATE // D_MODEL)),
                  pl.BlockSpec((ROW_T, D_MODEL), lambda i: (i, HC_GATE // D_MODEL + 1))],
        out_specs=row, out_shape=jax.ShapeDtypeStruct((t, D_MODEL), BF16),
        compiler_params=_params(dimension_semantics=("parallel",)),
    )(ya, yb, hcat, hcat)


def _merge_bwd(dmix, ya, yb, hcat, *, name):
    t = ya.shape[0]

    def body(d_ref, ya_ref, yb_ref, ga_ref, gb_ref, dya_ref, dyb_ref, dgate_ref):
        d = d_ref[...]
        sa = _sigmoid(ga_ref[...])
        sb = _sigmoid(gb_ref[...])
        dya_ref[...] = (d * sa).astype(BF16)
        dyb_ref[...] = (d * sb).astype(BF16)
        dgate_ref[:, :D_MODEL] = (d * ya_ref[...] * sa * (1.0 - sa)).astype(BF16)
        dgate_ref[:, D_MODEL:] = (d * yb_ref[...] * sb * (1.0 - sb)).astype(BF16)

    row = pl.BlockSpec((ROW_T, D_MODEL), lambda i: (i, 0))
    return pl.pallas_call(
        body, name=name, grid=(t // ROW_T,),
        in_specs=[row, row, row, pl.BlockSpec((ROW_T, D_MODEL), lambda i: (i, HC_GATE // D_MODEL)),
                  pl.BlockSpec((ROW_T, D_MODEL), lambda i: (i, HC_GATE // D_MODEL + 1))],
        out_specs=[row, row, pl.BlockSpec((ROW_T, 2 * D_MODEL), lambda i: (i, 0))],
        out_shape=[jax.ShapeDtypeStruct((t, D_MODEL), BF16)] * 2 + [jax.ShapeDtypeStruct((t, 2 * D_MODEL), BF16)],
        compiler_params=_params(dimension_semantics=("parallel",)),
    )(dmix, ya, yb, hcat, hcat)


def _loss_head(y, target, *, name):
    t, n = y.shape
    tm = _tile(t, (512, 256, 128))

    def body(y_ref, t_ref, part_ref, dy_ref):
        i = pl.program_id(0)
        e = y_ref[...] - t_ref[...]
        dy_ref[...] = e * (1.0 / n)
        p = jnp.sum((e * e).reshape(tm // 8, 8, n), axis=0) * (0.5 / n)

        @pl.when(i == 0)
        def _():
            part_ref[...] = p

        @pl.when(i > 0)
        def _():
            part_ref[...] += p

    row = pl.BlockSpec((tm, n), lambda i: (i, 0))
    return pl.pallas_call(
        body, name=name, grid=(t // tm,),
        in_specs=[row, row], out_specs=[pl.BlockSpec((8, n), lambda i: (0, 0)), row],
        out_shape=[jax.ShapeDtypeStruct((8, n), F32), jax.ShapeDtypeStruct((t, n), F32)],
        compiler_params=_params(dimension_semantics=("arbitrary",)),
    )(y, target)


def _adamw_math(w, g, m, v):
    nm = ADAM_B1 * m + (1.0 - ADAM_B1) * g
    nv = ADAM_B2 * v + (1.0 - ADAM_B2) * (g * g)
    m_hat = nm / (1.0 - ADAM_B1 ** ADAM_STEP)
    v_hat = nv / (1.0 - ADAM_B2 ** ADAM_STEP)
    return -ADAM_LR * (m_hat / (jnp.sqrt(v_hat) + ADAM_EPS) + ADAM_WD * w), nm, nv


def _adamw(w, g, m, v, *, name):
    shape = w.shape
    cols = shape[-1]
    rows = int(np.prod(shape[:-1]))
    w2, g2, m2, v2 = (a.reshape(rows, cols) for a in (w, g, m, v))
    tr = rows
    if rows * cols > 512 * 1024:
        tr = _tile(rows, tuple(c for c in (512, 256, 128, 64, 32, 16, 8) if c * cols <= 256 * 1024))

    def body(w_ref, g_ref, m_ref, v_ref, d_ref, nm_ref, nv_ref):
        d_ref[...], nm_ref[...], nv_ref[...] = _adamw_math(w_ref[...], g_ref[...], m_ref[...], v_ref[...])

    blk = pl.BlockSpec((tr, cols), lambda i: (i, 0))
    outs = pl.pallas_call(
        body, name=name, grid=(rows // tr,),
        in_specs=[blk] * 4, out_specs=[blk] * 3,
        out_shape=[jax.ShapeDtypeStruct((rows, cols), F32)] * 3,
        compiler_params=_params(dimension_semantics=("parallel",)),
    )(w2, g2, m2, v2)
    return tuple(o.reshape(shape) for o in outs)


def _repack_w_in(w_in):
    d = w_in.shape[0]
    o = 0
    parts = {}
    for nm, wd in (("q", Q_A), ("k", KV_W), ("v", KV_W), ("conv", CONV_CH), ("beta", N_HEADS_B), ("dt", N_HEADS_B),
                   ("z", V_B), ("gate", 2 * D_MODEL)):
        parts[nm] = w_in[:, o:o + wd]
        o += wd
    z = lambda n: jnp.zeros((d, n), w_in.dtype)
    return jnp.concatenate([parts["q"], parts["z"], parts["k"], parts["v"], parts["beta"], parts["dt"],
                            z(128 - 2 * N_HEADS_B), z(HC_CONV - HC_BD - 128), parts["conv"], parts["gate"]], axis=1)


MATRIX_NAMES = ("ffn_w13", "ffn_w2", "w_in", "w_branch_a", "w_branch_b", "w_out")


def _dw_in_by_owner(dw):
    sections = ((Q_A, HC_Q), (2 * KV_W, HC_K), (CONV_CH, HC_CONV), (2 * N_HEADS_B, HC_BD), (V_B, HC_Z),
                (2 * D_MODEL, HC_GATE))
    per = N_IN // 4
    owners = []
    for o in range(4):
        lo, hi, start, parts = o * per, (o + 1) * per, 0, []
        for width, off in sections:
            a, b = max(lo, start), min(hi, start + width)
            if a < b:
                parts.append(dw[:, off + a - start:off + b - start])
            start += width
        owners.append(jnp.concatenate(parts, axis=1))
    return jnp.stack(owners)


def _lane_row(vals):
    return jnp.pad(vals.astype(F32).reshape(1, N_HEADS_B), ((0, 0), (N_HEADS_B, 128 - 2 * N_HEADS_B)))


def _local_step(x, target, rel_bias, layer_wts, side_shards=None, side_assemble=None):
    nbatch, seq, d = x.shape
    t = nbatch * seq
    depth = len(layer_wts)
    layer_wts = list(layer_wts)
    x0 = x.reshape(t, d)
    tgt = target.reshape(t, d)

    onehot = jnp.asarray(_bucket_onehot())
    rel_t = jnp.pad(rel_bias.T, ((0, 0), (0, 128 - NUM_BUCKETS)))
    bias = _mm(rel_t, onehot, tb=True, exact=True, name="pos_bias")
    bias = bias.reshape(N_HEADS_A, WINDOW, 2 * WINDOW)

    saved = []
    xin, xin_b = x0, x0.astype(BF16)
    for i in range(depth):
        L = {}
        W = layer_wts[i]
        tag = f"_l{i}"
        a = _ffn_up_act(xin_b, W["ffn_w13"][0], name="ffn_up_act" + tag + "a")
        r1, x1, x1_b = _mm_res_ln(a, W["ffn_w2"][0], xin, W["ln_g"][0], W["ln_b"][0],
                                  alpha=DN_ALPHA, c=0.5, name="ffn_down_ln" + tag + "a")
        L.update(x0_b=xin_b, a0=a, r1=r1, x1=x1, x1_b=x1_b)
        hcat = _mm(x1_b, W["w_in_p"], name="in_proj" + tag)
        ao = _attn_fwd(hcat, bias, W["sinks"], nbatch, name="swa" + tag)
        ya = _mm(ao, W["w_branch_a"], name="branch_a" + tag)
        c, qn, kn, vs = _conv_prep(hcat, W["conv_w"], nbatch, name="conv_prep" + tag)
        a_row = _lane_row(W["a_log"])
        dt_row = _lane_row(W["dt_bias"])
        gb, bb = _gates(hcat, a_row, dt_row, name="gates" + tag)
        if i + 1 < depth and layer_wts[i + 1] is None:
            (u, w, qd, kt, aa, gl, ti), gathered = _dn_prep(qn, kn, vs, gb, bb, side=side_shards[i + 1],
                                                            name="dn_prep" + tag)
            layer_wts[i + 1] = side_assemble(i + 1, gathered)
        else:
            (u, w, qd, kt, aa, gl, ti), _ = _dn_prep(qn, kn, vs, gb, bb, name="dn_prep" + tag)
        o, ss = _dn_scan(u, w, qd, kt, aa, gl, nbatch, name="dn_scan" + tag)
        on = _rms_gate(o, hcat, W["dn_norm_g"], name="rms_gate" + tag)
        yb = _mm(on, W["w_branch_b"], name="branch_b" + tag)
        mix = _merge(ya, yb, hcat, name="merge" + tag)
        r2, x2, x2_b = _mm_res_ln(mix, W["w_out"], x1, W["ln_g"][1], W["ln_b"][1],
                                  alpha=DN_ALPHA, c=1.0, name="out_proj_ln" + tag)
        L.update(hcat=hcat, ao=ao, ya=ya, c=c, qn=qn, kn=kn, vs=vs, gb=gb, bb=bb, a_row=a_row, dt_row=dt_row,
                 u=u, w=w, qd=qd, kt=kt, aa=aa, gl=gl, ti=ti, o=o, ss=ss, on=on, yb=yb, mix=mix, r2=r2, x2_b=x2_b)
        a = _ffn_up_act(x2_b, W["ffn_w13"][1], name="ffn_up_act" + tag + "b")
        r3, x3, x3_b = _mm_res_ln(a, W["ffn_w2"][1], x2, W["ln_g"][2], W["ln_b"][2],
                                  alpha=DN_ALPHA, c=0.5, name="ffn_down_ln" + tag + "b")
        L.update(a1=a, r3=r3)
        saved.append(L)
        xin, xin_b = x3, x3_b

    part, dy = _loss_head(xin, tgt, name="loss_head")
    loss = jnp.sum(part)

    grads = {k: [None] * depth for k in ("ln_g", "ln_b", "ffn_w13", "ffn_w2", "w_in", "conv_w", "a_log", "dt_bias",
                                          "dn_norm_g", "sinks", "w_branch_a", "w_branch_b", "w_out")}
    dbias_total = None
    for i in reversed(range(depth)):
        L = saved[i]
        W = layer_wts[i]
        tag = f"_l{i}"
        dln_g, dln_b, dw13, dw2 = [None] * 3, [None] * 3, [None] * 2, [None] * 2

        def ffn_bwd(dyo, r, xprev_b, asave, j, sfx):
            dres, df, dln_g[2 * j], dln_b[2 * j] = _ln_bwd(dyo, r, W["ln_g"][2 * j], alpha=DN_ALPHA, c=0.5,
                                                           name="ln_bwd" + tag + sfx)
            dh = _ffn_bwd_mid(xprev_b, W["ffn_w13"][j], df, W["ffn_w2"][j], name="ffn_bwd_mid" + tag + sfx)
            dw2[j] = _mm(asave, df, ta=True, name="ffn_w2_grad" + tag + sfx)
            dw13[j] = _mm(xprev_b, dh, ta=True, b_halves=True, name="ffn_w13_grad" + tag + sfx)
            return _mm(dh, W["ffn_w13"][j], tb=True, a_halves=True, add=dres, name="ffn_up_bwd" + tag + sfx)

        dx2 = ffn_bwd(dy, L["r3"], L["x2_b"], L["a1"], 1, "b")

        dres2, dymix, dln_g[1], dln_b[1] = _ln_bwd(dx2, L["r2"], W["ln_g"][1], alpha=DN_ALPHA, c=1.0,
                                                   name="ln_bwd" + tag + "m")
        hcat = L["hcat"]
        dmix = _mm(dymix, W["w_out"], tb=True, name="out_proj_bwd" + tag)
        grads["w_out"][i] = _mm(L["mix"], dymix, ta=True, name="w_out_grad" + tag)
        dya, dyb, dgate = _merge_bwd(dmix, L["ya"], L["yb"], hcat, name="merge_bwd" + tag)
        dao = _mm(dya, W["w_branch_a"], tb=True, name="branch_a_bwd" + tag)
        grads["w_branch_a"][i] = _mm(L["ao"], dya, ta=True, name="w_branch_a_grad" + tag)
        don = _mm(dyb, W["w_branch_b"], tb=True, name="branch_b_bwd" + tag)
        grads["w_branch_b"][i] = _mm(L["on"], dyb, ta=True, name="w_branch_b_grad" + tag)
        do, dz, ddn = _rms_gate_bwd(don, L["o"], hcat, W["dn_norm_g"], name="rms_gate_bwd" + tag)
        grads["dn_norm_g"][i] = ddn.reshape(VAL_DIM_B)
        du, dw, dqd, dkt, daa, dgl = _dn_scan_bwd(L["u"], L["w"], L["qd"], L["kt"], L["aa"], L["gl"], L["ss"], do,
                                                  nbatch, name="dn_scan_bwd" + tag)
        dqn, dkn, dvs, dgb, dbb = _dn_prep_bwd(L["qn"], L["kn"], L["vs"], L["gb"], L["bb"], L["ti"], du, dw, dqd, dkt,
                                               daa, dgl, name="dn_prep_bwd" + tag)
        dc = _conv_prep_bwd_pointwise(dqn, dkn, dvs, L["c"], name="conv_prep_bwd" + tag)
        dconv, dconv_w = _conv_bwd(dc, hcat, W["conv_w"], nbatch, name="conv_bwd" + tag)
        grads["conv_w"][i] = dconv_w[:CONV_K]
        dbd, da_log, ddt = _gates_bwd(dgb, dbb, hcat, L["a_row"], L["dt_row"], name="gates_bwd" + tag)
        grads["a_log"][i] = da_log[0, N_HEADS_B:2 * N_HEADS_B]
        grads["dt_bias"][i] = ddt[0, N_HEADS_B:2 * N_HEADS_B]
        dq, dk, dv, dbias, dsink = _attn_bwd(hcat, bias, W["sinks"], dao, nbatch, name="swa_bwd" + tag)
        grads["sinks"][i] = dsink.reshape(N_HEADS_A)
        dbias_total = dbias if dbias_total is None else dbias_total + dbias
        dhcat = jnp.concatenate([dq, dz, dk, dv, dbd, jnp.zeros((t, HC_CONV - HC_BD - 128), BF16), dconv, dgate], axis=1)
        dw_in_p = _mm(L["x1_b"], dhcat, ta=True, name="w_in_grad" + tag)
        grads["w_in"][i] = _dw_in_by_owner(dw_in_p)
        dx1 = _mm(dhcat, W["w_in_p"], tb=True, add=dres2, name="in_proj_bwd" + tag)

        dy = ffn_bwd(dx1, L["r1"], L["x0_b"], L["a0"], 0, "a")
        grads["ln_g"][i] = jnp.concatenate(dln_g, axis=0)
        grads["ln_b"][i] = jnp.concatenate(dln_b, axis=0)
        grads["ffn_w13"][i] = dw13
        grads["ffn_w2"][i] = dw2

    out = {k: (v if k in MATRIX_NAMES else jnp.stack(v)) for k, v in grads.items()}
    drel = _mm(dbias_total.reshape(N_HEADS_A, WINDOW * 2 * WINDOW), onehot, name="rel_bias_grad")
    out["rel_bias"] = drel[:, :NUM_BUCKETS].T
    return loss, dy.reshape(nbatch, seq, d), out


N_CHIPS = 4
MESH_ID = pl.DeviceIdType.MESH
HBM_SPEC = pl.BlockSpec(memory_space=pltpu.HBM)


def _place():
    x, y, c = lax.axis_index("x"), lax.axis_index("y"), lax.axis_index("c")
    others = [(1 - x, y), (x, 1 - y), (1 - x, 1 - y)]
    return x, y, c, others


def _chip_index(cx, cy):
    return 2 * cx + cy


def _gather_sems(n):
    return [pltpu.SemaphoreType.DMA((n, 3)), pltpu.SemaphoreType.DMA((n, 3)), pltpu.SemaphoreType.DMA((n,))]


def _gather_copies(ins, outs, send_sems, recv_sems, local_sems):
    x, y, c, others = _place()
    me = _chip_index(x, y)
    copies = []
    for i in range(len(ins)):
        copies.append(pltpu.make_async_copy(ins[i], outs[i].at[me], local_sems.at[i]))
        for k, (ox, oy) in enumerate(others):
            copies.append(pltpu.make_async_remote_copy(src_ref=ins[i], dst_ref=outs[i].at[me], send_sem=send_sems.at[i, k],
                                                       recv_sem=recv_sems.at[i, k], device_id=(ox, oy, c),
                                                       device_id_type=MESH_ID))
    return copies


def _allgather_chips(tensors, *, name):
    n = len(tensors)

    def body(*refs):
        copies = _gather_copies(refs[:n], refs[n:2 * n], *refs[2 * n:])
        for cp in copies:
            cp.start()
        for cp in copies:
            cp.wait()

    return pl.pallas_call(
        body, name=name,
        in_specs=[HBM_SPEC] * n, out_specs=[HBM_SPEC] * n,
        out_shape=[jax.ShapeDtypeStruct((N_CHIPS,) + t.shape, t.dtype) for t in tensors],
        scratch_shapes=_gather_sems(n),
    )(*tensors)


def _allgather_devices(v, *, name):
    def body(v_ref, o_ref, send_sems, recv_sems, local_sem):
        x, y, c, _ = _place()
        me = 4 * x + 2 * y + c
        loc = pltpu.make_async_copy(v_ref, o_ref.at[me], local_sem)
        loc.start()
        copies = [loc]
        for k in range(1, 8):
            px, py, pc = x ^ (k >> 2), y ^ ((k >> 1) & 1), c ^ (k & 1)
            cp = pltpu.make_async_remote_copy(src_ref=v_ref, dst_ref=o_ref.at[me], send_sem=send_sems.at[k - 1],
                                              recv_sem=recv_sems.at[k - 1], device_id=(px, py, pc), device_id_type=MESH_ID)
            cp.start()
            copies.append(cp)
        for cp in copies:
            cp.wait()

    return pl.pallas_call(
        body, name=name, in_specs=[HBM_SPEC], out_specs=HBM_SPEC,
        out_shape=jax.ShapeDtypeStruct((8,) + v.shape, v.dtype),
        scratch_shapes=[pltpu.SemaphoreType.DMA((7,)), pltpu.SemaphoreType.DMA((7,)), pltpu.SemaphoreType.DMA],
    )(v)


def _sum_slots(g, *, name):
    nb, n, r, l = g.shape
    tr = r // 2 if r % 32 == 0 else r

    def body(g_ref, o_ref):
        acc = g_ref[0].astype(F32)
        for k in range(1, n):
            acc = acc + g_ref[k].astype(F32)
        o_ref[...] = acc

    return pl.pallas_call(
        body, name=name, grid=(nb, r // tr),
        in_specs=[pl.BlockSpec((None, n, tr, l), lambda b, i: (b, 0, i, 0))],
        out_specs=pl.BlockSpec((None, tr, l), lambda b, i: (b, i, 0)),
        out_shape=jax.ShapeDtypeStruct((nb, r, l), F32),
        compiler_params=_params(dimension_semantics=("parallel", "parallel")),
    )(g)


def _half_window(ref, kind, h):
    if kind == "rows":
        return ref.at[:, h]
    r = ref.shape[0] // 2
    return ref.at[pl.ds(pl.multiple_of(h * r, r), r), :]


def _owner_window(ref, kind, o):
    if kind == "rows":
        return ref.at[o]
    cols = ref.shape[1] // N_CHIPS
    return ref.at[:, pl.ds(pl.multiple_of(o * cols, cols), cols)]


def _half_shape(g, kind):
    return (g.shape[0],) + g.shape[2:] if kind == "rows" else (g.shape[0] // 2, g.shape[1])


def _swap_halves(gs, kinds, *, name):
    n = len(gs)

    def body(*refs):
        ins, outs, send_sems, recv_sems = refs[:n], refs[n:2 * n], refs[2 * n], refs[2 * n + 1]
        x, y, c, _ = _place()
        copies = []
        for i in range(n):
            cp = pltpu.make_async_remote_copy(src_ref=_half_window(ins[i], kinds[i], 1 - c), dst_ref=outs[i],
                                              send_sem=send_sems.at[i], recv_sem=recv_sems.at[i],
                                              device_id=(x, y, 1 - c), device_id_type=MESH_ID)
            cp.start()
            copies.append(cp)
        for cp in copies:
            cp.wait()

    return pl.pallas_call(
        body, name=name, in_specs=[HBM_SPEC] * n, out_specs=[HBM_SPEC] * n,
        out_shape=[jax.ShapeDtypeStruct(_half_shape(g, k), g.dtype) for g, k in zip(gs, kinds)],
        scratch_shapes=[pltpu.SemaphoreType.DMA((n,)), pltpu.SemaphoreType.DMA((n,))],
    )(*gs)


def _pair_sum(g, got, kind, c_idx, *, name):
    hs = _half_shape(g, kind)

    def body(c_ref, g_ref, r_ref, o_ref):
        o_ref[...] = (g_ref[...] + r_ref[...]).astype(BF16)

    if kind == "rows":
        _, _, r, cols = g.shape
        grid = (N_CHIPS,)
        in_specs = [pl.BlockSpec((None, None, r, cols), lambda o, c_ref: (o, c_ref[0], 0, 0)),
                    pl.BlockSpec((None, r, cols), lambda o, c_ref: (o, 0, 0))]
        out_spec = pl.BlockSpec((None, r, cols), lambda o, c_ref: (o, 0, 0))
    else:
        r, cols = hs
        steps = 4
        tr = r // steps
        grid = (steps,)
        in_specs = [pl.BlockSpec((tr, cols), lambda i, c_ref: (c_ref[0] * steps + i, 0)),
                    pl.BlockSpec((tr, cols), lambda i, c_ref: (i, 0))]
        out_spec = pl.BlockSpec((tr, cols), lambda i, c_ref: (i, 0))
    return pl.pallas_call(
        body, name=name,
        grid_spec=pltpu.PrefetchScalarGridSpec(num_scalar_prefetch=1, grid=grid, in_specs=in_specs, out_specs=out_spec),
        out_shape=jax.ShapeDtypeStruct(hs, BF16),
        compiler_params=_params(dimension_semantics=("parallel",)),
    )(c_idx, g, got)


def _exchange_chips(ss, kinds, places, out_shapes, *, name):
    n = len(ss)
    no = len(out_shapes)

    def body(*refs):
        ins, outs = refs[:n], refs[n:n + no]
        send_sems, recv_sems, local_sems = refs[n + no:]
        x, y, c, others = _place()
        me = _chip_index(x, y)
        copies = []
        for i in range(n):
            oi, lead = places[i]
            dst = outs[oi].at[(*lead, me)]
            copies.append(pltpu.make_async_copy(_owner_window(ins[i], kinds[i], me), dst, local_sems.at[i]))
            for k, (ox, oy) in enumerate(others):
                copies.append(pltpu.make_async_remote_copy(
                    src_ref=_owner_window(ins[i], kinds[i], _chip_index(ox, oy)), dst_ref=dst, send_sem=send_sems.at[i, k],
                    recv_sem=recv_sems.at[i, k], device_id=(ox, oy, c), device_id_type=MESH_ID))
        for cp in copies:
            cp.start()
        for cp in copies:
            cp.wait()

    return pl.pallas_call(
        body, name=name, in_specs=[HBM_SPEC] * n, out_specs=[HBM_SPEC] * no,
        out_shape=[jax.ShapeDtypeStruct(s, BF16) for s in out_shapes],
        scratch_shapes=_gather_sems(n),
    )(*ss)


def _send_halves(fs, *, name):
    n = len(fs)

    def body(*refs):
        ins, outs, send_sems, recv_sems = refs[:n], refs[n:2 * n], refs[2 * n], refs[2 * n + 1]
        x, y, c, _ = _place()
        copies = [pltpu.make_async_remote_copy(src_ref=ins[i], dst_ref=outs[i], send_sem=send_sems.at[i],
                                               recv_sem=recv_sems.at[i], device_id=(x, y, 1 - c), device_id_type=MESH_ID)
                  for i in range(n)]
        for cp in copies:
            cp.start()
        for cp in copies:
            cp.wait()

    return pl.pallas_call(
        body, name=name, in_specs=[HBM_SPEC] * n, out_specs=[HBM_SPEC] * n,
        out_shape=[jax.ShapeDtypeStruct(f.shape, f.dtype) for f in fs],
        scratch_shapes=[pltpu.SemaphoreType.DMA((n,)), pltpu.SemaphoreType.DMA((n,))],
    )(*fs)


def _adamw_halves(w, m, v, own, other, c_idx, *, name):
    shape = w.shape
    nl, r, cols = own.shape
    w4, m4, v4 = (a.reshape(nl, 2, r, cols) for a in (w, m, v))
    tr = r if r * cols * 4 <= 3 * 512 * 1024 else _tile(r, tuple(c for c in (256, 128, 64, 32, 16, 8) if c * cols <= 256 * 1024))

    def body(c_ref, w_ref, m_ref, v_ref, own_ref, other_ref, g_ref, d_ref, nm_ref, nv_ref):
        g_ = jnp.where(pl.program_id(1) == c_ref[0], own_ref[...], other_ref[...])
        g_ref[...] = g_
        d_ref[...], nm_ref[...], nv_ref[...] = _adamw_math(w_ref[...], g_, m_ref[...], v_ref[...])

    full = pl.BlockSpec((None, None, tr, cols), lambda l, h, i, c_ref: (l, h, i, 0))
    half = pl.BlockSpec((None, tr, cols), lambda l, h, i, c_ref: (l, i, 0))
    outs = pl.pallas_call(
        body, name=name,
        grid_spec=pltpu.PrefetchScalarGridSpec(num_scalar_prefetch=1, grid=(nl, 2, r // tr),
                                               in_specs=[full, full, full, half, half], out_specs=[full] * 4),
        out_shape=[jax.ShapeDtypeStruct((nl, 2, r, cols), F32)] * 4,
        compiler_params=_params(dimension_semantics=("parallel", "parallel", "parallel")),
    )(c_idx, w4, m4, v4, own, other)
    return tuple(o.reshape(shape) for o in outs)


SHARD_AXIS = {"rel_bias": None, "ln_g": 2, "ln_b": 2, "ffn_w13": 3, "ffn_w2": 2, "w_in": 2, "conv_w": 2, "a_log": None,
              "dt_bias": None, "dn_norm_g": None, "sinks": None, "w_branch_a": 1, "w_branch_b": 1, "w_out": 1}
WEIGHT_NAMES = tuple(SHARD_AXIS)
SMALL_NAMES = tuple(n for n in WEIGHT_NAMES if n not in MATRIX_NAMES)
PACK_LANES = 1024


def _unshard(gathered, axis):
    g = jnp.moveaxis(gathered, 0, axis)
    return g.reshape(g.shape[:axis] + (g.shape[axis] * g.shape[axis + 1],) + g.shape[axis + 2:])


def _reduce_matrices(grads, shard_shapes, c_idx):
    gs, kinds, places = [], [], []
    names = [n for n in MATRIX_NAMES]
    for oi, n in enumerate(names):
        for i, g in enumerate(grads[n]):
            if n == "ffn_w13":
                for j, gj in enumerate(g):
                    gs.append(gj)
                    kinds.append("cols")
                    places.append((oi, (i, j)))
            elif n == "ffn_w2":
                for j, gj in enumerate(g):
                    gs.append(gj.reshape(N_CHIPS, 2, gj.shape[0] // (2 * N_CHIPS), gj.shape[1]))
                    kinds.append("rows")
                    places.append((oi, (i, j)))
            elif n == "w_in":
                gs.append(g.reshape(N_CHIPS, 2, g.shape[1] // 2, g.shape[2]))
                kinds.append("rows")
                places.append((oi, (i,)))
            else:
                gs.append(g.reshape(N_CHIPS, 2, g.shape[0] // (2 * N_CHIPS), g.shape[1]))
                kinds.append("rows")
                places.append((oi, (i,)))
    got = _swap_halves(gs, kinds, name="rs_swap_halves")
    ss = [_pair_sum(g, r, k, c_idx, name=f"rs_pair_sum_{i}") for i, (g, r, k) in enumerate(zip(gs, got, kinds))]
    ex_shapes = []
    for n in names:
        s = shard_shapes[n]
        ex_shapes.append(s[:-2] + (N_CHIPS, s[-2] // 2, s[-1]))
    ex = _exchange_chips(ss, kinds, places, ex_shapes, name="rs_exchange_chips")
    own = [_sum_slots(e.reshape((-1,) + e.shape[-3:]), name="rs_chip_sum_" + n) for n, e in zip(names, ex)]
    other = _send_halves(own, name="rs_send_halves")
    return {n: (a, b) for n, a, b in zip(names, own, other)}


def _reduce_small(grads):
    flat = [grads[n].astype(F32).reshape(-1) for n in SMALL_NAMES]
    total = sum(f.shape[0] for f in flat)
    rows = -(-total // (16 * PACK_LANES)) * 16
    vec = jnp.concatenate(flat + [jnp.zeros((rows * PACK_LANES - total,), F32)]).reshape(rows, PACK_LANES)
    s = _sum_slots(_allgather_devices(vec, name="small_allgather")[None], name="small_sum").reshape(-1)
    out, o = {}, 0
    for n, f in zip(SMALL_NAMES, flat):
        out[n] = s[o:o + f.shape[0]].reshape(grads[n].shape)
        o += f.shape[0]
    return out


def kernel(x, rel_bias, ln_g, ln_b, ffn_w13, ffn_w2, w_in, conv_w, a_log, dt_bias, dn_norm_g, sinks, w_branch_a, w_branch_b, w_out, loss_target, m_rel_bias, m_ln_g, m_ln_b, m_ffn_w13, m_ffn_w2, m_w_in, m_conv_w, m_a_log, m_dt_bias, m_dn_norm_g, m_sinks, m_w_branch_a, m_w_branch_b, m_w_out, v_rel_bias, v_ln_g, v_ln_b, v_ffn_w13, v_ffn_w2, v_w_in, v_conv_w, v_a_log, v_dt_bias, v_dn_norm_g, v_sinks, v_w_branch_a, v_w_branch_b, v_w_out):
    w = dict(rel_bias=rel_bias, ln_g=ln_g, ln_b=ln_b, ffn_w13=ffn_w13, ffn_w2=ffn_w2, w_in=w_in, conv_w=conv_w,
             a_log=a_log, dt_bias=dt_bias, dn_norm_g=dn_norm_g, sinks=sinks, w_branch_a=w_branch_a,
             w_branch_b=w_branch_b, w_out=w_out)
    m = dict(rel_bias=m_rel_bias, ln_g=m_ln_g, ln_b=m_ln_b, ffn_w13=m_ffn_w13, ffn_w2=m_ffn_w2, w_in=m_w_in,
             conv_w=m_conv_w, a_log=m_a_log, dt_bias=m_dt_bias, dn_norm_g=m_dn_norm_g, sinks=m_sinks,
             w_branch_a=m_w_branch_a, w_branch_b=m_w_branch_b, w_out=m_w_out)
    v = dict(rel_bias=v_rel_bias, ln_g=v_ln_g, ln_b=v_ln_b, ffn_w13=v_ffn_w13, ffn_w2=v_ffn_w2, w_in=v_w_in,
             conv_w=v_conv_w, a_log=v_a_log, dt_bias=v_dt_bias, dn_norm_g=v_dn_norm_g, sinks=v_sinks,
             w_branch_a=v_w_branch_a, w_branch_b=v_w_branch_b, w_out=v_w_out)

    depth = w_in.shape[0]
    sharded = [n for n in WEIGHT_NAMES if SHARD_AXIS[n] is not None]

    def shards_of(i):
        return [w[n][i].astype(MXU_DTYPE) if n in MATRIX_NAMES else w[n][i] for n in sharded]

    def assemble(i, gathered):
        lw = {n: _unshard(g, SHARD_AXIS[n] - 1) for n, g in zip(sharded, gathered)}
        lw["w_in_p"] = _repack_w_in(lw.pop("w_in"))
        for n in ("a_log", "dt_bias", "dn_norm_g", "sinks"):
            lw[n] = w[n][i]
        return lw

    first = assemble(0, _allgather_chips(shards_of(0), name="weights_allgather_l0"))
    loss_part, grad_x, grads = _local_step(x, loss_target, rel_bias, [first] + [None] * (depth - 1),
                                           side_shards=[None] + [shards_of(i) for i in range(1, depth)],
                                           side_assemble=assemble)
    loss = lax.psum(loss_part, ("x", "y", "c"))

    c_idx = lax.axis_index("c").astype(jnp.int32).reshape(1)
    halves = _reduce_matrices(grads, {n: w[n].shape for n in MATRIX_NAMES}, c_idx)
    chip = _chip_index(lax.axis_index("x"), lax.axis_index("y"))
    small = _reduce_small(grads)
    outs = {}
    for n in WEIGHT_NAMES:
        if n in MATRIX_NAMES:
            outs[n] = _adamw_halves(w[n], m[n], v[n], *halves[n], c_idx, name="adamw_" + n)
        else:
            axis = SHARD_AXIS[n]
            g = small[n]
            if axis is not None:
                g = lax.dynamic_slice_in_dim(g, chip * w[n].shape[axis], w[n].shape[axis], axis)
            outs[n] = (g,) + _adamw(w[n], g, m[n], v[n], name="adamw_" + n)
    return (loss, grad_x, *[outs[n][0] for n in WEIGHT_NAMES], *[outs[n][1] for n in WEIGHT_NAMES],
            *[outs[n][2] for n in WEIGHT_NAMES], *[outs[n][3] for n in WEIGHT_NAMES])
```

```python
import functools
import math

import numpy as np
import jax
import jax.numpy as jnp
from jax import lax
from jax.experimental import pallas as pl
from jax.experimental.pallas import tpu as pltpu

F32 = jnp.float32
BF16 = jnp.bfloat16
MXU_DTYPE = BF16
HIGHEST = lax.Precision.HIGHEST

D_MODEL = 1024
N_HEADS_A = 16
N_KV_A = 4
HEAD_DIM_A = 64
GROUP_A = N_HEADS_A // N_KV_A
WINDOW = 128
N_HEADS_B = 8
KEY_DIM_B = 128
VAL_DIM_B = 128
CONV_K = 4
CHUNK = 64
D_FF = 2816
NUM_BUCKETS = 32
MAX_DISTANCE = 128
DEPTH = 4
DN_ALPHA = (2 * DEPTH) ** 0.25
LN_EPS = 1e-5
NORM_EPS = 1e-6
NEG_INF = -1e30

Q_A = N_HEADS_A * HEAD_DIM_A
KV_W = N_KV_A * HEAD_DIM_A
QK_B = N_HEADS_B * KEY_DIM_B
V_B = N_HEADS_B * VAL_DIM_B
CONV_CH = 2 * QK_B + V_B
N_IN = Q_A + 2 * KV_W + CONV_CH + 2 * N_HEADS_B + V_B + 2 * D_MODEL

ADAM_LR = 0.001
ADAM_B1 = 0.9
ADAM_B2 = 0.999
ADAM_EPS = 1e-08
ADAM_WD = 0.01
ADAM_STEP = 10

HC_W = 8192
HC_Q = 0
HC_Z = 1024
HC_K = 2048
HC_V = 2304
HC_BD = 2560
HC_CONV = 3072
HC_GATE = 6144

GROUP_T = 256
DN_BWD_HEADS = 4
ROW_T = 256
VMEM_LIMIT_BYTES = 48 * 1024 * 1024


def _params(**kw):
    return pltpu.CompilerParams(vmem_limit_bytes=VMEM_LIMIT_BYTES, **kw)


def _tile(n, cands):
    for c in cands:
        if n % c == 0:
            return c
    return n


def _dot(a, b, dims=(((1,), (0,)), ((), ())), exact=False):
    if exact:
        return lax.dot_general(a.astype(F32), b.astype(F32), dims, precision=HIGHEST, preferred_element_type=F32)
    return lax.dot_general(a.astype(MXU_DTYPE), b.astype(MXU_DTYPE), dims, preferred_element_type=F32)


_NN = (((1,), (0,)), ((), ()))
_NT = (((1,), (1,)), ((), ()))
_TN = (((0,), (0,)), ((), ()))


def _sigmoid(x):
    return 1.0 / (1.0 + jnp.exp(-x))


def _silu(x):
    return x * _sigmoid(x)


def _dsilu(x):
    s = _sigmoid(x)
    return s * (1.0 + x * (1.0 - s))


def _call(body, *, name, grid, in_specs, out_specs, out_shape, scratch_shapes=(), semantics, args, side=None):
    in_specs, out_specs, out_shape = list(in_specs), list(out_specs), list(out_shape)
    if side is None:
        outs = pl.pallas_call(body, name=name, grid=grid, in_specs=in_specs, out_specs=out_specs, out_shape=out_shape,
                              scratch_shapes=list(scratch_shapes),
                              compiler_params=_params(dimension_semantics=semantics))(*args)
        return outs, None
    n_in, n_out, n_scr = len(in_specs), len(out_specs), len(scratch_shapes)
    s_in, s_out = len(side["ins"]), len(side["out_shape"])

    def hosted(*refs):
        main_in, side_in = refs[:n_in], refs[n_in:n_in + s_in]
        o0 = n_in + s_in
        main_out, side_out = refs[o0:o0 + n_out], refs[o0 + n_out:o0 + n_out + s_out]
        rest = refs[o0 + n_out + s_out:]
        copies = side["make"](side_in, side_out, *rest[n_scr:])
        ids = [pl.program_id(d) for d in range(len(grid))]
        first = functools.reduce(jnp.logical_and, [i == 0 for i in ids])
        last = functools.reduce(jnp.logical_and, [i == g - 1 for i, g in zip(ids, grid)])

        @pl.when(first)
        def _():
            for cp in copies:
                cp.start()

        body(*main_in, *main_out, *rest[:n_scr])

        @pl.when(last)
        def _():
            for cp in copies:
                cp.wait()

    outs = pl.pallas_call(
        hosted, name=name, grid=grid,
        in_specs=in_specs + [HBM_SPEC] * s_in, out_specs=out_specs + [HBM_SPEC] * s_out,
        out_shape=out_shape + list(side["out_shape"]),
        scratch_shapes=list(scratch_shapes) + list(side["scratch"]),
        compiler_params=_params(dimension_semantics=("arbitrary",) * len(grid)),
    )(*args, *side["ins"])
    return list(outs[:n_out]), list(outs[n_out:])


def _mm(a, b, *, ta=False, tb=False, a_halves=False, b_halves=False, out_dtype=F32, add=None, exact=False, side=None,
        name):
    if a_halves:
        m, kdim = a.shape[1], 2 * a.shape[2]
    else:
        (kdim, m) = a.shape if ta else a.shape[::-1]
    if b_halves:
        kb, n = b.shape[1], 2 * b.shape[2]
    else:
        (n, kb) = b.shape if tb else b.shape[::-1]
    assert kdim == kb, (a.shape, b.shape, ta, tb)
    tn = _tile(n, (1024, 1408, 512, 256, 128))
    tk = _tile(kdim, (1024, 1408, 512, 256, 128))
    nk = kdim // tk
    tm = _tile(m, (1024, 1408, 512, 256, 128) if nk > 1 else (512, 256, 128))
    nj = n // tn
    dims = (((0 if ta else 1,), (1 if tb else 0,)), ((), ()))
    has_add = add is not None

    def body(*refs):
        if has_add:
            a_ref, b_ref, add_ref, o_ref = refs[:4]
        else:
            a_ref, b_ref, o_ref = refs[:3]
            add_ref = None
        part = _dot(a_ref[...], b_ref[...], dims, exact)

        def finish(acc):
            if has_add:
                acc = acc + add_ref[...].astype(F32)
            o_ref[...] = acc.astype(out_dtype)

        if nk == 1:
            finish(part)
        else:
            acc_ref = refs[-1]
            k = pl.program_id(2)

            @pl.when(k == 0)
            def _():
                acc_ref[...] = part

            @pl.when(k > 0)
            def _():
                acc_ref[...] += part

            @pl.when(k == nk - 1)
            def _():
                finish(acc_ref[...])

    if a_halves:
        assert not ta and nk % 2 == 0
        a_spec = pl.BlockSpec((None, tm, tk), lambda i, j, k: (k // (nk // 2), i, k % (nk // 2)))
    elif ta:
        a_spec = pl.BlockSpec((tk, tm), lambda i, j, k: (k, i))
    else:
        a_spec = pl.BlockSpec((tm, tk), lambda i, j, k: (i, k))
    if b_halves:
        assert not tb and nj % 2 == 0
        b_spec = pl.BlockSpec((None, tk, tn), lambda i, j, k: (j // (nj // 2), k, j % (nj // 2)))
    elif tb:
        b_spec = pl.BlockSpec((tn, tk), lambda i, j, k: (j, k))
    else:
        b_spec = pl.BlockSpec((tk, tn), lambda i, j, k: (k, j))
    o_spec = pl.BlockSpec((tm, tn), lambda i, j, k: (i, j))
    in_specs = [a_spec, b_spec] + ([o_spec] if has_add else [])
    args = (a, b) + ((add,) if has_add else ())
    outs, side_outs = _call(
        body, name=name, grid=(m // tm, nj, nk), in_specs=in_specs, out_specs=[o_spec],
        out_shape=[jax.ShapeDtypeStruct((m, n), out_dtype)],
        scratch_shapes=[pltpu.VMEM((tm, tn), F32)] if nk > 1 else [],
        semantics=("parallel", "parallel", "arbitrary"), args=args, side=side)
    return outs[0] if side is None else (outs[0], side_outs)


def _layernorm_rows(r, g, b):
    mu = jnp.mean(r, axis=-1, keepdims=True)
    xc = r - mu
    var = jnp.mean(xc * xc, axis=-1, keepdims=True)
    return xc * lax.rsqrt(var + LN_EPS) * g + b


def _mm_res_ln(a, w, resid, g, b, *, alpha, c, name):
    m, kdim = a.shape
    n = w.shape[1]
    tm = _tile(m, (512, 256, 128))

    def body(a_ref, w_ref, x_ref, g_ref, b_ref, r_ref, y_ref, yb_ref):
        f = _dot(a_ref[...], w_ref[...])
        r = alpha * x_ref[...] + c * f
        r_ref[...] = r
        y = _layernorm_rows(r, g_ref[...], b_ref[...])
        y_ref[...] = y
        yb_ref[...] = y.astype(BF16)

    row = pl.BlockSpec((tm, n), lambda i: (i, 0))
    vec = pl.BlockSpec((1, n), lambda i: (0, 0))
    return pl.pallas_call(
        body, name=name, grid=(m // tm,),
        in_specs=[pl.BlockSpec((tm, kdim), lambda i: (i, 0)), pl.BlockSpec((kdim, n), lambda i: (0, 0)), row, vec, vec],
        out_specs=[row, row, row],
        out_shape=[jax.ShapeDtypeStruct((m, n), F32)] * 2 + [jax.ShapeDtypeStruct((m, n), BF16)],
        compiler_params=_params(dimension_semantics=("parallel",)),
    )(a, w, resid, g.reshape(1, n), b.reshape(1, n))


def _ln_bwd(dy, r, g, *, alpha, c, name):
    m, n = dy.shape
    tm = _tile(m, (512, 256, 128))

    def body(dy_ref, r_ref, g_ref, dres_ref, dbr_ref, dg_ref, db_ref):
        i = pl.program_id(0)
        dy_ = dy_ref[...]
        r_ = r_ref[...]
        mu = jnp.mean(r_, axis=-1, keepdims=True)
        xc = r_ - mu
        var = jnp.mean(xc * xc, axis=-1, keepdims=True)
        rstd = lax.rsqrt(var + LN_EPS)
        xh = xc * rstd
        dxh = dy_ * g_ref[...]
        dr = rstd * (dxh - jnp.mean(dxh, axis=-1, keepdims=True) - xh * jnp.mean(dxh * xh, axis=-1, keepdims=True))
        dres_ref[...] = alpha * dr
        dbr_ref[...] = (c * dr).astype(BF16)
        dg_p = jnp.sum(dy_ * xh, axis=0, keepdims=True)
        db_p = jnp.sum(dy_, axis=0, keepdims=True)

        @pl.when(i == 0)
        def _():
            dg_ref[...] = dg_p
            db_ref[...] = db_p

        @pl.when(i > 0)
        def _():
            dg_ref[...] += dg_p
            db_ref[...] += db_p

    row = pl.BlockSpec((tm, n), lambda i: (i, 0))
    vec = pl.BlockSpec((1, n), lambda i: (0, 0))
    return pl.pallas_call(
        body, name=name, grid=(m // tm,),
        in_specs=[row, row, vec], out_specs=[row, row, vec, vec],
        out_shape=[jax.ShapeDtypeStruct((m, n), F32), jax.ShapeDtypeStruct((m, n), BF16),
                   jax.ShapeDtypeStruct((1, n), F32), jax.ShapeDtypeStruct((1, n), F32)],
        compiler_params=_params(dimension_semantics=("arbitrary",)),
    )(dy, r, g.reshape(1, n))


FFN_TN = D_FF // 2


def _ffn_up_act(x, w13, *, name):
    m, d = x.shape
    tm = _tile(m, (512, 256, 128))
    nj = D_FF // FFN_TN

    def body(x_ref, g_ref, u_ref, o_ref):
        x_ = x_ref[...]
        o_ref[...] = (_silu(_dot(x_, g_ref[...])) * _dot(x_, u_ref[...])).astype(BF16)

    return pl.pallas_call(
        body, name=name, grid=(nj, m // tm),
        in_specs=[pl.BlockSpec((tm, d), lambda j, i: (i, 0)), pl.BlockSpec((d, FFN_TN), lambda j, i: (0, j)),
                  pl.BlockSpec((d, FFN_TN), lambda j, i: (0, j + nj))],
        out_specs=pl.BlockSpec((tm, FFN_TN), lambda j, i: (i, j)),
        out_shape=jax.ShapeDtypeStruct((m, D_FF), BF16),
        compiler_params=_params(dimension_semantics=("parallel", "parallel")),
    )(x, w13, w13)


def _ffn_bwd_mid(x, w13, df, w2, *, side=None, name):
    m, d = x.shape
    tm = _tile(m, (512, 256, 128))
    nj = D_FF // FFN_TN

    def body(x_ref, g_ref, u_ref, df_ref, w2_ref, o_ref):
        x_ = x_ref[...]
        g = _dot(x_, g_ref[...])
        u = _dot(x_, u_ref[...])
        da = _dot(df_ref[...], w2_ref[...], _NT)
        o_ref[0] = (da * u * _dsilu(g)).astype(BF16)
        o_ref[1] = (da * _silu(g)).astype(BF16)

    outs, side_outs = _call(
        body, name=name, grid=(nj, m // tm),
        in_specs=[pl.BlockSpec((tm, d), lambda j, i: (i, 0)), pl.BlockSpec((d, FFN_TN), lambda j, i: (0, j)),
                  pl.BlockSpec((d, FFN_TN), lambda j, i: (0, j + nj)), pl.BlockSpec((tm, d), lambda j, i: (i, 0)),
                  pl.BlockSpec((FFN_TN, d), lambda j, i: (j, 0))],
        out_specs=[pl.BlockSpec((2, tm, FFN_TN), lambda j, i: (0, i, j))],
        out_shape=[jax.ShapeDtypeStruct((2, m, D_FF), BF16)],
        semantics=("parallel", "parallel"), args=(x, w13, w13, df, w2), side=side)
    return outs[0], side_outs


def _t5_bucket_table():
    r = np.arange(WINDOW)[:, None]
    j = np.arange(2 * WINDOW)[None, :]
    rel = r + WINDOW - j
    n = np.maximum(rel, 0)
    max_exact = NUM_BUCKETS // 2
    nf = np.maximum(n, 1).astype(np.float32)
    large = max_exact + (np.log(nf / np.float32(max_exact)) / np.float32(math.log(MAX_DISTANCE / max_exact))
                         * np.float32(NUM_BUCKETS - max_exact)).astype(np.int32)
    large = np.minimum(large, NUM_BUCKETS - 1)
    bucket = np.where(n < max_exact, n, large)
    in_band = (rel >= 0) & (rel < WINDOW)
    return bucket.astype(np.int32), in_band


def _bucket_onehot():
    bucket, _ = _t5_bucket_table()
    oh = np.zeros((WINDOW * 2 * WINDOW, 128), np.float32)
    oh[np.arange(oh.shape[0]), bucket.reshape(-1)] = 1.0
    return oh


def _stack_heads(x, g):
    hd = HEAD_DIM_A
    return jnp.concatenate([x[:, (GROUP_A * g + h) * hd:(GROUP_A * g + h + 1) * hd] for h in range(GROUP_A)], axis=0)


def _unstack_heads(x):
    return jnp.concatenate([x[h * WINDOW:(h + 1) * WINDOW] for h in range(GROUP_A)], axis=1)


def _attn_probs(q_ref, kp_ref, ko_ref, vp_ref, vo_ref, bias_ref, sink_ref, first_block):
    hd = HEAD_DIM_A
    groups = range(N_KV_A)
    q = q_ref[...]
    qs = [_stack_heads(q, g) * (hd ** -0.5) for g in groups]
    k2 = [jnp.concatenate([kp_ref[:, g * hd:(g + 1) * hd], ko_ref[:, g * hd:(g + 1) * hd]], axis=0) for g in groups]
    v2 = [jnp.concatenate([vp_ref[:, g * hd:(g + 1) * hd], vo_ref[:, g * hd:(g + 1) * hd]], axis=0) for g in groups]
    rr = lax.broadcasted_iota(jnp.int32, (GROUP_A * WINDOW, 2 * WINDOW), 0) % WINDOW
    jj = lax.broadcasted_iota(jnp.int32, (GROUP_A * WINDOW, 2 * WINDOW), 1)
    rel = rr + WINDOW - jj
    valid = (rel >= 0) & (rel < WINDOW) & (jnp.logical_not(first_block) | (jj >= WINDOW))
    s = [_dot(qs[g], k2[g], _NT) for g in groups]
    s = [jnp.where(valid, s[g] + bias_ref[GROUP_A * g:GROUP_A * (g + 1)].reshape(GROUP_A * WINDOW, 2 * WINDOW), NEG_INF)
         for g in groups]
    sk = [jnp.concatenate([jnp.broadcast_to(sink_ref[0:1, GROUP_A * g + h:GROUP_A * g + h + 1], (WINDOW, 1))
                           for h in range(GROUP_A)], axis=0) for g in groups]
    mx = [jnp.maximum(jnp.max(s[g], axis=-1, keepdims=True), sk[g]) for g in groups]
    p = [jnp.exp(s[g] - mx[g]) for g in groups]
    ps = [jnp.exp(sk[g] - mx[g]) for g in groups]
    den = [jnp.sum(p[g], axis=-1, keepdims=True) + ps[g] for g in groups]
    return qs, k2, v2, [p[g] / den[g] for g in groups], [ps[g] / den[g] for g in groups]


def _attn_specs(nb):
    def prev(b, i):
        return (b * nb + jnp.maximum(i - 1, 0))

    q_spec = pl.BlockSpec((WINDOW, Q_A), lambda b, i: (b * nb + i, HC_Q // Q_A))
    kp_spec = pl.BlockSpec((WINDOW, KV_W), lambda b, i: (prev(b, i), HC_K // KV_W))
    ko_spec = pl.BlockSpec((WINDOW, KV_W), lambda b, i: (b * nb + i, HC_K // KV_W))
    vp_spec = pl.BlockSpec((WINDOW, KV_W), lambda b, i: (prev(b, i), HC_V // KV_W))
    vo_spec = pl.BlockSpec((WINDOW, KV_W), lambda b, i: (b * nb + i, HC_V // KV_W))
    bias_spec = pl.BlockSpec((N_HEADS_A, WINDOW, 2 * WINDOW), lambda b, i: (0, 0, 0))
    sink_spec = pl.BlockSpec((1, N_HEADS_A), lambda b, i: (0, 0))
    return [q_spec, kp_spec, ko_spec, vp_spec, vo_spec, bias_spec, sink_spec]


def _attn_fwd(hcat, bias, sink, nbatch, *, name):
    t = hcat.shape[0]
    nb = t // nbatch // WINDOW

    def body(q_ref, kp_ref, ko_ref, vp_ref, vo_ref, bias_ref, sink_ref, o_ref):
        first = pl.program_id(1) == 0
        _, _, v2, p, _ = _attn_probs(q_ref, kp_ref, ko_ref, vp_ref, vo_ref, bias_ref, sink_ref, first)
        o = [_dot(p[g], v2[g]) for g in range(N_KV_A)]
        o_ref[...] = jnp.concatenate([_unstack_heads(og) for og in o], axis=1).astype(BF16)

    return pl.pallas_call(
        body, name=name, grid=(nbatch, nb),
        in_specs=_attn_specs(nb),
        out_specs=pl.BlockSpec((WINDOW, Q_A), lambda b, i: (b * nb + i, 0)),
        out_shape=jax.ShapeDtypeStruct((t, Q_A), BF16),
        compiler_params=_params(dimension_semantics=("parallel", "arbitrary")),
    )(hcat, hcat, hcat, hcat, hcat, bias, sink.reshape(1, N_HEADS_A))


def _attn_bwd(hcat, bias, sink, do, nbatch, *, name):
    t = hcat.shape[0]
    nb = t // nbatch // WINDOW
    hd = HEAD_DIM_A

    def body(q_ref, kp_ref, ko_ref, vp_ref, vo_ref, bias_ref, sink_ref, do_ref,
             dq_ref, dk_ref, dv_ref, dbias_ref, dsink_ref, ck_ref, cv_ref):
        b = pl.program_id(0)
        j = pl.program_id(1)
        first = j == nb - 1

        @pl.when((b == 0) & (j == 0))
        def _():
            dbias_ref[...] = jnp.zeros_like(dbias_ref)
            dsink_ref[...] = jnp.zeros_like(dsink_ref)

        @pl.when(j == 0)
        def _():
            ck_ref[...] = jnp.zeros_like(ck_ref)
            cv_ref[...] = jnp.zeros_like(cv_ref)

        do_ = do_ref[...]
        groups = range(N_KV_A)
        lane = lax.broadcasted_iota(jnp.int32, (1, N_HEADS_A), 1)
        qs, k2, v2, p, ps = _attn_probs(q_ref, kp_ref, ko_ref, vp_ref, vo_ref, bias_ref, sink_ref, first)
        dos = [_stack_heads(do_, g) for g in groups]
        dv2 = [_dot(p[g], dos[g], _TN) for g in groups]
        dp = [_dot(dos[g], v2[g], _NT) for g in groups]
        delta = [jnp.sum(p[g] * dp[g], axis=-1, keepdims=True) for g in groups]
        ds = [p[g] * (dp[g] - delta[g]) for g in groups]
        dqs = [_dot(ds[g], k2[g]) * (hd ** -0.5) for g in groups]
        dk2 = [_dot(ds[g], qs[g], _TN) for g in groups]
        dsink = jnp.zeros((1, N_HEADS_A), F32)
        for g in groups:
            dsk = -(ps[g] * delta[g])
            for h in range(GROUP_A):
                tot = jnp.sum(dsk[h * WINDOW:(h + 1) * WINDOW], axis=0, keepdims=True)
                dsink = dsink + jnp.where(lane == GROUP_A * g + h, tot, 0.0)
            dbias_ref[GROUP_A * g:GROUP_A * (g + 1)] += ds[g].reshape(GROUP_A, WINDOW, 2 * WINDOW)
        dq_ref[...] = jnp.concatenate([_unstack_heads(d) for d in dqs], axis=1).astype(BF16)
        dk_ref[...] = (jnp.concatenate([d[WINDOW:] for d in dk2], axis=1) + ck_ref[...]).astype(BF16)
        dv_ref[...] = (jnp.concatenate([d[WINDOW:] for d in dv2], axis=1) + cv_ref[...]).astype(BF16)
        ck_ref[...] = jnp.concatenate([d[:WINDOW] for d in dk2], axis=1)
        cv_ref[...] = jnp.concatenate([d[:WINDOW] for d in dv2], axis=1)
        dsink_ref[...] += dsink

    def rev(spec):
        return pl.BlockSpec(spec.block_shape, lambda b, j, f=spec.index_map: f(b, nb - 1 - j))

    in_specs = [rev(s) for s in _attn_specs(nb)[:5]] + _attn_specs(nb)[5:]
    in_specs.append(pl.BlockSpec((WINDOW, Q_A), lambda b, j: (b * nb + nb - 1 - j, 0)))
    return pl.pallas_call(
        body, name=name, grid=(nbatch, nb),
        in_specs=in_specs,
        out_specs=[pl.BlockSpec((WINDOW, Q_A), lambda b, j: (b * nb + nb - 1 - j, 0)),
                   pl.BlockSpec((WINDOW, KV_W), lambda b, j: (b * nb + nb - 1 - j, 0)),
                   pl.BlockSpec((WINDOW, KV_W), lambda b, j: (b * nb + nb - 1 - j, 0)),
                   pl.BlockSpec((N_HEADS_A, WINDOW, 2 * WINDOW), lambda b, j: (0, 0, 0)),
                   pl.BlockSpec((1, N_HEADS_A), lambda b, j: (0, 0))],
        out_shape=[jax.ShapeDtypeStruct((t, Q_A), BF16), jax.ShapeDtypeStruct((t, KV_W), BF16),
                   jax.ShapeDtypeStruct((t, KV_W), BF16),
                   jax.ShapeDtypeStruct((N_HEADS_A, WINDOW, 2 * WINDOW), F32),
                   jax.ShapeDtypeStruct((1, N_HEADS_A), F32)],
        scratch_shapes=[pltpu.VMEM((WINDOW, KV_W), F32), pltpu.VMEM((WINDOW, KV_W), F32)],
        compiler_params=_params(dimension_semantics=("arbitrary", "arbitrary")),
    )(hcat, hcat, hcat, hcat, hcat, bias, sink.reshape(1, N_HEADS_A), do)


def _shift_down(x, halo8, s):
    if s == 0:
        return x
    rolled = pltpu.roll(x, s, axis=0)
    row8 = lax.broadcasted_iota(jnp.int32, halo8.shape, 0)
    top = jnp.where(row8 < s, pltpu.roll(halo8, s, axis=0), rolled[0:8])
    return jnp.concatenate([top, rolled[8:]], axis=0)


def _shift_up(x, halo8, s):
    if s == 0:
        return x
    n = x.shape[0]
    rolled = pltpu.roll(x, n - s, axis=0)
    row8 = lax.broadcasted_iota(jnp.int32, halo8.shape, 0)
    bottom = jnp.where(row8 >= 8 - s, pltpu.roll(halo8, 8 - s, axis=0), rolled[n - 8:n])
    return jnp.concatenate([rolled[:n - 8], bottom], axis=0)


def _l2n(x, scale):
    r = lax.rsqrt(jnp.sum(x * x, axis=-1, keepdims=True) + NORM_EPS)
    return x * (r * scale)


def _conv_prep(hcat, conv_w, nbatch, *, name):
    t = hcat.shape[0]
    nt = t // nbatch // ROW_T
    cb = HC_CONV // CONV_CH

    def body(u_ref, halo_ref, w_ref, c_ref, q_ref, k_ref, v_ref):
        i = pl.program_id(1)
        u = u_ref[...]
        halo = jnp.where(i == 0, 0.0, halo_ref[...])
        c = jnp.zeros_like(u)
        for j in range(CONV_K):
            c = c + w_ref[j:j + 1, :] * _shift_down(u, halo, CONV_K - 1 - j)
        c_ref[...] = c
        s = _silu(c)
        for h in range(N_HEADS_B):
            lo, hi = h * KEY_DIM_B, (h + 1) * KEY_DIM_B
            q_ref[:, lo:hi] = _l2n(s[:, lo:hi], KEY_DIM_B ** -0.5)
            k_ref[:, lo:hi] = _l2n(s[:, QK_B + lo:QK_B + hi], 1.0)
        v_ref[...] = s[:, 2 * QK_B:]

    row = lambda w: pl.BlockSpec((ROW_T, w), lambda b, i: (b * nt + i, 0))
    return pl.pallas_call(
        body, name=name, grid=(nbatch, nt),
        in_specs=[pl.BlockSpec((ROW_T, CONV_CH), lambda b, i: (b * nt + i, cb)),
                  pl.BlockSpec((8, CONV_CH), lambda b, i: (jnp.maximum((b * nt + i) * (ROW_T // 8) - 1, 0), cb)),
                  pl.BlockSpec((CONV_K, CONV_CH), lambda b, i: (0, 0))],
        out_specs=[row(CONV_CH), row(QK_B), row(QK_B), row(V_B)],
        out_shape=[jax.ShapeDtypeStruct((t, CONV_CH), F32)] + [jax.ShapeDtypeStruct((t, QK_B), F32)] * 3,
        compiler_params=_params(dimension_semantics=("parallel", "parallel")),
    )(hcat, hcat, conv_w)


def _conv_prep_bwd_pointwise(dq, dk, dv, c, *, name):
    t = c.shape[0]

    def l2n_bwd(x, dy, scale):
        r = lax.rsqrt(jnp.sum(x * x, axis=-1, keepdims=True) + NORM_EPS)
        return scale * (r * dy - x * (r * r * r) * jnp.sum(x * dy, axis=-1, keepdims=True))

    def body(dq_ref, dk_ref, dv_ref, c_ref, dc_ref):
        c_ = c_ref[...]
        s = _silu(c_)
        ds = _dsilu(c_)
        for h in range(N_HEADS_B):
            lo, hi = h * KEY_DIM_B, (h + 1) * KEY_DIM_B
            dc_ref[:, lo:hi] = l2n_bwd(s[:, lo:hi], dq_ref[:, lo:hi], KEY_DIM_B ** -0.5) * ds[:, lo:hi]
            dc_ref[:, QK_B + lo:QK_B + hi] = (l2n_bwd(s[:, QK_B + lo:QK_B + hi], dk_ref[:, lo:hi], 1.0)
                                              * ds[:, QK_B + lo:QK_B + hi])
        dc_ref[:, 2 * QK_B:] = dv_ref[...] * ds[:, 2 * QK_B:]

    row = lambda w: pl.BlockSpec((ROW_T, w), lambda i: (i, 0))
    return pl.pallas_call(
        body, name=name, grid=(t // ROW_T,),
        in_specs=[row(QK_B), row(QK_B), row(V_B), row(CONV_CH)], out_specs=row(CONV_CH),
        out_shape=jax.ShapeDtypeStruct((t, CONV_CH), F32),
        compiler_params=_params(dimension_semantics=("parallel",)),
    )(dq, dk, dv, c)


def _conv_bwd(dc, hcat, conv_w, nbatch, *, name):
    t = dc.shape[0]
    nt = t // nbatch // ROW_T
    cb = HC_CONV // CONV_CH
    last_blk = t // 8 - 1

    def body(dc_ref, dnext_ref, u_ref, uprev_ref, w_ref, du_ref, dw_ref):
        b = pl.program_id(0)
        i = pl.program_id(1)
        dc_ = dc_ref[...]
        u = u_ref[...]
        dnext = jnp.where(i == nt - 1, 0.0, dnext_ref[...])
        uprev = jnp.where(i == 0, 0.0, uprev_ref[...])
        du = jnp.zeros_like(dc_)
        rows = []
        for j in range(CONV_K):
            s = CONV_K - 1 - j
            du = du + w_ref[j:j + 1, :] * _shift_up(dc_, dnext, s)
            rows.append(jnp.sum(dc_ * _shift_down(u, uprev, s), axis=0, keepdims=True))
        du_ref[...] = du.astype(BF16)
        dw_p = jnp.concatenate(rows + [jnp.zeros((8 - CONV_K, CONV_CH), F32)], axis=0)

        @pl.when((b == 0) & (i == 0))
        def _():
            dw_ref[...] = dw_p

        @pl.when((b > 0) | (i > 0))
        def _():
            dw_ref[...] += dw_p

    return pl.pallas_call(
        body, name=name, grid=(nbatch, nt),
        in_specs=[pl.BlockSpec((ROW_T, CONV_CH), lambda b, i: (b * nt + i, 0)),
                  pl.BlockSpec((8, CONV_CH), lambda b, i: (jnp.minimum((b * nt + i + 1) * (ROW_T // 8), last_blk), 0)),
                  pl.BlockSpec((ROW_T, CONV_CH), lambda b, i: (b * nt + i, cb)),
                  pl.BlockSpec((8, CONV_CH), lambda b, i: (jnp.maximum((b * nt + i) * (ROW_T // 8) - 1, 0), cb)),
                  pl.BlockSpec((CONV_K, CONV_CH), lambda b, i: (0, 0))],
        out_specs=[pl.BlockSpec((ROW_T, CONV_CH), lambda b, i: (b * nt + i, 0)),
                   pl.BlockSpec((8, CONV_CH), lambda b, i: (0, 0))],
        out_shape=[jax.ShapeDtypeStruct((t, CONV_CH), BF16), jax.ShapeDtypeStruct((8, CONV_CH), F32)],
        compiler_params=_params(dimension_semantics=("arbitrary", "arbitrary")),
    )(dc, dc, hcat, hcat, conv_w)


def _softplus(x):
    return jnp.maximum(x, 0.0) + jnp.log(1.0 + jnp.exp(-jnp.abs(x)))


def _gates(hcat, a_row, dt_row, *, name):
    t = hcat.shape[0]

    def body(bd_ref, a_ref, dt_ref, gb_ref, bb_ref):
        bd = bd_ref[...]
        beta = _sigmoid(bd)
        g = -jnp.exp(a_ref[...]) * _softplus(bd + dt_ref[...])
        for h in range(N_HEADS_B):
            lo, hi = h * VAL_DIM_B, (h + 1) * VAL_DIM_B
            bb_ref[:, lo:hi] = jnp.broadcast_to(beta[:, h:h + 1], (ROW_T, VAL_DIM_B))
            gb_ref[:, lo:hi] = jnp.broadcast_to(g[:, N_HEADS_B + h:N_HEADS_B + h + 1], (ROW_T, VAL_DIM_B))

    vec = pl.BlockSpec((1, 128), lambda i: (0, 0))
    row = pl.BlockSpec((ROW_T, V_B), lambda i: (i, 0))
    return pl.pallas_call(
        body, name=name, grid=(t // ROW_T,),
        in_specs=[pl.BlockSpec((ROW_T, 128), lambda i: (i, HC_BD // 128)), vec, vec],
        out_specs=[row, row], out_shape=[jax.ShapeDtypeStruct((t, V_B), F32)] * 2,
        compiler_params=_params(dimension_semantics=("parallel",)),
    )(hcat, a_row, dt_row)


def _gates_bwd(dgb, dbb, hcat, a_row, dt_row, *, name):
    t = hcat.shape[0]

    def body(dgb_ref, dbb_ref, bd_ref, a_ref, dt_ref, dbd_ref, da_ref, ddt_ref):
        i = pl.program_id(0)
        bd = bd_ref[...]
        beta = _sigmoid(bd)
        ea = jnp.exp(a_ref[...])
        x = bd + dt_ref[...]
        g = -ea * _softplus(x)
        lane = lax.broadcasted_iota(jnp.int32, (ROW_T, 128), 1)
        dbeta = jnp.zeros((ROW_T, 128), F32)
        dg = jnp.zeros((ROW_T, 128), F32)
        for h in range(N_HEADS_B):
            lo, hi = h * VAL_DIM_B, (h + 1) * VAL_DIM_B
            dbeta = dbeta + jnp.where(lane == h, jnp.sum(dbb_ref[:, lo:hi], axis=-1, keepdims=True), 0.0)
            dg = dg + jnp.where(lane == N_HEADS_B + h, jnp.sum(dgb_ref[:, lo:hi], axis=-1, keepdims=True), 0.0)
        ddt_raw = dg * (-ea) * _sigmoid(x)
        dbd_ref[...] = (dbeta * beta * (1.0 - beta) + ddt_raw).astype(BF16)
        da_p = jnp.sum(dg * g, axis=0, keepdims=True)
        ddt_p = jnp.sum(ddt_raw, axis=0, keepdims=True)

        @pl.when(i == 0)
        def _():
            da_ref[...] = da_p
            ddt_ref[...] = ddt_p

        @pl.when(i > 0)
        def _():
            da_ref[...] += da_p
            ddt_ref[...] += ddt_p

    vec = pl.BlockSpec((1, 128), lambda i: (0, 0))
    row = pl.BlockSpec((ROW_T, V_B), lambda i: (i, 0))
    return pl.pallas_call(
        body, name=name, grid=(t // ROW_T,),
        in_specs=[row, row, pl.BlockSpec((ROW_T, 128), lambda i: (i, HC_BD // 128)), vec, vec],
        out_specs=[pl.BlockSpec((ROW_T, 128), lambda i: (i, 0)), vec, vec],
        out_shape=[jax.ShapeDtypeStruct((t, 128), BF16), jax.ShapeDtypeStruct((1, 128), F32),
                   jax.ShapeDtypeStruct((1, 128), F32)],
        compiler_params=_params(dimension_semantics=("arbitrary",)),
    )(dgb, dbb, hcat, a_row, dt_row)


def _group_masks():
    r = lax.broadcasted_iota(jnp.int32, (GROUP_T, GROUP_T), 0)
    c = lax.broadcasted_iota(jnp.int32, (GROUP_T, GROUP_T), 1)
    same = (r // CHUNK) == (c // CHUNK)
    return same, same & (r >= c), same & (r > c)


def _split2(a):
    hi = a.astype(MXU_DTYPE)
    return hi, (a - hi.astype(F32)).astype(MXU_DTYPE)


def _dot3(a2, b2, dims=_NN):
    (ah, al), (bh, bl) = a2, b2
    d = functools.partial(lax.dot_general, dimension_numbers=dims, preferred_element_type=F32)
    return d(ah, bh) + (d(ah, bl) + d(al, bh))


def _inv_unit_lower(lows):
    shape = lows[0].shape
    eye = (lax.broadcasted_iota(jnp.int32, shape, 0) == lax.broadcasted_iota(jnp.int32, shape, 1)).astype(F32)
    p2 = [_split2(-low) for low in lows]
    ts = [eye - low for low in lows]
    for _ in range(int(math.log2(CHUNK)) - 1):
        p2 = [_split2(_dot3(p, p)) for p in p2]
        ts = [t + _dot3(_split2(t), p) for t, p in zip(ts, p2)]
    return ts


@jax.custom_vjp
def _inv_saved(low, t):
    return t


def _inv_saved_fwd(low, t):
    return t, t


def _inv_saved_bwd(t, dt):
    t2 = _split2(t)
    return -_dot3(t2, _split2(_dot3(_split2(dt), t2, _NT)), _TN), jnp.zeros_like(t)


_inv_saved.defvjp(_inv_saved_fwd, _inv_saved_bwd)


def _mask_dot(mask, x, dims):
    m = mask.astype(MXU_DTYPE)
    hi = x.astype(MXU_DTYPE)
    r1 = x - hi.astype(F32)
    mid = r1.astype(MXU_DTYPE)
    lo = (r1 - mid.astype(F32)).astype(MXU_DTYPE)
    d = functools.partial(lax.dot_general, dimension_numbers=dims, preferred_element_type=F32)
    return d(m, hi) + (d(m, mid) + d(m, lo))


@jax.custom_vjp
def _chunk_sums(gb):
    same, causal, _ = _group_masks()
    return _mask_dot(causal, gb, _NN), _mask_dot(same, gb, _NN)


def _chunk_sums_fwd(gb):
    return _chunk_sums(gb), None


def _chunk_sums_bwd(_, cot):
    same, causal, _ = _group_masks()
    return (_mask_dot(causal, cot[0], _TN) + _mask_dot(same, cot[1], _TN),)


_chunk_sums.defvjp(_chunk_sums_fwd, _chunk_sums_bwd)


def _fold_blocks(m):
    return m[:, 0:CHUNK] + m[:, CHUNK:2 * CHUNK] + m[:, 2 * CHUNK:3 * CHUNK] + m[:, 3 * CHUNK:4 * CHUNK]


def _dn_prep_heads(q, k, v, gb, bb, tsaved=None):
    same, causal, strict = _group_masks()
    heads = range(len(q))
    sums = [_chunk_sums(gb[h]) for h in heads]
    gc = [s[0] for s in sums]
    glast = [s[1] for s in sums]
    decay = [jnp.exp(jnp.where(causal, gc[h][:, 0:1] - gc[h].T[0:1, :], NEG_INF)) for h in heads]
    kb = [k[h] * bb[h] for h in heads]
    vb = [v[h] * bb[h] for h in heads]
    lower = [jnp.where(strict, _dot(kb[h], k[h], _NT) * decay[h], 0.0) for h in heads]
    if tsaved is None:
        tinv = _inv_unit_lower(lower)
    else:
        tinv = [_inv_saved(lower[h], jnp.where(same, jnp.concatenate([tsaved[h]] * (GROUP_T // CHUNK), axis=1), 0.0))
                for h in heads]
    egc = [jnp.exp(gc[h]) for h in heads]
    u = [_dot(tinv[h], vb[h]) for h in heads]
    w = [_dot(tinv[h], kb[h] * egc[h]) for h in heads]
    a = [_fold_blocks(jnp.where(causal, _dot(q[h], k[h], _NT) * decay[h], 0.0)) for h in heads]
    k_tail = [k[h] * jnp.exp(glast[h] - gc[h]) for h in heads]
    q_dec = [q[h] * egc[h] for h in heads]
    return u, w, q_dec, k_tail, a, glast, [_fold_blocks(t) for t in tinv]


def _head_slices(ref, width):
    return [ref[:, h * width:(h + 1) * width] for h in range(N_HEADS_B)]


def _store_heads(ref, vals, width):
    for h, val in enumerate(vals):
        ref[:, h * width:(h + 1) * width] = val


def _dn_prep(q, k, v, gb, bb, *, side=None, name):
    t = q.shape[0]

    def body(q_ref, k_ref, v_ref, gb_ref, bb_ref, u_ref, w_ref, qd_ref, kt_ref, a_ref, gl_ref, ti_ref):
        outs = _dn_prep_heads(*[_head_slices(r, KEY_DIM_B) for r in (q_ref, k_ref, v_ref, gb_ref, bb_ref)])
        for ref, vals in zip((u_ref, w_ref, qd_ref, kt_ref, a_ref, gl_ref, ti_ref), outs):
            _store_heads(ref, vals, vals[0].shape[1])

    row = pl.BlockSpec((GROUP_T, V_B), lambda i: (i, 0))
    arow = pl.BlockSpec((GROUP_T, N_HEADS_B * CHUNK), lambda i: (i, 0))
    big = jax.ShapeDtypeStruct((t, V_B), F32)
    small = jax.ShapeDtypeStruct((t, N_HEADS_B * CHUNK), F32)
    outs, side_outs = _call(
        body, name=name, grid=(t // GROUP_T,), in_specs=[row] * 5, out_specs=[row, row, row, row, arow, row, arow],
        out_shape=[big, big, big, big, small, big, small], semantics=("parallel",), args=(q, k, v, gb, bb), side=side)
    return tuple(outs), side_outs


def _dn_prep_bwd(q, k, v, gb, bb, ti, du, dw, dqd, dkt, da, dgl, *, side=None, name):
    t = q.shape[0]

    def body(q_ref, k_ref, v_ref, gb_ref, bb_ref, ti_ref, du_ref, dw_ref, dqd_ref, dkt_ref, da_ref, dgl_ref,
             dq_ref, dk_ref, dv_ref, dgb_ref, dbb_ref):
        for lo in range(0, N_HEADS_B, DN_BWD_HEADS):
            grp = slice(lo, lo + DN_BWD_HEADS)
            tsaved = _head_slices(ti_ref, CHUNK)[grp]
            _, vjp = jax.vjp(lambda *a, ts=tsaved: _dn_prep_heads(*a, tsaved=ts)[:6],
                             *[_head_slices(r, KEY_DIM_B)[grp] for r in (q_ref, k_ref, v_ref, gb_ref, bb_ref)])
            cot = tuple(_head_slices(r, CHUNK if r is da_ref else KEY_DIM_B)[grp]
                        for r in (du_ref, dw_ref, dqd_ref, dkt_ref, da_ref, dgl_ref))
            for ref, vals in zip((dq_ref, dk_ref, dv_ref, dgb_ref, dbb_ref), vjp(cot)):
                for h, val in enumerate(vals):
                    ref[:, (lo + h) * KEY_DIM_B:(lo + h + 1) * KEY_DIM_B] = val

    row = pl.BlockSpec((GROUP_T, V_B), lambda i: (i, 0))
    arow = pl.BlockSpec((GROUP_T, N_HEADS_B * CHUNK), lambda i: (i, 0))
    big = jax.ShapeDtypeStruct((t, V_B), F32)
    outs, side_outs = _call(
        body, name=name, grid=(t // GROUP_T,),
        in_specs=[row] * 5 + [arow] + [row] * 4 + [arow, row], out_specs=[row] * 5, out_shape=[big] * 5,
        semantics=("parallel",), args=(q, k, v, gb, bb, ti, du, dw, dqd, dkt, da, dgl), side=side)
    return tuple(outs), side_outs


def _dn_steps(s, qd, kt, u, w, a, gl):
    heads = range(len(s))
    v_new = [u[h] - _dot(w[h], s[h]) for h in heads]
    qs = [_dot(qd[h], s[h]) for h in heads]
    o = [qs[h] + _dot(a[h], v_new[h]) for h in heads]
    s_new = [s[h] * jnp.exp(gl[h][0:1, :]) + _dot(kt[h], v_new[h], _TN) for h in heads]
    return s_new, o


def _dn_scan(u, w, qd, kt, a, gl, nbatch, *, name):
    t = u.shape[0]
    ng = t // nbatch // GROUP_T
    cpg = GROUP_T // CHUNK

    def body(u_ref, w_ref, qd_ref, kt_ref, a_ref, gl_ref, o_ref, ss_ref, s_ref):
        @pl.when(pl.program_id(1) == 0)
        def _():
            s_ref[...] = jnp.zeros_like(s_ref)

        def chunk(c, carry):
            rows = pl.ds(pl.multiple_of(c * CHUNK, CHUNK), CHUNK)
            heads = range(N_HEADS_B)
            s = [s_ref[h] for h in heads]
            for h in heads:
                ss_ref[c, h] = s[h]
            s_new, o = _dn_steps(s, *[[r[rows, h * wd:(h + 1) * wd] for h in heads] for r, wd in
                                      ((qd_ref, KEY_DIM_B), (kt_ref, KEY_DIM_B), (u_ref, VAL_DIM_B), (w_ref, KEY_DIM_B),
                                       (a_ref, CHUNK), (gl_ref, VAL_DIM_B))])
            for h in heads:
                s_ref[h] = s_new[h]
                o_ref[rows, h * VAL_DIM_B:(h + 1) * VAL_DIM_B] = o[h]
            return carry

        lax.fori_loop(0, cpg, chunk, 0)

    row = pl.BlockSpec((GROUP_T, V_B), lambda b, i: (b * ng + i, 0))
    arow = pl.BlockSpec((GROUP_T, N_HEADS_B * CHUNK), lambda b, i: (b * ng + i, 0))
    return pl.pallas_call(
        body, name=name, grid=(nbatch, ng),
        in_specs=[row, row, row, row, arow, row],
        out_specs=[row, pl.BlockSpec((cpg, N_HEADS_B, KEY_DIM_B, VAL_DIM_B), lambda b, i: (b * ng + i, 0, 0, 0))],
        out_shape=[jax.ShapeDtypeStruct((t, V_B), F32),
                   jax.ShapeDtypeStruct((t // CHUNK, N_HEADS_B, KEY_DIM_B, VAL_DIM_B), F32)],
        scratch_shapes=[pltpu.VMEM((N_HEADS_B, KEY_DIM_B, VAL_DIM_B), F32)],
        compiler_params=_params(dimension_semantics=("parallel", "arbitrary")),
    )(u, w, qd, kt, a, gl)


def _dn_scan_bwd(u, w, qd, kt, a, gl, ss, do, nbatch, *, name):
    t = u.shape[0]
    ng = t // nbatch // GROUP_T
    cpg = GROUP_T // CHUNK

    def body(u_ref, w_ref, qd_ref, kt_ref, a_ref, gl_ref, ss_ref, do_ref,
             du_ref, dw_ref, dqd_ref, dkt_ref, da_ref, dgl_ref, ds_ref):
        @pl.when(pl.program_id(1) == 0)
        def _():
            ds_ref[...] = jnp.zeros_like(ds_ref)

        def chunk(cc, carry):
            c = cpg - 1 - cc
            rows = pl.ds(pl.multiple_of(c * CHUNK, CHUNK), CHUNK)
            heads = range(N_HEADS_B)
            ins = ((qd_ref, KEY_DIM_B), (kt_ref, KEY_DIM_B), (u_ref, VAL_DIM_B), (w_ref, KEY_DIM_B), (a_ref, CHUNK),
                   (gl_ref, VAL_DIM_B))
            _, vjp = jax.vjp(_dn_steps, [ss_ref[c, h] for h in heads],
                             *[[r[rows, h * wd:(h + 1) * wd] for h in heads] for r, wd in ins])
            grads = vjp(([ds_ref[h] for h in heads], [do_ref[rows, h * VAL_DIM_B:(h + 1) * VAL_DIM_B] for h in heads]))
            for h in heads:
                ds_ref[h] = grads[0][h]
            outs = ((dqd_ref, KEY_DIM_B), (dkt_ref, KEY_DIM_B), (du_ref, VAL_DIM_B), (dw_ref, KEY_DIM_B), (da_ref, CHUNK),
                    (dgl_ref, VAL_DIM_B))
            for (r, wd), vals in zip(outs, grads[1:]):
                for h in heads:
                    r[rows, h * wd:(h + 1) * wd] = vals[h]
            return carry

        lax.fori_loop(0, cpg, chunk, 0)

    row = pl.BlockSpec((GROUP_T, V_B), lambda b, j: (b * ng + ng - 1 - j, 0))
    arow = pl.BlockSpec((GROUP_T, N_HEADS_B * CHUNK), lambda b, j: (b * ng + ng - 1 - j, 0))
    big = jax.ShapeDtypeStruct((t, V_B), F32)
    return pl.pallas_call(
        body, name=name, grid=(nbatch, ng),
        in_specs=[row, row, row, row, arow, row,
                  pl.BlockSpec((cpg, N_HEADS_B, KEY_DIM_B, VAL_DIM_B), lambda b, j: (b * ng + ng - 1 - j, 0, 0, 0)), row],
        out_specs=[row, row, row, row, arow, row],
        out_shape=[big, big, big, big, jax.ShapeDtypeStruct((t, N_HEADS_B * CHUNK), F32), big],
        scratch_shapes=[pltpu.VMEM((N_HEADS_B, KEY_DIM_B, VAL_DIM_B), F32)],
        compiler_params=_params(dimension_semantics=("parallel", "arbitrary")),
    )(u, w, qd, kt, a, gl, ss, do)


def _rms_gate(o, hcat, dn_g, *, name):
    t = o.shape[0]

    def body(o_ref, z_ref, g_ref, y_ref):
        for h in range(N_HEADS_B):
            sl = slice(h * VAL_DIM_B, (h + 1) * VAL_DIM_B)
            o_ = o_ref[:, sl]
            r = lax.rsqrt(jnp.mean(o_ * o_, axis=-1, keepdims=True) + NORM_EPS)
            y_ref[:, sl] = (o_ * r * g_ref[...] * _silu(z_ref[:, sl])).astype(BF16)

    row = pl.BlockSpec((ROW_T, V_B), lambda i: (i, 0))
    return pl.pallas_call(
        body, name=name, grid=(t // ROW_T,),
        in_specs=[row, pl.BlockSpec((ROW_T, V_B), lambda i: (i, HC_Z // V_B)), pl.BlockSpec((1, VAL_DIM_B), lambda i: (0, 0))],
        out_specs=row, out_shape=jax.ShapeDtypeStruct((t, V_B), BF16),
        compiler_params=_params(dimension_semantics=("parallel",)),
    )(o, hcat, dn_g.reshape(1, VAL_DIM_B))


def _rms_gate_bwd(dy, o, hcat, dn_g, *, name):
    t = o.shape[0]

    def body(dy_ref, o_ref, z_ref, g_ref, do_ref, dz_ref, dg_ref):
        i = pl.program_id(0)
        g = g_ref[...]
        dg_p = jnp.zeros((1, VAL_DIM_B), F32)
        for h in range(N_HEADS_B):
            sl = slice(h * VAL_DIM_B, (h + 1) * VAL_DIM_B)
            o_ = o_ref[:, sl]
            z_ = z_ref[:, sl]
            dy_ = dy_ref[:, sl]
            r = lax.rsqrt(jnp.mean(o_ * o_, axis=-1, keepdims=True) + NORM_EPS)
            n = o_ * r
            sz = _silu(z_)
            dz_ref[:, sl] = (dy_ * n * g * _dsilu(z_)).astype(BF16)
            dg_p = dg_p + jnp.sum(dy_ * n * sz, axis=0, keepdims=True)
            dn = dy_ * g * sz
            do_ref[:, sl] = r * dn - o_ * (r * r * r) * jnp.mean(o_ * dn, axis=-1, keepdims=True)

        @pl.when(i == 0)
        def _():
            dg_ref[...] = dg_p

        @pl.when(i > 0)
        def _():
            dg_ref[...] += dg_p

    row = pl.BlockSpec((ROW_T, V_B), lambda i: (i, 0))
    vec = pl.BlockSpec((1, VAL_DIM_B), lambda i: (0, 0))
    return pl.pallas_call(
        body, name=name, grid=(t // ROW_T,),
        in_specs=[row, row, pl.BlockSpec((ROW_T, V_B), lambda i: (i, HC_Z // V_B)), vec],
        out_specs=[row, row, vec],
        out_shape=[jax.ShapeDtypeStruct((t, V_B), F32), jax.ShapeDtypeStruct((t, V_B), BF16),
                   jax.ShapeDtypeStruct((1, VAL_DIM_B), F32)],
        compiler_params=_params(dimension_semantics=("arbitrary",)),
    )(dy, o, hcat, dn_g.reshape(1, VAL_DIM_B))


def _merge(ya, yb, hcat, *, name):
    t = ya.shape[0]

    def body(ya_ref, yb_ref, ga_ref, gb_ref, y_ref):
        y_ref[...] = (_sigmoid(ga_ref[...]) * ya_ref[...] + _sigmoid(gb_ref[...]) * yb_ref[...]).astype(BF16)

    row = pl.BlockSpec((ROW_T, D_MODEL), lambda i: (i, 0))
    return pl.pallas_call(
        body, name=name, grid=(t // ROW_T,),
        in_specs=[row, row, pl.BlockSpec((ROW_T, D_MODEL), lambda i: (i, HC_GATE // D_MODEL)),
                  pl.BlockSpec((ROW_T, D_MODEL), lambda i: (i, HC_GATE // D_MODEL + 1))],
        out_specs=row, out_shape=jax.ShapeDtypeStruct((t, D_MODEL), BF16),
        compiler_params=_params(dimension_semantics=("parallel",)),
    )(ya, yb, hcat, hcat)


def _merge_bwd(dmix, ya, yb, hcat, *, name):
    t = ya.shape[0]

    def body(d_ref, ya_ref, yb_ref, ga_ref, gb_ref, dya_ref, dyb_ref, dgate_ref):
        d = d_ref[...]
        sa = _sigmoid(ga_ref[...])
        sb = _sigmoid(gb_ref[...])
        dya_ref[...] = (d * sa).astype(BF16)
        dyb_ref[...] = (d * sb).astype(BF16)
        dgate_ref[:, :D_MODEL] = (d * ya_ref[...] * sa * (1.0 - sa)).astype(BF16)
        dgate_ref[:, D_MODEL:] = (d * yb_ref[...] * sb * (1.0 - sb)).astype(BF16)

    row = pl.BlockSpec((ROW_T, D_MODEL), lambda i: (i, 0))
    return pl.pallas_call(
        body, name=name, grid=(t // ROW_T,),
        in_specs=[row, row, row, pl.BlockSpec((ROW_T, D_MODEL), lambda i: (i, HC_GATE // D_MODEL)),
                  pl.BlockSpec((ROW_T, D_MODEL), lambda i: (i, HC_GATE // D_MODEL + 1))],
        out_specs=[row, row, pl.BlockSpec((ROW_T, 2 * D_MODEL), lambda i: (i, 0))],
        out_shape=[jax.ShapeDtypeStruct((t, D_MODEL), BF16)] * 2 + [jax.ShapeDtypeStruct((t, 2 * D_MODEL), BF16)],
        compiler_params=_params(dimension_semantics=("parallel",)),
    )(dmix, ya, yb, hcat, hcat)


def _loss_head(y, target, *, name):
    t, n = y.shape
    tm = _tile(t, (512, 256, 128))

    def body(y_ref, t_ref, part_ref, dy_ref):
        i = pl.program_id(0)
        e = y_ref[...] - t_ref[...]
        dy_ref[...] = e * (1.0 / n)
        p = jnp.sum((e * e).reshape(tm // 8, 8, n), axis=0) * (0.5 / n)

        @pl.when(i == 0)
        def _():
            part_ref[...] = p

        @pl.when(i > 0)
        def _():
            part_ref[...] += p

    row = pl.BlockSpec((tm, n), lambda i: (i, 0))
    return pl.pallas_call(
        body, name=name, grid=(t // tm,),
        in_specs=[row, row], out_specs=[pl.BlockSpec((8, n), lambda i: (0, 0)), row],
        out_shape=[jax.ShapeDtypeStruct((8, n), F32), jax.ShapeDtypeStruct((t, n), F32)],
        compiler_params=_params(dimension_semantics=("arbitrary",)),
    )(y, target)


def _adamw_math(w, g, m, v):
    nm = ADAM_B1 * m + (1.0 - ADAM_B1) * g
    nv = ADAM_B2 * v + (1.0 - ADAM_B2) * (g * g)
    m_hat = nm / (1.0 - ADAM_B1 ** ADAM_STEP)
    v_hat = nv / (1.0 - ADAM_B2 ** ADAM_STEP)
    return -ADAM_LR * (m_hat / (jnp.sqrt(v_hat) + ADAM_EPS) + ADAM_WD * w), nm, nv


def _adamw(w, g, m, v, *, name):
    shape = w.shape
    cols = shape[-1]
    rows = int(np.prod(shape[:-1]))
    w2, g2, m2, v2 = (a.reshape(rows, cols) for a in (w, g, m, v))
    tr = rows
    if rows * cols > 512 * 1024:
        tr = _tile(rows, tuple(c for c in (512, 256, 128, 64, 32, 16, 8) if c * cols <= 256 * 1024))

    def body(w_ref, g_ref, m_ref, v_ref, d_ref, nm_ref, nv_ref):
        d_ref[...], nm_ref[...], nv_ref[...] = _adamw_math(w_ref[...], g_ref[...], m_ref[...], v_ref[...])

    blk = pl.BlockSpec((tr, cols), lambda i: (i, 0))
    outs = pl.pallas_call(
        body, name=name, grid=(rows // tr,),
        in_specs=[blk] * 4, out_specs=[blk] * 3,
        out_shape=[jax.ShapeDtypeStruct((rows, cols), F32)] * 3,
        compiler_params=_params(dimension_semantics=("parallel",)),
    )(w2, g2, m2, v2)
    return tuple(o.reshape(shape) for o in outs)


def _repack_w_in(w_in):
    d = w_in.shape[0]
    o = 0
    parts = {}
    for nm, wd in (("q", Q_A), ("k", KV_W), ("v", KV_W), ("conv", CONV_CH), ("beta", N_HEADS_B), ("dt", N_HEADS_B),
                   ("z", V_B), ("gate", 2 * D_MODEL)):
        parts[nm] = w_in[:, o:o + wd]
        o += wd
    z = lambda n: jnp.zeros((d, n), w_in.dtype)
    return jnp.concatenate([parts["q"], parts["z"], parts["k"], parts["v"], parts["beta"], parts["dt"],
                            z(128 - 2 * N_HEADS_B), z(HC_CONV - HC_BD - 128), parts["conv"], parts["gate"]], axis=1)


MATRIX_NAMES = ("ffn_w13", "ffn_w2", "w_in", "w_branch_a", "w_branch_b", "w_out")
GATHER_BESIDE_IN_PROJ = ("w_in", "conv_w", "w_branch_a", "w_branch_b", "w_out")


def _dw_in_by_owner(dw):
    sections = ((Q_A, HC_Q), (2 * KV_W, HC_K), (CONV_CH, HC_CONV), (2 * N_HEADS_B, HC_BD), (V_B, HC_Z),
                (2 * D_MODEL, HC_GATE))
    per = N_IN // 4
    owners = []
    for o in range(4):
        lo, hi, start, parts = o * per, (o + 1) * per, 0, []
        for width, off in sections:
            a, b = max(lo, start), min(hi, start + width)
            if a < b:
                parts.append(dw[:, off + a - start:off + b - start])
            start += width
        owners.append(jnp.concatenate(parts, axis=1))
    return jnp.stack(owners)


def _lane_row(vals):
    return jnp.pad(vals.astype(F32).reshape(1, N_HEADS_B), ((0, 0), (N_HEADS_B, 128 - 2 * N_HEADS_B)))


def _local_step(x, target, rel_bias, layer_wts, side_shards=None, side_assemble=None, reducer=None):
    nbatch, seq, d = x.shape
    t = nbatch * seq
    depth = len(layer_wts)
    layer_wts = list(layer_wts)
    x0 = x.reshape(t, d)
    tgt = target.reshape(t, d)

    onehot = jnp.asarray(_bucket_onehot())
    rel_t = jnp.pad(rel_bias.T, ((0, 0), (0, 128 - NUM_BUCKETS)))
    bias = _mm(rel_t, onehot, tb=True, exact=True, name="pos_bias")
    bias = bias.reshape(N_HEADS_A, WINDOW, 2 * WINDOW)

    saved = []
    xin, xin_b = x0, x0.astype(BF16)
    for i in range(depth):
        L = {}
        W = layer_wts[i]
        tag = f"_l{i}"
        a = _ffn_up_act(xin_b, W["ffn_w13"][0], name="ffn_up_act" + tag + "a")
        r1, x1, x1_b = _mm_res_ln(a, W["ffn_w2"][0], xin, W["ln_g"][0], W["ln_b"][0],
                                  alpha=DN_ALPHA, c=0.5, name="ffn_down_ln" + tag + "a")
        L.update(x0_b=xin_b, a0=a, r1=r1, x1=x1, x1_b=x1_b)
        gather_next = i + 1 < depth and layer_wts[i + 1] is None
        gathered = {}
        if gather_next:
            names1 = [n for n in side_shards[i + 1] if n in GATHER_BESIDE_IN_PROJ]
            hcat, g1 = _mm(x1_b, W["w_in_p"], side=_gather_job([side_shards[i + 1][n] for n in names1]),
                           name="in_proj" + tag)
            gathered.update(zip(names1, g1))
        else:
            hcat = _mm(x1_b, W["w_in_p"], name="in_proj" + tag)
        ao = _attn_fwd(hcat, bias, W["sinks"], nbatch, name="swa" + tag)
        ya = _mm(ao, W["w_branch_a"], name="branch_a" + tag)
        c, qn, kn, vs = _conv_prep(hcat, W["conv_w"], nbatch, name="conv_prep" + tag)
        a_row = _lane_row(W["a_log"])
        dt_row = _lane_row(W["dt_bias"])
        gb, bb = _gates(hcat, a_row, dt_row, name="gates" + tag)
        if gather_next:
            names2 = [n for n in side_shards[i + 1] if n not in GATHER_BESIDE_IN_PROJ]
            (u, w, qd, kt, aa, gl, ti), g2 = _dn_prep(qn, kn, vs, gb, bb, name="dn_prep" + tag,
                                                      side=_gather_job([side_shards[i + 1][n] for n in names2]))
            gathered.update(zip(names2, g2))
            layer_wts[i + 1] = side_assemble(i + 1, gathered)
        else:
            (u, w, qd, kt, aa, gl, ti), _ = _dn_prep(qn, kn, vs, gb, bb, name="dn_prep" + tag)
        o, ss = _dn_scan(u, w, qd, kt, aa, gl, nbatch, name="dn_scan" + tag)
        on = _rms_gate(o, hcat, W["dn_norm_g"], name="rms_gate" + tag)
        yb = _mm(on, W["w_branch_b"], name="branch_b" + tag)
        mix = _merge(ya, yb, hcat, name="merge" + tag)
        r2, x2, x2_b = _mm_res_ln(mix, W["w_out"], x1, W["ln_g"][1], W["ln_b"][1],
                                  alpha=DN_ALPHA, c=1.0, name="out_proj_ln" + tag)
        L.update(hcat=hcat, ao=ao, ya=ya, c=c, qn=qn, kn=kn, vs=vs, gb=gb, bb=bb, a_row=a_row, dt_row=dt_row,
                 u=u, w=w, qd=qd, kt=kt, aa=aa, gl=gl, ti=ti, o=o, ss=ss, on=on, yb=yb, mix=mix, r2=r2, x2_b=x2_b)
        a = _ffn_up_act(x2_b, W["ffn_w13"][1], name="ffn_up_act" + tag + "b")
        r3, x3, x3_b = _mm_res_ln(a, W["ffn_w2"][1], x2, W["ln_g"][2], W["ln_b"][2],
                                  alpha=DN_ALPHA, c=0.5, name="ffn_down_ln" + tag + "b")
        L.update(a1=a, r3=r3)
        saved.append(L)
        xin, xin_b = x3, x3_b

    part, dy = _loss_head(xin, tgt, name="loss_head")
    loss = jnp.sum(part)

    grads = {k: [None] * depth for k in ("ln_g", "ln_b", "ffn_w13", "ffn_w2", "w_in", "conv_w", "a_log", "dt_bias",
                                          "dn_norm_g", "sinks", "w_branch_a", "w_branch_b", "w_out")}
    dbias_total = None
    for i in reversed(range(depth)):
        L = saved[i]
        W = layer_wts[i]
        tag = f"_l{i}"
        dln_g, dln_b, dw13, dw2 = [None] * 3, [None] * 3, [None] * 2, [None] * 2

        def ffn_bwd(dyo, r, xprev_b, asave, j, sfx):
            dres, df, dln_g[2 * j], dln_b[2 * j] = _ln_bwd(dyo, r, W["ln_g"][2 * j], alpha=DN_ALPHA, c=0.5,
                                                           name="ln_bwd" + tag + sfx)
            job = reducer.job_a() if (reducer is not None and j == 1) else None
            dh, swapped = _ffn_bwd_mid(xprev_b, W["ffn_w13"][j], df, W["ffn_w2"][j], side=job,
                                       name="ffn_bwd_mid" + tag + sfx)
            if job is not None:
                reducer.done_a(swapped)
            dw2[j] = _mm(asave, df, ta=True, name="ffn_w2_grad" + tag + sfx)
            dw13[j] = _mm(xprev_b, dh, ta=True, b_halves=True, name="ffn_w13_grad" + tag + sfx)
            return _mm(dh, W["ffn_w13"][j], tb=True, a_halves=True, add=dres, name="ffn_up_bwd" + tag + sfx)

        dx2 = ffn_bwd(dy, L["r3"], L["x2_b"], L["a1"], 1, "b")

        dres2, dymix, dln_g[1], dln_b[1] = _ln_bwd(dx2, L["r2"], W["ln_g"][1], alpha=DN_ALPHA, c=1.0,
                                                   name="ln_bwd" + tag + "m")
        hcat = L["hcat"]
        dmix = _mm(dymix, W["w_out"], tb=True, name="out_proj_bwd" + tag)
        grads["w_out"][i] = _mm(L["mix"], dymix, ta=True, name="w_out_grad" + tag)
        dya, dyb, dgate = _merge_bwd(dmix, L["ya"], L["yb"], hcat, name="merge_bwd" + tag)
        dao = _mm(dya, W["w_branch_a"], tb=True, name="branch_a_bwd" + tag)
        grads["w_branch_a"][i] = _mm(L["ao"], dya, ta=True, name="w_branch_a_grad" + tag)
        don = _mm(dyb, W["w_branch_b"], tb=True, name="branch_b_bwd" + tag)
        grads["w_branch_b"][i] = _mm(L["on"], dyb, ta=True, name="w_branch_b_grad" + tag)
        do, dz, ddn = _rms_gate_bwd(don, L["o"], hcat, W["dn_norm_g"], name="rms_gate_bwd" + tag)
        grads["dn_norm_g"][i] = ddn.reshape(VAL_DIM_B)
        du, dw, dqd, dkt, daa, dgl = _dn_scan_bwd(L["u"], L["w"], L["qd"], L["kt"], L["aa"], L["gl"], L["ss"], do,
                                                  nbatch, name="dn_scan_bwd" + tag)
        job = reducer.job_b() if reducer is not None else None
        (dqn, dkn, dvs, dgb, dbb), exchanged = _dn_prep_bwd(L["qn"], L["kn"], L["vs"], L["gb"], L["bb"], L["ti"], du, dw,
                                                            dqd, dkt, daa, dgl, side=job, name="dn_prep_bwd" + tag)
        if job is not None:
            reducer.done_b(exchanged)
        dc = _conv_prep_bwd_pointwise(dqn, dkn, dvs, L["c"], name="conv_prep_bwd" + tag)
        dconv, dconv_w = _conv_bwd(dc, hcat, W["conv_w"], nbatch, name="conv_bwd" + tag)
        grads["conv_w"][i] = dconv_w[:CONV_K]
        dbd, da_log, ddt = _gates_bwd(dgb, dbb, hcat, L["a_row"], L["dt_row"], name="gates_bwd" + tag)
        grads["a_log"][i] = da_log[0, N_HEADS_B:2 * N_HEADS_B]
        grads["dt_bias"][i] = ddt[0, N_HEADS_B:2 * N_HEADS_B]
        dq, dk, dv, dbias, dsink = _attn_bwd(hcat, bias, W["sinks"], dao, nbatch, name="swa_bwd" + tag)
        grads["sinks"][i] = dsink.reshape(N_HEADS_A)
        dbias_total = dbias if dbias_total is None else dbias_total + dbias
        dhcat = jnp.concatenate([dq, dz, dk, dv, dbd, jnp.zeros((t, HC_CONV - HC_BD - 128), BF16), dconv, dgate], axis=1)
        dw_in_p = _mm(L["x1_b"], dhcat, ta=True, name="w_in_grad" + tag)
        grads["w_in"][i] = _dw_in_by_owner(dw_in_p)
        dx1 = _mm(dhcat, W["w_in_p"], tb=True, add=dres2, name="in_proj_bwd" + tag)

        dy = ffn_bwd(dx1, L["r1"], L["x0_b"], L["a0"], 0, "a")
        grads["ln_g"][i] = jnp.concatenate(dln_g, axis=0)
        grads["ln_b"][i] = jnp.concatenate(dln_b, axis=0)
        grads["ffn_w13"][i] = dw13
        grads["ffn_w2"][i] = dw2
        if reducer is not None:
            reducer.layer_done(i, {n: grads[n][i] for n in MATRIX_NAMES})

    out = {k: (v if k in MATRIX_NAMES else jnp.stack(v)) for k, v in grads.items()}
    drel = _mm(dbias_total.reshape(N_HEADS_A, WINDOW * 2 * WINDOW), onehot, name="rel_bias_grad")
    out["rel_bias"] = drel[:, :NUM_BUCKETS].T
    return loss, dy.reshape(nbatch, seq, d), out


N_CHIPS = 4
MESH_ID = pl.DeviceIdType.MESH
HBM_SPEC = pl.BlockSpec(memory_space=pltpu.HBM)


def _place():
    x, y, c = lax.axis_index("x"), lax.axis_index("y"), lax.axis_index("c")
    others = [(1 - x, y), (x, 1 - y), (1 - x, 1 - y)]
    return x, y, c, others


def _chip_index(cx, cy):
    return 2 * cx + cy


def _gather_sems(n):
    return [pltpu.SemaphoreType.DMA((n, 3)), pltpu.SemaphoreType.DMA((n, 3)), pltpu.SemaphoreType.DMA((n,))]


def _gather_copies(ins, outs, send_sems, recv_sems, local_sems):
    x, y, c, others = _place()
    me = _chip_index(x, y)
    copies = []
    for i in range(len(ins)):
        copies.append(pltpu.make_async_copy(ins[i], outs[i].at[me], local_sems.at[i]))
        for k, (ox, oy) in enumerate(others):
            copies.append(pltpu.make_async_remote_copy(src_ref=ins[i], dst_ref=outs[i].at[me], send_sem=send_sems.at[i, k],
                                                       recv_sem=recv_sems.at[i, k], device_id=(ox, oy, c),
                                                       device_id_type=MESH_ID))
    return copies


def _gather_job(tensors):
    return dict(ins=list(tensors), out_shape=[jax.ShapeDtypeStruct((N_CHIPS,) + t.shape, t.dtype) for t in tensors],
                scratch=_gather_sems(len(tensors)), make=_gather_copies)


def _run_job(job, *, name):
    n_in, n_out = len(job["ins"]), len(job["out_shape"])

    def body(*refs):
        copies = job["make"](refs[:n_in], refs[n_in:n_in + n_out], *refs[n_in + n_out:])
        for cp in copies:
            cp.start()
        for cp in copies:
            cp.wait()

    return pl.pallas_call(
        body, name=name, in_specs=[HBM_SPEC] * n_in, out_specs=[HBM_SPEC] * n_out,
        out_shape=list(job["out_shape"]), scratch_shapes=list(job["scratch"]),
    )(*job["ins"])


def _allgather_devices(v, *, name):
    def body(v_ref, o_ref, send_sems, recv_sems, local_sem):
        x, y, c, _ = _place()
        me = 4 * x + 2 * y + c
        loc = pltpu.make_async_copy(v_ref, o_ref.at[me], local_sem)
        loc.start()
        copies = [loc]
        for k in range(1, 8):
            px, py, pc = x ^ (k >> 2), y ^ ((k >> 1) & 1), c ^ (k & 1)
            cp = pltpu.make_async_remote_copy(src_ref=v_ref, dst_ref=o_ref.at[me], send_sem=send_sems.at[k - 1],
                                              recv_sem=recv_sems.at[k - 1], device_id=(px, py, pc), device_id_type=MESH_ID)
            cp.start()
            copies.append(cp)
        for cp in copies:
            cp.wait()

    return pl.pallas_call(
        body, name=name, in_specs=[HBM_SPEC], out_specs=HBM_SPEC,
        out_shape=jax.ShapeDtypeStruct((8,) + v.shape, v.dtype),
        scratch_shapes=[pltpu.SemaphoreType.DMA((7,)), pltpu.SemaphoreType.DMA((7,)), pltpu.SemaphoreType.DMA],
    )(v)


def _sum_slots(g, *, name):
    nb, n, r, l = g.shape
    tr = r // 2 if r % 32 == 0 else r

    def body(g_ref, o_ref):
        acc = g_ref[0].astype(F32)
        for k in range(1, n):
            acc = acc + g_ref[k].astype(F32)
        o_ref[...] = acc

    return pl.pallas_call(
        body, name=name, grid=(nb, r // tr),
        in_specs=[pl.BlockSpec((None, n, tr, l), lambda b, i: (b, 0, i, 0))],
        out_specs=pl.BlockSpec((None, tr, l), lambda b, i: (b, i, 0)),
        out_shape=jax.ShapeDtypeStruct((nb, r, l), F32),
        compiler_params=_params(dimension_semantics=("parallel", "parallel")),
    )(g)


def _half_window(ref, kind, h):
    if kind == "rows":
        return ref.at[:, h]
    r = ref.shape[0] // 2
    return ref.at[pl.ds(pl.multiple_of(h * r, r), r), :]


def _owner_window(ref, kind, o):
    if kind == "rows":
        return ref.at[o]
    cols = ref.shape[1] // N_CHIPS
    return ref.at[:, pl.ds(pl.multiple_of(o * cols, cols), cols)]


def _half_shape(g, kind):
    return (g.shape[0],) + g.shape[2:] if kind == "rows" else (g.shape[0] // 2, g.shape[1])


def _swap_job(gs, kinds):
    n = len(gs)

    def make(ins, outs, send_sems, recv_sems):
        x, y, c, _ = _place()
        return [pltpu.make_async_remote_copy(src_ref=_half_window(ins[i], kinds[i], 1 - c), dst_ref=outs[i],
                                             send_sem=send_sems.at[i], recv_sem=recv_sems.at[i],
                                             device_id=(x, y, 1 - c), device_id_type=MESH_ID) for i in range(n)]

    return dict(ins=list(gs), out_shape=[jax.ShapeDtypeStruct(_half_shape(g, k), g.dtype) for g, k in zip(gs, kinds)],
                scratch=[pltpu.SemaphoreType.DMA((n,)), pltpu.SemaphoreType.DMA((n,))], make=make)


def _pair_sum(g, got, kind, c_idx, *, name):
    hs = _half_shape(g, kind)

    def body(c_ref, g_ref, r_ref, o_ref):
        o_ref[...] = (g_ref[...] + r_ref[...]).astype(BF16)

    if kind == "rows":
        _, _, r, cols = g.shape
        grid = (N_CHIPS,)
        in_specs = [pl.BlockSpec((None, None, r, cols), lambda o, c_ref: (o, c_ref[0], 0, 0)),
                    pl.BlockSpec((None, r, cols), lambda o, c_ref: (o, 0, 0))]
        out_spec = pl.BlockSpec((None, r, cols), lambda o, c_ref: (o, 0, 0))
    else:
        r, cols = hs
        steps = 4
        tr = r // steps
        grid = (steps,)
        in_specs = [pl.BlockSpec((tr, cols), lambda i, c_ref: (c_ref[0] * steps + i, 0)),
                    pl.BlockSpec((tr, cols), lambda i, c_ref: (i, 0))]
        out_spec = pl.BlockSpec((tr, cols), lambda i, c_ref: (i, 0))
    return pl.pallas_call(
        body, name=name,
        grid_spec=pltpu.PrefetchScalarGridSpec(num_scalar_prefetch=1, grid=grid, in_specs=in_specs, out_specs=out_spec),
        out_shape=jax.ShapeDtypeStruct(hs, BF16),
        compiler_params=_params(dimension_semantics=("parallel",)),
    )(c_idx, g, got)


def _exchange_job(ss, kinds):
    n = len(ss)

    def shard_shape(s, kind):
        return s.shape[1:] if kind == "rows" else (s.shape[0], s.shape[1] // N_CHIPS)

    def make(ins, outs, send_sems, recv_sems, local_sems):
        x, y, c, others = _place()
        me = _chip_index(x, y)
        copies = []
        for i in range(n):
            dst = outs[i].at[me]
            copies.append(pltpu.make_async_copy(_owner_window(ins[i], kinds[i], me), dst, local_sems.at[i]))
            for k, (ox, oy) in enumerate(others):
                copies.append(pltpu.make_async_remote_copy(
                    src_ref=_owner_window(ins[i], kinds[i], _chip_index(ox, oy)), dst_ref=dst, send_sem=send_sems.at[i, k],
                    recv_sem=recv_sems.at[i, k], device_id=(ox, oy, c), device_id_type=MESH_ID))
        return copies

    return dict(ins=list(ss), out_shape=[jax.ShapeDtypeStruct((N_CHIPS,) + shard_shape(s, k), s.dtype)
                                         for s, k in zip(ss, kinds)],
                scratch=_gather_sems(n), make=make)


def _send_halves(fs, *, name):
    n = len(fs)

    def body(*refs):
        ins, outs, send_sems, recv_sems = refs[:n], refs[n:2 * n], refs[2 * n], refs[2 * n + 1]
        x, y, c, _ = _place()
        copies = [pltpu.make_async_remote_copy(src_ref=ins[i], dst_ref=outs[i], send_sem=send_sems.at[i],
                                               recv_sem=recv_sems.at[i], device_id=(x, y, 1 - c), device_id_type=MESH_ID)
                  for i in range(n)]
        for cp in copies:
            cp.start()
        for cp in copies:
            cp.wait()

    return pl.pallas_call(
        body, name=name, in_specs=[HBM_SPEC] * n, out_specs=[HBM_SPEC] * n,
        out_shape=[jax.ShapeDtypeStruct(f.shape, f.dtype) for f in fs],
        scratch_shapes=[pltpu.SemaphoreType.DMA((n,)), pltpu.SemaphoreType.DMA((n,))],
    )(*fs)


def _adamw_halves(w, m, v, own, other, c_idx, *, name):
    shape = w.shape
    nl, r, cols = own.shape
    w4, m4, v4 = (a.reshape(nl, 2, r, cols) for a in (w, m, v))
    tr = r if r * cols * 4 <= 3 * 512 * 1024 else _tile(r, tuple(c for c in (256, 128, 64, 32, 16, 8) if c * cols <= 256 * 1024))

    def body(c_ref, w_ref, m_ref, v_ref, own_ref, other_ref, g_ref, d_ref, nm_ref, nv_ref):
        g_ = jnp.where(pl.program_id(1) == c_ref[0], own_ref[...], other_ref[...])
        g_ref[...] = g_
        d_ref[...], nm_ref[...], nv_ref[...] = _adamw_math(w_ref[...], g_, m_ref[...], v_ref[...])

    full = pl.BlockSpec((None, None, tr, cols), lambda l, h, i, c_ref: (l, h, i, 0))
    half = pl.BlockSpec((None, tr, cols), lambda l, h, i, c_ref: (l, i, 0))
    outs = pl.pallas_call(
        body, name=name,
        grid_spec=pltpu.PrefetchScalarGridSpec(num_scalar_prefetch=1, grid=(nl, 2, r // tr),
                                               in_specs=[full, full, full, half, half], out_specs=[full] * 4),
        out_shape=[jax.ShapeDtypeStruct((nl, 2, r, cols), F32)] * 4,
        compiler_params=_params(dimension_semantics=("parallel", "parallel", "parallel")),
    )(c_idx, w4, m4, v4, own, other)
    return tuple(o.reshape(shape) for o in outs)


SHARD_AXIS = {"rel_bias": None, "ln_g": 2, "ln_b": 2, "ffn_w13": 3, "ffn_w2": 2, "w_in": 2, "conv_w": 2, "a_log": None,
              "dt_bias": None, "dn_norm_g": None, "sinks": None, "w_branch_a": 1, "w_branch_b": 1, "w_out": 1}
WEIGHT_NAMES = tuple(SHARD_AXIS)
SMALL_NAMES = tuple(n for n in WEIGHT_NAMES if n not in MATRIX_NAMES)
PACK_LANES = 1024


def _unshard(gathered, axis):
    g = jnp.moveaxis(gathered, 0, axis)
    return g.reshape(g.shape[:axis] + (g.shape[axis] * g.shape[axis + 1],) + g.shape[axis + 2:])


class _GradReducer:
    def __init__(self, c_idx):
        self.c_idx = c_idx
        self.swapping = None
        self.exchanging = None
        self.reduced = {}

    @staticmethod
    def _views(grads):
        out = []
        for n in MATRIX_NAMES:
            for g in (grads[n] if n in ("ffn_w13", "ffn_w2") else [grads[n]]):
                if n == "ffn_w13":
                    out.append((n, g, "cols"))
                elif n == "w_in":
                    out.append((n, g.reshape(N_CHIPS, 2, g.shape[1] // 2, g.shape[2]), "rows"))
                else:
                    out.append((n, g.reshape(N_CHIPS, 2, g.shape[0] // (2 * N_CHIPS), g.shape[1]), "rows"))
        return out

    def layer_done(self, layer, grads):
        assert self.swapping is None
        self.swapping = (layer, self._views(grads))

    def job_a(self):
        if self.swapping is None:
            return None
        _, views = self.swapping
        return _swap_job([g for _, g, _ in views], [k for _, _, k in views])

    def done_a(self, got):
        layer, views = self.swapping
        self.swapping = None
        assert self.exchanging is None
        ss = [_pair_sum(g, r, k, self.c_idx, name=f"rs_pair_sum_l{layer}_{i}")
              for i, ((_, g, k), r) in enumerate(zip(views, got))]
        self.exchanging = (layer, views, ss)

    def job_b(self):
        if self.exchanging is None:
            return None
        _, views, ss = self.exchanging
        return _exchange_job(ss, [k for _, _, k in views])

    def done_b(self, ex):
        layer, views, _ = self.exchanging
        self.exchanging = None
        red = {}
        for i, ((n, _, _), e) in enumerate(zip(views, ex)):
            red.setdefault(n, []).append(_sum_slots(e[None], name=f"rs_chip_sum_l{layer}_{i}")[0])
        self.reduced[layer] = red

    def flush(self):
        if self.swapping is not None:
            self.done_a(_run_job(self.job_a(), name="rs_swap_halves_last"))
        if self.exchanging is not None:
            self.done_b(_run_job(self.job_b(), name="rs_exchange_chips_last"))

    def result(self):
        self.flush()
        own = [jnp.stack([f for layer in sorted(self.reduced) for f in self.reduced[layer][n]]) for n in MATRIX_NAMES]
        other = _send_halves(own, name="rs_send_halves")
        return {n: (a, b) for n, a, b in zip(MATRIX_NAMES, own, other)}


def _reduce_small(grads):
    flat = [grads[n].astype(F32).reshape(-1) for n in SMALL_NAMES]
    total = sum(f.shape[0] for f in flat)
    rows = -(-total // (16 * PACK_LANES)) * 16
    vec = jnp.concatenate(flat + [jnp.zeros((rows * PACK_LANES - total,), F32)]).reshape(rows, PACK_LANES)
    s = _sum_slots(_allgather_devices(vec, name="small_allgather")[None], name="small_sum").reshape(-1)
    out, o = {}, 0
    for n, f in zip(SMALL_NAMES, flat):
        out[n] = s[o:o + f.shape[0]].reshape(grads[n].shape)
        o += f.shape[0]
    return out


def kernel(x, rel_bias, ln_g, ln_b, ffn_w13, ffn_w2, w_in, conv_w, a_log, dt_bias, dn_norm_g, sinks, w_branch_a, w_branch_b, w_out, loss_target, m_rel_bias, m_ln_g, m_ln_b, m_ffn_w13, m_ffn_w2, m_w_in, m_conv_w, m_a_log, m_dt_bias, m_dn_norm_g, m_sinks, m_w_branch_a, m_w_branch_b, m_w_out, v_rel_bias, v_ln_g, v_ln_b, v_ffn_w13, v_ffn_w2, v_w_in, v_conv_w, v_a_log, v_dt_bias, v_dn_norm_g, v_sinks, v_w_branch_a, v_w_branch_b, v_w_out):
    w = dict(rel_bias=rel_bias, ln_g=ln_g, ln_b=ln_b, ffn_w13=ffn_w13, ffn_w2=ffn_w2, w_in=w_in, conv_w=conv_w,
             a_log=a_log, dt_bias=dt_bias, dn_norm_g=dn_norm_g, sinks=sinks, w_branch_a=w_branch_a,
             w_branch_b=w_branch_b, w_out=w_out)
    m = dict(rel_bias=m_rel_bias, ln_g=m_ln_g, ln_b=m_ln_b, ffn_w13=m_ffn_w13, ffn_w2=m_ffn_w2, w_in=m_w_in,
             conv_w=m_conv_w, a_log=m_a_log, dt_bias=m_dt_bias, dn_norm_g=m_dn_norm_g, sinks=m_sinks,
             w_branch_a=m_w_branch_a, w_branch_b=m_w_branch_b, w_out=m_w_out)
    v = dict(rel_bias=v_rel_bias, ln_g=v_ln_g, ln_b=v_ln_b, ffn_w13=v_ffn_w13, ffn_w2=v_ffn_w2, w_in=v_w_in,
             conv_w=v_conv_w, a_log=v_a_log, dt_bias=v_dt_bias, dn_norm_g=v_dn_norm_g, sinks=v_sinks,
             w_branch_a=v_w_branch_a, w_branch_b=v_w_branch_b, w_out=v_w_out)

    depth = w_in.shape[0]
    sharded = [n for n in WEIGHT_NAMES if SHARD_AXIS[n] is not None]

    def shards_of(i):
        return {n: (w[n][i].astype(MXU_DTYPE) if n in MATRIX_NAMES else w[n][i]) for n in sharded}

    def assemble(i, gathered):
        lw = {n: _unshard(g, SHARD_AXIS[n] - 1) for n, g in gathered.items()}
        lw["w_in_p"] = _repack_w_in(lw.pop("w_in"))
        for n in ("a_log", "dt_bias", "dn_norm_g", "sinks"):
            lw[n] = w[n][i]
        return lw

    s0 = shards_of(0)
    first = assemble(0, dict(zip(s0, _run_job(_gather_job(list(s0.values())), name="weights_allgather_l0"))))
    c_idx = lax.axis_index("c").astype(jnp.int32).reshape(1)
    reducer = _GradReducer(c_idx)
    loss_part, grad_x, grads = _local_step(x, loss_target, rel_bias, [first] + [None] * (depth - 1),
                                           side_shards=[None] + [shards_of(i) for i in range(1, depth)],
                                           side_assemble=assemble, reducer=reducer)
    loss = lax.psum(loss_part, ("x", "y", "c"))

    halves = reducer.result()
    chip = _chip_index(lax.axis_index("x"), lax.axis_index("y"))
    small = _reduce_small(grads)
    outs = {}
    for n in WEIGHT_NAMES:
        if n in MATRIX_NAMES:
            outs[n] = _adamw_halves(w[n], m[n], v[n], *halves[n], c_idx, name="adamw_" + n)
        else:
            axis = SHARD_AXIS[n]
            g = small[n]
            if axis is not None:
                g = lax.dynamic_slice_in_dim(g, chip * w[n].shape[axis], w[n].shape[axis], axis)
            outs[n] = (g,) + _adamw(w[n], g, m[n], v[n], name="adamw_" + n)
    return (loss, grad_x, *[outs[n][0] for n in WEIGHT_NAMES], *[outs[n][1] for n in WEIGHT_NAMES],
            *[outs[n][2] for n in WEIGHT_NAMES], *[outs[n][3] for n in WEIGHT_NAMES])
```

```python
import functools
import math

import numpy as np
import jax
import jax.numpy as jnp
from jax import lax
from jax.experimental import pallas as pl
from jax.experimental.pallas import tpu as pltpu

F32 = jnp.float32
BF16 = jnp.bfloat16
MXU_DTYPE = BF16
HIGHEST = lax.Precision.HIGHEST

D_MODEL = 1024
N_HEADS_A = 16
N_KV_A = 4
HEAD_DIM_A = 64
GROUP_A = N_HEADS_A // N_KV_A
WINDOW = 128
N_HEADS_B = 8
KEY_DIM_B = 128
VAL_DIM_B = 128
CONV_K = 4
CHUNK = 64
D_FF = 2816
NUM_BUCKETS = 32
MAX_DISTANCE = 128
DEPTH = 4
DN_ALPHA = (2 * DEPTH) ** 0.25
LN_EPS = 1e-5
NORM_EPS = 1e-6
NEG_INF = -1e30

Q_A = N_HEADS_A * HEAD_DIM_A
KV_W = N_KV_A * HEAD_DIM_A
QK_B = N_HEADS_B * KEY_DIM_B
V_B = N_HEADS_B * VAL_DIM_B
CONV_CH = 2 * QK_B + V_B
N_IN = Q_A + 2 * KV_W + CONV_CH + 2 * N_HEADS_B + V_B + 2 * D_MODEL

ADAM_LR = 0.001
ADAM_B1 = 0.9
ADAM_B2 = 0.999
ADAM_EPS = 1e-08
ADAM_WD = 0.01
ADAM_STEP = 10

HC_W = 8192
HC_Q = 0
HC_Z = 1024
HC_K = 2048
HC_V = 2304
HC_BD = 2560
HC_CONV = 3072
HC_GATE = 6144

GROUP_T = 256
DN_BWD_HEADS = 4
ROW_T = 256
VMEM_LIMIT_BYTES = 48 * 1024 * 1024


def _params(**kw):
    return pltpu.CompilerParams(vmem_limit_bytes=VMEM_LIMIT_BYTES, **kw)


def _tile(n, cands):
    for c in cands:
        if n % c == 0:
            return c
    return n


def _dot(a, b, dims=(((1,), (0,)), ((), ())), exact=False):
    if exact:
        return lax.dot_general(a.astype(F32), b.astype(F32), dims, precision=HIGHEST, preferred_element_type=F32)
    return lax.dot_general(a.astype(MXU_DTYPE), b.astype(MXU_DTYPE), dims, preferred_element_type=F32)


_NN = (((1,), (0,)), ((), ()))
_NT = (((1,), (1,)), ((), ()))
_TN = (((0,), (0,)), ((), ()))


def _sigmoid(x):
    return 1.0 / (1.0 + jnp.exp(-x))


def _silu(x):
    return x * _sigmoid(x)


def _dsilu(x):
    s = _sigmoid(x)
    return s * (1.0 + x * (1.0 - s))


def _call(body, *, name, grid, in_specs, out_specs, out_shape, scratch_shapes=(), semantics, args, side=None):
    in_specs, out_specs, out_shape = list(in_specs), list(out_specs), list(out_shape)
    if side is None:
        outs = pl.pallas_call(body, name=name, grid=grid, in_specs=in_specs, out_specs=out_specs, out_shape=out_shape,
                              scratch_shapes=list(scratch_shapes),
                              compiler_params=_params(dimension_semantics=semantics))(*args)
        return outs, None
    n_in, n_out, n_scr = len(in_specs), len(out_specs), len(scratch_shapes)
    s_in, s_out = len(side["ins"]), len(side["out_shape"])

    def hosted(*refs):
        main_in, side_in = refs[:n_in], refs[n_in:n_in + s_in]
        o0 = n_in + s_in
        main_out, side_out = refs[o0:o0 + n_out], refs[o0 + n_out:o0 + n_out + s_out]
        rest = refs[o0 + n_out + s_out:]
        copies = side["make"](side_in, side_out, *rest[n_scr:])
        ids = [pl.program_id(d) for d in range(len(grid))]
        first = functools.reduce(jnp.logical_and, [i == 0 for i in ids])
        last = functools.reduce(jnp.logical_and, [i == g - 1 for i, g in zip(ids, grid)])

        @pl.when(first)
        def _():
            for cp in copies:
                cp.start()

        body(*main_in, *main_out, *rest[:n_scr])

        @pl.when(last)
        def _():
            for cp in copies:
                cp.wait()

    outs = pl.pallas_call(
        hosted, name=name, grid=grid,
        in_specs=in_specs + [HBM_SPEC] * s_in, out_specs=out_specs + [HBM_SPEC] * s_out,
        out_shape=out_shape + list(side["out_shape"]),
        scratch_shapes=list(scratch_shapes) + list(side["scratch"]),
        compiler_params=_params(dimension_semantics=("arbitrary",) * len(grid)),
    )(*args, *side["ins"])
    return list(outs[:n_out]), list(outs[n_out:])


def _mm(a, b, *, ta=False, tb=False, a_halves=False, b_halves=False, out_dtype=F32, add=None, exact=False, side=None,
        name):
    if a_halves:
        m, kdim = a.shape[1], 2 * a.shape[2]
    else:
        (kdim, m) = a.shape if ta else a.shape[::-1]
    if b_halves:
        kb, n = b.shape[1], 2 * b.shape[2]
    else:
        (n, kb) = b.shape if tb else b.shape[::-1]
    assert kdim == kb, (a.shape, b.shape, ta, tb)
    if ta and not tb and not exact and add is None and side is None and kdim <= TN_WHOLE_K:
        return _mm_tn(a, b, b_halves=b_halves, out_dtype=out_dtype, name=name)
    tn = _tile(n, (1024, 1408, 512, 256, 128))
    tk = _tile(kdim, (1024, 1408, 512, 256, 128))
    nk = kdim // tk
    tm = _tile(m, (1024, 1408, 512, 256, 128) if nk > 1 else (512, 256, 128))
    nj = n // tn
    dims = (((0 if ta else 1,), (1 if tb else 0,)), ((), ()))
    has_add = add is not None

    def body(*refs):
        if has_add:
            a_ref, b_ref, add_ref, o_ref = refs[:4]
        else:
            a_ref, b_ref, o_ref = refs[:3]
            add_ref = None
        part = _dot(a_ref[...], b_ref[...], dims, exact)

        def finish(acc):
            if has_add:
                acc = acc + add_ref[...].astype(F32)
            o_ref[...] = acc.astype(out_dtype)

        if nk == 1:
            finish(part)
        else:
            acc_ref = refs[-1]
            k = pl.program_id(2)

            @pl.when(k == 0)
            def _():
                acc_ref[...] = part

            @pl.when(k > 0)
            def _():
                acc_ref[...] += part

            @pl.when(k == nk - 1)
            def _():
                finish(acc_ref[...])

    if a_halves:
        assert not ta and nk % 2 == 0
        a_spec = pl.BlockSpec((None, tm, tk), lambda i, j, k: (k // (nk // 2), i, k % (nk // 2)))
    elif ta:
        a_spec = pl.BlockSpec((tk, tm), lambda i, j, k: (k, i))
    else:
        a_spec = pl.BlockSpec((tm, tk), lambda i, j, k: (i, k))
    if b_halves:
        assert not tb and nj % 2 == 0
        b_spec = pl.BlockSpec((None, tk, tn), lambda i, j, k: (j // (nj // 2), k, j % (nj // 2)))
    elif tb:
        b_spec = pl.BlockSpec((tn, tk), lambda i, j, k: (j, k))
    else:
        b_spec = pl.BlockSpec((tk, tn), lambda i, j, k: (k, j))
    o_spec = pl.BlockSpec((tm, tn), lambda i, j, k: (i, j))
    in_specs = [a_spec, b_spec] + ([o_spec] if has_add else [])
    args = (a, b) + ((add,) if has_add else ())
    outs, side_outs = _call(
        body, name=name, grid=(m // tm, nj, nk), in_specs=in_specs, out_specs=[o_spec],
        out_shape=[jax.ShapeDtypeStruct((m, n), out_dtype)],
        scratch_shapes=[pltpu.VMEM((tm, tn), F32)] if nk > 1 else [],
        semantics=("parallel", "parallel", "arbitrary"), args=args, side=side)
    return outs[0] if side is None else (outs[0], side_outs)


TN_WHOLE_K = 8192
VMEM_LIMIT_BIG_BYTES = 56 * 1024 * 1024


def _mm_tn(a, b, *, b_halves=False, out_dtype=F32, name):
    kdim, m = a.shape
    n = 2 * b.shape[2] if b_halves else b.shape[1]
    if m <= 1024:
        tm, tn = m, _tile(n // 2 if b_halves else n, (256, 128))
    else:
        tm, tn = _tile(m, (256, 128)), _tile(n, (1024, 512, 256, 128))
    nj = n // tn

    def body(a_ref, b_ref, o_ref):
        o_ref[...] = _dot(a_ref[...], b_ref[...], _TN).astype(out_dtype)

    if b_halves:
        b_spec = pl.BlockSpec((None, kdim, tn), lambda i, j: (j // (nj // 2), 0, j % (nj // 2)))
    else:
        b_spec = pl.BlockSpec((kdim, tn), lambda i, j: (0, j))
    return pl.pallas_call(
        body, name=name, grid=(m // tm, nj),
        in_specs=[pl.BlockSpec((kdim, tm), lambda i, j: (0, i)), b_spec],
        out_specs=pl.BlockSpec((tm, tn), lambda i, j: (i, j)),
        out_shape=jax.ShapeDtypeStruct((m, n), out_dtype),
        compiler_params=pltpu.CompilerParams(vmem_limit_bytes=VMEM_LIMIT_BIG_BYTES,
                                             dimension_semantics=("parallel", "parallel")),
    )(a, b)


def _mm_wt(a, w, add, *, a_halves=False, name):
    n, kdim = w.shape
    m = a.shape[1] if a_halves else a.shape[0]
    tm = _tile(m, (512, 256, 128))
    half = kdim // 2

    def body(a_ref, w_ref, add_ref, o_ref):
        if a_halves:
            acc = _dot(a_ref[0], w_ref[:, :half], _NT) + _dot(a_ref[1], w_ref[:, half:], _NT)
        else:
            acc = _dot(a_ref[...], w_ref[...], _NT)
        o_ref[...] = acc + add_ref[...]

    a_spec = pl.BlockSpec((2, tm, half), lambda i: (0, i, 0)) if a_halves else pl.BlockSpec((tm, kdim), lambda i: (i, 0))
    row = pl.BlockSpec((tm, n), lambda i: (i, 0))
    return pl.pallas_call(
        body, name=name, grid=(m // tm,),
        in_specs=[a_spec, pl.BlockSpec((n, kdim), lambda i: (0, 0), pipeline_mode=pl.Buffered(1)), row],
        out_specs=row, out_shape=jax.ShapeDtypeStruct((m, n), F32),
        compiler_params=pltpu.CompilerParams(vmem_limit_bytes=VMEM_LIMIT_BIG_BYTES, dimension_semantics=("parallel",)),
    )(a, w, add)


def _layernorm_rows(r, g, b):
    mu = jnp.mean(r, axis=-1, keepdims=True)
    xc = r - mu
    var = jnp.mean(xc * xc, axis=-1, keepdims=True)
    return xc * lax.rsqrt(var + LN_EPS) * g + b


def _mm_res_ln(a, w, resid, g, b, *, alpha, c, name):
    m, kdim = a.shape
    n = w.shape[1]
    tm = _tile(m, (512, 256, 128))

    def body(a_ref, w_ref, x_ref, g_ref, b_ref, r_ref, y_ref, yb_ref):
        f = _dot(a_ref[...], w_ref[...])
        r = alpha * x_ref[...] + c * f
        r_ref[...] = r
        y = _layernorm_rows(r, g_ref[...], b_ref[...])
        y_ref[...] = y
        yb_ref[...] = y.astype(BF16)

    row = pl.BlockSpec((tm, n), lambda i: (i, 0))
    vec = pl.BlockSpec((1, n), lambda i: (0, 0))
    return pl.pallas_call(
        body, name=name, grid=(m // tm,),
        in_specs=[pl.BlockSpec((tm, kdim), lambda i: (i, 0)), pl.BlockSpec((kdim, n), lambda i: (0, 0)), row, vec, vec],
        out_specs=[row, row, row],
        out_shape=[jax.ShapeDtypeStruct((m, n), F32)] * 2 + [jax.ShapeDtypeStruct((m, n), BF16)],
        compiler_params=_params(dimension_semantics=("parallel",)),
    )(a, w, resid, g.reshape(1, n), b.reshape(1, n))


def _ln_bwd(dy, r, g, *, alpha, c, name):
    m, n = dy.shape
    tm = _tile(m, (512, 256, 128))

    def body(dy_ref, r_ref, g_ref, dres_ref, dbr_ref, dg_ref, db_ref):
        i = pl.program_id(0)
        dy_ = dy_ref[...]
        r_ = r_ref[...]
        mu = jnp.mean(r_, axis=-1, keepdims=True)
        xc = r_ - mu
        var = jnp.mean(xc * xc, axis=-1, keepdims=True)
        rstd = lax.rsqrt(var + LN_EPS)
        xh = xc * rstd
        dxh = dy_ * g_ref[...]
        dr = rstd * (dxh - jnp.mean(dxh, axis=-1, keepdims=True) - xh * jnp.mean(dxh * xh, axis=-1, keepdims=True))
        dres_ref[...] = alpha * dr
        dbr_ref[...] = (c * dr).astype(BF16)
        dg_p = jnp.sum(dy_ * xh, axis=0, keepdims=True)
        db_p = jnp.sum(dy_, axis=0, keepdims=True)

        @pl.when(i == 0)
        def _():
            dg_ref[...] = dg_p
            db_ref[...] = db_p

        @pl.when(i > 0)
        def _():
            dg_ref[...] += dg_p
            db_ref[...] += db_p

    row = pl.BlockSpec((tm, n), lambda i: (i, 0))
    vec = pl.BlockSpec((1, n), lambda i: (0, 0))
    return pl.pallas_call(
        body, name=name, grid=(m // tm,),
        in_specs=[row, row, vec], out_specs=[row, row, vec, vec],
        out_shape=[jax.ShapeDtypeStruct((m, n), F32), jax.ShapeDtypeStruct((m, n), BF16),
                   jax.ShapeDtypeStruct((1, n), F32), jax.ShapeDtypeStruct((1, n), F32)],
        compiler_params=_params(dimension_semantics=("arbitrary",)),
    )(dy, r, g.reshape(1, n))


FFN_TN = D_FF // 2


def _ffn_up_act(x, w13, *, side=None, name):
    m, d = x.shape
    tm = _tile(m, (512, 256, 128))
    nj = D_FF // FFN_TN

    def body(x_ref, g_ref, u_ref, o_ref):
        x_ = x_ref[...]
        o_ref[...] = (_silu(_dot(x_, g_ref[...])) * _dot(x_, u_ref[...])).astype(BF16)

    outs, side_outs = _call(
        body, name=name, grid=(nj, m // tm),
        in_specs=[pl.BlockSpec((tm, d), lambda j, i: (i, 0)), pl.BlockSpec((d, FFN_TN), lambda j, i: (0, j)),
                  pl.BlockSpec((d, FFN_TN), lambda j, i: (0, j + nj))],
        out_specs=[pl.BlockSpec((tm, FFN_TN), lambda j, i: (i, j))],
        out_shape=[jax.ShapeDtypeStruct((m, D_FF), BF16)],
        semantics=("parallel", "parallel"), args=(x, w13, w13), side=side)
    return outs[0], side_outs


def _ffn_bwd_mid(x, w13, df, w2, *, side=None, name):
    m, d = x.shape
    tm = _tile(m, (512, 256, 128))
    nj = D_FF // FFN_TN

    def body(x_ref, g_ref, u_ref, df_ref, w2_ref, o_ref):
        x_ = x_ref[...]
        g = _dot(x_, g_ref[...])
        u = _dot(x_, u_ref[...])
        da = _dot(df_ref[...], w2_ref[...], _NT)
        o_ref[0] = (da * u * _dsilu(g)).astype(BF16)
        o_ref[1] = (da * _silu(g)).astype(BF16)

    outs, side_outs = _call(
        body, name=name, grid=(nj, m // tm),
        in_specs=[pl.BlockSpec((tm, d), lambda j, i: (i, 0)), pl.BlockSpec((d, FFN_TN), lambda j, i: (0, j)),
                  pl.BlockSpec((d, FFN_TN), lambda j, i: (0, j + nj)), pl.BlockSpec((tm, d), lambda j, i: (i, 0)),
                  pl.BlockSpec((FFN_TN, d), lambda j, i: (j, 0))],
        out_specs=[pl.BlockSpec((2, tm, FFN_TN), lambda j, i: (0, i, j))],
        out_shape=[jax.ShapeDtypeStruct((2, m, D_FF), BF16)],
        semantics=("parallel", "parallel"), args=(x, w13, w13, df, w2), side=side)
    return outs[0], side_outs


def _t5_bucket_table():
    r = np.arange(WINDOW)[:, None]
    j = np.arange(2 * WINDOW)[None, :]
    rel = r + WINDOW - j
    n = np.maximum(rel, 0)
    max_exact = NUM_BUCKETS // 2
    nf = np.maximum(n, 1).astype(np.float32)
    large = max_exact + (np.log(nf / np.float32(max_exact)) / np.float32(math.log(MAX_DISTANCE / max_exact))
                         * np.float32(NUM_BUCKETS - max_exact)).astype(np.int32)
    large = np.minimum(large, NUM_BUCKETS - 1)
    bucket = np.where(n < max_exact, n, large)
    in_band = (rel >= 0) & (rel < WINDOW)
    return bucket.astype(np.int32), in_band


def _bucket_onehot():
    bucket, _ = _t5_bucket_table()
    oh = np.zeros((WINDOW * 2 * WINDOW, 128), np.float32)
    oh[np.arange(oh.shape[0]), bucket.reshape(-1)] = 1.0
    return oh


def _stack_heads(x, g):
    hd = HEAD_DIM_A
    return jnp.concatenate([x[:, (GROUP_A * g + h) * hd:(GROUP_A * g + h + 1) * hd] for h in range(GROUP_A)], axis=0)


def _unstack_heads(x):
    return jnp.concatenate([x[h * WINDOW:(h + 1) * WINDOW] for h in range(GROUP_A)], axis=1)


def _attn_probs(q_ref, kp_ref, ko_ref, vp_ref, vo_ref, bias_ref, sink_ref, first_block):
    hd = HEAD_DIM_A
    groups = range(N_KV_A)
    q = q_ref[...]
    qs = [_stack_heads(q, g) * (hd ** -0.5) for g in groups]
    k2 = [jnp.concatenate([kp_ref[:, g * hd:(g + 1) * hd], ko_ref[:, g * hd:(g + 1) * hd]], axis=0) for g in groups]
    v2 = [jnp.concatenate([vp_ref[:, g * hd:(g + 1) * hd], vo_ref[:, g * hd:(g + 1) * hd]], axis=0) for g in groups]
    rr = lax.broadcasted_iota(jnp.int32, (GROUP_A * WINDOW, 2 * WINDOW), 0) % WINDOW
    jj = lax.broadcasted_iota(jnp.int32, (GROUP_A * WINDOW, 2 * WINDOW), 1)
    rel = rr + WINDOW - jj
    valid = (rel >= 0) & (rel < WINDOW) & (jnp.logical_not(first_block) | (jj >= WINDOW))
    s = [_dot(qs[g], k2[g], _NT) for g in groups]
    s = [jnp.where(valid, s[g] + bias_ref[GROUP_A * g:GROUP_A * (g + 1)].reshape(GROUP_A * WINDOW, 2 * WINDOW), NEG_INF)
         for g in groups]
    sk = [jnp.concatenate([jnp.broadcast_to(sink_ref[0:1, GROUP_A * g + h:GROUP_A * g + h + 1], (WINDOW, 1))
                           for h in range(GROUP_A)], axis=0) for g in groups]
    mx = [jnp.maximum(jnp.max(s[g], axis=-1, keepdims=True), sk[g]) for g in groups]
    p = [jnp.exp(s[g] - mx[g]) for g in groups]
    ps = [jnp.exp(sk[g] - mx[g]) for g in groups]
    den = [jnp.sum(p[g], axis=-1, keepdims=True) + ps[g] for g in groups]
    return qs, k2, v2, [p[g] / den[g] for g in groups], [ps[g] / den[g] for g in groups]


def _attn_specs(nb):
    def prev(b, i):
        return (b * nb + jnp.maximum(i - 1, 0))

    q_spec = pl.BlockSpec((WINDOW, Q_A), lambda b, i: (b * nb + i, HC_Q // Q_A))
    kp_spec = pl.BlockSpec((WINDOW, KV_W), lambda b, i: (prev(b, i), HC_K // KV_W))
    ko_spec = pl.BlockSpec((WINDOW, KV_W), lambda b, i: (b * nb + i, HC_K // KV_W))
    vp_spec = pl.BlockSpec((WINDOW, KV_W), lambda b, i: (prev(b, i), HC_V // KV_W))
    vo_spec = pl.BlockSpec((WINDOW, KV_W), lambda b, i: (b * nb + i, HC_V // KV_W))
    bias_spec = pl.BlockSpec((N_HEADS_A, WINDOW, 2 * WINDOW), lambda b, i: (0, 0, 0))
    sink_spec = pl.BlockSpec((1, N_HEADS_A), lambda b, i: (0, 0))
    return [q_spec, kp_spec, ko_spec, vp_spec, vo_spec, bias_spec, sink_spec]


def _attn_fwd(hcat, bias, sink, nbatch, *, side=None, name):
    t = hcat.shape[0]
    nb = t // nbatch // WINDOW

    def body(q_ref, kp_ref, ko_ref, vp_ref, vo_ref, bias_ref, sink_ref, o_ref):
        first = pl.program_id(1) == 0
        _, _, v2, p, _ = _attn_probs(q_ref, kp_ref, ko_ref, vp_ref, vo_ref, bias_ref, sink_ref, first)
        o = [_dot(p[g], v2[g]) for g in range(N_KV_A)]
        o_ref[...] = jnp.concatenate([_unstack_heads(og) for og in o], axis=1).astype(BF16)

    outs, side_outs = _call(
        body, name=name, grid=(nbatch, nb), in_specs=_attn_specs(nb),
        out_specs=[pl.BlockSpec((WINDOW, Q_A), lambda b, i: (b * nb + i, 0))],
        out_shape=[jax.ShapeDtypeStruct((t, Q_A), BF16)], semantics=("parallel", "arbitrary"),
        args=(hcat, hcat, hcat, hcat, hcat, bias, sink.reshape(1, N_HEADS_A)), side=side)
    return outs[0], side_outs


def _attn_bwd(hcat, bias, sink, do, nbatch, *, name):
    t = hcat.shape[0]
    nb = t // nbatch // WINDOW
    hd = HEAD_DIM_A

    def body(q_ref, kp_ref, ko_ref, vp_ref, vo_ref, bias_ref, sink_ref, do_ref,
             dq_ref, dk_ref, dv_ref, dbias_ref, dsink_ref, ck_ref, cv_ref):
        b = pl.program_id(0)
        j = pl.program_id(1)
        first = j == nb - 1

        @pl.when((b == 0) & (j == 0))
        def _():
            dbias_ref[...] = jnp.zeros_like(dbias_ref)
            dsink_ref[...] = jnp.zeros_like(dsink_ref)

        @pl.when(j == 0)
        def _():
            ck_ref[...] = jnp.zeros_like(ck_ref)
            cv_ref[...] = jnp.zeros_like(cv_ref)

        do_ = do_ref[...]
        groups = range(N_KV_A)
        lane = lax.broadcasted_iota(jnp.int32, (1, N_HEADS_A), 1)
        qs, k2, v2, p, ps = _attn_probs(q_ref, kp_ref, ko_ref, vp_ref, vo_ref, bias_ref, sink_ref, first)
        dos = [_stack_heads(do_, g) for g in groups]
        dv2 = [_dot(p[g], dos[g], _TN) for g in groups]
        dp = [_dot(dos[g], v2[g], _NT) for g in groups]
        delta = [jnp.sum(p[g] * dp[g], axis=-1, keepdims=True) for g in groups]
        ds = [p[g] * (dp[g] - delta[g]) for g in groups]
        dqs = [_dot(ds[g], k2[g]) * (hd ** -0.5) for g in groups]
        dk2 = [_dot(ds[g], qs[g], _TN) for g in groups]
        dsink = jnp.zeros((1, N_HEADS_A), F32)
        for g in groups:
            dsk = -(ps[g] * delta[g])
            for h in range(GROUP_A):
                tot = jnp.sum(dsk[h * WINDOW:(h + 1) * WINDOW], axis=0, keepdims=True)
                dsink = dsink + jnp.where(lane == GROUP_A * g + h, tot, 0.0)
            dbias_ref[GROUP_A * g:GROUP_A * (g + 1)] += ds[g].reshape(GROUP_A, WINDOW, 2 * WINDOW)
        dq_ref[...] = jnp.concatenate([_unstack_heads(d) for d in dqs], axis=1).astype(BF16)
        dk_ref[...] = (jnp.concatenate([d[WINDOW:] for d in dk2], axis=1) + ck_ref[...]).astype(BF16)
        dv_ref[...] = (jnp.concatenate([d[WINDOW:] for d in dv2], axis=1) + cv_ref[...]).astype(BF16)
        ck_ref[...] = jnp.concatenate([d[:WINDOW] for d in dk2], axis=1)
        cv_ref[...] = jnp.concatenate([d[:WINDOW] for d in dv2], axis=1)
        dsink_ref[...] += dsink

    def rev(spec):
        return pl.BlockSpec(spec.block_shape, lambda b, j, f=spec.index_map: f(b, nb - 1 - j))

    in_specs = [rev(s) for s in _attn_specs(nb)[:5]] + _attn_specs(nb)[5:]
    in_specs.append(pl.BlockSpec((WINDOW, Q_A), lambda b, j: (b * nb + nb - 1 - j, 0)))
    return pl.pallas_call(
        body, name=name, grid=(nbatch, nb),
        in_specs=in_specs,
        out_specs=[pl.BlockSpec((WINDOW, Q_A), lambda b, j: (b * nb + nb - 1 - j, 0)),
                   pl.BlockSpec((WINDOW, KV_W), lambda b, j: (b * nb + nb - 1 - j, 0)),
                   pl.BlockSpec((WINDOW, KV_W), lambda b, j: (b * nb + nb - 1 - j, 0)),
                   pl.BlockSpec((N_HEADS_A, WINDOW, 2 * WINDOW), lambda b, j: (0, 0, 0)),
                   pl.BlockSpec((1, N_HEADS_A), lambda b, j: (0, 0))],
        out_shape=[jax.ShapeDtypeStruct((t, Q_A), BF16), jax.ShapeDtypeStruct((t, KV_W), BF16),
                   jax.ShapeDtypeStruct((t, KV_W), BF16),
                   jax.ShapeDtypeStruct((N_HEADS_A, WINDOW, 2 * WINDOW), F32),
                   jax.ShapeDtypeStruct((1, N_HEADS_A), F32)],
        scratch_shapes=[pltpu.VMEM((WINDOW, KV_W), F32), pltpu.VMEM((WINDOW, KV_W), F32)],
        compiler_params=_params(dimension_semantics=("arbitrary", "arbitrary")),
    )(hcat, hcat, hcat, hcat, hcat, bias, sink.reshape(1, N_HEADS_A), do)


def _shift_down(x, halo8, s):
    if s == 0:
        return x
    rolled = pltpu.roll(x, s, axis=0)
    row8 = lax.broadcasted_iota(jnp.int32, halo8.shape, 0)
    top = jnp.where(row8 < s, pltpu.roll(halo8, s, axis=0), rolled[0:8])
    return jnp.concatenate([top, rolled[8:]], axis=0)


def _shift_up(x, halo8, s):
    if s == 0:
        return x
    n = x.shape[0]
    rolled = pltpu.roll(x, n - s, axis=0)
    row8 = lax.broadcasted_iota(jnp.int32, halo8.shape, 0)
    bottom = jnp.where(row8 >= 8 - s, pltpu.roll(halo8, 8 - s, axis=0), rolled[n - 8:n])
    return jnp.concatenate([rolled[:n - 8], bottom], axis=0)


def _l2n(x, scale):
    r = lax.rsqrt(jnp.sum(x * x, axis=-1, keepdims=True) + NORM_EPS)
    return x * (r * scale)


def _conv_prep(hcat, conv_w, nbatch, *, name):
    t = hcat.shape[0]
    nt = t // nbatch // ROW_T
    cb = HC_CONV // CONV_CH

    def body(u_ref, halo_ref, w_ref, c_ref, q_ref, k_ref, v_ref):
        i = pl.program_id(1)
        u = u_ref[...]
        halo = jnp.where(i == 0, 0.0, halo_ref[...])
        c = jnp.zeros_like(u)
        for j in range(CONV_K):
            c = c + w_ref[j:j + 1, :] * _shift_down(u, halo, CONV_K - 1 - j)
        c_ref[...] = c
        s = _silu(c)
        for h in range(N_HEADS_B):
            lo, hi = h * KEY_DIM_B, (h + 1) * KEY_DIM_B
            q_ref[:, lo:hi] = _l2n(s[:, lo:hi], KEY_DIM_B ** -0.5)
            k_ref[:, lo:hi] = _l2n(s[:, QK_B + lo:QK_B + hi], 1.0)
        v_ref[...] = s[:, 2 * QK_B:]

    row = lambda w: pl.BlockSpec((ROW_T, w), lambda b, i: (b * nt + i, 0))
    return pl.pallas_call(
        body, name=name, grid=(nbatch, nt),
        in_specs=[pl.BlockSpec((ROW_T, CONV_CH), lambda b, i: (b * nt + i, cb)),
                  pl.BlockSpec((8, CONV_CH), lambda b, i: (jnp.maximum((b * nt + i) * (ROW_T // 8) - 1, 0), cb)),
                  pl.BlockSpec((CONV_K, CONV_CH), lambda b, i: (0, 0))],
        out_specs=[row(CONV_CH), row(QK_B), row(QK_B), row(V_B)],
        out_shape=[jax.ShapeDtypeStruct((t, CONV_CH), F32)] + [jax.ShapeDtypeStruct((t, QK_B), F32)] * 3,
        compiler_params=_params(dimension_semantics=("parallel", "parallel")),
    )(hcat, hcat, conv_w)


def _conv_prep_bwd_pointwise(dq, dk, dv, c, *, name):
    t = c.shape[0]

    def l2n_bwd(x, dy, scale):
        r = lax.rsqrt(jnp.sum(x * x, axis=-1, keepdims=True) + NORM_EPS)
        return scale * (r * dy - x * (r * r * r) * jnp.sum(x * dy, axis=-1, keepdims=True))

    def body(dq_ref, dk_ref, dv_ref, c_ref, dc_ref):
        c_ = c_ref[...]
        s = _silu(c_)
        ds = _dsilu(c_)
        for h in range(N_HEADS_B):
            lo, hi = h * KEY_DIM_B, (h + 1) * KEY_DIM_B
            dc_ref[:, lo:hi] = l2n_bwd(s[:, lo:hi], dq_ref[:, lo:hi], KEY_DIM_B ** -0.5) * ds[:, lo:hi]
            dc_ref[:, QK_B + lo:QK_B + hi] = (l2n_bwd(s[:, QK_B + lo:QK_B + hi], dk_ref[:, lo:hi], 1.0)
                                              * ds[:, QK_B + lo:QK_B + hi])
        dc_ref[:, 2 * QK_B:] = dv_ref[...] * ds[:, 2 * QK_B:]

    row = lambda w: pl.BlockSpec((ROW_T, w), lambda i: (i, 0))
    return pl.pallas_call(
        body, name=name, grid=(t // ROW_T,),
        in_specs=[row(QK_B), row(QK_B), row(V_B), row(CONV_CH)], out_specs=row(CONV_CH),
        out_shape=jax.ShapeDtypeStruct((t, CONV_CH), F32),
        compiler_params=_params(dimension_semantics=("parallel",)),
    )(dq, dk, dv, c)


def _conv_bwd(dc, hcat, conv_w, nbatch, *, name):
    t = dc.shape[0]
    nt = t // nbatch // ROW_T
    cb = HC_CONV // CONV_CH
    last_blk = t // 8 - 1

    def body(dc_ref, dnext_ref, u_ref, uprev_ref, w_ref, du_ref, dw_ref):
        b = pl.program_id(0)
        i = pl.program_id(1)
        dc_ = dc_ref[...]
        u = u_ref[...]
        dnext = jnp.where(i == nt - 1, 0.0, dnext_ref[...])
        uprev = jnp.where(i == 0, 0.0, uprev_ref[...])
        du = jnp.zeros_like(dc_)
        rows = []
        for j in range(CONV_K):
            s = CONV_K - 1 - j
            du = du + w_ref[j:j + 1, :] * _shift_up(dc_, dnext, s)
            rows.append(jnp.sum(dc_ * _shift_down(u, uprev, s), axis=0, keepdims=True))
        du_ref[...] = du.astype(BF16)
        dw_p = jnp.concatenate(rows + [jnp.zeros((8 - CONV_K, CONV_CH), F32)], axis=0)

        @pl.when((b == 0) & (i == 0))
        def _():
            dw_ref[...] = dw_p

        @pl.when((b > 0) | (i > 0))
        def _():
            dw_ref[...] += dw_p

    return pl.pallas_call(
        body, name=name, grid=(nbatch, nt),
        in_specs=[pl.BlockSpec((ROW_T, CONV_CH), lambda b, i: (b * nt + i, 0)),
                  pl.BlockSpec((8, CONV_CH), lambda b, i: (jnp.minimum((b * nt + i + 1) * (ROW_T // 8), last_blk), 0)),
                  pl.BlockSpec((ROW_T, CONV_CH), lambda b, i: (b * nt + i, cb)),
                  pl.BlockSpec((8, CONV_CH), lambda b, i: (jnp.maximum((b * nt + i) * (ROW_T // 8) - 1, 0), cb)),
                  pl.BlockSpec((CONV_K, CONV_CH), lambda b, i: (0, 0))],
        out_specs=[pl.BlockSpec((ROW_T, CONV_CH), lambda b, i: (b * nt + i, 0)),
                   pl.BlockSpec((8, CONV_CH), lambda b, i: (0, 0))],
        out_shape=[jax.ShapeDtypeStruct((t, CONV_CH), BF16), jax.ShapeDtypeStruct((8, CONV_CH), F32)],
        compiler_params=_params(dimension_semantics=("arbitrary", "arbitrary")),
    )(dc, dc, hcat, hcat, conv_w)


def _softplus(x):
    return jnp.maximum(x, 0.0) + jnp.log(1.0 + jnp.exp(-jnp.abs(x)))


def _gates(hcat, a_row, dt_row, *, name):
    t = hcat.shape[0]

    def body(bd_ref, a_ref, dt_ref, gb_ref, bb_ref):
        bd = bd_ref[...]
        beta = _sigmoid(bd)
        g = -jnp.exp(a_ref[...]) * _softplus(bd + dt_ref[...])
        for h in range(N_HEADS_B):
            lo, hi = h * VAL_DIM_B, (h + 1) * VAL_DIM_B
            bb_ref[:, lo:hi] = jnp.broadcast_to(beta[:, h:h + 1], (ROW_T, VAL_DIM_B))
            gb_ref[:, lo:hi] = jnp.broadcast_to(g[:, N_HEADS_B + h:N_HEADS_B + h + 1], (ROW_T, VAL_DIM_B))

    vec = pl.BlockSpec((1, 128), lambda i: (0, 0))
    row = pl.BlockSpec((ROW_T, V_B), lambda i: (i, 0))
    return pl.pallas_call(
        body, name=name, grid=(t // ROW_T,),
        in_specs=[pl.BlockSpec((ROW_T, 128), lambda i: (i, HC_BD // 128)), vec, vec],
        out_specs=[row, row], out_shape=[jax.ShapeDtypeStruct((t, V_B), F32)] * 2,
        compiler_params=_params(dimension_semantics=("parallel",)),
    )(hcat, a_row, dt_row)


def _gates_bwd(dgb, dbb, hcat, a_row, dt_row, *, name):
    t = hcat.shape[0]

    def body(dgb_ref, dbb_ref, bd_ref, a_ref, dt_ref, dbd_ref, da_ref, ddt_ref):
        i = pl.program_id(0)
        bd = bd_ref[...]
        beta = _sigmoid(bd)
        ea = jnp.exp(a_ref[...])
        x = bd + dt_ref[...]
        g = -ea * _softplus(x)
        lane = lax.broadcasted_iota(jnp.int32, (ROW_T, 128), 1)
        dbeta = jnp.zeros((ROW_T, 128), F32)
        dg = jnp.zeros((ROW_T, 128), F32)
        for h in range(N_HEADS_B):
            lo, hi = h * VAL_DIM_B, (h + 1) * VAL_DIM_B
            dbeta = dbeta + jnp.where(lane == h, jnp.sum(dbb_ref[:, lo:hi], axis=-1, keepdims=True), 0.0)
            dg = dg + jnp.where(lane == N_HEADS_B + h, jnp.sum(dgb_ref[:, lo:hi], axis=-1, keepdims=True), 0.0)
        ddt_raw = dg * (-ea) * _sigmoid(x)
        dbd_ref[...] = (dbeta * beta * (1.0 - beta) + ddt_raw).astype(BF16)
        da_p = jnp.sum(dg * g, axis=0, keepdims=True)
        ddt_p = jnp.sum(ddt_raw, axis=0, keepdims=True)

        @pl.when(i == 0)
        def _():
            da_ref[...] = da_p
            ddt_ref[...] = ddt_p

        @pl.when(i > 0)
        def _():
            da_ref[...] += da_p
            ddt_ref[...] += ddt_p

    vec = pl.BlockSpec((1, 128), lambda i: (0, 0))
    row = pl.BlockSpec((ROW_T, V_B), lambda i: (i, 0))
    return pl.pallas_call(
        body, name=name, grid=(t // ROW_T,),
        in_specs=[row, row, pl.BlockSpec((ROW_T, 128), lambda i: (i, HC_BD // 128)), vec, vec],
        out_specs=[pl.BlockSpec((ROW_T, 128), lambda i: (i, 0)), vec, vec],
        out_shape=[jax.ShapeDtypeStruct((t, 128), BF16), jax.ShapeDtypeStruct((1, 128), F32),
                   jax.ShapeDtypeStruct((1, 128), F32)],
        compiler_params=_params(dimension_semantics=("arbitrary",)),
    )(dgb, dbb, hcat, a_row, dt_row)


def _group_masks():
    r = lax.broadcasted_iota(jnp.int32, (GROUP_T, GROUP_T), 0)
    c = lax.broadcasted_iota(jnp.int32, (GROUP_T, GROUP_T), 1)
    same = (r // CHUNK) == (c // CHUNK)
    return same, same & (r >= c), same & (r > c)


def _split2(a):
    hi = a.astype(MXU_DTYPE)
    return hi, (a - hi.astype(F32)).astype(MXU_DTYPE)


def _dot3(a2, b2, dims=_NN):
    (ah, al), (bh, bl) = a2, b2
    d = functools.partial(lax.dot_general, dimension_numbers=dims, preferred_element_type=F32)
    return d(ah, bh) + (d(ah, bl) + d(al, bh))


def _inv_unit_lower(lows):
    shape = lows[0].shape
    eye = (lax.broadcasted_iota(jnp.int32, shape, 0) == lax.broadcasted_iota(jnp.int32, shape, 1)).astype(F32)
    p2 = [_split2(-low) for low in lows]
    ts = [eye - low for low in lows]
    for _ in range(int(math.log2(CHUNK)) - 1):
        p2 = [_split2(_dot3(p, p)) for p in p2]
        ts = [t + _dot3(_split2(t), p) for t, p in zip(ts, p2)]
    return ts


@jax.custom_vjp
def _inv_saved(low, t):
    return t


def _inv_saved_fwd(low, t):
    return t, t


def _inv_saved_bwd(t, dt):
    t2 = _split2(t)
    return -_dot3(t2, _split2(_dot3(_split2(dt), t2, _NT)), _TN), jnp.zeros_like(t)


_inv_saved.defvjp(_inv_saved_fwd, _inv_saved_bwd)


def _mask_dot(mask, x, dims):
    m = mask.astype(MXU_DTYPE)
    hi = x.astype(MXU_DTYPE)
    r1 = x - hi.astype(F32)
    mid = r1.astype(MXU_DTYPE)
    lo = (r1 - mid.astype(F32)).astype(MXU_DTYPE)
    d = functools.partial(lax.dot_general, dimension_numbers=dims, preferred_element_type=F32)
    return d(m, hi) + (d(m, mid) + d(m, lo))


@jax.custom_vjp
def _chunk_sums(gb):
    same, causal, _ = _group_masks()
    return _mask_dot(causal, gb, _NN), _mask_dot(same, gb, _NN)


def _chunk_sums_fwd(gb):
    return _chunk_sums(gb), None


def _chunk_sums_bwd(_, cot):
    same, causal, _ = _group_masks()
    return (_mask_dot(causal, cot[0], _TN) + _mask_dot(same, cot[1], _TN),)


_chunk_sums.defvjp(_chunk_sums_fwd, _chunk_sums_bwd)


def _fold_blocks(m):
    return m[:, 0:CHUNK] + m[:, CHUNK:2 * CHUNK] + m[:, 2 * CHUNK:3 * CHUNK] + m[:, 3 * CHUNK:4 * CHUNK]


def _dn_prep_heads(q, k, v, gb, bb, tsaved=None):
    same, causal, strict = _group_masks()
    heads = range(len(q))
    sums = [_chunk_sums(gb[h]) for h in heads]
    gc = [s[0] for s in sums]
    glast = [s[1] for s in sums]
    decay = [jnp.exp(jnp.where(causal, gc[h][:, 0:1] - gc[h].T[0:1, :], NEG_INF)) for h in heads]
    kb = [k[h] * bb[h] for h in heads]
    vb = [v[h] * bb[h] for h in heads]
    lower = [jnp.where(strict, _dot(kb[h], k[h], _NT) * decay[h], 0.0) for h in heads]
    if tsaved is None:
        tinv = _inv_unit_lower(lower)
    else:
        tinv = [_inv_saved(lower[h], jnp.where(same, jnp.concatenate([tsaved[h]] * (GROUP_T // CHUNK), axis=1), 0.0))
                for h in heads]
    egc = [jnp.exp(gc[h]) for h in heads]
    u = [_dot(tinv[h], vb[h]) for h in heads]
    w = [_dot(tinv[h], kb[h] * egc[h]) for h in heads]
    a = [_fold_blocks(jnp.where(causal, _dot(q[h], k[h], _NT) * decay[h], 0.0)) for h in heads]
    k_tail = [k[h] * jnp.exp(glast[h] - gc[h]) for h in heads]
    q_dec = [q[h] * egc[h] for h in heads]
    return u, w, q_dec, k_tail, a, glast, [_fold_blocks(t) for t in tinv]


def _head_slices(ref, width):
    return [ref[:, h * width:(h + 1) * width] for h in range(N_HEADS_B)]


def _store_heads(ref, vals, width):
    for h, val in enumerate(vals):
        ref[:, h * width:(h + 1) * width] = val


def _dn_prep(q, k, v, gb, bb, *, side=None, name):
    t = q.shape[0]

    def body(q_ref, k_ref, v_ref, gb_ref, bb_ref, u_ref, w_ref, qd_ref, kt_ref, a_ref, gl_ref, ti_ref):
        outs = _dn_prep_heads(*[_head_slices(r, KEY_DIM_B) for r in (q_ref, k_ref, v_ref, gb_ref, bb_ref)])
        for ref, vals in zip((u_ref, w_ref, qd_ref, kt_ref, a_ref, gl_ref, ti_ref), outs):
            _store_heads(ref, vals, vals[0].shape[1])

    row = pl.BlockSpec((GROUP_T, V_B), lambda i: (i, 0))
    arow = pl.BlockSpec((GROUP_T, N_HEADS_B * CHUNK), lambda i: (i, 0))
    big = jax.ShapeDtypeStruct((t, V_B), F32)
    small = jax.ShapeDtypeStruct((t, N_HEADS_B * CHUNK), F32)
    outs, side_outs = _call(
        body, name=name, grid=(t // GROUP_T,), in_specs=[row] * 5, out_specs=[row, row, row, row, arow, row, arow],
        out_shape=[big, big, big, big, small, big, small], semantics=("parallel",), args=(q, k, v, gb, bb), side=side)
    return tuple(outs), side_outs


def _dn_prep_bwd(q, k, v, gb, bb, ti, du, dw, dqd, dkt, da, dgl, *, side=None, name):
    t = q.shape[0]

    def body(q_ref, k_ref, v_ref, gb_ref, bb_ref, ti_ref, du_ref, dw_ref, dqd_ref, dkt_ref, da_ref, dgl_ref,
             dq_ref, dk_ref, dv_ref, dgb_ref, dbb_ref):
        for lo in range(0, N_HEADS_B, DN_BWD_HEADS):
            grp = slice(lo, lo + DN_BWD_HEADS)
            tsaved = _head_slices(ti_ref, CHUNK)[grp]
            _, vjp = jax.vjp(lambda *a, ts=tsaved: _dn_prep_heads(*a, tsaved=ts)[:6],
                             *[_head_slices(r, KEY_DIM_B)[grp] for r in (q_ref, k_ref, v_ref, gb_ref, bb_ref)])
            cot = tuple(_head_slices(r, CHUNK if r is da_ref else KEY_DIM_B)[grp]
                        for r in (du_ref, dw_ref, dqd_ref, dkt_ref, da_ref, dgl_ref))
            for ref, vals in zip((dq_ref, dk_ref, dv_ref, dgb_ref, dbb_ref), vjp(cot)):
                for h, val in enumerate(vals):
                    ref[:, (lo + h) * KEY_DIM_B:(lo + h + 1) * KEY_DIM_B] = val

    row = pl.BlockSpec((GROUP_T, V_B), lambda i: (i, 0))
    arow = pl.BlockSpec((GROUP_T, N_HEADS_B * CHUNK), lambda i: (i, 0))
    big = jax.ShapeDtypeStruct((t, V_B), F32)
    outs, side_outs = _call(
        body, name=name, grid=(t // GROUP_T,),
        in_specs=[row] * 5 + [arow] + [row] * 4 + [arow, row], out_specs=[row] * 5, out_shape=[big] * 5,
        semantics=("parallel",), args=(q, k, v, gb, bb, ti, du, dw, dqd, dkt, da, dgl), side=side)
    return tuple(outs), side_outs


def _dn_steps(s, qd, kt, u, w, a, gl):
    heads = range(len(s))
    v_new = [u[h] - _dot(w[h], s[h]) for h in heads]
    qs = [_dot(qd[h], s[h]) for h in heads]
    o = [qs[h] + _dot(a[h], v_new[h]) for h in heads]
    s_new = [s[h] * jnp.exp(gl[h][0:1, :]) + _dot(kt[h], v_new[h], _TN) for h in heads]
    return s_new, o


def _dn_scan(u, w, qd, kt, a, gl, nbatch, *, name):
    t = u.shape[0]
    ng = t // nbatch // GROUP_T
    cpg = GROUP_T // CHUNK

    def body(u_ref, w_ref, qd_ref, kt_ref, a_ref, gl_ref, o_ref, ss_ref, s_ref):
        @pl.when(pl.program_id(1) == 0)
        def _():
            s_ref[...] = jnp.zeros_like(s_ref)

        def chunk(c, carry):
            rows = pl.ds(pl.multiple_of(c * CHUNK, CHUNK), CHUNK)
            heads = range(N_HEADS_B)
            s = [s_ref[h] for h in heads]
            for h in heads:
                ss_ref[c, h] = s[h]
            s_new, o = _dn_steps(s, *[[r[rows, h * wd:(h + 1) * wd] for h in heads] for r, wd in
                                      ((qd_ref, KEY_DIM_B), (kt_ref, KEY_DIM_B), (u_ref, VAL_DIM_B), (w_ref, KEY_DIM_B),
                                       (a_ref, CHUNK), (gl_ref, VAL_DIM_B))])
            for h in heads:
                s_ref[h] = s_new[h]
                o_ref[rows, h * VAL_DIM_B:(h + 1) * VAL_DIM_B] = o[h]
            return carry

        lax.fori_loop(0, cpg, chunk, 0)

    row = pl.BlockSpec((GROUP_T, V_B), lambda b, i: (b * ng + i, 0))
    arow = pl.BlockSpec((GROUP_T, N_HEADS_B * CHUNK), lambda b, i: (b * ng + i, 0))
    return pl.pallas_call(
        body, name=name, grid=(nbatch, ng),
        in_specs=[row, row, row, row, arow, row],
        out_specs=[row, pl.BlockSpec((cpg, N_HEADS_B, KEY_DIM_B, VAL_DIM_B), lambda b, i: (b * ng + i, 0, 0, 0))],
        out_shape=[jax.ShapeDtypeStruct((t, V_B), F32),
                   jax.ShapeDtypeStruct((t // CHUNK, N_HEADS_B, KEY_DIM_B, VAL_DIM_B), F32)],
        scratch_shapes=[pltpu.VMEM((N_HEADS_B, KEY_DIM_B, VAL_DIM_B), F32)],
        compiler_params=_params(dimension_semantics=("parallel", "arbitrary")),
    )(u, w, qd, kt, a, gl)


def _dn_scan_bwd(u, w, qd, kt, a, gl, ss, do, nbatch, *, name):
    t = u.shape[0]
    ng = t // nbatch // GROUP_T
    cpg = GROUP_T // CHUNK

    def body(u_ref, w_ref, qd_ref, kt_ref, a_ref, gl_ref, ss_ref, do_ref,
             du_ref, dw_ref, dqd_ref, dkt_ref, da_ref, dgl_ref, ds_ref):
        @pl.when(pl.program_id(1) == 0)
        def _():
            ds_ref[...] = jnp.zeros_like(ds_ref)

        def chunk(cc, carry):
            c = cpg - 1 - cc
            rows = pl.ds(pl.multiple_of(c * CHUNK, CHUNK), CHUNK)
            heads = range(N_HEADS_B)
            ins = ((qd_ref, KEY_DIM_B), (kt_ref, KEY_DIM_B), (u_ref, VAL_DIM_B), (w_ref, KEY_DIM_B), (a_ref, CHUNK),
                   (gl_ref, VAL_DIM_B))
            _, vjp = jax.vjp(_dn_steps, [ss_ref[c, h] for h in heads],
                             *[[r[rows, h * wd:(h + 1) * wd] for h in heads] for r, wd in ins])
            grads = vjp(([ds_ref[h] for h in heads], [do_ref[rows, h * VAL_DIM_B:(h + 1) * VAL_DIM_B] for h in heads]))
            for h in heads:
                ds_ref[h] = grads[0][h]
            outs = ((dqd_ref, KEY_DIM_B), (dkt_ref, KEY_DIM_B), (du_ref, VAL_DIM_B), (dw_ref, KEY_DIM_B), (da_ref, CHUNK),
                    (dgl_ref, VAL_DIM_B))
            for (r, wd), vals in zip(outs, grads[1:]):
                for h in heads:
                    r[rows, h * wd:(h + 1) * wd] = vals[h]
            return carry

        lax.fori_loop(0, cpg, chunk, 0)

    row = pl.BlockSpec((GROUP_T, V_B), lambda b, j: (b * ng + ng - 1 - j, 0))
    arow = pl.BlockSpec((GROUP_T, N_HEADS_B * CHUNK), lambda b, j: (b * ng + ng - 1 - j, 0))
    big = jax.ShapeDtypeStruct((t, V_B), F32)
    return pl.pallas_call(
        body, name=name, grid=(nbatch, ng),
        in_specs=[row, row, row, row, arow, row,
                  pl.BlockSpec((cpg, N_HEADS_B, KEY_DIM_B, VAL_DIM_B), lambda b, j: (b * ng + ng - 1 - j, 0, 0, 0)), row],
        out_specs=[row, row, row, row, arow, row],
        out_shape=[big, big, big, big, jax.ShapeDtypeStruct((t, N_HEADS_B * CHUNK), F32), big],
        scratch_shapes=[pltpu.VMEM((N_HEADS_B, KEY_DIM_B, VAL_DIM_B), F32)],
        compiler_params=_params(dimension_semantics=("parallel", "arbitrary")),
    )(u, w, qd, kt, a, gl, ss, do)


def _rms_gate(o, hcat, dn_g, *, name):
    t = o.shape[0]

    def body(o_ref, z_ref, g_ref, y_ref):
        for h in range(N_HEADS_B):
            sl = slice(h * VAL_DIM_B, (h + 1) * VAL_DIM_B)
            o_ = o_ref[:, sl]
            r = lax.rsqrt(jnp.mean(o_ * o_, axis=-1, keepdims=True) + NORM_EPS)
            y_ref[:, sl] = (o_ * r * g_ref[...] * _silu(z_ref[:, sl])).astype(BF16)

    row = pl.BlockSpec((ROW_T, V_B), lambda i: (i, 0))
    return pl.pallas_call(
        body, name=name, grid=(t // ROW_T,),
        in_specs=[row, pl.BlockSpec((ROW_T, V_B), lambda i: (i, HC_Z // V_B)), pl.BlockSpec((1, VAL_DIM_B), lambda i: (0, 0))],
        out_specs=row, out_shape=jax.ShapeDtypeStruct((t, V_B), BF16),
        compiler_params=_params(dimension_semantics=("parallel",)),
    )(o, hcat, dn_g.reshape(1, VAL_DIM_B))


def _rms_gate_bwd(dy, o, hcat, dn_g, *, name):
    t = o.shape[0]

    def body(dy_ref, o_ref, z_ref, g_ref, do_ref, dz_ref, dg_ref):
        i = pl.program_id(0)
        g = g_ref[...]
        dg_p = jnp.zeros((1, VAL_DIM_B), F32)
        for h in range(N_HEADS_B):
            sl = slice(h * VAL_DIM_B, (h + 1) * VAL_DIM_B)
            o_ = o_ref[:, sl]
            z_ = z_ref[:, sl]
            dy_ = dy_ref[:, sl]
            r = lax.rsqrt(jnp.mean(o_ * o_, axis=-1, keepdims=True) + NORM_EPS)
            n = o_ * r
            sz = _silu(z_)
            dz_ref[:, sl] = (dy_ * n * g * _dsilu(z_)).astype(BF16)
            dg_p = dg_p + jnp.sum(dy_ * n * sz, axis=0, keepdims=True)
            dn = dy_ * g * sz
            do_ref[:, sl] = r * dn - o_ * (r * r * r) * jnp.mean(o_ * dn, axis=-1, keepdims=True)

        @pl.when(i == 0)
        def _():
            dg_ref[...] = dg_p

        @pl.when(i > 0)
        def _():
            dg_ref[...] += dg_p

    row = pl.BlockSpec((ROW_T, V_B), lambda i: (i, 0))
    vec = pl.BlockSpec((1, VAL_DIM_B), lambda i: (0, 0))
    return pl.pallas_call(
        body, name=name, grid=(t // ROW_T,),
        in_specs=[row, row, pl.BlockSpec((ROW_T, V_B), lambda i: (i, HC_Z // V_B)), vec],
        out_specs=[row, row, vec],
        out_shape=[jax.ShapeDtypeStruct((t, V_B), F32), jax.ShapeDtypeStruct((t, V_B), BF16),
                   jax.ShapeDtypeStruct((1, VAL_DIM_B), F32)],
        compiler_params=_params(dimension_semantics=("arbitrary",)),
    )(dy, o, hcat, dn_g.reshape(1, VAL_DIM_B))


def _merge(ya, yb, hcat, *, name):
    t = ya.shape[0]

    def body(ya_ref, yb_ref, ga_ref, gb_ref, y_ref):
        y_ref[...] = (_sigmoid(ga_ref[...]) * ya_ref[...] + _sigmoid(gb_ref[...]) * yb_ref[...]).astype(BF16)

    row = pl.BlockSpec((ROW_T, D_MODEL), lambda i: (i, 0))
    return pl.pallas_call(
        body, name=name, grid=(t // ROW_T,),
        in_specs=[row, row, pl.BlockSpec((ROW_T, D_MODEL), lambda i: (i, HC_GATE // D_MODEL)),
                  pl.BlockSpec((ROW_T, D_MODEL), lambda i: (i, HC_GATE // D_MODEL + 1))],
        out_specs=row, out_shape=jax.ShapeDtypeStruct((t, D_MODEL), BF16),
        compiler_params=_params(dimension_semantics=("parallel",)),
    )(ya, yb, hcat, hcat)


def _merge_bwd(dmix, ya, yb, hcat, *, name):
    t = ya.shape[0]

    def body(d_ref, ya_ref, yb_ref, ga_ref, gb_ref, dya_ref, dyb_ref, dgate_ref):
        d = d_ref[...]
        sa = _sigmoid(ga_ref[...])
        sb = _sigmoid(gb_ref[...])
        dya_ref[...] = (d * sa).astype(BF16)
        dyb_ref[...] = (d * sb).astype(BF16)
        dgate_ref[:, :D_MODEL] = (d * ya_ref[...] * sa * (1.0 - sa)).astype(BF16)
        dgate_ref[:, D_MODEL:] = (d * yb_ref[...] * sb * (1.0 - sb)).astype(BF16)

    row = pl.BlockSpec((ROW_T, D_MODEL), lambda i: (i, 0))
    return pl.pallas_call(
        body, name=name, grid=(t // ROW_T,),
        in_specs=[row, row, row, pl.BlockSpec((ROW_T, D_MODEL), lambda i: (i, HC_GATE // D_MODEL)),
                  pl.BlockSpec((ROW_T, D_MODEL), lambda i: (i, HC_GATE // D_MODEL + 1))],
        out_specs=[row, row, pl.BlockSpec((ROW_T, 2 * D_MODEL), lambda i: (i, 0))],
        out_shape=[jax.ShapeDtypeStruct((t, D_MODEL), BF16)] * 2 + [jax.ShapeDtypeStruct((t, 2 * D_MODEL), BF16)],
        compiler_params=_params(dimension_semantics=("parallel",)),
    )(dmix, ya, yb, hcat, hcat)


def _loss_head(y, target, *, name):
    t, n = y.shape
    tm = _tile(t, (512, 256, 128))

    def body(y_ref, t_ref, part_ref, dy_ref):
        i = pl.program_id(0)
        e = y_ref[...] - t_ref[...]
        dy_ref[...] = e * (1.0 / n)
        p = jnp.sum((e * e).reshape(tm // 8, 8, n), axis=0) * (0.5 / n)

        @pl.when(i == 0)
        def _():
            part_ref[...] = p

        @pl.when(i > 0)
        def _():
            part_ref[...] += p

    row = pl.BlockSpec((tm, n), lambda i: (i, 0))
    return pl.pallas_call(
        body, name=name, grid=(t // tm,),
        in_specs=[row, row], out_specs=[pl.BlockSpec((8, n), lambda i: (0, 0)), row],
        out_shape=[jax.ShapeDtypeStruct((8, n), F32), jax.ShapeDtypeStruct((t, n), F32)],
        compiler_params=_params(dimension_semantics=("arbitrary",)),
    )(y, target)


def _adamw_math(w, g, m, v):
    nm = ADAM_B1 * m + (1.0 - ADAM_B1) * g
    nv = ADAM_B2 * v + (1.0 - ADAM_B2) * (g * g)
    m_hat = nm / (1.0 - ADAM_B1 ** ADAM_STEP)
    v_hat = nv / (1.0 - ADAM_B2 ** ADAM_STEP)
    return -ADAM_LR * (m_hat / (jnp.sqrt(v_hat) + ADAM_EPS) + ADAM_WD * w), nm, nv


def _adamw(w, g, m, v, *, name):
    shape = w.shape
    cols = shape[-1]
    rows = int(np.prod(shape[:-1]))
    w2, g2, m2, v2 = (a.reshape(rows, cols) for a in (w, g, m, v))
    tr = rows
    if rows * cols > 512 * 1024:
        tr = _tile(rows, tuple(c for c in (512, 256, 128, 64, 32, 16, 8) if c * cols <= 256 * 1024))

    def body(w_ref, g_ref, m_ref, v_ref, d_ref, nm_ref, nv_ref):
        d_ref[...], nm_ref[...], nv_ref[...] = _adamw_math(w_ref[...], g_ref[...], m_ref[...], v_ref[...])

    blk = pl.BlockSpec((tr, cols), lambda i: (i, 0))
    outs = pl.pallas_call(
        body, name=name, grid=(rows // tr,),
        in_specs=[blk] * 4, out_specs=[blk] * 3,
        out_shape=[jax.ShapeDtypeStruct((rows, cols), F32)] * 3,
        compiler_params=_params(dimension_semantics=("parallel",)),
    )(w2, g2, m2, v2)
    return tuple(o.reshape(shape) for o in outs)


def _repack_w_in(w_in):
    d = w_in.shape[0]
    o = 0
    parts = {}
    for nm, wd in (("q", Q_A), ("k", KV_W), ("v", KV_W), ("conv", CONV_CH), ("beta", N_HEADS_B), ("dt", N_HEADS_B),
                   ("z", V_B), ("gate", 2 * D_MODEL)):
        parts[nm] = w_in[:, o:o + wd]
        o += wd
    z = lambda n: jnp.zeros((d, n), w_in.dtype)
    return jnp.concatenate([parts["q"], parts["z"], parts["k"], parts["v"], parts["beta"], parts["dt"],
                            z(128 - 2 * N_HEADS_B), z(HC_CONV - HC_BD - 128), parts["conv"], parts["gate"]], axis=1)


MATRIX_NAMES = ("ffn_w13", "ffn_w2", "w_in", "w_branch_a", "w_branch_b", "w_out")
GATHER_BESIDE_IN_PROJ = ("w_in", "conv_w", "w_branch_a", "w_branch_b", "w_out")


def _dw_in_by_owner(dw):
    sections = ((Q_A, HC_Q), (2 * KV_W, HC_K), (CONV_CH, HC_CONV), (2 * N_HEADS_B, HC_BD), (V_B, HC_Z),
                (2 * D_MODEL, HC_GATE))
    per = N_IN // 4
    owners = []
    for o in range(4):
        lo, hi, start, parts = o * per, (o + 1) * per, 0, []
        for width, off in sections:
            a, b = max(lo, start), min(hi, start + width)
            if a < b:
                parts.append(dw[:, off + a - start:off + b - start])
            start += width
        owners.append(jnp.concatenate(parts, axis=1))
    return jnp.stack(owners)


def _lane_row(vals):
    return jnp.pad(vals.astype(F32).reshape(1, N_HEADS_B), ((0, 0), (N_HEADS_B, 128 - 2 * N_HEADS_B)))


def _local_step(x, target, rel_bias, layer_wts, side_shards=None, side_assemble=None, reducer=None):
    nbatch, seq, d = x.shape
    t = nbatch * seq
    depth = len(layer_wts)
    layer_wts = list(layer_wts)
    x0 = x.reshape(t, d)
    tgt = target.reshape(t, d)

    onehot = jnp.asarray(_bucket_onehot())
    rel_t = jnp.pad(rel_bias.T, ((0, 0), (0, 128 - NUM_BUCKETS)))
    bias = _mm(rel_t, onehot, tb=True, exact=True, name="pos_bias")
    bias = bias.reshape(N_HEADS_A, WINDOW, 2 * WINDOW)

    pending = [dict(s) if s else {} for s in (side_shards or [None] * depth)] + [{}]

    def fetch(layer, pick):
        names = [n for n in pending[layer] if pick(n)]
        if not names:
            return None, lambda outs: None
        job = _gather_job([pending[layer].pop(n) for n in names])

        def finish(outs):
            for k, val in side_assemble(layer, dict(zip(names, outs))).items():
                if isinstance(val, dict):
                    layer_wts[layer].setdefault(k, {}).update(val)
                else:
                    layer_wts[layer][k] = val
        return job, finish

    def in_mixer(n):
        return n in GATHER_BESIDE_IN_PROJ

    saved = []
    xin, xin_b = x0, x0.astype(BF16)
    for i in range(depth):
        L = {}
        W = layer_wts[i]
        tag = f"_l{i}"
        job, finish = fetch(i, in_mixer)
        a, got = _ffn_up_act(xin_b, W["ffn_w13"][0], side=job, name="ffn_up_act" + tag + "a")
        finish(got)
        r1, x1, x1_b = _mm_res_ln(a, W["ffn_w2"][0], xin, W["ln_g"][0], W["ln_b"][0],
                                  alpha=DN_ALPHA, c=0.5, name="ffn_down_ln" + tag + "a")
        L.update(x0_b=xin_b, a0=a, r1=r1, x1=x1, x1_b=x1_b)
        job, finish = fetch(i + 1, in_mixer)
        if job is None:
            hcat = _mm(x1_b, W["w_in_p"], name="in_proj" + tag)
        else:
            hcat, got = _mm(x1_b, W["w_in_p"], side=job, name="in_proj" + tag)
            finish(got)
        job, finish = fetch(i, lambda n: True)
        ao, got = _attn_fwd(hcat, bias, W["sinks"], nbatch, side=job, name="swa" + tag)
        finish(got)
        ya = _mm(ao, W["w_branch_a"], name="branch_a" + tag)
        c, qn, kn, vs = _conv_prep(hcat, W["conv_w"], nbatch, name="conv_prep" + tag)
        a_row = _lane_row(W["a_log"])
        dt_row = _lane_row(W["dt_bias"])
        gb, bb = _gates(hcat, a_row, dt_row, name="gates" + tag)
        job, finish = fetch(i + 1, lambda n: True)
        (u, w, qd, kt, aa, gl, ti), got = _dn_prep(qn, kn, vs, gb, bb, side=job, name="dn_prep" + tag)
        finish(got)
        o, ss = _dn_scan(u, w, qd, kt, aa, gl, nbatch, name="dn_scan" + tag)
        on = _rms_gate(o, hcat, W["dn_norm_g"], name="rms_gate" + tag)
        yb = _mm(on, W["w_branch_b"], name="branch_b" + tag)
        mix = _merge(ya, yb, hcat, name="merge" + tag)
        r2, x2, x2_b = _mm_res_ln(mix, W["w_out"], x1, W["ln_g"][1], W["ln_b"][1],
                                  alpha=DN_ALPHA, c=1.0, name="out_proj_ln" + tag)
        L.update(hcat=hcat, ao=ao, ya=ya, c=c, qn=qn, kn=kn, vs=vs, gb=gb, bb=bb, a_row=a_row, dt_row=dt_row,
                 u=u, w=w, qd=qd, kt=kt, aa=aa, gl=gl, ti=ti, o=o, ss=ss, on=on, yb=yb, mix=mix, r2=r2, x2_b=x2_b)
        a, _ = _ffn_up_act(x2_b, W["ffn_w13"][1], name="ffn_up_act" + tag + "b")
        r3, x3, x3_b = _mm_res_ln(a, W["ffn_w2"][1], x2, W["ln_g"][2], W["ln_b"][2],
                                  alpha=DN_ALPHA, c=0.5, name="ffn_down_ln" + tag + "b")
        L.update(a1=a, r3=r3)
        saved.append(L)
        xin, xin_b = x3, x3_b

    part, dy = _loss_head(xin, tgt, name="loss_head")
    loss = jnp.sum(part)

    grads = {k: [None] * depth for k in ("ln_g", "ln_b", "ffn_w13", "ffn_w2", "w_in", "conv_w", "a_log", "dt_bias",
                                          "dn_norm_g", "sinks", "w_branch_a", "w_branch_b", "w_out")}
    dbias_total = None
    for i in reversed(range(depth)):
        L = saved[i]
        W = layer_wts[i]
        tag = f"_l{i}"
        dln_g, dln_b, dw13, dw2 = [None] * 3, [None] * 3, [None] * 2, [None] * 2

        def ffn_bwd(dyo, r, xprev_b, asave, j, sfx):
            dres, df, dln_g[2 * j], dln_b[2 * j] = _ln_bwd(dyo, r, W["ln_g"][2 * j], alpha=DN_ALPHA, c=0.5,
                                                           name="ln_bwd" + tag + sfx)
            job = reducer.job_a() if (reducer is not None and j == 1) else None
            dh, swapped = _ffn_bwd_mid(xprev_b, W["ffn_w13"][j], df, W["ffn_w2"][j], side=job,
                                       name="ffn_bwd_mid" + tag + sfx)
            if job is not None:
                reducer.done_a(swapped)
            dw2[j] = _mm(asave, df, ta=True, name="ffn_w2_grad" + tag + sfx)
            dw13[j] = _mm(xprev_b, dh, ta=True, b_halves=True, name="ffn_w13_grad" + tag + sfx)
            return _mm_wt(dh, W["ffn_w13"][j], dres, a_halves=True, name="ffn_up_bwd" + tag + sfx)

        dx2 = ffn_bwd(dy, L["r3"], L["x2_b"], L["a1"], 1, "b")

        dres2, dymix, dln_g[1], dln_b[1] = _ln_bwd(dx2, L["r2"], W["ln_g"][1], alpha=DN_ALPHA, c=1.0,
                                                   name="ln_bwd" + tag + "m")
        hcat = L["hcat"]
        dmix = _mm(dymix, W["w_out"], tb=True, name="out_proj_bwd" + tag)
        grads["w_out"][i] = _mm(L["mix"], dymix, ta=True, name="w_out_grad" + tag)
        dya, dyb, dgate = _merge_bwd(dmix, L["ya"], L["yb"], hcat, name="merge_bwd" + tag)
        dao = _mm(dya, W["w_branch_a"], tb=True, name="branch_a_bwd" + tag)
        grads["w_branch_a"][i] = _mm(L["ao"], dya, ta=True, name="w_branch_a_grad" + tag)
        don = _mm(dyb, W["w_branch_b"], tb=True, name="branch_b_bwd" + tag)
        grads["w_branch_b"][i] = _mm(L["on"], dyb, ta=True, name="w_branch_b_grad" + tag)
        do, dz, ddn = _rms_gate_bwd(don, L["o"], hcat, W["dn_norm_g"], name="rms_gate_bwd" + tag)
        grads["dn_norm_g"][i] = ddn.reshape(VAL_DIM_B)
        du, dw, dqd, dkt, daa, dgl = _dn_scan_bwd(L["u"], L["w"], L["qd"], L["kt"], L["aa"], L["gl"], L["ss"], do,
                                                  nbatch, name="dn_scan_bwd" + tag)
        job = reducer.job_b() if reducer is not None else None
        (dqn, dkn, dvs, dgb, dbb), exchanged = _dn_prep_bwd(L["qn"], L["kn"], L["vs"], L["gb"], L["bb"], L["ti"], du, dw,
                                                            dqd, dkt, daa, dgl, side=job, name="dn_prep_bwd" + tag)
        if job is not None:
            reducer.done_b(exchanged)
        dc = _conv_prep_bwd_pointwise(dqn, dkn, dvs, L["c"], name="conv_prep_bwd" + tag)
        dconv, dconv_w = _conv_bwd(dc, hcat, W["conv_w"], nbatch, name="conv_bwd" + tag)
        grads["conv_w"][i] = dconv_w[:CONV_K]
        dbd, da_log, ddt = _gates_bwd(dgb, dbb, hcat, L["a_row"], L["dt_row"], name="gates_bwd" + tag)
        grads["a_log"][i] = da_log[0, N_HEADS_B:2 * N_HEADS_B]
        grads["dt_bias"][i] = ddt[0, N_HEADS_B:2 * N_HEADS_B]
        dq, dk, dv, dbias, dsink = _attn_bwd(hcat, bias, W["sinks"], dao, nbatch, name="swa_bwd" + tag)
        grads["sinks"][i] = dsink.reshape(N_HEADS_A)
        dbias_total = dbias if dbias_total is None else dbias_total + dbias
        dhcat = jnp.concatenate([dq, dz, dk, dv, dbd, jnp.zeros((t, HC_CONV - HC_BD - 128), BF16), dconv, dgate], axis=1)
        dw_in_p = _mm(L["x1_b"], dhcat, ta=True, name="w_in_grad" + tag)
        grads["w_in"][i] = _dw_in_by_owner(dw_in_p)
        dx1 = _mm_wt(dhcat, W["w_in_p"], dres2, name="in_proj_bwd" + tag)

        dy = ffn_bwd(dx1, L["r1"], L["x0_b"], L["a0"], 0, "a")
        grads["ln_g"][i] = jnp.concatenate(dln_g, axis=0)
        grads["ln_b"][i] = jnp.concatenate(dln_b, axis=0)
        grads["ffn_w13"][i] = dw13
        grads["ffn_w2"][i] = dw2
        if reducer is not None:
            reducer.layer_done(i, {n: grads[n][i] for n in MATRIX_NAMES})

    out = {k: (v if k in MATRIX_NAMES else jnp.stack(v)) for k, v in grads.items()}
    drel = _mm(dbias_total.reshape(N_HEADS_A, WINDOW * 2 * WINDOW), onehot, name="rel_bias_grad")
    out["rel_bias"] = drel[:, :NUM_BUCKETS].T
    return loss, dy.reshape(nbatch, seq, d), out


N_CHIPS = 4
MESH_ID = pl.DeviceIdType.MESH
HBM_SPEC = pl.BlockSpec(memory_space=pltpu.HBM)


def _place():
    x, y, c = lax.axis_index("x"), lax.axis_index("y"), lax.axis_index("c")
    others = [(1 - x, y), (x, 1 - y), (1 - x, 1 - y)]
    return x, y, c, others


def _chip_index(cx, cy):
    return 2 * cx + cy


def _gather_sems(n):
    return [pltpu.SemaphoreType.DMA((n, 3)), pltpu.SemaphoreType.DMA((n, 3)), pltpu.SemaphoreType.DMA((n,))]


def _gather_copies(ins, outs, send_sems, recv_sems, local_sems):
    x, y, c, others = _place()
    me = _chip_index(x, y)
    copies = []
    for i in range(len(ins)):
        copies.append(pltpu.make_async_copy(ins[i], outs[i].at[me], local_sems.at[i]))
        for k, (ox, oy) in enumerate(others):
            copies.append(pltpu.make_async_remote_copy(src_ref=ins[i], dst_ref=outs[i].at[me], send_sem=send_sems.at[i, k],
                                                       recv_sem=recv_sems.at[i, k], device_id=(ox, oy, c),
                                                       device_id_type=MESH_ID))
    return copies


def _gather_job(tensors):
    return dict(ins=list(tensors), out_shape=[jax.ShapeDtypeStruct((N_CHIPS,) + t.shape, t.dtype) for t in tensors],
                scratch=_gather_sems(len(tensors)), make=_gather_copies)


def _run_job(job, *, name):
    n_in, n_out = len(job["ins"]), len(job["out_shape"])

    def body(*refs):
        copies = job["make"](refs[:n_in], refs[n_in:n_in + n_out], *refs[n_in + n_out:])
        for cp in copies:
            cp.start()
        for cp in copies:
            cp.wait()

    return pl.pallas_call(
        body, name=name, in_specs=[HBM_SPEC] * n_in, out_specs=[HBM_SPEC] * n_out,
        out_shape=list(job["out_shape"]), scratch_shapes=list(job["scratch"]),
    )(*job["ins"])


def _allgather_devices(v, *, name):
    def body(v_ref, o_ref, send_sems, recv_sems, local_sem):
        x, y, c, _ = _place()
        me = 4 * x + 2 * y + c
        loc = pltpu.make_async_copy(v_ref, o_ref.at[me], local_sem)
        loc.start()
        copies = [loc]
        for k in range(1, 8):
            px, py, pc = x ^ (k >> 2), y ^ ((k >> 1) & 1), c ^ (k & 1)
            cp = pltpu.make_async_remote_copy(src_ref=v_ref, dst_ref=o_ref.at[me], send_sem=send_sems.at[k - 1],
                                              recv_sem=recv_sems.at[k - 1], device_id=(px, py, pc), device_id_type=MESH_ID)
            cp.start()
            copies.append(cp)
        for cp in copies:
            cp.wait()

    return pl.pallas_call(
        body, name=name, in_specs=[HBM_SPEC], out_specs=HBM_SPEC,
        out_shape=jax.ShapeDtypeStruct((8,) + v.shape, v.dtype),
        scratch_shapes=[pltpu.SemaphoreType.DMA((7,)), pltpu.SemaphoreType.DMA((7,)), pltpu.SemaphoreType.DMA],
    )(v)


def _sum_slots(g, *, name):
    nb, n, r, l = g.shape
    tr = r // 2 if r % 32 == 0 else r

    def body(g_ref, o_ref):
        acc = g_ref[0].astype(F32)
        for k in range(1, n):
            acc = acc + g_ref[k].astype(F32)
        o_ref[...] = acc

    return pl.pallas_call(
        body, name=name, grid=(nb, r // tr),
        in_specs=[pl.BlockSpec((None, n, tr, l), lambda b, i: (b, 0, i, 0))],
        out_specs=pl.BlockSpec((None, tr, l), lambda b, i: (b, i, 0)),
        out_shape=jax.ShapeDtypeStruct((nb, r, l), F32),
        compiler_params=_params(dimension_semantics=("parallel", "parallel")),
    )(g)


def _half_window(ref, kind, h):
    if kind == "rows":
        return ref.at[:, h]
    r = ref.shape[0] // 2
    return ref.at[pl.ds(pl.multiple_of(h * r, r), r), :]


def _owner_window(ref, kind, o):
    if kind == "rows":
        return ref.at[o]
    cols = ref.shape[1] // N_CHIPS
    return ref.at[:, pl.ds(pl.multiple_of(o * cols, cols), cols)]


def _half_shape(g, kind):
    return (g.shape[0],) + g.shape[2:] if kind == "rows" else (g.shape[0] // 2, g.shape[1])


def _swap_job(gs, kinds):
    n = len(gs)

    def make(ins, outs, send_sems, recv_sems):
        x, y, c, _ = _place()
        return [pltpu.make_async_remote_copy(src_ref=_half_window(ins[i], kinds[i], 1 - c), dst_ref=outs[i],
                                             send_sem=send_sems.at[i], recv_sem=recv_sems.at[i],
                                             device_id=(x, y, 1 - c), device_id_type=MESH_ID) for i in range(n)]

    return dict(ins=list(gs), out_shape=[jax.ShapeDtypeStruct(_half_shape(g, k), g.dtype) for g, k in zip(gs, kinds)],
                scratch=[pltpu.SemaphoreType.DMA((n,)), pltpu.SemaphoreType.DMA((n,))], make=make)


def _pair_sum(g, got, kind, c_idx, *, name):
    hs = _half_shape(g, kind)

    def body(c_ref, g_ref, r_ref, o_ref):
        o_ref[...] = (g_ref[...] + r_ref[...]).astype(BF16)

    if kind == "rows":
        _, _, r, cols = g.shape
        grid = (N_CHIPS,)
        in_specs = [pl.BlockSpec((None, None, r, cols), lambda o, c_ref: (o, c_ref[0], 0, 0)),
                    pl.BlockSpec((None, r, cols), lambda o, c_ref: (o, 0, 0))]
        out_spec = pl.BlockSpec((None, r, cols), lambda o, c_ref: (o, 0, 0))
    else:
        r, cols = hs
        steps = 4
        tr = r // steps
        grid = (steps,)
        in_specs = [pl.BlockSpec((tr, cols), lambda i, c_ref: (c_ref[0] * steps + i, 0)),
                    pl.BlockSpec((tr, cols), lambda i, c_ref: (i, 0))]
        out_spec = pl.BlockSpec((tr, cols), lambda i, c_ref: (i, 0))
    return pl.pallas_call(
        body, name=name,
        grid_spec=pltpu.PrefetchScalarGridSpec(num_scalar_prefetch=1, grid=grid, in_specs=in_specs, out_specs=out_spec),
        out_shape=jax.ShapeDtypeStruct(hs, BF16),
        compiler_params=_params(dimension_semantics=("parallel",)),
    )(c_idx, g, got)


def _exchange_job(ss, kinds):
    n = len(ss)

    def shard_shape(s, kind):
        return s.shape[1:] if kind == "rows" else (s.shape[0], s.shape[1] // N_CHIPS)

    def make(ins, outs, send_sems, recv_sems, local_sems):
        x, y, c, others = _place()
        me = _chip_index(x, y)
        copies = []
        for i in range(n):
            dst = outs[i].at[me]
            copies.append(pltpu.make_async_copy(_owner_window(ins[i], kinds[i], me), dst, local_sems.at[i]))
            for k, (ox, oy) in enumerate(others):
                copies.append(pltpu.make_async_remote_copy(
                    src_ref=_owner_window(ins[i], kinds[i], _chip_index(ox, oy)), dst_ref=dst, send_sem=send_sems.at[i, k],
                    recv_sem=recv_sems.at[i, k], device_id=(ox, oy, c), device_id_type=MESH_ID))
        return copies

    return dict(ins=list(ss), out_shape=[jax.ShapeDtypeStruct((N_CHIPS,) + shard_shape(s, k), s.dtype)
                                         for s, k in zip(ss, kinds)],
                scratch=_gather_sems(n), make=make)


def _send_halves(fs, *, name):
    n = len(fs)

    def body(*refs):
        ins, outs, send_sems, recv_sems = refs[:n], refs[n:2 * n], refs[2 * n], refs[2 * n + 1]
        x, y, c, _ = _place()
        copies = [pltpu.make_async_remote_copy(src_ref=ins[i], dst_ref=outs[i], send_sem=send_sems.at[i],
                                               recv_sem=recv_sems.at[i], device_id=(x, y, 1 - c), device_id_type=MESH_ID)
                  for i in range(n)]
        for cp in copies:
            cp.start()
        for cp in copies:
            cp.wait()

    return pl.pallas_call(
        body, name=name, in_specs=[HBM_SPEC] * n, out_specs=[HBM_SPEC] * n,
        out_shape=[jax.ShapeDtypeStruct(f.shape, f.dtype) for f in fs],
        scratch_shapes=[pltpu.SemaphoreType.DMA((n,)), pltpu.SemaphoreType.DMA((n,))],
    )(*fs)


def _adamw_halves(w, m, v, own, other, c_idx, *, name):
    shape = w.shape
    nl, r, cols = own.shape
    w4, m4, v4 = (a.reshape(nl, 2, r, cols) for a in (w, m, v))
    tr = r if r * cols * 4 <= 3 * 512 * 1024 else _tile(r, tuple(c for c in (256, 128, 64, 32, 16, 8) if c * cols <= 256 * 1024))

    def body(c_ref, w_ref, m_ref, v_ref, own_ref, other_ref, g_ref, d_ref, nm_ref, nv_ref):
        g_ = jnp.where(pl.program_id(1) == c_ref[0], own_ref[...], other_ref[...])
        g_ref[...] = g_
        d_ref[...], nm_ref[...], nv_ref[...] = _adamw_math(w_ref[...], g_, m_ref[...], v_ref[...])

    full = pl.BlockSpec((None, None, tr, cols), lambda l, h, i, c_ref: (l, h, i, 0))
    half = pl.BlockSpec((None, tr, cols), lambda l, h, i, c_ref: (l, i, 0))
    outs = pl.pallas_call(
        body, name=name,
        grid_spec=pltpu.PrefetchScalarGridSpec(num_scalar_prefetch=1, grid=(nl, 2, r // tr),
                                               in_specs=[full, full, full, half, half], out_specs=[full] * 4),
        out_shape=[jax.ShapeDtypeStruct((nl, 2, r, cols), F32)] * 4,
        compiler_params=_params(dimension_semantics=("parallel", "parallel", "parallel")),
    )(c_idx, w4, m4, v4, own, other)
    return tuple(o.reshape(shape) for o in outs)


SHARD_AXIS = {"rel_bias": None, "ln_g": 2, "ln_b": 2, "ffn_w13": 3, "ffn_w2": 2, "w_in": 2, "conv_w": 2, "a_log": None,
              "dt_bias": None, "dn_norm_g": None, "sinks": None, "w_branch_a": 1, "w_branch_b": 1, "w_out": 1}
WEIGHT_NAMES = tuple(SHARD_AXIS)
SMALL_NAMES = tuple(n for n in WEIGHT_NAMES if n not in MATRIX_NAMES)
PACK_LANES = 1024


def _unshard(gathered, axis):
    g = jnp.moveaxis(gathered, 0, axis)
    return g.reshape(g.shape[:axis] + (g.shape[axis] * g.shape[axis + 1],) + g.shape[axis + 2:])


class _GradReducer:
    def __init__(self, c_idx):
        self.c_idx = c_idx
        self.swapping = None
        self.exchanging = None
        self.reduced = {}

    @staticmethod
    def _views(grads):
        out = []
        for n in MATRIX_NAMES:
            for g in (grads[n] if n in ("ffn_w13", "ffn_w2") else [grads[n]]):
                if n == "ffn_w13":
                    out.append((n, g, "cols"))
                elif n == "w_in":
                    out.append((n, g.reshape(N_CHIPS, 2, g.shape[1] // 2, g.shape[2]), "rows"))
                else:
                    out.append((n, g.reshape(N_CHIPS, 2, g.shape[0] // (2 * N_CHIPS), g.shape[1]), "rows"))
        return out

    def layer_done(self, layer, grads):
        assert self.swapping is None
        self.swapping = (layer, self._views(grads))

    def job_a(self):
        if self.swapping is None:
            return None
        _, views = self.swapping
        return _swap_job([g for _, g, _ in views], [k for _, _, k in views])

    def done_a(self, got):
        layer, views = self.swapping
        self.swapping = None
        assert self.exchanging is None
        ss = [_pair_sum(g, r, k, self.c_idx, name=f"rs_pair_sum_l{layer}_{i}")
              for i, ((_, g, k), r) in enumerate(zip(views, got))]
        self.exchanging = (layer, views, ss)

    def job_b(self):
        if self.exchanging is None:
            return None
        _, views, ss = self.exchanging
        return _exchange_job(ss, [k for _, _, k in views])

    def done_b(self, ex):
        layer, views, _ = self.exchanging
        self.exchanging = None
        red = {}
        for i, ((n, _, _), e) in enumerate(zip(views, ex)):
            red.setdefault(n, []).append(_sum_slots(e[None], name=f"rs_chip_sum_l{layer}_{i}")[0])
        self.reduced[layer] = red

    def flush(self):
        if self.swapping is not None:
            self.done_a(_run_job(self.job_a(), name="rs_swap_halves_last"))
        if self.exchanging is not None:
            self.done_b(_run_job(self.job_b(), name="rs_exchange_chips_last"))

    def result(self):
        self.flush()
        own = [jnp.stack([f for layer in sorted(self.reduced) for f in self.reduced[layer][n]]) for n in MATRIX_NAMES]
        other = _send_halves(own, name="rs_send_halves")
        return {n: (a, b) for n, a, b in zip(MATRIX_NAMES, own, other)}


def _reduce_small(grads):
    flat = [grads[n].astype(F32).reshape(-1) for n in SMALL_NAMES]
    total = sum(f.shape[0] for f in flat)
    rows = -(-total // (16 * PACK_LANES)) * 16
    vec = jnp.concatenate(flat + [jnp.zeros((rows * PACK_LANES - total,), F32)]).reshape(rows, PACK_LANES)
    s = _sum_slots(_allgather_devices(vec, name="small_allgather")[None], name="small_sum").reshape(-1)
    out, o = {}, 0
    for n, f in zip(SMALL_NAMES, flat):
        out[n] = s[o:o + f.shape[0]].reshape(grads[n].shape)
        o += f.shape[0]
    return out


def kernel(x, rel_bias, ln_g, ln_b, ffn_w13, ffn_w2, w_in, conv_w, a_log, dt_bias, dn_norm_g, sinks, w_branch_a, w_branch_b, w_out, loss_target, m_rel_bias, m_ln_g, m_ln_b, m_ffn_w13, m_ffn_w2, m_w_in, m_conv_w, m_a_log, m_dt_bias, m_dn_norm_g, m_sinks, m_w_branch_a, m_w_branch_b, m_w_out, v_rel_bias, v_ln_g, v_ln_b, v_ffn_w13, v_ffn_w2, v_w_in, v_conv_w, v_a_log, v_dt_bias, v_dn_norm_g, v_sinks, v_w_branch_a, v_w_branch_b, v_w_out):
    w = dict(rel_bias=rel_bias, ln_g=ln_g, ln_b=ln_b, ffn_w13=ffn_w13, ffn_w2=ffn_w2, w_in=w_in, conv_w=conv_w,
             a_log=a_log, dt_bias=dt_bias, dn_norm_g=dn_norm_g, sinks=sinks, w_branch_a=w_branch_a,
             w_branch_b=w_branch_b, w_out=w_out)
    m = dict(rel_bias=m_rel_bias, ln_g=m_ln_g, ln_b=m_ln_b, ffn_w13=m_ffn_w13, ffn_w2=m_ffn_w2, w_in=m_w_in,
             conv_w=m_conv_w, a_log=m_a_log, dt_bias=m_dt_bias, dn_norm_g=m_dn_norm_g, sinks=m_sinks,
             w_branch_a=m_w_branch_a, w_branch_b=m_w_branch_b, w_out=m_w_out)
    v = dict(rel_bias=v_rel_bias, ln_g=v_ln_g, ln_b=v_ln_b, ffn_w13=v_ffn_w13, ffn_w2=v_ffn_w2, w_in=v_w_in,
             conv_w=v_conv_w, a_log=v_a_log, dt_bias=v_dt_bias, dn_norm_g=v_dn_norm_g, sinks=v_sinks,
             w_branch_a=v_w_branch_a, w_branch_b=v_w_branch_b, w_out=v_w_out)

    depth = w_in.shape[0]
    sharded = [n for n in WEIGHT_NAMES if SHARD_AXIS[n] is not None]

    def shards_of(i):
        out = {}
        for n in sharded:
            s = w[n][i].astype(MXU_DTYPE) if n in MATRIX_NAMES else w[n][i]
            if n in ("ffn_w13", "ffn_w2"):
                out[n + "_0"], out[n + "_1"] = s[0], s[1]
            else:
                out[n] = s
        return out

    def assemble(i, gathered):
        lw = {}
        for n, g in gathered.items():
            if n[:-2] in ("ffn_w13", "ffn_w2"):
                lw.setdefault(n[:-2], {})[int(n[-1])] = _unshard(g, SHARD_AXIS[n[:-2]] - 2)
            elif n == "w_in":
                lw["w_in_p"] = _repack_w_in(_unshard(g, SHARD_AXIS[n] - 1))
            else:
                lw[n] = _unshard(g, SHARD_AXIS[n] - 1)
        return lw

    layer_wts = [{n: w[n][i] for n in ("a_log", "dt_bias", "dn_norm_g", "sinks")} for i in range(depth)]
    shards = [shards_of(i) for i in range(depth)]
    first = ("ffn_w13_0", "ffn_w2_0", "ln_g", "ln_b")
    got = _run_job(_gather_job([shards[0].pop(n) for n in first]), name="weights_allgather_first")
    layer_wts[0].update(assemble(0, dict(zip(first, got))))
    c_idx = lax.axis_index("c").astype(jnp.int32).reshape(1)
    reducer = _GradReducer(c_idx)
    loss_part, grad_x, grads = _local_step(x, loss_target, rel_bias, layer_wts, side_shards=shards,
                                           side_assemble=assemble, reducer=reducer)
    loss = lax.psum(loss_part, ("x", "y", "c"))

    halves = reducer.result()
    chip = _chip_index(lax.axis_index("x"), lax.axis_index("y"))
    small = _reduce_small(grads)
    outs = {}
    for n in WEIGHT_NAMES:
        if n in MATRIX_NAMES:
            outs[n] = _adamw_halves(w[n], m[n], v[n], *halves[n], c_idx, name="adamw_" + n)
        else:
            axis = SHARD_AXIS[n]
            g = small[n]
            if axis is not None:
                g = lax.dynamic_slice_in_dim(g, chip * w[n].shape[axis], w[n].shape[axis], axis)
            outs[n] = (g,) + _adamw(w[n], g, m[n], v[n], name="adamw_" + n)
    return (loss, grad_x, *[outs[n][0] for n in WEIGHT_NAMES], *[outs[n][1] for n in WEIGHT_NAMES],
            *[outs[n][2] for n in WEIGHT_NAMES], *[outs[n][3] for n in WEIGHT_NAMES])
```

```python
import functools
import math

import numpy as np
import jax
import jax.numpy as jnp
from jax import lax
from jax.experimental import pallas as pl
from jax.experimental.pallas import tpu as pltpu

F32 = jnp.float32
BF16 = jnp.bfloat16
MXU_DTYPE = BF16
HIGHEST = lax.Precision.HIGHEST

D_MODEL = 1024
N_HEADS_A = 16
N_KV_A = 4
HEAD_DIM_A = 64
GROUP_A = N_HEADS_A // N_KV_A
WINDOW = 128
N_HEADS_B = 8
KEY_DIM_B = 128
VAL_DIM_B = 128
CONV_K = 4
CHUNK = 64
D_FF = 2816
NUM_BUCKETS = 32
MAX_DISTANCE = 128
DEPTH = 4
DN_ALPHA = (2 * DEPTH) ** 0.25
LN_EPS = 1e-5
NORM_EPS = 1e-6
NEG_INF = -1e30

Q_A = N_HEADS_A * HEAD_DIM_A
KV_W = N_KV_A * HEAD_DIM_A
QK_B = N_HEADS_B * KEY_DIM_B
V_B = N_HEADS_B * VAL_DIM_B
CONV_CH = 2 * QK_B + V_B
N_IN = Q_A + 2 * KV_W + CONV_CH + 2 * N_HEADS_B + V_B + 2 * D_MODEL

ADAM_LR = 0.001
ADAM_B1 = 0.9
ADAM_B2 = 0.999
ADAM_EPS = 1e-08
ADAM_WD = 0.01
ADAM_STEP = 10

HC_W = 8192
HC_Q = 0
HC_Z = 1024
HC_K = 2048
HC_V = 2304
HC_BD = 2560
HC_CONV = 3072
HC_GATE = 6144

GROUP_T = 256
DN_BWD_HEADS = 4
ROW_T = 256
VMEM_LIMIT_BYTES = 48 * 1024 * 1024


def _params(vmem=VMEM_LIMIT_BYTES, **kw):
    return pltpu.CompilerParams(vmem_limit_bytes=vmem, **kw)


def _tile(n, cands):
    for c in cands:
        if n % c == 0:
            return c
    return n


def _dot(a, b, dims=(((1,), (0,)), ((), ())), exact=False):
    if exact:
        return lax.dot_general(a.astype(F32), b.astype(F32), dims, precision=HIGHEST, preferred_element_type=F32)
    return lax.dot_general(a.astype(MXU_DTYPE), b.astype(MXU_DTYPE), dims, preferred_element_type=F32)


_NN = (((1,), (0,)), ((), ()))
_NT = (((1,), (1,)), ((), ()))
_TN = (((0,), (0,)), ((), ()))


def _sigmoid(x):
    return 1.0 / (1.0 + jnp.exp(-x))


def _silu(x):
    return x * _sigmoid(x)


def _dsilu(x):
    s = _sigmoid(x)
    return s * (1.0 + x * (1.0 - s))


def _call(body, *, name, grid, in_specs, out_specs, out_shape, scratch_shapes=(), semantics, args, side=None,
          vmem=VMEM_LIMIT_BYTES):
    in_specs, out_specs, out_shape = list(in_specs), list(out_specs), list(out_shape)
    if side is None:
        outs = pl.pallas_call(body, name=name, grid=grid, in_specs=in_specs, out_specs=out_specs, out_shape=out_shape,
                              scratch_shapes=list(scratch_shapes),
                              compiler_params=_params(vmem=vmem, dimension_semantics=semantics))(*args)
        return outs, None
    n_in, n_out, n_scr = len(in_specs), len(out_specs), len(scratch_shapes)
    s_in, s_out = len(side["ins"]), len(side["out_shape"])

    def hosted(*refs):
        main_in, side_in = refs[:n_in], refs[n_in:n_in + s_in]
        o0 = n_in + s_in
        main_out, side_out = refs[o0:o0 + n_out], refs[o0 + n_out:o0 + n_out + s_out]
        rest = refs[o0 + n_out + s_out:]
        copies = side["make"](side_in, side_out, *rest[n_scr:])
        ids = [pl.program_id(d) for d in range(len(grid))]
        first = functools.reduce(jnp.logical_and, [i == 0 for i in ids])
        last = functools.reduce(jnp.logical_and, [i == g - 1 for i, g in zip(ids, grid)])

        @pl.when(first)
        def _():
            for cp in copies:
                cp.start()

        body(*main_in, *main_out, *rest[:n_scr])

        @pl.when(last)
        def _():
            for cp in copies:
                cp.wait()

    outs = pl.pallas_call(
        hosted, name=name, grid=grid,
        in_specs=in_specs + [HBM_SPEC] * s_in, out_specs=out_specs + [HBM_SPEC] * s_out,
        out_shape=out_shape + list(side["out_shape"]),
        scratch_shapes=list(scratch_shapes) + list(side["scratch"]),
        compiler_params=_params(vmem=vmem, dimension_semantics=("arbitrary",) * len(grid)),
    )(*args, *side["ins"])
    return list(outs[:n_out]), list(outs[n_out:])


def _mm(a, b, *, ta=False, tb=False, a_halves=False, b_halves=False, out_dtype=F32, add=None, exact=False, side=None,
        name):
    if a_halves:
        m, kdim = a.shape[1], 2 * a.shape[2]
    else:
        (kdim, m) = a.shape if ta else a.shape[::-1]
    if b_halves:
        kb, n = b.shape[1], 2 * b.shape[2]
    else:
        (n, kb) = b.shape if tb else b.shape[::-1]
    assert kdim == kb, (a.shape, b.shape, ta, tb)
    if ta and not tb and not exact and add is None and side is None and kdim <= TN_WHOLE_K:
        return _mm_tn(a, b, b_halves=b_halves, out_dtype=out_dtype, name=name)
    tn = _tile(n, (1024, 1408, 512, 256, 128))
    tk = _tile(kdim, (1024, 1408, 512, 256, 128))
    nk = kdim // tk
    tm = _tile(m, (1024, 1408, 512, 256, 128) if nk > 1 else (512, 256, 128))
    nj = n // tn
    dims = (((0 if ta else 1,), (1 if tb else 0,)), ((), ()))
    has_add = add is not None

    def body(*refs):
        if has_add:
            a_ref, b_ref, add_ref, o_ref = refs[:4]
        else:
            a_ref, b_ref, o_ref = refs[:3]
            add_ref = None
        part = _dot(a_ref[...], b_ref[...], dims, exact)

        def finish(acc):
            if has_add:
                acc = acc + add_ref[...].astype(F32)
            o_ref[...] = acc.astype(out_dtype)

        if nk == 1:
            finish(part)
        else:
            acc_ref = refs[-1]
            k = pl.program_id(2)

            @pl.when(k == 0)
            def _():
                acc_ref[...] = part

            @pl.when(k > 0)
            def _():
                acc_ref[...] += part

            @pl.when(k == nk - 1)
            def _():
                finish(acc_ref[...])

    if a_halves:
        assert not ta and nk % 2 == 0
        a_spec = pl.BlockSpec((None, tm, tk), lambda i, j, k: (k // (nk // 2), i, k % (nk // 2)))
    elif ta:
        a_spec = pl.BlockSpec((tk, tm), lambda i, j, k: (k, i))
    else:
        a_spec = pl.BlockSpec((tm, tk), lambda i, j, k: (i, k))
    if b_halves:
        assert not tb and nj % 2 == 0
        b_spec = pl.BlockSpec((None, tk, tn), lambda i, j, k: (j // (nj // 2), k, j % (nj // 2)))
    elif tb:
        b_spec = pl.BlockSpec((tn, tk), lambda i, j, k: (j, k))
    else:
        b_spec = pl.BlockSpec((tk, tn), lambda i, j, k: (k, j))
    o_spec = pl.BlockSpec((tm, tn), lambda i, j, k: (i, j))
    in_specs = [a_spec, b_spec] + ([o_spec] if has_add else [])
    args = (a, b) + ((add,) if has_add else ())
    outs, side_outs = _call(
        body, name=name, grid=(m // tm, nj, nk), in_specs=in_specs, out_specs=[o_spec],
        out_shape=[jax.ShapeDtypeStruct((m, n), out_dtype)],
        scratch_shapes=[pltpu.VMEM((tm, tn), F32)] if nk > 1 else [],
        semantics=("parallel", "parallel", "arbitrary"), args=args, side=side)
    return outs[0] if side is None else (outs[0], side_outs)


TN_WHOLE_K = 8192
VMEM_LIMIT_BIG_BYTES = 56 * 1024 * 1024


def _mm_tn(a, b, *, b_halves=False, out_dtype=F32, name):
    kdim, m = a.shape
    n = 2 * b.shape[2] if b_halves else b.shape[1]
    if m <= 1024:
        tm, tn = m, _tile(n // 2 if b_halves else n, (256, 128))
    else:
        tm, tn = _tile(m, (256, 128)), _tile(n, (1024, 512, 256, 128))
    nj = n // tn

    def body(a_ref, b_ref, o_ref):
        o_ref[...] = _dot(a_ref[...], b_ref[...], _TN).astype(out_dtype)

    if b_halves:
        b_spec = pl.BlockSpec((None, kdim, tn), lambda i, j: (j // (nj // 2), 0, j % (nj // 2)))
    else:
        b_spec = pl.BlockSpec((kdim, tn), lambda i, j: (0, j))
    return pl.pallas_call(
        body, name=name, grid=(m // tm, nj),
        in_specs=[pl.BlockSpec((kdim, tm), lambda i, j: (0, i)), b_spec],
        out_specs=pl.BlockSpec((tm, tn), lambda i, j: (i, j)),
        out_shape=jax.ShapeDtypeStruct((m, n), out_dtype),
        compiler_params=pltpu.CompilerParams(vmem_limit_bytes=VMEM_LIMIT_BIG_BYTES,
                                             dimension_semantics=("parallel", "parallel")),
    )(a, b)


def _mm_w(a, w, *, side=None, name):
    m, kdim = a.shape
    n = w.shape[1]
    tm = _tile(m, (256, 128))

    def body(a_ref, w_ref, o_ref):
        o_ref[...] = _dot(a_ref[...], w_ref[...])

    outs, side_outs = _call(
        body, name=name, grid=(m // tm,),
        in_specs=[pl.BlockSpec((tm, kdim), lambda i: (i, 0)),
                  pl.BlockSpec((kdim, n), lambda i: (0, 0), pipeline_mode=pl.Buffered(1))],
        out_specs=[pl.BlockSpec((tm, n), lambda i: (i, 0))], out_shape=[jax.ShapeDtypeStruct((m, n), F32)],
        semantics=("parallel",), args=(a, w), side=side, vmem=VMEM_LIMIT_BIG_BYTES)
    return outs[0], side_outs


def _mm_wt(a, w, add, *, a_halves=False, name):
    n, kdim = w.shape
    m = a.shape[1] if a_halves else a.shape[0]
    tm = _tile(m, (512, 256, 128))
    half = kdim // 2

    def body(a_ref, w_ref, add_ref, o_ref):
        if a_halves:
            acc = _dot(a_ref[0], w_ref[:, :half], _NT) + _dot(a_ref[1], w_ref[:, half:], _NT)
        else:
            acc = _dot(a_ref[...], w_ref[...], _NT)
        o_ref[...] = acc + add_ref[...]

    a_spec = pl.BlockSpec((2, tm, half), lambda i: (0, i, 0)) if a_halves else pl.BlockSpec((tm, kdim), lambda i: (i, 0))
    row = pl.BlockSpec((tm, n), lambda i: (i, 0))
    return pl.pallas_call(
        body, name=name, grid=(m // tm,),
        in_specs=[a_spec, pl.BlockSpec((n, kdim), lambda i: (0, 0), pipeline_mode=pl.Buffered(1)), row],
        out_specs=row, out_shape=jax.ShapeDtypeStruct((m, n), F32),
        compiler_params=pltpu.CompilerParams(vmem_limit_bytes=VMEM_LIMIT_BIG_BYTES, dimension_semantics=("parallel",)),
    )(a, w, add)


def _layernorm_rows(r, g, b):
    mu = jnp.mean(r, axis=-1, keepdims=True)
    xc = r - mu
    var = jnp.mean(xc * xc, axis=-1, keepdims=True)
    return xc * lax.rsqrt(var + LN_EPS) * g + b


def _mm_res_ln(a, w, resid, g, b, *, alpha, c, name):
    m, kdim = a.shape
    n = w.shape[1]
    tm = _tile(m, (512, 256, 128))

    def body(a_ref, w_ref, x_ref, g_ref, b_ref, r_ref, y_ref, yb_ref):
        f = _dot(a_ref[...], w_ref[...])
        r = alpha * x_ref[...] + c * f
        r_ref[...] = r
        y = _layernorm_rows(r, g_ref[...], b_ref[...])
        y_ref[...] = y
        yb_ref[...] = y.astype(BF16)

    row = pl.BlockSpec((tm, n), lambda i: (i, 0))
    vec = pl.BlockSpec((1, n), lambda i: (0, 0))
    return pl.pallas_call(
        body, name=name, grid=(m // tm,),
        in_specs=[pl.BlockSpec((tm, kdim), lambda i: (i, 0)), pl.BlockSpec((kdim, n), lambda i: (0, 0)), row, vec, vec],
        out_specs=[row, row, row],
        out_shape=[jax.ShapeDtypeStruct((m, n), F32)] * 2 + [jax.ShapeDtypeStruct((m, n), BF16)],
        compiler_params=_params(dimension_semantics=("parallel",)),
    )(a, w, resid, g.reshape(1, n), b.reshape(1, n))


def _ln_bwd(dy, r, g, *, alpha, c, name):
    m, n = dy.shape
    tm = _tile(m, (512, 256, 128))

    def body(dy_ref, r_ref, g_ref, dres_ref, dbr_ref, dg_ref, db_ref):
        i = pl.program_id(0)
        dy_ = dy_ref[...]
        r_ = r_ref[...]
        mu = jnp.mean(r_, axis=-1, keepdims=True)
        xc = r_ - mu
        var = jnp.mean(xc * xc, axis=-1, keepdims=True)
        rstd = lax.rsqrt(var + LN_EPS)
        xh = xc * rstd
        dxh = dy_ * g_ref[...]
        dr = rstd * (dxh - jnp.mean(dxh, axis=-1, keepdims=True) - xh * jnp.mean(dxh * xh, axis=-1, keepdims=True))
        dres_ref[...] = alpha * dr
        dbr_ref[...] = (c * dr).astype(BF16)
        dg_p = jnp.sum(dy_ * xh, axis=0, keepdims=True)
        db_p = jnp.sum(dy_, axis=0, keepdims=True)

        @pl.when(i == 0)
        def _():
            dg_ref[...] = dg_p
            db_ref[...] = db_p

        @pl.when(i > 0)
        def _():
            dg_ref[...] += dg_p
            db_ref[...] += db_p

    row = pl.BlockSpec((tm, n), lambda i: (i, 0))
    vec = pl.BlockSpec((1, n), lambda i: (0, 0))
    return pl.pallas_call(
        body, name=name, grid=(m // tm,),
        in_specs=[row, row, vec], out_specs=[row, row, vec, vec],
        out_shape=[jax.ShapeDtypeStruct((m, n), F32), jax.ShapeDtypeStruct((m, n), BF16),
                   jax.ShapeDtypeStruct((1, n), F32), jax.ShapeDtypeStruct((1, n), F32)],
        compiler_params=_params(dimension_semantics=("arbitrary",)),
    )(dy, r, g.reshape(1, n))


FFN_TN = D_FF // 2


def _ffn_up_act(x, w13, *, side=None, name):
    m, d = x.shape
    tm = _tile(m, (512, 256, 128))
    nj = D_FF // FFN_TN

    def body(x_ref, g_ref, u_ref, o_ref):
        x_ = x_ref[...]
        o_ref[...] = (_silu(_dot(x_, g_ref[...])) * _dot(x_, u_ref[...])).astype(BF16)

    outs, side_outs = _call(
        body, name=name, grid=(nj, m // tm),
        in_specs=[pl.BlockSpec((tm, d), lambda j, i: (i, 0)), pl.BlockSpec((d, FFN_TN), lambda j, i: (0, j)),
                  pl.BlockSpec((d, FFN_TN), lambda j, i: (0, j + nj))],
        out_specs=[pl.BlockSpec((tm, FFN_TN), lambda j, i: (i, j))],
        out_shape=[jax.ShapeDtypeStruct((m, D_FF), BF16)],
        semantics=("parallel", "parallel"), args=(x, w13, w13), side=side)
    return outs[0], side_outs


def _ffn_bwd_mid(x, w13, df, w2, *, side=None, name):
    m, d = x.shape
    tm = _tile(m, (512, 256, 128))
    nj = D_FF // FFN_TN

    def body(x_ref, g_ref, u_ref, df_ref, w2_ref, o_ref):
        x_ = x_ref[...]
        g = _dot(x_, g_ref[...])
        u = _dot(x_, u_ref[...])
        da = _dot(df_ref[...], w2_ref[...], _NT)
        o_ref[0] = (da * u * _dsilu(g)).astype(BF16)
        o_ref[1] = (da * _silu(g)).astype(BF16)

    outs, side_outs = _call(
        body, name=name, grid=(nj, m // tm),
        in_specs=[pl.BlockSpec((tm, d), lambda j, i: (i, 0)), pl.BlockSpec((d, FFN_TN), lambda j, i: (0, j)),
                  pl.BlockSpec((d, FFN_TN), lambda j, i: (0, j + nj)), pl.BlockSpec((tm, d), lambda j, i: (i, 0)),
                  pl.BlockSpec((FFN_TN, d), lambda j, i: (j, 0))],
        out_specs=[pl.BlockSpec((2, tm, FFN_TN), lambda j, i: (0, i, j))],
        out_shape=[jax.ShapeDtypeStruct((2, m, D_FF), BF16)],
        semantics=("parallel", "parallel"), args=(x, w13, w13, df, w2), side=side)
    return outs[0], side_outs


def _t5_bucket_table():
    r = np.arange(WINDOW)[:, None]
    j = np.arange(2 * WINDOW)[None, :]
    rel = r + WINDOW - j
    n = np.maximum(rel, 0)
    max_exact = NUM_BUCKETS // 2
    nf = np.maximum(n, 1).astype(np.float32)
    large = max_exact + (np.log(nf / np.float32(max_exact)) / np.float32(math.log(MAX_DISTANCE / max_exact))
                         * np.float32(NUM_BUCKETS - max_exact)).astype(np.int32)
    large = np.minimum(large, NUM_BUCKETS - 1)
    bucket = np.where(n < max_exact, n, large)
    in_band = (rel >= 0) & (rel < WINDOW)
    return bucket.astype(np.int32), in_band


def _bucket_onehot():
    bucket, _ = _t5_bucket_table()
    oh = np.zeros((WINDOW * 2 * WINDOW, 128), np.float32)
    oh[np.arange(oh.shape[0]), bucket.reshape(-1)] = 1.0
    return oh


def _stack_heads(x, g):
    hd = HEAD_DIM_A
    return jnp.concatenate([x[:, (GROUP_A * g + h) * hd:(GROUP_A * g + h + 1) * hd] for h in range(GROUP_A)], axis=0)


def _unstack_heads(x):
    return jnp.concatenate([x[h * WINDOW:(h + 1) * WINDOW] for h in range(GROUP_A)], axis=1)


def _attn_probs(q_ref, kp_ref, ko_ref, vp_ref, vo_ref, bias_ref, sink_ref, first_block):
    hd = HEAD_DIM_A
    groups = range(N_KV_A)
    q = q_ref[...]
    qs = [_stack_heads(q, g) * (hd ** -0.5) for g in groups]
    k2 = [jnp.concatenate([kp_ref[:, g * hd:(g + 1) * hd], ko_ref[:, g * hd:(g + 1) * hd]], axis=0) for g in groups]
    v2 = [jnp.concatenate([vp_ref[:, g * hd:(g + 1) * hd], vo_ref[:, g * hd:(g + 1) * hd]], axis=0) for g in groups]
    rr = lax.broadcasted_iota(jnp.int32, (GROUP_A * WINDOW, 2 * WINDOW), 0) % WINDOW
    jj = lax.broadcasted_iota(jnp.int32, (GROUP_A * WINDOW, 2 * WINDOW), 1)
    rel = rr + WINDOW - jj
    valid = (rel >= 0) & (rel < WINDOW) & (jnp.logical_not(first_block) | (jj >= WINDOW))
    s = [_dot(qs[g], k2[g], _NT) for g in groups]
    s = [jnp.where(valid, s[g] + bias_ref[GROUP_A * g:GROUP_A * (g + 1)].reshape(GROUP_A * WINDOW, 2 * WINDOW), NEG_INF)
         for g in groups]
    sk = [jnp.concatenate([jnp.broadcast_to(sink_ref[0:1, GROUP_A * g + h:GROUP_A * g + h + 1], (WINDOW, 1))
                           for h in range(GROUP_A)], axis=0) for g in groups]
    mx = [jnp.maximum(jnp.max(s[g], axis=-1, keepdims=True), sk[g]) for g in groups]
    p = [jnp.exp(s[g] - mx[g]) for g in groups]
    ps = [jnp.exp(sk[g] - mx[g]) for g in groups]
    den = [jnp.sum(p[g], axis=-1, keepdims=True) + ps[g] for g in groups]
    return qs, k2, v2, [p[g] / den[g] for g in groups], [ps[g] / den[g] for g in groups]


def _attn_specs(nb):
    def prev(b, i):
        return (b * nb + jnp.maximum(i - 1, 0))

    q_spec = pl.BlockSpec((WINDOW, Q_A), lambda b, i: (b * nb + i, HC_Q // Q_A))
    kp_spec = pl.BlockSpec((WINDOW, KV_W), lambda b, i: (prev(b, i), HC_K // KV_W))
    ko_spec = pl.BlockSpec((WINDOW, KV_W), lambda b, i: (b * nb + i, HC_K // KV_W))
    vp_spec = pl.BlockSpec((WINDOW, KV_W), lambda b, i: (prev(b, i), HC_V // KV_W))
    vo_spec = pl.BlockSpec((WINDOW, KV_W), lambda b, i: (b * nb + i, HC_V // KV_W))
    bias_spec = pl.BlockSpec((N_HEADS_A, WINDOW, 2 * WINDOW), lambda b, i: (0, 0, 0))
    sink_spec = pl.BlockSpec((1, N_HEADS_A), lambda b, i: (0, 0))
    return [q_spec, kp_spec, ko_spec, vp_spec, vo_spec, bias_spec, sink_spec]


def _attn_fwd(hcat, bias, sink, nbatch, *, side=None, name):
    t = hcat.shape[0]
    nb = t // nbatch // WINDOW

    def body(q_ref, kp_ref, ko_ref, vp_ref, vo_ref, bias_ref, sink_ref, o_ref):
        first = pl.program_id(1) == 0
        _, _, v2, p, _ = _attn_probs(q_ref, kp_ref, ko_ref, vp_ref, vo_ref, bias_ref, sink_ref, first)
        o = [_dot(p[g], v2[g]) for g in range(N_KV_A)]
        o_ref[...] = jnp.concatenate([_unstack_heads(og) for og in o], axis=1).astype(BF16)

    outs, side_outs = _call(
        body, name=name, grid=(nbatch, nb), in_specs=_attn_specs(nb),
        out_specs=[pl.BlockSpec((WINDOW, Q_A), lambda b, i: (b * nb + i, 0))],
        out_shape=[jax.ShapeDtypeStruct((t, Q_A), BF16)], semantics=("parallel", "arbitrary"),
        args=(hcat, hcat, hcat, hcat, hcat, bias, sink.reshape(1, N_HEADS_A)), side=side)
    return outs[0], side_outs


def _attn_bwd(hcat, bias, sink, do, nbatch, *, name):
    t = hcat.shape[0]
    nb = t // nbatch // WINDOW
    hd = HEAD_DIM_A

    def body(q_ref, kp_ref, ko_ref, vp_ref, vo_ref, bias_ref, sink_ref, do_ref,
             dq_ref, dk_ref, dv_ref, dbias_ref, dsink_ref, ck_ref, cv_ref):
        b = pl.program_id(0)
        j = pl.program_id(1)
        first = j == nb - 1

        @pl.when((b == 0) & (j == 0))
        def _():
            dbias_ref[...] = jnp.zeros_like(dbias_ref)
            dsink_ref[...] = jnp.zeros_like(dsink_ref)

        @pl.when(j == 0)
        def _():
            ck_ref[...] = jnp.zeros_like(ck_ref)
            cv_ref[...] = jnp.zeros_like(cv_ref)

        do_ = do_ref[...]
        groups = range(N_KV_A)
        lane = lax.broadcasted_iota(jnp.int32, (1, N_HEADS_A), 1)
        qs, k2, v2, p, ps = _attn_probs(q_ref, kp_ref, ko_ref, vp_ref, vo_ref, bias_ref, sink_ref, first)
        dos = [_stack_heads(do_, g) for g in groups]
        dv2 = [_dot(p[g], dos[g], _TN) for g in groups]
        dp = [_dot(dos[g], v2[g], _NT) for g in groups]
        delta = [jnp.sum(p[g] * dp[g], axis=-1, keepdims=True) for g in groups]
        ds = [p[g] * (dp[g] - delta[g]) for g in groups]
        dqs = [_dot(ds[g], k2[g]) * (hd ** -0.5) for g in groups]
        dk2 = [_dot(ds[g], qs[g], _TN) for g in groups]
        dsink = jnp.zeros((1, N_HEADS_A), F32)
        for g in groups:
            dsk = -(ps[g] * delta[g])
            for h in range(GROUP_A):
                tot = jnp.sum(dsk[h * WINDOW:(h + 1) * WINDOW], axis=0, keepdims=True)
                dsink = dsink + jnp.where(lane == GROUP_A * g + h, tot, 0.0)
            dbias_ref[GROUP_A * g:GROUP_A * (g + 1)] += ds[g].reshape(GROUP_A, WINDOW, 2 * WINDOW)
        dq_ref[...] = jnp.concatenate([_unstack_heads(d) for d in dqs], axis=1).astype(BF16)
        dk_ref[...] = (jnp.concatenate([d[WINDOW:] for d in dk2], axis=1) + ck_ref[...]).astype(BF16)
        dv_ref[...] = (jnp.concatenate([d[WINDOW:] for d in dv2], axis=1) + cv_ref[...]).astype(BF16)
        ck_ref[...] = jnp.concatenate([d[:WINDOW] for d in dk2], axis=1)
        cv_ref[...] = jnp.concatenate([d[:WINDOW] for d in dv2], axis=1)
        dsink_ref[...] += dsink

    def rev(spec):
        return pl.BlockSpec(spec.block_shape, lambda b, j, f=spec.index_map: f(b, nb - 1 - j))

    in_specs = [rev(s) for s in _attn_specs(nb)[:5]] + _attn_specs(nb)[5:]
    in_specs.append(pl.BlockSpec((WINDOW, Q_A), lambda b, j: (b * nb + nb - 1 - j, 0)))
    return pl.pallas_call(
        body, name=name, grid=(nbatch, nb),
        in_specs=in_specs,
        out_specs=[pl.BlockSpec((WINDOW, Q_A), lambda b, j: (b * nb + nb - 1 - j, 0)),
                   pl.BlockSpec((WINDOW, KV_W), lambda b, j: (b * nb + nb - 1 - j, 0)),
                   pl.BlockSpec((WINDOW, KV_W), lambda b, j: (b * nb + nb - 1 - j, 0)),
                   pl.BlockSpec((N_HEADS_A, WINDOW, 2 * WINDOW), lambda b, j: (0, 0, 0)),
                   pl.BlockSpec((1, N_HEADS_A), lambda b, j: (0, 0))],
        out_shape=[jax.ShapeDtypeStruct((t, Q_A), BF16), jax.ShapeDtypeStruct((t, KV_W), BF16),
                   jax.ShapeDtypeStruct((t, KV_W), BF16),
                   jax.ShapeDtypeStruct((N_HEADS_A, WINDOW, 2 * WINDOW), F32),
                   jax.ShapeDtypeStruct((1, N_HEADS_A), F32)],
        scratch_shapes=[pltpu.VMEM((WINDOW, KV_W), F32), pltpu.VMEM((WINDOW, KV_W), F32)],
        compiler_params=_params(dimension_semantics=("arbitrary", "arbitrary")),
    )(hcat, hcat, hcat, hcat, hcat, bias, sink.reshape(1, N_HEADS_A), do)


def _shift_down(x, halo8, s):
    if s == 0:
        return x
    rolled = pltpu.roll(x, s, axis=0)
    row8 = lax.broadcasted_iota(jnp.int32, halo8.shape, 0)
    top = jnp.where(row8 < s, pltpu.roll(halo8, s, axis=0), rolled[0:8])
    return top if x.shape[0] == 8 else jnp.concatenate([top, rolled[8:]], axis=0)


def _shift_up(x, halo8, s):
    if s == 0:
        return x
    n = x.shape[0]
    rolled = pltpu.roll(x, n - s, axis=0)
    row8 = lax.broadcasted_iota(jnp.int32, halo8.shape, 0)
    bottom = jnp.where(row8 >= 8 - s, pltpu.roll(halo8, 8 - s, axis=0), rolled[n - 8:n])
    return jnp.concatenate([rolled[:n - 8], bottom], axis=0)


def _l2n(x, scale):
    r = lax.rsqrt(jnp.sum(x * x, axis=-1, keepdims=True) + NORM_EPS)
    return x * (r * scale)


def _conv_prep(hcat, conv_w, nbatch, *, name):
    t = hcat.shape[0]
    nt = t // nbatch // ROW_T
    cb = HC_CONV // CONV_CH

    def body(u_ref, halo_ref, w_ref, q_ref, k_ref, v_ref):
        i = pl.program_id(1)
        s = _silu(_conv_rows(u_ref[...], jnp.where(i == 0, 0.0, halo_ref[...]), w_ref))
        for h in range(N_HEADS_B):
            lo, hi = h * KEY_DIM_B, (h + 1) * KEY_DIM_B
            q_ref[:, lo:hi] = _l2n(s[:, lo:hi], KEY_DIM_B ** -0.5)
            k_ref[:, lo:hi] = _l2n(s[:, QK_B + lo:QK_B + hi], 1.0)
        v_ref[...] = s[:, 2 * QK_B:]

    row = lambda w: pl.BlockSpec((ROW_T, w), lambda b, i: (b * nt + i, 0))
    return pl.pallas_call(
        body, name=name, grid=(nbatch, nt),
        in_specs=[pl.BlockSpec((ROW_T, CONV_CH), lambda b, i: (b * nt + i, cb)),
                  pl.BlockSpec((8, CONV_CH), lambda b, i: (jnp.maximum((b * nt + i) * (ROW_T // 8) - 1, 0), cb)),
                  pl.BlockSpec((CONV_K, CONV_CH), lambda b, i: (0, 0))],
        out_specs=[row(QK_B), row(QK_B), row(V_B)],
        out_shape=[jax.ShapeDtypeStruct((t, QK_B), F32)] * 3,
        compiler_params=_params(dimension_semantics=("parallel", "parallel")),
    )(hcat, hcat, conv_w)


def _conv_rows(u, halo8, w_ref):
    c = jnp.zeros_like(u)
    for j in range(CONV_K):
        c = c + w_ref[j:j + 1, :] * _shift_down(u, halo8, CONV_K - 1 - j)
    return c


def _conv_prep_pointwise_bwd(c, dq, dk, dv):
    def l2n_bwd(x, dy, scale):
        r = lax.rsqrt(jnp.sum(x * x, axis=-1, keepdims=True) + NORM_EPS)
        return scale * (r * dy - x * (r * r * r) * jnp.sum(x * dy, axis=-1, keepdims=True))

    sig = _sigmoid(c)
    s = c * sig
    ds = sig * (1.0 + c * (1.0 - sig))
    parts_q, parts_k = [], []
    for h in range(N_HEADS_B):
        lo, hi = h * KEY_DIM_B, (h + 1) * KEY_DIM_B
        parts_q.append(l2n_bwd(s[:, lo:hi], dq[:, lo:hi], KEY_DIM_B ** -0.5) * ds[:, lo:hi])
        parts_k.append(l2n_bwd(s[:, QK_B + lo:QK_B + hi], dk[:, lo:hi], 1.0) * ds[:, QK_B + lo:QK_B + hi])
    return jnp.concatenate(parts_q + parts_k + [dv * ds[:, 2 * QK_B:]], axis=1)


def _conv_bwd(dq, dk, dv, hcat, conv_w, nbatch, *, name):
    t = dq.shape[0]
    nt = t // nbatch // ROW_T
    cb = HC_CONV // CONV_CH
    last_blk = t // 8 - 1

    def body(dq_ref, dk_ref, dv_ref, dqn_ref, dkn_ref, dvn_ref, u_ref, uprev_ref, unext_ref, w_ref, du_ref, dw_ref):
        b = pl.program_id(0)
        i = pl.program_id(1)
        u = u_ref[...]
        uprev = jnp.where(i == 0, 0.0, uprev_ref[...])
        shifted = [_shift_down(u, uprev, CONV_K - 1 - j) for j in range(CONV_K)]
        c = jnp.zeros_like(u)
        for j in range(CONV_K):
            c = c + w_ref[j:j + 1, :] * shifted[j]
        dc_ = _conv_prep_pointwise_bwd(c, dq_ref[...], dk_ref[...], dv_ref[...])
        c_next = _conv_rows(unext_ref[...], u[ROW_T - 8:], w_ref)
        dnext = _conv_prep_pointwise_bwd(c_next, dqn_ref[...], dkn_ref[...], dvn_ref[...])
        dnext = jnp.where(i == nt - 1, 0.0, dnext)
        du = jnp.zeros_like(dc_)
        rows = []
        for j in range(CONV_K):
            du = du + w_ref[j:j + 1, :] * _shift_up(dc_, dnext, CONV_K - 1 - j)
            rows.append(jnp.sum(dc_ * shifted[j], axis=0, keepdims=True))
        du_ref[...] = du.astype(BF16)
        dw_p = jnp.concatenate(rows + [jnp.zeros((8 - CONV_K, CONV_CH), F32)], axis=0)

        @pl.when((b == 0) & (i == 0))
        def _():
            dw_ref[...] = dw_p

        @pl.when((b > 0) | (i > 0))
        def _():
            dw_ref[...] += dw_p

    def nxt(b, i):
        return jnp.minimum((b * nt + i + 1) * (ROW_T // 8), last_blk)

    row = lambda wd: pl.BlockSpec((ROW_T, wd), lambda b, i: (b * nt + i, 0))
    halo = lambda wd: pl.BlockSpec((8, wd), lambda b, i: (nxt(b, i), 0))
    return pl.pallas_call(
        body, name=name, grid=(nbatch, nt),
        in_specs=[row(QK_B), row(QK_B), row(V_B), halo(QK_B), halo(QK_B), halo(V_B),
                  pl.BlockSpec((ROW_T, CONV_CH), lambda b, i: (b * nt + i, cb)),
                  pl.BlockSpec((8, CONV_CH), lambda b, i: (jnp.maximum((b * nt + i) * (ROW_T // 8) - 1, 0), cb)),
                  pl.BlockSpec((8, CONV_CH), lambda b, i: (nxt(b, i), cb)),
                  pl.BlockSpec((CONV_K, CONV_CH), lambda b, i: (0, 0))],
        out_specs=[pl.BlockSpec((ROW_T, CONV_CH), lambda b, i: (b * nt + i, 0)),
                   pl.BlockSpec((8, CONV_CH), lambda b, i: (0, 0))],
        out_shape=[jax.ShapeDtypeStruct((t, CONV_CH), BF16), jax.ShapeDtypeStruct((8, CONV_CH), F32)],
        compiler_params=_params(dimension_semantics=("arbitrary", "arbitrary")),
    )(dq, dk, dv, dq, dk, dv, hcat, hcat, hcat, conv_w)


def _softplus(x):
    return jnp.maximum(x, 0.0) + jnp.log(1.0 + jnp.exp(-jnp.abs(x)))


def _gates(hcat, a_row, dt_row, *, name):
    t = hcat.shape[0]

    def body(bd_ref, a_ref, dt_ref, gb_ref, bb_ref):
        bd = bd_ref[...]
        beta = _sigmoid(bd)
        g = -jnp.exp(a_ref[...]) * _softplus(bd + dt_ref[...])
        for h in range(N_HEADS_B):
            lo, hi = h * VAL_DIM_B, (h + 1) * VAL_DIM_B
            bb_ref[:, lo:hi] = jnp.broadcast_to(beta[:, h:h + 1], (ROW_T, VAL_DIM_B))
            gb_ref[:, lo:hi] = jnp.broadcast_to(g[:, N_HEADS_B + h:N_HEADS_B + h + 1], (ROW_T, VAL_DIM_B))

    vec = pl.BlockSpec((1, 128), lambda i: (0, 0))
    row = pl.BlockSpec((ROW_T, V_B), lambda i: (i, 0))
    return pl.pallas_call(
        body, name=name, grid=(t // ROW_T,),
        in_specs=[pl.BlockSpec((ROW_T, 128), lambda i: (i, HC_BD // 128)), vec, vec],
        out_specs=[row, row], out_shape=[jax.ShapeDtypeStruct((t, V_B), F32)] * 2,
        compiler_params=_params(dimension_semantics=("parallel",)),
    )(hcat, a_row, dt_row)


def _gates_bwd(dgb, dbb, hcat, a_row, dt_row, *, name):
    t = hcat.shape[0]

    def body(dgb_ref, dbb_ref, bd_ref, a_ref, dt_ref, dbd_ref, da_ref, ddt_ref):
        i = pl.program_id(0)
        bd = bd_ref[...]
        beta = _sigmoid(bd)
        ea = jnp.exp(a_ref[...])
        x = bd + dt_ref[...]
        g = -ea * _softplus(x)
        lane = lax.broadcasted_iota(jnp.int32, (ROW_T, 128), 1)
        dbeta = jnp.zeros((ROW_T, 128), F32)
        dg = jnp.zeros((ROW_T, 128), F32)
        for h in range(N_HEADS_B):
            lo, hi = h * VAL_DIM_B, (h + 1) * VAL_DIM_B
            dbeta = dbeta + jnp.where(lane == h, jnp.sum(dbb_ref[:, lo:hi], axis=-1, keepdims=True), 0.0)
            dg = dg + jnp.where(lane == N_HEADS_B + h, jnp.sum(dgb_ref[:, lo:hi], axis=-1, keepdims=True), 0.0)
        ddt_raw = dg * (-ea) * _sigmoid(x)
        dbd_ref[...] = (dbeta * beta * (1.0 - beta) + ddt_raw).astype(BF16)
        da_p = jnp.sum(dg * g, axis=0, keepdims=True)
        ddt_p = jnp.sum(ddt_raw, axis=0, keepdims=True)

        @pl.when(i == 0)
        def _():
            da_ref[...] = da_p
            ddt_ref[...] = ddt_p

        @pl.when(i > 0)
        def _():
            da_ref[...] += da_p
            ddt_ref[...] += ddt_p

    vec = pl.BlockSpec((1, 128), lambda i: (0, 0))
    row = pl.BlockSpec((ROW_T, V_B), lambda i: (i, 0))
    return pl.pallas_call(
        body, name=name, grid=(t // ROW_T,),
        in_specs=[row, row, pl.BlockSpec((ROW_T, 128), lambda i: (i, HC_BD // 128)), vec, vec],
        out_specs=[pl.BlockSpec((ROW_T, 128), lambda i: (i, 0)), vec, vec],
        out_shape=[jax.ShapeDtypeStruct((t, 128), BF16), jax.ShapeDtypeStruct((1, 128), F32),
                   jax.ShapeDtypeStruct((1, 128), F32)],
        compiler_params=_params(dimension_semantics=("arbitrary",)),
    )(dgb, dbb, hcat, a_row, dt_row)


def _group_masks():
    r = lax.broadcasted_iota(jnp.int32, (GROUP_T, GROUP_T), 0)
    c = lax.broadcasted_iota(jnp.int32, (GROUP_T, GROUP_T), 1)
    same = (r // CHUNK) == (c // CHUNK)
    return same, same & (r >= c), same & (r > c)


def _split2(a):
    hi = a.astype(MXU_DTYPE)
    return hi, (a - hi.astype(F32)).astype(MXU_DTYPE)


def _dot3(a2, b2, dims=_NN):
    (ah, al), (bh, bl) = a2, b2
    d = functools.partial(lax.dot_general, dimension_numbers=dims, preferred_element_type=F32)
    return d(ah, bh) + (d(ah, bl) + d(al, bh))


def _inv_unit_lower(lows):
    shape = lows[0].shape
    eye = (lax.broadcasted_iota(jnp.int32, shape, 0) == lax.broadcasted_iota(jnp.int32, shape, 1)).astype(F32)
    p2 = [_split2(-low) for low in lows]
    ts = [eye - low for low in lows]
    for _ in range(int(math.log2(CHUNK)) - 1):
        p2 = [_split2(_dot3(p, p)) for p in p2]
        ts = [t + _dot3(_split2(t), p) for t, p in zip(ts, p2)]
    return ts


@jax.custom_vjp
def _inv_saved(low, t):
    return t


def _inv_saved_fwd(low, t):
    return t, t


def _inv_saved_bwd(t, dt):
    t2 = _split2(t)
    return -_dot3(t2, _split2(_dot3(_split2(dt), t2, _NT)), _TN), jnp.zeros_like(t)


_inv_saved.defvjp(_inv_saved_fwd, _inv_saved_bwd)


def _mask_dot(mask, x, dims):
    m = mask.astype(MXU_DTYPE)
    hi = x.astype(MXU_DTYPE)
    r1 = x - hi.astype(F32)
    mid = r1.astype(MXU_DTYPE)
    lo = (r1 - mid.astype(F32)).astype(MXU_DTYPE)
    d = functools.partial(lax.dot_general, dimension_numbers=dims, preferred_element_type=F32)
    return d(m, hi) + (d(m, mid) + d(m, lo))


@jax.custom_vjp
def _chunk_sums(gb):
    same, causal, _ = _group_masks()
    return _mask_dot(causal, gb, _NN), _mask_dot(same, gb, _NN)


def _chunk_sums_fwd(gb):
    return _chunk_sums(gb), None


def _chunk_sums_bwd(_, cot):
    same, causal, _ = _group_masks()
    return (_mask_dot(causal, cot[0], _TN) + _mask_dot(same, cot[1], _TN),)


_chunk_sums.defvjp(_chunk_sums_fwd, _chunk_sums_bwd)


def _fold_blocks(m):
    return m[:, 0:CHUNK] + m[:, CHUNK:2 * CHUNK] + m[:, 2 * CHUNK:3 * CHUNK] + m[:, 3 * CHUNK:4 * CHUNK]


def _dn_prep_heads(q, k, v, gb, bb, tsaved=None):
    same, causal, strict = _group_masks()
    heads = range(len(q))
    sums = [_chunk_sums(gb[h]) for h in heads]
    gc = [s[0] for s in sums]
    glast = [s[1] for s in sums]
    decay = [jnp.exp(jnp.where(causal, gc[h][:, 0:1] - gc[h].T[0:1, :], NEG_INF)) for h in heads]
    kb = [k[h] * bb[h] for h in heads]
    vb = [v[h] * bb[h] for h in heads]
    lower = [jnp.where(strict, _dot(kb[h], k[h], _NT) * decay[h], 0.0) for h in heads]
    if tsaved is None:
        tinv = _inv_unit_lower(lower)
    else:
        tinv = [_inv_saved(lower[h], jnp.where(same, jnp.concatenate([tsaved[h]] * (GROUP_T // CHUNK), axis=1), 0.0))
                for h in heads]
    egc = [jnp.exp(gc[h]) for h in heads]
    u = [_dot(tinv[h], vb[h]) for h in heads]
    w = [_dot(tinv[h], kb[h] * egc[h]) for h in heads]
    a = [_fold_blocks(jnp.where(causal, _dot(q[h], k[h], _NT) * decay[h], 0.0)) for h in heads]
    k_tail = [k[h] * jnp.exp(glast[h] - gc[h]) for h in heads]
    q_dec = [q[h] * egc[h] for h in heads]
    return u, w, q_dec, k_tail, a, glast, [_fold_blocks(t) for t in tinv]


def _head_slices(ref, width):
    return [ref[:, h * width:(h + 1) * width] for h in range(N_HEADS_B)]


def _store_heads(ref, vals, width):
    for h, val in enumerate(vals):
        ref[:, h * width:(h + 1) * width] = val


def _dn_prep(q, k, v, gb, bb, *, side=None, name):
    t = q.shape[0]

    def body(q_ref, k_ref, v_ref, gb_ref, bb_ref, u_ref, w_ref, qd_ref, kt_ref, a_ref, gl_ref, ti_ref):
        outs = _dn_prep_heads(*[_head_slices(r, KEY_DIM_B) for r in (q_ref, k_ref, v_ref, gb_ref, bb_ref)])
        for ref, vals in zip((u_ref, w_ref, qd_ref, kt_ref, a_ref, gl_ref, ti_ref), outs):
            _store_heads(ref, vals, vals[0].shape[1])

    row = pl.BlockSpec((GROUP_T, V_B), lambda i: (i, 0))
    arow = pl.BlockSpec((GROUP_T, N_HEADS_B * CHUNK), lambda i: (i, 0))
    big = jax.ShapeDtypeStruct((t, V_B), F32)
    small = jax.ShapeDtypeStruct((t, N_HEADS_B * CHUNK), F32)
    outs, side_outs = _call(
        body, name=name, grid=(t // GROUP_T,), in_specs=[row] * 5, out_specs=[row, row, row, row, arow, row, arow],
        out_shape=[big, big, big, big, small, big, small], semantics=("parallel",), args=(q, k, v, gb, bb), side=side)
    return tuple(outs), side_outs


def _dn_prep_bwd(q, k, v, gb, bb, ti, du, dw, dqd, dkt, da, dgl, *, side=None, name):
    t = q.shape[0]

    def body(q_ref, k_ref, v_ref, gb_ref, bb_ref, ti_ref, du_ref, dw_ref, dqd_ref, dkt_ref, da_ref, dgl_ref,
             dq_ref, dk_ref, dv_ref, dgb_ref, dbb_ref):
        for lo in range(0, N_HEADS_B, DN_BWD_HEADS):
            grp = slice(lo, lo + DN_BWD_HEADS)
            tsaved = _head_slices(ti_ref, CHUNK)[grp]
            _, vjp = jax.vjp(lambda *a, ts=tsaved: _dn_prep_heads(*a, tsaved=ts)[:6],
                             *[_head_slices(r, KEY_DIM_B)[grp] for r in (q_ref, k_ref, v_ref, gb_ref, bb_ref)])
            cot = tuple(_head_slices(r, CHUNK if r is da_ref else KEY_DIM_B)[grp]
                        for r in (du_ref, dw_ref, dqd_ref, dkt_ref, da_ref, dgl_ref))
            for ref, vals in zip((dq_ref, dk_ref, dv_ref, dgb_ref, dbb_ref), vjp(cot)):
                for h, val in enumerate(vals):
                    ref[:, (lo + h) * KEY_DIM_B:(lo + h + 1) * KEY_DIM_B] = val

    row = pl.BlockSpec((GROUP_T, V_B), lambda i: (i, 0))
    arow = pl.BlockSpec((GROUP_T, N_HEADS_B * CHUNK), lambda i: (i, 0))
    big = jax.ShapeDtypeStruct((t, V_B), F32)
    outs, side_outs = _call(
        body, name=name, grid=(t // GROUP_T,),
        in_specs=[row] * 5 + [arow] + [row] * 4 + [arow, row], out_specs=[row] * 5, out_shape=[big] * 5,
        semantics=("parallel",), args=(q, k, v, gb, bb, ti, du, dw, dqd, dkt, da, dgl), side=side)
    return tuple(outs), side_outs


def _dn_steps(s, qd, kt, u, w, a, gl):
    heads = range(len(s))
    v_new = [u[h] - _dot(w[h], s[h]) for h in heads]
    qs = [_dot(qd[h], s[h]) for h in heads]
    o = [qs[h] + _dot(a[h], v_new[h]) for h in heads]
    s_new = [s[h] * jnp.exp(gl[h][0:1, :]) + _dot(kt[h], v_new[h], _TN) for h in heads]
    return s_new, o


def _dn_scan(u, w, qd, kt, a, gl, nbatch, *, name):
    t = u.shape[0]
    ng = t // nbatch // GROUP_T
    cpg = GROUP_T // CHUNK

    def body(u_ref, w_ref, qd_ref, kt_ref, a_ref, gl_ref, o_ref, ss_ref, s_ref):
        @pl.when(pl.program_id(1) == 0)
        def _():
            s_ref[...] = jnp.zeros_like(s_ref)

        def chunk(c, carry):
            rows = pl.ds(pl.multiple_of(c * CHUNK, CHUNK), CHUNK)
            heads = range(N_HEADS_B)
            s = [s_ref[h] for h in heads]
            for h in heads:
                ss_ref[c, h] = s[h]
            s_new, o = _dn_steps(s, *[[r[rows, h * wd:(h + 1) * wd] for h in heads] for r, wd in
                                      ((qd_ref, KEY_DIM_B), (kt_ref, KEY_DIM_B), (u_ref, VAL_DIM_B), (w_ref, KEY_DIM_B),
                                       (a_ref, CHUNK), (gl_ref, VAL_DIM_B))])
            for h in heads:
                s_ref[h] = s_new[h]
                o_ref[rows, h * VAL_DIM_B:(h + 1) * VAL_DIM_B] = o[h]
            return carry

        lax.fori_loop(0, cpg, chunk, 0)

    row = pl.BlockSpec((GROUP_T, V_B), lambda b, i: (b * ng + i, 0))
    arow = pl.BlockSpec((GROUP_T, N_HEADS_B * CHUNK), lambda b, i: (b * ng + i, 0))
    return pl.pallas_call(
        body, name=name, grid=(nbatch, ng),
        in_specs=[row, row, row, row, arow, row],
        out_specs=[row, pl.BlockSpec((cpg, N_HEADS_B, KEY_DIM_B, VAL_DIM_B), lambda b, i: (b * ng + i, 0, 0, 0))],
        out_shape=[jax.ShapeDtypeStruct((t, V_B), F32),
                   jax.ShapeDtypeStruct((t // CHUNK, N_HEADS_B, KEY_DIM_B, VAL_DIM_B), F32)],
        scratch_shapes=[pltpu.VMEM((N_HEADS_B, KEY_DIM_B, VAL_DIM_B), F32)],
        compiler_params=_params(dimension_semantics=("parallel", "arbitrary")),
    )(u, w, qd, kt, a, gl)


def _dn_scan_bwd(u, w, qd, kt, a, gl, ss, do, nbatch, *, name):
    t = u.shape[0]
    ng = t // nbatch // GROUP_T
    cpg = GROUP_T // CHUNK

    def body(u_ref, w_ref, qd_ref, kt_ref, a_ref, gl_ref, ss_ref, do_ref,
             du_ref, dw_ref, dqd_ref, dkt_ref, da_ref, dgl_ref, ds_ref):
        @pl.when(pl.program_id(1) == 0)
        def _():
            ds_ref[...] = jnp.zeros_like(ds_ref)

        def chunk(cc, carry):
            c = cpg - 1 - cc
            rows = pl.ds(pl.multiple_of(c * CHUNK, CHUNK), CHUNK)
            heads = range(N_HEADS_B)
            ins = ((qd_ref, KEY_DIM_B), (kt_ref, KEY_DIM_B), (u_ref, VAL_DIM_B), (w_ref, KEY_DIM_B), (a_ref, CHUNK),
                   (gl_ref, VAL_DIM_B))
            _, vjp = jax.vjp(_dn_steps, [ss_ref[c, h] for h in heads],
                             *[[r[rows, h * wd:(h + 1) * wd] for h in heads] for r, wd in ins])
            grads = vjp(([ds_ref[h] for h in heads], [do_ref[rows, h * VAL_DIM_B:(h + 1) * VAL_DIM_B] for h in heads]))
            for h in heads:
                ds_ref[h] = grads[0][h]
            outs = ((dqd_ref, KEY_DIM_B), (dkt_ref, KEY_DIM_B), (du_ref, VAL_DIM_B), (dw_ref, KEY_DIM_B), (da_ref, CHUNK),
                    (dgl_ref, VAL_DIM_B))
            for (r, wd), vals in zip(outs, grads[1:]):
                for h in heads:
                    r[rows, h * wd:(h + 1) * wd] = vals[h]
            return carry

        lax.fori_loop(0, cpg, chunk, 0)

    row = pl.BlockSpec((GROUP_T, V_B), lambda b, j: (b * ng + ng - 1 - j, 0))
    arow = pl.BlockSpec((GROUP_T, N_HEADS_B * CHUNK), lambda b, j: (b * ng + ng - 1 - j, 0))
    big = jax.ShapeDtypeStruct((t, V_B), F32)
    return pl.pallas_call(
        body, name=name, grid=(nbatch, ng),
        in_specs=[row, row, row, row, arow, row,
                  pl.BlockSpec((cpg, N_HEADS_B, KEY_DIM_B, VAL_DIM_B), lambda b, j: (b * ng + ng - 1 - j, 0, 0, 0)), row],
        out_specs=[row, row, row, row, arow, row],
        out_shape=[big, big, big, big, jax.ShapeDtypeStruct((t, N_HEADS_B * CHUNK), F32), big],
        scratch_shapes=[pltpu.VMEM((N_HEADS_B, KEY_DIM_B, VAL_DIM_B), F32)],
        compiler_params=_params(dimension_semantics=("parallel", "arbitrary")),
    )(u, w, qd, kt, a, gl, ss, do)


def _rms_gate(o, hcat, dn_g, *, name):
    t = o.shape[0]

    def body(o_ref, z_ref, g_ref, y_ref):
        for h in range(N_HEADS_B):
            sl = slice(h * VAL_DIM_B, (h + 1) * VAL_DIM_B)
            o_ = o_ref[:, sl]
            r = lax.rsqrt(jnp.mean(o_ * o_, axis=-1, keepdims=True) + NORM_EPS)
            y_ref[:, sl] = (o_ * r * g_ref[...] * _silu(z_ref[:, sl])).astype(BF16)

    row = pl.BlockSpec((ROW_T, V_B), lambda i: (i, 0))
    return pl.pallas_call(
        body, name=name, grid=(t // ROW_T,),
        in_specs=[row, pl.BlockSpec((ROW_T, V_B), lambda i: (i, HC_Z // V_B)), pl.BlockSpec((1, VAL_DIM_B), lambda i: (0, 0))],
        out_specs=row, out_shape=jax.ShapeDtypeStruct((t, V_B), BF16),
        compiler_params=_params(dimension_semantics=("parallel",)),
    )(o, hcat, dn_g.reshape(1, VAL_DIM_B))


def _rms_gate_bwd(dy, o, hcat, dn_g, *, name):
    t = o.shape[0]

    def body(dy_ref, o_ref, z_ref, g_ref, do_ref, dz_ref, dg_ref):
        i = pl.program_id(0)
        g = g_ref[...]
        dg_p = jnp.zeros((1, VAL_DIM_B), F32)
        for h in range(N_HEADS_B):
            sl = slice(h * VAL_DIM_B, (h + 1) * VAL_DIM_B)
            o_ = o_ref[:, sl]
            z_ = z_ref[:, sl]
            dy_ = dy_ref[:, sl]
            r = lax.rsqrt(jnp.mean(o_ * o_, axis=-1, keepdims=True) + NORM_EPS)
            n = o_ * r
            sz = _silu(z_)
            dz_ref[:, sl] = (dy_ * n * g * _dsilu(z_)).astype(BF16)
            dg_p = dg_p + jnp.sum(dy_ * n * sz, axis=0, keepdims=True)
            dn = dy_ * g * sz
            do_ref[:, sl] = r * dn - o_ * (r * r * r) * jnp.mean(o_ * dn, axis=-1, keepdims=True)

        @pl.when(i == 0)
        def _():
            dg_ref[...] = dg_p

        @pl.when(i > 0)
        def _():
            dg_ref[...] += dg_p

    row = pl.BlockSpec((ROW_T, V_B), lambda i: (i, 0))
    vec = pl.BlockSpec((1, VAL_DIM_B), lambda i: (0, 0))
    return pl.pallas_call(
        body, name=name, grid=(t // ROW_T,),
        in_specs=[row, row, pl.BlockSpec((ROW_T, V_B), lambda i: (i, HC_Z // V_B)), vec],
        out_specs=[row, row, vec],
        out_shape=[jax.ShapeDtypeStruct((t, V_B), F32), jax.ShapeDtypeStruct((t, V_B), BF16),
                   jax.ShapeDtypeStruct((1, VAL_DIM_B), F32)],
        compiler_params=_params(dimension_semantics=("arbitrary",)),
    )(dy, o, hcat, dn_g.reshape(1, VAL_DIM_B))


def _merge(ya, yb, hcat, *, name):
    t = ya.shape[0]

    def body(ya_ref, yb_ref, ga_ref, gb_ref, y_ref):
        y_ref[...] = (_sigmoid(ga_ref[...]) * ya_ref[...] + _sigmoid(gb_ref[...]) * yb_ref[...]).astype(BF16)

    row = pl.BlockSpec((ROW_T, D_MODEL), lambda i: (i, 0))
    return pl.pallas_call(
        body, name=name, grid=(t // ROW_T,),
        in_specs=[row, row, pl.BlockSpec((ROW_T, D_MODEL), lambda i: (i, HC_GATE // D_MODEL)),
                  pl.BlockSpec((ROW_T, D_MODEL), lambda i: (i, HC_GATE // D_MODEL + 1))],
        out_specs=row, out_shape=jax.ShapeDtypeStruct((t, D_MODEL), BF16),
        compiler_params=_params(dimension_semantics=("parallel",)),
    )(ya, yb, hcat, hcat)


def _merge_bwd(dmix, ya, yb, hcat, *, name):
    t = ya.shape[0]

    def body(d_ref, ya_ref, yb_ref, ga_ref, gb_ref, dya_ref, dyb_ref, dgate_ref):
        d = d_ref[...]
        sa = _sigmoid(ga_ref[...])
        sb = _sigmoid(gb_ref[...])
        dya_ref[...] = (d * sa).astype(BF16)
        dyb_ref[...] = (d * sb).astype(BF16)
        dgate_ref[:, :D_MODEL] = (d * ya_ref[...] * sa * (1.0 - sa)).astype(BF16)
        dgate_ref[:, D_MODEL:] = (d * yb_ref[...] * sb * (1.0 - sb)).astype(BF16)

    row = pl.BlockSpec((ROW_T, D_MODEL), lambda i: (i, 0))
    return pl.pallas_call(
        body, name=name, grid=(t // ROW_T,),
        in_specs=[row, row, row, pl.BlockSpec((ROW_T, D_MODEL), lambda i: (i, HC_GATE // D_MODEL)),
                  pl.BlockSpec((ROW_T, D_MODEL), lambda i: (i, HC_GATE // D_MODEL + 1))],
        out_specs=[row, row, pl.BlockSpec((ROW_T, 2 * D_MODEL), lambda i: (i, 0))],
        out_shape=[jax.ShapeDtypeStruct((t, D_MODEL), BF16)] * 2 + [jax.ShapeDtypeStruct((t, 2 * D_MODEL), BF16)],
        compiler_params=_params(dimension_semantics=("parallel",)),
    )(dmix, ya, yb, hcat, hcat)


def _loss_head(y, target, *, name):
    t, n = y.shape
    tm = _tile(t, (512, 256, 128))

    def body(y_ref, t_ref, part_ref, dy_ref):
        i = pl.program_id(0)
        e = y_ref[...] - t_ref[...]
        dy_ref[...] = e * (1.0 / n)
        p = jnp.sum((e * e).reshape(tm // 8, 8, n), axis=0) * (0.5 / n)

        @pl.when(i == 0)
        def _():
            part_ref[...] = p

        @pl.when(i > 0)
        def _():
            part_ref[...] += p

    row = pl.BlockSpec((tm, n), lambda i: (i, 0))
    return pl.pallas_call(
        body, name=name, grid=(t // tm,),
        in_specs=[row, row], out_specs=[pl.BlockSpec((8, n), lambda i: (0, 0)), row],
        out_shape=[jax.ShapeDtypeStruct((8, n), F32), jax.ShapeDtypeStruct((t, n), F32)],
        compiler_params=_params(dimension_semantics=("arbitrary",)),
    )(y, target)


def _adamw_math(w, g, m, v):
    nm = ADAM_B1 * m + (1.0 - ADAM_B1) * g
    nv = ADAM_B2 * v + (1.0 - ADAM_B2) * (g * g)
    m_hat = nm / (1.0 - ADAM_B1 ** ADAM_STEP)
    v_hat = nv / (1.0 - ADAM_B2 ** ADAM_STEP)
    return -ADAM_LR * (m_hat / (jnp.sqrt(v_hat) + ADAM_EPS) + ADAM_WD * w), nm, nv


def _adamw(w, g, m, v, *, name):
    shape = w.shape
    cols = shape[-1]
    rows = int(np.prod(shape[:-1]))
    w2, g2, m2, v2 = (a.reshape(rows, cols) for a in (w, g, m, v))
    tr = rows
    if rows * cols > 512 * 1024:
        tr = _tile(rows, tuple(c for c in (512, 256, 128, 64, 32, 16, 8) if c * cols <= 256 * 1024))

    def body(w_ref, g_ref, m_ref, v_ref, d_ref, nm_ref, nv_ref):
        d_ref[...], nm_ref[...], nv_ref[...] = _adamw_math(w_ref[...], g_ref[...], m_ref[...], v_ref[...])

    blk = pl.BlockSpec((tr, cols), lambda i: (i, 0))
    outs = pl.pallas_call(
        body, name=name, grid=(rows // tr,),
        in_specs=[blk] * 4, out_specs=[blk] * 3,
        out_shape=[jax.ShapeDtypeStruct((rows, cols), F32)] * 3,
        compiler_params=_params(dimension_semantics=("parallel",)),
    )(w2, g2, m2, v2)
    return tuple(o.reshape(shape) for o in outs)


def _repack_w_in(w_in):
    d = w_in.shape[0]
    o = 0
    parts = {}
    for nm, wd in (("q", Q_A), ("k", KV_W), ("v", KV_W), ("conv", CONV_CH), ("beta", N_HEADS_B), ("dt", N_HEADS_B),
                   ("z", V_B), ("gate", 2 * D_MODEL)):
        parts[nm] = w_in[:, o:o + wd]
        o += wd
    z = lambda n: jnp.zeros((d, n), w_in.dtype)
    return jnp.concatenate([parts["q"], parts["z"], parts["k"], parts["v"], parts["beta"], parts["dt"],
                            z(128 - 2 * N_HEADS_B), z(HC_CONV - HC_BD - 128), parts["conv"], parts["gate"]], axis=1)


MATRIX_NAMES = ("ffn_w13", "ffn_w2", "w_in", "w_branch_a", "w_branch_b", "w_out")
GATHER_BESIDE_IN_PROJ = ("w_in", "conv_w", "w_branch_a", "w_branch_b", "w_out")


def _dw_in_by_owner(dw):
    sections = ((Q_A, HC_Q), (2 * KV_W, HC_K), (CONV_CH, HC_CONV), (2 * N_HEADS_B, HC_BD), (V_B, HC_Z),
                (2 * D_MODEL, HC_GATE))
    per = N_IN // 4
    owners = []
    for o in range(4):
        lo, hi, start, parts = o * per, (o + 1) * per, 0, []
        for width, off in sections:
            a, b = max(lo, start), min(hi, start + width)
            if a < b:
                parts.append(dw[:, off + a - start:off + b - start])
            start += width
        rows = jnp.concatenate(parts, axis=1)
        owners.append(jnp.stack([rows[:rows.shape[0] // 2], rows[rows.shape[0] // 2:]]))
    return jnp.stack(owners)


def _lane_row(vals):
    return jnp.pad(vals.astype(F32).reshape(1, N_HEADS_B), ((0, 0), (N_HEADS_B, 128 - 2 * N_HEADS_B)))


def _local_step(x, target, rel_bias, layer_wts, side_shards=None, side_assemble=None, reducer=None):
    nbatch, seq, d = x.shape
    t = nbatch * seq
    depth = len(layer_wts)
    layer_wts = list(layer_wts)
    x0 = x.reshape(t, d)
    tgt = target.reshape(t, d)

    onehot = jnp.asarray(_bucket_onehot())
    rel_t = jnp.pad(rel_bias.T, ((0, 0), (0, 128 - NUM_BUCKETS)))
    bias = _mm(rel_t, onehot, tb=True, exact=True, name="pos_bias")
    bias = bias.reshape(N_HEADS_A, WINDOW, 2 * WINDOW)

    pending = [dict(s) if s else {} for s in (side_shards or [None] * depth)] + [{}]

    def fetch(layer, pick):
        names = [n for n in pending[layer] if pick(n)]
        if not names:
            return None, lambda outs: None
        job = _gather_job([pending[layer].pop(n) for n in names])

        def finish(outs):
            for k, val in side_assemble(layer, dict(zip(names, outs))).items():
                if isinstance(val, dict):
                    layer_wts[layer].setdefault(k, {}).update(val)
                else:
                    layer_wts[layer][k] = val
        return job, finish

    def in_mixer(n):
        return n in GATHER_BESIDE_IN_PROJ

    saved = []
    xin, xin_b = x0, x0.astype(BF16)
    for i in range(depth):
        L = {}
        W = layer_wts[i]
        tag = f"_l{i}"
        job, finish = fetch(i, in_mixer)
        a, got = _ffn_up_act(xin_b, W["ffn_w13"][0], side=job, name="ffn_up_act" + tag + "a")
        finish(got)
        r1, x1, x1_b = _mm_res_ln(a, W["ffn_w2"][0], xin, W["ln_g"][0], W["ln_b"][0],
                                  alpha=DN_ALPHA, c=0.5, name="ffn_down_ln" + tag + "a")
        L.update(x0_b=xin_b, a0=a, r1=r1, x1=x1, x1_b=x1_b)
        job, finish = fetch(i + 1, in_mixer)
        hcat, got = _mm_w(x1_b, W["w_in_p"], side=job, name="in_proj" + tag)
        finish(got)
        job, finish = fetch(i, lambda n: True)
        ao, got = _attn_fwd(hcat, bias, W["sinks"], nbatch, side=job, name="swa" + tag)
        finish(got)
        ya = _mm(ao, W["w_branch_a"], name="branch_a" + tag)
        qn, kn, vs = _conv_prep(hcat, W["conv_w"], nbatch, name="conv_prep" + tag)
        a_row = _lane_row(W["a_log"])
        dt_row = _lane_row(W["dt_bias"])
        gb, bb = _gates(hcat, a_row, dt_row, name="gates" + tag)
        job, finish = fetch(i + 1, lambda n: True)
        (u, w, qd, kt, aa, gl, ti), got = _dn_prep(qn, kn, vs, gb, bb, side=job, name="dn_prep" + tag)
        finish(got)
        o, ss = _dn_scan(u, w, qd, kt, aa, gl, nbatch, name="dn_scan" + tag)
        on = _rms_gate(o, hcat, W["dn_norm_g"], name="rms_gate" + tag)
        yb = _mm(on, W["w_branch_b"], name="branch_b" + tag)
        mix = _merge(ya, yb, hcat, name="merge" + tag)
        r2, x2, x2_b = _mm_res_ln(mix, W["w_out"], x1, W["ln_g"][1], W["ln_b"][1],
                                  alpha=DN_ALPHA, c=1.0, name="out_proj_ln" + tag)
        L.update(hcat=hcat, ao=ao, ya=ya, qn=qn, kn=kn, vs=vs, gb=gb, bb=bb, a_row=a_row, dt_row=dt_row,
                 u=u, w=w, qd=qd, kt=kt, aa=aa, gl=gl, ti=ti, o=o, ss=ss, on=on, yb=yb, mix=mix, r2=r2, x2_b=x2_b)
        a, _ = _ffn_up_act(x2_b, W["ffn_w13"][1], name="ffn_up_act" + tag + "b")
        r3, x3, x3_b = _mm_res_ln(a, W["ffn_w2"][1], x2, W["ln_g"][2], W["ln_b"][2],
                                  alpha=DN_ALPHA, c=0.5, name="ffn_down_ln" + tag + "b")
        L.update(a1=a, r3=r3)
        saved.append(L)
        xin, xin_b = x3, x3_b

    part, dy = _loss_head(xin, tgt, name="loss_head")
    loss = jnp.sum(part)

    grads = {k: [None] * depth for k in ("ln_g", "ln_b", "ffn_w13", "ffn_w2", "w_in", "conv_w", "a_log", "dt_bias",
                                          "dn_norm_g", "sinks", "w_branch_a", "w_branch_b", "w_out")}
    dbias_total = None
    for i in reversed(range(depth)):
        L = saved[i]
        W = layer_wts[i]
        tag = f"_l{i}"
        dln_g, dln_b, dw13, dw2 = [None] * 3, [None] * 3, [None] * 2, [None] * 2

        def ffn_bwd(dyo, r, xprev_b, asave, j, sfx):
            dres, df, dln_g[2 * j], dln_b[2 * j] = _ln_bwd(dyo, r, W["ln_g"][2 * j], alpha=DN_ALPHA, c=0.5,
                                                           name="ln_bwd" + tag + sfx)
            job = reducer.job_a() if (reducer is not None and j == 1) else None
            dh, swapped = _ffn_bwd_mid(xprev_b, W["ffn_w13"][j], df, W["ffn_w2"][j], side=job,
                                       name="ffn_bwd_mid" + tag + sfx)
            if job is not None:
                reducer.done_a(swapped)
            dw2[j] = _mm(asave, df, ta=True, name="ffn_w2_grad" + tag + sfx)
            dw13[j] = _mm(xprev_b, dh, ta=True, b_halves=True, name="ffn_w13_grad" + tag + sfx)
            return _mm_wt(dh, W["ffn_w13"][j], dres, a_halves=True, name="ffn_up_bwd" + tag + sfx)

        dx2 = ffn_bwd(dy, L["r3"], L["x2_b"], L["a1"], 1, "b")

        dres2, dymix, dln_g[1], dln_b[1] = _ln_bwd(dx2, L["r2"], W["ln_g"][1], alpha=DN_ALPHA, c=1.0,
                                                   name="ln_bwd" + tag + "m")
        hcat = L["hcat"]
        dmix = _mm(dymix, W["w_out"], tb=True, name="out_proj_bwd" + tag)
        grads["w_out"][i] = _mm(L["mix"], dymix, ta=True, name="w_out_grad" + tag)
        dya, dyb, dgate = _merge_bwd(dmix, L["ya"], L["yb"], hcat, name="merge_bwd" + tag)
        dao = _mm(dya, W["w_branch_a"], tb=True, name="branch_a_bwd" + tag)
        grads["w_branch_a"][i] = _mm(L["ao"], dya, ta=True, name="w_branch_a_grad" + tag)
        don = _mm(dyb, W["w_branch_b"], tb=True, name="branch_b_bwd" + tag)
        grads["w_branch_b"][i] = _mm(L["on"], dyb, ta=True, name="w_branch_b_grad" + tag)
        do, dz, ddn = _rms_gate_bwd(don, L["o"], hcat, W["dn_norm_g"], name="rms_gate_bwd" + tag)
        grads["dn_norm_g"][i] = ddn.reshape(VAL_DIM_B)
        du, dw, dqd, dkt, daa, dgl = _dn_scan_bwd(L["u"], L["w"], L["qd"], L["kt"], L["aa"], L["gl"], L["ss"], do,
                                                  nbatch, name="dn_scan_bwd" + tag)
        job = reducer.job_b() if reducer is not None else None
        (dqn, dkn, dvs, dgb, dbb), exchanged = _dn_prep_bwd(L["qn"], L["kn"], L["vs"], L["gb"], L["bb"], L["ti"], du, dw,
                                                            dqd, dkt, daa, dgl, side=job, name="dn_prep_bwd" + tag)
        if job is not None:
            reducer.done_b(exchanged)
        dconv, dconv_w = _conv_bwd(dqn, dkn, dvs, hcat, W["conv_w"], nbatch, name="conv_bwd" + tag)
        grads["conv_w"][i] = dconv_w[:CONV_K]
        dbd, da_log, ddt = _gates_bwd(dgb, dbb, hcat, L["a_row"], L["dt_row"], name="gates_bwd" + tag)
        grads["a_log"][i] = da_log[0, N_HEADS_B:2 * N_HEADS_B]
        grads["dt_bias"][i] = ddt[0, N_HEADS_B:2 * N_HEADS_B]
        dq, dk, dv, dbias, dsink = _attn_bwd(hcat, bias, W["sinks"], dao, nbatch, name="swa_bwd" + tag)
        grads["sinks"][i] = dsink.reshape(N_HEADS_A)
        dbias_total = dbias if dbias_total is None else dbias_total + dbias
        dhcat = jnp.concatenate([dq, dz, dk, dv, dbd, jnp.zeros((t, HC_CONV - HC_BD - 128), BF16), dconv, dgate], axis=1)
        dw_in_p = _mm(L["x1_b"], dhcat, ta=True, name="w_in_grad" + tag)
        grads["w_in"][i] = _dw_in_by_owner(dw_in_p)
        dx1 = _mm_wt(dhcat, W["w_in_p"], dres2, name="in_proj_bwd" + tag)

        dy = ffn_bwd(dx1, L["r1"], L["x0_b"], L["a0"], 0, "a")
        grads["ln_g"][i] = jnp.concatenate(dln_g, axis=0)
        grads["ln_b"][i] = jnp.concatenate(dln_b, axis=0)
        grads["ffn_w13"][i] = dw13
        grads["ffn_w2"][i] = dw2
        if reducer is not None:
            reducer.layer_done(i, {n: grads[n][i] for n in MATRIX_NAMES})

    out = {k: (v if k in MATRIX_NAMES else jnp.stack(v)) for k, v in grads.items()}
    drel = _mm(dbias_total.reshape(N_HEADS_A, WINDOW * 2 * WINDOW), onehot, name="rel_bias_grad")
    out["rel_bias"] = drel[:, :NUM_BUCKETS].T
    return loss, dy.reshape(nbatch, seq, d), out


N_CHIPS = 4
MESH_ID = pl.DeviceIdType.MESH
HBM_SPEC = pl.BlockSpec(memory_space=pltpu.HBM)


def _place():
    x, y, c = lax.axis_index("x"), lax.axis_index("y"), lax.axis_index("c")
    others = [(1 - x, y), (x, 1 - y), (1 - x, 1 - y)]
    return x, y, c, others


def _chip_index(cx, cy):
    return 2 * cx + cy


def _gather_sems(n):
    return [pltpu.SemaphoreType.DMA((n, 3)), pltpu.SemaphoreType.DMA((n, 3)), pltpu.SemaphoreType.DMA((n,))]


def _gather_copies(ins, outs, send_sems, recv_sems, local_sems):
    x, y, c, others = _place()
    me = _chip_index(x, y)
    copies = []
    for i in range(len(ins)):
        copies.append(pltpu.make_async_copy(ins[i], outs[i].at[me], local_sems.at[i]))
        for k, (ox, oy) in enumerate(others):
            copies.append(pltpu.make_async_remote_copy(src_ref=ins[i], dst_ref=outs[i].at[me], send_sem=send_sems.at[i, k],
                                                       recv_sem=recv_sems.at[i, k], device_id=(ox, oy, c),
                                                       device_id_type=MESH_ID))
    return copies


def _gather_job(tensors):
    return dict(ins=list(tensors), out_shape=[jax.ShapeDtypeStruct((N_CHIPS,) + t.shape, t.dtype) for t in tensors],
                scratch=_gather_sems(len(tensors)), make=_gather_copies)


def _run_job(job, *, name):
    n_in, n_out = len(job["ins"]), len(job["out_shape"])

    def body(*refs):
        copies = job["make"](refs[:n_in], refs[n_in:n_in + n_out], *refs[n_in + n_out:])
        for cp in copies:
            cp.start()
        for cp in copies:
            cp.wait()

    return pl.pallas_call(
        body, name=name, in_specs=[HBM_SPEC] * n_in, out_specs=[HBM_SPEC] * n_out,
        out_shape=list(job["out_shape"]), scratch_shapes=list(job["scratch"]),
    )(*job["ins"])


def _allgather_devices(v, *, name):
    def body(v_ref, o_ref, send_sems, recv_sems, local_sem):
        x, y, c, _ = _place()
        me = 4 * x + 2 * y + c
        loc = pltpu.make_async_copy(v_ref, o_ref.at[me], local_sem)
        loc.start()
        copies = [loc]
        for k in range(1, 8):
            px, py, pc = x ^ (k >> 2), y ^ ((k >> 1) & 1), c ^ (k & 1)
            cp = pltpu.make_async_remote_copy(src_ref=v_ref, dst_ref=o_ref.at[me], send_sem=send_sems.at[k - 1],
                                              recv_sem=recv_sems.at[k - 1], device_id=(px, py, pc), device_id_type=MESH_ID)
            cp.start()
            copies.append(cp)
        for cp in copies:
            cp.wait()

    return pl.pallas_call(
        body, name=name, in_specs=[HBM_SPEC], out_specs=HBM_SPEC,
        out_shape=jax.ShapeDtypeStruct((8,) + v.shape, v.dtype),
        scratch_shapes=[pltpu.SemaphoreType.DMA((7,)), pltpu.SemaphoreType.DMA((7,)), pltpu.SemaphoreType.DMA],
    )(v)


def _sum_slots(g, *, name):
    nb, n, r, l = g.shape
    tr = r // 2 if r % 32 == 0 else r

    def body(g_ref, o_ref):
        acc = g_ref[0].astype(F32)
        for k in range(1, n):
            acc = acc + g_ref[k].astype(F32)
        o_ref[...] = acc

    return pl.pallas_call(
        body, name=name, grid=(nb, r // tr),
        in_specs=[pl.BlockSpec((None, n, tr, l), lambda b, i: (b, 0, i, 0))],
        out_specs=pl.BlockSpec((None, tr, l), lambda b, i: (b, i, 0)),
        out_shape=jax.ShapeDtypeStruct((nb, r, l), F32),
        compiler_params=_params(dimension_semantics=("parallel", "parallel")),
    )(g)


def _half_window(ref, kind, h):
    if kind == "rows":
        return ref.at[:, h]
    r = ref.shape[0] // 2
    return ref.at[pl.ds(pl.multiple_of(h * r, r), r), :]


def _owner_window(ref, kind, o):
    if kind == "rows":
        return ref.at[o]
    cols = ref.shape[1] // N_CHIPS
    return ref.at[:, pl.ds(pl.multiple_of(o * cols, cols), cols)]


def _half_shape(g, kind):
    return (g.shape[0],) + g.shape[2:] if kind == "rows" else (g.shape[0] // 2, g.shape[1])


def _swap_job(gs, kinds):
    n = len(gs)

    def make(ins, outs, send_sems, recv_sems):
        x, y, c, _ = _place()
        return [pltpu.make_async_remote_copy(src_ref=_half_window(ins[i], kinds[i], 1 - c), dst_ref=outs[i],
                                             send_sem=send_sems.at[i], recv_sem=recv_sems.at[i],
                                             device_id=(x, y, 1 - c), device_id_type=MESH_ID) for i in range(n)]

    return dict(ins=list(gs), out_shape=[jax.ShapeDtypeStruct(_half_shape(g, k), g.dtype) for g, k in zip(gs, kinds)],
                scratch=[pltpu.SemaphoreType.DMA((n,)), pltpu.SemaphoreType.DMA((n,))], make=make)


def _pair_sum(g, got, kind, c_idx, *, name):
    hs = _half_shape(g, kind)

    def body(c_ref, g_ref, r_ref, o_ref):
        o_ref[...] = (g_ref[...] + r_ref[...]).astype(BF16)

    if kind == "rows":
        _, _, r, cols = g.shape
        grid = (N_CHIPS,)
        in_specs = [pl.BlockSpec((None, None, r, cols), lambda o, c_ref: (o, c_ref[0], 0, 0)),
                    pl.BlockSpec((None, r, cols), lambda o, c_ref: (o, 0, 0))]
        out_spec = pl.BlockSpec((None, r, cols), lambda o, c_ref: (o, 0, 0))
    else:
        r, cols = hs
        steps = 4
        tr = r // steps
        grid = (steps,)
        in_specs = [pl.BlockSpec((tr, cols), lambda i, c_ref: (c_ref[0] * steps + i, 0)),
                    pl.BlockSpec((tr, cols), lambda i, c_ref: (i, 0))]
        out_spec = pl.BlockSpec((tr, cols), lambda i, c_ref: (i, 0))
    return pl.pallas_call(
        body, name=name,
        grid_spec=pltpu.PrefetchScalarGridSpec(num_scalar_prefetch=1, grid=grid, in_specs=in_specs, out_specs=out_spec),
        out_shape=jax.ShapeDtypeStruct(hs, BF16),
        compiler_params=_params(dimension_semantics=("parallel",)),
    )(c_idx, g, got)


def _exchange_job(ss, kinds):
    n = len(ss)

    def shard_shape(s, kind):
        return s.shape[1:] if kind == "rows" else (s.shape[0], s.shape[1] // N_CHIPS)

    def make(ins, outs, send_sems, recv_sems, local_sems):
        x, y, c, others = _place()
        me = _chip_index(x, y)
        copies = []
        for i in range(n):
            dst = outs[i].at[me]
            copies.append(pltpu.make_async_copy(_owner_window(ins[i], kinds[i], me), dst, local_sems.at[i]))
            for k, (ox, oy) in enumerate(others):
                copies.append(pltpu.make_async_remote_copy(
                    src_ref=_owner_window(ins[i], kinds[i], _chip_index(ox, oy)), dst_ref=dst, send_sem=send_sems.at[i, k],
                    recv_sem=recv_sems.at[i, k], device_id=(ox, oy, c), device_id_type=MESH_ID))
        return copies

    return dict(ins=list(ss), out_shape=[jax.ShapeDtypeStruct((N_CHIPS,) + shard_shape(s, k), s.dtype)
                                         for s, k in zip(ss, kinds)],
                scratch=_gather_sems(n), make=make)


def _send_halves(fs, *, name):
    n = len(fs)

    def body(*refs):
        ins, outs, send_sems, recv_sems = refs[:n], refs[n:2 * n], refs[2 * n], refs[2 * n + 1]
        x, y, c, _ = _place()
        copies = [pltpu.make_async_remote_copy(src_ref=ins[i], dst_ref=outs[i], send_sem=send_sems.at[i],
                                               recv_sem=recv_sems.at[i], device_id=(x, y, 1 - c), device_id_type=MESH_ID)
                  for i in range(n)]
        for cp in copies:
            cp.start()
        for cp in copies:
            cp.wait()

    return pl.pallas_call(
        body, name=name, in_specs=[HBM_SPEC] * n, out_specs=[HBM_SPEC] * n,
        out_shape=[jax.ShapeDtypeStruct(f.shape, f.dtype) for f in fs],
        scratch_shapes=[pltpu.SemaphoreType.DMA((n,)), pltpu.SemaphoreType.DMA((n,))],
    )(*fs)


def _adamw_halves(w, m, v, own, other, c_idx, *, name):
    shape = w.shape
    nl, r, cols = own.shape
    w4, m4, v4 = (a.reshape(nl, 2, r, cols) for a in (w, m, v))
    tr = r if r * cols * 4 <= 3 * 512 * 1024 else _tile(r, tuple(c for c in (256, 128, 64, 32, 16, 8) if c * cols <= 256 * 1024))

    def body(c_ref, w_ref, m_ref, v_ref, own_ref, other_ref, g_ref, d_ref, nm_ref, nv_ref):
        g_ = jnp.where(pl.program_id(1) == c_ref[0], own_ref[...], other_ref[...])
        g_ref[...] = g_
        d_ref[...], nm_ref[...], nv_ref[...] = _adamw_math(w_ref[...], g_, m_ref[...], v_ref[...])

    full = pl.BlockSpec((None, None, tr, cols), lambda l, h, i, c_ref: (l, h, i, 0))
    half = pl.BlockSpec((None, tr, cols), lambda l, h, i, c_ref: (l, i, 0))
    outs = pl.pallas_call(
        body, name=name,
        grid_spec=pltpu.PrefetchScalarGridSpec(num_scalar_prefetch=1, grid=(nl, 2, r // tr),
                                               in_specs=[full, full, full, half, half], out_specs=[full] * 4),
        out_shape=[jax.ShapeDtypeStruct((nl, 2, r, cols), F32)] * 4,
        compiler_params=_params(dimension_semantics=("parallel", "parallel", "parallel")),
    )(c_idx, w4, m4, v4, own, other)
    return tuple(o.reshape(shape) for o in outs)


SHARD_AXIS = {"rel_bias": None, "ln_g": 2, "ln_b": 2, "ffn_w13": 3, "ffn_w2": 2, "w_in": 2, "conv_w": 2, "a_log": None,
              "dt_bias": None, "dn_norm_g": None, "sinks": None, "w_branch_a": 1, "w_branch_b": 1, "w_out": 1}
WEIGHT_NAMES = tuple(SHARD_AXIS)
SMALL_NAMES = tuple(n for n in WEIGHT_NAMES if n not in MATRIX_NAMES)
PACK_LANES = 1024


def _unshard(gathered, axis):
    g = jnp.moveaxis(gathered, 0, axis)
    return g.reshape(g.shape[:axis] + (g.shape[axis] * g.shape[axis + 1],) + g.shape[axis + 2:])


class _GradReducer:
    def __init__(self, c_idx):
        self.c_idx = c_idx
        self.swapping = None
        self.exchanging = None
        self.reduced = {}

    @staticmethod
    def _views(grads):
        out = []
        for n in MATRIX_NAMES:
            for g in (grads[n] if n in ("ffn_w13", "ffn_w2") else [grads[n]]):
                if n == "ffn_w13":
                    out.append((n, g, "cols"))
                elif n == "w_in":
                    out.append((n, g, "rows"))
                else:
                    out.append((n, g.reshape(N_CHIPS, 2, g.shape[0] // (2 * N_CHIPS), g.shape[1]), "rows"))
        return out

    def layer_done(self, layer, grads):
        assert self.swapping is None
        self.swapping = (layer, self._views(grads))

    def job_a(self):
        if self.swapping is None:
            return None
        _, views = self.swapping
        return _swap_job([g for _, g, _ in views], [k for _, _, k in views])

    def done_a(self, got):
        layer, views = self.swapping
        self.swapping = None
        assert self.exchanging is None
        ss = [_pair_sum(g, r, k, self.c_idx, name=f"rs_pair_sum_l{layer}_{i}")
              for i, ((_, g, k), r) in enumerate(zip(views, got))]
        self.exchanging = (layer, views, ss)

    def job_b(self):
        if self.exchanging is None:
            return None
        _, views, ss = self.exchanging
        return _exchange_job(ss, [k for _, _, k in views])

    def done_b(self, ex):
        layer, views, _ = self.exchanging
        self.exchanging = None
        red = {}
        for i, ((n, _, _), e) in enumerate(zip(views, ex)):
            red.setdefault(n, []).append(_sum_slots(e[None], name=f"rs_chip_sum_l{layer}_{i}")[0])
        self.reduced[layer] = red

    def flush(self):
        if self.swapping is not None:
            self.done_a(_run_job(self.job_a(), name="rs_swap_halves_last"))
        if self.exchanging is not None:
            self.done_b(_run_job(self.job_b(), name="rs_exchange_chips_last"))

    def result(self):
        self.flush()
        own = [jnp.stack([f for layer in sorted(self.reduced) for f in self.reduced[layer][n]]) for n in MATRIX_NAMES]
        other = _send_halves(own, name="rs_send_halves")
        return {n: (a, b) for n, a, b in zip(MATRIX_NAMES, own, other)}


def _reduce_small(grads):
    flat = [grads[n].astype(F32).reshape(-1) for n in SMALL_NAMES]
    total = sum(f.shape[0] for f in flat)
    rows = -(-total // (16 * PACK_LANES)) * 16
    vec = jnp.concatenate(flat + [jnp.zeros((rows * PACK_LANES - total,), F32)]).reshape(rows, PACK_LANES)
    s = _sum_slots(_allgather_devices(vec, name="small_allgather")[None], name="small_sum").reshape(-1)
    out, o = {}, 0
    for n, f in zip(SMALL_NAMES, flat):
        out[n] = s[o:o + f.shape[0]].reshape(grads[n].shape)
        o += f.shape[0]
    return out


def kernel(x, rel_bias, ln_g, ln_b, ffn_w13, ffn_w2, w_in, conv_w, a_log, dt_bias, dn_norm_g, sinks, w_branch_a, w_branch_b, w_out, loss_target, m_rel_bias, m_ln_g, m_ln_b, m_ffn_w13, m_ffn_w2, m_w_in, m_conv_w, m_a_log, m_dt_bias, m_dn_norm_g, m_sinks, m_w_branch_a, m_w_branch_b, m_w_out, v_rel_bias, v_ln_g, v_ln_b, v_ffn_w13, v_ffn_w2, v_w_in, v_conv_w, v_a_log, v_dt_bias, v_dn_norm_g, v_sinks, v_w_branch_a, v_w_branch_b, v_w_out):
    w = dict(rel_bias=rel_bias, ln_g=ln_g, ln_b=ln_b, ffn_w13=ffn_w13, ffn_w2=ffn_w2, w_in=w_in, conv_w=conv_w,
             a_log=a_log, dt_bias=dt_bias, dn_norm_g=dn_norm_g, sinks=sinks, w_branch_a=w_branch_a,
             w_branch_b=w_branch_b, w_out=w_out)
    m = dict(rel_bias=m_rel_bias, ln_g=m_ln_g, ln_b=m_ln_b, ffn_w13=m_ffn_w13, ffn_w2=m_ffn_w2, w_in=m_w_in,
             conv_w=m_conv_w, a_log=m_a_log, dt_bias=m_dt_bias, dn_norm_g=m_dn_norm_g, sinks=m_sinks,
             w_branch_a=m_w_branch_a, w_branch_b=m_w_branch_b, w_out=m_w_out)
    v = dict(rel_bias=v_rel_bias, ln_g=v_ln_g, ln_b=v_ln_b, ffn_w13=v_ffn_w13, ffn_w2=v_ffn_w2, w_in=v_w_in,
             conv_w=v_conv_w, a_log=v_a_log, dt_bias=v_dt_bias, dn_norm_g=v_dn_norm_g, sinks=v_sinks,
             w_branch_a=v_w_branch_a, w_branch_b=v_w_branch_b, w_out=v_w_out)

    depth = w_in.shape[0]
    sharded = [n for n in WEIGHT_NAMES if SHARD_AXIS[n] is not None]

    def shards_of(i):
        out = {}
        for n in sharded:
            s = w[n][i].astype(MXU_DTYPE) if n in MATRIX_NAMES else w[n][i]
            if n in ("ffn_w13", "ffn_w2"):
                out[n + "_0"], out[n + "_1"] = s[0], s[1]
            else:
                out[n] = s
        return out

    def assemble(i, gathered):
        lw = {}
        for n, g in gathered.items():
            if n[:-2] in ("ffn_w13", "ffn_w2"):
                lw.setdefault(n[:-2], {})[int(n[-1])] = _unshard(g, SHARD_AXIS[n[:-2]] - 2)
            elif n == "w_in":
                lw["w_in_p"] = _repack_w_in(_unshard(g, SHARD_AXIS[n] - 1))
            else:
                lw[n] = _unshard(g, SHARD_AXIS[n] - 1)
        return lw

    layer_wts = [{n: w[n][i] for n in ("a_log", "dt_bias", "dn_norm_g", "sinks")} for i in range(depth)]
    shards = [shards_of(i) for i in range(depth)]
    first = ("ffn_w13_0", "ffn_w2_0", "ln_g", "ln_b")
    got = _run_job(_gather_job([shards[0].pop(n) for n in first]), name="weights_allgather_first")
    layer_wts[0].update(assemble(0, dict(zip(first, got))))
    c_idx = lax.axis_index("c").astype(jnp.int32).reshape(1)
    reducer = _GradReducer(c_idx)
    loss_part, grad_x, grads = _local_step(x, loss_target, rel_bias, layer_wts, side_shards=shards,
                                           side_assemble=assemble, reducer=reducer)
    loss = lax.psum(loss_part, ("x", "y", "c"))

    halves = reducer.result()
    chip = _chip_index(lax.axis_index("x"), lax.axis_index("y"))
    small = _reduce_small(grads)
    outs = {}
    for n in WEIGHT_NAMES:
        if n in MATRIX_NAMES:
            outs[n] = _adamw_halves(w[n], m[n], v[n], *halves[n], c_idx, name="adamw_" + n)
        else:
            axis = SHARD_AXIS[n]
            g = small[n]
            if axis is not None:
                g = lax.dynamic_slice_in_dim(g, chip * w[n].shape[axis], w[n].shape[axis], axis)
            outs[n] = (g,) + _adamw(w[n], g, m[n], v[n], name="adamw_" + n)
    return (loss, grad_x, *[outs[n][0] for n in WEIGHT_NAMES], *[outs[n][1] for n in WEIGHT_NAMES],
            *[outs[n][2] for n in WEIGHT_NAMES], *[outs[n][3] for n in WEIGHT_NAMES])
```

```python
import functools
import math

import numpy as np
import jax
import jax.numpy as jnp
from jax import lax
from jax.experimental import pallas as pl
from jax.experimental.pallas import tpu as pltpu

F32 = jnp.float32
BF16 = jnp.bfloat16
MXU_DTYPE = BF16
HIGHEST = lax.Precision.HIGHEST

D_MODEL = 1024
N_HEADS_A = 16
N_KV_A = 4
HEAD_DIM_A = 64
GROUP_A = N_HEADS_A // N_KV_A
WINDOW = 128
N_HEADS_B = 8
KEY_DIM_B = 128
VAL_DIM_B = 128
CONV_K = 4
CHUNK = 64
D_FF = 2816
NUM_BUCKETS = 32
MAX_DISTANCE = 128
DEPTH = 4
DN_ALPHA = (2 * DEPTH) ** 0.25
LN_EPS = 1e-5
NORM_EPS = 1e-6
NEG_INF = -1e30

Q_A = N_HEADS_A * HEAD_DIM_A
KV_W = N_KV_A * HEAD_DIM_A
QK_B = N_HEADS_B * KEY_DIM_B
V_B = N_HEADS_B * VAL_DIM_B
CONV_CH = 2 * QK_B + V_B
N_IN = Q_A + 2 * KV_W + CONV_CH + 2 * N_HEADS_B + V_B + 2 * D_MODEL

ADAM_LR = 0.001
ADAM_B1 = 0.9
ADAM_B2 = 0.999
ADAM_EPS = 1e-08
ADAM_WD = 0.01
ADAM_STEP = 10

HC_W = 8192
HC_Q = 0
HC_Z = 1024
HC_K = 2048
HC_V = 2304
HC_BD = 2560
HC_CONV = 3072
HC_GATE = 6144

GROUP_T = 256
DN_BWD_HEADS = 4
ROW_T = 256
VMEM_LIMIT_BYTES = 48 * 1024 * 1024


def _params(vmem=VMEM_LIMIT_BYTES, **kw):
    return pltpu.CompilerParams(vmem_limit_bytes=vmem, **kw)


def _tile(n, cands):
    for c in cands:
        if n % c == 0:
            return c
    return n


def _dot(a, b, dims=(((1,), (0,)), ((), ())), exact=False):
    if exact:
        return lax.dot_general(a.astype(F32), b.astype(F32), dims, precision=HIGHEST, preferred_element_type=F32)
    return lax.dot_general(a.astype(MXU_DTYPE), b.astype(MXU_DTYPE), dims, preferred_element_type=F32)


_NN = (((1,), (0,)), ((), ()))
_NT = (((1,), (1,)), ((), ()))
_TN = (((0,), (0,)), ((), ()))


def _sigmoid(x):
    return 1.0 / (1.0 + jnp.exp(-x))


def _silu(x):
    return x * _sigmoid(x)


def _dsilu(x):
    s = _sigmoid(x)
    return s * (1.0 + x * (1.0 - s))


def _call(body, *, name, grid, in_specs, out_specs, out_shape, scratch_shapes=(), semantics, args, side=None,
          vmem=VMEM_LIMIT_BYTES):
    in_specs, out_specs, out_shape = list(in_specs), list(out_specs), list(out_shape)
    if side is None:
        outs = pl.pallas_call(body, name=name, grid=grid, in_specs=in_specs, out_specs=out_specs, out_shape=out_shape,
                              scratch_shapes=list(scratch_shapes),
                              compiler_params=_params(vmem=vmem, dimension_semantics=semantics))(*args)
        return outs, None
    n_in, n_out, n_scr = len(in_specs), len(out_specs), len(scratch_shapes)
    s_in, s_out = len(side["ins"]), len(side["out_shape"])

    def hosted(*refs):
        main_in, side_in = refs[:n_in], refs[n_in:n_in + s_in]
        o0 = n_in + s_in
        main_out, side_out = refs[o0:o0 + n_out], refs[o0 + n_out:o0 + n_out + s_out]
        rest = refs[o0 + n_out + s_out:]
        copies = side["make"](side_in, side_out, *rest[n_scr:])
        ids = [pl.program_id(d) for d in range(len(grid))]
        first = functools.reduce(jnp.logical_and, [i == 0 for i in ids])
        last = functools.reduce(jnp.logical_and, [i == g - 1 for i, g in zip(ids, grid)])

        @pl.when(first)
        def _():
            for cp in copies:
                cp.start()

        body(*main_in, *main_out, *rest[:n_scr])

        @pl.when(last)
        def _():
            for cp in copies:
                cp.wait()

    outs = pl.pallas_call(
        hosted, name=name, grid=grid,
        in_specs=in_specs + [HBM_SPEC] * s_in, out_specs=out_specs + [HBM_SPEC] * s_out,
        out_shape=out_shape + list(side["out_shape"]),
        scratch_shapes=list(scratch_shapes) + list(side["scratch"]),
        compiler_params=_params(vmem=vmem, dimension_semantics=("arbitrary",) * len(grid)),
    )(*args, *side["ins"])
    return list(outs[:n_out]), list(outs[n_out:])


def _mm(a, b, *, ta=False, tb=False, a_halves=False, b_halves=False, out_dtype=F32, add=None, exact=False, side=None,
        name):
    if a_halves:
        m, kdim = a.shape[1], 2 * a.shape[2]
    else:
        (kdim, m) = a.shape if ta else a.shape[::-1]
    if b_halves:
        kb, n = b.shape[1], 2 * b.shape[2]
    else:
        (n, kb) = b.shape if tb else b.shape[::-1]
    assert kdim == kb, (a.shape, b.shape, ta, tb)
    if ta and not tb and not exact and add is None and side is None and kdim <= TN_WHOLE_K:
        return _mm_tn(a, b, b_halves=b_halves, out_dtype=out_dtype, name=name)
    tn = _tile(n, (1024, 1408, 512, 256, 128))
    tk = _tile(kdim, (1024, 1408, 512, 256, 128))
    nk = kdim // tk
    tm = _tile(m, (1024, 1408, 512, 256, 128) if nk > 1 else (512, 256, 128))
    nj = n // tn
    dims = (((0 if ta else 1,), (1 if tb else 0,)), ((), ()))
    has_add = add is not None

    def body(*refs):
        if has_add:
            a_ref, b_ref, add_ref, o_ref = refs[:4]
        else:
            a_ref, b_ref, o_ref = refs[:3]
            add_ref = None
        part = _dot(a_ref[...], b_ref[...], dims, exact)

        def finish(acc):
            if has_add:
                acc = acc + add_ref[...].astype(F32)
            o_ref[...] = acc.astype(out_dtype)

        if nk == 1:
            finish(part)
        else:
            acc_ref = refs[-1]
            k = pl.program_id(2)

            @pl.when(k == 0)
            def _():
                acc_ref[...] = part

            @pl.when(k > 0)
            def _():
                acc_ref[...] += part

            @pl.when(k == nk - 1)
            def _():
                finish(acc_ref[...])

    if a_halves:
        assert not ta and nk % 2 == 0
        a_spec = pl.BlockSpec((None, tm, tk), lambda i, j, k: (k // (nk // 2), i, k % (nk // 2)))
    elif ta:
        a_spec = pl.BlockSpec((tk, tm), lambda i, j, k: (k, i))
    else:
        a_spec = pl.BlockSpec((tm, tk), lambda i, j, k: (i, k))
    if b_halves:
        assert not tb and nj % 2 == 0
        b_spec = pl.BlockSpec((None, tk, tn), lambda i, j, k: (j // (nj // 2), k, j % (nj // 2)))
    elif tb:
        b_spec = pl.BlockSpec((tn, tk), lambda i, j, k: (j, k))
    else:
        b_spec = pl.BlockSpec((tk, tn), lambda i, j, k: (k, j))
    o_spec = pl.BlockSpec((tm, tn), lambda i, j, k: (i, j))
    in_specs = [a_spec, b_spec] + ([o_spec] if has_add else [])
    args = (a, b) + ((add,) if has_add else ())
    outs, side_outs = _call(
        body, name=name, grid=(m // tm, nj, nk), in_specs=in_specs, out_specs=[o_spec],
        out_shape=[jax.ShapeDtypeStruct((m, n), out_dtype)],
        scratch_shapes=[pltpu.VMEM((tm, tn), F32)] if nk > 1 else [],
        semantics=("parallel", "parallel", "arbitrary"), args=args, side=side)
    return outs[0] if side is None else (outs[0], side_outs)


TN_WHOLE_K = 8192
VMEM_LIMIT_BIG_BYTES = 56 * 1024 * 1024


def _mm_tn(a, b, *, b_halves=False, out_dtype=F32, name):
    kdim, m = a.shape
    n = 2 * b.shape[2] if b_halves else b.shape[1]
    if m <= 1024:
        tm, tn = m, _tile(n // 2 if b_halves else n, (256, 128))
    else:
        tm, tn = _tile(m, (256, 128)), _tile(n, (1024, 512, 256, 128))
    nj = n // tn

    def body(a_ref, b_ref, o_ref):
        o_ref[...] = _dot(a_ref[...], b_ref[...], _TN).astype(out_dtype)

    if b_halves:
        b_spec = pl.BlockSpec((None, kdim, tn), lambda i, j: (j // (nj // 2), 0, j % (nj // 2)))
    else:
        b_spec = pl.BlockSpec((kdim, tn), lambda i, j: (0, j))
    return pl.pallas_call(
        body, name=name, grid=(m // tm, nj),
        in_specs=[pl.BlockSpec((kdim, tm), lambda i, j: (0, i)), b_spec],
        out_specs=pl.BlockSpec((tm, tn), lambda i, j: (i, j)),
        out_shape=jax.ShapeDtypeStruct((m, n), out_dtype),
        compiler_params=pltpu.CompilerParams(vmem_limit_bytes=VMEM_LIMIT_BIG_BYTES,
                                             dimension_semantics=("parallel", "parallel")),
    )(a, b)


def _mm_w(a, w, *, side=None, name):
    m, kdim = a.shape
    n = w.shape[1]
    tm = _tile(m, (256, 128))

    def body(a_ref, w_ref, o_ref):
        o_ref[...] = _dot(a_ref[...], w_ref[...])

    outs, side_outs = _call(
        body, name=name, grid=(m // tm,),
        in_specs=[pl.BlockSpec((tm, kdim), lambda i: (i, 0)),
                  pl.BlockSpec((kdim, n), lambda i: (0, 0), pipeline_mode=pl.Buffered(1))],
        out_specs=[pl.BlockSpec((tm, n), lambda i: (i, 0))], out_shape=[jax.ShapeDtypeStruct((m, n), F32)],
        semantics=("parallel",), args=(a, w), side=side, vmem=VMEM_LIMIT_BIG_BYTES)
    return outs[0], side_outs


def _mm_wt(a, w, add, *, a_halves=False, name):
    n, kdim = w.shape
    m = a.shape[1] if a_halves else a.shape[0]
    tm = _tile(m, (512, 256, 128))
    half = kdim // 2

    def body(a_ref, w_ref, add_ref, o_ref):
        if a_halves:
            acc = _dot(a_ref[0], w_ref[:, :half], _NT) + _dot(a_ref[1], w_ref[:, half:], _NT)
        else:
            acc = _dot(a_ref[...], w_ref[...], _NT)
        o_ref[...] = acc + add_ref[...]

    a_spec = pl.BlockSpec((2, tm, half), lambda i: (0, i, 0)) if a_halves else pl.BlockSpec((tm, kdim), lambda i: (i, 0))
    row = pl.BlockSpec((tm, n), lambda i: (i, 0))
    return pl.pallas_call(
        body, name=name, grid=(m // tm,),
        in_specs=[a_spec, pl.BlockSpec((n, kdim), lambda i: (0, 0), pipeline_mode=pl.Buffered(1)), row],
        out_specs=row, out_shape=jax.ShapeDtypeStruct((m, n), F32),
        compiler_params=pltpu.CompilerParams(vmem_limit_bytes=VMEM_LIMIT_BIG_BYTES, dimension_semantics=("parallel",)),
    )(a, w, add)


def _layernorm_rows(r, g, b):
    mu = jnp.mean(r, axis=-1, keepdims=True)
    xc = r - mu
    var = jnp.mean(xc * xc, axis=-1, keepdims=True)
    return xc * lax.rsqrt(var + LN_EPS) * g + b


def _mm_res_ln(a, w, resid, g, b, *, alpha, c, name):
    m, kdim = a.shape
    n = w.shape[1]
    tm = _tile(m, (512, 256, 128))

    def body(a_ref, w_ref, x_ref, g_ref, b_ref, r_ref, y_ref, yb_ref):
        f = _dot(a_ref[...], w_ref[...])
        r = alpha * x_ref[...] + c * f
        r_ref[...] = r
        y = _layernorm_rows(r, g_ref[...], b_ref[...])
        y_ref[...] = y
        yb_ref[...] = y.astype(BF16)

    row = pl.BlockSpec((tm, n), lambda i: (i, 0))
    vec = pl.BlockSpec((1, n), lambda i: (0, 0))
    return pl.pallas_call(
        body, name=name, grid=(m // tm,),
        in_specs=[pl.BlockSpec((tm, kdim), lambda i: (i, 0)), pl.BlockSpec((kdim, n), lambda i: (0, 0)), row, vec, vec],
        out_specs=[row, row, row],
        out_shape=[jax.ShapeDtypeStruct((m, n), F32)] * 2 + [jax.ShapeDtypeStruct((m, n), BF16)],
        compiler_params=_params(dimension_semantics=("parallel",)),
    )(a, w, resid, g.reshape(1, n), b.reshape(1, n))


def _ln_bwd(dy, r, g, *, alpha, c, name):
    m, n = dy.shape
    tm = _tile(m, (512, 256, 128))

    def body(dy_ref, r_ref, g_ref, dres_ref, dbr_ref, dg_ref, db_ref):
        i = pl.program_id(0)
        dy_ = dy_ref[...]
        r_ = r_ref[...]
        mu = jnp.mean(r_, axis=-1, keepdims=True)
        xc = r_ - mu
        var = jnp.mean(xc * xc, axis=-1, keepdims=True)
        rstd = lax.rsqrt(var + LN_EPS)
        xh = xc * rstd
        dxh = dy_ * g_ref[...]
        dr = rstd * (dxh - jnp.mean(dxh, axis=-1, keepdims=True) - xh * jnp.mean(dxh * xh, axis=-1, keepdims=True))
        dres_ref[...] = alpha * dr
        dbr_ref[...] = (c * dr).astype(BF16)
        dg_p = jnp.sum(dy_ * xh, axis=0, keepdims=True)
        db_p = jnp.sum(dy_, axis=0, keepdims=True)

        @pl.when(i == 0)
        def _():
            dg_ref[...] = dg_p
            db_ref[...] = db_p

        @pl.when(i > 0)
        def _():
            dg_ref[...] += dg_p
            db_ref[...] += db_p

    row = pl.BlockSpec((tm, n), lambda i: (i, 0))
    vec = pl.BlockSpec((1, n), lambda i: (0, 0))
    return pl.pallas_call(
        body, name=name, grid=(m // tm,),
        in_specs=[row, row, vec], out_specs=[row, row, vec, vec],
        out_shape=[jax.ShapeDtypeStruct((m, n), F32), jax.ShapeDtypeStruct((m, n), BF16),
                   jax.ShapeDtypeStruct((1, n), F32), jax.ShapeDtypeStruct((1, n), F32)],
        compiler_params=_params(dimension_semantics=("arbitrary",)),
    )(dy, r, g.reshape(1, n))


FFN_TN = D_FF // 2


def _ffn_up_act(x, w13, *, side=None, name):
    m, d = x.shape
    tm = _tile(m, (512, 256, 128))
    nj = D_FF // FFN_TN

    def body(x_ref, g_ref, u_ref, o_ref):
        x_ = x_ref[...]
        o_ref[...] = (_silu(_dot(x_, g_ref[...])) * _dot(x_, u_ref[...])).astype(BF16)

    outs, side_outs = _call(
        body, name=name, grid=(nj, m // tm),
        in_specs=[pl.BlockSpec((tm, d), lambda j, i: (i, 0)), pl.BlockSpec((d, FFN_TN), lambda j, i: (0, j)),
                  pl.BlockSpec((d, FFN_TN), lambda j, i: (0, j + nj))],
        out_specs=[pl.BlockSpec((tm, FFN_TN), lambda j, i: (i, j))],
        out_shape=[jax.ShapeDtypeStruct((m, D_FF), BF16)],
        semantics=("parallel", "parallel"), args=(x, w13, w13), side=side)
    return outs[0], side_outs


def _ffn_bwd_mid(x, w13, df, w2, *, side=None, name):
    m, d = x.shape
    tm = _tile(m, (512, 256, 128))
    nj = D_FF // FFN_TN

    def body(x_ref, g_ref, u_ref, df_ref, w2_ref, o_ref):
        x_ = x_ref[...]
        g = _dot(x_, g_ref[...])
        u = _dot(x_, u_ref[...])
        da = _dot(df_ref[...], w2_ref[...], _NT)
        o_ref[0] = (da * u * _dsilu(g)).astype(BF16)
        o_ref[1] = (da * _silu(g)).astype(BF16)

    outs, side_outs = _call(
        body, name=name, grid=(nj, m // tm),
        in_specs=[pl.BlockSpec((tm, d), lambda j, i: (i, 0)), pl.BlockSpec((d, FFN_TN), lambda j, i: (0, j)),
                  pl.BlockSpec((d, FFN_TN), lambda j, i: (0, j + nj)), pl.BlockSpec((tm, d), lambda j, i: (i, 0)),
                  pl.BlockSpec((FFN_TN, d), lambda j, i: (j, 0))],
        out_specs=[pl.BlockSpec((2, tm, FFN_TN), lambda j, i: (0, i, j))],
        out_shape=[jax.ShapeDtypeStruct((2, m, D_FF), BF16)],
        semantics=("parallel", "parallel"), args=(x, w13, w13, df, w2), side=side)
    return outs[0], side_outs


def _t5_bucket_table():
    r = np.arange(WINDOW)[:, None]
    j = np.arange(2 * WINDOW)[None, :]
    rel = r + WINDOW - j
    n = np.maximum(rel, 0)
    max_exact = NUM_BUCKETS // 2
    nf = np.maximum(n, 1).astype(np.float32)
    large = max_exact + (np.log(nf / np.float32(max_exact)) / np.float32(math.log(MAX_DISTANCE / max_exact))
                         * np.float32(NUM_BUCKETS - max_exact)).astype(np.int32)
    large = np.minimum(large, NUM_BUCKETS - 1)
    bucket = np.where(n < max_exact, n, large)
    in_band = (rel >= 0) & (rel < WINDOW)
    return bucket.astype(np.int32), in_band


def _bucket_onehot():
    bucket, _ = _t5_bucket_table()
    oh = np.zeros((WINDOW * 2 * WINDOW, 128), np.float32)
    oh[np.arange(oh.shape[0]), bucket.reshape(-1)] = 1.0
    return oh


def _stack_heads(x, g):
    hd = HEAD_DIM_A
    return jnp.concatenate([x[:, (GROUP_A * g + h) * hd:(GROUP_A * g + h + 1) * hd] for h in range(GROUP_A)], axis=0)


def _unstack_heads(x):
    return jnp.concatenate([x[h * WINDOW:(h + 1) * WINDOW] for h in range(GROUP_A)], axis=1)


def _attn_probs(q_ref, kp_ref, ko_ref, vp_ref, vo_ref, bias_ref, sink_ref, first_block):
    hd = HEAD_DIM_A
    groups = range(N_KV_A)
    q = q_ref[...]
    qs = [_stack_heads(q, g) * (hd ** -0.5) for g in groups]
    k2 = [jnp.concatenate([kp_ref[:, g * hd:(g + 1) * hd], ko_ref[:, g * hd:(g + 1) * hd]], axis=0) for g in groups]
    v2 = [jnp.concatenate([vp_ref[:, g * hd:(g + 1) * hd], vo_ref[:, g * hd:(g + 1) * hd]], axis=0) for g in groups]
    rr = lax.broadcasted_iota(jnp.int32, (GROUP_A * WINDOW, 2 * WINDOW), 0) % WINDOW
    jj = lax.broadcasted_iota(jnp.int32, (GROUP_A * WINDOW, 2 * WINDOW), 1)
    rel = rr + WINDOW - jj
    valid = (rel >= 0) & (rel < WINDOW) & (jnp.logical_not(first_block) | (jj >= WINDOW))
    s = [_dot(qs[g], k2[g], _NT) for g in groups]
    s = [jnp.where(valid, s[g] + bias_ref[GROUP_A * g:GROUP_A * (g + 1)].reshape(GROUP_A * WINDOW, 2 * WINDOW), NEG_INF)
         for g in groups]
    sk = [jnp.concatenate([jnp.broadcast_to(sink_ref[0:1, GROUP_A * g + h:GROUP_A * g + h + 1], (WINDOW, 1))
                           for h in range(GROUP_A)], axis=0) for g in groups]
    mx = [jnp.maximum(jnp.max(s[g], axis=-1, keepdims=True), sk[g]) for g in groups]
    p = [jnp.exp(s[g] - mx[g]) for g in groups]
    ps = [jnp.exp(sk[g] - mx[g]) for g in groups]
    den = [jnp.sum(p[g], axis=-1, keepdims=True) + ps[g] for g in groups]
    return qs, k2, v2, [p[g] / den[g] for g in groups], [ps[g] / den[g] for g in groups]


def _attn_specs(nb):
    def prev(b, i):
        return (b * nb + jnp.maximum(i - 1, 0))

    q_spec = pl.BlockSpec((WINDOW, Q_A), lambda b, i: (b * nb + i, HC_Q // Q_A))
    kp_spec = pl.BlockSpec((WINDOW, KV_W), lambda b, i: (prev(b, i), HC_K // KV_W))
    ko_spec = pl.BlockSpec((WINDOW, KV_W), lambda b, i: (b * nb + i, HC_K // KV_W))
    vp_spec = pl.BlockSpec((WINDOW, KV_W), lambda b, i: (prev(b, i), HC_V // KV_W))
    vo_spec = pl.BlockSpec((WINDOW, KV_W), lambda b, i: (b * nb + i, HC_V // KV_W))
    bias_spec = pl.BlockSpec((N_HEADS_A, WINDOW, 2 * WINDOW), lambda b, i: (0, 0, 0))
    sink_spec = pl.BlockSpec((1, N_HEADS_A), lambda b, i: (0, 0))
    return [q_spec, kp_spec, ko_spec, vp_spec, vo_spec, bias_spec, sink_spec]


def _attn_fwd(hcat, bias, sink, nbatch, *, side=None, name):
    t = hcat.shape[0]
    nb = t // nbatch // WINDOW

    def body(q_ref, kp_ref, ko_ref, vp_ref, vo_ref, bias_ref, sink_ref, o_ref):
        first = pl.program_id(1) == 0
        _, _, v2, p, _ = _attn_probs(q_ref, kp_ref, ko_ref, vp_ref, vo_ref, bias_ref, sink_ref, first)
        o = [_dot(p[g], v2[g]) for g in range(N_KV_A)]
        o_ref[...] = jnp.concatenate([_unstack_heads(og) for og in o], axis=1).astype(BF16)

    outs, side_outs = _call(
        body, name=name, grid=(nbatch, nb), in_specs=_attn_specs(nb),
        out_specs=[pl.BlockSpec((WINDOW, Q_A), lambda b, i: (b * nb + i, 0))],
        out_shape=[jax.ShapeDtypeStruct((t, Q_A), BF16)], semantics=("parallel", "arbitrary"),
        args=(hcat, hcat, hcat, hcat, hcat, bias, sink.reshape(1, N_HEADS_A)), side=side)
    return outs[0], side_outs


def _attn_bwd(hcat, bias, sink, do, nbatch, *, name):
    t = hcat.shape[0]
    nb = t // nbatch // WINDOW
    hd = HEAD_DIM_A

    def body(q_ref, kp_ref, ko_ref, vp_ref, vo_ref, bias_ref, sink_ref, do_ref,
             dq_ref, dk_ref, dv_ref, dbias_ref, dsink_ref, ck_ref, cv_ref):
        b = pl.program_id(0)
        j = pl.program_id(1)
        first = j == nb - 1

        @pl.when((b == 0) & (j == 0))
        def _():
            dbias_ref[...] = jnp.zeros_like(dbias_ref)
            dsink_ref[...] = jnp.zeros_like(dsink_ref)

        @pl.when(j == 0)
        def _():
            ck_ref[...] = jnp.zeros_like(ck_ref)
            cv_ref[...] = jnp.zeros_like(cv_ref)

        do_ = do_ref[...]
        groups = range(N_KV_A)
        lane = lax.broadcasted_iota(jnp.int32, (1, N_HEADS_A), 1)
        qs, k2, v2, p, ps = _attn_probs(q_ref, kp_ref, ko_ref, vp_ref, vo_ref, bias_ref, sink_ref, first)
        dos = [_stack_heads(do_, g) for g in groups]
        dv2 = [_dot(p[g], dos[g], _TN) for g in groups]
        dp = [_dot(dos[g], v2[g], _NT) for g in groups]
        delta = [jnp.sum(p[g] * dp[g], axis=-1, keepdims=True) for g in groups]
        ds = [p[g] * (dp[g] - delta[g]) for g in groups]
        dqs = [_dot(ds[g], k2[g]) * (hd ** -0.5) for g in groups]
        dk2 = [_dot(ds[g], qs[g], _TN) for g in groups]
        dsink = jnp.zeros((1, N_HEADS_A), F32)
        for g in groups:
            dsk = -(ps[g] * delta[g])
            for h in range(GROUP_A):
                tot = jnp.sum(dsk[h * WINDOW:(h + 1) * WINDOW], axis=0, keepdims=True)
                dsink = dsink + jnp.where(lane == GROUP_A * g + h, tot, 0.0)
            dbias_ref[GROUP_A * g:GROUP_A * (g + 1)] += ds[g].reshape(GROUP_A, WINDOW, 2 * WINDOW)
        dq_ref[...] = jnp.concatenate([_unstack_heads(d) for d in dqs], axis=1).astype(BF16)
        dk_ref[...] = (jnp.concatenate([d[WINDOW:] for d in dk2], axis=1) + ck_ref[...]).astype(BF16)
        dv_ref[...] = (jnp.concatenate([d[WINDOW:] for d in dv2], axis=1) + cv_ref[...]).astype(BF16)
        ck_ref[...] = jnp.concatenate([d[:WINDOW] for d in dk2], axis=1)
        cv_ref[...] = jnp.concatenate([d[:WINDOW] for d in dv2], axis=1)
        dsink_ref[...] += dsink

    def rev(spec):
        return pl.BlockSpec(spec.block_shape, lambda b, j, f=spec.index_map: f(b, nb - 1 - j))

    in_specs = [rev(s) for s in _attn_specs(nb)[:5]] + _attn_specs(nb)[5:]
    in_specs.append(pl.BlockSpec((WINDOW, Q_A), lambda b, j: (b * nb + nb - 1 - j, 0)))
    return pl.pallas_call(
        body, name=name, grid=(nbatch, nb),
        in_specs=in_specs,
        out_specs=[pl.BlockSpec((WINDOW, Q_A), lambda b, j: (b * nb + nb - 1 - j, 0)),
                   pl.BlockSpec((WINDOW, KV_W), lambda b, j: (b * nb + nb - 1 - j, 0)),
                   pl.BlockSpec((WINDOW, KV_W), lambda b, j: (b * nb + nb - 1 - j, 0)),
                   pl.BlockSpec((N_HEADS_A, WINDOW, 2 * WINDOW), lambda b, j: (0, 0, 0)),
                   pl.BlockSpec((1, N_HEADS_A), lambda b, j: (0, 0))],
        out_shape=[jax.ShapeDtypeStruct((t, Q_A), BF16), jax.ShapeDtypeStruct((t, KV_W), BF16),
                   jax.ShapeDtypeStruct((t, KV_W), BF16),
                   jax.ShapeDtypeStruct((N_HEADS_A, WINDOW, 2 * WINDOW), F32),
                   jax.ShapeDtypeStruct((1, N_HEADS_A), F32)],
        scratch_shapes=[pltpu.VMEM((WINDOW, KV_W), F32), pltpu.VMEM((WINDOW, KV_W), F32)],
        compiler_params=_params(dimension_semantics=("arbitrary", "arbitrary")),
    )(hcat, hcat, hcat, hcat, hcat, bias, sink.reshape(1, N_HEADS_A), do)


def _shift_down(x, halo8, s):
    if s == 0:
        return x
    rolled = pltpu.roll(x, s, axis=0)
    row8 = lax.broadcasted_iota(jnp.int32, halo8.shape, 0)
    top = jnp.where(row8 < s, pltpu.roll(halo8, s, axis=0), rolled[0:8])
    return top if x.shape[0] == 8 else jnp.concatenate([top, rolled[8:]], axis=0)


def _shift_up(x, halo8, s):
    if s == 0:
        return x
    n = x.shape[0]
    rolled = pltpu.roll(x, n - s, axis=0)
    row8 = lax.broadcasted_iota(jnp.int32, halo8.shape, 0)
    bottom = jnp.where(row8 >= 8 - s, pltpu.roll(halo8, 8 - s, axis=0), rolled[n - 8:n])
    return jnp.concatenate([rolled[:n - 8], bottom], axis=0)


def _l2n(x, scale):
    r = lax.rsqrt(jnp.sum(x * x, axis=-1, keepdims=True) + NORM_EPS)
    return x * (r * scale)


def _conv_prep(hcat, conv_w, nbatch, *, side=None, name):
    t = hcat.shape[0]
    nt = t // nbatch // ROW_T
    cb = HC_CONV // CONV_CH

    def body(u_ref, halo_ref, w_ref, q_ref, k_ref, v_ref):
        i = pl.program_id(1)
        s = _silu(_conv_rows(u_ref[...], jnp.where(i == 0, 0.0, halo_ref[...]), w_ref))
        for h in range(N_HEADS_B):
            lo, hi = h * KEY_DIM_B, (h + 1) * KEY_DIM_B
            q_ref[:, lo:hi] = _l2n(s[:, lo:hi], KEY_DIM_B ** -0.5)
            k_ref[:, lo:hi] = _l2n(s[:, QK_B + lo:QK_B + hi], 1.0)
        v_ref[...] = s[:, 2 * QK_B:]

    row = lambda w: pl.BlockSpec((ROW_T, w), lambda b, i: (b * nt + i, 0))
    outs, side_outs = _call(
        body, name=name, grid=(nbatch, nt),
        in_specs=[pl.BlockSpec((ROW_T, CONV_CH), lambda b, i: (b * nt + i, cb)),
                  pl.BlockSpec((8, CONV_CH), lambda b, i: (jnp.maximum((b * nt + i) * (ROW_T // 8) - 1, 0), cb)),
                  pl.BlockSpec((CONV_K, CONV_CH), lambda b, i: (0, 0))],
        out_specs=[row(QK_B), row(QK_B), row(V_B)],
        out_shape=[jax.ShapeDtypeStruct((t, QK_B), F32)] * 3,
        semantics=("parallel", "parallel"), args=(hcat, hcat, conv_w), side=side)
    return tuple(outs), side_outs


def _conv_rows(u, halo8, w_ref):
    c = jnp.zeros_like(u)
    for j in range(CONV_K):
        c = c + w_ref[j:j + 1, :] * _shift_down(u, halo8, CONV_K - 1 - j)
    return c


def _conv_prep_pointwise_bwd(c, dq, dk, dv):
    def l2n_bwd(x, dy, scale):
        r = lax.rsqrt(jnp.sum(x * x, axis=-1, keepdims=True) + NORM_EPS)
        return scale * (r * dy - x * (r * r * r) * jnp.sum(x * dy, axis=-1, keepdims=True))

    sig = _sigmoid(c)
    s = c * sig
    ds = sig * (1.0 + c * (1.0 - sig))
    parts_q, parts_k = [], []
    for h in range(N_HEADS_B):
        lo, hi = h * KEY_DIM_B, (h + 1) * KEY_DIM_B
        parts_q.append(l2n_bwd(s[:, lo:hi], dq[:, lo:hi], KEY_DIM_B ** -0.5) * ds[:, lo:hi])
        parts_k.append(l2n_bwd(s[:, QK_B + lo:QK_B + hi], dk[:, lo:hi], 1.0) * ds[:, QK_B + lo:QK_B + hi])
    return jnp.concatenate(parts_q + parts_k + [dv * ds[:, 2 * QK_B:]], axis=1)


def _conv_bwd(dq, dk, dv, hcat, conv_w, nbatch, *, name):
    t = dq.shape[0]
    nt = t // nbatch // ROW_T
    cb = HC_CONV // CONV_CH
    last_blk = t // 8 - 1

    def body(dq_ref, dk_ref, dv_ref, dqn_ref, dkn_ref, dvn_ref, u_ref, uprev_ref, unext_ref, w_ref, du_ref, dw_ref):
        b = pl.program_id(0)
        i = pl.program_id(1)
        u = u_ref[...]
        uprev = jnp.where(i == 0, 0.0, uprev_ref[...])
        shifted = [_shift_down(u, uprev, CONV_K - 1 - j) for j in range(CONV_K)]
        c = jnp.zeros_like(u)
        for j in range(CONV_K):
            c = c + w_ref[j:j + 1, :] * shifted[j]
        dc_ = _conv_prep_pointwise_bwd(c, dq_ref[...], dk_ref[...], dv_ref[...])
        c_next = _conv_rows(unext_ref[...], u[ROW_T - 8:], w_ref)
        dnext = _conv_prep_pointwise_bwd(c_next, dqn_ref[...], dkn_ref[...], dvn_ref[...])
        dnext = jnp.where(i == nt - 1, 0.0, dnext)
        du = jnp.zeros_like(dc_)
        rows = []
        for j in range(CONV_K):
            du = du + w_ref[j:j + 1, :] * _shift_up(dc_, dnext, CONV_K - 1 - j)
            rows.append(jnp.sum(dc_ * shifted[j], axis=0, keepdims=True))
        du_ref[...] = du.astype(BF16)
        dw_p = jnp.concatenate(rows + [jnp.zeros((8 - CONV_K, CONV_CH), F32)], axis=0)

        @pl.when((b == 0) & (i == 0))
        def _():
            dw_ref[...] = dw_p

        @pl.when((b > 0) | (i > 0))
        def _():
            dw_ref[...] += dw_p

    def nxt(b, i):
        return jnp.minimum((b * nt + i + 1) * (ROW_T // 8), last_blk)

    row = lambda wd: pl.BlockSpec((ROW_T, wd), lambda b, i: (b * nt + i, 0))
    halo = lambda wd: pl.BlockSpec((8, wd), lambda b, i: (nxt(b, i), 0))
    return pl.pallas_call(
        body, name=name, grid=(nbatch, nt),
        in_specs=[row(QK_B), row(QK_B), row(V_B), halo(QK_B), halo(QK_B), halo(V_B),
                  pl.BlockSpec((ROW_T, CONV_CH), lambda b, i: (b * nt + i, cb)),
                  pl.BlockSpec((8, CONV_CH), lambda b, i: (jnp.maximum((b * nt + i) * (ROW_T // 8) - 1, 0), cb)),
                  pl.BlockSpec((8, CONV_CH), lambda b, i: (nxt(b, i), cb)),
                  pl.BlockSpec((CONV_K, CONV_CH), lambda b, i: (0, 0))],
        out_specs=[pl.BlockSpec((ROW_T, CONV_CH), lambda b, i: (b * nt + i, 0)),
                   pl.BlockSpec((8, CONV_CH), lambda b, i: (0, 0))],
        out_shape=[jax.ShapeDtypeStruct((t, CONV_CH), BF16), jax.ShapeDtypeStruct((8, CONV_CH), F32)],
        compiler_params=_params(dimension_semantics=("arbitrary", "arbitrary")),
    )(dq, dk, dv, dq, dk, dv, hcat, hcat, hcat, conv_w)


def _softplus(x):
    return jnp.maximum(x, 0.0) + jnp.log(1.0 + jnp.exp(-jnp.abs(x)))


def _gates(hcat, a_row, dt_row, *, name):
    t = hcat.shape[0]

    def body(bd_ref, a_ref, dt_ref, gb_ref, bb_ref):
        bd = bd_ref[...]
        beta = _sigmoid(bd)
        g = -jnp.exp(a_ref[...]) * _softplus(bd + dt_ref[...])
        for h in range(N_HEADS_B):
            lo, hi = h * VAL_DIM_B, (h + 1) * VAL_DIM_B
            bb_ref[:, lo:hi] = jnp.broadcast_to(beta[:, h:h + 1], (ROW_T, VAL_DIM_B))
            gb_ref[:, lo:hi] = jnp.broadcast_to(g[:, N_HEADS_B + h:N_HEADS_B + h + 1], (ROW_T, VAL_DIM_B))

    vec = pl.BlockSpec((1, 128), lambda i: (0, 0))
    row = pl.BlockSpec((ROW_T, V_B), lambda i: (i, 0))
    return pl.pallas_call(
        body, name=name, grid=(t // ROW_T,),
        in_specs=[pl.BlockSpec((ROW_T, 128), lambda i: (i, HC_BD // 128)), vec, vec],
        out_specs=[row, row], out_shape=[jax.ShapeDtypeStruct((t, V_B), F32)] * 2,
        compiler_params=_params(dimension_semantics=("parallel",)),
    )(hcat, a_row, dt_row)


def _gates_bwd(dgb, dbb, hcat, a_row, dt_row, *, name):
    t = hcat.shape[0]

    def body(dgb_ref, dbb_ref, bd_ref, a_ref, dt_ref, dbd_ref, da_ref, ddt_ref):
        i = pl.program_id(0)
        bd = bd_ref[...]
        beta = _sigmoid(bd)
        ea = jnp.exp(a_ref[...])
        x = bd + dt_ref[...]
        g = -ea * _softplus(x)
        lane = lax.broadcasted_iota(jnp.int32, (ROW_T, 128), 1)
        dbeta = jnp.zeros((ROW_T, 128), F32)
        dg = jnp.zeros((ROW_T, 128), F32)
        for h in range(N_HEADS_B):
            lo, hi = h * VAL_DIM_B, (h + 1) * VAL_DIM_B
            dbeta = dbeta + jnp.where(lane == h, jnp.sum(dbb_ref[:, lo:hi], axis=-1, keepdims=True), 0.0)
            dg = dg + jnp.where(lane == N_HEADS_B + h, jnp.sum(dgb_ref[:, lo:hi], axis=-1, keepdims=True), 0.0)
        ddt_raw = dg * (-ea) * _sigmoid(x)
        dbd_ref[...] = (dbeta * beta * (1.0 - beta) + ddt_raw).astype(BF16)
        da_p = jnp.sum(dg * g, axis=0, keepdims=True)
        ddt_p = jnp.sum(ddt_raw, axis=0, keepdims=True)

        @pl.when(i == 0)
        def _():
            da_ref[...] = da_p
            ddt_ref[...] = ddt_p

        @pl.when(i > 0)
        def _():
            da_ref[...] += da_p
            ddt_ref[...] += ddt_p

    vec = pl.BlockSpec((1, 128), lambda i: (0, 0))
    row = pl.BlockSpec((ROW_T, V_B), lambda i: (i, 0))
    return pl.pallas_call(
        body, name=name, grid=(t // ROW_T,),
        in_specs=[row, row, pl.BlockSpec((ROW_T, 128), lambda i: (i, HC_BD // 128)), vec, vec],
        out_specs=[pl.BlockSpec((ROW_T, 128), lambda i: (i, 0)), vec, vec],
        out_shape=[jax.ShapeDtypeStruct((t, 128), BF16), jax.ShapeDtypeStruct((1, 128), F32),
                   jax.ShapeDtypeStruct((1, 128), F32)],
        compiler_params=_params(dimension_semantics=("arbitrary",)),
    )(dgb, dbb, hcat, a_row, dt_row)


def _group_masks():
    r = lax.broadcasted_iota(jnp.int32, (GROUP_T, GROUP_T), 0)
    c = lax.broadcasted_iota(jnp.int32, (GROUP_T, GROUP_T), 1)
    same = (r // CHUNK) == (c // CHUNK)
    return same, same & (r >= c), same & (r > c)


def _split2(a):
    hi = a.astype(MXU_DTYPE)
    return hi, (a - hi.astype(F32)).astype(MXU_DTYPE)


def _dot3(a2, b2, dims=_NN):
    (ah, al), (bh, bl) = a2, b2
    d = functools.partial(lax.dot_general, dimension_numbers=dims, preferred_element_type=F32)
    return d(ah, bh) + (d(ah, bl) + d(al, bh))


def _inv_unit_lower(lows):
    shape = lows[0].shape
    eye = (lax.broadcasted_iota(jnp.int32, shape, 0) == lax.broadcasted_iota(jnp.int32, shape, 1)).astype(F32)
    p2 = [_split2(-low) for low in lows]
    ts = [eye - low for low in lows]
    for _ in range(int(math.log2(CHUNK)) - 1):
        p2 = [_split2(_dot3(p, p)) for p in p2]
        ts = [t + _dot3(_split2(t), p) for t, p in zip(ts, p2)]
    return ts


@jax.custom_vjp
def _inv_saved(low, t):
    return t


def _inv_saved_fwd(low, t):
    return t, t


def _inv_saved_bwd(t, dt):
    t2 = _split2(t)
    return -_dot3(t2, _split2(_dot3(_split2(dt), t2, _NT)), _TN), jnp.zeros_like(t)


_inv_saved.defvjp(_inv_saved_fwd, _inv_saved_bwd)


def _mask_dot(mask, x, dims):
    m = mask.astype(MXU_DTYPE)
    hi = x.astype(MXU_DTYPE)
    r1 = x - hi.astype(F32)
    mid = r1.astype(MXU_DTYPE)
    lo = (r1 - mid.astype(F32)).astype(MXU_DTYPE)
    d = functools.partial(lax.dot_general, dimension_numbers=dims, preferred_element_type=F32)
    return d(m, hi) + (d(m, mid) + d(m, lo))


@jax.custom_vjp
def _chunk_sums(gb):
    same, causal, _ = _group_masks()
    return _mask_dot(causal, gb, _NN), _mask_dot(same, gb, _NN)


def _chunk_sums_fwd(gb):
    return _chunk_sums(gb), None


def _chunk_sums_bwd(_, cot):
    same, causal, _ = _group_masks()
    return (_mask_dot(causal, cot[0], _TN) + _mask_dot(same, cot[1], _TN),)


_chunk_sums.defvjp(_chunk_sums_fwd, _chunk_sums_bwd)


def _fold_blocks(m):
    return m[:, 0:CHUNK] + m[:, CHUNK:2 * CHUNK] + m[:, 2 * CHUNK:3 * CHUNK] + m[:, 3 * CHUNK:4 * CHUNK]


def _dn_prep_heads(q, k, v, gb, bb, tsaved=None):
    same, causal, strict = _group_masks()
    heads = range(len(q))
    sums = [_chunk_sums(gb[h]) for h in heads]
    gc = [s[0] for s in sums]
    glast = [s[1] for s in sums]
    decay = [jnp.exp(jnp.where(causal, gc[h][:, 0:1] - gc[h].T[0:1, :], NEG_INF)) for h in heads]
    kb = [k[h] * bb[h] for h in heads]
    vb = [v[h] * bb[h] for h in heads]
    lower = [jnp.where(strict, _dot(kb[h], k[h], _NT) * decay[h], 0.0) for h in heads]
    if tsaved is None:
        tinv = _inv_unit_lower(lower)
    else:
        tinv = [_inv_saved(lower[h], jnp.where(same, jnp.concatenate([tsaved[h]] * (GROUP_T // CHUNK), axis=1), 0.0))
                for h in heads]
    egc = [jnp.exp(gc[h]) for h in heads]
    u = [_dot(tinv[h], vb[h]) for h in heads]
    w = [_dot(tinv[h], kb[h] * egc[h]) for h in heads]
    a = [_fold_blocks(jnp.where(causal, _dot(q[h], k[h], _NT) * decay[h], 0.0)) for h in heads]
    k_tail = [k[h] * jnp.exp(glast[h] - gc[h]) for h in heads]
    q_dec = [q[h] * egc[h] for h in heads]
    return u, w, q_dec, k_tail, a, glast, [_fold_blocks(t) for t in tinv]


def _head_slices(ref, width):
    return [ref[:, h * width:(h + 1) * width] for h in range(N_HEADS_B)]


def _store_heads(ref, vals, width):
    for h, val in enumerate(vals):
        ref[:, h * width:(h + 1) * width] = val


def _dn_prep(q, k, v, gb, bb, *, side=None, name):
    t = q.shape[0]

    def body(q_ref, k_ref, v_ref, gb_ref, bb_ref, u_ref, w_ref, qd_ref, kt_ref, a_ref, gl_ref, ti_ref):
        outs = _dn_prep_heads(*[_head_slices(r, KEY_DIM_B) for r in (q_ref, k_ref, v_ref, gb_ref, bb_ref)])
        for ref, vals in zip((u_ref, w_ref, qd_ref, kt_ref, a_ref, gl_ref, ti_ref), outs):
            _store_heads(ref, vals, vals[0].shape[1])

    row = pl.BlockSpec((GROUP_T, V_B), lambda i: (i, 0))
    arow = pl.BlockSpec((GROUP_T, N_HEADS_B * CHUNK), lambda i: (i, 0))
    big = jax.ShapeDtypeStruct((t, V_B), F32)
    small = jax.ShapeDtypeStruct((t, N_HEADS_B * CHUNK), F32)
    outs, side_outs = _call(
        body, name=name, grid=(t // GROUP_T,), in_specs=[row] * 5, out_specs=[row, row, row, row, arow, row, arow],
        out_shape=[big, big, big, big, small, big, small], semantics=("parallel",), args=(q, k, v, gb, bb), side=side)
    return tuple(outs), side_outs


def _dn_prep_bwd(q, k, v, gb, bb, ti, du, dw, dqd, dkt, da, dgl, *, side=None, name):
    t = q.shape[0]

    def body(q_ref, k_ref, v_ref, gb_ref, bb_ref, ti_ref, du_ref, dw_ref, dqd_ref, dkt_ref, da_ref, dgl_ref,
             dq_ref, dk_ref, dv_ref, dgb_ref, dbb_ref):
        for lo in range(0, N_HEADS_B, DN_BWD_HEADS):
            grp = slice(lo, lo + DN_BWD_HEADS)
            tsaved = _head_slices(ti_ref, CHUNK)[grp]
            _, vjp = jax.vjp(lambda *a, ts=tsaved: _dn_prep_heads(*a, tsaved=ts)[:6],
                             *[_head_slices(r, KEY_DIM_B)[grp] for r in (q_ref, k_ref, v_ref, gb_ref, bb_ref)])
            cot = tuple(_head_slices(r, CHUNK if r is da_ref else KEY_DIM_B)[grp]
                        for r in (du_ref, dw_ref, dqd_ref, dkt_ref, da_ref, dgl_ref))
            for ref, vals in zip((dq_ref, dk_ref, dv_ref, dgb_ref, dbb_ref), vjp(cot)):
                for h, val in enumerate(vals):
                    ref[:, (lo + h) * KEY_DIM_B:(lo + h + 1) * KEY_DIM_B] = val

    row = pl.BlockSpec((GROUP_T, V_B), lambda i: (i, 0))
    arow = pl.BlockSpec((GROUP_T, N_HEADS_B * CHUNK), lambda i: (i, 0))
    big = jax.ShapeDtypeStruct((t, V_B), F32)
    outs, side_outs = _call(
        body, name=name, grid=(t // GROUP_T,),
        in_specs=[row] * 5 + [arow] + [row] * 4 + [arow, row], out_specs=[row] * 5, out_shape=[big] * 5,
        semantics=("parallel",), args=(q, k, v, gb, bb, ti, du, dw, dqd, dkt, da, dgl), side=side)
    return tuple(outs), side_outs


def _dn_steps(s, qd, kt, u, w, a, gl):
    heads = range(len(s))
    v_new = [u[h] - _dot(w[h], s[h]) for h in heads]
    qs = [_dot(qd[h], s[h]) for h in heads]
    o = [qs[h] + _dot(a[h], v_new[h]) for h in heads]
    s_new = [s[h] * jnp.exp(gl[h][0:1, :]) + _dot(kt[h], v_new[h], _TN) for h in heads]
    return s_new, o


def _dn_scan(u, w, qd, kt, a, gl, nbatch, *, side=None, name):
    t = u.shape[0]
    ng = t // nbatch // GROUP_T
    cpg = GROUP_T // CHUNK

    def body(u_ref, w_ref, qd_ref, kt_ref, a_ref, gl_ref, o_ref, ss_ref, s_ref):
        @pl.when(pl.program_id(1) == 0)
        def _():
            s_ref[...] = jnp.zeros_like(s_ref)

        def chunk(c, carry):
            rows = pl.ds(pl.multiple_of(c * CHUNK, CHUNK), CHUNK)
            heads = range(N_HEADS_B)
            s = [s_ref[h] for h in heads]
            for h in heads:
                ss_ref[c, h] = s[h]
            s_new, o = _dn_steps(s, *[[r[rows, h * wd:(h + 1) * wd] for h in heads] for r, wd in
                                      ((qd_ref, KEY_DIM_B), (kt_ref, KEY_DIM_B), (u_ref, VAL_DIM_B), (w_ref, KEY_DIM_B),
                                       (a_ref, CHUNK), (gl_ref, VAL_DIM_B))])
            for h in heads:
                s_ref[h] = s_new[h]
                o_ref[rows, h * VAL_DIM_B:(h + 1) * VAL_DIM_B] = o[h]
            return carry

        lax.fori_loop(0, cpg, chunk, 0)

    row = pl.BlockSpec((GROUP_T, V_B), lambda b, i: (b * ng + i, 0))
    arow = pl.BlockSpec((GROUP_T, N_HEADS_B * CHUNK), lambda b, i: (b * ng + i, 0))
    outs, side_outs = _call(
        body, name=name, grid=(nbatch, ng),
        in_specs=[row, row, row, row, arow, row],
        out_specs=[row, pl.BlockSpec((cpg, N_HEADS_B, KEY_DIM_B, VAL_DIM_B), lambda b, i: (b * ng + i, 0, 0, 0))],
        out_shape=[jax.ShapeDtypeStruct((t, V_B), F32),
                   jax.ShapeDtypeStruct((t // CHUNK, N_HEADS_B, KEY_DIM_B, VAL_DIM_B), F32)],
        scratch_shapes=[pltpu.VMEM((N_HEADS_B, KEY_DIM_B, VAL_DIM_B), F32)],
        semantics=("parallel", "arbitrary"), args=(u, w, qd, kt, a, gl), side=side)
    return tuple(outs), side_outs


def _dn_scan_bwd(u, w, qd, kt, a, gl, ss, do, nbatch, *, name):
    t = u.shape[0]
    ng = t // nbatch // GROUP_T
    cpg = GROUP_T // CHUNK

    def body(u_ref, w_ref, qd_ref, kt_ref, a_ref, gl_ref, ss_ref, do_ref,
             du_ref, dw_ref, dqd_ref, dkt_ref, da_ref, dgl_ref, ds_ref):
        @pl.when(pl.program_id(1) == 0)
        def _():
            ds_ref[...] = jnp.zeros_like(ds_ref)

        def chunk(cc, carry):
            c = cpg - 1 - cc
            rows = pl.ds(pl.multiple_of(c * CHUNK, CHUNK), CHUNK)
            heads = range(N_HEADS_B)
            ins = ((qd_ref, KEY_DIM_B), (kt_ref, KEY_DIM_B), (u_ref, VAL_DIM_B), (w_ref, KEY_DIM_B), (a_ref, CHUNK),
                   (gl_ref, VAL_DIM_B))
            _, vjp = jax.vjp(_dn_steps, [ss_ref[c, h] for h in heads],
                             *[[r[rows, h * wd:(h + 1) * wd] for h in heads] for r, wd in ins])
            grads = vjp(([ds_ref[h] for h in heads], [do_ref[rows, h * VAL_DIM_B:(h + 1) * VAL_DIM_B] for h in heads]))
            for h in heads:
                ds_ref[h] = grads[0][h]
            outs = ((dqd_ref, KEY_DIM_B), (dkt_ref, KEY_DIM_B), (du_ref, VAL_DIM_B), (dw_ref, KEY_DIM_B), (da_ref, CHUNK),
                    (dgl_ref, VAL_DIM_B))
            for (r, wd), vals in zip(outs, grads[1:]):
                for h in heads:
                    r[rows, h * wd:(h + 1) * wd] = vals[h]
            return carry

        lax.fori_loop(0, cpg, chunk, 0)

    row = pl.BlockSpec((GROUP_T, V_B), lambda b, j: (b * ng + ng - 1 - j, 0))
    arow = pl.BlockSpec((GROUP_T, N_HEADS_B * CHUNK), lambda b, j: (b * ng + ng - 1 - j, 0))
    big = jax.ShapeDtypeStruct((t, V_B), F32)
    return pl.pallas_call(
        body, name=name, grid=(nbatch, ng),
        in_specs=[row, row, row, row, arow, row,
                  pl.BlockSpec((cpg, N_HEADS_B, KEY_DIM_B, VAL_DIM_B), lambda b, j: (b * ng + ng - 1 - j, 0, 0, 0)), row],
        out_specs=[row, row, row, row, arow, row],
        out_shape=[big, big, big, big, jax.ShapeDtypeStruct((t, N_HEADS_B * CHUNK), F32), big],
        scratch_shapes=[pltpu.VMEM((N_HEADS_B, KEY_DIM_B, VAL_DIM_B), F32)],
        compiler_params=_params(dimension_semantics=("parallel", "arbitrary")),
    )(u, w, qd, kt, a, gl, ss, do)


def _rms_gate(o, hcat, dn_g, *, name):
    t = o.shape[0]

    def body(o_ref, z_ref, g_ref, y_ref):
        for h in range(N_HEADS_B):
            sl = slice(h * VAL_DIM_B, (h + 1) * VAL_DIM_B)
            o_ = o_ref[:, sl]
            r = lax.rsqrt(jnp.mean(o_ * o_, axis=-1, keepdims=True) + NORM_EPS)
            y_ref[:, sl] = (o_ * r * g_ref[...] * _silu(z_ref[:, sl])).astype(BF16)

    row = pl.BlockSpec((ROW_T, V_B), lambda i: (i, 0))
    return pl.pallas_call(
        body, name=name, grid=(t // ROW_T,),
        in_specs=[row, pl.BlockSpec((ROW_T, V_B), lambda i: (i, HC_Z // V_B)), pl.BlockSpec((1, VAL_DIM_B), lambda i: (0, 0))],
        out_specs=row, out_shape=jax.ShapeDtypeStruct((t, V_B), BF16),
        compiler_params=_params(dimension_semantics=("parallel",)),
    )(o, hcat, dn_g.reshape(1, VAL_DIM_B))


def _rms_gate_bwd(dy, o, hcat, dn_g, *, name):
    t = o.shape[0]

    def body(dy_ref, o_ref, z_ref, g_ref, do_ref, dz_ref, dg_ref):
        i = pl.program_id(0)
        g = g_ref[...]
        dg_p = jnp.zeros((1, VAL_DIM_B), F32)
        for h in range(N_HEADS_B):
            sl = slice(h * VAL_DIM_B, (h + 1) * VAL_DIM_B)
            o_ = o_ref[:, sl]
            z_ = z_ref[:, sl]
            dy_ = dy_ref[:, sl]
            r = lax.rsqrt(jnp.mean(o_ * o_, axis=-1, keepdims=True) + NORM_EPS)
            n = o_ * r
            sz = _silu(z_)
            dz_ref[:, sl] = (dy_ * n * g * _dsilu(z_)).astype(BF16)
            dg_p = dg_p + jnp.sum(dy_ * n * sz, axis=0, keepdims=True)
            dn = dy_ * g * sz
            do_ref[:, sl] = r * dn - o_ * (r * r * r) * jnp.mean(o_ * dn, axis=-1, keepdims=True)

        @pl.when(i == 0)
        def _():
            dg_ref[...] = dg_p

        @pl.when(i > 0)
        def _():
            dg_ref[...] += dg_p

    row = pl.BlockSpec((ROW_T, V_B), lambda i: (i, 0))
    vec = pl.BlockSpec((1, VAL_DIM_B), lambda i: (0, 0))
    return pl.pallas_call(
        body, name=name, grid=(t // ROW_T,),
        in_specs=[row, row, pl.BlockSpec((ROW_T, V_B), lambda i: (i, HC_Z // V_B)), vec],
        out_specs=[row, row, vec],
        out_shape=[jax.ShapeDtypeStruct((t, V_B), F32), jax.ShapeDtypeStruct((t, V_B), BF16),
                   jax.ShapeDtypeStruct((1, VAL_DIM_B), F32)],
        compiler_params=_params(dimension_semantics=("arbitrary",)),
    )(dy, o, hcat, dn_g.reshape(1, VAL_DIM_B))


def _merge(ya, yb, hcat, *, name):
    t = ya.shape[0]

    def body(ya_ref, yb_ref, ga_ref, gb_ref, y_ref):
        y_ref[...] = (_sigmoid(ga_ref[...]) * ya_ref[...] + _sigmoid(gb_ref[...]) * yb_ref[...]).astype(BF16)

    row = pl.BlockSpec((ROW_T, D_MODEL), lambda i: (i, 0))
    return pl.pallas_call(
        body, name=name, grid=(t // ROW_T,),
        in_specs=[row, row, pl.BlockSpec((ROW_T, D_MODEL), lambda i: (i, HC_GATE // D_MODEL)),
                  pl.BlockSpec((ROW_T, D_MODEL), lambda i: (i, HC_GATE // D_MODEL + 1))],
        out_specs=row, out_shape=jax.ShapeDtypeStruct((t, D_MODEL), BF16),
        compiler_params=_params(dimension_semantics=("parallel",)),
    )(ya, yb, hcat, hcat)


def _merge_bwd(dmix, ya, yb, hcat, *, name):
    t = ya.shape[0]

    def body(d_ref, ya_ref, yb_ref, ga_ref, gb_ref, dya_ref, dyb_ref, dgate_ref):
        d = d_ref[...]
        sa = _sigmoid(ga_ref[...])
        sb = _sigmoid(gb_ref[...])
        dya_ref[...] = (d * sa).astype(BF16)
        dyb_ref[...] = (d * sb).astype(BF16)
        dgate_ref[:, :D_MODEL] = (d * ya_ref[...] * sa * (1.0 - sa)).astype(BF16)
        dgate_ref[:, D_MODEL:] = (d * yb_ref[...] * sb * (1.0 - sb)).astype(BF16)

    row = pl.BlockSpec((ROW_T, D_MODEL), lambda i: (i, 0))
    return pl.pallas_call(
        body, name=name, grid=(t // ROW_T,),
        in_specs=[row, row, row, pl.BlockSpec((ROW_T, D_MODEL), lambda i: (i, HC_GATE // D_MODEL)),
                  pl.BlockSpec((ROW_T, D_MODEL), lambda i: (i, HC_GATE // D_MODEL + 1))],
        out_specs=[row, row, pl.BlockSpec((ROW_T, 2 * D_MODEL), lambda i: (i, 0))],
        out_shape=[jax.ShapeDtypeStruct((t, D_MODEL), BF16)] * 2 + [jax.ShapeDtypeStruct((t, 2 * D_MODEL), BF16)],
        compiler_params=_params(dimension_semantics=("parallel",)),
    )(dmix, ya, yb, hcat, hcat)


def _loss_head(y, target, *, name):
    t, n = y.shape
    tm = _tile(t, (512, 256, 128))

    def body(y_ref, t_ref, part_ref, dy_ref):
        i = pl.program_id(0)
        e = y_ref[...] - t_ref[...]
        dy_ref[...] = e * (1.0 / n)
        p = jnp.sum((e * e).reshape(tm // 8, 8, n), axis=0) * (0.5 / n)

        @pl.when(i == 0)
        def _():
            part_ref[...] = p

        @pl.when(i > 0)
        def _():
            part_ref[...] += p

    row = pl.BlockSpec((tm, n), lambda i: (i, 0))
    return pl.pallas_call(
        body, name=name, grid=(t // tm,),
        in_specs=[row, row], out_specs=[pl.BlockSpec((8, n), lambda i: (0, 0)), row],
        out_shape=[jax.ShapeDtypeStruct((8, n), F32), jax.ShapeDtypeStruct((t, n), F32)],
        compiler_params=_params(dimension_semantics=("arbitrary",)),
    )(y, target)


def _adamw_math(w, g, m, v):
    nm = ADAM_B1 * m + (1.0 - ADAM_B1) * g
    nv = ADAM_B2 * v + (1.0 - ADAM_B2) * (g * g)
    m_hat = nm / (1.0 - ADAM_B1 ** ADAM_STEP)
    v_hat = nv / (1.0 - ADAM_B2 ** ADAM_STEP)
    return -ADAM_LR * (m_hat / (jnp.sqrt(v_hat) + ADAM_EPS) + ADAM_WD * w), nm, nv


def _adamw(w, g, m, v, *, name):
    shape = w.shape
    cols = shape[-1]
    rows = int(np.prod(shape[:-1]))
    w2, g2, m2, v2 = (a.reshape(rows, cols) for a in (w, g, m, v))
    tr = rows
    if rows * cols > 512 * 1024:
        tr = _tile(rows, tuple(c for c in (512, 256, 128, 64, 32, 16, 8) if c * cols <= 256 * 1024))

    def body(w_ref, g_ref, m_ref, v_ref, d_ref, nm_ref, nv_ref):
        d_ref[...], nm_ref[...], nv_ref[...] = _adamw_math(w_ref[...], g_ref[...], m_ref[...], v_ref[...])

    blk = pl.BlockSpec((tr, cols), lambda i: (i, 0))
    outs = pl.pallas_call(
        body, name=name, grid=(rows // tr,),
        in_specs=[blk] * 4, out_specs=[blk] * 3,
        out_shape=[jax.ShapeDtypeStruct((rows, cols), F32)] * 3,
        compiler_params=_params(dimension_semantics=("parallel",)),
    )(w2, g2, m2, v2)
    return tuple(o.reshape(shape) for o in outs)


def _repack_w_in(w_in):
    d = w_in.shape[0]
    o = 0
    parts = {}
    for nm, wd in (("q", Q_A), ("k", KV_W), ("v", KV_W), ("conv", CONV_CH), ("beta", N_HEADS_B), ("dt", N_HEADS_B),
                   ("z", V_B), ("gate", 2 * D_MODEL)):
        parts[nm] = w_in[:, o:o + wd]
        o += wd
    z = lambda n: jnp.zeros((d, n), w_in.dtype)
    return jnp.concatenate([parts["q"], parts["z"], parts["k"], parts["v"], parts["beta"], parts["dt"],
                            z(128 - 2 * N_HEADS_B), z(HC_CONV - HC_BD - 128), parts["conv"], parts["gate"]], axis=1)


MATRIX_NAMES = ("ffn_w13", "ffn_w2", "w_in", "w_branch_a", "w_branch_b", "w_out")
GATHER_BESIDE_IN_PROJ = ("conv_w", "w_branch_a", "w_branch_b", "w_out")


def _dw_in_by_owner(dw):
    sections = ((Q_A, HC_Q), (2 * KV_W, HC_K), (CONV_CH, HC_CONV), (2 * N_HEADS_B, HC_BD), (V_B, HC_Z),
                (2 * D_MODEL, HC_GATE))
    per = N_IN // 4
    owners = []
    for o in range(4):
        lo, hi, start, parts = o * per, (o + 1) * per, 0, []
        for width, off in sections:
            a, b = max(lo, start), min(hi, start + width)
            if a < b:
                parts.append(dw[:, off + a - start:off + b - start])
            start += width
        rows = jnp.concatenate(parts, axis=1)
        owners.append(jnp.stack([rows[:rows.shape[0] // 2], rows[rows.shape[0] // 2:]]))
    return jnp.stack(owners)


def _lane_row(vals):
    return jnp.pad(vals.astype(F32).reshape(1, N_HEADS_B), ((0, 0), (N_HEADS_B, 128 - 2 * N_HEADS_B)))


def _local_step(x, target, rel_bias, layer_wts, side_shards=None, side_assemble=None, reducer=None):
    nbatch, seq, d = x.shape
    t = nbatch * seq
    depth = len(layer_wts)
    layer_wts = list(layer_wts)
    x0 = x.reshape(t, d)
    tgt = target.reshape(t, d)

    onehot = jnp.asarray(_bucket_onehot())
    rel_t = jnp.pad(rel_bias.T, ((0, 0), (0, 128 - NUM_BUCKETS)))
    bias = _mm(rel_t, onehot, tb=True, exact=True, name="pos_bias")
    bias = bias.reshape(N_HEADS_A, WINDOW, 2 * WINDOW)

    pending = [dict(s) if s else {} for s in (side_shards or [None] * depth)] + [{}]

    def fetch(*wanted):
        picked = [(layer, n) for layer, names in wanted for n in list(pending[layer]) if names is None or n in names]
        if not picked:
            return None, lambda outs: None
        job = _gather_job([pending[layer].pop(n) for layer, n in picked])

        def finish(outs):
            for (layer, n), out in zip(picked, outs):
                for k, val in side_assemble(layer, {n: out}).items():
                    if isinstance(val, dict):
                        layer_wts[layer].setdefault(k, {}).update(val)
                    else:
                        layer_wts[layer][k] = val
        return job, finish

    saved = []
    xin, xin_b = x0, x0.astype(BF16)
    for i in range(depth):
        L = {}
        W = layer_wts[i]
        tag = f"_l{i}"
        job, finish = fetch((i, ("w_in",)))
        a, got = _ffn_up_act(xin_b, W["ffn_w13"][0], side=job, name="ffn_up_act" + tag + "a")
        finish(got)
        r1, x1, x1_b = _mm_res_ln(a, W["ffn_w2"][0], xin, W["ln_g"][0], W["ln_b"][0],
                                  alpha=DN_ALPHA, c=0.5, name="ffn_down_ln" + tag + "a")
        L.update(x0_b=xin_b, a0=a, r1=r1, x1=x1, x1_b=x1_b)
        job, finish = fetch((i, GATHER_BESIDE_IN_PROJ), (i + 1, GATHER_BESIDE_IN_PROJ))
        hcat, got = _mm_w(x1_b, W["w_in_p"], side=job, name="in_proj" + tag)
        finish(got)
        job, finish = fetch((i + 1, ("w_in",)))
        ao, got = _attn_fwd(hcat, bias, W["sinks"], nbatch, side=job, name="swa" + tag)
        finish(got)
        ya = _mm(ao, W["w_branch_a"], name="branch_a" + tag)
        job, finish = fetch((i, ("ffn_w13_1",)))
        (qn, kn, vs), got = _conv_prep(hcat, W["conv_w"], nbatch, side=job, name="conv_prep" + tag)
        finish(got)
        a_row = _lane_row(W["a_log"])
        dt_row = _lane_row(W["dt_bias"])
        gb, bb = _gates(hcat, a_row, dt_row, name="gates" + tag)
        job, finish = fetch((i + 1, ("ln_g", "ln_b", "ffn_w13_0", "ffn_w13_1", "ffn_w2_0", "ffn_w2_1")))
        (u, w, qd, kt, aa, gl, ti), got = _dn_prep(qn, kn, vs, gb, bb, side=job, name="dn_prep" + tag)
        finish(got)
        job, finish = fetch((i, None))
        (o, ss), got = _dn_scan(u, w, qd, kt, aa, gl, nbatch, side=job, name="dn_scan" + tag)
        finish(got)
        on = _rms_gate(o, hcat, W["dn_norm_g"], name="rms_gate" + tag)
        yb = _mm(on, W["w_branch_b"], name="branch_b" + tag)
        mix = _merge(ya, yb, hcat, name="merge" + tag)
        r2, x2, x2_b = _mm_res_ln(mix, W["w_out"], x1, W["ln_g"][1], W["ln_b"][1],
                                  alpha=DN_ALPHA, c=1.0, name="out_proj_ln" + tag)
        L.update(hcat=hcat, ao=ao, ya=ya, qn=qn, kn=kn, vs=vs, gb=gb, bb=bb, a_row=a_row, dt_row=dt_row,
                 u=u, w=w, qd=qd, kt=kt, aa=aa, gl=gl, ti=ti, o=o, ss=ss, on=on, yb=yb, mix=mix, r2=r2, x2_b=x2_b)
        a, _ = _ffn_up_act(x2_b, W["ffn_w13"][1], name="ffn_up_act" + tag + "b")
        r3, x3, x3_b = _mm_res_ln(a, W["ffn_w2"][1], x2, W["ln_g"][2], W["ln_b"][2],
                                  alpha=DN_ALPHA, c=0.5, name="ffn_down_ln" + tag + "b")
        L.update(a1=a, r3=r3)
        saved.append(L)
        xin, xin_b = x3, x3_b

    part, dy = _loss_head(xin, tgt, name="loss_head")
    loss = jnp.sum(part)

    grads = {k: [None] * depth for k in ("ln_g", "ln_b", "ffn_w13", "ffn_w2", "w_in", "conv_w", "a_log", "dt_bias",
                                          "dn_norm_g", "sinks", "w_branch_a", "w_branch_b", "w_out")}
    dbias_total = None
    for i in reversed(range(depth)):
        L = saved[i]
        W = layer_wts[i]
        tag = f"_l{i}"
        dln_g, dln_b, dw13, dw2 = [None] * 3, [None] * 3, [None] * 2, [None] * 2

        def ffn_bwd(dyo, r, xprev_b, asave, j, sfx):
            dres, df, dln_g[2 * j], dln_b[2 * j] = _ln_bwd(dyo, r, W["ln_g"][2 * j], alpha=DN_ALPHA, c=0.5,
                                                           name="ln_bwd" + tag + sfx)
            job = reducer.job_a() if (reducer is not None and j == 1) else None
            dh, swapped = _ffn_bwd_mid(xprev_b, W["ffn_w13"][j], df, W["ffn_w2"][j], side=job,
                                       name="ffn_bwd_mid" + tag + sfx)
            if job is not None:
                reducer.done_a(swapped)
            dw2[j] = _mm(asave, df, ta=True, name="ffn_w2_grad" + tag + sfx)
            dw13[j] = _mm(xprev_b, dh, ta=True, b_halves=True, name="ffn_w13_grad" + tag + sfx)
            return _mm_wt(dh, W["ffn_w13"][j], dres, a_halves=True, name="ffn_up_bwd" + tag + sfx)

        dx2 = ffn_bwd(dy, L["r3"], L["x2_b"], L["a1"], 1, "b")

        dres2, dymix, dln_g[1], dln_b[1] = _ln_bwd(dx2, L["r2"], W["ln_g"][1], alpha=DN_ALPHA, c=1.0,
                                                   name="ln_bwd" + tag + "m")
        hcat = L["hcat"]
        dmix = _mm(dymix, W["w_out"], tb=True, name="out_proj_bwd" + tag)
        grads["w_out"][i] = _mm(L["mix"], dymix, ta=True, name="w_out_grad" + tag)
        dya, dyb, dgate = _merge_bwd(dmix, L["ya"], L["yb"], hcat, name="merge_bwd" + tag)
        dao = _mm(dya, W["w_branch_a"], tb=True, name="branch_a_bwd" + tag)
        grads["w_branch_a"][i] = _mm(L["ao"], dya, ta=True, name="w_branch_a_grad" + tag)
        don = _mm(dyb, W["w_branch_b"], tb=True, name="branch_b_bwd" + tag)
        grads["w_branch_b"][i] = _mm(L["on"], dyb, ta=True, name="w_branch_b_grad" + tag)
        do, dz, ddn = _rms_gate_bwd(don, L["o"], hcat, W["dn_norm_g"], name="rms_gate_bwd" + tag)
        grads["dn_norm_g"][i] = ddn.reshape(VAL_DIM_B)
        du, dw, dqd, dkt, daa, dgl = _dn_scan_bwd(L["u"], L["w"], L["qd"], L["kt"], L["aa"], L["gl"], L["ss"], do,
                                                  nbatch, name="dn_scan_bwd" + tag)
        job = reducer.job_b() if reducer is not None else None
        (dqn, dkn, dvs, dgb, dbb), exchanged = _dn_prep_bwd(L["qn"], L["kn"], L["vs"], L["gb"], L["bb"], L["ti"], du, dw,
                                                            dqd, dkt, daa, dgl, side=job, name="dn_prep_bwd" + tag)
        if job is not None:
            reducer.done_b(exchanged)
        dconv, dconv_w = _conv_bwd(dqn, dkn, dvs, hcat, W["conv_w"], nbatch, name="conv_bwd" + tag)
        grads["conv_w"][i] = dconv_w[:CONV_K]
        dbd, da_log, ddt = _gates_bwd(dgb, dbb, hcat, L["a_row"], L["dt_row"], name="gates_bwd" + tag)
        grads["a_log"][i] = da_log[0, N_HEADS_B:2 * N_HEADS_B]
        grads["dt_bias"][i] = ddt[0, N_HEADS_B:2 * N_HEADS_B]
        dq, dk, dv, dbias, dsink = _attn_bwd(hcat, bias, W["sinks"], dao, nbatch, name="swa_bwd" + tag)
        grads["sinks"][i] = dsink.reshape(N_HEADS_A)
        dbias_total = dbias if dbias_total is None else dbias_total + dbias
        dhcat = jnp.concatenate([dq, dz, dk, dv, dbd, jnp.zeros((t, HC_CONV - HC_BD - 128), BF16), dconv, dgate], axis=1)
        dw_in_p = _mm(L["x1_b"], dhcat, ta=True, name="w_in_grad" + tag)
        grads["w_in"][i] = _dw_in_by_owner(dw_in_p)
        dx1 = _mm_wt(dhcat, W["w_in_p"], dres2, name="in_proj_bwd" + tag)

        dy = ffn_bwd(dx1, L["r1"], L["x0_b"], L["a0"], 0, "a")
        grads["ln_g"][i] = jnp.concatenate(dln_g, axis=0)
        grads["ln_b"][i] = jnp.concatenate(dln_b, axis=0)
        grads["ffn_w13"][i] = dw13
        grads["ffn_w2"][i] = dw2
        if reducer is not None:
            reducer.layer_done(i, {n: grads[n][i] for n in MATRIX_NAMES})

    out = {k: (v if k in MATRIX_NAMES else jnp.stack(v)) for k, v in grads.items()}
    drel = _mm(dbias_total.reshape(N_HEADS_A, WINDOW * 2 * WINDOW), onehot, name="rel_bias_grad")
    out["rel_bias"] = drel[:, :NUM_BUCKETS].T
    return loss, dy.reshape(nbatch, seq, d), out


N_CHIPS = 4
MESH_ID = pl.DeviceIdType.MESH
HBM_SPEC = pl.BlockSpec(memory_space=pltpu.HBM)


def _place():
    x, y, c = lax.axis_index("x"), lax.axis_index("y"), lax.axis_index("c")
    others = [(1 - x, y), (x, 1 - y), (1 - x, 1 - y)]
    return x, y, c, others


def _chip_index(cx, cy):
    return 2 * cx + cy


def _gather_sems(n):
    return [pltpu.SemaphoreType.DMA((n, 3)), pltpu.SemaphoreType.DMA((n, 3)), pltpu.SemaphoreType.DMA((n,))]


def _gather_copies(ins, outs, send_sems, recv_sems, local_sems):
    x, y, c, others = _place()
    me = _chip_index(x, y)
    copies = []
    for i in range(len(ins)):
        copies.append(pltpu.make_async_copy(ins[i], outs[i].at[me], local_sems.at[i]))
        for k, (ox, oy) in enumerate(others):
            copies.append(pltpu.make_async_remote_copy(src_ref=ins[i], dst_ref=outs[i].at[me], send_sem=send_sems.at[i, k],
                                                       recv_sem=recv_sems.at[i, k], device_id=(ox, oy, c),
                                                       device_id_type=MESH_ID))
    return copies


def _gather_job(tensors):
    return dict(ins=list(tensors), out_shape=[jax.ShapeDtypeStruct((N_CHIPS,) + t.shape, t.dtype) for t in tensors],
                scratch=_gather_sems(len(tensors)), make=_gather_copies)


def _run_job(job, *, name):
    n_in, n_out = len(job["ins"]), len(job["out_shape"])

    def body(*refs):
        copies = job["make"](refs[:n_in], refs[n_in:n_in + n_out], *refs[n_in + n_out:])
        for cp in copies:
            cp.start()
        for cp in copies:
            cp.wait()

    return pl.pallas_call(
        body, name=name, in_specs=[HBM_SPEC] * n_in, out_specs=[HBM_SPEC] * n_out,
        out_shape=list(job["out_shape"]), scratch_shapes=list(job["scratch"]),
    )(*job["ins"])


def _allgather_devices(v, *, name):
    def body(v_ref, o_ref, send_sems, recv_sems, local_sem):
        x, y, c, _ = _place()
        me = 4 * x + 2 * y + c
        loc = pltpu.make_async_copy(v_ref, o_ref.at[me], local_sem)
        loc.start()
        copies = [loc]
        for k in range(1, 8):
            px, py, pc = x ^ (k >> 2), y ^ ((k >> 1) & 1), c ^ (k & 1)
            cp = pltpu.make_async_remote_copy(src_ref=v_ref, dst_ref=o_ref.at[me], send_sem=send_sems.at[k - 1],
                                              recv_sem=recv_sems.at[k - 1], device_id=(px, py, pc), device_id_type=MESH_ID)
            cp.start()
            copies.append(cp)
        for cp in copies:
            cp.wait()

    return pl.pallas_call(
        body, name=name, in_specs=[HBM_SPEC], out_specs=HBM_SPEC,
        out_shape=jax.ShapeDtypeStruct((8,) + v.shape, v.dtype),
        scratch_shapes=[pltpu.SemaphoreType.DMA((7,)), pltpu.SemaphoreType.DMA((7,)), pltpu.SemaphoreType.DMA],
    )(v)


def _sum_slots(g, *, name):
    nb, n, r, l = g.shape
    tr = r // 2 if r % 32 == 0 else r

    def body(g_ref, o_ref):
        acc = g_ref[0].astype(F32)
        for k in range(1, n):
            acc = acc + g_ref[k].astype(F32)
        o_ref[...] = acc

    return pl.pallas_call(
        body, name=name, grid=(nb, r // tr),
        in_specs=[pl.BlockSpec((None, n, tr, l), lambda b, i: (b, 0, i, 0))],
        out_specs=pl.BlockSpec((None, tr, l), lambda b, i: (b, i, 0)),
        out_shape=jax.ShapeDtypeStruct((nb, r, l), F32),
        compiler_params=_params(dimension_semantics=("parallel", "parallel")),
    )(g)


def _half_window(ref, kind, h):
    if kind == "rows":
        return ref.at[:, h]
    r = ref.shape[0] // 2
    return ref.at[pl.ds(pl.multiple_of(h * r, r), r), :]


def _owner_window(ref, kind, o):
    if kind == "rows":
        return ref.at[o]
    cols = ref.shape[1] // N_CHIPS
    return ref.at[:, pl.ds(pl.multiple_of(o * cols, cols), cols)]


def _half_shape(g, kind):
    return (g.shape[0],) + g.shape[2:] if kind == "rows" else (g.shape[0] // 2, g.shape[1])


def _swap_job(gs, kinds):
    n = len(gs)

    def make(ins, outs, send_sems, recv_sems):
        x, y, c, _ = _place()
        return [pltpu.make_async_remote_copy(src_ref=_half_window(ins[i], kinds[i], 1 - c), dst_ref=outs[i],
                                             send_sem=send_sems.at[i], recv_sem=recv_sems.at[i],
                                             device_id=(x, y, 1 - c), device_id_type=MESH_ID) for i in range(n)]

    return dict(ins=list(gs), out_shape=[jax.ShapeDtypeStruct(_half_shape(g, k), g.dtype) for g, k in zip(gs, kinds)],
                scratch=[pltpu.SemaphoreType.DMA((n,)), pltpu.SemaphoreType.DMA((n,))], make=make)


def _pair_sum(g, got, kind, c_idx, *, name):
    hs = _half_shape(g, kind)

    def body(c_ref, g_ref, r_ref, o_ref):
        o_ref[...] = (g_ref[...] + r_ref[...]).astype(BF16)

    if kind == "rows":
        _, _, r, cols = g.shape
        grid = (N_CHIPS,)
        in_specs = [pl.BlockSpec((None, None, r, cols), lambda o, c_ref: (o, c_ref[0], 0, 0)),
                    pl.BlockSpec((None, r, cols), lambda o, c_ref: (o, 0, 0))]
        out_spec = pl.BlockSpec((None, r, cols), lambda o, c_ref: (o, 0, 0))
    else:
        r, cols = hs
        steps = 4
        tr = r // steps
        grid = (steps,)
        in_specs = [pl.BlockSpec((tr, cols), lambda i, c_ref: (c_ref[0] * steps + i, 0)),
                    pl.BlockSpec((tr, cols), lambda i, c_ref: (i, 0))]
        out_spec = pl.BlockSpec((tr, cols), lambda i, c_ref: (i, 0))
    return pl.pallas_call(
        body, name=name,
        grid_spec=pltpu.PrefetchScalarGridSpec(num_scalar_prefetch=1, grid=grid, in_specs=in_specs, out_specs=out_spec),
        out_shape=jax.ShapeDtypeStruct(hs, BF16),
        compiler_params=_params(dimension_semantics=("parallel",)),
    )(c_idx, g, got)


def _exchange_job(ss, kinds):
    n = len(ss)

    def shard_shape(s, kind):
        return s.shape[1:] if kind == "rows" else (s.shape[0], s.shape[1] // N_CHIPS)

    def make(ins, outs, send_sems, recv_sems, local_sems):
        x, y, c, others = _place()
        me = _chip_index(x, y)
        copies = []
        for i in range(n):
            dst = outs[i].at[me]
            copies.append(pltpu.make_async_copy(_owner_window(ins[i], kinds[i], me), dst, local_sems.at[i]))
            for k, (ox, oy) in enumerate(others):
                copies.append(pltpu.make_async_remote_copy(
                    src_ref=_owner_window(ins[i], kinds[i], _chip_index(ox, oy)), dst_ref=dst, send_sem=send_sems.at[i, k],
                    recv_sem=recv_sems.at[i, k], device_id=(ox, oy, c), device_id_type=MESH_ID))
        return copies

    return dict(ins=list(ss), out_shape=[jax.ShapeDtypeStruct((N_CHIPS,) + shard_shape(s, k), s.dtype)
                                         for s, k in zip(ss, kinds)],
                scratch=_gather_sems(n), make=make)


def _send_halves(fs, *, name):
    n = len(fs)

    def body(*refs):
        ins, outs, send_sems, recv_sems = refs[:n], refs[n:2 * n], refs[2 * n], refs[2 * n + 1]
        x, y, c, _ = _place()
        copies = [pltpu.make_async_remote_copy(src_ref=ins[i], dst_ref=outs[i], send_sem=send_sems.at[i],
                                               recv_sem=recv_sems.at[i], device_id=(x, y, 1 - c), device_id_type=MESH_ID)
                  for i in range(n)]
        for cp in copies:
            cp.start()
        for cp in copies:
            cp.wait()

    return pl.pallas_call(
        body, name=name, in_specs=[HBM_SPEC] * n, out_specs=[HBM_SPEC] * n,
        out_shape=[jax.ShapeDtypeStruct(f.shape, f.dtype) for f in fs],
        scratch_shapes=[pltpu.SemaphoreType.DMA((n,)), pltpu.SemaphoreType.DMA((n,))],
    )(*fs)


def _adamw_halves(w, m, v, own, other, c_idx, *, name):
    shape = w.shape
    nl, r, cols = own.shape
    w4, m4, v4 = (a.reshape(nl, 2, r, cols) for a in (w, m, v))
    tr = r if r * cols * 4 <= 3 * 512 * 1024 else _tile(r, tuple(c for c in (256, 128, 64, 32, 16, 8) if c * cols <= 256 * 1024))

    def body(c_ref, w_ref, m_ref, v_ref, own_ref, other_ref, g_ref, d_ref, nm_ref, nv_ref):
        g_ = jnp.where(pl.program_id(1) == c_ref[0], own_ref[...], other_ref[...])
        g_ref[...] = g_
        d_ref[...], nm_ref[...], nv_ref[...] = _adamw_math(w_ref[...], g_, m_ref[...], v_ref[...])

    full = pl.BlockSpec((None, None, tr, cols), lambda l, h, i, c_ref: (l, h, i, 0))
    half = pl.BlockSpec((None, tr, cols), lambda l, h, i, c_ref: (l, i, 0))
    outs = pl.pallas_call(
        body, name=name,
        grid_spec=pltpu.PrefetchScalarGridSpec(num_scalar_prefetch=1, grid=(nl, 2, r // tr),
                                               in_specs=[full, full, full, half, half], out_specs=[full] * 4),
        out_shape=[jax.ShapeDtypeStruct((nl, 2, r, cols), F32)] * 4,
        compiler_params=_params(dimension_semantics=("parallel", "parallel", "parallel")),
    )(c_idx, w4, m4, v4, own, other)
    return tuple(o.reshape(shape) for o in outs)


SHARD_AXIS = {"rel_bias": None, "ln_g": 2, "ln_b": 2, "ffn_w13": 3, "ffn_w2": 2, "w_in": 2, "conv_w": 2, "a_log": None,
              "dt_bias": None, "dn_norm_g": None, "sinks": None, "w_branch_a": 1, "w_branch_b": 1, "w_out": 1}
WEIGHT_NAMES = tuple(SHARD_AXIS)
SMALL_NAMES = tuple(n for n in WEIGHT_NAMES if n not in MATRIX_NAMES)
PACK_LANES = 1024


def _unshard(gathered, axis):
    g = jnp.moveaxis(gathered, 0, axis)
    return g.reshape(g.shape[:axis] + (g.shape[axis] * g.shape[axis + 1],) + g.shape[axis + 2:])


class _GradReducer:
    def __init__(self, c_idx):
        self.c_idx = c_idx
        self.swapping = None
        self.exchanging = None
        self.reduced = {}

    @staticmethod
    def _views(grads):
        out = []
        for n in MATRIX_NAMES:
            for g in (grads[n] if n in ("ffn_w13", "ffn_w2") else [grads[n]]):
                if n == "ffn_w13":
                    out.append((n, g, "cols"))
                elif n == "w_in":
                    out.append((n, g, "rows"))
                else:
                    out.append((n, g.reshape(N_CHIPS, 2, g.shape[0] // (2 * N_CHIPS), g.shape[1]), "rows"))
        return out

    def layer_done(self, layer, grads):
        assert self.swapping is None
        self.swapping = (layer, self._views(grads))

    def job_a(self):
        if self.swapping is None:
            return None
        _, views = self.swapping
        return _swap_job([g for _, g, _ in views], [k for _, _, k in views])

    def done_a(self, got):
        layer, views = self.swapping
        self.swapping = None
        assert self.exchanging is None
        ss = [_pair_sum(g, r, k, self.c_idx, name=f"rs_pair_sum_l{layer}_{i}")
              for i, ((_, g, k), r) in enumerate(zip(views, got))]
        self.exchanging = (layer, views, ss)

    def job_b(self):
        if self.exchanging is None:
            return None
        _, views, ss = self.exchanging
        return _exchange_job(ss, [k for _, _, k in views])

    def done_b(self, ex):
        layer, views, _ = self.exchanging
        self.exchanging = None
        red = {}
        for i, ((n, _, _), e) in enumerate(zip(views, ex)):
            red.setdefault(n, []).append(_sum_slots(e[None], name=f"rs_chip_sum_l{layer}_{i}")[0])
        self.reduced[layer] = red

    def flush(self):
        if self.swapping is not None:
            self.done_a(_run_job(self.job_a(), name="rs_swap_halves_last"))
        if self.exchanging is not None:
            self.done_b(_run_job(self.job_b(), name="rs_exchange_chips_last"))

    def result(self):
        self.flush()
        own = [jnp.stack([f for layer in sorted(self.reduced) for f in self.reduced[layer][n]]) for n in MATRIX_NAMES]
        other = _send_halves(own, name="rs_send_halves")
        return {n: (a, b) for n, a, b in zip(MATRIX_NAMES, own, other)}


def _reduce_small(grads):
    flat = [grads[n].astype(F32).reshape(-1) for n in SMALL_NAMES]
    total = sum(f.shape[0] for f in flat)
    rows = -(-total // (16 * PACK_LANES)) * 16
    vec = jnp.concatenate(flat + [jnp.zeros((rows * PACK_LANES - total,), F32)]).reshape(rows, PACK_LANES)
    s = _sum_slots(_allgather_devices(vec, name="small_allgather")[None], name="small_sum").reshape(-1)
    out, o = {}, 0
    for n, f in zip(SMALL_NAMES, flat):
        out[n] = s[o:o + f.shape[0]].reshape(grads[n].shape)
        o += f.shape[0]
    return out


def kernel(x, rel_bias, ln_g, ln_b, ffn_w13, ffn_w2, w_in, conv_w, a_log, dt_bias, dn_norm_g, sinks, w_branch_a, w_branch_b, w_out, loss_target, m_rel_bias, m_ln_g, m_ln_b, m_ffn_w13, m_ffn_w2, m_w_in, m_conv_w, m_a_log, m_dt_bias, m_dn_norm_g, m_sinks, m_w_branch_a, m_w_branch_b, m_w_out, v_rel_bias, v_ln_g, v_ln_b, v_ffn_w13, v_ffn_w2, v_w_in, v_conv_w, v_a_log, v_dt_bias, v_dn_norm_g, v_sinks, v_w_branch_a, v_w_branch_b, v_w_out):
    w = dict(rel_bias=rel_bias, ln_g=ln_g, ln_b=ln_b, ffn_w13=ffn_w13, ffn_w2=ffn_w2, w_in=w_in, conv_w=conv_w,
             a_log=a_log, dt_bias=dt_bias, dn_norm_g=dn_norm_g, sinks=sinks, w_branch_a=w_branch_a,
             w_branch_b=w_branch_b, w_out=w_out)
    m = dict(rel_bias=m_rel_bias, ln_g=m_ln_g, ln_b=m_ln_b, ffn_w13=m_ffn_w13, ffn_w2=m_ffn_w2, w_in=m_w_in,
             conv_w=m_conv_w, a_log=m_a_log, dt_bias=m_dt_bias, dn_norm_g=m_dn_norm_g, sinks=m_sinks,
             w_branch_a=m_w_branch_a, w_branch_b=m_w_branch_b, w_out=m_w_out)
    v = dict(rel_bias=v_rel_bias, ln_g=v_ln_g, ln_b=v_ln_b, ffn_w13=v_ffn_w13, ffn_w2=v_ffn_w2, w_in=v_w_in,
             conv_w=v_conv_w, a_log=v_a_log, dt_bias=v_dt_bias, dn_norm_g=v_dn_norm_g, sinks=v_sinks,
             w_branch_a=v_w_branch_a, w_branch_b=v_w_branch_b, w_out=v_w_out)

    depth = w_in.shape[0]
    sharded = [n for n in WEIGHT_NAMES if SHARD_AXIS[n] is not None]

    def shards_of(i):
        out = {}
        for n in sharded:
            s = w[n][i].astype(MXU_DTYPE) if n in MATRIX_NAMES else w[n][i]
            if n in ("ffn_w13", "ffn_w2"):
                out[n + "_0"], out[n + "_1"] = s[0], s[1]
            else:
                out[n] = s
        return out

    def assemble(i, gathered):
        lw = {}
        for n, g in gathered.items():
            if n[:-2] in ("ffn_w13", "ffn_w2"):
                lw.setdefault(n[:-2], {})[int(n[-1])] = _unshard(g, SHARD_AXIS[n[:-2]] - 2)
            elif n == "w_in":
                lw["w_in_p"] = _repack_w_in(_unshard(g, SHARD_AXIS[n] - 1))
            else:
                lw[n] = _unshard(g, SHARD_AXIS[n] - 1)
        return lw

    layer_wts = [{n: w[n][i] for n in ("a_log", "dt_bias", "dn_norm_g", "sinks")} for i in range(depth)]
    shards = [shards_of(i) for i in range(depth)]
    first = ("ffn_w13_0", "ffn_w2_0", "ln_g", "ln_b")
    got = _run_job(_gather_job([shards[0].pop(n) for n in first]), name="weights_allgather_first")
    layer_wts[0].update(assemble(0, dict(zip(first, got))))
    c_idx = lax.axis_index("c").astype(jnp.int32).reshape(1)
    reducer = _GradReducer(c_idx)
    loss_part, grad_x, grads = _local_step(x, loss_target, rel_bias, layer_wts, side_shards=shards,
                                           side_assemble=assemble, reducer=reducer)
    loss = lax.psum(loss_part, ("x", "y", "c"))

    halves = reducer.result()
    chip = _chip_index(lax.axis_index("x"), lax.axis_index("y"))
    small = _reduce_small(grads)
    outs = {}
    for n in WEIGHT_NAMES:
        if n in MATRIX_NAMES:
            outs[n] = _adamw_halves(w[n], m[n], v[n], *halves[n], c_idx, name="adamw_" + n)
        else:
            axis = SHARD_AXIS[n]
            g = small[n]
            if axis is not None:
                g = lax.dynamic_slice_in_dim(g, chip * w[n].shape[axis], w[n].shape[axis], axis)
            outs[n] = (g,) + _adamw(w[n], g, m[n], v[n], name="adamw_" + n)
    return (loss, grad_x, *[outs[n][0] for n in WEIGHT_NAMES], *[outs[n][1] for n in WEIGHT_NAMES],
            *[outs[n][2] for n in WEIGHT_NAMES], *[outs[n][3] for n in WEIGHT_NAMES])
```

```python
import functools
import math

import numpy as np
import jax
import jax.numpy as jnp
from jax import lax
from jax.experimental import pallas as pl
from jax.experimental.pallas import tpu as pltpu

F32 = jnp.float32
BF16 = jnp.bfloat16
MXU_DTYPE = BF16
HIGHEST = lax.Precision.HIGHEST

D_MODEL = 1024
N_HEADS_A = 16
N_KV_A = 4
HEAD_DIM_A = 64
GROUP_A = N_HEADS_A // N_KV_A
WINDOW = 128
N_HEADS_B = 8
KEY_DIM_B = 128
VAL_DIM_B = 128
CONV_K = 4
CHUNK = 64
D_FF = 2816
NUM_BUCKETS = 32
MAX_DISTANCE = 128
DEPTH = 4
DN_ALPHA = (2 * DEPTH) ** 0.25
LN_EPS = 1e-5
NORM_EPS = 1e-6
NEG_INF = -1e30

Q_A = N_HEADS_A * HEAD_DIM_A
KV_W = N_KV_A * HEAD_DIM_A
QK_B = N_HEADS_B * KEY_DIM_B
V_B = N_HEADS_B * VAL_DIM_B
CONV_CH = 2 * QK_B + V_B
N_IN = Q_A + 2 * KV_W + CONV_CH + 2 * N_HEADS_B + V_B + 2 * D_MODEL

ADAM_LR = 0.001
ADAM_B1 = 0.9
ADAM_B2 = 0.999
ADAM_EPS = 1e-08
ADAM_WD = 0.01
ADAM_STEP = 10

HC_W = 8192
HC_Q = 0
HC_Z = 1024
HC_K = 2048
HC_V = 2304
HC_BD = 2560
HC_CONV = 3072
HC_GATE = 6144

GROUP_T = 256
DN_BWD_HEADS = 4
ROW_T = 256
VMEM_LIMIT_BYTES = 48 * 1024 * 1024


def _params(vmem=VMEM_LIMIT_BYTES, **kw):
    return pltpu.CompilerParams(vmem_limit_bytes=vmem, **kw)


def _tile(n, cands):
    for c in cands:
        if n % c == 0:
            return c
    return n


def _dot(a, b, dims=(((1,), (0,)), ((), ())), exact=False):
    if exact:
        return lax.dot_general(a.astype(F32), b.astype(F32), dims, precision=HIGHEST, preferred_element_type=F32)
    return lax.dot_general(a.astype(MXU_DTYPE), b.astype(MXU_DTYPE), dims, preferred_element_type=F32)


_NN = (((1,), (0,)), ((), ()))
_NT = (((1,), (1,)), ((), ()))
_TN = (((0,), (0,)), ((), ()))


def _sigmoid(x):
    return 1.0 / (1.0 + jnp.exp(-x))


def _silu(x):
    return x * _sigmoid(x)


def _dsilu(x):
    s = _sigmoid(x)
    return s * (1.0 + x * (1.0 - s))


def _call(body, *, name, grid, in_specs, out_specs, out_shape, scratch_shapes=(), semantics, args, side=None,
          vmem=VMEM_LIMIT_BYTES):
    in_specs, out_specs, out_shape = list(in_specs), list(out_specs), list(out_shape)
    if side is None:
        outs = pl.pallas_call(body, name=name, grid=grid, in_specs=in_specs, out_specs=out_specs, out_shape=out_shape,
                              scratch_shapes=list(scratch_shapes),
                              compiler_params=_params(vmem=vmem, dimension_semantics=semantics))(*args)
        return outs, None
    n_in, n_out, n_scr = len(in_specs), len(out_specs), len(scratch_shapes)
    s_in, s_out = len(side["ins"]), len(side["out_shape"])

    def hosted(*refs):
        main_in, side_in = refs[:n_in], refs[n_in:n_in + s_in]
        o0 = n_in + s_in
        main_out, side_out = refs[o0:o0 + n_out], refs[o0 + n_out:o0 + n_out + s_out]
        rest = refs[o0 + n_out + s_out:]
        copies = side["make"](side_in, side_out, *rest[n_scr:])
        ids = [pl.program_id(d) for d in range(len(grid))]
        first = functools.reduce(jnp.logical_and, [i == 0 for i in ids])
        last = functools.reduce(jnp.logical_and, [i == g - 1 for i, g in zip(ids, grid)])

        @pl.when(first)
        def _():
            for cp in copies:
                cp.start()

        body(*main_in, *main_out, *rest[:n_scr])

        @pl.when(last)
        def _():
            for cp in copies:
                cp.wait()

    outs = pl.pallas_call(
        hosted, name=name, grid=grid,
        in_specs=in_specs + [HBM_SPEC] * s_in, out_specs=out_specs + [HBM_SPEC] * s_out,
        out_shape=out_shape + list(side["out_shape"]),
        scratch_shapes=list(scratch_shapes) + list(side["scratch"]),
        compiler_params=_params(vmem=vmem, dimension_semantics=("arbitrary",) * len(grid)),
    )(*args, *side["ins"])
    return list(outs[:n_out]), list(outs[n_out:])


def _mm(a, b, *, ta=False, tb=False, a_halves=False, b_halves=False, out_dtype=F32, add=None, exact=False, side=None,
        name):
    if a_halves:
        m, kdim = a.shape[1], 2 * a.shape[2]
    else:
        (kdim, m) = a.shape if ta else a.shape[::-1]
    if b_halves:
        kb, n = b.shape[1], 2 * b.shape[2]
    else:
        (n, kb) = b.shape if tb else b.shape[::-1]
    assert kdim == kb, (a.shape, b.shape, ta, tb)
    if ta and not tb and not exact and add is None and side is None and kdim <= TN_WHOLE_K:
        return _mm_tn(a, b, b_halves=b_halves, out_dtype=out_dtype, name=name)
    tn = _tile(n, (1024, 1408, 512, 256, 128))
    tk = _tile(kdim, (1024, 1408, 512, 256, 128))
    nk = kdim // tk
    tm = _tile(m, (1024, 1408, 512, 256, 128) if nk > 1 else (512, 256, 128))
    nj = n // tn
    dims = (((0 if ta else 1,), (1 if tb else 0,)), ((), ()))
    has_add = add is not None

    def body(*refs):
        if has_add:
            a_ref, b_ref, add_ref, o_ref = refs[:4]
        else:
            a_ref, b_ref, o_ref = refs[:3]
            add_ref = None
        part = _dot(a_ref[...], b_ref[...], dims, exact)

        def finish(acc):
            if has_add:
                acc = acc + add_ref[...].astype(F32)
            o_ref[...] = acc.astype(out_dtype)

        if nk == 1:
            finish(part)
        else:
            acc_ref = refs[-1]
            k = pl.program_id(2)

            @pl.when(k == 0)
            def _():
                acc_ref[...] = part

            @pl.when(k > 0)
            def _():
                acc_ref[...] += part

            @pl.when(k == nk - 1)
            def _():
                finish(acc_ref[...])

    if a_halves:
        assert not ta and nk % 2 == 0
        a_spec = pl.BlockSpec((None, tm, tk), lambda i, j, k: (k // (nk // 2), i, k % (nk // 2)))
    elif ta:
        a_spec = pl.BlockSpec((tk, tm), lambda i, j, k: (k, i))
    else:
        a_spec = pl.BlockSpec((tm, tk), lambda i, j, k: (i, k))
    if b_halves:
        assert not tb and nj % 2 == 0
        b_spec = pl.BlockSpec((None, tk, tn), lambda i, j, k: (j // (nj // 2), k, j % (nj // 2)))
    elif tb:
        b_spec = pl.BlockSpec((tn, tk), lambda i, j, k: (j, k))
    else:
        b_spec = pl.BlockSpec((tk, tn), lambda i, j, k: (k, j))
    o_spec = pl.BlockSpec((tm, tn), lambda i, j, k: (i, j))
    in_specs = [a_spec, b_spec] + ([o_spec] if has_add else [])
    args = (a, b) + ((add,) if has_add else ())
    outs, side_outs = _call(
        body, name=name, grid=(m // tm, nj, nk), in_specs=in_specs, out_specs=[o_spec],
        out_shape=[jax.ShapeDtypeStruct((m, n), out_dtype)],
        scratch_shapes=[pltpu.VMEM((tm, tn), F32)] if nk > 1 else [],
        semantics=("parallel", "parallel", "arbitrary"), args=args, side=side)
    return outs[0] if side is None else (outs[0], side_outs)


TN_WHOLE_K = 8192
VMEM_LIMIT_BIG_BYTES = 56 * 1024 * 1024


def _mm_tn(a, b, *, b_halves=False, out_dtype=F32, name):
    kdim, m = a.shape
    n = 2 * b.shape[2] if b_halves else b.shape[1]
    if m <= 1024:
        tm, tn = m, _tile(n // 2 if b_halves else n, (256, 128))
    else:
        tm, tn = _tile(m, (256, 128)), _tile(n, (1024, 512, 256, 128))
    nj = n // tn

    def body(a_ref, b_ref, o_ref):
        o_ref[...] = _dot(a_ref[...], b_ref[...], _TN).astype(out_dtype)

    if b_halves:
        b_spec = pl.BlockSpec((None, kdim, tn), lambda i, j: (j // (nj // 2), 0, j % (nj // 2)))
    else:
        b_spec = pl.BlockSpec((kdim, tn), lambda i, j: (0, j))
    return pl.pallas_call(
        body, name=name, grid=(m // tm, nj),
        in_specs=[pl.BlockSpec((kdim, tm), lambda i, j: (0, i)), b_spec],
        out_specs=pl.BlockSpec((tm, tn), lambda i, j: (i, j)),
        out_shape=jax.ShapeDtypeStruct((m, n), out_dtype),
        compiler_params=pltpu.CompilerParams(vmem_limit_bytes=VMEM_LIMIT_BIG_BYTES,
                                             dimension_semantics=("parallel", "parallel")),
    )(a, b)


def _mm_w(a, w, *, side=None, name):
    m, kdim = a.shape
    n = w.shape[1]
    tm = _tile(m, (256, 128))

    def body(a_ref, w_ref, o_ref):
        o_ref[...] = _dot(a_ref[...], w_ref[...])

    outs, side_outs = _call(
        body, name=name, grid=(m // tm,),
        in_specs=[pl.BlockSpec((tm, kdim), lambda i: (i, 0)),
                  pl.BlockSpec((kdim, n), lambda i: (0, 0), pipeline_mode=pl.Buffered(1))],
        out_specs=[pl.BlockSpec((tm, n), lambda i: (i, 0))], out_shape=[jax.ShapeDtypeStruct((m, n), F32)],
        semantics=("parallel",), args=(a, w), side=side, vmem=VMEM_LIMIT_BIG_BYTES)
    return outs[0], side_outs


def _mm_wt(a, w, add, *, a_halves=False, name):
    n, kdim = w.shape
    m = a.shape[1] if a_halves else a.shape[0]
    tm = _tile(m, (512, 256, 128))
    half = kdim // 2

    def body(a_ref, w_ref, add_ref, o_ref):
        if a_halves:
            acc = _dot(a_ref[0], w_ref[:, :half], _NT) + _dot(a_ref[1], w_ref[:, half:], _NT)
        else:
            acc = _dot(a_ref[...], w_ref[...], _NT)
        o_ref[...] = acc + add_ref[...]

    a_spec = pl.BlockSpec((2, tm, half), lambda i: (0, i, 0)) if a_halves else pl.BlockSpec((tm, kdim), lambda i: (i, 0))
    row = pl.BlockSpec((tm, n), lambda i: (i, 0))
    return pl.pallas_call(
        body, name=name, grid=(m // tm,),
        in_specs=[a_spec, pl.BlockSpec((n, kdim), lambda i: (0, 0), pipeline_mode=pl.Buffered(1)), row],
        out_specs=row, out_shape=jax.ShapeDtypeStruct((m, n), F32),
        compiler_params=pltpu.CompilerParams(vmem_limit_bytes=VMEM_LIMIT_BIG_BYTES, dimension_semantics=("parallel",)),
    )(a, w, add)


def _layernorm_rows(r, g, b):
    mu = jnp.mean(r, axis=-1, keepdims=True)
    xc = r - mu
    var = jnp.mean(xc * xc, axis=-1, keepdims=True)
    return xc * lax.rsqrt(var + LN_EPS) * g + b


def _mm_res_ln(a, w, resid, g, b, *, alpha, c, side=None, name):
    m, kdim = a.shape
    n = w.shape[1]
    tm = _tile(m, (512, 256, 128))

    def body(a_ref, w_ref, x_ref, g_ref, b_ref, r_ref, y_ref, yb_ref):
        f = _dot(a_ref[...], w_ref[...])
        r = alpha * x_ref[...] + c * f
        r_ref[...] = r
        y = _layernorm_rows(r, g_ref[...], b_ref[...])
        y_ref[...] = y
        yb_ref[...] = y.astype(BF16)

    row = pl.BlockSpec((tm, n), lambda i: (i, 0))
    vec = pl.BlockSpec((1, n), lambda i: (0, 0))
    outs, side_outs = _call(
        body, name=name, grid=(m // tm,),
        in_specs=[pl.BlockSpec((tm, kdim), lambda i: (i, 0)), pl.BlockSpec((kdim, n), lambda i: (0, 0)), row, vec, vec],
        out_specs=[row, row, row],
        out_shape=[jax.ShapeDtypeStruct((m, n), F32)] * 2 + [jax.ShapeDtypeStruct((m, n), BF16)],
        semantics=("parallel",), args=(a, w, resid, g.reshape(1, n), b.reshape(1, n)), side=side)
    return tuple(outs) if side is None else (*outs, side_outs)


def _ln_bwd(dy, r, g, *, alpha, c, name):
    m, n = dy.shape
    tm = _tile(m, (512, 256, 128))

    def body(dy_ref, r_ref, g_ref, dres_ref, dbr_ref, dg_ref, db_ref):
        i = pl.program_id(0)
        dy_ = dy_ref[...]
        r_ = r_ref[...]
        mu = jnp.mean(r_, axis=-1, keepdims=True)
        xc = r_ - mu
        var = jnp.mean(xc * xc, axis=-1, keepdims=True)
        rstd = lax.rsqrt(var + LN_EPS)
        xh = xc * rstd
        dxh = dy_ * g_ref[...]
        dr = rstd * (dxh - jnp.mean(dxh, axis=-1, keepdims=True) - xh * jnp.mean(dxh * xh, axis=-1, keepdims=True))
        dres_ref[...] = alpha * dr
        dbr_ref[...] = (c * dr).astype(BF16)
        dg_p = jnp.sum(dy_ * xh, axis=0, keepdims=True)
        db_p = jnp.sum(dy_, axis=0, keepdims=True)

        @pl.when(i == 0)
        def _():
            dg_ref[...] = dg_p
            db_ref[...] = db_p

        @pl.when(i > 0)
        def _():
            dg_ref[...] += dg_p
            db_ref[...] += db_p

    row = pl.BlockSpec((tm, n), lambda i: (i, 0))
    vec = pl.BlockSpec((1, n), lambda i: (0, 0))
    return pl.pallas_call(
        body, name=name, grid=(m // tm,),
        in_specs=[row, row, vec], out_specs=[row, row, vec, vec],
        out_shape=[jax.ShapeDtypeStruct((m, n), F32), jax.ShapeDtypeStruct((m, n), BF16),
                   jax.ShapeDtypeStruct((1, n), F32), jax.ShapeDtypeStruct((1, n), F32)],
        compiler_params=_params(dimension_semantics=("arbitrary",)),
    )(dy, r, g.reshape(1, n))


FFN_TN = D_FF // 2


def _ffn_up_act(x, w13, *, side=None, name):
    m, d = x.shape
    tm = _tile(m, (512, 256, 128))
    nj = D_FF // FFN_TN

    def body(x_ref, g_ref, u_ref, o_ref):
        x_ = x_ref[...]
        o_ref[...] = (_silu(_dot(x_, g_ref[...])) * _dot(x_, u_ref[...])).astype(BF16)

    outs, side_outs = _call(
        body, name=name, grid=(nj, m // tm),
        in_specs=[pl.BlockSpec((tm, d), lambda j, i: (i, 0)), pl.BlockSpec((d, FFN_TN), lambda j, i: (0, j)),
                  pl.BlockSpec((d, FFN_TN), lambda j, i: (0, j + nj))],
        out_specs=[pl.BlockSpec((tm, FFN_TN), lambda j, i: (i, j))],
        out_shape=[jax.ShapeDtypeStruct((m, D_FF), BF16)],
        semantics=("parallel", "parallel"), args=(x, w13, w13), side=side)
    return outs[0], side_outs


def _ffn_bwd_mid(x, w13, df, w2, *, side=None, name):
    m, d = x.shape
    tm = _tile(m, (512, 256, 128))
    nj = D_FF // FFN_TN

    def body(x_ref, g_ref, u_ref, df_ref, w2_ref, o_ref):
        x_ = x_ref[...]
        g = _dot(x_, g_ref[...])
        u = _dot(x_, u_ref[...])
        da = _dot(df_ref[...], w2_ref[...], _NT)
        o_ref[0] = (da * u * _dsilu(g)).astype(BF16)
        o_ref[1] = (da * _silu(g)).astype(BF16)

    outs, side_outs = _call(
        body, name=name, grid=(nj, m // tm),
        in_specs=[pl.BlockSpec((tm, d), lambda j, i: (i, 0)), pl.BlockSpec((d, FFN_TN), lambda j, i: (0, j)),
                  pl.BlockSpec((d, FFN_TN), lambda j, i: (0, j + nj)), pl.BlockSpec((tm, d), lambda j, i: (i, 0)),
                  pl.BlockSpec((FFN_TN, d), lambda j, i: (j, 0))],
        out_specs=[pl.BlockSpec((2, tm, FFN_TN), lambda j, i: (0, i, j))],
        out_shape=[jax.ShapeDtypeStruct((2, m, D_FF), BF16)],
        semantics=("parallel", "parallel"), args=(x, w13, w13, df, w2), side=side)
    return outs[0], side_outs


def _t5_bucket_table():
    r = np.arange(WINDOW)[:, None]
    j = np.arange(2 * WINDOW)[None, :]
    rel = r + WINDOW - j
    n = np.maximum(rel, 0)
    max_exact = NUM_BUCKETS // 2
    nf = np.maximum(n, 1).astype(np.float32)
    large = max_exact + (np.log(nf / np.float32(max_exact)) / np.float32(math.log(MAX_DISTANCE / max_exact))
                         * np.float32(NUM_BUCKETS - max_exact)).astype(np.int32)
    large = np.minimum(large, NUM_BUCKETS - 1)
    bucket = np.where(n < max_exact, n, large)
    in_band = (rel >= 0) & (rel < WINDOW)
    return bucket.astype(np.int32), in_band


def _bucket_onehot():
    bucket, _ = _t5_bucket_table()
    oh = np.zeros((WINDOW * 2 * WINDOW, 128), np.float32)
    oh[np.arange(oh.shape[0]), bucket.reshape(-1)] = 1.0
    return oh


def _stack_heads(x, g):
    hd = HEAD_DIM_A
    return jnp.concatenate([x[:, (GROUP_A * g + h) * hd:(GROUP_A * g + h + 1) * hd] for h in range(GROUP_A)], axis=0)


def _unstack_heads(x):
    return jnp.concatenate([x[h * WINDOW:(h + 1) * WINDOW] for h in range(GROUP_A)], axis=1)


def _attn_probs(q_ref, kp_ref, ko_ref, vp_ref, vo_ref, bias_ref, sink_ref, first_block):
    hd = HEAD_DIM_A
    groups = range(N_KV_A)
    q = q_ref[...]
    qs = [_stack_heads(q, g) * (hd ** -0.5) for g in groups]
    k2 = [jnp.concatenate([kp_ref[:, g * hd:(g + 1) * hd], ko_ref[:, g * hd:(g + 1) * hd]], axis=0) for g in groups]
    v2 = [jnp.concatenate([vp_ref[:, g * hd:(g + 1) * hd], vo_ref[:, g * hd:(g + 1) * hd]], axis=0) for g in groups]
    rr = lax.broadcasted_iota(jnp.int32, (GROUP_A * WINDOW, 2 * WINDOW), 0) % WINDOW
    jj = lax.broadcasted_iota(jnp.int32, (GROUP_A * WINDOW, 2 * WINDOW), 1)
    rel = rr + WINDOW - jj
    valid = (rel >= 0) & (rel < WINDOW) & (jnp.logical_not(first_block) | (jj >= WINDOW))
    s = [_dot(qs[g], k2[g], _NT) for g in groups]
    s = [jnp.where(valid, s[g] + bias_ref[GROUP_A * g:GROUP_A * (g + 1)].reshape(GROUP_A * WINDOW, 2 * WINDOW), NEG_INF)
         for g in groups]
    sk = [jnp.concatenate([jnp.broadcast_to(sink_ref[0:1, GROUP_A * g + h:GROUP_A * g + h + 1], (WINDOW, 1))
                           for h in range(GROUP_A)], axis=0) for g in groups]
    mx = [jnp.maximum(jnp.max(s[g], axis=-1, keepdims=True), sk[g]) for g in groups]
    p = [jnp.exp(s[g] - mx[g]) for g in groups]
    ps = [jnp.exp(sk[g] - mx[g]) for g in groups]
    den = [jnp.sum(p[g], axis=-1, keepdims=True) + ps[g] for g in groups]
    return qs, k2, v2, [p[g] / den[g] for g in groups], [ps[g] / den[g] for g in groups]


def _attn_specs(nb):
    def prev(b, i):
        return (b * nb + jnp.maximum(i - 1, 0))

    q_spec = pl.BlockSpec((WINDOW, Q_A), lambda b, i: (b * nb + i, HC_Q // Q_A))
    kp_spec = pl.BlockSpec((WINDOW, KV_W), lambda b, i: (prev(b, i), HC_K // KV_W))
    ko_spec = pl.BlockSpec((WINDOW, KV_W), lambda b, i: (b * nb + i, HC_K // KV_W))
    vp_spec = pl.BlockSpec((WINDOW, KV_W), lambda b, i: (prev(b, i), HC_V // KV_W))
    vo_spec = pl.BlockSpec((WINDOW, KV_W), lambda b, i: (b * nb + i, HC_V // KV_W))
    bias_spec = pl.BlockSpec((N_HEADS_A, WINDOW, 2 * WINDOW), lambda b, i: (0, 0, 0))
    sink_spec = pl.BlockSpec((1, N_HEADS_A), lambda b, i: (0, 0))
    return [q_spec, kp_spec, ko_spec, vp_spec, vo_spec, bias_spec, sink_spec]


def _attn_fwd(hcat, bias, sink, nbatch, *, side=None, name):
    t = hcat.shape[0]
    nb = t // nbatch // WINDOW

    def body(q_ref, kp_ref, ko_ref, vp_ref, vo_ref, bias_ref, sink_ref, o_ref):
        first = pl.program_id(1) == 0
        _, _, v2, p, _ = _attn_probs(q_ref, kp_ref, ko_ref, vp_ref, vo_ref, bias_ref, sink_ref, first)
        o = [_dot(p[g], v2[g]) for g in range(N_KV_A)]
        o_ref[...] = jnp.concatenate([_unstack_heads(og) for og in o], axis=1).astype(BF16)

    outs, side_outs = _call(
        body, name=name, grid=(nbatch, nb), in_specs=_attn_specs(nb),
        out_specs=[pl.BlockSpec((WINDOW, Q_A), lambda b, i: (b * nb + i, 0))],
        out_shape=[jax.ShapeDtypeStruct((t, Q_A), BF16)], semantics=("parallel", "arbitrary"),
        args=(hcat, hcat, hcat, hcat, hcat, bias, sink.reshape(1, N_HEADS_A)), side=side)
    return outs[0], side_outs


def _attn_bwd(hcat, bias, sink, do, nbatch, *, name):
    t = hcat.shape[0]
    nb = t // nbatch // WINDOW
    hd = HEAD_DIM_A

    def body(q_ref, kp_ref, ko_ref, vp_ref, vo_ref, bias_ref, sink_ref, do_ref,
             dq_ref, dk_ref, dv_ref, dbias_ref, dsink_ref, ck_ref, cv_ref):
        b = pl.program_id(0)
        j = pl.program_id(1)
        first = j == nb - 1

        @pl.when((b == 0) & (j == 0))
        def _():
            dbias_ref[...] = jnp.zeros_like(dbias_ref)
            dsink_ref[...] = jnp.zeros_like(dsink_ref)

        @pl.when(j == 0)
        def _():
            ck_ref[...] = jnp.zeros_like(ck_ref)
            cv_ref[...] = jnp.zeros_like(cv_ref)

        do_ = do_ref[...]
        groups = range(N_KV_A)
        lane = lax.broadcasted_iota(jnp.int32, (1, N_HEADS_A), 1)
        qs, k2, v2, p, ps = _attn_probs(q_ref, kp_ref, ko_ref, vp_ref, vo_ref, bias_ref, sink_ref, first)
        dos = [_stack_heads(do_, g) for g in groups]
        dv2 = [_dot(p[g], dos[g], _TN) for g in groups]
        dp = [_dot(dos[g], v2[g], _NT) for g in groups]
        delta = [jnp.sum(p[g] * dp[g], axis=-1, keepdims=True) for g in groups]
        ds = [p[g] * (dp[g] - delta[g]) for g in groups]
        dqs = [_dot(ds[g], k2[g]) * (hd ** -0.5) for g in groups]
        dk2 = [_dot(ds[g], qs[g], _TN) for g in groups]
        dsink = jnp.zeros((1, N_HEADS_A), F32)
        for g in groups:
            dsk = -(ps[g] * delta[g])
            for h in range(GROUP_A):
                tot = jnp.sum(dsk[h * WINDOW:(h + 1) * WINDOW], axis=0, keepdims=True)
                dsink = dsink + jnp.where(lane == GROUP_A * g + h, tot, 0.0)
            dbias_ref[GROUP_A * g:GROUP_A * (g + 1)] += ds[g].reshape(GROUP_A, WINDOW, 2 * WINDOW)
        dq_ref[...] = jnp.concatenate([_unstack_heads(d) for d in dqs], axis=1).astype(BF16)
        dk_ref[...] = (jnp.concatenate([d[WINDOW:] for d in dk2], axis=1) + ck_ref[...]).astype(BF16)
        dv_ref[...] = (jnp.concatenate([d[WINDOW:] for d in dv2], axis=1) + cv_ref[...]).astype(BF16)
        ck_ref[...] = jnp.concatenate([d[:WINDOW] for d in dk2], axis=1)
        cv_ref[...] = jnp.concatenate([d[:WINDOW] for d in dv2], axis=1)
        dsink_ref[...] += dsink

    def rev(spec):
        return pl.BlockSpec(spec.block_shape, lambda b, j, f=spec.index_map: f(b, nb - 1 - j))

    in_specs = [rev(s) for s in _attn_specs(nb)[:5]] + _attn_specs(nb)[5:]
    in_specs.append(pl.BlockSpec((WINDOW, Q_A), lambda b, j: (b * nb + nb - 1 - j, 0)))
    return pl.pallas_call(
        body, name=name, grid=(nbatch, nb),
        in_specs=in_specs,
        out_specs=[pl.BlockSpec((WINDOW, Q_A), lambda b, j: (b * nb + nb - 1 - j, 0)),
                   pl.BlockSpec((WINDOW, KV_W), lambda b, j: (b * nb + nb - 1 - j, 0)),
                   pl.BlockSpec((WINDOW, KV_W), lambda b, j: (b * nb + nb - 1 - j, 0)),
                   pl.BlockSpec((N_HEADS_A, WINDOW, 2 * WINDOW), lambda b, j: (0, 0, 0)),
                   pl.BlockSpec((1, N_HEADS_A), lambda b, j: (0, 0))],
        out_shape=[jax.ShapeDtypeStruct((t, Q_A), BF16), jax.ShapeDtypeStruct((t, KV_W), BF16),
                   jax.ShapeDtypeStruct((t, KV_W), BF16),
                   jax.ShapeDtypeStruct((N_HEADS_A, WINDOW, 2 * WINDOW), F32),
                   jax.ShapeDtypeStruct((1, N_HEADS_A), F32)],
        scratch_shapes=[pltpu.VMEM((WINDOW, KV_W), F32), pltpu.VMEM((WINDOW, KV_W), F32)],
        compiler_params=_params(dimension_semantics=("arbitrary", "arbitrary")),
    )(hcat, hcat, hcat, hcat, hcat, bias, sink.reshape(1, N_HEADS_A), do)


def _shift_down(x, halo8, s):
    if s == 0:
        return x
    rolled = pltpu.roll(x, s, axis=0)
    row8 = lax.broadcasted_iota(jnp.int32, halo8.shape, 0)
    top = jnp.where(row8 < s, pltpu.roll(halo8, s, axis=0), rolled[0:8])
    return top if x.shape[0] == 8 else jnp.concatenate([top, rolled[8:]], axis=0)


def _shift_up(x, halo8, s):
    if s == 0:
        return x
    n = x.shape[0]
    rolled = pltpu.roll(x, n - s, axis=0)
    row8 = lax.broadcasted_iota(jnp.int32, halo8.shape, 0)
    bottom = jnp.where(row8 >= 8 - s, pltpu.roll(halo8, 8 - s, axis=0), rolled[n - 8:n])
    return jnp.concatenate([rolled[:n - 8], bottom], axis=0)


def _l2n(x, scale):
    r = lax.rsqrt(jnp.sum(x * x, axis=-1, keepdims=True) + NORM_EPS)
    return x * (r * scale)


def _conv_prep(hcat, conv_w, nbatch, *, side=None, name):
    t = hcat.shape[0]
    nt = t // nbatch // ROW_T
    cb = HC_CONV // CONV_CH

    def body(u_ref, halo_ref, w_ref, q_ref, k_ref, v_ref):
        i = pl.program_id(1)
        s = _silu(_conv_rows(u_ref[...], jnp.where(i == 0, 0.0, halo_ref[...]), w_ref))
        for h in range(N_HEADS_B):
            lo, hi = h * KEY_DIM_B, (h + 1) * KEY_DIM_B
            q_ref[:, lo:hi] = _l2n(s[:, lo:hi], KEY_DIM_B ** -0.5)
            k_ref[:, lo:hi] = _l2n(s[:, QK_B + lo:QK_B + hi], 1.0)
        v_ref[...] = s[:, 2 * QK_B:]

    row = lambda w: pl.BlockSpec((ROW_T, w), lambda b, i: (b * nt + i, 0))
    outs, side_outs = _call(
        body, name=name, grid=(nbatch, nt),
        in_specs=[pl.BlockSpec((ROW_T, CONV_CH), lambda b, i: (b * nt + i, cb)),
                  pl.BlockSpec((8, CONV_CH), lambda b, i: (jnp.maximum((b * nt + i) * (ROW_T // 8) - 1, 0), cb)),
                  pl.BlockSpec((CONV_K, CONV_CH), lambda b, i: (0, 0))],
        out_specs=[row(QK_B), row(QK_B), row(V_B)],
        out_shape=[jax.ShapeDtypeStruct((t, QK_B), F32)] * 3,
        semantics=("parallel", "parallel"), args=(hcat, hcat, conv_w), side=side)
    return tuple(outs), side_outs


def _conv_rows(u, halo8, w_ref):
    c = jnp.zeros_like(u)
    for j in range(CONV_K):
        c = c + w_ref[j:j + 1, :] * _shift_down(u, halo8, CONV_K - 1 - j)
    return c


def _conv_prep_pointwise_bwd(c, dq, dk, dv):
    def l2n_bwd(x, dy, scale):
        r = lax.rsqrt(jnp.sum(x * x, axis=-1, keepdims=True) + NORM_EPS)
        return scale * (r * dy - x * (r * r * r) * jnp.sum(x * dy, axis=-1, keepdims=True))

    sig = _sigmoid(c)
    s = c * sig
    ds = sig * (1.0 + c * (1.0 - sig))
    parts_q, parts_k = [], []
    for h in range(N_HEADS_B):
        lo, hi = h * KEY_DIM_B, (h + 1) * KEY_DIM_B
        parts_q.append(l2n_bwd(s[:, lo:hi], dq[:, lo:hi], KEY_DIM_B ** -0.5) * ds[:, lo:hi])
        parts_k.append(l2n_bwd(s[:, QK_B + lo:QK_B + hi], dk[:, lo:hi], 1.0) * ds[:, QK_B + lo:QK_B + hi])
    return jnp.concatenate(parts_q + parts_k + [dv * ds[:, 2 * QK_B:]], axis=1)


def _conv_bwd(dq, dk, dv, hcat, conv_w, nbatch, *, name):
    t = dq.shape[0]
    nt = t // nbatch // ROW_T
    cb = HC_CONV // CONV_CH
    last_blk = t // 8 - 1

    def body(dq_ref, dk_ref, dv_ref, dqn_ref, dkn_ref, dvn_ref, u_ref, uprev_ref, unext_ref, w_ref, du_ref, dw_ref):
        b = pl.program_id(0)
        i = pl.program_id(1)
        u = u_ref[...]
        uprev = jnp.where(i == 0, 0.0, uprev_ref[...])
        shifted = [_shift_down(u, uprev, CONV_K - 1 - j) for j in range(CONV_K)]
        c = jnp.zeros_like(u)
        for j in range(CONV_K):
            c = c + w_ref[j:j + 1, :] * shifted[j]
        dc_ = _conv_prep_pointwise_bwd(c, dq_ref[...], dk_ref[...], dv_ref[...])
        c_next = _conv_rows(unext_ref[...], u[ROW_T - 8:], w_ref)
        dnext = _conv_prep_pointwise_bwd(c_next, dqn_ref[...], dkn_ref[...], dvn_ref[...])
        dnext = jnp.where(i == nt - 1, 0.0, dnext)
        du = jnp.zeros_like(dc_)
        rows = []
        for j in range(CONV_K):
            du = du + w_ref[j:j + 1, :] * _shift_up(dc_, dnext, CONV_K - 1 - j)
            rows.append(jnp.sum(dc_ * shifted[j], axis=0, keepdims=True))
        du_ref[...] = du.astype(BF16)
        dw_p = jnp.concatenate(rows + [jnp.zeros((8 - CONV_K, CONV_CH), F32)], axis=0)

        @pl.when((b == 0) & (i == 0))
        def _():
            dw_ref[...] = dw_p

        @pl.when((b > 0) | (i > 0))
        def _():
            dw_ref[...] += dw_p

    def nxt(b, i):
        return jnp.minimum((b * nt + i + 1) * (ROW_T // 8), last_blk)

    row = lambda wd: pl.BlockSpec((ROW_T, wd), lambda b, i: (b * nt + i, 0))
    halo = lambda wd: pl.BlockSpec((8, wd), lambda b, i: (nxt(b, i), 0))
    return pl.pallas_call(
        body, name=name, grid=(nbatch, nt),
        in_specs=[row(QK_B), row(QK_B), row(V_B), halo(QK_B), halo(QK_B), halo(V_B),
                  pl.BlockSpec((ROW_T, CONV_CH), lambda b, i: (b * nt + i, cb)),
                  pl.BlockSpec((8, CONV_CH), lambda b, i: (jnp.maximum((b * nt + i) * (ROW_T // 8) - 1, 0), cb)),
                  pl.BlockSpec((8, CONV_CH), lambda b, i: (nxt(b, i), cb)),
                  pl.BlockSpec((CONV_K, CONV_CH), lambda b, i: (0, 0))],
        out_specs=[pl.BlockSpec((ROW_T, CONV_CH), lambda b, i: (b * nt + i, 0)),
                   pl.BlockSpec((8, CONV_CH), lambda b, i: (0, 0))],
        out_shape=[jax.ShapeDtypeStruct((t, CONV_CH), BF16), jax.ShapeDtypeStruct((8, CONV_CH), F32)],
        compiler_params=_params(dimension_semantics=("arbitrary", "arbitrary")),
    )(dq, dk, dv, dq, dk, dv, hcat, hcat, hcat, conv_w)


def _softplus(x):
    return jnp.maximum(x, 0.0) + jnp.log(1.0 + jnp.exp(-jnp.abs(x)))


def _gates(hcat, a_row, dt_row, *, name):
    t = hcat.shape[0]

    def body(bd_ref, a_ref, dt_ref, gb_ref, bb_ref):
        bd = bd_ref[...]
        beta = _sigmoid(bd)
        g = -jnp.exp(a_ref[...]) * _softplus(bd + dt_ref[...])
        for h in range(N_HEADS_B):
            lo, hi = h * VAL_DIM_B, (h + 1) * VAL_DIM_B
            bb_ref[:, lo:hi] = jnp.broadcast_to(beta[:, h:h + 1], (ROW_T, VAL_DIM_B))
            gb_ref[:, lo:hi] = jnp.broadcast_to(g[:, N_HEADS_B + h:N_HEADS_B + h + 1], (ROW_T, VAL_DIM_B))

    vec = pl.BlockSpec((1, 128), lambda i: (0, 0))
    row = pl.BlockSpec((ROW_T, V_B), lambda i: (i, 0))
    return pl.pallas_call(
        body, name=name, grid=(t // ROW_T,),
        in_specs=[pl.BlockSpec((ROW_T, 128), lambda i: (i, HC_BD // 128)), vec, vec],
        out_specs=[row, row], out_shape=[jax.ShapeDtypeStruct((t, V_B), F32)] * 2,
        compiler_params=_params(dimension_semantics=("parallel",)),
    )(hcat, a_row, dt_row)


def _gates_bwd(dgb, dbb, hcat, a_row, dt_row, *, name):
    t = hcat.shape[0]

    def body(dgb_ref, dbb_ref, bd_ref, a_ref, dt_ref, dbd_ref, da_ref, ddt_ref):
        i = pl.program_id(0)
        bd = bd_ref[...]
        beta = _sigmoid(bd)
        ea = jnp.exp(a_ref[...])
        x = bd + dt_ref[...]
        g = -ea * _softplus(x)
        lane = lax.broadcasted_iota(jnp.int32, (ROW_T, 128), 1)
        dbeta = jnp.zeros((ROW_T, 128), F32)
        dg = jnp.zeros((ROW_T, 128), F32)
        for h in range(N_HEADS_B):
            lo, hi = h * VAL_DIM_B, (h + 1) * VAL_DIM_B
            dbeta = dbeta + jnp.where(lane == h, jnp.sum(dbb_ref[:, lo:hi], axis=-1, keepdims=True), 0.0)
            dg = dg + jnp.where(lane == N_HEADS_B + h, jnp.sum(dgb_ref[:, lo:hi], axis=-1, keepdims=True), 0.0)
        ddt_raw = dg * (-ea) * _sigmoid(x)
        dbd_ref[...] = (dbeta * beta * (1.0 - beta) + ddt_raw).astype(BF16)
        da_p = jnp.sum(dg * g, axis=0, keepdims=True)
        ddt_p = jnp.sum(ddt_raw, axis=0, keepdims=True)

        @pl.when(i == 0)
        def _():
            da_ref[...] = da_p
            ddt_ref[...] = ddt_p

        @pl.when(i > 0)
        def _():
            da_ref[...] += da_p
            ddt_ref[...] += ddt_p

    vec = pl.BlockSpec((1, 128), lambda i: (0, 0))
    row = pl.BlockSpec((ROW_T, V_B), lambda i: (i, 0))
    return pl.pallas_call(
        body, name=name, grid=(t // ROW_T,),
        in_specs=[row, row, pl.BlockSpec((ROW_T, 128), lambda i: (i, HC_BD // 128)), vec, vec],
        out_specs=[pl.BlockSpec((ROW_T, 128), lambda i: (i, 0)), vec, vec],
        out_shape=[jax.ShapeDtypeStruct((t, 128), BF16), jax.ShapeDtypeStruct((1, 128), F32),
                   jax.ShapeDtypeStruct((1, 128), F32)],
        compiler_params=_params(dimension_semantics=("arbitrary",)),
    )(dgb, dbb, hcat, a_row, dt_row)


def _group_masks():
    r = lax.broadcasted_iota(jnp.int32, (GROUP_T, GROUP_T), 0)
    c = lax.broadcasted_iota(jnp.int32, (GROUP_T, GROUP_T), 1)
    same = (r // CHUNK) == (c // CHUNK)
    return same, same & (r >= c), same & (r > c)


def _split2(a):
    hi = a.astype(MXU_DTYPE)
    return hi, (a - hi.astype(F32)).astype(MXU_DTYPE)


def _dot3(a2, b2, dims=_NN):
    (ah, al), (bh, bl) = a2, b2
    d = functools.partial(lax.dot_general, dimension_numbers=dims, preferred_element_type=F32)
    return d(ah, bh) + (d(ah, bl) + d(al, bh))


def _inv_unit_lower(lows):
    shape = lows[0].shape
    eye = (lax.broadcasted_iota(jnp.int32, shape, 0) == lax.broadcasted_iota(jnp.int32, shape, 1)).astype(F32)
    p2 = [_split2(-low) for low in lows]
    ts = [eye - low for low in lows]
    for _ in range(int(math.log2(CHUNK)) - 1):
        p2 = [_split2(_dot3(p, p)) for p in p2]
        ts = [t + _dot3(_split2(t), p) for t, p in zip(ts, p2)]
    return ts


@jax.custom_vjp
def _inv_saved(low, t):
    return t


def _inv_saved_fwd(low, t):
    return t, t


def _inv_saved_bwd(t, dt):
    t2 = _split2(t)
    return -_dot3(t2, _split2(_dot3(_split2(dt), t2, _NT)), _TN), jnp.zeros_like(t)


_inv_saved.defvjp(_inv_saved_fwd, _inv_saved_bwd)


def _mask_dot(mask, x, dims):
    m = mask.astype(MXU_DTYPE)
    hi = x.astype(MXU_DTYPE)
    r1 = x - hi.astype(F32)
    mid = r1.astype(MXU_DTYPE)
    lo = (r1 - mid.astype(F32)).astype(MXU_DTYPE)
    d = functools.partial(lax.dot_general, dimension_numbers=dims, preferred_element_type=F32)
    return d(m, hi) + (d(m, mid) + d(m, lo))


@jax.custom_vjp
def _chunk_sums(gb):
    same, causal, _ = _group_masks()
    return _mask_dot(causal, gb, _NN), _mask_dot(same, gb, _NN)


def _chunk_sums_fwd(gb):
    return _chunk_sums(gb), None


def _chunk_sums_bwd(_, cot):
    same, causal, _ = _group_masks()
    return (_mask_dot(causal, cot[0], _TN) + _mask_dot(same, cot[1], _TN),)


_chunk_sums.defvjp(_chunk_sums_fwd, _chunk_sums_bwd)


def _fold_blocks(m):
    return m[:, 0:CHUNK] + m[:, CHUNK:2 * CHUNK] + m[:, 2 * CHUNK:3 * CHUNK] + m[:, 3 * CHUNK:4 * CHUNK]


def _dn_prep_heads(q, k, v, gb, bb, tsaved=None):
    same, causal, strict = _group_masks()
    heads = range(len(q))
    sums = [_chunk_sums(gb[h]) for h in heads]
    gc = [s[0] for s in sums]
    glast = [s[1] for s in sums]
    decay = [jnp.exp(jnp.where(causal, gc[h][:, 0:1] - gc[h].T[0:1, :], NEG_INF)) for h in heads]
    kb = [k[h] * bb[h] for h in heads]
    vb = [v[h] * bb[h] for h in heads]
    lower = [jnp.where(strict, _dot(kb[h], k[h], _NT) * decay[h], 0.0) for h in heads]
    if tsaved is None:
        tinv = _inv_unit_lower(lower)
    else:
        tinv = [_inv_saved(lower[h], jnp.where(same, jnp.concatenate([tsaved[h]] * (GROUP_T // CHUNK), axis=1), 0.0))
                for h in heads]
    egc = [jnp.exp(gc[h]) for h in heads]
    u = [_dot(tinv[h], vb[h]) for h in heads]
    w = [_dot(tinv[h], kb[h] * egc[h]) for h in heads]
    a = [_fold_blocks(jnp.where(causal, _dot(q[h], k[h], _NT) * decay[h], 0.0)) for h in heads]
    k_tail = [k[h] * jnp.exp(glast[h] - gc[h]) for h in heads]
    q_dec = [q[h] * egc[h] for h in heads]
    return u, w, q_dec, k_tail, a, glast, [_fold_blocks(t) for t in tinv]


def _head_slices(ref, width):
    return [ref[:, h * width:(h + 1) * width] for h in range(N_HEADS_B)]


def _store_heads(ref, vals, width):
    for h, val in enumerate(vals):
        ref[:, h * width:(h + 1) * width] = val


def _dn_prep(q, k, v, gb, bb, *, side=None, name):
    t = q.shape[0]

    def body(q_ref, k_ref, v_ref, gb_ref, bb_ref, u_ref, w_ref, qd_ref, kt_ref, a_ref, gl_ref, ti_ref):
        outs = _dn_prep_heads(*[_head_slices(r, KEY_DIM_B) for r in (q_ref, k_ref, v_ref, gb_ref, bb_ref)])
        for ref, vals in zip((u_ref, w_ref, qd_ref, kt_ref, a_ref, gl_ref, ti_ref), outs):
            _store_heads(ref, vals, vals[0].shape[1])

    row = pl.BlockSpec((GROUP_T, V_B), lambda i: (i, 0))
    arow = pl.BlockSpec((GROUP_T, N_HEADS_B * CHUNK), lambda i: (i, 0))
    big = jax.ShapeDtypeStruct((t, V_B), F32)
    small = jax.ShapeDtypeStruct((t, N_HEADS_B * CHUNK), F32)
    outs, side_outs = _call(
        body, name=name, grid=(t // GROUP_T,), in_specs=[row] * 5, out_specs=[row, row, row, row, arow, row, arow],
        out_shape=[big, big, big, big, small, big, small], semantics=("parallel",), args=(q, k, v, gb, bb), side=side)
    return tuple(outs), side_outs


def _dn_prep_bwd(q, k, v, gb, bb, ti, du, dw, dqd, dkt, da, dgl, *, side=None, name):
    t = q.shape[0]

    def body(q_ref, k_ref, v_ref, gb_ref, bb_ref, ti_ref, du_ref, dw_ref, dqd_ref, dkt_ref, da_ref, dgl_ref,
             dq_ref, dk_ref, dv_ref, dgb_ref, dbb_ref):
        for lo in range(0, N_HEADS_B, DN_BWD_HEADS):
            grp = slice(lo, lo + DN_BWD_HEADS)
            tsaved = _head_slices(ti_ref, CHUNK)[grp]
            _, vjp = jax.vjp(lambda *a, ts=tsaved: _dn_prep_heads(*a, tsaved=ts)[:6],
                             *[_head_slices(r, KEY_DIM_B)[grp] for r in (q_ref, k_ref, v_ref, gb_ref, bb_ref)])
            cot = tuple(_head_slices(r, CHUNK if r is da_ref else KEY_DIM_B)[grp]
                        for r in (du_ref, dw_ref, dqd_ref, dkt_ref, da_ref, dgl_ref))
            for ref, vals in zip((dq_ref, dk_ref, dv_ref, dgb_ref, dbb_ref), vjp(cot)):
                for h, val in enumerate(vals):
                    ref[:, (lo + h) * KEY_DIM_B:(lo + h + 1) * KEY_DIM_B] = val

    row = pl.BlockSpec((GROUP_T, V_B), lambda i: (i, 0))
    arow = pl.BlockSpec((GROUP_T, N_HEADS_B * CHUNK), lambda i: (i, 0))
    big = jax.ShapeDtypeStruct((t, V_B), F32)
    outs, side_outs = _call(
        body, name=name, grid=(t // GROUP_T,),
        in_specs=[row] * 5 + [arow] + [row] * 4 + [arow, row], out_specs=[row] * 5, out_shape=[big] * 5,
        semantics=("parallel",), args=(q, k, v, gb, bb, ti, du, dw, dqd, dkt, da, dgl), side=side)
    return tuple(outs), side_outs


def _dn_steps(s, qd, kt, u, w, a, gl):
    heads = range(len(s))
    v_new = [u[h] - _dot(w[h], s[h]) for h in heads]
    qs = [_dot(qd[h], s[h]) for h in heads]
    o = [qs[h] + _dot(a[h], v_new[h]) for h in heads]
    s_new = [s[h] * jnp.exp(gl[h][0:1, :]) + _dot(kt[h], v_new[h], _TN) for h in heads]
    return s_new, o


def _dn_scan(u, w, qd, kt, a, gl, nbatch, *, side=None, name):
    t = u.shape[0]
    ng = t // nbatch // GROUP_T
    cpg = GROUP_T // CHUNK

    def body(u_ref, w_ref, qd_ref, kt_ref, a_ref, gl_ref, o_ref, ss_ref, s_ref):
        @pl.when(pl.program_id(1) == 0)
        def _():
            s_ref[...] = jnp.zeros_like(s_ref)

        def chunk(c, carry):
            rows = pl.ds(pl.multiple_of(c * CHUNK, CHUNK), CHUNK)
            heads = range(N_HEADS_B)
            s = [s_ref[h] for h in heads]
            for h in heads:
                ss_ref[c, h] = s[h]
            s_new, o = _dn_steps(s, *[[r[rows, h * wd:(h + 1) * wd] for h in heads] for r, wd in
                                      ((qd_ref, KEY_DIM_B), (kt_ref, KEY_DIM_B), (u_ref, VAL_DIM_B), (w_ref, KEY_DIM_B),
                                       (a_ref, CHUNK), (gl_ref, VAL_DIM_B))])
            for h in heads:
                s_ref[h] = s_new[h]
                o_ref[rows, h * VAL_DIM_B:(h + 1) * VAL_DIM_B] = o[h]
            return carry

        lax.fori_loop(0, cpg, chunk, 0)

    row = pl.BlockSpec((GROUP_T, V_B), lambda b, i: (b * ng + i, 0))
    arow = pl.BlockSpec((GROUP_T, N_HEADS_B * CHUNK), lambda b, i: (b * ng + i, 0))
    outs, side_outs = _call(
        body, name=name, grid=(nbatch, ng),
        in_specs=[row, row, row, row, arow, row],
        out_specs=[row, pl.BlockSpec((cpg, N_HEADS_B, KEY_DIM_B, VAL_DIM_B), lambda b, i: (b * ng + i, 0, 0, 0))],
        out_shape=[jax.ShapeDtypeStruct((t, V_B), F32),
                   jax.ShapeDtypeStruct((t // CHUNK, N_HEADS_B, KEY_DIM_B, VAL_DIM_B), F32)],
        scratch_shapes=[pltpu.VMEM((N_HEADS_B, KEY_DIM_B, VAL_DIM_B), F32)],
        semantics=("parallel", "arbitrary"), args=(u, w, qd, kt, a, gl), side=side)
    return tuple(outs), side_outs


def _dn_scan_bwd(u, w, qd, kt, a, gl, ss, do, nbatch, *, name):
    t = u.shape[0]
    ng = t // nbatch // GROUP_T
    cpg = GROUP_T // CHUNK

    def body(u_ref, w_ref, qd_ref, kt_ref, a_ref, gl_ref, ss_ref, do_ref,
             du_ref, dw_ref, dqd_ref, dkt_ref, da_ref, dgl_ref, ds_ref):
        @pl.when(pl.program_id(1) == 0)
        def _():
            ds_ref[...] = jnp.zeros_like(ds_ref)

        def chunk(cc, carry):
            c = cpg - 1 - cc
            rows = pl.ds(pl.multiple_of(c * CHUNK, CHUNK), CHUNK)
            heads = range(N_HEADS_B)
            ins = ((qd_ref, KEY_DIM_B), (kt_ref, KEY_DIM_B), (u_ref, VAL_DIM_B), (w_ref, KEY_DIM_B), (a_ref, CHUNK),
                   (gl_ref, VAL_DIM_B))
            _, vjp = jax.vjp(_dn_steps, [ss_ref[c, h] for h in heads],
                             *[[r[rows, h * wd:(h + 1) * wd] for h in heads] for r, wd in ins])
            grads = vjp(([ds_ref[h] for h in heads], [do_ref[rows, h * VAL_DIM_B:(h + 1) * VAL_DIM_B] for h in heads]))
            for h in heads:
                ds_ref[h] = grads[0][h]
            outs = ((dqd_ref, KEY_DIM_B), (dkt_ref, KEY_DIM_B), (du_ref, VAL_DIM_B), (dw_ref, KEY_DIM_B), (da_ref, CHUNK),
                    (dgl_ref, VAL_DIM_B))
            for (r, wd), vals in zip(outs, grads[1:]):
                for h in heads:
                    r[rows, h * wd:(h + 1) * wd] = vals[h]
            return carry

        lax.fori_loop(0, cpg, chunk, 0)

    row = pl.BlockSpec((GROUP_T, V_B), lambda b, j: (b * ng + ng - 1 - j, 0))
    arow = pl.BlockSpec((GROUP_T, N_HEADS_B * CHUNK), lambda b, j: (b * ng + ng - 1 - j, 0))
    big = jax.ShapeDtypeStruct((t, V_B), F32)
    return pl.pallas_call(
        body, name=name, grid=(nbatch, ng),
        in_specs=[row, row, row, row, arow, row,
                  pl.BlockSpec((cpg, N_HEADS_B, KEY_DIM_B, VAL_DIM_B), lambda b, j: (b * ng + ng - 1 - j, 0, 0, 0)), row],
        out_specs=[row, row, row, row, arow, row],
        out_shape=[big, big, big, big, jax.ShapeDtypeStruct((t, N_HEADS_B * CHUNK), F32), big],
        scratch_shapes=[pltpu.VMEM((N_HEADS_B, KEY_DIM_B, VAL_DIM_B), F32)],
        compiler_params=_params(dimension_semantics=("parallel", "arbitrary")),
    )(u, w, qd, kt, a, gl, ss, do)


def _rms_gate(o, hcat, dn_g, *, name):
    t = o.shape[0]

    def body(o_ref, z_ref, g_ref, y_ref):
        for h in range(N_HEADS_B):
            sl = slice(h * VAL_DIM_B, (h + 1) * VAL_DIM_B)
            o_ = o_ref[:, sl]
            r = lax.rsqrt(jnp.mean(o_ * o_, axis=-1, keepdims=True) + NORM_EPS)
            y_ref[:, sl] = (o_ * r * g_ref[...] * _silu(z_ref[:, sl])).astype(BF16)

    row = pl.BlockSpec((ROW_T, V_B), lambda i: (i, 0))
    return pl.pallas_call(
        body, name=name, grid=(t // ROW_T,),
        in_specs=[row, pl.BlockSpec((ROW_T, V_B), lambda i: (i, HC_Z // V_B)), pl.BlockSpec((1, VAL_DIM_B), lambda i: (0, 0))],
        out_specs=row, out_shape=jax.ShapeDtypeStruct((t, V_B), BF16),
        compiler_params=_params(dimension_semantics=("parallel",)),
    )(o, hcat, dn_g.reshape(1, VAL_DIM_B))


def _rms_gate_bwd(dy, o, hcat, dn_g, *, name):
    t = o.shape[0]

    def body(dy_ref, o_ref, z_ref, g_ref, do_ref, dz_ref, dg_ref):
        i = pl.program_id(0)
        g = g_ref[...]
        dg_p = jnp.zeros((1, VAL_DIM_B), F32)
        for h in range(N_HEADS_B):
            sl = slice(h * VAL_DIM_B, (h + 1) * VAL_DIM_B)
            o_ = o_ref[:, sl]
            z_ = z_ref[:, sl]
            dy_ = dy_ref[:, sl]
            r = lax.rsqrt(jnp.mean(o_ * o_, axis=-1, keepdims=True) + NORM_EPS)
            n = o_ * r
            sz = _silu(z_)
            dz_ref[:, sl] = (dy_ * n * g * _dsilu(z_)).astype(BF16)
            dg_p = dg_p + jnp.sum(dy_ * n * sz, axis=0, keepdims=True)
            dn = dy_ * g * sz
            do_ref[:, sl] = r * dn - o_ * (r * r * r) * jnp.mean(o_ * dn, axis=-1, keepdims=True)

        @pl.when(i == 0)
        def _():
            dg_ref[...] = dg_p

        @pl.when(i > 0)
        def _():
            dg_ref[...] += dg_p

    row = pl.BlockSpec((ROW_T, V_B), lambda i: (i, 0))
    vec = pl.BlockSpec((1, VAL_DIM_B), lambda i: (0, 0))
    return pl.pallas_call(
        body, name=name, grid=(t // ROW_T,),
        in_specs=[row, row, pl.BlockSpec((ROW_T, V_B), lambda i: (i, HC_Z // V_B)), vec],
        out_specs=[row, row, vec],
        out_shape=[jax.ShapeDtypeStruct((t, V_B), F32), jax.ShapeDtypeStruct((t, V_B), BF16),
                   jax.ShapeDtypeStruct((1, VAL_DIM_B), F32)],
        compiler_params=_params(dimension_semantics=("arbitrary",)),
    )(dy, o, hcat, dn_g.reshape(1, VAL_DIM_B))


def _merge(ya, yb, hcat, *, name):
    t = ya.shape[0]

    def body(ya_ref, yb_ref, ga_ref, gb_ref, y_ref):
        y_ref[...] = (_sigmoid(ga_ref[...]) * ya_ref[...] + _sigmoid(gb_ref[...]) * yb_ref[...]).astype(BF16)

    row = pl.BlockSpec((ROW_T, D_MODEL), lambda i: (i, 0))
    return pl.pallas_call(
        body, name=name, grid=(t // ROW_T,),
        in_specs=[row, row, pl.BlockSpec((ROW_T, D_MODEL), lambda i: (i, HC_GATE // D_MODEL)),
                  pl.BlockSpec((ROW_T, D_MODEL), lambda i: (i, HC_GATE // D_MODEL + 1))],
        out_specs=row, out_shape=jax.ShapeDtypeStruct((t, D_MODEL), BF16),
        compiler_params=_params(dimension_semantics=("parallel",)),
    )(ya, yb, hcat, hcat)


def _merge_bwd(dmix, ya, yb, hcat, *, name):
    t = ya.shape[0]

    def body(d_ref, ya_ref, yb_ref, ga_ref, gb_ref, dya_ref, dyb_ref, dgate_ref):
        d = d_ref[...]
        sa = _sigmoid(ga_ref[...])
        sb = _sigmoid(gb_ref[...])
        dya_ref[...] = (d * sa).astype(BF16)
        dyb_ref[...] = (d * sb).astype(BF16)
        dgate_ref[:, :D_MODEL] = (d * ya_ref[...] * sa * (1.0 - sa)).astype(BF16)
        dgate_ref[:, D_MODEL:] = (d * yb_ref[...] * sb * (1.0 - sb)).astype(BF16)

    row = pl.BlockSpec((ROW_T, D_MODEL), lambda i: (i, 0))
    return pl.pallas_call(
        body, name=name, grid=(t // ROW_T,),
        in_specs=[row, row, row, pl.BlockSpec((ROW_T, D_MODEL), lambda i: (i, HC_GATE // D_MODEL)),
                  pl.BlockSpec((ROW_T, D_MODEL), lambda i: (i, HC_GATE // D_MODEL + 1))],
        out_specs=[row, row, pl.BlockSpec((ROW_T, 2 * D_MODEL), lambda i: (i, 0))],
        out_shape=[jax.ShapeDtypeStruct((t, D_MODEL), BF16)] * 2 + [jax.ShapeDtypeStruct((t, 2 * D_MODEL), BF16)],
        compiler_params=_params(dimension_semantics=("parallel",)),
    )(dmix, ya, yb, hcat, hcat)


def _loss_head(y, target, *, name):
    t, n = y.shape
    tm = _tile(t, (512, 256, 128))

    def body(y_ref, t_ref, part_ref, dy_ref):
        i = pl.program_id(0)
        e = y_ref[...] - t_ref[...]
        dy_ref[...] = e * (1.0 / n)
        p = jnp.sum((e * e).reshape(tm // 8, 8, n), axis=0) * (0.5 / n)

        @pl.when(i == 0)
        def _():
            part_ref[...] = p

        @pl.when(i > 0)
        def _():
            part_ref[...] += p

    row = pl.BlockSpec((tm, n), lambda i: (i, 0))
    return pl.pallas_call(
        body, name=name, grid=(t // tm,),
        in_specs=[row, row], out_specs=[pl.BlockSpec((8, n), lambda i: (0, 0)), row],
        out_shape=[jax.ShapeDtypeStruct((8, n), F32), jax.ShapeDtypeStruct((t, n), F32)],
        compiler_params=_params(dimension_semantics=("arbitrary",)),
    )(y, target)


def _adamw_math(w, g, m, v):
    nm = ADAM_B1 * m + (1.0 - ADAM_B1) * g
    nv = ADAM_B2 * v + (1.0 - ADAM_B2) * (g * g)
    m_hat = nm / (1.0 - ADAM_B1 ** ADAM_STEP)
    v_hat = nv / (1.0 - ADAM_B2 ** ADAM_STEP)
    return -ADAM_LR * (m_hat / (jnp.sqrt(v_hat) + ADAM_EPS) + ADAM_WD * w), nm, nv


def _adamw(w, g, m, v, *, name):
    shape = w.shape
    cols = shape[-1]
    rows = int(np.prod(shape[:-1]))
    w2, g2, m2, v2 = (a.reshape(rows, cols) for a in (w, g, m, v))
    tr = rows
    if rows * cols > 512 * 1024:
        tr = _tile(rows, tuple(c for c in (512, 256, 128, 64, 32, 16, 8) if c * cols <= 256 * 1024))

    def body(w_ref, g_ref, m_ref, v_ref, d_ref, nm_ref, nv_ref):
        d_ref[...], nm_ref[...], nv_ref[...] = _adamw_math(w_ref[...], g_ref[...], m_ref[...], v_ref[...])

    blk = pl.BlockSpec((tr, cols), lambda i: (i, 0))
    outs = pl.pallas_call(
        body, name=name, grid=(rows // tr,),
        in_specs=[blk] * 4, out_specs=[blk] * 3,
        out_shape=[jax.ShapeDtypeStruct((rows, cols), F32)] * 3,
        compiler_params=_params(dimension_semantics=("parallel",)),
    )(w2, g2, m2, v2)
    return tuple(o.reshape(shape) for o in outs)


def _repack_w_in(w_in):
    d = w_in.shape[0]
    o = 0
    parts = {}
    for nm, wd in (("q", Q_A), ("k", KV_W), ("v", KV_W), ("conv", CONV_CH), ("beta", N_HEADS_B), ("dt", N_HEADS_B),
                   ("z", V_B), ("gate", 2 * D_MODEL)):
        parts[nm] = w_in[:, o:o + wd]
        o += wd
    z = lambda n: jnp.zeros((d, n), w_in.dtype)
    return jnp.concatenate([parts["q"], parts["z"], parts["k"], parts["v"], parts["beta"], parts["dt"],
                            z(128 - 2 * N_HEADS_B), z(HC_CONV - HC_BD - 128), parts["conv"], parts["gate"]], axis=1)


MATRIX_NAMES = ("ffn_w13", "ffn_w2", "w_in", "w_branch_a", "w_branch_b", "w_out")
GATHER_BESIDE_IN_PROJ = ("conv_w", "w_branch_a", "w_branch_b", "w_out")


def _dw_in_by_owner(dw):
    sections = ((Q_A, HC_Q), (2 * KV_W, HC_K), (CONV_CH, HC_CONV), (2 * N_HEADS_B, HC_BD), (V_B, HC_Z),
                (2 * D_MODEL, HC_GATE))
    per = N_IN // 4
    owners = []
    for o in range(4):
        lo, hi, start, parts = o * per, (o + 1) * per, 0, []
        for width, off in sections:
            a, b = max(lo, start), min(hi, start + width)
            if a < b:
                parts.append(dw[:, off + a - start:off + b - start])
            start += width
        rows = jnp.concatenate(parts, axis=1)
        owners.append(jnp.stack([rows[:rows.shape[0] // 2], rows[rows.shape[0] // 2:]]))
    return jnp.stack(owners)


def _lane_row(vals):
    return jnp.pad(vals.astype(F32).reshape(1, N_HEADS_B), ((0, 0), (N_HEADS_B, 128 - 2 * N_HEADS_B)))


def _local_step(x, target, rel_bias, layer_wts, side_shards=None, side_assemble=None, reducer=None):
    nbatch, seq, d = x.shape
    t = nbatch * seq
    depth = len(layer_wts)
    layer_wts = list(layer_wts)
    x0 = x.reshape(t, d)
    tgt = target.reshape(t, d)

    onehot = jnp.asarray(_bucket_onehot())
    rel_t = jnp.pad(rel_bias.T, ((0, 0), (0, 128 - NUM_BUCKETS)))
    bias = _mm(rel_t, onehot, tb=True, exact=True, name="pos_bias")
    bias = bias.reshape(N_HEADS_A, WINDOW, 2 * WINDOW)

    pending = [dict(s) if s else {} for s in (side_shards or [None] * depth)] + [{}]

    def fetch(*wanted):
        picked = [(layer, n) for layer, names in wanted for n in list(pending[layer]) if names is None or n in names]
        if not picked:
            return None, lambda outs: None
        job = _gather_job([pending[layer].pop(n) for layer, n in picked])

        def finish(outs):
            for (layer, n), out in zip(picked, outs):
                for k, val in side_assemble(layer, {n: out}).items():
                    if isinstance(val, dict):
                        layer_wts[layer].setdefault(k, {}).update(val)
                    else:
                        layer_wts[layer][k] = val
        return job, finish

    saved = []
    xin, xin_b = x0, x0.astype(BF16)
    for i in range(depth):
        L = {}
        W = layer_wts[i]
        tag = f"_l{i}"
        job, finish = fetch((i, ("w_in_0",)))
        a, got = _ffn_up_act(xin_b, W["ffn_w13"][0], side=job, name="ffn_up_act" + tag + "a")
        finish(got)
        job, finish = fetch((i, ("w_in_1",)))
        if job is None:
            r1, x1, x1_b = _mm_res_ln(a, W["ffn_w2"][0], xin, W["ln_g"][0], W["ln_b"][0],
                                      alpha=DN_ALPHA, c=0.5, name="ffn_down_ln" + tag + "a")
        else:
            r1, x1, x1_b, got = _mm_res_ln(a, W["ffn_w2"][0], xin, W["ln_g"][0], W["ln_b"][0],
                                           alpha=DN_ALPHA, c=0.5, side=job, name="ffn_down_ln" + tag + "a")
            finish(got)
        L.update(x0_b=xin_b, a0=a, r1=r1, x1=x1, x1_b=x1_b)
        job, finish = fetch((i, GATHER_BESIDE_IN_PROJ), (i + 1, GATHER_BESIDE_IN_PROJ + ("w_in_0",)))
        hcat, got = _mm_w(x1_b, W["w_in_p"], side=job, name="in_proj" + tag)
        finish(got)
        job, finish = fetch((i + 1, ("w_in_1",)))
        ao, got = _attn_fwd(hcat, bias, W["sinks"], nbatch, side=job, name="swa" + tag)
        finish(got)
        ya = _mm(ao, W["w_branch_a"], name="branch_a" + tag)
        job, finish = fetch((i, ("ffn_w13_1",)))
        (qn, kn, vs), got = _conv_prep(hcat, W["conv_w"], nbatch, side=job, name="conv_prep" + tag)
        finish(got)
        a_row = _lane_row(W["a_log"])
        dt_row = _lane_row(W["dt_bias"])
        gb, bb = _gates(hcat, a_row, dt_row, name="gates" + tag)
        job, finish = fetch((i + 1, ("ln_g", "ln_b", "ffn_w13_0", "ffn_w13_1", "ffn_w2_0", "ffn_w2_1")))
        (u, w, qd, kt, aa, gl, ti), got = _dn_prep(qn, kn, vs, gb, bb, side=job, name="dn_prep" + tag)
        finish(got)
        job, finish = fetch((i, None))
        (o, ss), got = _dn_scan(u, w, qd, kt, aa, gl, nbatch, side=job, name="dn_scan" + tag)
        finish(got)
        on = _rms_gate(o, hcat, W["dn_norm_g"], name="rms_gate" + tag)
        yb = _mm(on, W["w_branch_b"], name="branch_b" + tag)
        mix = _merge(ya, yb, hcat, name="merge" + tag)
        r2, x2, x2_b = _mm_res_ln(mix, W["w_out"], x1, W["ln_g"][1], W["ln_b"][1],
                                  alpha=DN_ALPHA, c=1.0, name="out_proj_ln" + tag)
        L.update(hcat=hcat, ao=ao, ya=ya, qn=qn, kn=kn, vs=vs, gb=gb, bb=bb, a_row=a_row, dt_row=dt_row,
                 u=u, w=w, qd=qd, kt=kt, aa=aa, gl=gl, ti=ti, o=o, ss=ss, on=on, yb=yb, mix=mix, r2=r2, x2_b=x2_b)
        a, _ = _ffn_up_act(x2_b, W["ffn_w13"][1], name="ffn_up_act" + tag + "b")
        r3, x3, x3_b = _mm_res_ln(a, W["ffn_w2"][1], x2, W["ln_g"][2], W["ln_b"][2],
                                  alpha=DN_ALPHA, c=0.5, name="ffn_down_ln" + tag + "b")
        L.update(a1=a, r3=r3)
        saved.append(L)
        xin, xin_b = x3, x3_b

    part, dy = _loss_head(xin, tgt, name="loss_head")
    loss = jnp.sum(part)

    grads = {k: [None] * depth for k in ("ln_g", "ln_b", "ffn_w13", "ffn_w2", "w_in", "conv_w", "a_log", "dt_bias",
                                          "dn_norm_g", "sinks", "w_branch_a", "w_branch_b", "w_out")}
    dbias_total = None
    for i in reversed(range(depth)):
        L = saved[i]
        W = layer_wts[i]
        tag = f"_l{i}"
        dln_g, dln_b, dw13, dw2 = [None] * 3, [None] * 3, [None] * 2, [None] * 2

        def ffn_bwd(dyo, r, xprev_b, asave, j, sfx):
            dres, df, dln_g[2 * j], dln_b[2 * j] = _ln_bwd(dyo, r, W["ln_g"][2 * j], alpha=DN_ALPHA, c=0.5,
                                                           name="ln_bwd" + tag + sfx)
            job = reducer.job_a() if (reducer is not None and j == 1) else None
            dh, swapped = _ffn_bwd_mid(xprev_b, W["ffn_w13"][j], df, W["ffn_w2"][j], side=job,
                                       name="ffn_bwd_mid" + tag + sfx)
            if job is not None:
                reducer.done_a(swapped)
            dw2[j] = _mm(asave, df, ta=True, name="ffn_w2_grad" + tag + sfx)
            dw13[j] = _mm(xprev_b, dh, ta=True, b_halves=True, name="ffn_w13_grad" + tag + sfx)
            return _mm_wt(dh, W["ffn_w13"][j], dres, a_halves=True, name="ffn_up_bwd" + tag + sfx)

        dx2 = ffn_bwd(dy, L["r3"], L["x2_b"], L["a1"], 1, "b")

        dres2, dymix, dln_g[1], dln_b[1] = _ln_bwd(dx2, L["r2"], W["ln_g"][1], alpha=DN_ALPHA, c=1.0,
                                                   name="ln_bwd" + tag + "m")
        hcat = L["hcat"]
        dmix = _mm(dymix, W["w_out"], tb=True, name="out_proj_bwd" + tag)
        grads["w_out"][i] = _mm(L["mix"], dymix, ta=True, name="w_out_grad" + tag)
        dya, dyb, dgate = _merge_bwd(dmix, L["ya"], L["yb"], hcat, name="merge_bwd" + tag)
        dao = _mm(dya, W["w_branch_a"], tb=True, name="branch_a_bwd" + tag)
        grads["w_branch_a"][i] = _mm(L["ao"], dya, ta=True, name="w_branch_a_grad" + tag)
        don = _mm(dyb, W["w_branch_b"], tb=True, name="branch_b_bwd" + tag)
        grads["w_branch_b"][i] = _mm(L["on"], dyb, ta=True, name="w_branch_b_grad" + tag)
        do, dz, ddn = _rms_gate_bwd(don, L["o"], hcat, W["dn_norm_g"], name="rms_gate_bwd" + tag)
        grads["dn_norm_g"][i] = ddn.reshape(VAL_DIM_B)
        du, dw, dqd, dkt, daa, dgl = _dn_scan_bwd(L["u"], L["w"], L["qd"], L["kt"], L["aa"], L["gl"], L["ss"], do,
                                                  nbatch, name="dn_scan_bwd" + tag)
        job = reducer.job_b() if reducer is not None else None
        (dqn, dkn, dvs, dgb, dbb), exchanged = _dn_prep_bwd(L["qn"], L["kn"], L["vs"], L["gb"], L["bb"], L["ti"], du, dw,
                                                            dqd, dkt, daa, dgl, side=job, name="dn_prep_bwd" + tag)
        if job is not None:
            reducer.done_b(exchanged)
        dconv, dconv_w = _conv_bwd(dqn, dkn, dvs, hcat, W["conv_w"], nbatch, name="conv_bwd" + tag)
        grads["conv_w"][i] = dconv_w[:CONV_K]
        dbd, da_log, ddt = _gates_bwd(dgb, dbb, hcat, L["a_row"], L["dt_row"], name="gates_bwd" + tag)
        grads["a_log"][i] = da_log[0, N_HEADS_B:2 * N_HEADS_B]
        grads["dt_bias"][i] = ddt[0, N_HEADS_B:2 * N_HEADS_B]
        dq, dk, dv, dbias, dsink = _attn_bwd(hcat, bias, W["sinks"], dao, nbatch, name="swa_bwd" + tag)
        grads["sinks"][i] = dsink.reshape(N_HEADS_A)
        dbias_total = dbias if dbias_total is None else dbias_total + dbias
        dhcat = jnp.concatenate([dq, dz, dk, dv, dbd, jnp.zeros((t, HC_CONV - HC_BD - 128), BF16), dconv, dgate], axis=1)
        dw_in_p = _mm(L["x1_b"], dhcat, ta=True, name="w_in_grad" + tag)
        grads["w_in"][i] = _dw_in_by_owner(dw_in_p)
        dx1 = _mm_wt(dhcat, W["w_in_p"], dres2, name="in_proj_bwd" + tag)

        dy = ffn_bwd(dx1, L["r1"], L["x0_b"], L["a0"], 0, "a")
        grads["ln_g"][i] = jnp.concatenate(dln_g, axis=0)
        grads["ln_b"][i] = jnp.concatenate(dln_b, axis=0)
        grads["ffn_w13"][i] = dw13
        grads["ffn_w2"][i] = dw2
        if reducer is not None:
            reducer.layer_done(i, {n: grads[n][i] for n in MATRIX_NAMES})

    out = {k: (v if k in MATRIX_NAMES else jnp.stack(v)) for k, v in grads.items()}
    drel = _mm(dbias_total.reshape(N_HEADS_A, WINDOW * 2 * WINDOW), onehot, name="rel_bias_grad")
    out["rel_bias"] = drel[:, :NUM_BUCKETS].T
    return loss, dy.reshape(nbatch, seq, d), out


N_CHIPS = 4
MESH_ID = pl.DeviceIdType.MESH
HBM_SPEC = pl.BlockSpec(memory_space=pltpu.HBM)


def _place():
    x, y, c = lax.axis_index("x"), lax.axis_index("y"), lax.axis_index("c")
    others = [(1 - x, y), (x, 1 - y), (1 - x, 1 - y)]
    return x, y, c, others


def _chip_index(cx, cy):
    return 2 * cx + cy


def _gather_sems(n):
    return [pltpu.SemaphoreType.DMA((n, 3)), pltpu.SemaphoreType.DMA((n, 3)), pltpu.SemaphoreType.DMA((n,))]


def _gather_copies(ins, outs, send_sems, recv_sems, local_sems):
    x, y, c, others = _place()
    me = _chip_index(x, y)
    copies = []
    for i in range(len(ins)):
        copies.append(pltpu.make_async_copy(ins[i], outs[i].at[me], local_sems.at[i]))
        for k, (ox, oy) in enumerate(others):
            copies.append(pltpu.make_async_remote_copy(src_ref=ins[i], dst_ref=outs[i].at[me], send_sem=send_sems.at[i, k],
                                                       recv_sem=recv_sems.at[i, k], device_id=(ox, oy, c),
                                                       device_id_type=MESH_ID))
    return copies


def _gather_job(tensors):
    return dict(ins=list(tensors), out_shape=[jax.ShapeDtypeStruct((N_CHIPS,) + t.shape, t.dtype) for t in tensors],
                scratch=_gather_sems(len(tensors)), make=_gather_copies)


def _run_job(job, *, name):
    n_in, n_out = len(job["ins"]), len(job["out_shape"])

    def body(*refs):
        copies = job["make"](refs[:n_in], refs[n_in:n_in + n_out], *refs[n_in + n_out:])
        for cp in copies:
            cp.start()
        for cp in copies:
            cp.wait()

    return pl.pallas_call(
        body, name=name, in_specs=[HBM_SPEC] * n_in, out_specs=[HBM_SPEC] * n_out,
        out_shape=list(job["out_shape"]), scratch_shapes=list(job["scratch"]),
    )(*job["ins"])


def _allgather_devices(v, *, name):
    def body(v_ref, o_ref, send_sems, recv_sems, local_sem):
        x, y, c, _ = _place()
        me = 4 * x + 2 * y + c
        loc = pltpu.make_async_copy(v_ref, o_ref.at[me], local_sem)
        loc.start()
        copies = [loc]
        for k in range(1, 8):
            px, py, pc = x ^ (k >> 2), y ^ ((k >> 1) & 1), c ^ (k & 1)
            cp = pltpu.make_async_remote_copy(src_ref=v_ref, dst_ref=o_ref.at[me], send_sem=send_sems.at[k - 1],
                                              recv_sem=recv_sems.at[k - 1], device_id=(px, py, pc), device_id_type=MESH_ID)
            cp.start()
            copies.append(cp)
        for cp in copies:
            cp.wait()

    return pl.pallas_call(
        body, name=name, in_specs=[HBM_SPEC], out_specs=HBM_SPEC,
        out_shape=jax.ShapeDtypeStruct((8,) + v.shape, v.dtype),
        scratch_shapes=[pltpu.SemaphoreType.DMA((7,)), pltpu.SemaphoreType.DMA((7,)), pltpu.SemaphoreType.DMA],
    )(v)


def _sum_slots(g, *, name):
    nb, n, r, l = g.shape
    tr = r // 2 if r % 32 == 0 else r

    def body(g_ref, o_ref):
        acc = g_ref[0].astype(F32)
        for k in range(1, n):
            acc = acc + g_ref[k].astype(F32)
        o_ref[...] = acc

    return pl.pallas_call(
        body, name=name, grid=(nb, r // tr),
        in_specs=[pl.BlockSpec((None, n, tr, l), lambda b, i: (b, 0, i, 0))],
        out_specs=pl.BlockSpec((None, tr, l), lambda b, i: (b, i, 0)),
        out_shape=jax.ShapeDtypeStruct((nb, r, l), F32),
        compiler_params=_params(dimension_semantics=("parallel", "parallel")),
    )(g)


def _half_window(ref, kind, h):
    if kind == "rows":
        return ref.at[:, h]
    r = ref.shape[0] // 2
    return ref.at[pl.ds(pl.multiple_of(h * r, r), r), :]


def _owner_window(ref, kind, o):
    if kind == "rows":
        return ref.at[o]
    cols = ref.shape[1] // N_CHIPS
    return ref.at[:, pl.ds(pl.multiple_of(o * cols, cols), cols)]


def _half_shape(g, kind):
    return (g.shape[0],) + g.shape[2:] if kind == "rows" else (g.shape[0] // 2, g.shape[1])


def _swap_job(gs, kinds):
    n = len(gs)

    def make(ins, outs, send_sems, recv_sems):
        x, y, c, _ = _place()
        return [pltpu.make_async_remote_copy(src_ref=_half_window(ins[i], kinds[i], 1 - c), dst_ref=outs[i],
                                             send_sem=send_sems.at[i], recv_sem=recv_sems.at[i],
                                             device_id=(x, y, 1 - c), device_id_type=MESH_ID) for i in range(n)]

    return dict(ins=list(gs), out_shape=[jax.ShapeDtypeStruct(_half_shape(g, k), g.dtype) for g, k in zip(gs, kinds)],
                scratch=[pltpu.SemaphoreType.DMA((n,)), pltpu.SemaphoreType.DMA((n,))], make=make)


def _pair_sum(g, got, kind, c_idx, *, name):
    hs = _half_shape(g, kind)

    def body(c_ref, g_ref, r_ref, o_ref):
        o_ref[...] = (g_ref[...] + r_ref[...]).astype(BF16)

    if kind == "rows":
        _, _, r, cols = g.shape
        grid = (N_CHIPS,)
        in_specs = [pl.BlockSpec((None, None, r, cols), lambda o, c_ref: (o, c_ref[0], 0, 0)),
                    pl.BlockSpec((None, r, cols), lambda o, c_ref: (o, 0, 0))]
        out_spec = pl.BlockSpec((None, r, cols), lambda o, c_ref: (o, 0, 0))
    else:
        r, cols = hs
        steps = 4
        tr = r // steps
        grid = (steps,)
        in_specs = [pl.BlockSpec((tr, cols), lambda i, c_ref: (c_ref[0] * steps + i, 0)),
                    pl.BlockSpec((tr, cols), lambda i, c_ref: (i, 0))]
        out_spec = pl.BlockSpec((tr, cols), lambda i, c_ref: (i, 0))
    return pl.pallas_call(
        body, name=name,
        grid_spec=pltpu.PrefetchScalarGridSpec(num_scalar_prefetch=1, grid=grid, in_specs=in_specs, out_specs=out_spec),
        out_shape=jax.ShapeDtypeStruct(hs, BF16),
        compiler_params=_params(dimension_semantics=("parallel",)),
    )(c_idx, g, got)


def _exchange_job(ss, kinds):
    n = len(ss)

    def shard_shape(s, kind):
        return s.shape[1:] if kind == "rows" else (s.shape[0], s.shape[1] // N_CHIPS)

    def make(ins, outs, send_sems, recv_sems, local_sems):
        x, y, c, others = _place()
        me = _chip_index(x, y)
        copies = []
        for i in range(n):
            dst = outs[i].at[me]
            copies.append(pltpu.make_async_copy(_owner_window(ins[i], kinds[i], me), dst, local_sems.at[i]))
            for k, (ox, oy) in enumerate(others):
                copies.append(pltpu.make_async_remote_copy(
                    src_ref=_owner_window(ins[i], kinds[i], _chip_index(ox, oy)), dst_ref=dst, send_sem=send_sems.at[i, k],
                    recv_sem=recv_sems.at[i, k], device_id=(ox, oy, c), device_id_type=MESH_ID))
        return copies

    return dict(ins=list(ss), out_shape=[jax.ShapeDtypeStruct((N_CHIPS,) + shard_shape(s, k), s.dtype)
                                         for s, k in zip(ss, kinds)],
                scratch=_gather_sems(n), make=make)


def _send_halves(fs, *, name):
    n = len(fs)

    def body(*refs):
        ins, outs, send_sems, recv_sems = refs[:n], refs[n:2 * n], refs[2 * n], refs[2 * n + 1]
        x, y, c, _ = _place()
        copies = [pltpu.make_async_remote_copy(src_ref=ins[i], dst_ref=outs[i], send_sem=send_sems.at[i],
                                               recv_sem=recv_sems.at[i], device_id=(x, y, 1 - c), device_id_type=MESH_ID)
                  for i in range(n)]
        for cp in copies:
            cp.start()
        for cp in copies:
            cp.wait()

    return pl.pallas_call(
        body, name=name, in_specs=[HBM_SPEC] * n, out_specs=[HBM_SPEC] * n,
        out_shape=[jax.ShapeDtypeStruct(f.shape, f.dtype) for f in fs],
        scratch_shapes=[pltpu.SemaphoreType.DMA((n,)), pltpu.SemaphoreType.DMA((n,))],
    )(*fs)


def _adamw_halves(w, m, v, own, other, c_idx, *, name):
    shape = w.shape
    nl, r, cols = own.shape
    w4, m4, v4 = (a.reshape(nl, 2, r, cols) for a in (w, m, v))
    tr = r if r * cols * 4 <= 3 * 512 * 1024 else _tile(r, tuple(c for c in (256, 128, 64, 32, 16, 8) if c * cols <= 256 * 1024))

    def body(c_ref, w_ref, m_ref, v_ref, own_ref, other_ref, g_ref, d_ref, nm_ref, nv_ref):
        g_ = jnp.where(pl.program_id(1) == c_ref[0], own_ref[...], other_ref[...])
        g_ref[...] = g_
        d_ref[...], nm_ref[...], nv_ref[...] = _adamw_math(w_ref[...], g_, m_ref[...], v_ref[...])

    full = pl.BlockSpec((None, None, tr, cols), lambda l, h, i, c_ref: (l, h, i, 0))
    half = pl.BlockSpec((None, tr, cols), lambda l, h, i, c_ref: (l, i, 0))
    outs = pl.pallas_call(
        body, name=name,
        grid_spec=pltpu.PrefetchScalarGridSpec(num_scalar_prefetch=1, grid=(nl, 2, r // tr),
                                               in_specs=[full, full, full, half, half], out_specs=[full] * 4),
        out_shape=[jax.ShapeDtypeStruct((nl, 2, r, cols), F32)] * 4,
        compiler_params=_params(dimension_semantics=("parallel", "parallel", "parallel")),
    )(c_idx, w4, m4, v4, own, other)
    return tuple(o.reshape(shape) for o in outs)


SHARD_AXIS = {"rel_bias": None, "ln_g": 2, "ln_b": 2, "ffn_w13": 3, "ffn_w2": 2, "w_in": 2, "conv_w": 2, "a_log": None,
              "dt_bias": None, "dn_norm_g": None, "sinks": None, "w_branch_a": 1, "w_branch_b": 1, "w_out": 1}
WEIGHT_NAMES = tuple(SHARD_AXIS)
SMALL_NAMES = tuple(n for n in WEIGHT_NAMES if n not in MATRIX_NAMES)
PACK_LANES = 1024


def _unshard(gathered, axis):
    g = jnp.moveaxis(gathered, 0, axis)
    return g.reshape(g.shape[:axis] + (g.shape[axis] * g.shape[axis + 1],) + g.shape[axis + 2:])


class _GradReducer:
    def __init__(self, c_idx):
        self.c_idx = c_idx
        self.swapping = None
        self.exchanging = None
        self.reduced = {}

    @staticmethod
    def _views(grads):
        out = []
        for n in MATRIX_NAMES:
            for g in (grads[n] if n in ("ffn_w13", "ffn_w2") else [grads[n]]):
                if n == "ffn_w13":
                    out.append((n, g, "cols"))
                elif n == "w_in":
                    out.append((n, g, "rows"))
                else:
                    out.append((n, g.reshape(N_CHIPS, 2, g.shape[0] // (2 * N_CHIPS), g.shape[1]), "rows"))
        return out

    def layer_done(self, layer, grads):
        assert self.swapping is None
        self.swapping = (layer, self._views(grads))

    def job_a(self):
        if self.swapping is None:
            return None
        _, views = self.swapping
        return _swap_job([g for _, g, _ in views], [k for _, _, k in views])

    def done_a(self, got):
        layer, views = self.swapping
        self.swapping = None
        assert self.exchanging is None
        ss = [_pair_sum(g, r, k, self.c_idx, name=f"rs_pair_sum_l{layer}_{i}")
              for i, ((_, g, k), r) in enumerate(zip(views, got))]
        self.exchanging = (layer, views, ss)

    def job_b(self):
        if self.exchanging is None:
            return None
        _, views, ss = self.exchanging
        return _exchange_job(ss, [k for _, _, k in views])

    def done_b(self, ex):
        layer, views, _ = self.exchanging
        self.exchanging = None
        red = {}
        for i, ((n, _, _), e) in enumerate(zip(views, ex)):
            red.setdefault(n, []).append(_sum_slots(e[None], name=f"rs_chip_sum_l{layer}_{i}")[0])
        self.reduced[layer] = red

    def flush(self):
        if self.swapping is not None:
            self.done_a(_run_job(self.job_a(), name="rs_swap_halves_last"))
        if self.exchanging is not None:
            self.done_b(_run_job(self.job_b(), name="rs_exchange_chips_last"))

    def result(self):
        self.flush()
        own = [jnp.stack([f for layer in sorted(self.reduced) for f in self.reduced[layer][n]]) for n in MATRIX_NAMES]
        other = _send_halves(own, name="rs_send_halves")
        return {n: (a, b) for n, a, b in zip(MATRIX_NAMES, own, other)}


def _reduce_small(grads):
    flat = [grads[n].astype(F32).reshape(-1) for n in SMALL_NAMES]
    total = sum(f.shape[0] for f in flat)
    rows = -(-total // (16 * PACK_LANES)) * 16
    vec = jnp.concatenate(flat + [jnp.zeros((rows * PACK_LANES - total,), F32)]).reshape(rows, PACK_LANES)
    s = _sum_slots(_allgather_devices(vec, name="small_allgather")[None], name="small_sum").reshape(-1)
    out, o = {}, 0
    for n, f in zip(SMALL_NAMES, flat):
        out[n] = s[o:o + f.shape[0]].reshape(grads[n].shape)
        o += f.shape[0]
    return out


def kernel(x, rel_bias, ln_g, ln_b, ffn_w13, ffn_w2, w_in, conv_w, a_log, dt_bias, dn_norm_g, sinks, w_branch_a, w_branch_b, w_out, loss_target, m_rel_bias, m_ln_g, m_ln_b, m_ffn_w13, m_ffn_w2, m_w_in, m_conv_w, m_a_log, m_dt_bias, m_dn_norm_g, m_sinks, m_w_branch_a, m_w_branch_b, m_w_out, v_rel_bias, v_ln_g, v_ln_b, v_ffn_w13, v_ffn_w2, v_w_in, v_conv_w, v_a_log, v_dt_bias, v_dn_norm_g, v_sinks, v_w_branch_a, v_w_branch_b, v_w_out):
    w = dict(rel_bias=rel_bias, ln_g=ln_g, ln_b=ln_b, ffn_w13=ffn_w13, ffn_w2=ffn_w2, w_in=w_in, conv_w=conv_w,
             a_log=a_log, dt_bias=dt_bias, dn_norm_g=dn_norm_g, sinks=sinks, w_branch_a=w_branch_a,
             w_branch_b=w_branch_b, w_out=w_out)
    m = dict(rel_bias=m_rel_bias, ln_g=m_ln_g, ln_b=m_ln_b, ffn_w13=m_ffn_w13, ffn_w2=m_ffn_w2, w_in=m_w_in,
             conv_w=m_conv_w, a_log=m_a_log, dt_bias=m_dt_bias, dn_norm_g=m_dn_norm_g, sinks=m_sinks,
             w_branch_a=m_w_branch_a, w_branch_b=m_w_branch_b, w_out=m_w_out)
    v = dict(rel_bias=v_rel_bias, ln_g=v_ln_g, ln_b=v_ln_b, ffn_w13=v_ffn_w13, ffn_w2=v_ffn_w2, w_in=v_w_in,
             conv_w=v_conv_w, a_log=v_a_log, dt_bias=v_dt_bias, dn_norm_g=v_dn_norm_g, sinks=v_sinks,
             w_branch_a=v_w_branch_a, w_branch_b=v_w_branch_b, w_out=v_w_out)

    depth = w_in.shape[0]
    sharded = [n for n in WEIGHT_NAMES if SHARD_AXIS[n] is not None]

    def shards_of(i):
        out = {}
        for n in sharded:
            s = w[n][i].astype(MXU_DTYPE) if n in MATRIX_NAMES else w[n][i]
            if n in ("ffn_w13", "ffn_w2"):
                out[n + "_0"], out[n + "_1"] = s[0], s[1]
            elif n == "w_in":
                out[n + "_0"], out[n + "_1"] = s[:s.shape[0] // 2], s[s.shape[0] // 2:]
            else:
                out[n] = s
        return out

    w_in_halves = [{} for _ in range(depth)]

    def assemble(i, gathered):
        lw = {}
        for n, g in gathered.items():
            if n[:-2] in ("ffn_w13", "ffn_w2"):
                lw.setdefault(n[:-2], {})[int(n[-1])] = _unshard(g, SHARD_AXIS[n[:-2]] - 2)
            elif n[:-2] == "w_in":
                w_in_halves[i][n] = g
                if len(w_in_halves[i]) == 2:
                    whole = jnp.concatenate([w_in_halves[i]["w_in_0"], w_in_halves[i]["w_in_1"]], axis=1)
                    lw["w_in_p"] = _repack_w_in(_unshard(whole, SHARD_AXIS["w_in"] - 1))
            else:
                lw[n] = _unshard(g, SHARD_AXIS[n] - 1)
        return lw

    layer_wts = [{n: w[n][i] for n in ("a_log", "dt_bias", "dn_norm_g", "sinks")} for i in range(depth)]
    shards = [shards_of(i) for i in range(depth)]
    first = ("ffn_w13_0", "ffn_w2_0", "ln_g", "ln_b")
    got = _run_job(_gather_job([shards[0].pop(n) for n in first]), name="weights_allgather_first")
    layer_wts[0].update(assemble(0, dict(zip(first, got))))
    c_idx = lax.axis_index("c").astype(jnp.int32).reshape(1)
    reducer = _GradReducer(c_idx)
    loss_part, grad_x, grads = _local_step(x, loss_target, rel_bias, layer_wts, side_shards=shards,
                                           side_assemble=assemble, reducer=reducer)
    loss = lax.psum(loss_part, ("x", "y", "c"))

    halves = reducer.result()
    chip = _chip_index(lax.axis_index("x"), lax.axis_index("y"))
    small = _reduce_small(grads)
    outs = {}
    for n in WEIGHT_NAMES:
        if n in MATRIX_NAMES:
            outs[n] = _adamw_halves(w[n], m[n], v[n], *halves[n], c_idx, name="adamw_" + n)
        else:
            axis = SHARD_AXIS[n]
            g = small[n]
            if axis is not None:
                g = lax.dynamic_slice_in_dim(g, chip * w[n].shape[axis], w[n].shape[axis], axis)
            outs[n] = (g,) + _adamw(w[n], g, m[n], v[n], name="adamw_" + n)
    return (loss, grad_x, *[outs[n][0] for n in WEIGHT_NAMES], *[outs[n][1] for n in WEIGHT_NAMES],
            *[outs[n][2] for n in WEIGHT_NAMES], *[outs[n][3] for n in WEIGHT_NAMES])
```

```python
import functools
import math

import numpy as np
import jax
import jax.numpy as jnp
from jax import lax
from jax.experimental import pallas as pl
from jax.experimental.pallas import tpu as pltpu

F32 = jnp.float32
BF16 = jnp.bfloat16
MXU_DTYPE = BF16
HIGHEST = lax.Precision.HIGHEST

D_MODEL = 1024
N_HEADS_A = 16
N_KV_A = 4
HEAD_DIM_A = 64
GROUP_A = N_HEADS_A // N_KV_A
WINDOW = 128
N_HEADS_B = 8
KEY_DIM_B = 128
VAL_DIM_B = 128
CONV_K = 4
CHUNK = 64
D_FF = 2816
NUM_BUCKETS = 32
MAX_DISTANCE = 128
DEPTH = 4
DN_ALPHA = (2 * DEPTH) ** 0.25
LN_EPS = 1e-5
NORM_EPS = 1e-6
NEG_INF = -1e30

Q_A = N_HEADS_A * HEAD_DIM_A
KV_W = N_KV_A * HEAD_DIM_A
QK_B = N_HEADS_B * KEY_DIM_B
V_B = N_HEADS_B * VAL_DIM_B
CONV_CH = 2 * QK_B + V_B
N_IN = Q_A + 2 * KV_W + CONV_CH + 2 * N_HEADS_B + V_B + 2 * D_MODEL

ADAM_LR = 0.001
ADAM_B1 = 0.9
ADAM_B2 = 0.999
ADAM_EPS = 1e-08
ADAM_WD = 0.01
ADAM_STEP = 10

HC_W = 8192
HC_Q = 0
HC_Z = 1024
HC_K = 2048
HC_V = 2304
HC_BD = 2560
HC_CONV = 3072
HC_GATE = 6144

GROUP_T = 256
DN_BWD_HEADS = 4
ROW_T = 256
VMEM_LIMIT_BYTES = 48 * 1024 * 1024


def _params(vmem=VMEM_LIMIT_BYTES, **kw):
    return pltpu.CompilerParams(vmem_limit_bytes=vmem, **kw)


def _tile(n, cands):
    for c in cands:
        if n % c == 0:
            return c
    return n


def _dot(a, b, dims=(((1,), (0,)), ((), ())), exact=False):
    if exact:
        return lax.dot_general(a.astype(F32), b.astype(F32), dims, precision=HIGHEST, preferred_element_type=F32)
    return lax.dot_general(a.astype(MXU_DTYPE), b.astype(MXU_DTYPE), dims, preferred_element_type=F32)


_NN = (((1,), (0,)), ((), ()))
_NT = (((1,), (1,)), ((), ()))
_TN = (((0,), (0,)), ((), ()))


def _sigmoid(x):
    return 1.0 / (1.0 + jnp.exp(-x))


def _silu(x):
    return x * _sigmoid(x)


def _dsilu(x):
    s = _sigmoid(x)
    return s * (1.0 + x * (1.0 - s))


def _call(body, *, name, grid, in_specs, out_specs, out_shape, scratch_shapes=(), semantics, args, side=None,
          vmem=VMEM_LIMIT_BYTES):
    in_specs, out_specs, out_shape = list(in_specs), list(out_specs), list(out_shape)
    if side is None:
        outs = pl.pallas_call(body, name=name, grid=grid, in_specs=in_specs, out_specs=out_specs, out_shape=out_shape,
                              scratch_shapes=list(scratch_shapes),
                              compiler_params=_params(vmem=vmem, dimension_semantics=semantics))(*args)
        return outs, None
    n_in, n_out, n_scr = len(in_specs), len(out_specs), len(scratch_shapes)
    s_in, s_out = len(side["ins"]), len(side["out_shape"])

    def hosted(*refs):
        main_in, side_in = refs[:n_in], refs[n_in:n_in + s_in]
        o0 = n_in + s_in
        main_out, side_out = refs[o0:o0 + n_out], refs[o0 + n_out:o0 + n_out + s_out]
        rest = refs[o0 + n_out + s_out:]
        copies = side["make"](side_in, side_out, *rest[n_scr:])
        ids = [pl.program_id(d) for d in range(len(grid))]
        first = functools.reduce(jnp.logical_and, [i == 0 for i in ids])
        last = functools.reduce(jnp.logical_and, [i == g - 1 for i, g in zip(ids, grid)])

        @pl.when(first)
        def _():
            for cp in copies:
                cp.start()

        body(*main_in, *main_out, *rest[:n_scr])

        @pl.when(last)
        def _():
            for cp in copies:
                cp.wait()

    outs = pl.pallas_call(
        hosted, name=name, grid=grid,
        in_specs=in_specs + [HBM_SPEC] * s_in, out_specs=out_specs + [HBM_SPEC] * s_out,
        out_shape=out_shape + list(side["out_shape"]),
        scratch_shapes=list(scratch_shapes) + list(side["scratch"]),
        compiler_params=_params(vmem=vmem, dimension_semantics=("arbitrary",) * len(grid)),
    )(*args, *side["ins"])
    return list(outs[:n_out]), list(outs[n_out:])


def _mm(a, b, *, ta=False, tb=False, a_halves=False, b_halves=False, out_dtype=F32, add=None, exact=False, side=None,
        name):
    if a_halves:
        m, kdim = a.shape[1], 2 * a.shape[2]
    else:
        (kdim, m) = a.shape if ta else a.shape[::-1]
    if b_halves:
        kb, n = b.shape[1], 2 * b.shape[2]
    else:
        (n, kb) = b.shape if tb else b.shape[::-1]
    assert kdim == kb, (a.shape, b.shape, ta, tb)
    if ta and not tb and not exact and add is None and side is None and kdim <= TN_WHOLE_K:
        return _mm_tn(a, b, b_halves=b_halves, out_dtype=out_dtype, name=name)
    tn = _tile(n, (1024, 1408, 512, 256, 128))
    tk = _tile(kdim, (1024, 1408, 512, 256, 128))
    nk = kdim // tk
    tm = _tile(m, (1024, 1408, 512, 256, 128) if nk > 1 else (512, 256, 128))
    nj = n // tn
    dims = (((0 if ta else 1,), (1 if tb else 0,)), ((), ()))
    has_add = add is not None

    def body(*refs):
        if has_add:
            a_ref, b_ref, add_ref, o_ref = refs[:4]
        else:
            a_ref, b_ref, o_ref = refs[:3]
            add_ref = None
        part = _dot(a_ref[...], b_ref[...], dims, exact)

        def finish(acc):
            if has_add:
                acc = acc + add_ref[...].astype(F32)
            o_ref[...] = acc.astype(out_dtype)

        if nk == 1:
            finish(part)
        else:
            acc_ref = refs[-1]
            k = pl.program_id(2)

            @pl.when(k == 0)
            def _():
                acc_ref[...] = part

            @pl.when(k > 0)
            def _():
                acc_ref[...] += part

            @pl.when(k == nk - 1)
            def _():
                finish(acc_ref[...])

    if a_halves:
        assert not ta and nk % 2 == 0
        a_spec = pl.BlockSpec((None, tm, tk), lambda i, j, k: (k // (nk // 2), i, k % (nk // 2)))
    elif ta:
        a_spec = pl.BlockSpec((tk, tm), lambda i, j, k: (k, i))
    else:
        a_spec = pl.BlockSpec((tm, tk), lambda i, j, k: (i, k))
    if b_halves:
        assert not tb and nj % 2 == 0
        b_spec = pl.BlockSpec((None, tk, tn), lambda i, j, k: (j // (nj // 2), k, j % (nj // 2)))
    elif tb:
        b_spec = pl.BlockSpec((tn, tk), lambda i, j, k: (j, k))
    else:
        b_spec = pl.BlockSpec((tk, tn), lambda i, j, k: (k, j))
    o_spec = pl.BlockSpec((tm, tn), lambda i, j, k: (i, j))
    in_specs = [a_spec, b_spec] + ([o_spec] if has_add else [])
    args = (a, b) + ((add,) if has_add else ())
    outs, side_outs = _call(
        body, name=name, grid=(m // tm, nj, nk), in_specs=in_specs, out_specs=[o_spec],
        out_shape=[jax.ShapeDtypeStruct((m, n), out_dtype)],
        scratch_shapes=[pltpu.VMEM((tm, tn), F32)] if nk > 1 else [],
        semantics=("parallel", "parallel", "arbitrary"), args=args, side=side)
    return outs[0] if side is None else (outs[0], side_outs)


TN_WHOLE_K = 8192
VMEM_LIMIT_BIG_BYTES = 56 * 1024 * 1024


def _mm_tn(a, b, *, b_halves=False, out_dtype=F32, name):
    kdim, m = a.shape
    n = 2 * b.shape[2] if b_halves else b.shape[1]
    if m <= 1024:
        tm, tn = m, _tile(n // 2 if b_halves else n, (256, 128))
    else:
        tm, tn = _tile(m, (256, 128)), _tile(n, (1024, 512, 256, 128))
    nj = n // tn

    def body(a_ref, b_ref, o_ref):
        o_ref[...] = _dot(a_ref[...], b_ref[...], _TN).astype(out_dtype)

    if b_halves:
        b_spec = pl.BlockSpec((None, kdim, tn), lambda i, j: (j // (nj // 2), 0, j % (nj // 2)))
    else:
        b_spec = pl.BlockSpec((kdim, tn), lambda i, j: (0, j))
    return pl.pallas_call(
        body, name=name, grid=(m // tm, nj),
        in_specs=[pl.BlockSpec((kdim, tm), lambda i, j: (0, i)), b_spec],
        out_specs=pl.BlockSpec((tm, tn), lambda i, j: (i, j)),
        out_shape=jax.ShapeDtypeStruct((m, n), out_dtype),
        compiler_params=pltpu.CompilerParams(vmem_limit_bytes=VMEM_LIMIT_BIG_BYTES,
                                             dimension_semantics=("parallel", "parallel")),
    )(a, b)


def _mm_w(a, w, *, side=None, name):
    m, kdim = a.shape
    n = w.shape[1]
    tm = _tile(m, (256, 128))

    def body(a_ref, w_ref, o_ref):
        o_ref[...] = _dot(a_ref[...], w_ref[...])

    outs, side_outs = _call(
        body, name=name, grid=(m // tm,),
        in_specs=[pl.BlockSpec((tm, kdim), lambda i: (i, 0)),
                  pl.BlockSpec((kdim, n), lambda i: (0, 0), pipeline_mode=pl.Buffered(1))],
        out_specs=[pl.BlockSpec((tm, n), lambda i: (i, 0))], out_shape=[jax.ShapeDtypeStruct((m, n), F32)],
        semantics=("parallel",), args=(a, w), side=side, vmem=VMEM_LIMIT_BIG_BYTES)
    return outs[0], side_outs


def _mm_wt(a, w, add, *, a_halves=False, name):
    n, kdim = w.shape
    m = a.shape[1] if a_halves else a.shape[0]
    tm = _tile(m, (512, 256, 128))
    half = kdim // 2

    def body(a_ref, w_ref, add_ref, o_ref):
        if a_halves:
            acc = _dot(a_ref[0], w_ref[:, :half], _NT) + _dot(a_ref[1], w_ref[:, half:], _NT)
        else:
            acc = _dot(a_ref[...], w_ref[...], _NT)
        o_ref[...] = acc + add_ref[...]

    a_spec = pl.BlockSpec((2, tm, half), lambda i: (0, i, 0)) if a_halves else pl.BlockSpec((tm, kdim), lambda i: (i, 0))
    row = pl.BlockSpec((tm, n), lambda i: (i, 0))
    return pl.pallas_call(
        body, name=name, grid=(m // tm,),
        in_specs=[a_spec, pl.BlockSpec((n, kdim), lambda i: (0, 0), pipeline_mode=pl.Buffered(1)), row],
        out_specs=row, out_shape=jax.ShapeDtypeStruct((m, n), F32),
        compiler_params=pltpu.CompilerParams(vmem_limit_bytes=VMEM_LIMIT_BIG_BYTES, dimension_semantics=("parallel",)),
    )(a, w, add)


def _layernorm_rows(r, g, b):
    mu = jnp.mean(r, axis=-1, keepdims=True)
    xc = r - mu
    var = jnp.mean(xc * xc, axis=-1, keepdims=True)
    return xc * lax.rsqrt(var + LN_EPS) * g + b


def _mm_res_ln(a, w, resid, g, b, *, alpha, c, name):
    m, kdim = a.shape
    n = w.shape[1]
    tm = _tile(m, (512, 256, 128))

    def body(a_ref, w_ref, x_ref, g_ref, b_ref, r_ref, y_ref, yb_ref):
        f = _dot(a_ref[...], w_ref[...])
        r = alpha * x_ref[...] + c * f
        r_ref[...] = r
        y = _layernorm_rows(r, g_ref[...], b_ref[...])
        y_ref[...] = y
        yb_ref[...] = y.astype(BF16)

    row = pl.BlockSpec((tm, n), lambda i: (i, 0))
    vec = pl.BlockSpec((1, n), lambda i: (0, 0))
    return pl.pallas_call(
        body, name=name, grid=(m // tm,),
        in_specs=[pl.BlockSpec((tm, kdim), lambda i: (i, 0)), pl.BlockSpec((kdim, n), lambda i: (0, 0)), row, vec, vec],
        out_specs=[row, row, row],
        out_shape=[jax.ShapeDtypeStruct((m, n), F32)] * 2 + [jax.ShapeDtypeStruct((m, n), BF16)],
        compiler_params=_params(dimension_semantics=("parallel",)),
    )(a, w, resid, g.reshape(1, n), b.reshape(1, n))


def _ln_bwd(dy, r, g, *, alpha, c, name):
    m, n = dy.shape
    tm = _tile(m, (512, 256, 128))

    def body(dy_ref, r_ref, g_ref, dres_ref, dbr_ref, dg_ref, db_ref):
        i = pl.program_id(0)
        dy_ = dy_ref[...]
        r_ = r_ref[...]
        mu = jnp.mean(r_, axis=-1, keepdims=True)
        xc = r_ - mu
        var = jnp.mean(xc * xc, axis=-1, keepdims=True)
        rstd = lax.rsqrt(var + LN_EPS)
        xh = xc * rstd
        dxh = dy_ * g_ref[...]
        dr = rstd * (dxh - jnp.mean(dxh, axis=-1, keepdims=True) - xh * jnp.mean(dxh * xh, axis=-1, keepdims=True))
        dres_ref[...] = alpha * dr
        dbr_ref[...] = (c * dr).astype(BF16)
        dg_p = jnp.sum(dy_ * xh, axis=0, keepdims=True)
        db_p = jnp.sum(dy_, axis=0, keepdims=True)

        @pl.when(i == 0)
        def _():
            dg_ref[...] = dg_p
            db_ref[...] = db_p

        @pl.when(i > 0)
        def _():
            dg_ref[...] += dg_p
            db_ref[...] += db_p

    row = pl.BlockSpec((tm, n), lambda i: (i, 0))
    vec = pl.BlockSpec((1, n), lambda i: (0, 0))
    return pl.pallas_call(
        body, name=name, grid=(m // tm,),
        in_specs=[row, row, vec], out_specs=[row, row, vec, vec],
        out_shape=[jax.ShapeDtypeStruct((m, n), F32), jax.ShapeDtypeStruct((m, n), BF16),
                   jax.ShapeDtypeStruct((1, n), F32), jax.ShapeDtypeStruct((1, n), F32)],
        compiler_params=_params(dimension_semantics=("arbitrary",)),
    )(dy, r, g.reshape(1, n))


FFN_TN = D_FF // 2


def _ffn_up_act(x, w13, *, side=None, name):
    m, d = x.shape
    tm = _tile(m, (512, 256, 128))
    nj = D_FF // FFN_TN

    def body(x_ref, g_ref, u_ref, o_ref):
        x_ = x_ref[...]
        o_ref[...] = (_silu(_dot(x_, g_ref[...])) * _dot(x_, u_ref[...])).astype(BF16)

    outs, side_outs = _call(
        body, name=name, grid=(nj, m // tm),
        in_specs=[pl.BlockSpec((tm, d), lambda j, i: (i, 0)), pl.BlockSpec((d, FFN_TN), lambda j, i: (0, j)),
                  pl.BlockSpec((d, FFN_TN), lambda j, i: (0, j + nj))],
        out_specs=[pl.BlockSpec((tm, FFN_TN), lambda j, i: (i, j))],
        out_shape=[jax.ShapeDtypeStruct((m, D_FF), BF16)],
        semantics=("parallel", "parallel"), args=(x, w13, w13), side=side)
    return outs[0], side_outs


def _ffn_bwd_mid(x, w13, df, w2, *, side=None, name):
    m, d = x.shape
    tm = _tile(m, (512, 256, 128))
    nj = D_FF // FFN_TN

    def body(x_ref, g_ref, u_ref, df_ref, w2_ref, o_ref):
        x_ = x_ref[...]
        g = _dot(x_, g_ref[...])
        u = _dot(x_, u_ref[...])
        da = _dot(df_ref[...], w2_ref[...], _NT)
        o_ref[0] = (da * u * _dsilu(g)).astype(BF16)
        o_ref[1] = (da * _silu(g)).astype(BF16)

    outs, side_outs = _call(
        body, name=name, grid=(nj, m // tm),
        in_specs=[pl.BlockSpec((tm, d), lambda j, i: (i, 0)), pl.BlockSpec((d, FFN_TN), lambda j, i: (0, j)),
                  pl.BlockSpec((d, FFN_TN), lambda j, i: (0, j + nj)), pl.BlockSpec((tm, d), lambda j, i: (i, 0)),
                  pl.BlockSpec((FFN_TN, d), lambda j, i: (j, 0))],
        out_specs=[pl.BlockSpec((2, tm, FFN_TN), lambda j, i: (0, i, j))],
        out_shape=[jax.ShapeDtypeStruct((2, m, D_FF), BF16)],
        semantics=("parallel", "parallel"), args=(x, w13, w13, df, w2), side=side)
    return outs[0], side_outs


def _t5_bucket_table():
    r = np.arange(WINDOW)[:, None]
    j = np.arange(2 * WINDOW)[None, :]
    rel = r + WINDOW - j
    n = np.maximum(rel, 0)
    max_exact = NUM_BUCKETS // 2
    nf = np.maximum(n, 1).astype(np.float32)
    large = max_exact + (np.log(nf / np.float32(max_exact)) / np.float32(math.log(MAX_DISTANCE / max_exact))
                         * np.float32(NUM_BUCKETS - max_exact)).astype(np.int32)
    large = np.minimum(large, NUM_BUCKETS - 1)
    bucket = np.where(n < max_exact, n, large)
    in_band = (rel >= 0) & (rel < WINDOW)
    return bucket.astype(np.int32), in_band


def _bucket_onehot():
    bucket, _ = _t5_bucket_table()
    oh = np.zeros((WINDOW * 2 * WINDOW, 128), np.float32)
    oh[np.arange(oh.shape[0]), bucket.reshape(-1)] = 1.0
    return oh


def _stack_heads(x, g):
    hd = HEAD_DIM_A
    return jnp.concatenate([x[:, (GROUP_A * g + h) * hd:(GROUP_A * g + h + 1) * hd] for h in range(GROUP_A)], axis=0)


def _unstack_heads(x):
    return jnp.concatenate([x[h * WINDOW:(h + 1) * WINDOW] for h in range(GROUP_A)], axis=1)


def _attn_probs(q_ref, kp_ref, ko_ref, vp_ref, vo_ref, bias_ref, sink_ref, first_block):
    hd = HEAD_DIM_A
    groups = range(N_KV_A)
    q = q_ref[...]
    qs = [_stack_heads(q, g) * (hd ** -0.5) for g in groups]
    k2 = [jnp.concatenate([kp_ref[:, g * hd:(g + 1) * hd], ko_ref[:, g * hd:(g + 1) * hd]], axis=0) for g in groups]
    v2 = [jnp.concatenate([vp_ref[:, g * hd:(g + 1) * hd], vo_ref[:, g * hd:(g + 1) * hd]], axis=0) for g in groups]
    rr = lax.broadcasted_iota(jnp.int32, (GROUP_A * WINDOW, 2 * WINDOW), 0) % WINDOW
    jj = lax.broadcasted_iota(jnp.int32, (GROUP_A * WINDOW, 2 * WINDOW), 1)
    rel = rr + WINDOW - jj
    valid = (rel >= 0) & (rel < WINDOW) & (jnp.logical_not(first_block) | (jj >= WINDOW))
    s = [_dot(qs[g], k2[g], _NT) for g in groups]
    s = [jnp.where(valid, s[g] + bias_ref[GROUP_A * g:GROUP_A * (g + 1)].reshape(GROUP_A * WINDOW, 2 * WINDOW), NEG_INF)
         for g in groups]
    sk = [jnp.concatenate([jnp.broadcast_to(sink_ref[0:1, GROUP_A * g + h:GROUP_A * g + h + 1], (WINDOW, 1))
                           for h in range(GROUP_A)], axis=0) for g in groups]
    mx = [jnp.maximum(jnp.max(s[g], axis=-1, keepdims=True), sk[g]) for g in groups]
    p = [jnp.exp(s[g] - mx[g]) for g in groups]
    ps = [jnp.exp(sk[g] - mx[g]) for g in groups]
    den = [jnp.sum(p[g], axis=-1, keepdims=True) + ps[g] for g in groups]
    return qs, k2, v2, [p[g] / den[g] for g in groups], [ps[g] / den[g] for g in groups]


def _attn_specs(nb):
    def prev(b, i):
        return (b * nb + jnp.maximum(i - 1, 0))

    q_spec = pl.BlockSpec((WINDOW, Q_A), lambda b, i: (b * nb + i, HC_Q // Q_A))
    kp_spec = pl.BlockSpec((WINDOW, KV_W), lambda b, i: (prev(b, i), HC_K // KV_W))
    ko_spec = pl.BlockSpec((WINDOW, KV_W), lambda b, i: (b * nb + i, HC_K // KV_W))
    vp_spec = pl.BlockSpec((WINDOW, KV_W), lambda b, i: (prev(b, i), HC_V // KV_W))
    vo_spec = pl.BlockSpec((WINDOW, KV_W), lambda b, i: (b * nb + i, HC_V // KV_W))
    bias_spec = pl.BlockSpec((N_HEADS_A, WINDOW, 2 * WINDOW), lambda b, i: (0, 0, 0))
    sink_spec = pl.BlockSpec((1, N_HEADS_A), lambda b, i: (0, 0))
    return [q_spec, kp_spec, ko_spec, vp_spec, vo_spec, bias_spec, sink_spec]


def _attn_fwd(hcat, bias, sink, nbatch, *, side=None, name):
    t = hcat.shape[0]
    nb = t // nbatch // WINDOW

    def body(q_ref, kp_ref, ko_ref, vp_ref, vo_ref, bias_ref, sink_ref, o_ref):
        first = pl.program_id(1) == 0
        _, _, v2, p, _ = _attn_probs(q_ref, kp_ref, ko_ref, vp_ref, vo_ref, bias_ref, sink_ref, first)
        o = [_dot(p[g], v2[g]) for g in range(N_KV_A)]
        o_ref[...] = jnp.concatenate([_unstack_heads(og) for og in o], axis=1).astype(BF16)

    outs, side_outs = _call(
        body, name=name, grid=(nbatch, nb), in_specs=_attn_specs(nb),
        out_specs=[pl.BlockSpec((WINDOW, Q_A), lambda b, i: (b * nb + i, 0))],
        out_shape=[jax.ShapeDtypeStruct((t, Q_A), BF16)], semantics=("parallel", "arbitrary"),
        args=(hcat, hcat, hcat, hcat, hcat, bias, sink.reshape(1, N_HEADS_A)), side=side)
    return outs[0], side_outs


def _attn_bwd(hcat, bias, sink, do, nbatch, *, name):
    t = hcat.shape[0]
    nb = t // nbatch // WINDOW
    hd = HEAD_DIM_A

    def body(q_ref, kp_ref, ko_ref, vp_ref, vo_ref, bias_ref, sink_ref, do_ref,
             dq_ref, dk_ref, dv_ref, dbias_ref, dsink_ref, ck_ref, cv_ref):
        b = pl.program_id(0)
        j = pl.program_id(1)
        first = j == nb - 1

        @pl.when((b == 0) & (j == 0))
        def _():
            dbias_ref[...] = jnp.zeros_like(dbias_ref)
            dsink_ref[...] = jnp.zeros_like(dsink_ref)

        @pl.when(j == 0)
        def _():
            ck_ref[...] = jnp.zeros_like(ck_ref)
            cv_ref[...] = jnp.zeros_like(cv_ref)

        do_ = do_ref[...]
        groups = range(N_KV_A)
        lane = lax.broadcasted_iota(jnp.int32, (1, N_HEADS_A), 1)
        qs, k2, v2, p, ps = _attn_probs(q_ref, kp_ref, ko_ref, vp_ref, vo_ref, bias_ref, sink_ref, first)
        dos = [_stack_heads(do_, g) for g in groups]
        dv2 = [_dot(p[g], dos[g], _TN) for g in groups]
        dp = [_dot(dos[g], v2[g], _NT) for g in groups]
        delta = [jnp.sum(p[g] * dp[g], axis=-1, keepdims=True) for g in groups]
        ds = [p[g] * (dp[g] - delta[g]) for g in groups]
        dqs = [_dot(ds[g], k2[g]) * (hd ** -0.5) for g in groups]
        dk2 = [_dot(ds[g], qs[g], _TN) for g in groups]
        dsink = jnp.zeros((1, N_HEADS_A), F32)
        for g in groups:
            dsk = -(ps[g] * delta[g])
            for h in range(GROUP_A):
                tot = jnp.sum(dsk[h * WINDOW:(h + 1) * WINDOW], axis=0, keepdims=True)
                dsink = dsink + jnp.where(lane == GROUP_A * g + h, tot, 0.0)
            dbias_ref[GROUP_A * g:GROUP_A * (g + 1)] += ds[g].reshape(GROUP_A, WINDOW, 2 * WINDOW)
        dq_ref[...] = jnp.concatenate([_unstack_heads(d) for d in dqs], axis=1).astype(BF16)
        dk_ref[...] = (jnp.concatenate([d[WINDOW:] for d in dk2], axis=1) + ck_ref[...]).astype(BF16)
        dv_ref[...] = (jnp.concatenate([d[WINDOW:] for d in dv2], axis=1) + cv_ref[...]).astype(BF16)
        ck_ref[...] = jnp.concatenate([d[:WINDOW] for d in dk2], axis=1)
        cv_ref[...] = jnp.concatenate([d[:WINDOW] for d in dv2], axis=1)
        dsink_ref[...] += dsink

    def rev(spec):
        return pl.BlockSpec(spec.block_shape, lambda b, j, f=spec.index_map: f(b, nb - 1 - j))

    in_specs = [rev(s) for s in _attn_specs(nb)[:5]] + _attn_specs(nb)[5:]
    in_specs.append(pl.BlockSpec((WINDOW, Q_A), lambda b, j: (b * nb + nb - 1 - j, 0)))
    return pl.pallas_call(
        body, name=name, grid=(nbatch, nb),
        in_specs=in_specs,
        out_specs=[pl.BlockSpec((WINDOW, Q_A), lambda b, j: (b * nb + nb - 1 - j, 0)),
                   pl.BlockSpec((WINDOW, KV_W), lambda b, j: (b * nb + nb - 1 - j, 0)),
                   pl.BlockSpec((WINDOW, KV_W), lambda b, j: (b * nb + nb - 1 - j, 0)),
                   pl.BlockSpec((N_HEADS_A, WINDOW, 2 * WINDOW), lambda b, j: (0, 0, 0)),
                   pl.BlockSpec((1, N_HEADS_A), lambda b, j: (0, 0))],
        out_shape=[jax.ShapeDtypeStruct((t, Q_A), BF16), jax.ShapeDtypeStruct((t, KV_W), BF16),
                   jax.ShapeDtypeStruct((t, KV_W), BF16),
                   jax.ShapeDtypeStruct((N_HEADS_A, WINDOW, 2 * WINDOW), F32),
                   jax.ShapeDtypeStruct((1, N_HEADS_A), F32)],
        scratch_shapes=[pltpu.VMEM((WINDOW, KV_W), F32), pltpu.VMEM((WINDOW, KV_W), F32)],
        compiler_params=_params(dimension_semantics=("arbitrary", "arbitrary")),
    )(hcat, hcat, hcat, hcat, hcat, bias, sink.reshape(1, N_HEADS_A), do)


def _shift_down(x, halo8, s):
    if s == 0:
        return x
    rolled = pltpu.roll(x, s, axis=0)
    row8 = lax.broadcasted_iota(jnp.int32, halo8.shape, 0)
    top = jnp.where(row8 < s, pltpu.roll(halo8, s, axis=0), rolled[0:8])
    return top if x.shape[0] == 8 else jnp.concatenate([top, rolled[8:]], axis=0)


def _shift_up(x, halo8, s):
    if s == 0:
        return x
    n = x.shape[0]
    rolled = pltpu.roll(x, n - s, axis=0)
    row8 = lax.broadcasted_iota(jnp.int32, halo8.shape, 0)
    bottom = jnp.where(row8 >= 8 - s, pltpu.roll(halo8, 8 - s, axis=0), rolled[n - 8:n])
    return jnp.concatenate([rolled[:n - 8], bottom], axis=0)


def _l2n(x, scale):
    r = lax.rsqrt(jnp.sum(x * x, axis=-1, keepdims=True) + NORM_EPS)
    return x * (r * scale)


def _conv_prep(hcat, conv_w, nbatch, *, side=None, name):
    t = hcat.shape[0]
    nt = t // nbatch // ROW_T
    cb = HC_CONV // CONV_CH

    def body(u_ref, halo_ref, w_ref, q_ref, k_ref, v_ref):
        i = pl.program_id(1)
        s = _silu(_conv_rows(u_ref[...], jnp.where(i == 0, 0.0, halo_ref[...]), w_ref))
        for h in range(N_HEADS_B):
            lo, hi = h * KEY_DIM_B, (h + 1) * KEY_DIM_B
            q_ref[:, lo:hi] = _l2n(s[:, lo:hi], KEY_DIM_B ** -0.5)
            k_ref[:, lo:hi] = _l2n(s[:, QK_B + lo:QK_B + hi], 1.0)
        v_ref[...] = s[:, 2 * QK_B:]

    row = lambda w: pl.BlockSpec((ROW_T, w), lambda b, i: (b * nt + i, 0))
    outs, side_outs = _call(
        body, name=name, grid=(nbatch, nt),
        in_specs=[pl.BlockSpec((ROW_T, CONV_CH), lambda b, i: (b * nt + i, cb)),
                  pl.BlockSpec((8, CONV_CH), lambda b, i: (jnp.maximum((b * nt + i) * (ROW_T // 8) - 1, 0), cb)),
                  pl.BlockSpec((CONV_K, CONV_CH), lambda b, i: (0, 0))],
        out_specs=[row(QK_B), row(QK_B), row(V_B)],
        out_shape=[jax.ShapeDtypeStruct((t, QK_B), F32)] * 3,
        semantics=("parallel", "parallel"), args=(hcat, hcat, conv_w), side=side)
    return tuple(outs), side_outs


def _conv_rows(u, halo8, w_ref):
    c = jnp.zeros_like(u)
    for j in range(CONV_K):
        c = c + w_ref[j:j + 1, :] * _shift_down(u, halo8, CONV_K - 1 - j)
    return c


def _conv_prep_pointwise_bwd(c, dq, dk, dv):
    def l2n_bwd(x, dy, scale):
        r = lax.rsqrt(jnp.sum(x * x, axis=-1, keepdims=True) + NORM_EPS)
        return scale * (r * dy - x * (r * r * r) * jnp.sum(x * dy, axis=-1, keepdims=True))

    sig = _sigmoid(c)
    s = c * sig
    ds = sig * (1.0 + c * (1.0 - sig))
    parts_q, parts_k = [], []
    for h in range(N_HEADS_B):
        lo, hi = h * KEY_DIM_B, (h + 1) * KEY_DIM_B
        parts_q.append(l2n_bwd(s[:, lo:hi], dq[:, lo:hi], KEY_DIM_B ** -0.5) * ds[:, lo:hi])
        parts_k.append(l2n_bwd(s[:, QK_B + lo:QK_B + hi], dk[:, lo:hi], 1.0) * ds[:, QK_B + lo:QK_B + hi])
    return jnp.concatenate(parts_q + parts_k + [dv * ds[:, 2 * QK_B:]], axis=1)


def _conv_bwd(dq, dk, dv, hcat, conv_w, nbatch, *, name):
    t = dq.shape[0]
    nt = t // nbatch // ROW_T
    cb = HC_CONV // CONV_CH
    last_blk = t // 8 - 1

    def body(dq_ref, dk_ref, dv_ref, dqn_ref, dkn_ref, dvn_ref, u_ref, uprev_ref, unext_ref, w_ref, du_ref, dw_ref):
        b = pl.program_id(0)
        i = pl.program_id(1)
        u = u_ref[...]
        uprev = jnp.where(i == 0, 0.0, uprev_ref[...])
        shifted = [_shift_down(u, uprev, CONV_K - 1 - j) for j in range(CONV_K)]
        c = jnp.zeros_like(u)
        for j in range(CONV_K):
            c = c + w_ref[j:j + 1, :] * shifted[j]
        dc_ = _conv_prep_pointwise_bwd(c, dq_ref[...], dk_ref[...], dv_ref[...])
        c_next = _conv_rows(unext_ref[...], u[ROW_T - 8:], w_ref)
        dnext = _conv_prep_pointwise_bwd(c_next, dqn_ref[...], dkn_ref[...], dvn_ref[...])
        dnext = jnp.where(i == nt - 1, 0.0, dnext)
        du = jnp.zeros_like(dc_)
        rows = []
        for j in range(CONV_K):
            du = du + w_ref[j:j + 1, :] * _shift_up(dc_, dnext, CONV_K - 1 - j)
            rows.append(jnp.sum(dc_ * shifted[j], axis=0, keepdims=True))
        du_ref[...] = du.astype(BF16)
        dw_p = jnp.concatenate(rows + [jnp.zeros((8 - CONV_K, CONV_CH), F32)], axis=0)

        @pl.when((b == 0) & (i == 0))
        def _():
            dw_ref[...] = dw_p

        @pl.when((b > 0) | (i > 0))
        def _():
            dw_ref[...] += dw_p

    def nxt(b, i):
        return jnp.minimum((b * nt + i + 1) * (ROW_T // 8), last_blk)

    row = lambda wd: pl.BlockSpec((ROW_T, wd), lambda b, i: (b * nt + i, 0))
    halo = lambda wd: pl.BlockSpec((8, wd), lambda b, i: (nxt(b, i), 0))
    return pl.pallas_call(
        body, name=name, grid=(nbatch, nt),
        in_specs=[row(QK_B), row(QK_B), row(V_B), halo(QK_B), halo(QK_B), halo(V_B),
                  pl.BlockSpec((ROW_T, CONV_CH), lambda b, i: (b * nt + i, cb)),
                  pl.BlockSpec((8, CONV_CH), lambda b, i: (jnp.maximum((b * nt + i) * (ROW_T // 8) - 1, 0), cb)),
                  pl.BlockSpec((8, CONV_CH), lambda b, i: (nxt(b, i), cb)),
                  pl.BlockSpec((CONV_K, CONV_CH), lambda b, i: (0, 0))],
        out_specs=[pl.BlockSpec((ROW_T, CONV_CH), lambda b, i: (b * nt + i, 0)),
                   pl.BlockSpec((8, CONV_CH), lambda b, i: (0, 0))],
        out_shape=[jax.ShapeDtypeStruct((t, CONV_CH), BF16), jax.ShapeDtypeStruct((8, CONV_CH), F32)],
        compiler_params=_params(dimension_semantics=("arbitrary", "arbitrary")),
    )(dq, dk, dv, dq, dk, dv, hcat, hcat, hcat, conv_w)


def _softplus(x):
    return jnp.maximum(x, 0.0) + jnp.log(1.0 + jnp.exp(-jnp.abs(x)))


def _gates(hcat, a_row, dt_row, *, name):
    t = hcat.shape[0]

    def body(bd_ref, a_ref, dt_ref, gb_ref, bb_ref):
        bd = bd_ref[...]
        beta = _sigmoid(bd)
        g = -jnp.exp(a_ref[...]) * _softplus(bd + dt_ref[...])
        for h in range(N_HEADS_B):
            lo, hi = h * VAL_DIM_B, (h + 1) * VAL_DIM_B
            bb_ref[:, lo:hi] = jnp.broadcast_to(beta[:, h:h + 1], (ROW_T, VAL_DIM_B))
            gb_ref[:, lo:hi] = jnp.broadcast_to(g[:, N_HEADS_B + h:N_HEADS_B + h + 1], (ROW_T, VAL_DIM_B))

    vec = pl.BlockSpec((1, 128), lambda i: (0, 0))
    row = pl.BlockSpec((ROW_T, V_B), lambda i: (i, 0))
    return pl.pallas_call(
        body, name=name, grid=(t // ROW_T,),
        in_specs=[pl.BlockSpec((ROW_T, 128), lambda i: (i, HC_BD // 128)), vec, vec],
        out_specs=[row, row], out_shape=[jax.ShapeDtypeStruct((t, V_B), F32)] * 2,
        compiler_params=_params(dimension_semantics=("parallel",)),
    )(hcat, a_row, dt_row)


def _gates_bwd(dgb, dbb, hcat, a_row, dt_row, *, name):
    t = hcat.shape[0]

    def body(dgb_ref, dbb_ref, bd_ref, a_ref, dt_ref, dbd_ref, da_ref, ddt_ref):
        i = pl.program_id(0)
        bd = bd_ref[...]
        beta = _sigmoid(bd)
        ea = jnp.exp(a_ref[...])
        x = bd + dt_ref[...]
        g = -ea * _softplus(x)
        lane = lax.broadcasted_iota(jnp.int32, (ROW_T, 128), 1)
        dbeta = jnp.zeros((ROW_T, 128), F32)
        dg = jnp.zeros((ROW_T, 128), F32)
        for h in range(N_HEADS_B):
            lo, hi = h * VAL_DIM_B, (h + 1) * VAL_DIM_B
            dbeta = dbeta + jnp.where(lane == h, jnp.sum(dbb_ref[:, lo:hi], axis=-1, keepdims=True), 0.0)
            dg = dg + jnp.where(lane == N_HEADS_B + h, jnp.sum(dgb_ref[:, lo:hi], axis=-1, keepdims=True), 0.0)
        ddt_raw = dg * (-ea) * _sigmoid(x)
        dbd_ref[...] = (dbeta * beta * (1.0 - beta) + ddt_raw).astype(BF16)
        da_p = jnp.sum(dg * g, axis=0, keepdims=True)
        ddt_p = jnp.sum(ddt_raw, axis=0, keepdims=True)

        @pl.when(i == 0)
        def _():
            da_ref[...] = da_p
            ddt_ref[...] = ddt_p

        @pl.when(i > 0)
        def _():
            da_ref[...] += da_p
            ddt_ref[...] += ddt_p

    vec = pl.BlockSpec((1, 128), lambda i: (0, 0))
    row = pl.BlockSpec((ROW_T, V_B), lambda i: (i, 0))
    return pl.pallas_call(
        body, name=name, grid=(t // ROW_T,),
        in_specs=[row, row, pl.BlockSpec((ROW_T, 128), lambda i: (i, HC_BD // 128)), vec, vec],
        out_specs=[pl.BlockSpec((ROW_T, 128), lambda i: (i, 0)), vec, vec],
        out_shape=[jax.ShapeDtypeStruct((t, 128), BF16), jax.ShapeDtypeStruct((1, 128), F32),
                   jax.ShapeDtypeStruct((1, 128), F32)],
        compiler_params=_params(dimension_semantics=("arbitrary",)),
    )(dgb, dbb, hcat, a_row, dt_row)


def _group_masks():
    r = lax.broadcasted_iota(jnp.int32, (GROUP_T, GROUP_T), 0)
    c = lax.broadcasted_iota(jnp.int32, (GROUP_T, GROUP_T), 1)
    same = (r // CHUNK) == (c // CHUNK)
    return same, same & (r >= c), same & (r > c)


def _split2(a):
    hi = a.astype(MXU_DTYPE)
    return hi, (a - hi.astype(F32)).astype(MXU_DTYPE)


def _dot3(a2, b2, dims=_NN):
    (ah, al), (bh, bl) = a2, b2
    d = functools.partial(lax.dot_general, dimension_numbers=dims, preferred_element_type=F32)
    return d(ah, bh) + (d(ah, bl) + d(al, bh))


def _inv_unit_lower(lows):
    shape = lows[0].shape
    eye = (lax.broadcasted_iota(jnp.int32, shape, 0) == lax.broadcasted_iota(jnp.int32, shape, 1)).astype(F32)
    p2 = [_split2(-low) for low in lows]
    ts = [eye - low for low in lows]
    for _ in range(int(math.log2(CHUNK)) - 1):
        p2 = [_split2(_dot3(p, p)) for p in p2]
        ts = [t + _dot3(_split2(t), p) for t, p in zip(ts, p2)]
    return ts


@jax.custom_vjp
def _inv_saved(low, t):
    return t


def _inv_saved_fwd(low, t):
    return t, t


def _inv_saved_bwd(t, dt):
    t2 = _split2(t)
    return -_dot3(t2, _split2(_dot3(_split2(dt), t2, _NT)), _TN), jnp.zeros_like(t)


_inv_saved.defvjp(_inv_saved_fwd, _inv_saved_bwd)


def _mask_dot(mask, x, dims):
    m = mask.astype(MXU_DTYPE)
    hi = x.astype(MXU_DTYPE)
    r1 = x - hi.astype(F32)
    mid = r1.astype(MXU_DTYPE)
    lo = (r1 - mid.astype(F32)).astype(MXU_DTYPE)
    d = functools.partial(lax.dot_general, dimension_numbers=dims, preferred_element_type=F32)
    return d(m, hi) + (d(m, mid) + d(m, lo))


@jax.custom_vjp
def _chunk_sums(gb):
    same, causal, _ = _group_masks()
    return _mask_dot(causal, gb, _NN), _mask_dot(same, gb, _NN)


def _chunk_sums_fwd(gb):
    return _chunk_sums(gb), None


def _chunk_sums_bwd(_, cot):
    same, causal, _ = _group_masks()
    return (_mask_dot(causal, cot[0], _TN) + _mask_dot(same, cot[1], _TN),)


_chunk_sums.defvjp(_chunk_sums_fwd, _chunk_sums_bwd)


def _fold_blocks(m):
    return m[:, 0:CHUNK] + m[:, CHUNK:2 * CHUNK] + m[:, 2 * CHUNK:3 * CHUNK] + m[:, 3 * CHUNK:4 * CHUNK]


def _dn_prep_heads(q, k, v, gb, bb, tsaved=None):
    same, causal, strict = _group_masks()
    heads = range(len(q))
    sums = [_chunk_sums(gb[h]) for h in heads]
    gc = [s[0] for s in sums]
    glast = [s[1] for s in sums]
    decay = [jnp.exp(jnp.where(causal, gc[h][:, 0:1] - gc[h].T[0:1, :], NEG_INF)) for h in heads]
    kb = [k[h] * bb[h] for h in heads]
    vb = [v[h] * bb[h] for h in heads]
    lower = [jnp.where(strict, _dot(kb[h], k[h], _NT) * decay[h], 0.0) for h in heads]
    if tsaved is None:
        tinv = _inv_unit_lower(lower)
    else:
        tinv = [_inv_saved(lower[h], jnp.where(same, jnp.concatenate([tsaved[h]] * (GROUP_T // CHUNK), axis=1), 0.0))
                for h in heads]
    egc = [jnp.exp(gc[h]) for h in heads]
    u = [_dot(tinv[h], vb[h]) for h in heads]
    w = [_dot(tinv[h], kb[h] * egc[h]) for h in heads]
    a = [_fold_blocks(jnp.where(causal, _dot(q[h], k[h], _NT) * decay[h], 0.0)) for h in heads]
    k_tail = [k[h] * jnp.exp(glast[h] - gc[h]) for h in heads]
    q_dec = [q[h] * egc[h] for h in heads]
    return u, w, q_dec, k_tail, a, glast, [_fold_blocks(t) for t in tinv]


def _head_slices(ref, width):
    return [ref[:, h * width:(h + 1) * width] for h in range(N_HEADS_B)]


def _store_heads(ref, vals, width):
    for h, val in enumerate(vals):
        ref[:, h * width:(h + 1) * width] = val


def _dn_prep(q, k, v, gb, bb, *, side=None, name):
    t = q.shape[0]

    def body(q_ref, k_ref, v_ref, gb_ref, bb_ref, u_ref, w_ref, qd_ref, kt_ref, a_ref, gl_ref, ti_ref):
        outs = _dn_prep_heads(*[_head_slices(r, KEY_DIM_B) for r in (q_ref, k_ref, v_ref, gb_ref, bb_ref)])
        for ref, vals in zip((u_ref, w_ref, qd_ref, kt_ref, a_ref, gl_ref, ti_ref), outs):
            _store_heads(ref, vals, vals[0].shape[1])

    row = pl.BlockSpec((GROUP_T, V_B), lambda i: (i, 0))
    arow = pl.BlockSpec((GROUP_T, N_HEADS_B * CHUNK), lambda i: (i, 0))
    big = jax.ShapeDtypeStruct((t, V_B), F32)
    small = jax.ShapeDtypeStruct((t, N_HEADS_B * CHUNK), F32)
    outs, side_outs = _call(
        body, name=name, grid=(t // GROUP_T,), in_specs=[row] * 5, out_specs=[row, row, row, row, arow, row, arow],
        out_shape=[big, big, big, big, small, big, small], semantics=("parallel",), args=(q, k, v, gb, bb), side=side)
    return tuple(outs), side_outs


def _dn_prep_bwd(q, k, v, gb, bb, ti, du, dw, dqd, dkt, da, dgl, *, side=None, name):
    t = q.shape[0]

    def body(q_ref, k_ref, v_ref, gb_ref, bb_ref, ti_ref, du_ref, dw_ref, dqd_ref, dkt_ref, da_ref, dgl_ref,
             dq_ref, dk_ref, dv_ref, dgb_ref, dbb_ref):
        for lo in range(0, N_HEADS_B, DN_BWD_HEADS):
            grp = slice(lo, lo + DN_BWD_HEADS)
            tsaved = _head_slices(ti_ref, CHUNK)[grp]
            _, vjp = jax.vjp(lambda *a, ts=tsaved: _dn_prep_heads(*a, tsaved=ts)[:6],
                             *[_head_slices(r, KEY_DIM_B)[grp] for r in (q_ref, k_ref, v_ref, gb_ref, bb_ref)])
            cot = tuple(_head_slices(r, CHUNK if r is da_ref else KEY_DIM_B)[grp]
                        for r in (du_ref, dw_ref, dqd_ref, dkt_ref, da_ref, dgl_ref))
            for ref, vals in zip((dq_ref, dk_ref, dv_ref, dgb_ref, dbb_ref), vjp(cot)):
                for h, val in enumerate(vals):
                    ref[:, (lo + h) * KEY_DIM_B:(lo + h + 1) * KEY_DIM_B] = val

    row = pl.BlockSpec((GROUP_T, V_B), lambda i: (i, 0))
    arow = pl.BlockSpec((GROUP_T, N_HEADS_B * CHUNK), lambda i: (i, 0))
    big = jax.ShapeDtypeStruct((t, V_B), F32)
    outs, side_outs = _call(
        body, name=name, grid=(t // GROUP_T,),
        in_specs=[row] * 5 + [arow] + [row] * 4 + [arow, row], out_specs=[row] * 5, out_shape=[big] * 5,
        semantics=("parallel",), args=(q, k, v, gb, bb, ti, du, dw, dqd, dkt, da, dgl), side=side)
    return tuple(outs), side_outs


def _dn_steps(s, qd, kt, u, w, a, gl):
    heads = range(len(s))
    v_new = [u[h] - _dot(w[h], s[h]) for h in heads]
    qs = [_dot(qd[h], s[h]) for h in heads]
    o = [qs[h] + _dot(a[h], v_new[h]) for h in heads]
    s_new = [s[h] * jnp.exp(gl[h][0:1, :]) + _dot(kt[h], v_new[h], _TN) for h in heads]
    return s_new, o


def _dn_scan(u, w, qd, kt, a, gl, nbatch, *, side=None, name):
    t = u.shape[0]
    ng = t // nbatch // GROUP_T
    cpg = GROUP_T // CHUNK

    def body(u_ref, w_ref, qd_ref, kt_ref, a_ref, gl_ref, o_ref, ss_ref, s_ref):
        @pl.when(pl.program_id(1) == 0)
        def _():
            s_ref[...] = jnp.zeros_like(s_ref)

        def chunk(c, carry):
            rows = pl.ds(pl.multiple_of(c * CHUNK, CHUNK), CHUNK)
            heads = range(N_HEADS_B)
            s = [s_ref[h] for h in heads]
            for h in heads:
                ss_ref[c, h] = s[h]
            s_new, o = _dn_steps(s, *[[r[rows, h * wd:(h + 1) * wd] for h in heads] for r, wd in
                                      ((qd_ref, KEY_DIM_B), (kt_ref, KEY_DIM_B), (u_ref, VAL_DIM_B), (w_ref, KEY_DIM_B),
                                       (a_ref, CHUNK), (gl_ref, VAL_DIM_B))])
            for h in heads:
                s_ref[h] = s_new[h]
                o_ref[rows, h * VAL_DIM_B:(h + 1) * VAL_DIM_B] = o[h]
            return carry

        lax.fori_loop(0, cpg, chunk, 0)

    row = pl.BlockSpec((GROUP_T, V_B), lambda b, i: (b * ng + i, 0))
    arow = pl.BlockSpec((GROUP_T, N_HEADS_B * CHUNK), lambda b, i: (b * ng + i, 0))
    outs, side_outs = _call(
        body, name=name, grid=(nbatch, ng),
        in_specs=[row, row, row, row, arow, row],
        out_specs=[row, pl.BlockSpec((cpg, N_HEADS_B, KEY_DIM_B, VAL_DIM_B), lambda b, i: (b * ng + i, 0, 0, 0))],
        out_shape=[jax.ShapeDtypeStruct((t, V_B), F32),
                   jax.ShapeDtypeStruct((t // CHUNK, N_HEADS_B, KEY_DIM_B, VAL_DIM_B), F32)],
        scratch_shapes=[pltpu.VMEM((N_HEADS_B, KEY_DIM_B, VAL_DIM_B), F32)],
        semantics=("parallel", "arbitrary"), args=(u, w, qd, kt, a, gl), side=side)
    return tuple(outs), side_outs


def _dn_scan_bwd(u, w, qd, kt, a, gl, ss, do, nbatch, *, name):
    t = u.shape[0]
    ng = t // nbatch // GROUP_T
    cpg = GROUP_T // CHUNK

    def body(u_ref, w_ref, qd_ref, kt_ref, a_ref, gl_ref, ss_ref, do_ref,
             du_ref, dw_ref, dqd_ref, dkt_ref, da_ref, dgl_ref, ds_ref):
        @pl.when(pl.program_id(1) == 0)
        def _():
            ds_ref[...] = jnp.zeros_like(ds_ref)

        def chunk(cc, carry):
            c = cpg - 1 - cc
            rows = pl.ds(pl.multiple_of(c * CHUNK, CHUNK), CHUNK)
            heads = range(N_HEADS_B)
            ins = ((qd_ref, KEY_DIM_B), (kt_ref, KEY_DIM_B), (u_ref, VAL_DIM_B), (w_ref, KEY_DIM_B), (a_ref, CHUNK),
                   (gl_ref, VAL_DIM_B))
            _, vjp = jax.vjp(_dn_steps, [ss_ref[c, h] for h in heads],
                             *[[r[rows, h * wd:(h + 1) * wd] for h in heads] for r, wd in ins])
            grads = vjp(([ds_ref[h] for h in heads], [do_ref[rows, h * VAL_DIM_B:(h + 1) * VAL_DIM_B] for h in heads]))
            for h in heads:
                ds_ref[h] = grads[0][h]
            outs = ((dqd_ref, KEY_DIM_B), (dkt_ref, KEY_DIM_B), (du_ref, VAL_DIM_B), (dw_ref, KEY_DIM_B), (da_ref, CHUNK),
                    (dgl_ref, VAL_DIM_B))
            for (r, wd), vals in zip(outs, grads[1:]):
                for h in heads:
                    r[rows, h * wd:(h + 1) * wd] = vals[h]
            return carry

        lax.fori_loop(0, cpg, chunk, 0)

    row = pl.BlockSpec((GROUP_T, V_B), lambda b, j: (b * ng + ng - 1 - j, 0))
    arow = pl.BlockSpec((GROUP_T, N_HEADS_B * CHUNK), lambda b, j: (b * ng + ng - 1 - j, 0))
    big = jax.ShapeDtypeStruct((t, V_B), F32)
    return pl.pallas_call(
        body, name=name, grid=(nbatch, ng),
        in_specs=[row, row, row, row, arow, row,
                  pl.BlockSpec((cpg, N_HEADS_B, KEY_DIM_B, VAL_DIM_B), lambda b, j: (b * ng + ng - 1 - j, 0, 0, 0)), row],
        out_specs=[row, row, row, row, arow, row],
        out_shape=[big, big, big, big, jax.ShapeDtypeStruct((t, N_HEADS_B * CHUNK), F32), big],
        scratch_shapes=[pltpu.VMEM((N_HEADS_B, KEY_DIM_B, VAL_DIM_B), F32)],
        compiler_params=_params(dimension_semantics=("parallel", "arbitrary")),
    )(u, w, qd, kt, a, gl, ss, do)


def _rms_gate(o, hcat, dn_g, *, name):
    t = o.shape[0]

    def body(o_ref, z_ref, g_ref, y_ref):
        for h in range(N_HEADS_B):
            sl = slice(h * VAL_DIM_B, (h + 1) * VAL_DIM_B)
            o_ = o_ref[:, sl]
            r = lax.rsqrt(jnp.mean(o_ * o_, axis=-1, keepdims=True) + NORM_EPS)
            y_ref[:, sl] = (o_ * r * g_ref[...] * _silu(z_ref[:, sl])).astype(BF16)

    row = pl.BlockSpec((ROW_T, V_B), lambda i: (i, 0))
    return pl.pallas_call(
        body, name=name, grid=(t // ROW_T,),
        in_specs=[row, pl.BlockSpec((ROW_T, V_B), lambda i: (i, HC_Z // V_B)), pl.BlockSpec((1, VAL_DIM_B), lambda i: (0, 0))],
        out_specs=row, out_shape=jax.ShapeDtypeStruct((t, V_B), BF16),
        compiler_params=_params(dimension_semantics=("parallel",)),
    )(o, hcat, dn_g.reshape(1, VAL_DIM_B))


def _rms_gate_bwd(dy, o, hcat, dn_g, *, name):
    t = o.shape[0]

    def body(dy_ref, o_ref, z_ref, g_ref, do_ref, dz_ref, dg_ref):
        i = pl.program_id(0)
        g = g_ref[...]
        dg_p = jnp.zeros((1, VAL_DIM_B), F32)
        for h in range(N_HEADS_B):
            sl = slice(h * VAL_DIM_B, (h + 1) * VAL_DIM_B)
            o_ = o_ref[:, sl]
            z_ = z_ref[:, sl]
            dy_ = dy_ref[:, sl]
            r = lax.rsqrt(jnp.mean(o_ * o_, axis=-1, keepdims=True) + NORM_EPS)
            n = o_ * r
            sz = _silu(z_)
            dz_ref[:, sl] = (dy_ * n * g * _dsilu(z_)).astype(BF16)
            dg_p = dg_p + jnp.sum(dy_ * n * sz, axis=0, keepdims=True)
            dn = dy_ * g * sz
            do_ref[:, sl] = r * dn - o_ * (r * r * r) * jnp.mean(o_ * dn, axis=-1, keepdims=True)

        @pl.when(i == 0)
        def _():
            dg_ref[...] = dg_p

        @pl.when(i > 0)
        def _():
            dg_ref[...] += dg_p

    row = pl.BlockSpec((ROW_T, V_B), lambda i: (i, 0))
    vec = pl.BlockSpec((1, VAL_DIM_B), lambda i: (0, 0))
    return pl.pallas_call(
        body, name=name, grid=(t // ROW_T,),
        in_specs=[row, row, pl.BlockSpec((ROW_T, V_B), lambda i: (i, HC_Z // V_B)), vec],
        out_specs=[row, row, vec],
        out_shape=[jax.ShapeDtypeStruct((t, V_B), F32), jax.ShapeDtypeStruct((t, V_B), BF16),
                   jax.ShapeDtypeStruct((1, VAL_DIM_B), F32)],
        compiler_params=_params(dimension_semantics=("arbitrary",)),
    )(dy, o, hcat, dn_g.reshape(1, VAL_DIM_B))


def _merge(ya, yb, hcat, *, name):
    t = ya.shape[0]

    def body(ya_ref, yb_ref, ga_ref, gb_ref, y_ref):
        y_ref[...] = (_sigmoid(ga_ref[...]) * ya_ref[...] + _sigmoid(gb_ref[...]) * yb_ref[...]).astype(BF16)

    row = pl.BlockSpec((ROW_T, D_MODEL), lambda i: (i, 0))
    return pl.pallas_call(
        body, name=name, grid=(t // ROW_T,),
        in_specs=[row, row, pl.BlockSpec((ROW_T, D_MODEL), lambda i: (i, HC_GATE // D_MODEL)),
                  pl.BlockSpec((ROW_T, D_MODEL), lambda i: (i, HC_GATE // D_MODEL + 1))],
        out_specs=row, out_shape=jax.ShapeDtypeStruct((t, D_MODEL), BF16),
        compiler_params=_params(dimension_semantics=("parallel",)),
    )(ya, yb, hcat, hcat)


def _merge_bwd(dmix, ya, yb, hcat, *, name):
    t = ya.shape[0]

    def body(d_ref, ya_ref, yb_ref, ga_ref, gb_ref, dya_ref, dyb_ref, dgate_ref):
        d = d_ref[...]
        sa = _sigmoid(ga_ref[...])
        sb = _sigmoid(gb_ref[...])
        dya_ref[...] = (d * sa).astype(BF16)
        dyb_ref[...] = (d * sb).astype(BF16)
        dgate_ref[:, :D_MODEL] = (d * ya_ref[...] * sa * (1.0 - sa)).astype(BF16)
        dgate_ref[:, D_MODEL:] = (d * yb_ref[...] * sb * (1.0 - sb)).astype(BF16)

    row = pl.BlockSpec((ROW_T, D_MODEL), lambda i: (i, 0))
    return pl.pallas_call(
        body, name=name, grid=(t // ROW_T,),
        in_specs=[row, row, row, pl.BlockSpec((ROW_T, D_MODEL), lambda i: (i, HC_GATE // D_MODEL)),
                  pl.BlockSpec((ROW_T, D_MODEL), lambda i: (i, HC_GATE // D_MODEL + 1))],
        out_specs=[row, row, pl.BlockSpec((ROW_T, 2 * D_MODEL), lambda i: (i, 0))],
        out_shape=[jax.ShapeDtypeStruct((t, D_MODEL), BF16)] * 2 + [jax.ShapeDtypeStruct((t, 2 * D_MODEL), BF16)],
        compiler_params=_params(dimension_semantics=("parallel",)),
    )(dmix, ya, yb, hcat, hcat)


def _loss_head(y, target, *, name):
    t, n = y.shape
    tm = _tile(t, (512, 256, 128))

    def body(y_ref, t_ref, part_ref, dy_ref):
        i = pl.program_id(0)
        e = y_ref[...] - t_ref[...]
        dy_ref[...] = e * (1.0 / n)
        p = jnp.sum((e * e).reshape(tm // 8, 8, n), axis=0) * (0.5 / n)

        @pl.when(i == 0)
        def _():
            part_ref[...] = p

        @pl.when(i > 0)
        def _():
            part_ref[...] += p

    row = pl.BlockSpec((tm, n), lambda i: (i, 0))
    return pl.pallas_call(
        body, name=name, grid=(t // tm,),
        in_specs=[row, row], out_specs=[pl.BlockSpec((8, n), lambda i: (0, 0)), row],
        out_shape=[jax.ShapeDtypeStruct((8, n), F32), jax.ShapeDtypeStruct((t, n), F32)],
        compiler_params=_params(dimension_semantics=("arbitrary",)),
    )(y, target)


def _adamw_math(w, g, m, v):
    nm = ADAM_B1 * m + (1.0 - ADAM_B1) * g
    nv = ADAM_B2 * v + (1.0 - ADAM_B2) * (g * g)
    m_hat = nm / (1.0 - ADAM_B1 ** ADAM_STEP)
    v_hat = nv / (1.0 - ADAM_B2 ** ADAM_STEP)
    return -ADAM_LR * (m_hat / (jnp.sqrt(v_hat) + ADAM_EPS) + ADAM_WD * w), nm, nv


def _adamw(w, g, m, v, *, name):
    shape = w.shape
    cols = shape[-1]
    rows = int(np.prod(shape[:-1]))
    w2, g2, m2, v2 = (a.reshape(rows, cols) for a in (w, g, m, v))
    tr = rows
    if rows * cols > 512 * 1024:
        tr = _tile(rows, tuple(c for c in (512, 256, 128, 64, 32, 16, 8) if c * cols <= 256 * 1024))

    def body(w_ref, g_ref, m_ref, v_ref, d_ref, nm_ref, nv_ref):
        d_ref[...], nm_ref[...], nv_ref[...] = _adamw_math(w_ref[...], g_ref[...], m_ref[...], v_ref[...])

    blk = pl.BlockSpec((tr, cols), lambda i: (i, 0))
    outs = pl.pallas_call(
        body, name=name, grid=(rows // tr,),
        in_specs=[blk] * 4, out_specs=[blk] * 3,
        out_shape=[jax.ShapeDtypeStruct((rows, cols), F32)] * 3,
        compiler_params=_params(dimension_semantics=("parallel",)),
    )(w2, g2, m2, v2)
    return tuple(o.reshape(shape) for o in outs)


def _repack_w_in(w_in):
    d = w_in.shape[0]
    o = 0
    parts = {}
    for nm, wd in (("q", Q_A), ("k", KV_W), ("v", KV_W), ("conv", CONV_CH), ("beta", N_HEADS_B), ("dt", N_HEADS_B),
                   ("z", V_B), ("gate", 2 * D_MODEL)):
        parts[nm] = w_in[:, o:o + wd]
        o += wd
    z = lambda n: jnp.zeros((d, n), w_in.dtype)
    return jnp.concatenate([parts["q"], parts["z"], parts["k"], parts["v"], parts["beta"], parts["dt"],
                            z(128 - 2 * N_HEADS_B), z(HC_CONV - HC_BD - 128), parts["conv"], parts["gate"]], axis=1)


MATRIX_NAMES = ("ffn_w13", "ffn_w2", "w_in", "w_branch_a", "w_branch_b", "w_out")
GATHER_BESIDE_IN_PROJ = ("conv_w", "w_branch_a", "w_branch_b", "w_out")


def _dw_in_by_owner(dw):
    sections = ((Q_A, HC_Q), (2 * KV_W, HC_K), (CONV_CH, HC_CONV), (2 * N_HEADS_B, HC_BD), (V_B, HC_Z),
                (2 * D_MODEL, HC_GATE))
    per = N_IN // 4
    owners = []
    for o in range(4):
        lo, hi, start, parts = o * per, (o + 1) * per, 0, []
        for width, off in sections:
            a, b = max(lo, start), min(hi, start + width)
            if a < b:
                parts.append(dw[:, off + a - start:off + b - start])
            start += width
        rows = jnp.concatenate(parts, axis=1)
        owners.append(jnp.stack([rows[:rows.shape[0] // 2], rows[rows.shape[0] // 2:]]))
    return jnp.stack(owners)


def _lane_row(vals):
    return jnp.pad(vals.astype(F32).reshape(1, N_HEADS_B), ((0, 0), (N_HEADS_B, 128 - 2 * N_HEADS_B)))


def _local_step(x, target, rel_bias, layer_wts, side_shards=None, side_assemble=None, reducer=None):
    nbatch, seq, d = x.shape
    t = nbatch * seq
    depth = len(layer_wts)
    layer_wts = list(layer_wts)
    x0 = x.reshape(t, d)
    tgt = target.reshape(t, d)

    onehot = jnp.asarray(_bucket_onehot())
    rel_t = jnp.pad(rel_bias.T, ((0, 0), (0, 128 - NUM_BUCKETS)))
    bias = _mm(rel_t, onehot, tb=True, exact=True, name="pos_bias")
    bias = bias.reshape(N_HEADS_A, WINDOW, 2 * WINDOW)

    pending = [dict(s) if s else {} for s in (side_shards or [None] * depth)] + [{}]

    def fetch(*wanted):
        picked = [(layer, n) for layer, names in wanted for n in list(pending[layer]) if names is None or n in names]
        if not picked:
            return None, lambda outs: None
        job = _gather_job([pending[layer].pop(n) for layer, n in picked])

        def finish(outs):
            for (layer, n), out in zip(picked, outs):
                for k, val in side_assemble(layer, {n: out}).items():
                    if isinstance(val, dict):
                        layer_wts[layer].setdefault(k, {}).update(val)
                    else:
                        layer_wts[layer][k] = val
        return job, finish

    saved = []
    xin, xin_b = x0, x0.astype(BF16)
    for i in range(depth):
        L = {}
        W = layer_wts[i]
        tag = f"_l{i}"
        job, finish = fetch((i, ("w_in",)))
        a, got = _ffn_up_act(xin_b, W["ffn_w13"][0], side=job, name="ffn_up_act" + tag + "a")
        finish(got)
        r1, x1, x1_b = _mm_res_ln(a, W["ffn_w2"][0], xin, W["ln_g"][0], W["ln_b"][0],
                                  alpha=DN_ALPHA, c=0.5, name="ffn_down_ln" + tag + "a")
        L.update(x0_b=xin_b, a0=a, r1=r1, x1=x1, x1_b=x1_b)
        job, finish = fetch((i, GATHER_BESIDE_IN_PROJ), (i + 1, GATHER_BESIDE_IN_PROJ))
        hcat, got = _mm_w(x1_b, W["w_in_p"], side=job, name="in_proj" + tag)
        finish(got)
        job, finish = fetch((i + 1, ("ffn_w13_0",)))
        ao, got = _attn_fwd(hcat, bias, W["sinks"], nbatch, side=job, name="swa" + tag)
        finish(got)
        ya = _mm(ao, W["w_branch_a"], name="branch_a" + tag)
        job, finish = fetch((i, ("ffn_w13_1",)))
        (qn, kn, vs), got = _conv_prep(hcat, W["conv_w"], nbatch, side=job, name="conv_prep" + tag)
        finish(got)
        a_row = _lane_row(W["a_log"])
        dt_row = _lane_row(W["dt_bias"])
        gb, bb = _gates(hcat, a_row, dt_row, name="gates" + tag)
        job, finish = fetch((i + 1, ("ln_g", "ln_b", "w_in", "ffn_w13_1", "ffn_w2_1")))
        (u, w, qd, kt, aa, gl, ti), got = _dn_prep(qn, kn, vs, gb, bb, side=job, name="dn_prep" + tag)
        finish(got)
        job, finish = fetch((i, None), (i + 1, ("ffn_w2_0",)))
        (o, ss), got = _dn_scan(u, w, qd, kt, aa, gl, nbatch, side=job, name="dn_scan" + tag)
        finish(got)
        on = _rms_gate(o, hcat, W["dn_norm_g"], name="rms_gate" + tag)
        yb = _mm(on, W["w_branch_b"], name="branch_b" + tag)
        mix = _merge(ya, yb, hcat, name="merge" + tag)
        r2, x2, x2_b = _mm_res_ln(mix, W["w_out"], x1, W["ln_g"][1], W["ln_b"][1],
                                  alpha=DN_ALPHA, c=1.0, name="out_proj_ln" + tag)
        L.update(hcat=hcat, ao=ao, ya=ya, qn=qn, kn=kn, vs=vs, gb=gb, bb=bb, a_row=a_row, dt_row=dt_row,
                 u=u, w=w, qd=qd, kt=kt, aa=aa, gl=gl, ti=ti, o=o, ss=ss, on=on, yb=yb, mix=mix, r2=r2, x2_b=x2_b)
        a, _ = _ffn_up_act(x2_b, W["ffn_w13"][1], name="ffn_up_act" + tag + "b")
        r3, x3, x3_b = _mm_res_ln(a, W["ffn_w2"][1], x2, W["ln_g"][2], W["ln_b"][2],
                                  alpha=DN_ALPHA, c=0.5, name="ffn_down_ln" + tag + "b")
        L.update(a1=a, r3=r3)
        saved.append(L)
        xin, xin_b = x3, x3_b

    part, dy = _loss_head(xin, tgt, name="loss_head")
    loss = jnp.sum(part)

    grads = {k: [None] * depth for k in ("ln_g", "ln_b", "ffn_w13", "ffn_w2", "w_in", "conv_w", "a_log", "dt_bias",
                                          "dn_norm_g", "sinks", "w_branch_a", "w_branch_b", "w_out")}
    dbias_total = None
    for i in reversed(range(depth)):
        L = saved[i]
        W = layer_wts[i]
        tag = f"_l{i}"
        dln_g, dln_b, dw13, dw2 = [None] * 3, [None] * 3, [None] * 2, [None] * 2

        def ffn_bwd(dyo, r, xprev_b, asave, j, sfx):
            dres, df, dln_g[2 * j], dln_b[2 * j] = _ln_bwd(dyo, r, W["ln_g"][2 * j], alpha=DN_ALPHA, c=0.5,
                                                           name="ln_bwd" + tag + sfx)
            job = reducer.job_a() if (reducer is not None and j == 1) else None
            dh, swapped = _ffn_bwd_mid(xprev_b, W["ffn_w13"][j], df, W["ffn_w2"][j], side=job,
                                       name="ffn_bwd_mid" + tag + sfx)
            if job is not None:
                reducer.done_a(swapped)
            dw2[j] = _mm(asave, df, ta=True, name="ffn_w2_grad" + tag + sfx)
            dw13[j] = _mm(xprev_b, dh, ta=True, b_halves=True, name="ffn_w13_grad" + tag + sfx)
            return _mm_wt(dh, W["ffn_w13"][j], dres, a_halves=True, name="ffn_up_bwd" + tag + sfx)

        dx2 = ffn_bwd(dy, L["r3"], L["x2_b"], L["a1"], 1, "b")

        dres2, dymix, dln_g[1], dln_b[1] = _ln_bwd(dx2, L["r2"], W["ln_g"][1], alpha=DN_ALPHA, c=1.0,
                                                   name="ln_bwd" + tag + "m")
        hcat = L["hcat"]
        dmix = _mm(dymix, W["w_out"], tb=True, name="out_proj_bwd" + tag)
        grads["w_out"][i] = _mm(L["mix"], dymix, ta=True, name="w_out_grad" + tag)
        dya, dyb, dgate = _merge_bwd(dmix, L["ya"], L["yb"], hcat, name="merge_bwd" + tag)
        dao = _mm(dya, W["w_branch_a"], tb=True, name="branch_a_bwd" + tag)
        grads["w_branch_a"][i] = _mm(L["ao"], dya, ta=True, name="w_branch_a_grad" + tag)
        don = _mm(dyb, W["w_branch_b"], tb=True, name="branch_b_bwd" + tag)
        grads["w_branch_b"][i] = _mm(L["on"], dyb, ta=True, name="w_branch_b_grad" + tag)
        do, dz, ddn = _rms_gate_bwd(don, L["o"], hcat, W["dn_norm_g"], name="rms_gate_bwd" + tag)
        grads["dn_norm_g"][i] = ddn.reshape(VAL_DIM_B)
        du, dw, dqd, dkt, daa, dgl = _dn_scan_bwd(L["u"], L["w"], L["qd"], L["kt"], L["aa"], L["gl"], L["ss"], do,
                                                  nbatch, name="dn_scan_bwd" + tag)
        job = reducer.job_b() if reducer is not None else None
        (dqn, dkn, dvs, dgb, dbb), exchanged = _dn_prep_bwd(L["qn"], L["kn"], L["vs"], L["gb"], L["bb"], L["ti"], du, dw,
                                                            dqd, dkt, daa, dgl, side=job, name="dn_prep_bwd" + tag)
        if job is not None:
            reducer.done_b(exchanged)
        dconv, dconv_w = _conv_bwd(dqn, dkn, dvs, hcat, W["conv_w"], nbatch, name="conv_bwd" + tag)
        grads["conv_w"][i] = dconv_w[:CONV_K]
        dbd, da_log, ddt = _gates_bwd(dgb, dbb, hcat, L["a_row"], L["dt_row"], name="gates_bwd" + tag)
        grads["a_log"][i] = da_log[0, N_HEADS_B:2 * N_HEADS_B]
        grads["dt_bias"][i] = ddt[0, N_HEADS_B:2 * N_HEADS_B]
        dq, dk, dv, dbias, dsink = _attn_bwd(hcat, bias, W["sinks"], dao, nbatch, name="swa_bwd" + tag)
        grads["sinks"][i] = dsink.reshape(N_HEADS_A)
        dbias_total = dbias if dbias_total is None else dbias_total + dbias
        dhcat = jnp.concatenate([dq, dz, dk, dv, dbd, jnp.zeros((t, HC_CONV - HC_BD - 128), BF16), dconv, dgate], axis=1)
        dw_in_p = _mm(L["x1_b"], dhcat, ta=True, name="w_in_grad" + tag)
        grads["w_in"][i] = _dw_in_by_owner(dw_in_p)
        dx1 = _mm_wt(dhcat, W["w_in_p"], dres2, name="in_proj_bwd" + tag)

        dy = ffn_bwd(dx1, L["r1"], L["x0_b"], L["a0"], 0, "a")
        grads["ln_g"][i] = jnp.concatenate(dln_g, axis=0)
        grads["ln_b"][i] = jnp.concatenate(dln_b, axis=0)
        grads["ffn_w13"][i] = dw13
        grads["ffn_w2"][i] = dw2
        if reducer is not None:
            reducer.layer_done(i, {n: grads[n][i] for n in MATRIX_NAMES})

    out = {k: (v if k in MATRIX_NAMES else jnp.stack(v)) for k, v in grads.items()}
    drel = _mm(dbias_total.reshape(N_HEADS_A, WINDOW * 2 * WINDOW), onehot, name="rel_bias_grad")
    out["rel_bias"] = drel[:, :NUM_BUCKETS].T
    return loss, dy.reshape(nbatch, seq, d), out


N_CHIPS = 4
MESH_ID = pl.DeviceIdType.MESH
HBM_SPEC = pl.BlockSpec(memory_space=pltpu.HBM)


def _place():
    x, y, c = lax.axis_index("x"), lax.axis_index("y"), lax.axis_index("c")
    others = [(1 - x, y), (x, 1 - y), (1 - x, 1 - y)]
    return x, y, c, others


def _chip_index(cx, cy):
    return 2 * cx + cy


def _gather_sems(n):
    return [pltpu.SemaphoreType.DMA((n, 3)), pltpu.SemaphoreType.DMA((n, 3)), pltpu.SemaphoreType.DMA((n,))]


def _gather_copies(ins, outs, send_sems, recv_sems, local_sems):
    x, y, c, others = _place()
    me = _chip_index(x, y)
    copies = []
    for i in range(len(ins)):
        copies.append(pltpu.make_async_copy(ins[i], outs[i].at[me], local_sems.at[i]))
        for k, (ox, oy) in enumerate(others):
            copies.append(pltpu.make_async_remote_copy(src_ref=ins[i], dst_ref=outs[i].at[me], send_sem=send_sems.at[i, k],
                                                       recv_sem=recv_sems.at[i, k], device_id=(ox, oy, c),
                                                       device_id_type=MESH_ID))
    return copies


def _gather_job(tensors):
    return dict(ins=list(tensors), out_shape=[jax.ShapeDtypeStruct((N_CHIPS,) + t.shape, t.dtype) for t in tensors],
                scratch=_gather_sems(len(tensors)), make=_gather_copies)


def _run_job(job, *, name):
    n_in, n_out = len(job["ins"]), len(job["out_shape"])

    def body(*refs):
        copies = job["make"](refs[:n_in], refs[n_in:n_in + n_out], *refs[n_in + n_out:])
        for cp in copies:
            cp.start()
        for cp in copies:
            cp.wait()

    return pl.pallas_call(
        body, name=name, in_specs=[HBM_SPEC] * n_in, out_specs=[HBM_SPEC] * n_out,
        out_shape=list(job["out_shape"]), scratch_shapes=list(job["scratch"]),
    )(*job["ins"])


def _allgather_devices(v, *, name):
    def body(v_ref, o_ref, send_sems, recv_sems, local_sem):
        x, y, c, _ = _place()
        me = 4 * x + 2 * y + c
        loc = pltpu.make_async_copy(v_ref, o_ref.at[me], local_sem)
        loc.start()
        copies = [loc]
        for k in range(1, 8):
            px, py, pc = x ^ (k >> 2), y ^ ((k >> 1) & 1), c ^ (k & 1)
            cp = pltpu.make_async_remote_copy(src_ref=v_ref, dst_ref=o_ref.at[me], send_sem=send_sems.at[k - 1],
                                              recv_sem=recv_sems.at[k - 1], device_id=(px, py, pc), device_id_type=MESH_ID)
            cp.start()
            copies.append(cp)
        for cp in copies:
            cp.wait()

    return pl.pallas_call(
        body, name=name, in_specs=[HBM_SPEC], out_specs=HBM_SPEC,
        out_shape=jax.ShapeDtypeStruct((8,) + v.shape, v.dtype),
        scratch_shapes=[pltpu.SemaphoreType.DMA((7,)), pltpu.SemaphoreType.DMA((7,)), pltpu.SemaphoreType.DMA],
    )(v)


def _sum_slots(g, *, name):
    nb, n, r, l = g.shape
    tr = r // 2 if r % 32 == 0 else r

    def body(g_ref, o_ref):
        acc = g_ref[0].astype(F32)
        for k in range(1, n):
            acc = acc + g_ref[k].astype(F32)
        o_ref[...] = acc

    return pl.pallas_call(
        body, name=name, grid=(nb, r // tr),
        in_specs=[pl.BlockSpec((None, n, tr, l), lambda b, i: (b, 0, i, 0))],
        out_specs=pl.BlockSpec((None, tr, l), lambda b, i: (b, i, 0)),
        out_shape=jax.ShapeDtypeStruct((nb, r, l), F32),
        compiler_params=_params(dimension_semantics=("parallel", "parallel")),
    )(g)


def _half_window(ref, kind, h):
    if kind == "rows":
        return ref.at[:, h]
    r = ref.shape[0] // 2
    return ref.at[pl.ds(pl.multiple_of(h * r, r), r), :]


def _owner_window(ref, kind, o):
    if kind == "rows":
        return ref.at[o]
    cols = ref.shape[1] // N_CHIPS
    return ref.at[:, pl.ds(pl.multiple_of(o * cols, cols), cols)]


def _half_shape(g, kind):
    return (g.shape[0],) + g.shape[2:] if kind == "rows" else (g.shape[0] // 2, g.shape[1])


def _swap_job(gs, kinds):
    n = len(gs)

    def make(ins, outs, send_sems, recv_sems):
        x, y, c, _ = _place()
        return [pltpu.make_async_remote_copy(src_ref=_half_window(ins[i], kinds[i], 1 - c), dst_ref=outs[i],
                                             send_sem=send_sems.at[i], recv_sem=recv_sems.at[i],
                                             device_id=(x, y, 1 - c), device_id_type=MESH_ID) for i in range(n)]

    return dict(ins=list(gs), out_shape=[jax.ShapeDtypeStruct(_half_shape(g, k), g.dtype) for g, k in zip(gs, kinds)],
                scratch=[pltpu.SemaphoreType.DMA((n,)), pltpu.SemaphoreType.DMA((n,))], make=make)


def _pair_sum(g, got, kind, c_idx, *, name):
    hs = _half_shape(g, kind)

    def body(c_ref, g_ref, r_ref, o_ref):
        o_ref[...] = (g_ref[...] + r_ref[...]).astype(BF16)

    if kind == "rows":
        _, _, r, cols = g.shape
        grid = (N_CHIPS,)
        in_specs = [pl.BlockSpec((None, None, r, cols), lambda o, c_ref: (o, c_ref[0], 0, 0)),
                    pl.BlockSpec((None, r, cols), lambda o, c_ref: (o, 0, 0))]
        out_spec = pl.BlockSpec((None, r, cols), lambda o, c_ref: (o, 0, 0))
    else:
        r, cols = hs
        steps = 4
        tr = r // steps
        grid = (steps,)
        in_specs = [pl.BlockSpec((tr, cols), lambda i, c_ref: (c_ref[0] * steps + i, 0)),
                    pl.BlockSpec((tr, cols), lambda i, c_ref: (i, 0))]
        out_spec = pl.BlockSpec((tr, cols), lambda i, c_ref: (i, 0))
    return pl.pallas_call(
        body, name=name,
        grid_spec=pltpu.PrefetchScalarGridSpec(num_scalar_prefetch=1, grid=grid, in_specs=in_specs, out_specs=out_spec),
        out_shape=jax.ShapeDtypeStruct(hs, BF16),
        compiler_params=_params(dimension_semantics=("parallel",)),
    )(c_idx, g, got)


def _exchange_job(ss, kinds):
    n = len(ss)

    def shard_shape(s, kind):
        return s.shape[1:] if kind == "rows" else (s.shape[0], s.shape[1] // N_CHIPS)

    def make(ins, outs, send_sems, recv_sems, local_sems):
        x, y, c, others = _place()
        me = _chip_index(x, y)
        copies = []
        for i in range(n):
            dst = outs[i].at[me]
            copies.append(pltpu.make_async_copy(_owner_window(ins[i], kinds[i], me), dst, local_sems.at[i]))
            for k, (ox, oy) in enumerate(others):
                copies.append(pltpu.make_async_remote_copy(
                    src_ref=_owner_window(ins[i], kinds[i], _chip_index(ox, oy)), dst_ref=dst, send_sem=send_sems.at[i, k],
                    recv_sem=recv_sems.at[i, k], device_id=(ox, oy, c), device_id_type=MESH_ID))
        return copies

    return dict(ins=list(ss), out_shape=[jax.ShapeDtypeStruct((N_CHIPS,) + shard_shape(s, k), s.dtype)
                                         for s, k in zip(ss, kinds)],
                scratch=_gather_sems(n), make=make)


def _send_halves(fs, *, name):
    n = len(fs)

    def body(*refs):
        ins, outs, send_sems, recv_sems = refs[:n], refs[n:2 * n], refs[2 * n], refs[2 * n + 1]
        x, y, c, _ = _place()
        copies = [pltpu.make_async_remote_copy(src_ref=ins[i], dst_ref=outs[i], send_sem=send_sems.at[i],
                                               recv_sem=recv_sems.at[i], device_id=(x, y, 1 - c), device_id_type=MESH_ID)
                  for i in range(n)]
        for cp in copies:
            cp.start()
        for cp in copies:
            cp.wait()

    return pl.pallas_call(
        body, name=name, in_specs=[HBM_SPEC] * n, out_specs=[HBM_SPEC] * n,
        out_shape=[jax.ShapeDtypeStruct(f.shape, f.dtype) for f in fs],
        scratch_shapes=[pltpu.SemaphoreType.DMA((n,)), pltpu.SemaphoreType.DMA((n,))],
    )(*fs)


def _adamw_halves(w, m, v, own, other, c_idx, *, name):
    shape = w.shape
    nl, r, cols = own.shape
    w4, m4, v4 = (a.reshape(nl, 2, r, cols) for a in (w, m, v))
    tr = r if r * cols * 4 <= 3 * 512 * 1024 else _tile(r, tuple(c for c in (256, 128, 64, 32, 16, 8) if c * cols <= 256 * 1024))

    def body(c_ref, w_ref, m_ref, v_ref, own_ref, other_ref, g_ref, d_ref, nm_ref, nv_ref):
        g_ = jnp.where(pl.program_id(1) == c_ref[0], own_ref[...], other_ref[...])
        g_ref[...] = g_
        d_ref[...], nm_ref[...], nv_ref[...] = _adamw_math(w_ref[...], g_, m_ref[...], v_ref[...])

    full = pl.BlockSpec((None, None, tr, cols), lambda l, h, i, c_ref: (l, h, i, 0))
    half = pl.BlockSpec((None, tr, cols), lambda l, h, i, c_ref: (l, i, 0))
    outs = pl.pallas_call(
        body, name=name,
        grid_spec=pltpu.PrefetchScalarGridSpec(num_scalar_prefetch=1, grid=(nl, 2, r // tr),
                                               in_specs=[full, full, full, half, half], out_specs=[full] * 4),
        out_shape=[jax.ShapeDtypeStruct((nl, 2, r, cols), F32)] * 4,
        compiler_params=_params(dimension_semantics=("parallel", "parallel", "parallel")),
    )(c_idx, w4, m4, v4, own, other)
    return tuple(o.reshape(shape) for o in outs)


SHARD_AXIS = {"rel_bias": None, "ln_g": 2, "ln_b": 2, "ffn_w13": 3, "ffn_w2": 2, "w_in": 2, "conv_w": 2, "a_log": None,
              "dt_bias": None, "dn_norm_g": None, "sinks": None, "w_branch_a": 1, "w_branch_b": 1, "w_out": 1}
WEIGHT_NAMES = tuple(SHARD_AXIS)
SMALL_NAMES = tuple(n for n in WEIGHT_NAMES if n not in MATRIX_NAMES)
PACK_LANES = 1024


def _unshard(gathered, axis):
    g = jnp.moveaxis(gathered, 0, axis)
    return g.reshape(g.shape[:axis] + (g.shape[axis] * g.shape[axis + 1],) + g.shape[axis + 2:])


class _GradReducer:
    def __init__(self, c_idx):
        self.c_idx = c_idx
        self.swapping = None
        self.exchanging = None
        self.reduced = {}

    @staticmethod
    def _views(grads):
        out = []
        for n in MATRIX_NAMES:
            for g in (grads[n] if n in ("ffn_w13", "ffn_w2") else [grads[n]]):
                if n == "ffn_w13":
                    out.append((n, g, "cols"))
                elif n == "w_in":
                    out.append((n, g, "rows"))
                else:
                    out.append((n, g.reshape(N_CHIPS, 2, g.shape[0] // (2 * N_CHIPS), g.shape[1]), "rows"))
        return out

    def layer_done(self, layer, grads):
        assert self.swapping is None
        self.swapping = (layer, self._views(grads))

    def job_a(self):
        if self.swapping is None:
            return None
        _, views = self.swapping
        return _swap_job([g for _, g, _ in views], [k for _, _, k in views])

    def done_a(self, got):
        layer, views = self.swapping
        self.swapping = None
        assert self.exchanging is None
        ss = [_pair_sum(g, r, k, self.c_idx, name=f"rs_pair_sum_l{layer}_{i}")
              for i, ((_, g, k), r) in enumerate(zip(views, got))]
        self.exchanging = (layer, views, ss)

    def job_b(self):
        if self.exchanging is None:
            return None
        _, views, ss = self.exchanging
        return _exchange_job(ss, [k for _, _, k in views])

    def done_b(self, ex):
        layer, views, _ = self.exchanging
        self.exchanging = None
        red = {}
        for i, ((n, _, _), e) in enumerate(zip(views, ex)):
            red.setdefault(n, []).append(_sum_slots(e[None], name=f"rs_chip_sum_l{layer}_{i}")[0])
        self.reduced[layer] = red

    def flush(self):
        if self.swapping is not None:
            self.done_a(_run_job(self.job_a(), name="rs_swap_halves_last"))
        if self.exchanging is not None:
            self.done_b(_run_job(self.job_b(), name="rs_exchange_chips_last"))

    def result(self):
        self.flush()
        own = [jnp.stack([f for layer in sorted(self.reduced) for f in self.reduced[layer][n]]) for n in MATRIX_NAMES]
        other = _send_halves(own, name="rs_send_halves")
        return {n: (a, b) for n, a, b in zip(MATRIX_NAMES, own, other)}


def _reduce_small(grads):
    flat = [grads[n].astype(F32).reshape(-1) for n in SMALL_NAMES]
    total = sum(f.shape[0] for f in flat)
    rows = -(-total // (16 * PACK_LANES)) * 16
    vec = jnp.concatenate(flat + [jnp.zeros((rows * PACK_LANES - total,), F32)]).reshape(rows, PACK_LANES)
    s = _sum_slots(_allgather_devices(vec, name="small_allgather")[None], name="small_sum").reshape(-1)
    out, o = {}, 0
    for n, f in zip(SMALL_NAMES, flat):
        out[n] = s[o:o + f.shape[0]].reshape(grads[n].shape)
        o += f.shape[0]
    return out


def kernel(x, rel_bias, ln_g, ln_b, ffn_w13, ffn_w2, w_in, conv_w, a_log, dt_bias, dn_norm_g, sinks, w_branch_a, w_branch_b, w_out, loss_target, m_rel_bias, m_ln_g, m_ln_b, m_ffn_w13, m_ffn_w2, m_w_in, m_conv_w, m_a_log, m_dt_bias, m_dn_norm_g, m_sinks, m_w_branch_a, m_w_branch_b, m_w_out, v_rel_bias, v_ln_g, v_ln_b, v_ffn_w13, v_ffn_w2, v_w_in, v_conv_w, v_a_log, v_dt_bias, v_dn_norm_g, v_sinks, v_w_branch_a, v_w_branch_b, v_w_out):
    w = dict(rel_bias=rel_bias, ln_g=ln_g, ln_b=ln_b, ffn_w13=ffn_w13, ffn_w2=ffn_w2, w_in=w_in, conv_w=conv_w,
             a_log=a_log, dt_bias=dt_bias, dn_norm_g=dn_norm_g, sinks=sinks, w_branch_a=w_branch_a,
             w_branch_b=w_branch_b, w_out=w_out)
    m = dict(rel_bias=m_rel_bias, ln_g=m_ln_g, ln_b=m_ln_b, ffn_w13=m_ffn_w13, ffn_w2=m_ffn_w2, w_in=m_w_in,
             conv_w=m_conv_w, a_log=m_a_log, dt_bias=m_dt_bias, dn_norm_g=m_dn_norm_g, sinks=m_sinks,
             w_branch_a=m_w_branch_a, w_branch_b=m_w_branch_b, w_out=m_w_out)
    v = dict(rel_bias=v_rel_bias, ln_g=v_ln_g, ln_b=v_ln_b, ffn_w13=v_ffn_w13, ffn_w2=v_ffn_w2, w_in=v_w_in,
             conv_w=v_conv_w, a_log=v_a_log, dt_bias=v_dt_bias, dn_norm_g=v_dn_norm_g, sinks=v_sinks,
             w_branch_a=v_w_branch_a, w_branch_b=v_w_branch_b, w_out=v_w_out)

    depth = w_in.shape[0]
    sharded = [n for n in WEIGHT_NAMES if SHARD_AXIS[n] is not None]

    def shards_of(i):
        out = {}
        for n in sharded:
            s = w[n][i].astype(MXU_DTYPE) if n in MATRIX_NAMES else w[n][i]
            if n in ("ffn_w13", "ffn_w2"):
                out[n + "_0"], out[n + "_1"] = s[0], s[1]
            else:
                out[n] = s
        return out

    def assemble(i, gathered):
        lw = {}
        for n, g in gathered.items():
            if n[:-2] in ("ffn_w13", "ffn_w2"):
                lw.setdefault(n[:-2], {})[int(n[-1])] = _unshard(g, SHARD_AXIS[n[:-2]] - 2)
            elif n == "w_in":
                lw["w_in_p"] = _repack_w_in(_unshard(g, SHARD_AXIS[n] - 1))
            else:
                lw[n] = _unshard(g, SHARD_AXIS[n] - 1)
        return lw

    layer_wts = [{n: w[n][i] for n in ("a_log", "dt_bias", "dn_norm_g", "sinks")} for i in range(depth)]
    shards = [shards_of(i) for i in range(depth)]
    first = ("ffn_w13_0", "ffn_w2_0", "ln_g", "ln_b")
    got = _run_job(_gather_job([shards[0].pop(n) for n in first]), name="weights_allgather_first")
    layer_wts[0].update(assemble(0, dict(zip(first, got))))
    c_idx = lax.axis_index("c").astype(jnp.int32).reshape(1)
    reducer = _GradReducer(c_idx)
    loss_part, grad_x, grads = _local_step(x, loss_target, rel_bias, layer_wts, side_shards=shards,
                                           side_assemble=assemble, reducer=reducer)
    loss = lax.psum(loss_part, ("x", "y", "c"))

    halves = reducer.result()
    chip = _chip_index(lax.axis_index("x"), lax.axis_index("y"))
    small = _reduce_small(grads)
    outs = {}
    for n in WEIGHT_NAMES:
        if n in MATRIX_NAMES:
            outs[n] = _adamw_halves(w[n], m[n], v[n], *halves[n], c_idx, name="adamw_" + n)
        else:
            axis = SHARD_AXIS[n]
            g = small[n]
            if axis is not None:
                g = lax.dynamic_slice_in_dim(g, chip * w[n].shape[axis], w[n].shape[axis], axis)
            outs[n] = (g,) + _adamw(w[n], g, m[n], v[n], name="adamw_" + n)
    return (loss, grad_x, *[outs[n][0] for n in WEIGHT_NAMES], *[outs[n][1] for n in WEIGHT_NAMES],
            *[outs[n][2] for n in WEIGHT_NAMES], *[outs[n][3] for n in WEIGHT_NAMES])
```

```python
import functools
import math

import numpy as np
import jax
import jax.numpy as jnp
from jax import lax
from jax.experimental import pallas as pl
from jax.experimental.pallas import tpu as pltpu

F32 = jnp.float32
BF16 = jnp.bfloat16
MXU_DTYPE = BF16
HIGHEST = lax.Precision.HIGHEST

D_MODEL = 1024
N_HEADS_A = 16
N_KV_A = 4
HEAD_DIM_A = 64
GROUP_A = N_HEADS_A // N_KV_A
WINDOW = 128
N_HEADS_B = 8
KEY_DIM_B = 128
VAL_DIM_B = 128
CONV_K = 4
CHUNK = 64
D_FF = 2816
NUM_BUCKETS = 32
MAX_DISTANCE = 128
DEPTH = 4
DN_ALPHA = (2 * DEPTH) ** 0.25
LN_EPS = 1e-5
NORM_EPS = 1e-6
NEG_INF = -1e30

Q_A = N_HEADS_A * HEAD_DIM_A
KV_W = N_KV_A * HEAD_DIM_A
QK_B = N_HEADS_B * KEY_DIM_B
V_B = N_HEADS_B * VAL_DIM_B
CONV_CH = 2 * QK_B + V_B
N_IN = Q_A + 2 * KV_W + CONV_CH + 2 * N_HEADS_B + V_B + 2 * D_MODEL

ADAM_LR = 0.001
ADAM_B1 = 0.9
ADAM_B2 = 0.999
ADAM_EPS = 1e-08
ADAM_WD = 0.01
ADAM_STEP = 10

HC_W = 8192
HC_Q = 0
HC_Z = 1024
HC_K = 2048
HC_V = 2304
HC_BD = 2560
HC_CONV = 3072
HC_GATE = 6144

GROUP_T = 256
DN_BWD_HEADS = 4
ROW_T = 256
VMEM_LIMIT_BYTES = 48 * 1024 * 1024


def _params(vmem=VMEM_LIMIT_BYTES, **kw):
    return pltpu.CompilerParams(vmem_limit_bytes=vmem, **kw)


def _tile(n, cands):
    for c in cands:
        if n % c == 0:
            return c
    return n


def _dot(a, b, dims=(((1,), (0,)), ((), ())), exact=False):
    if exact:
        return lax.dot_general(a.astype(F32), b.astype(F32), dims, precision=HIGHEST, preferred_element_type=F32)
    return lax.dot_general(a.astype(MXU_DTYPE), b.astype(MXU_DTYPE), dims, preferred_element_type=F32)


_NN = (((1,), (0,)), ((), ()))
_NT = (((1,), (1,)), ((), ()))
_TN = (((0,), (0,)), ((), ()))


def _sigmoid(x):
    return 1.0 / (1.0 + jnp.exp(-x))


def _silu(x):
    return x * _sigmoid(x)


def _dsilu(x):
    s = _sigmoid(x)
    return s * (1.0 + x * (1.0 - s))


def _call(body, *, name, grid, in_specs, out_specs, out_shape, scratch_shapes=(), semantics, args, side=None,
          vmem=VMEM_LIMIT_BYTES):
    in_specs, out_specs, out_shape = list(in_specs), list(out_specs), list(out_shape)
    if side is None:
        outs = pl.pallas_call(body, name=name, grid=grid, in_specs=in_specs, out_specs=out_specs, out_shape=out_shape,
                              scratch_shapes=list(scratch_shapes),
                              compiler_params=_params(vmem=vmem, dimension_semantics=semantics))(*args)
        return outs, None
    n_in, n_out, n_scr = len(in_specs), len(out_specs), len(scratch_shapes)
    s_in, s_out = len(side["ins"]), len(side["out_shape"])

    def hosted(*refs):
        main_in, side_in = refs[:n_in], refs[n_in:n_in + s_in]
        o0 = n_in + s_in
        main_out, side_out = refs[o0:o0 + n_out], refs[o0 + n_out:o0 + n_out + s_out]
        rest = refs[o0 + n_out + s_out:]
        copies = side["make"](side_in, side_out, *rest[n_scr:])
        ids = [pl.program_id(d) for d in range(len(grid))]
        first = functools.reduce(jnp.logical_and, [i == 0 for i in ids])
        last = functools.reduce(jnp.logical_and, [i == g - 1 for i, g in zip(ids, grid)])

        @pl.when(first)
        def _():
            for cp in copies:
                cp.start()

        body(*main_in, *main_out, *rest[:n_scr])

        @pl.when(last)
        def _():
            for cp in copies:
                cp.wait()

    outs = pl.pallas_call(
        hosted, name=name, grid=grid,
        in_specs=in_specs + [HBM_SPEC] * s_in, out_specs=out_specs + [HBM_SPEC] * s_out,
        out_shape=out_shape + list(side["out_shape"]),
        scratch_shapes=list(scratch_shapes) + list(side["scratch"]),
        compiler_params=_params(vmem=vmem, dimension_semantics=("arbitrary",) * len(grid)),
    )(*args, *side["ins"])
    return list(outs[:n_out]), list(outs[n_out:])


def _mm(a, b, *, ta=False, tb=False, a_halves=False, b_halves=False, out_dtype=F32, add=None, exact=False, side=None,
        name):
    if a_halves:
        m, kdim = a.shape[1], 2 * a.shape[2]
    else:
        (kdim, m) = a.shape if ta else a.shape[::-1]
    if b_halves:
        kb, n = b.shape[1], 2 * b.shape[2]
    else:
        (n, kb) = b.shape if tb else b.shape[::-1]
    assert kdim == kb, (a.shape, b.shape, ta, tb)
    if ta and not tb and not exact and add is None and side is None and kdim <= TN_WHOLE_K:
        return _mm_tn(a, b, b_halves=b_halves, out_dtype=out_dtype, name=name)
    tn = _tile(n, (1024, 1408, 512, 256, 128))
    tk = _tile(kdim, (1024, 1408, 512, 256, 128))
    nk = kdim // tk
    tm = _tile(m, (1024, 1408, 512, 256, 128) if nk > 1 else (512, 256, 128))
    nj = n // tn
    dims = (((0 if ta else 1,), (1 if tb else 0,)), ((), ()))
    has_add = add is not None

    def body(*refs):
        if has_add:
            a_ref, b_ref, add_ref, o_ref = refs[:4]
        else:
            a_ref, b_ref, o_ref = refs[:3]
            add_ref = None
        part = _dot(a_ref[...], b_ref[...], dims, exact)

        def finish(acc):
            if has_add:
                acc = acc + add_ref[...].astype(F32)
            o_ref[...] = acc.astype(out_dtype)

        if nk == 1:
            finish(part)
        else:
            acc_ref = refs[-1]
            k = pl.program_id(2)

            @pl.when(k == 0)
            def _():
                acc_ref[...] = part

            @pl.when(k > 0)
            def _():
                acc_ref[...] += part

            @pl.when(k == nk - 1)
            def _():
                finish(acc_ref[...])

    if a_halves:
        assert not ta and nk % 2 == 0
        a_spec = pl.BlockSpec((None, tm, tk), lambda i, j, k: (k // (nk // 2), i, k % (nk // 2)))
    elif ta:
        a_spec = pl.BlockSpec((tk, tm), lambda i, j, k: (k, i))
    else:
        a_spec = pl.BlockSpec((tm, tk), lambda i, j, k: (i, k))
    if b_halves:
        assert not tb and nj % 2 == 0
        b_spec = pl.BlockSpec((None, tk, tn), lambda i, j, k: (j // (nj // 2), k, j % (nj // 2)))
    elif tb:
        b_spec = pl.BlockSpec((tn, tk), lambda i, j, k: (j, k))
    else:
        b_spec = pl.BlockSpec((tk, tn), lambda i, j, k: (k, j))
    o_spec = pl.BlockSpec((tm, tn), lambda i, j, k: (i, j))
    in_specs = [a_spec, b_spec] + ([o_spec] if has_add else [])
    args = (a, b) + ((add,) if has_add else ())
    outs, side_outs = _call(
        body, name=name, grid=(m // tm, nj, nk), in_specs=in_specs, out_specs=[o_spec],
        out_shape=[jax.ShapeDtypeStruct((m, n), out_dtype)],
        scratch_shapes=[pltpu.VMEM((tm, tn), F32)] if nk > 1 else [],
        semantics=("parallel", "parallel", "arbitrary"), args=args, side=side)
    return outs[0] if side is None else (outs[0], side_outs)


TN_WHOLE_K = 8192
VMEM_LIMIT_BIG_BYTES = 56 * 1024 * 1024


def _mm_tn(a, b, *, b_halves=False, out_dtype=F32, name):
    kdim, m = a.shape
    n = 2 * b.shape[2] if b_halves else b.shape[1]
    if m <= 1024:
        tm, tn = m, _tile(n // 2 if b_halves else n, (256, 128))
    else:
        tm, tn = _tile(m, (256, 128)), _tile(n, (1024, 512, 256, 128))
    nj = n // tn

    def body(a_ref, b_ref, o_ref):
        o_ref[...] = _dot(a_ref[...], b_ref[...], _TN).astype(out_dtype)

    if b_halves:
        b_spec = pl.BlockSpec((None, kdim, tn), lambda i, j: (j // (nj // 2), 0, j % (nj // 2)))
    else:
        b_spec = pl.BlockSpec((kdim, tn), lambda i, j: (0, j))
    return pl.pallas_call(
        body, name=name, grid=(m // tm, nj),
        in_specs=[pl.BlockSpec((kdim, tm), lambda i, j: (0, i)), b_spec],
        out_specs=pl.BlockSpec((tm, tn), lambda i, j: (i, j)),
        out_shape=jax.ShapeDtypeStruct((m, n), out_dtype),
        compiler_params=pltpu.CompilerParams(vmem_limit_bytes=VMEM_LIMIT_BIG_BYTES,
                                             dimension_semantics=("parallel", "parallel")),
    )(a, b)


def _mm_w(a, w, *, side=None, name):
    m, kdim = a.shape
    n = w.shape[1]
    tm = _tile(m, (256, 128))

    def body(a_ref, w_ref, o_ref):
        o_ref[...] = _dot(a_ref[...], w_ref[...])

    outs, side_outs = _call(
        body, name=name, grid=(m // tm,),
        in_specs=[pl.BlockSpec((tm, kdim), lambda i: (i, 0)),
                  pl.BlockSpec((kdim, n), lambda i: (0, 0), pipeline_mode=pl.Buffered(1))],
        out_specs=[pl.BlockSpec((tm, n), lambda i: (i, 0))], out_shape=[jax.ShapeDtypeStruct((m, n), F32)],
        semantics=("parallel",), args=(a, w), side=side, vmem=VMEM_LIMIT_BIG_BYTES)
    return outs[0], side_outs


def _mm_wt(a, w, add, *, a_halves=False, name):
    n, kdim = w.shape
    m = a.shape[1] if a_halves else a.shape[0]
    tm = _tile(m, (512, 256, 128))
    half = kdim // 2

    def body(a_ref, w_ref, add_ref, o_ref):
        if a_halves:
            acc = _dot(a_ref[0], w_ref[:, :half], _NT) + _dot(a_ref[1], w_ref[:, half:], _NT)
        else:
            acc = _dot(a_ref[...], w_ref[...], _NT)
        o_ref[...] = acc + add_ref[...]

    a_spec = pl.BlockSpec((2, tm, half), lambda i: (0, i, 0)) if a_halves else pl.BlockSpec((tm, kdim), lambda i: (i, 0))
    row = pl.BlockSpec((tm, n), lambda i: (i, 0))
    return pl.pallas_call(
        body, name=name, grid=(m // tm,),
        in_specs=[a_spec, pl.BlockSpec((n, kdim), lambda i: (0, 0), pipeline_mode=pl.Buffered(1)), row],
        out_specs=row, out_shape=jax.ShapeDtypeStruct((m, n), F32),
        compiler_params=pltpu.CompilerParams(vmem_limit_bytes=VMEM_LIMIT_BIG_BYTES, dimension_semantics=("parallel",)),
    )(a, w, add)


def _layernorm_rows(r, g, b):
    mu = jnp.mean(r, axis=-1, keepdims=True)
    xc = r - mu
    var = jnp.mean(xc * xc, axis=-1, keepdims=True)
    return xc * lax.rsqrt(var + LN_EPS) * g + b


def _mm_res_ln(a, w, resid, g, b, *, alpha, c, name):
    m, kdim = a.shape
    n = w.shape[1]
    tm = _tile(m, (512, 256, 128))

    def body(a_ref, w_ref, x_ref, g_ref, b_ref, r_ref, y_ref, yb_ref):
        f = _dot(a_ref[...], w_ref[...])
        r = alpha * x_ref[...] + c * f
        r_ref[...] = r
        y = _layernorm_rows(r, g_ref[...], b_ref[...])
        y_ref[...] = y
        yb_ref[...] = y.astype(BF16)

    row = pl.BlockSpec((tm, n), lambda i: (i, 0))
    vec = pl.BlockSpec((1, n), lambda i: (0, 0))
    return pl.pallas_call(
        body, name=name, grid=(m // tm,),
        in_specs=[pl.BlockSpec((tm, kdim), lambda i: (i, 0)), pl.BlockSpec((kdim, n), lambda i: (0, 0)), row, vec, vec],
        out_specs=[row, row, row],
        out_shape=[jax.ShapeDtypeStruct((m, n), F32)] * 2 + [jax.ShapeDtypeStruct((m, n), BF16)],
        compiler_params=_params(dimension_semantics=("parallel",)),
    )(a, w, resid, g.reshape(1, n), b.reshape(1, n))


def _ln_bwd(dy, r, g, *, alpha, c, name):
    m, n = dy.shape
    tm = _tile(m, (512, 256, 128))

    def body(dy_ref, r_ref, g_ref, dres_ref, dbr_ref, dg_ref, db_ref):
        i = pl.program_id(0)
        dy_ = dy_ref[...]
        r_ = r_ref[...]
        mu = jnp.mean(r_, axis=-1, keepdims=True)
        xc = r_ - mu
        var = jnp.mean(xc * xc, axis=-1, keepdims=True)
        rstd = lax.rsqrt(var + LN_EPS)
        xh = xc * rstd
        dxh = dy_ * g_ref[...]
        dr = rstd * (dxh - jnp.mean(dxh, axis=-1, keepdims=True) - xh * jnp.mean(dxh * xh, axis=-1, keepdims=True))
        dres_ref[...] = alpha * dr
        dbr_ref[...] = (c * dr).astype(BF16)
        dg_p = jnp.sum(dy_ * xh, axis=0, keepdims=True)
        db_p = jnp.sum(dy_, axis=0, keepdims=True)

        @pl.when(i == 0)
        def _():
            dg_ref[...] = dg_p
            db_ref[...] = db_p

        @pl.when(i > 0)
        def _():
            dg_ref[...] += dg_p
            db_ref[...] += db_p

    row = pl.BlockSpec((tm, n), lambda i: (i, 0))
    vec = pl.BlockSpec((1, n), lambda i: (0, 0))
    return pl.pallas_call(
        body, name=name, grid=(m // tm,),
        in_specs=[row, row, vec], out_specs=[row, row, vec, vec],
        out_shape=[jax.ShapeDtypeStruct((m, n), F32), jax.ShapeDtypeStruct((m, n), BF16),
                   jax.ShapeDtypeStruct((1, n), F32), jax.ShapeDtypeStruct((1, n), F32)],
        compiler_params=_params(dimension_semantics=("arbitrary",)),
    )(dy, r, g.reshape(1, n))


FFN_TN = D_FF // 2


def _ffn_up_act(x, w13, *, side=None, name):
    m, d = x.shape
    tm = _tile(m, (512, 256, 128))
    nj = D_FF // FFN_TN

    def body(x_ref, g_ref, u_ref, o_ref):
        x_ = x_ref[...]
        o_ref[...] = (_silu(_dot(x_, g_ref[...])) * _dot(x_, u_ref[...])).astype(BF16)

    outs, side_outs = _call(
        body, name=name, grid=(nj, m // tm),
        in_specs=[pl.BlockSpec((tm, d), lambda j, i: (i, 0)), pl.BlockSpec((d, FFN_TN), lambda j, i: (0, j)),
                  pl.BlockSpec((d, FFN_TN), lambda j, i: (0, j + nj))],
        out_specs=[pl.BlockSpec((tm, FFN_TN), lambda j, i: (i, j))],
        out_shape=[jax.ShapeDtypeStruct((m, D_FF), BF16)],
        semantics=("parallel", "parallel"), args=(x, w13, w13), side=side)
    return outs[0], side_outs


def _ffn_bwd_mid(x, w13, df, w2, *, side=None, name):
    m, d = x.shape
    tm = _tile(m, (512, 256, 128))
    nj = D_FF // FFN_TN

    def body(x_ref, g_ref, u_ref, df_ref, w2_ref, o_ref):
        x_ = x_ref[...]
        g = _dot(x_, g_ref[...])
        u = _dot(x_, u_ref[...])
        da = _dot(df_ref[...], w2_ref[...], _NT)
        o_ref[0] = (da * u * _dsilu(g)).astype(BF16)
        o_ref[1] = (da * _silu(g)).astype(BF16)

    outs, side_outs = _call(
        body, name=name, grid=(nj, m // tm),
        in_specs=[pl.BlockSpec((tm, d), lambda j, i: (i, 0)), pl.BlockSpec((d, FFN_TN), lambda j, i: (0, j)),
                  pl.BlockSpec((d, FFN_TN), lambda j, i: (0, j + nj)), pl.BlockSpec((tm, d), lambda j, i: (i, 0)),
                  pl.BlockSpec((FFN_TN, d), lambda j, i: (j, 0))],
        out_specs=[pl.BlockSpec((2, tm, FFN_TN), lambda j, i: (0, i, j))],
        out_shape=[jax.ShapeDtypeStruct((2, m, D_FF), BF16)],
        semantics=("parallel", "parallel"), args=(x, w13, w13, df, w2), side=side)
    return outs[0], side_outs


def _t5_bucket_table():
    r = np.arange(WINDOW)[:, None]
    j = np.arange(2 * WINDOW)[None, :]
    rel = r + WINDOW - j
    n = np.maximum(rel, 0)
    max_exact = NUM_BUCKETS // 2
    nf = np.maximum(n, 1).astype(np.float32)
    large = max_exact + (np.log(nf / np.float32(max_exact)) / np.float32(math.log(MAX_DISTANCE / max_exact))
                         * np.float32(NUM_BUCKETS - max_exact)).astype(np.int32)
    large = np.minimum(large, NUM_BUCKETS - 1)
    bucket = np.where(n < max_exact, n, large)
    in_band = (rel >= 0) & (rel < WINDOW)
    return bucket.astype(np.int32), in_band


def _bucket_onehot():
    bucket, _ = _t5_bucket_table()
    oh = np.zeros((WINDOW * 2 * WINDOW, 128), np.float32)
    oh[np.arange(oh.shape[0]), bucket.reshape(-1)] = 1.0
    return oh


def _stack_heads(x, g):
    hd = HEAD_DIM_A
    return jnp.concatenate([x[:, (GROUP_A * g + h) * hd:(GROUP_A * g + h + 1) * hd] for h in range(GROUP_A)], axis=0)


def _unstack_heads(x):
    return jnp.concatenate([x[h * WINDOW:(h + 1) * WINDOW] for h in range(GROUP_A)], axis=1)


def _attn_probs(q_ref, kp_ref, ko_ref, vp_ref, vo_ref, bias_ref, sink_ref, first_block):
    hd = HEAD_DIM_A
    groups = range(N_KV_A)
    q = q_ref[...]
    qs = [_stack_heads(q, g) * (hd ** -0.5) for g in groups]
    k2 = [jnp.concatenate([kp_ref[:, g * hd:(g + 1) * hd], ko_ref[:, g * hd:(g + 1) * hd]], axis=0) for g in groups]
    v2 = [jnp.concatenate([vp_ref[:, g * hd:(g + 1) * hd], vo_ref[:, g * hd:(g + 1) * hd]], axis=0) for g in groups]
    rr = lax.broadcasted_iota(jnp.int32, (GROUP_A * WINDOW, 2 * WINDOW), 0) % WINDOW
    jj = lax.broadcasted_iota(jnp.int32, (GROUP_A * WINDOW, 2 * WINDOW), 1)
    rel = rr + WINDOW - jj
    valid = (rel >= 0) & (rel < WINDOW) & (jnp.logical_not(first_block) | (jj >= WINDOW))
    s = [_dot(qs[g], k2[g], _NT) for g in groups]
    s = [jnp.where(valid, s[g] + bias_ref[GROUP_A * g:GROUP_A * (g + 1)].reshape(GROUP_A * WINDOW, 2 * WINDOW), NEG_INF)
         for g in groups]
    sk = [jnp.concatenate([jnp.broadcast_to(sink_ref[0:1, GROUP_A * g + h:GROUP_A * g + h + 1], (WINDOW, 1))
                           for h in range(GROUP_A)], axis=0) for g in groups]
    mx = [jnp.maximum(jnp.max(s[g], axis=-1, keepdims=True), sk[g]) for g in groups]
    p = [jnp.exp(s[g] - mx[g]) for g in groups]
    ps = [jnp.exp(sk[g] - mx[g]) for g in groups]
    den = [jnp.sum(p[g], axis=-1, keepdims=True) + ps[g] for g in groups]
    return qs, k2, v2, [p[g] / den[g] for g in groups], [ps[g] / den[g] for g in groups]


def _attn_specs(nb):
    def prev(b, i):
        return (b * nb + jnp.maximum(i - 1, 0))

    q_spec = pl.BlockSpec((WINDOW, Q_A), lambda b, i: (b * nb + i, HC_Q // Q_A))
    kp_spec = pl.BlockSpec((WINDOW, KV_W), lambda b, i: (prev(b, i), HC_K // KV_W))
    ko_spec = pl.BlockSpec((WINDOW, KV_W), lambda b, i: (b * nb + i, HC_K // KV_W))
    vp_spec = pl.BlockSpec((WINDOW, KV_W), lambda b, i: (prev(b, i), HC_V // KV_W))
    vo_spec = pl.BlockSpec((WINDOW, KV_W), lambda b, i: (b * nb + i, HC_V // KV_W))
    bias_spec = pl.BlockSpec((N_HEADS_A, WINDOW, 2 * WINDOW), lambda b, i: (0, 0, 0))
    sink_spec = pl.BlockSpec((1, N_HEADS_A), lambda b, i: (0, 0))
    return [q_spec, kp_spec, ko_spec, vp_spec, vo_spec, bias_spec, sink_spec]


def _attn_fwd(hcat, bias, sink, nbatch, *, side=None, name):
    t = hcat.shape[0]
    nb = t // nbatch // WINDOW

    def body(q_ref, kp_ref, ko_ref, vp_ref, vo_ref, bias_ref, sink_ref, o_ref):
        first = pl.program_id(1) == 0
        _, _, v2, p, _ = _attn_probs(q_ref, kp_ref, ko_ref, vp_ref, vo_ref, bias_ref, sink_ref, first)
        o = [_dot(p[g], v2[g]) for g in range(N_KV_A)]
        o_ref[...] = jnp.concatenate([_unstack_heads(og) for og in o], axis=1).astype(BF16)

    outs, side_outs = _call(
        body, name=name, grid=(nbatch, nb), in_specs=_attn_specs(nb),
        out_specs=[pl.BlockSpec((WINDOW, Q_A), lambda b, i: (b * nb + i, 0))],
        out_shape=[jax.ShapeDtypeStruct((t, Q_A), BF16)], semantics=("parallel", "arbitrary"),
        args=(hcat, hcat, hcat, hcat, hcat, bias, sink.reshape(1, N_HEADS_A)), side=side)
    return outs[0], side_outs


def _attn_bwd(hcat, bias, sink, do, nbatch, *, name):
    t = hcat.shape[0]
    nb = t // nbatch // WINDOW
    hd = HEAD_DIM_A

    def body(q_ref, kp_ref, ko_ref, vp_ref, vo_ref, bias_ref, sink_ref, do_ref,
             dq_ref, dk_ref, dv_ref, dbias_ref, dsink_ref, ck_ref, cv_ref):
        b = pl.program_id(0)
        j = pl.program_id(1)
        first = j == nb - 1

        @pl.when((b == 0) & (j == 0))
        def _():
            dbias_ref[...] = jnp.zeros_like(dbias_ref)
            dsink_ref[...] = jnp.zeros_like(dsink_ref)

        @pl.when(j == 0)
        def _():
            ck_ref[...] = jnp.zeros_like(ck_ref)
            cv_ref[...] = jnp.zeros_like(cv_ref)

        do_ = do_ref[...]
        groups = range(N_KV_A)
        lane = lax.broadcasted_iota(jnp.int32, (1, N_HEADS_A), 1)
        qs, k2, v2, p, ps = _attn_probs(q_ref, kp_ref, ko_ref, vp_ref, vo_ref, bias_ref, sink_ref, first)
        dos = [_stack_heads(do_, g) for g in groups]
        dv2 = [_dot(p[g], dos[g], _TN) for g in groups]
        dp = [_dot(dos[g], v2[g], _NT) for g in groups]
        delta = [jnp.sum(p[g] * dp[g], axis=-1, keepdims=True) for g in groups]
        ds = [p[g] * (dp[g] - delta[g]) for g in groups]
        dqs = [_dot(ds[g], k2[g]) * (hd ** -0.5) for g in groups]
        dk2 = [_dot(ds[g], qs[g], _TN) for g in groups]
        dsink = jnp.zeros((1, N_HEADS_A), F32)
        for g in groups:
            dsk = -(ps[g] * delta[g])
            for h in range(GROUP_A):
                tot = jnp.sum(dsk[h * WINDOW:(h + 1) * WINDOW], axis=0, keepdims=True)
                dsink = dsink + jnp.where(lane == GROUP_A * g + h, tot, 0.0)
            dbias_ref[GROUP_A * g:GROUP_A * (g + 1)] += ds[g].reshape(GROUP_A, WINDOW, 2 * WINDOW)
        dq_ref[...] = jnp.concatenate([_unstack_heads(d) for d in dqs], axis=1).astype(BF16)
        dk_ref[...] = (jnp.concatenate([d[WINDOW:] for d in dk2], axis=1) + ck_ref[...]).astype(BF16)
        dv_ref[...] = (jnp.concatenate([d[WINDOW:] for d in dv2], axis=1) + cv_ref[...]).astype(BF16)
        ck_ref[...] = jnp.concatenate([d[:WINDOW] for d in dk2], axis=1)
        cv_ref[...] = jnp.concatenate([d[:WINDOW] for d in dv2], axis=1)
        dsink_ref[...] += dsink

    def rev(spec):
        return pl.BlockSpec(spec.block_shape, lambda b, j, f=spec.index_map: f(b, nb - 1 - j))

    in_specs = [rev(s) for s in _attn_specs(nb)[:5]] + _attn_specs(nb)[5:]
    in_specs.append(pl.BlockSpec((WINDOW, Q_A), lambda b, j: (b * nb + nb - 1 - j, 0)))
    return pl.pallas_call(
        body, name=name, grid=(nbatch, nb),
        in_specs=in_specs,
        out_specs=[pl.BlockSpec((WINDOW, Q_A), lambda b, j: (b * nb + nb - 1 - j, 0)),
                   pl.BlockSpec((WINDOW, KV_W), lambda b, j: (b * nb + nb - 1 - j, 0)),
                   pl.BlockSpec((WINDOW, KV_W), lambda b, j: (b * nb + nb - 1 - j, 0)),
                   pl.BlockSpec((N_HEADS_A, WINDOW, 2 * WINDOW), lambda b, j: (0, 0, 0)),
                   pl.BlockSpec((1, N_HEADS_A), lambda b, j: (0, 0))],
        out_shape=[jax.ShapeDtypeStruct((t, Q_A), BF16), jax.ShapeDtypeStruct((t, KV_W), BF16),
                   jax.ShapeDtypeStruct((t, KV_W), BF16),
                   jax.ShapeDtypeStruct((N_HEADS_A, WINDOW, 2 * WINDOW), F32),
                   jax.ShapeDtypeStruct((1, N_HEADS_A), F32)],
        scratch_shapes=[pltpu.VMEM((WINDOW, KV_W), F32), pltpu.VMEM((WINDOW, KV_W), F32)],
        compiler_params=_params(dimension_semantics=("arbitrary", "arbitrary")),
    )(hcat, hcat, hcat, hcat, hcat, bias, sink.reshape(1, N_HEADS_A), do)


def _shift_down(x, halo8, s):
    if s == 0:
        return x
    rolled = pltpu.roll(x, s, axis=0)
    row8 = lax.broadcasted_iota(jnp.int32, halo8.shape, 0)
    top = jnp.where(row8 < s, pltpu.roll(halo8, s, axis=0), rolled[0:8])
    return top if x.shape[0] == 8 else jnp.concatenate([top, rolled[8:]], axis=0)


def _shift_up(x, halo8, s):
    if s == 0:
        return x
    n = x.shape[0]
    rolled = pltpu.roll(x, n - s, axis=0)
    row8 = lax.broadcasted_iota(jnp.int32, halo8.shape, 0)
    bottom = jnp.where(row8 >= 8 - s, pltpu.roll(halo8, 8 - s, axis=0), rolled[n - 8:n])
    return jnp.concatenate([rolled[:n - 8], bottom], axis=0)


def _l2n(x, scale):
    r = lax.rsqrt(jnp.sum(x * x, axis=-1, keepdims=True) + NORM_EPS)
    return x * (r * scale)


def _conv_prep(hcat, conv_w, nbatch, *, side=None, name):
    t = hcat.shape[0]
    nt = t // nbatch // ROW_T
    cb = HC_CONV // CONV_CH

    def body(u_ref, halo_ref, w_ref, q_ref, k_ref, v_ref):
        i = pl.program_id(1)
        s = _silu(_conv_rows(u_ref[...], jnp.where(i == 0, 0.0, halo_ref[...]), w_ref))
        for h in range(N_HEADS_B):
            lo, hi = h * KEY_DIM_B, (h + 1) * KEY_DIM_B
            q_ref[:, lo:hi] = _l2n(s[:, lo:hi], KEY_DIM_B ** -0.5)
            k_ref[:, lo:hi] = _l2n(s[:, QK_B + lo:QK_B + hi], 1.0)
        v_ref[...] = s[:, 2 * QK_B:]

    row = lambda w: pl.BlockSpec((ROW_T, w), lambda b, i: (b * nt + i, 0))
    outs, side_outs = _call(
        body, name=name, grid=(nbatch, nt),
        in_specs=[pl.BlockSpec((ROW_T, CONV_CH), lambda b, i: (b * nt + i, cb)),
                  pl.BlockSpec((8, CONV_CH), lambda b, i: (jnp.maximum((b * nt + i) * (ROW_T // 8) - 1, 0), cb)),
                  pl.BlockSpec((CONV_K, CONV_CH), lambda b, i: (0, 0))],
        out_specs=[row(QK_B), row(QK_B), row(V_B)],
        out_shape=[jax.ShapeDtypeStruct((t, QK_B), F32)] * 3,
        semantics=("parallel", "parallel"), args=(hcat, hcat, conv_w), side=side)
    return tuple(outs), side_outs


def _conv_rows(u, halo8, w_ref):
    c = jnp.zeros_like(u)
    for j in range(CONV_K):
        c = c + w_ref[j:j + 1, :] * _shift_down(u, halo8, CONV_K - 1 - j)
    return c


def _conv_prep_pointwise_bwd(c, d, l2_scale):
    def l2n_bwd(x, dy, scale):
        r = lax.rsqrt(jnp.sum(x * x, axis=-1, keepdims=True) + NORM_EPS)
        return scale * (r * dy - x * (r * r * r) * jnp.sum(x * dy, axis=-1, keepdims=True))

    sig = _sigmoid(c)
    ds = sig * (1.0 + c * (1.0 - sig))
    if l2_scale is None:
        return d * ds
    s = c * sig
    parts = []
    for h in range(N_HEADS_B):
        lo, hi = h * KEY_DIM_B, (h + 1) * KEY_DIM_B
        parts.append(l2n_bwd(s[:, lo:hi], d[:, lo:hi], l2_scale) * ds[:, lo:hi])
    return jnp.concatenate(parts, axis=1)


def _conv_bwd(d, hcat, conv_w, nbatch, sec, *, name):
    t = d.shape[0]
    nt = t // nbatch // ROW_T
    cb = HC_CONV // QK_B + sec
    last_blk = t // 8 - 1
    l2_scale = (KEY_DIM_B ** -0.5, 1.0, None)[sec]

    def body(d_ref, dn_ref, u_ref, uprev_ref, unext_ref, w_ref, du_ref, dw_ref):
        b = pl.program_id(0)
        i = pl.program_id(1)
        u = u_ref[...]
        uprev = jnp.where(i == 0, 0.0, uprev_ref[...])
        shifted = [_shift_down(u, uprev, CONV_K - 1 - j) for j in range(CONV_K)]
        c = jnp.zeros_like(u)
        for j in range(CONV_K):
            c = c + w_ref[j:j + 1, :] * shifted[j]
        dc_ = _conv_prep_pointwise_bwd(c, d_ref[...], l2_scale)
        c_next = _conv_rows(unext_ref[...], u[ROW_T - 8:], w_ref)
        dnext = _conv_prep_pointwise_bwd(c_next, dn_ref[...], l2_scale)
        dnext = jnp.where(i == nt - 1, 0.0, dnext)
        du = jnp.zeros_like(dc_)
        rows = []
        for j in range(CONV_K):
            du = du + w_ref[j:j + 1, :] * _shift_up(dc_, dnext, CONV_K - 1 - j)
            rows.append(jnp.sum(dc_ * shifted[j], axis=0, keepdims=True))
        du_ref[...] = du.astype(BF16)
        dw_p = jnp.concatenate(rows + [jnp.zeros((8 - CONV_K, QK_B), F32)], axis=0)

        @pl.when((b == 0) & (i == 0))
        def _():
            dw_ref[...] = dw_p

        @pl.when((b > 0) | (i > 0))
        def _():
            dw_ref[...] += dw_p

    def nxt(b, i):
        return jnp.minimum((b * nt + i + 1) * (ROW_T // 8), last_blk)

    return pl.pallas_call(
        body, name=name, grid=(nbatch, nt),
        in_specs=[pl.BlockSpec((ROW_T, QK_B), lambda b, i: (b * nt + i, 0)),
                  pl.BlockSpec((8, QK_B), lambda b, i: (nxt(b, i), 0)),
                  pl.BlockSpec((ROW_T, QK_B), lambda b, i: (b * nt + i, cb)),
                  pl.BlockSpec((8, QK_B), lambda b, i: (jnp.maximum((b * nt + i) * (ROW_T // 8) - 1, 0), cb)),
                  pl.BlockSpec((8, QK_B), lambda b, i: (nxt(b, i), cb)),
                  pl.BlockSpec((CONV_K, QK_B), lambda b, i: (0, sec))],
        out_specs=[pl.BlockSpec((ROW_T, QK_B), lambda b, i: (b * nt + i, 0)),
                   pl.BlockSpec((8, QK_B), lambda b, i: (0, 0))],
        out_shape=[jax.ShapeDtypeStruct((t, QK_B), BF16), jax.ShapeDtypeStruct((8, QK_B), F32)],
        compiler_params=_params(dimension_semantics=("arbitrary", "arbitrary")),
    )(d, d, hcat, hcat, hcat, conv_w)


def _softplus(x):
    return jnp.maximum(x, 0.0) + jnp.log(1.0 + jnp.exp(-jnp.abs(x)))


def _gates(hcat, a_row, dt_row, *, name):
    t = hcat.shape[0]

    def body(bd_ref, a_ref, dt_ref, gb_ref, bb_ref):
        bd = bd_ref[...]
        beta = _sigmoid(bd)
        g = -jnp.exp(a_ref[...]) * _softplus(bd + dt_ref[...])
        for h in range(N_HEADS_B):
            lo, hi = h * VAL_DIM_B, (h + 1) * VAL_DIM_B
            bb_ref[:, lo:hi] = jnp.broadcast_to(beta[:, h:h + 1], (ROW_T, VAL_DIM_B))
            gb_ref[:, lo:hi] = jnp.broadcast_to(g[:, N_HEADS_B + h:N_HEADS_B + h + 1], (ROW_T, VAL_DIM_B))

    vec = pl.BlockSpec((1, 128), lambda i: (0, 0))
    row = pl.BlockSpec((ROW_T, V_B), lambda i: (i, 0))
    return pl.pallas_call(
        body, name=name, grid=(t // ROW_T,),
        in_specs=[pl.BlockSpec((ROW_T, 128), lambda i: (i, HC_BD // 128)), vec, vec],
        out_specs=[row, row], out_shape=[jax.ShapeDtypeStruct((t, V_B), F32)] * 2,
        compiler_params=_params(dimension_semantics=("parallel",)),
    )(hcat, a_row, dt_row)


def _gates_bwd(dgb, dbb, hcat, a_row, dt_row, *, name):
    t = hcat.shape[0]

    def body(dgb_ref, dbb_ref, bd_ref, a_ref, dt_ref, dbd_ref, da_ref, ddt_ref):
        i = pl.program_id(0)
        bd = bd_ref[...]
        beta = _sigmoid(bd)
        ea = jnp.exp(a_ref[...])
        x = bd + dt_ref[...]
        g = -ea * _softplus(x)
        lane = lax.broadcasted_iota(jnp.int32, (ROW_T, 128), 1)
        dbeta = jnp.zeros((ROW_T, 128), F32)
        dg = jnp.zeros((ROW_T, 128), F32)
        for h in range(N_HEADS_B):
            lo, hi = h * VAL_DIM_B, (h + 1) * VAL_DIM_B
            dbeta = dbeta + jnp.where(lane == h, jnp.sum(dbb_ref[:, lo:hi], axis=-1, keepdims=True), 0.0)
            dg = dg + jnp.where(lane == N_HEADS_B + h, jnp.sum(dgb_ref[:, lo:hi], axis=-1, keepdims=True), 0.0)
        ddt_raw = dg * (-ea) * _sigmoid(x)
        dbd_ref[...] = (dbeta * beta * (1.0 - beta) + ddt_raw).astype(BF16)
        da_p = jnp.sum(dg * g, axis=0, keepdims=True)
        ddt_p = jnp.sum(ddt_raw, axis=0, keepdims=True)

        @pl.when(i == 0)
        def _():
            da_ref[...] = da_p
            ddt_ref[...] = ddt_p

        @pl.when(i > 0)
        def _():
            da_ref[...] += da_p
            ddt_ref[...] += ddt_p

    vec = pl.BlockSpec((1, 128), lambda i: (0, 0))
    row = pl.BlockSpec((ROW_T, V_B), lambda i: (i, 0))
    return pl.pallas_call(
        body, name=name, grid=(t // ROW_T,),
        in_specs=[row, row, pl.BlockSpec((ROW_T, 128), lambda i: (i, HC_BD // 128)), vec, vec],
        out_specs=[pl.BlockSpec((ROW_T, 128), lambda i: (i, 0)), vec, vec],
        out_shape=[jax.ShapeDtypeStruct((t, 128), BF16), jax.ShapeDtypeStruct((1, 128), F32),
                   jax.ShapeDtypeStruct((1, 128), F32)],
        compiler_params=_params(dimension_semantics=("arbitrary",)),
    )(dgb, dbb, hcat, a_row, dt_row)


def _group_masks():
    r = lax.broadcasted_iota(jnp.int32, (GROUP_T, GROUP_T), 0)
    c = lax.broadcasted_iota(jnp.int32, (GROUP_T, GROUP_T), 1)
    same = (r // CHUNK) == (c // CHUNK)
    return same, same & (r >= c), same & (r > c)


def _split2(a):
    hi = a.astype(MXU_DTYPE)
    return hi, (a - hi.astype(F32)).astype(MXU_DTYPE)


def _dot3(a2, b2, dims=_NN):
    (ah, al), (bh, bl) = a2, b2
    d = functools.partial(lax.dot_general, dimension_numbers=dims, preferred_element_type=F32)
    return d(ah, bh) + (d(ah, bl) + d(al, bh))


def _inv_unit_lower(lows):
    shape = lows[0].shape
    eye = (lax.broadcasted_iota(jnp.int32, shape, 0) == lax.broadcasted_iota(jnp.int32, shape, 1)).astype(F32)
    p2 = [_split2(-low) for low in lows]
    ts = [eye - low for low in lows]
    for _ in range(int(math.log2(CHUNK)) - 1):
        p2 = [_split2(_dot3(p, p)) for p in p2]
        ts = [t + _dot3(_split2(t), p) for t, p in zip(ts, p2)]
    return ts


@jax.custom_vjp
def _inv_saved(low, t):
    return t


def _inv_saved_fwd(low, t):
    return t, t


def _inv_saved_bwd(t, dt):
    t2 = _split2(t)
    return -_dot3(t2, _split2(_dot3(_split2(dt), t2, _NT)), _TN), jnp.zeros_like(t)


_inv_saved.defvjp(_inv_saved_fwd, _inv_saved_bwd)


def _mask_dot(mask, x, dims):
    m = mask.astype(MXU_DTYPE)
    hi = x.astype(MXU_DTYPE)
    r1 = x - hi.astype(F32)
    mid = r1.astype(MXU_DTYPE)
    lo = (r1 - mid.astype(F32)).astype(MXU_DTYPE)
    d = functools.partial(lax.dot_general, dimension_numbers=dims, preferred_element_type=F32)
    return d(m, hi) + (d(m, mid) + d(m, lo))


@jax.custom_vjp
def _chunk_sums(gb):
    same, causal, _ = _group_masks()
    return _mask_dot(causal, gb, _NN), _mask_dot(same, gb, _NN)


def _chunk_sums_fwd(gb):
    return _chunk_sums(gb), None


def _chunk_sums_bwd(_, cot):
    same, causal, _ = _group_masks()
    return (_mask_dot(causal, cot[0], _TN) + _mask_dot(same, cot[1], _TN),)


_chunk_sums.defvjp(_chunk_sums_fwd, _chunk_sums_bwd)


def _fold_blocks(m):
    return m[:, 0:CHUNK] + m[:, CHUNK:2 * CHUNK] + m[:, 2 * CHUNK:3 * CHUNK] + m[:, 3 * CHUNK:4 * CHUNK]


def _dn_prep_heads(q, k, v, gb, bb, tsaved=None):
    same, causal, strict = _group_masks()
    heads = range(len(q))
    sums = [_chunk_sums(gb[h]) for h in heads]
    gc = [s[0] for s in sums]
    glast = [s[1] for s in sums]
    decay = [jnp.exp(jnp.where(causal, gc[h][:, 0:1] - gc[h].T[0:1, :], NEG_INF)) for h in heads]
    kb = [k[h] * bb[h] for h in heads]
    vb = [v[h] * bb[h] for h in heads]
    lower = [jnp.where(strict, _dot(kb[h], k[h], _NT) * decay[h], 0.0) for h in heads]
    if tsaved is None:
        tinv = _inv_unit_lower(lower)
    else:
        tinv = [_inv_saved(lower[h], jnp.where(same, jnp.concatenate([tsaved[h]] * (GROUP_T // CHUNK), axis=1), 0.0))
                for h in heads]
    egc = [jnp.exp(gc[h]) for h in heads]
    u = [_dot(tinv[h], vb[h]) for h in heads]
    w = [_dot(tinv[h], kb[h] * egc[h]) for h in heads]
    a = [_fold_blocks(jnp.where(causal, _dot(q[h], k[h], _NT) * decay[h], 0.0)) for h in heads]
    k_tail = [k[h] * jnp.exp(glast[h] - gc[h]) for h in heads]
    q_dec = [q[h] * egc[h] for h in heads]
    return u, w, q_dec, k_tail, a, glast, [_fold_blocks(t) for t in tinv]


def _head_slices(ref, width):
    return [ref[:, h * width:(h + 1) * width] for h in range(N_HEADS_B)]


def _store_heads(ref, vals, width):
    for h, val in enumerate(vals):
        ref[:, h * width:(h + 1) * width] = val


def _dn_prep(q, k, v, gb, bb, *, side=None, name):
    t = q.shape[0]

    def body(q_ref, k_ref, v_ref, gb_ref, bb_ref, u_ref, w_ref, qd_ref, kt_ref, a_ref, gl_ref, ti_ref):
        outs = _dn_prep_heads(*[_head_slices(r, KEY_DIM_B) for r in (q_ref, k_ref, v_ref, gb_ref, bb_ref)])
        for ref, vals in zip((u_ref, w_ref, qd_ref, kt_ref, a_ref, gl_ref, ti_ref), outs):
            _store_heads(ref, vals, vals[0].shape[1])

    row = pl.BlockSpec((GROUP_T, V_B), lambda i: (i, 0))
    arow = pl.BlockSpec((GROUP_T, N_HEADS_B * CHUNK), lambda i: (i, 0))
    big = jax.ShapeDtypeStruct((t, V_B), F32)
    small = jax.ShapeDtypeStruct((t, N_HEADS_B * CHUNK), F32)
    outs, side_outs = _call(
        body, name=name, grid=(t // GROUP_T,), in_specs=[row] * 5, out_specs=[row, row, row, row, arow, row, arow],
        out_shape=[big, big, big, big, small, big, small], semantics=("parallel",), args=(q, k, v, gb, bb), side=side)
    return tuple(outs), side_outs


def _dn_prep_bwd(q, k, v, gb, bb, ti, du, dw, dqd, dkt, da, dgl, *, side=None, name):
    t = q.shape[0]

    def body(q_ref, k_ref, v_ref, gb_ref, bb_ref, ti_ref, du_ref, dw_ref, dqd_ref, dkt_ref, da_ref, dgl_ref,
             dq_ref, dk_ref, dv_ref, dgb_ref, dbb_ref):
        for lo in range(0, N_HEADS_B, DN_BWD_HEADS):
            grp = slice(lo, lo + DN_BWD_HEADS)
            tsaved = _head_slices(ti_ref, CHUNK)[grp]
            _, vjp = jax.vjp(lambda *a, ts=tsaved: _dn_prep_heads(*a, tsaved=ts)[:6],
                             *[_head_slices(r, KEY_DIM_B)[grp] for r in (q_ref, k_ref, v_ref, gb_ref, bb_ref)])
            cot = tuple(_head_slices(r, CHUNK if r is da_ref else KEY_DIM_B)[grp]
                        for r in (du_ref, dw_ref, dqd_ref, dkt_ref, da_ref, dgl_ref))
            for ref, vals in zip((dq_ref, dk_ref, dv_ref, dgb_ref, dbb_ref), vjp(cot)):
                for h, val in enumerate(vals):
                    ref[:, (lo + h) * KEY_DIM_B:(lo + h + 1) * KEY_DIM_B] = val

    row = pl.BlockSpec((GROUP_T, V_B), lambda i: (i, 0))
    arow = pl.BlockSpec((GROUP_T, N_HEADS_B * CHUNK), lambda i: (i, 0))
    big = jax.ShapeDtypeStruct((t, V_B), F32)
    outs, side_outs = _call(
        body, name=name, grid=(t // GROUP_T,),
        in_specs=[row] * 5 + [arow] + [row] * 4 + [arow, row], out_specs=[row] * 5, out_shape=[big] * 5,
        semantics=("parallel",), args=(q, k, v, gb, bb, ti, du, dw, dqd, dkt, da, dgl), side=side)
    return tuple(outs), side_outs


def _dn_steps(s, qd, kt, u, w, a, gl):
    heads = range(len(s))
    v_new = [u[h] - _dot(w[h], s[h]) for h in heads]
    qs = [_dot(qd[h], s[h]) for h in heads]
    o = [qs[h] + _dot(a[h], v_new[h]) for h in heads]
    s_new = [s[h] * jnp.exp(gl[h][0:1, :]) + _dot(kt[h], v_new[h], _TN) for h in heads]
    return s_new, o


def _dn_scan(u, w, qd, kt, a, gl, nbatch, *, side=None, name):
    t = u.shape[0]
    ng = t // nbatch // GROUP_T
    cpg = GROUP_T // CHUNK

    def body(u_ref, w_ref, qd_ref, kt_ref, a_ref, gl_ref, o_ref, ss_ref, s_ref):
        @pl.when(pl.program_id(1) == 0)
        def _():
            s_ref[...] = jnp.zeros_like(s_ref)

        def chunk(c, carry):
            rows = pl.ds(pl.multiple_of(c * CHUNK, CHUNK), CHUNK)
            heads = range(N_HEADS_B)
            s = [s_ref[h] for h in heads]
            for h in heads:
                ss_ref[c, h] = s[h]
            s_new, o = _dn_steps(s, *[[r[rows, h * wd:(h + 1) * wd] for h in heads] for r, wd in
                                      ((qd_ref, KEY_DIM_B), (kt_ref, KEY_DIM_B), (u_ref, VAL_DIM_B), (w_ref, KEY_DIM_B),
                                       (a_ref, CHUNK), (gl_ref, VAL_DIM_B))])
            for h in heads:
                s_ref[h] = s_new[h]
                o_ref[rows, h * VAL_DIM_B:(h + 1) * VAL_DIM_B] = o[h]
            return carry

        lax.fori_loop(0, cpg, chunk, 0)

    row = pl.BlockSpec((GROUP_T, V_B), lambda b, i: (b * ng + i, 0))
    arow = pl.BlockSpec((GROUP_T, N_HEADS_B * CHUNK), lambda b, i: (b * ng + i, 0))
    outs, side_outs = _call(
        body, name=name, grid=(nbatch, ng),
        in_specs=[row, row, row, row, arow, row],
        out_specs=[row, pl.BlockSpec((cpg, N_HEADS_B, KEY_DIM_B, VAL_DIM_B), lambda b, i: (b * ng + i, 0, 0, 0))],
        out_shape=[jax.ShapeDtypeStruct((t, V_B), F32),
                   jax.ShapeDtypeStruct((t // CHUNK, N_HEADS_B, KEY_DIM_B, VAL_DIM_B), F32)],
        scratch_shapes=[pltpu.VMEM((N_HEADS_B, KEY_DIM_B, VAL_DIM_B), F32)],
        semantics=("parallel", "arbitrary"), args=(u, w, qd, kt, a, gl), side=side)
    return tuple(outs), side_outs


def _dn_scan_bwd(u, w, qd, kt, a, gl, ss, do, nbatch, *, name):
    t = u.shape[0]
    ng = t // nbatch // GROUP_T
    cpg = GROUP_T // CHUNK

    def body(u_ref, w_ref, qd_ref, kt_ref, a_ref, gl_ref, ss_ref, do_ref,
             du_ref, dw_ref, dqd_ref, dkt_ref, da_ref, dgl_ref, ds_ref):
        @pl.when(pl.program_id(1) == 0)
        def _():
            ds_ref[...] = jnp.zeros_like(ds_ref)

        def chunk(cc, carry):
            c = cpg - 1 - cc
            rows = pl.ds(pl.multiple_of(c * CHUNK, CHUNK), CHUNK)
            heads = range(N_HEADS_B)
            ins = ((qd_ref, KEY_DIM_B), (kt_ref, KEY_DIM_B), (u_ref, VAL_DIM_B), (w_ref, KEY_DIM_B), (a_ref, CHUNK),
                   (gl_ref, VAL_DIM_B))
            _, vjp = jax.vjp(_dn_steps, [ss_ref[c, h] for h in heads],
                             *[[r[rows, h * wd:(h + 1) * wd] for h in heads] for r, wd in ins])
            grads = vjp(([ds_ref[h] for h in heads], [do_ref[rows, h * VAL_DIM_B:(h + 1) * VAL_DIM_B] for h in heads]))
            for h in heads:
                ds_ref[h] = grads[0][h]
            outs = ((dqd_ref, KEY_DIM_B), (dkt_ref, KEY_DIM_B), (du_ref, VAL_DIM_B), (dw_ref, KEY_DIM_B), (da_ref, CHUNK),
                    (dgl_ref, VAL_DIM_B))
            for (r, wd), vals in zip(outs, grads[1:]):
                for h in heads:
                    r[rows, h * wd:(h + 1) * wd] = vals[h]
            return carry

        lax.fori_loop(0, cpg, chunk, 0)

    row = pl.BlockSpec((GROUP_T, V_B), lambda b, j: (b * ng + ng - 1 - j, 0))
    arow = pl.BlockSpec((GROUP_T, N_HEADS_B * CHUNK), lambda b, j: (b * ng + ng - 1 - j, 0))
    big = jax.ShapeDtypeStruct((t, V_B), F32)
    return pl.pallas_call(
        body, name=name, grid=(nbatch, ng),
        in_specs=[row, row, row, row, arow, row,
                  pl.BlockSpec((cpg, N_HEADS_B, KEY_DIM_B, VAL_DIM_B), lambda b, j: (b * ng + ng - 1 - j, 0, 0, 0)), row],
        out_specs=[row, row, row, row, arow, row],
        out_shape=[big, big, big, big, jax.ShapeDtypeStruct((t, N_HEADS_B * CHUNK), F32), big],
        scratch_shapes=[pltpu.VMEM((N_HEADS_B, KEY_DIM_B, VAL_DIM_B), F32)],
        compiler_params=_params(dimension_semantics=("parallel", "arbitrary")),
    )(u, w, qd, kt, a, gl, ss, do)


def _rms_gate(o, hcat, dn_g, *, name):
    t = o.shape[0]

    def body(o_ref, z_ref, g_ref, y_ref):
        for h in range(N_HEADS_B):
            sl = slice(h * VAL_DIM_B, (h + 1) * VAL_DIM_B)
            o_ = o_ref[:, sl]
            r = lax.rsqrt(jnp.mean(o_ * o_, axis=-1, keepdims=True) + NORM_EPS)
            y_ref[:, sl] = (o_ * r * g_ref[...] * _silu(z_ref[:, sl])).astype(BF16)

    row = pl.BlockSpec((ROW_T, V_B), lambda i: (i, 0))
    return pl.pallas_call(
        body, name=name, grid=(t // ROW_T,),
        in_specs=[row, pl.BlockSpec((ROW_T, V_B), lambda i: (i, HC_Z // V_B)), pl.BlockSpec((1, VAL_DIM_B), lambda i: (0, 0))],
        out_specs=row, out_shape=jax.ShapeDtypeStruct((t, V_B), BF16),
        compiler_params=_params(dimension_semantics=("parallel",)),
    )(o, hcat, dn_g.reshape(1, VAL_DIM_B))


def _rms_gate_bwd(dy, o, hcat, dn_g, *, name):
    t = o.shape[0]

    def body(dy_ref, o_ref, z_ref, g_ref, do_ref, dz_ref, dg_ref):
        i = pl.program_id(0)
        g = g_ref[...]
        dg_p = jnp.zeros((1, VAL_DIM_B), F32)
        for h in range(N_HEADS_B):
            sl = slice(h * VAL_DIM_B, (h + 1) * VAL_DIM_B)
            o_ = o_ref[:, sl]
            z_ = z_ref[:, sl]
            dy_ = dy_ref[:, sl]
            r = lax.rsqrt(jnp.mean(o_ * o_, axis=-1, keepdims=True) + NORM_EPS)
            n = o_ * r
            sz = _silu(z_)
            dz_ref[:, sl] = (dy_ * n * g * _dsilu(z_)).astype(BF16)
            dg_p = dg_p + jnp.sum(dy_ * n * sz, axis=0, keepdims=True)
            dn = dy_ * g * sz
            do_ref[:, sl] = r * dn - o_ * (r * r * r) * jnp.mean(o_ * dn, axis=-1, keepdims=True)

        @pl.when(i == 0)
        def _():
            dg_ref[...] = dg_p

        @pl.when(i > 0)
        def _():
            dg_ref[...] += dg_p

    row = pl.BlockSpec((ROW_T, V_B), lambda i: (i, 0))
    vec = pl.BlockSpec((1, VAL_DIM_B), lambda i: (0, 0))
    return pl.pallas_call(
        body, name=name, grid=(t // ROW_T,),
        in_specs=[row, row, pl.BlockSpec((ROW_T, V_B), lambda i: (i, HC_Z // V_B)), vec],
        out_specs=[row, row, vec],
        out_shape=[jax.ShapeDtypeStruct((t, V_B), F32), jax.ShapeDtypeStruct((t, V_B), BF16),
                   jax.ShapeDtypeStruct((1, VAL_DIM_B), F32)],
        compiler_params=_params(dimension_semantics=("arbitrary",)),
    )(dy, o, hcat, dn_g.reshape(1, VAL_DIM_B))


def _merge(ya, yb, hcat, *, name):
    t = ya.shape[0]

    def body(ya_ref, yb_ref, ga_ref, gb_ref, y_ref):
        y_ref[...] = (_sigmoid(ga_ref[...]) * ya_ref[...] + _sigmoid(gb_ref[...]) * yb_ref[...]).astype(BF16)

    row = pl.BlockSpec((ROW_T, D_MODEL), lambda i: (i, 0))
    return pl.pallas_call(
        body, name=name, grid=(t // ROW_T,),
        in_specs=[row, row, pl.BlockSpec((ROW_T, D_MODEL), lambda i: (i, HC_GATE // D_MODEL)),
                  pl.BlockSpec((ROW_T, D_MODEL), lambda i: (i, HC_GATE // D_MODEL + 1))],
        out_specs=row, out_shape=jax.ShapeDtypeStruct((t, D_MODEL), BF16),
        compiler_params=_params(dimension_semantics=("parallel",)),
    )(ya, yb, hcat, hcat)


def _merge_bwd(dmix, ya, yb, hcat, *, name):
    t = ya.shape[0]

    def body(d_ref, ya_ref, yb_ref, ga_ref, gb_ref, dya_ref, dyb_ref, dgate_ref):
        d = d_ref[...]
        sa = _sigmoid(ga_ref[...])
        sb = _sigmoid(gb_ref[...])
        dya_ref[...] = (d * sa).astype(BF16)
        dyb_ref[...] = (d * sb).astype(BF16)
        dgate_ref[:, :D_MODEL] = (d * ya_ref[...] * sa * (1.0 - sa)).astype(BF16)
        dgate_ref[:, D_MODEL:] = (d * yb_ref[...] * sb * (1.0 - sb)).astype(BF16)

    row = pl.BlockSpec((ROW_T, D_MODEL), lambda i: (i, 0))
    return pl.pallas_call(
        body, name=name, grid=(t // ROW_T,),
        in_specs=[row, row, row, pl.BlockSpec((ROW_T, D_MODEL), lambda i: (i, HC_GATE // D_MODEL)),
                  pl.BlockSpec((ROW_T, D_MODEL), lambda i: (i, HC_GATE // D_MODEL + 1))],
        out_specs=[row, row, pl.BlockSpec((ROW_T, 2 * D_MODEL), lambda i: (i, 0))],
        out_shape=[jax.ShapeDtypeStruct((t, D_MODEL), BF16)] * 2 + [jax.ShapeDtypeStruct((t, 2 * D_MODEL), BF16)],
        compiler_params=_params(dimension_semantics=("parallel",)),
    )(dmix, ya, yb, hcat, hcat)


def _loss_head(y, target, *, name):
    t, n = y.shape
    tm = _tile(t, (512, 256, 128))

    def body(y_ref, t_ref, part_ref, dy_ref):
        i = pl.program_id(0)
        e = y_ref[...] - t_ref[...]
        dy_ref[...] = e * (1.0 / n)
        p = jnp.sum((e * e).reshape(tm // 8, 8, n), axis=0) * (0.5 / n)

        @pl.when(i == 0)
        def _():
            part_ref[...] = p

        @pl.when(i > 0)
        def _():
            part_ref[...] += p

    row = pl.BlockSpec((tm, n), lambda i: (i, 0))
    return pl.pallas_call(
        body, name=name, grid=(t // tm,),
        in_specs=[row, row], out_specs=[pl.BlockSpec((8, n), lambda i: (0, 0)), row],
        out_shape=[jax.ShapeDtypeStruct((8, n), F32), jax.ShapeDtypeStruct((t, n), F32)],
        compiler_params=_params(dimension_semantics=("arbitrary",)),
    )(y, target)


def _adamw_math(w, g, m, v):
    nm = ADAM_B1 * m + (1.0 - ADAM_B1) * g
    nv = ADAM_B2 * v + (1.0 - ADAM_B2) * (g * g)
    m_hat = nm / (1.0 - ADAM_B1 ** ADAM_STEP)
    v_hat = nv / (1.0 - ADAM_B2 ** ADAM_STEP)
    return -ADAM_LR * (m_hat / (jnp.sqrt(v_hat) + ADAM_EPS) + ADAM_WD * w), nm, nv


def _adamw(w, g, m, v, *, name):
    shape = w.shape
    cols = shape[-1]
    rows = int(np.prod(shape[:-1]))
    w2, g2, m2, v2 = (a.reshape(rows, cols) for a in (w, g, m, v))
    tr = rows
    if rows * cols > 512 * 1024:
        tr = _tile(rows, tuple(c for c in (512, 256, 128, 64, 32, 16, 8) if c * cols <= 256 * 1024))

    def body(w_ref, g_ref, m_ref, v_ref, d_ref, nm_ref, nv_ref):
        d_ref[...], nm_ref[...], nv_ref[...] = _adamw_math(w_ref[...], g_ref[...], m_ref[...], v_ref[...])

    blk = pl.BlockSpec((tr, cols), lambda i: (i, 0))
    outs = pl.pallas_call(
        body, name=name, grid=(rows // tr,),
        in_specs=[blk] * 4, out_specs=[blk] * 3,
        out_shape=[jax.ShapeDtypeStruct((rows, cols), F32)] * 3,
        compiler_params=_params(dimension_semantics=("parallel",)),
    )(w2, g2, m2, v2)
    return tuple(o.reshape(shape) for o in outs)


def _repack_w_in(w_in):
    d = w_in.shape[0]
    o = 0
    parts = {}
    for nm, wd in (("q", Q_A), ("k", KV_W), ("v", KV_W), ("conv", CONV_CH), ("beta", N_HEADS_B), ("dt", N_HEADS_B),
                   ("z", V_B), ("gate", 2 * D_MODEL)):
        parts[nm] = w_in[:, o:o + wd]
        o += wd
    z = lambda n: jnp.zeros((d, n), w_in.dtype)
    return jnp.concatenate([parts["q"], parts["z"], parts["k"], parts["v"], parts["beta"], parts["dt"],
                            z(128 - 2 * N_HEADS_B), z(HC_CONV - HC_BD - 128), parts["conv"], parts["gate"]], axis=1)


MATRIX_NAMES = ("ffn_w13", "ffn_w2", "w_in", "w_branch_a", "w_branch_b", "w_out")
GATHER_BESIDE_IN_PROJ = ("conv_w", "w_branch_a", "w_branch_b", "w_out")


def _dw_in_by_owner(dw):
    sections = ((Q_A, HC_Q), (2 * KV_W, HC_K), (CONV_CH, HC_CONV), (2 * N_HEADS_B, HC_BD), (V_B, HC_Z),
                (2 * D_MODEL, HC_GATE))
    per = N_IN // 4
    owners = []
    for o in range(4):
        lo, hi, start, parts = o * per, (o + 1) * per, 0, []
        for width, off in sections:
            a, b = max(lo, start), min(hi, start + width)
            if a < b:
                parts.append(dw[:, off + a - start:off + b - start])
            start += width
        rows = jnp.concatenate(parts, axis=1)
        owners.append(jnp.stack([rows[:rows.shape[0] // 2], rows[rows.shape[0] // 2:]]))
    return jnp.stack(owners)


def _lane_row(vals):
    return jnp.pad(vals.astype(F32).reshape(1, N_HEADS_B), ((0, 0), (N_HEADS_B, 128 - 2 * N_HEADS_B)))


def _local_step(x, target, rel_bias, layer_wts, side_shards=None, side_assemble=None, reducer=None):
    nbatch, seq, d = x.shape
    t = nbatch * seq
    depth = len(layer_wts)
    layer_wts = list(layer_wts)
    x0 = x.reshape(t, d)
    tgt = target.reshape(t, d)

    onehot = jnp.asarray(_bucket_onehot())
    rel_t = jnp.pad(rel_bias.T, ((0, 0), (0, 128 - NUM_BUCKETS)))
    bias = _mm(rel_t, onehot, tb=True, exact=True, name="pos_bias")
    bias = bias.reshape(N_HEADS_A, WINDOW, 2 * WINDOW)

    pending = [dict(s) if s else {} for s in (side_shards or [None] * depth)] + [{}]

    def fetch(*wanted):
        picked = [(layer, n) for layer, names in wanted for n in list(pending[layer]) if names is None or n in names]
        if not picked:
            return None, lambda outs: None
        job = _gather_job([pending[layer].pop(n) for layer, n in picked])

        def finish(outs):
            for (layer, n), out in zip(picked, outs):
                for k, val in side_assemble(layer, {n: out}).items():
                    if isinstance(val, dict):
                        layer_wts[layer].setdefault(k, {}).update(val)
                    else:
                        layer_wts[layer][k] = val
        return job, finish

    saved = []
    xin, xin_b = x0, x0.astype(BF16)
    for i in range(depth):
        L = {}
        W = layer_wts[i]
        tag = f"_l{i}"
        job, finish = fetch((i, ("w_in",)))
        a, got = _ffn_up_act(xin_b, W["ffn_w13"][0], side=job, name="ffn_up_act" + tag + "a")
        finish(got)
        r1, x1, x1_b = _mm_res_ln(a, W["ffn_w2"][0], xin, W["ln_g"][0], W["ln_b"][0],
                                  alpha=DN_ALPHA, c=0.5, name="ffn_down_ln" + tag + "a")
        L.update(x0_b=xin_b, a0=a, r1=r1, x1=x1, x1_b=x1_b)
        job, finish = fetch((i, GATHER_BESIDE_IN_PROJ), (i + 1, GATHER_BESIDE_IN_PROJ))
        hcat, got = _mm_w(x1_b, W["w_in_p"], side=job, name="in_proj" + tag)
        finish(got)
        job, finish = fetch((i + 1, ("w_in",)))
        ao, got = _attn_fwd(hcat, bias, W["sinks"], nbatch, side=job, name="swa" + tag)
        finish(got)
        ya = _mm(ao, W["w_branch_a"], name="branch_a" + tag)
        job, finish = fetch((i, ("ffn_w13_1",)))
        (qn, kn, vs), got = _conv_prep(hcat, W["conv_w"], nbatch, side=job, name="conv_prep" + tag)
        finish(got)
        a_row = _lane_row(W["a_log"])
        dt_row = _lane_row(W["dt_bias"])
        gb, bb = _gates(hcat, a_row, dt_row, name="gates" + tag)
        job, finish = fetch((i + 1, ("ln_g", "ln_b", "ffn_w13_0", "ffn_w13_1", "ffn_w2_0", "ffn_w2_1")))
        (u, w, qd, kt, aa, gl, ti), got = _dn_prep(qn, kn, vs, gb, bb, side=job, name="dn_prep" + tag)
        finish(got)
        job, finish = fetch((i, None))
        (o, ss), got = _dn_scan(u, w, qd, kt, aa, gl, nbatch, side=job, name="dn_scan" + tag)
        finish(got)
        on = _rms_gate(o, hcat, W["dn_norm_g"], name="rms_gate" + tag)
        yb = _mm(on, W["w_branch_b"], name="branch_b" + tag)
        mix = _merge(ya, yb, hcat, name="merge" + tag)
        r2, x2, x2_b = _mm_res_ln(mix, W["w_out"], x1, W["ln_g"][1], W["ln_b"][1],
                                  alpha=DN_ALPHA, c=1.0, name="out_proj_ln" + tag)
        L.update(hcat=hcat, ao=ao, ya=ya, qn=qn, kn=kn, vs=vs, gb=gb, bb=bb, a_row=a_row, dt_row=dt_row,
                 u=u, w=w, qd=qd, kt=kt, aa=aa, gl=gl, ti=ti, o=o, ss=ss, on=on, yb=yb, mix=mix, r2=r2, x2_b=x2_b)
        a, _ = _ffn_up_act(x2_b, W["ffn_w13"][1], name="ffn_up_act" + tag + "b")
        r3, x3, x3_b = _mm_res_ln(a, W["ffn_w2"][1], x2, W["ln_g"][2], W["ln_b"][2],
                                  alpha=DN_ALPHA, c=0.5, name="ffn_down_ln" + tag + "b")
        L.update(a1=a, r3=r3)
        saved.append(L)
        xin, xin_b = x3, x3_b

    part, dy = _loss_head(xin, tgt, name="loss_head")
    loss = jnp.sum(part)

    grads = {k: [None] * depth for k in ("ln_g", "ln_b", "ffn_w13", "ffn_w2", "w_in", "conv_w", "a_log", "dt_bias",
                                          "dn_norm_g", "sinks", "w_branch_a", "w_branch_b", "w_out")}
    dbias_total = None
    for i in reversed(range(depth)):
        L = saved[i]
        W = layer_wts[i]
        tag = f"_l{i}"
        dln_g, dln_b, dw13, dw2 = [None] * 3, [None] * 3, [None] * 2, [None] * 2

        def ffn_bwd(dyo, r, xprev_b, asave, j, sfx):
            dres, df, dln_g[2 * j], dln_b[2 * j] = _ln_bwd(dyo, r, W["ln_g"][2 * j], alpha=DN_ALPHA, c=0.5,
                                                           name="ln_bwd" + tag + sfx)
            job = reducer.job_a() if (reducer is not None and j == 1) else None
            dh, swapped = _ffn_bwd_mid(xprev_b, W["ffn_w13"][j], df, W["ffn_w2"][j], side=job,
                                       name="ffn_bwd_mid" + tag + sfx)
            if job is not None:
                reducer.done_a(swapped)
            dw2[j] = _mm(asave, df, ta=True, name="ffn_w2_grad" + tag + sfx)
            dw13[j] = _mm(xprev_b, dh, ta=True, b_halves=True, name="ffn_w13_grad" + tag + sfx)
            return _mm_wt(dh, W["ffn_w13"][j], dres, a_halves=True, name="ffn_up_bwd" + tag + sfx)

        dx2 = ffn_bwd(dy, L["r3"], L["x2_b"], L["a1"], 1, "b")

        dres2, dymix, dln_g[1], dln_b[1] = _ln_bwd(dx2, L["r2"], W["ln_g"][1], alpha=DN_ALPHA, c=1.0,
                                                   name="ln_bwd" + tag + "m")
        hcat = L["hcat"]
        dmix = _mm(dymix, W["w_out"], tb=True, name="out_proj_bwd" + tag)
        grads["w_out"][i] = _mm(L["mix"], dymix, ta=True, name="w_out_grad" + tag)
        dya, dyb, dgate = _merge_bwd(dmix, L["ya"], L["yb"], hcat, name="merge_bwd" + tag)
        dao = _mm(dya, W["w_branch_a"], tb=True, name="branch_a_bwd" + tag)
        grads["w_branch_a"][i] = _mm(L["ao"], dya, ta=True, name="w_branch_a_grad" + tag)
        don = _mm(dyb, W["w_branch_b"], tb=True, name="branch_b_bwd" + tag)
        grads["w_branch_b"][i] = _mm(L["on"], dyb, ta=True, name="w_branch_b_grad" + tag)
        do, dz, ddn = _rms_gate_bwd(don, L["o"], hcat, W["dn_norm_g"], name="rms_gate_bwd" + tag)
        grads["dn_norm_g"][i] = ddn.reshape(VAL_DIM_B)
        du, dw, dqd, dkt, daa, dgl = _dn_scan_bwd(L["u"], L["w"], L["qd"], L["kt"], L["aa"], L["gl"], L["ss"], do,
                                                  nbatch, name="dn_scan_bwd" + tag)
        job = reducer.job_b() if reducer is not None else None
        (dqn, dkn, dvs, dgb, dbb), exchanged = _dn_prep_bwd(L["qn"], L["kn"], L["vs"], L["gb"], L["bb"], L["ti"], du, dw,
                                                            dqd, dkt, daa, dgl, side=job, name="dn_prep_bwd" + tag)
        if job is not None:
            reducer.done_b(exchanged)
        dconv, dconv_w = zip(*[_conv_bwd(dsec, hcat, W["conv_w"], nbatch, sec, name=f"conv_bwd{tag}_{sec}")
                               for sec, dsec in enumerate((dqn, dkn, dvs))])
        grads["conv_w"][i] = jnp.concatenate([dw_[:CONV_K] for dw_ in dconv_w], axis=1)
        dbd, da_log, ddt = _gates_bwd(dgb, dbb, hcat, L["a_row"], L["dt_row"], name="gates_bwd" + tag)
        grads["a_log"][i] = da_log[0, N_HEADS_B:2 * N_HEADS_B]
        grads["dt_bias"][i] = ddt[0, N_HEADS_B:2 * N_HEADS_B]
        dq, dk, dv, dbias, dsink = _attn_bwd(hcat, bias, W["sinks"], dao, nbatch, name="swa_bwd" + tag)
        grads["sinks"][i] = dsink.reshape(N_HEADS_A)
        dbias_total = dbias if dbias_total is None else dbias_total + dbias
        dhcat = jnp.concatenate([dq, dz, dk, dv, dbd, jnp.zeros((t, HC_CONV - HC_BD - 128), BF16), *dconv, dgate], axis=1)
        dw_in_p = _mm(L["x1_b"], dhcat, ta=True, name="w_in_grad" + tag)
        grads["w_in"][i] = _dw_in_by_owner(dw_in_p)
        dx1 = _mm_wt(dhcat, W["w_in_p"], dres2, name="in_proj_bwd" + tag)

        dy = ffn_bwd(dx1, L["r1"], L["x0_b"], L["a0"], 0, "a")
        grads["ln_g"][i] = jnp.concatenate(dln_g, axis=0)
        grads["ln_b"][i] = jnp.concatenate(dln_b, axis=0)
        grads["ffn_w13"][i] = dw13
        grads["ffn_w2"][i] = dw2
        if reducer is not None:
            reducer.layer_done(i, {n: grads[n][i] for n in MATRIX_NAMES})

    out = {k: (v if k in MATRIX_NAMES else jnp.stack(v)) for k, v in grads.items()}
    drel = _mm(dbias_total.reshape(N_HEADS_A, WINDOW * 2 * WINDOW), onehot, name="rel_bias_grad")
    out["rel_bias"] = drel[:, :NUM_BUCKETS].T
    return loss, dy.reshape(nbatch, seq, d), out


N_CHIPS = 4
MESH_ID = pl.DeviceIdType.MESH
HBM_SPEC = pl.BlockSpec(memory_space=pltpu.HBM)


def _place():
    x, y, c = lax.axis_index("x"), lax.axis_index("y"), lax.axis_index("c")
    others = [(1 - x, y), (x, 1 - y), (1 - x, 1 - y)]
    return x, y, c, others


def _chip_index(cx, cy):
    return 2 * cx + cy


def _gather_sems(n):
    return [pltpu.SemaphoreType.DMA((n, 3)), pltpu.SemaphoreType.DMA((n, 3)), pltpu.SemaphoreType.DMA((n,))]


def _gather_copies(ins, outs, send_sems, recv_sems, local_sems):
    x, y, c, others = _place()
    me = _chip_index(x, y)
    copies = []
    for i in range(len(ins)):
        copies.append(pltpu.make_async_copy(ins[i], outs[i].at[me], local_sems.at[i]))
        for k, (ox, oy) in enumerate(others):
            copies.append(pltpu.make_async_remote_copy(src_ref=ins[i], dst_ref=outs[i].at[me], send_sem=send_sems.at[i, k],
                                                       recv_sem=recv_sems.at[i, k], device_id=(ox, oy, c),
                                                       device_id_type=MESH_ID))
    return copies


def _gather_job(tensors):
    return dict(ins=list(tensors), out_shape=[jax.ShapeDtypeStruct((N_CHIPS,) + t.shape, t.dtype) for t in tensors],
                scratch=_gather_sems(len(tensors)), make=_gather_copies)


def _run_job(job, *, name):
    n_in, n_out = len(job["ins"]), len(job["out_shape"])

    def body(*refs):
        copies = job["make"](refs[:n_in], refs[n_in:n_in + n_out], *refs[n_in + n_out:])
        for cp in copies:
            cp.start()
        for cp in copies:
            cp.wait()

    return pl.pallas_call(
        body, name=name, in_specs=[HBM_SPEC] * n_in, out_specs=[HBM_SPEC] * n_out,
        out_shape=list(job["out_shape"]), scratch_shapes=list(job["scratch"]),
    )(*job["ins"])


def _allgather_devices(v, *, name):
    def body(v_ref, o_ref, send_sems, recv_sems, local_sem):
        x, y, c, _ = _place()
        me = 4 * x + 2 * y + c
        loc = pltpu.make_async_copy(v_ref, o_ref.at[me], local_sem)
        loc.start()
        copies = [loc]
        for k in range(1, 8):
            px, py, pc = x ^ (k >> 2), y ^ ((k >> 1) & 1), c ^ (k & 1)
            cp = pltpu.make_async_remote_copy(src_ref=v_ref, dst_ref=o_ref.at[me], send_sem=send_sems.at[k - 1],
                                              recv_sem=recv_sems.at[k - 1], device_id=(px, py, pc), device_id_type=MESH_ID)
            cp.start()
            copies.append(cp)
        for cp in copies:
            cp.wait()

    return pl.pallas_call(
        body, name=name, in_specs=[HBM_SPEC], out_specs=HBM_SPEC,
        out_shape=jax.ShapeDtypeStruct((8,) + v.shape, v.dtype),
        scratch_shapes=[pltpu.SemaphoreType.DMA((7,)), pltpu.SemaphoreType.DMA((7,)), pltpu.SemaphoreType.DMA],
    )(v)


def _sum_slots(g, *, name):
    nb, n, r, l = g.shape
    tr = r // 2 if r % 32 == 0 else r

    def body(g_ref, o_ref):
        acc = g_ref[0].astype(F32)
        for k in range(1, n):
            acc = acc + g_ref[k].astype(F32)
        o_ref[...] = acc

    return pl.pallas_call(
        body, name=name, grid=(nb, r // tr),
        in_specs=[pl.BlockSpec((None, n, tr, l), lambda b, i: (b, 0, i, 0))],
        out_specs=pl.BlockSpec((None, tr, l), lambda b, i: (b, i, 0)),
        out_shape=jax.ShapeDtypeStruct((nb, r, l), F32),
        compiler_params=_params(dimension_semantics=("parallel", "parallel")),
    )(g)


def _half_window(ref, kind, h):
    if kind == "rows":
        return ref.at[:, h]
    r = ref.shape[0] // 2
    return ref.at[pl.ds(pl.multiple_of(h * r, r), r), :]


def _owner_window(ref, kind, o):
    if kind == "rows":
        return ref.at[o]
    cols = ref.shape[1] // N_CHIPS
    return ref.at[:, pl.ds(pl.multiple_of(o * cols, cols), cols)]


def _half_shape(g, kind):
    return (g.shape[0],) + g.shape[2:] if kind == "rows" else (g.shape[0] // 2, g.shape[1])


def _swap_job(gs, kinds):
    n = len(gs)

    def make(ins, outs, send_sems, recv_sems):
        x, y, c, _ = _place()
        return [pltpu.make_async_remote_copy(src_ref=_half_window(ins[i], kinds[i], 1 - c), dst_ref=outs[i],
                                             send_sem=send_sems.at[i], recv_sem=recv_sems.at[i],
                                             device_id=(x, y, 1 - c), device_id_type=MESH_ID) for i in range(n)]

    return dict(ins=list(gs), out_shape=[jax.ShapeDtypeStruct(_half_shape(g, k), g.dtype) for g, k in zip(gs, kinds)],
                scratch=[pltpu.SemaphoreType.DMA((n,)), pltpu.SemaphoreType.DMA((n,))], make=make)


def _pair_sum(g, got, kind, c_idx, *, name):
    hs = _half_shape(g, kind)

    def body(c_ref, g_ref, r_ref, o_ref):
        o_ref[...] = (g_ref[...] + r_ref[...]).astype(BF16)

    if kind == "rows":
        _, _, r, cols = g.shape
        grid = (N_CHIPS,)
        in_specs = [pl.BlockSpec((None, None, r, cols), lambda o, c_ref: (o, c_ref[0], 0, 0)),
                    pl.BlockSpec((None, r, cols), lambda o, c_ref: (o, 0, 0))]
        out_spec = pl.BlockSpec((None, r, cols), lambda o, c_ref: (o, 0, 0))
    else:
        r, cols = hs
        steps = 4
        tr = r // steps
        grid = (steps,)
        in_specs = [pl.BlockSpec((tr, cols), lambda i, c_ref: (c_ref[0] * steps + i, 0)),
                    pl.BlockSpec((tr, cols), lambda i, c_ref: (i, 0))]
        out_spec = pl.BlockSpec((tr, cols), lambda i, c_ref: (i, 0))
    return pl.pallas_call(
        body, name=name,
        grid_spec=pltpu.PrefetchScalarGridSpec(num_scalar_prefetch=1, grid=grid, in_specs=in_specs, out_specs=out_spec),
        out_shape=jax.ShapeDtypeStruct(hs, BF16),
        compiler_params=_params(dimension_semantics=("parallel",)),
    )(c_idx, g, got)


def _exchange_job(ss, kinds):
    n = len(ss)

    def shard_shape(s, kind):
        return s.shape[1:] if kind == "rows" else (s.shape[0], s.shape[1] // N_CHIPS)

    def make(ins, outs, send_sems, recv_sems, local_sems):
        x, y, c, others = _place()
        me = _chip_index(x, y)
        copies = []
        for i in range(n):
            dst = outs[i].at[me]
            copies.append(pltpu.make_async_copy(_owner_window(ins[i], kinds[i], me), dst, local_sems.at[i]))
            for k, (ox, oy) in enumerate(others):
                copies.append(pltpu.make_async_remote_copy(
                    src_ref=_owner_window(ins[i], kinds[i], _chip_index(ox, oy)), dst_ref=dst, send_sem=send_sems.at[i, k],
                    recv_sem=recv_sems.at[i, k], device_id=(ox, oy, c), device_id_type=MESH_ID))
        return copies

    return dict(ins=list(ss), out_shape=[jax.ShapeDtypeStruct((N_CHIPS,) + shard_shape(s, k), s.dtype)
                                         for s, k in zip(ss, kinds)],
                scratch=_gather_sems(n), make=make)


def _send_halves(fs, *, name):
    n = len(fs)

    def body(*refs):
        ins, outs, send_sems, recv_sems = refs[:n], refs[n:2 * n], refs[2 * n], refs[2 * n + 1]
        x, y, c, _ = _place()
        copies = [pltpu.make_async_remote_copy(src_ref=ins[i], dst_ref=outs[i], send_sem=send_sems.at[i],
                                               recv_sem=recv_sems.at[i], device_id=(x, y, 1 - c), device_id_type=MESH_ID)
                  for i in range(n)]
        for cp in copies:
            cp.start()
        for cp in copies:
            cp.wait()

    return pl.pallas_call(
        body, name=name, in_specs=[HBM_SPEC] * n, out_specs=[HBM_SPEC] * n,
        out_shape=[jax.ShapeDtypeStruct(f.shape, f.dtype) for f in fs],
        scratch_shapes=[pltpu.SemaphoreType.DMA((n,)), pltpu.SemaphoreType.DMA((n,))],
    )(*fs)


def _adamw_halves(w, m, v, own, other, c_idx, *, name):
    shape = w.shape
    nl, r, cols = own.shape
    w4, m4, v4 = (a.reshape(nl, 2, r, cols) for a in (w, m, v))
    tr = r if r * cols * 4 <= 3 * 512 * 1024 else _tile(r, tuple(c for c in (256, 128, 64, 32, 16, 8) if c * cols <= 256 * 1024))

    def body(c_ref, w_ref, m_ref, v_ref, own_ref, other_ref, g_ref, d_ref, nm_ref, nv_ref):
        g_ = jnp.where(pl.program_id(1) == c_ref[0], own_ref[...], other_ref[...])
        g_ref[...] = g_
        d_ref[...], nm_ref[...], nv_ref[...] = _adamw_math(w_ref[...], g_, m_ref[...], v_ref[...])

    full = pl.BlockSpec((None, None, tr, cols), lambda l, h, i, c_ref: (l, h, i, 0))
    half = pl.BlockSpec((None, tr, cols), lambda l, h, i, c_ref: (l, i, 0))
    outs = pl.pallas_call(
        body, name=name,
        grid_spec=pltpu.PrefetchScalarGridSpec(num_scalar_prefetch=1, grid=(nl, 2, r // tr),
                                               in_specs=[full, full, full, half, half], out_specs=[full] * 4),
        out_shape=[jax.ShapeDtypeStruct((nl, 2, r, cols), F32)] * 4,
        compiler_params=_params(dimension_semantics=("parallel", "parallel", "parallel")),
    )(c_idx, w4, m4, v4, own, other)
    return tuple(o.reshape(shape) for o in outs)


SHARD_AXIS = {"rel_bias": None, "ln_g": 2, "ln_b": 2, "ffn_w13": 3, "ffn_w2": 2, "w_in": 2, "conv_w": 2, "a_log": None,
              "dt_bias": None, "dn_norm_g": None, "sinks": None, "w_branch_a": 1, "w_branch_b": 1, "w_out": 1}
WEIGHT_NAMES = tuple(SHARD_AXIS)
SMALL_NAMES = tuple(n for n in WEIGHT_NAMES if n not in MATRIX_NAMES)
PACK_LANES = 1024


def _unshard(gathered, axis):
    g = jnp.moveaxis(gathered, 0, axis)
    return g.reshape(g.shape[:axis] + (g.shape[axis] * g.shape[axis + 1],) + g.shape[axis + 2:])


class _GradReducer:
    def __init__(self, c_idx):
        self.c_idx = c_idx
        self.swapping = None
        self.exchanging = None
        self.reduced = {}

    @staticmethod
    def _views(grads):
        out = []
        for n in MATRIX_NAMES:
            for g in (grads[n] if n in ("ffn_w13", "ffn_w2") else [grads[n]]):
                if n == "ffn_w13":
                    out.append((n, g, "cols"))
                elif n == "w_in":
                    out.append((n, g, "rows"))
                else:
                    out.append((n, g.reshape(N_CHIPS, 2, g.shape[0] // (2 * N_CHIPS), g.shape[1]), "rows"))
        return out

    def layer_done(self, layer, grads):
        assert self.swapping is None
        self.swapping = (layer, self._views(grads))

    def job_a(self):
        if self.swapping is None:
            return None
        _, views = self.swapping
        return _swap_job([g for _, g, _ in views], [k for _, _, k in views])

    def done_a(self, got):
        layer, views = self.swapping
        self.swapping = None
        assert self.exchanging is None
        ss = [_pair_sum(g, r, k, self.c_idx, name=f"rs_pair_sum_l{layer}_{i}")
              for i, ((_, g, k), r) in enumerate(zip(views, got))]
        self.exchanging = (layer, views, ss)

    def job_b(self):
        if self.exchanging is None:
            return None
        _, views, ss = self.exchanging
        return _exchange_job(ss, [k for _, _, k in views])

    def done_b(self, ex):
        layer, views, _ = self.exchanging
        self.exchanging = None
        red = {}
        for i, ((n, _, _), e) in enumerate(zip(views, ex)):
            red.setdefault(n, []).append(_sum_slots(e[None], name=f"rs_chip_sum_l{layer}_{i}")[0])
        self.reduced[layer] = red

    def flush(self):
        if self.swapping is not None:
            self.done_a(_run_job(self.job_a(), name="rs_swap_halves_last"))
        if self.exchanging is not None:
            self.done_b(_run_job(self.job_b(), name="rs_exchange_chips_last"))

    def result(self):
        self.flush()
        own = [jnp.stack([f for layer in sorted(self.reduced) for f in self.reduced[layer][n]]) for n in MATRIX_NAMES]
        other = _send_halves(own, name="rs_send_halves")
        return {n: (a, b) for n, a, b in zip(MATRIX_NAMES, own, other)}


def _reduce_small(grads):
    flat = [grads[n].astype(F32).reshape(-1) for n in SMALL_NAMES]
    total = sum(f.shape[0] for f in flat)
    rows = -(-total // (16 * PACK_LANES)) * 16
    vec = jnp.concatenate(flat + [jnp.zeros((rows * PACK_LANES - total,), F32)]).reshape(rows, PACK_LANES)
    s = _sum_slots(_allgather_devices(vec, name="small_allgather")[None], name="small_sum").reshape(-1)
    out, o = {}, 0
    for n, f in zip(SMALL_NAMES, flat):
        out[n] = s[o:o + f.shape[0]].reshape(grads[n].shape)
        o += f.shape[0]
    return out


def kernel(x, rel_bias, ln_g, ln_b, ffn_w13, ffn_w2, w_in, conv_w, a_log, dt_bias, dn_norm_g, sinks, w_branch_a, w_branch_b, w_out, loss_target, m_rel_bias, m_ln_g, m_ln_b, m_ffn_w13, m_ffn_w2, m_w_in, m_conv_w, m_a_log, m_dt_bias, m_dn_norm_g, m_sinks, m_w_branch_a, m_w_branch_b, m_w_out, v_rel_bias, v_ln_g, v_ln_b, v_ffn_w13, v_ffn_w2, v_w_in, v_conv_w, v_a_log, v_dt_bias, v_dn_norm_g, v_sinks, v_w_branch_a, v_w_branch_b, v_w_out):
    w = dict(rel_bias=rel_bias, ln_g=ln_g, ln_b=ln_b, ffn_w13=ffn_w13, ffn_w2=ffn_w2, w_in=w_in, conv_w=conv_w,
             a_log=a_log, dt_bias=dt_bias, dn_norm_g=dn_norm_g, sinks=sinks, w_branch_a=w_branch_a,
             w_branch_b=w_branch_b, w_out=w_out)
    m = dict(rel_bias=m_rel_bias, ln_g=m_ln_g, ln_b=m_ln_b, ffn_w13=m_ffn_w13, ffn_w2=m_ffn_w2, w_in=m_w_in,
             conv_w=m_conv_w, a_log=m_a_log, dt_bias=m_dt_bias, dn_norm_g=m_dn_norm_g, sinks=m_sinks,
             w_branch_a=m_w_branch_a, w_branch_b=m_w_branch_b, w_out=m_w_out)
    v = dict(rel_bias=v_rel_bias, ln_g=v_ln_g, ln_b=v_ln_b, ffn_w13=v_ffn_w13, ffn_w2=v_ffn_w2, w_in=v_w_in,
             conv_w=v_conv_w, a_log=v_a_log, dt_bias=v_dt_bias, dn_norm_g=v_dn_norm_g, sinks=v_sinks,
             w_branch_a=v_w_branch_a, w_branch_b=v_w_branch_b, w_out=v_w_out)

    depth = w_in.shape[0]
    sharded = [n for n in WEIGHT_NAMES if SHARD_AXIS[n] is not None]

    def shards_of(i):
        out = {}
        for n in sharded:
            s = w[n][i].astype(MXU_DTYPE) if n in MATRIX_NAMES else w[n][i]
            if n in ("ffn_w13", "ffn_w2"):
                out[n + "_0"], out[n + "_1"] = s[0], s[1]
            else:
                out[n] = s
        return out

    def assemble(i, gathered):
        lw = {}
        for n, g in gathered.items():
            if n[:-2] in ("ffn_w13", "ffn_w2"):
                lw.setdefault(n[:-2], {})[int(n[-1])] = _unshard(g, SHARD_AXIS[n[:-2]] - 2)
            elif n == "w_in":
                lw["w_in_p"] = _repack_w_in(_unshard(g, SHARD_AXIS[n] - 1))
            else:
                lw[n] = _unshard(g, SHARD_AXIS[n] - 1)
        return lw

    layer_wts = [{n: w[n][i] for n in ("a_log", "dt_bias", "dn_norm_g", "sinks")} for i in range(depth)]
    shards = [shards_of(i) for i in range(depth)]
    first = ("ffn_w13_0", "ffn_w2_0", "ln_g", "ln_b")
    got = _run_job(_gather_job([shards[0].pop(n) for n in first]), name="weights_allgather_first")
    layer_wts[0].update(assemble(0, dict(zip(first, got))))
    c_idx = lax.axis_index("c").astype(jnp.int32).reshape(1)
    reducer = _GradReducer(c_idx)
    loss_part, grad_x, grads = _local_step(x, loss_target, rel_bias, layer_wts, side_shards=shards,
                                           side_assemble=assemble, reducer=reducer)
    loss = lax.psum(loss_part, ("x", "y", "c"))

    halves = reducer.result()
    chip = _chip_index(lax.axis_index("x"), lax.axis_index("y"))
    small = _reduce_small(grads)
    outs = {}
    for n in WEIGHT_NAMES:
        if n in MATRIX_NAMES:
            outs[n] = _adamw_halves(w[n], m[n], v[n], *halves[n], c_idx, name="adamw_" + n)
        else:
            axis = SHARD_AXIS[n]
            g = small[n]
            if axis is not None:
                g = lax.dynamic_slice_in_dim(g, chip * w[n].shape[axis], w[n].shape[axis], axis)
            outs[n] = (g,) + _adamw(w[n], g, m[n], v[n], name="adamw_" + n)
    return (loss, grad_x, *[outs[n][0] for n in WEIGHT_NAMES], *[outs[n][1] for n in WEIGHT_NAMES],
            *[outs[n][2] for n in WEIGHT_NAMES], *[outs[n][3] for n in WEIGHT_NAMES])
```
